```python
import jax, jax.numpy as jnp
from jax import lax
import numpy as np

D_MODEL = 1024
BATCH = 8
SEQ = 8192
DEPTH = 2

GM_WIDTH = D_MODEL
GM_GROUPS = 8
GM_GROUP_DIM = GM_WIDTH // GM_GROUPS
GM_CHUNK = 128
SSM_D_INNER = 2 * D_MODEL
SSM_HEAD_DIM = 64
SSM_HEADS = SSM_D_INNER // SSM_HEAD_DIM
SSM_GROUPS = 8
SSM_HEADS_PER_GROUP = SSM_HEADS // SSM_GROUPS
SSM_STATE = 128
SSM_CONV = 4
SSM_CHUNK = 128
SSM_CONV_DIM = SSM_D_INNER + 2 * SSM_GROUPS * SSM_STATE
N_BRANCH = 2
SPLITS = (
    GM_WIDTH,
    2 * GM_WIDTH,
    3 * GM_WIDTH,
    3 * GM_WIDTH + SSM_D_INNER,
    3 * GM_WIDTH + SSM_D_INNER + SSM_CONV_DIM,
    3 * GM_WIDTH + SSM_D_INNER + SSM_CONV_DIM + SSM_HEADS,
    3 * GM_WIDTH + SSM_D_INNER + SSM_CONV_DIM + SSM_HEADS + D_MODEL,
)
N_IN = 3 * GM_WIDTH + SSM_D_INNER + SSM_CONV_DIM + SSM_HEADS + N_BRANCH * D_MODEL
EPS = 1e-6

kernel_name = "hybrid_gmlp_ssd_gated_merge"


def rms_norm(x, w):
    xf = x.astype(jnp.float32)
    y = xf * lax.rsqrt(jnp.mean(xf * xf, axis=-1, keepdims=True) + EPS)
    return (y * w.astype(jnp.float32)).astype(x.dtype)


def layer_norm(x, w, b):
    xf = x.astype(jnp.float32)
    mu = jnp.mean(xf, axis=-1, keepdims=True)
    var = jnp.mean(jnp.square(xf - mu), axis=-1, keepdims=True)
    y = (xf - mu) * lax.rsqrt(var + EPS)
    return (y * w.astype(jnp.float32) + b.astype(jnp.float32)).astype(x.dtype)


def spatial_gating(u, v, ln_w, ln_b, w_s, b_s):
    b, s, _ = v.shape
    v = layer_norm(v, ln_w, ln_b)
    v = v.reshape(b, s // GM_CHUNK, GM_CHUNK, GM_GROUPS, GM_GROUP_DIM)
    mask = jnp.tril(jnp.ones((GM_CHUNK, GM_CHUNK), dtype=bool))
    w = jnp.where(mask[None], w_s, jnp.zeros_like(w_s))
    mixed = jnp.einsum('gts,bnsgd->bntgd', w, v) + b_s.T[None, None, :, :, None]
    return u * mixed.reshape(b, s, GM_WIDTH)


def causal_depthwise_conv(x, w, bias):
    k, ch = w.shape
    out = lax.conv_general_dilated(
        x, w[:, None, :], window_strides=(1,), padding=[(k - 1, 0)],
        dimension_numbers=('NWC', 'WIO', 'NWC'), feature_group_count=ch)
    return out + bias


def segsum_exp(a):
    t = a.shape[-1]
    cs = jnp.cumsum(a, axis=-1)
    diff = cs[..., :, None] - cs[..., None, :]
    mask = jnp.tril(jnp.ones((t, t), dtype=bool))
    return jnp.exp(jnp.where(mask, diff, -jnp.inf))


def ssd_scan(x, dt, a, bmat, cmat):
    b, s, h, p = x.shape
    q = SSM_CHUNK
    nc = s // q
    g, r, n = SSM_GROUPS, SSM_HEADS_PER_GROUP, SSM_STATE
    xd = (x * dt[..., None]).reshape(b, nc, q, g, r, p)
    adt = jnp.moveaxis((dt * a).astype(jnp.float32).reshape(b, nc, q, g, r), 2, -1)
    a_cs = jnp.cumsum(adt, axis=-1)
    bc = bmat.reshape(b, nc, q, g, n)
    cc = cmat.reshape(b, nc, q, g, n)
    decay = segsum_exp(adt)
    cb = jnp.einsum('bclgn,bcsgn->bcgls', cc, bc)
    y_diag = jnp.einsum('bcgls,bcgrls,bcsgrp->bclgrp', cb, decay, xd)
    decay_to_end = jnp.exp(a_cs[..., -1:] - a_cs)
    chunk_states = jnp.einsum('bcsgn,bcgrs,bcsgrp->bcgrpn', bc, decay_to_end, xd).astype(jnp.float32)
    chunk_decay = jnp.exp(a_cs[..., -1])

    def step(state, inp):
        cs, cd = inp
        return state * cd[..., None, None] + cs, state

    init = jnp.zeros((b, g, r, p, n), jnp.float32)
    _, prev_states = lax.scan(step, init, (jnp.moveaxis(chunk_states, 1, 0), jnp.moveaxis(chunk_decay, 1, 0)))
    prev_states = jnp.moveaxis(prev_states, 0, 1)
    y_off = jnp.einsum('bclgn,bcgrpn,bcgrl->bclgrp', cc, prev_states, jnp.exp(a_cs))
    return (y_diag + y_off).reshape(b, s, h, p).astype(x.dtype)


def hybrid_layer(x, c, ada_w, ada_b, norm_w, w_in, gm_ln_w, gm_ln_b, gm_ws, gm_bs,
                 conv_w, conv_b, dt_bias, a_log, d_skip, ssm_norm_w, w_proj_a, w_proj_b, w_out):
    b, s, _ = x.shape
    mod = jax.nn.silu(c) @ ada_w + ada_b
    shift, scale, gate = jnp.split(mod, 3, axis=-1)
    h = rms_norm(x, norm_w) * (1 + scale[:, None, :]) + shift[:, None, :]
    proj = h @ w_in
    gm_u, gm_v, gm_z, ssm_z, xbc, dt_raw, g_a, g_b = jnp.split(proj, SPLITS, axis=-1)
    y_a = spatial_gating(jax.nn.gelu(gm_u), jax.nn.gelu(gm_v), gm_ln_w, gm_ln_b, gm_ws, gm_bs) * jax.nn.silu(gm_z)
    xbc = jax.nn.silu(causal_depthwise_conv(xbc, conv_w, conv_b))
    xs, bm, cm = jnp.split(xbc, (SSM_D_INNER, SSM_D_INNER + SSM_GROUPS * SSM_STATE), axis=-1)
    dt = jax.nn.softplus((dt_raw + dt_bias).astype(jnp.float32))
    a = -jnp.exp(a_log.astype(jnp.float32))
    xh = xs.reshape(b, s, SSM_HEADS, SSM_HEAD_DIM)
    y_b = ssd_scan(xh, dt, a, bm.reshape(b, s, SSM_GROUPS, SSM_STATE), cm.reshape(b, s, SSM_GROUPS, SSM_STATE))
    y_b = y_b + xh * d_skip[:, None]
    yz = (y_b.reshape(b, s, SSM_D_INNER) * jax.nn.silu(ssm_z)).reshape(b, s, SSM_GROUPS, SSM_D_INNER // SSM_GROUPS)
    y_b = rms_norm(yz, ssm_norm_w.reshape(SSM_GROUPS, -1)).reshape(b, s, SSM_D_INNER)
    merged = jax.nn.sigmoid(g_a) * (y_a @ w_proj_a) + jax.nn.sigmoid(g_b) * (y_b @ w_proj_b)
    return x + gate[:, None, :] * (merged @ w_out)


def _fwd_setup_inputs(seed: int = 0) -> dict:
    key = jax.random.key(seed)
    ks = jax.random.split(key, 24)
    nrm = jax.random.normal
    L, D = DEPTH, D_MODEL
    dt0 = jnp.exp(jax.random.uniform(ks[10], (L, SSM_HEADS), minval=np.log(1e-3), maxval=np.log(1e-1)))
    return {
        "x": nrm(ks[0], (BATCH, SEQ, D), jnp.float32),
        "c": nrm(ks[1], (BATCH, D), jnp.float32),
        "ada_w": nrm(ks[2], (L, D, 3 * D), jnp.float32) * D ** -0.5,
        "ada_b": 0.01 * nrm(ks[3], (L, 3 * D), jnp.float32),
        "norm_w": 1.0 + 0.1 * nrm(ks[4], (L, D), jnp.float32),
        "w_in": nrm(ks[5], (L, D, N_IN), jnp.float32) * D ** -0.5,
        "gm_ln_w": 1.0 + 0.1 * nrm(ks[6], (L, GM_WIDTH), jnp.float32),
        "gm_ln_b": 0.01 * nrm(ks[7], (L, GM_WIDTH), jnp.float32),
        "gm_ws": nrm(ks[8], (L, GM_GROUPS, GM_CHUNK, GM_CHUNK), jnp.float32) * GM_CHUNK ** -0.5,
        "gm_bs": 1.0 + 0.1 * nrm(ks[9], (L, GM_GROUPS, GM_CHUNK), jnp.float32),
        "conv_w": nrm(ks[11], (L, SSM_CONV, SSM_CONV_DIM), jnp.float32) * SSM_CONV ** -0.5,
        "conv_b": 0.01 * nrm(ks[12], (L, SSM_CONV_DIM), jnp.float32),
        "dt_bias": dt0 + jnp.log(-jnp.expm1(-dt0)),
        "a_log": jnp.log(jax.random.uniform(ks[13], (L, SSM_HEADS), minval=1.0, maxval=16.0)),
        "d_skip": 1.0 + 0.1 * nrm(ks[14], (L, SSM_HEADS), jnp.float32),
        "ssm_norm_w": 1.0 + 0.1 * nrm(ks[15], (L, SSM_D_INNER), jnp.float32),
        "w_proj_a": nrm(ks[16], (L, GM_WIDTH, D), jnp.float32) * GM_WIDTH ** -0.5,
        "w_proj_b": nrm(ks[17], (L, SSM_D_INNER, D), jnp.float32) * SSM_D_INNER ** -0.5,
        "w_out": nrm(ks[18], (L, D, D), jnp.float32) * D ** -0.5,
        "final_norm_w": 1.0 + 0.1 * nrm(ks[19], (D,), jnp.float32),
    }


def _fwd_reference(x, c, ada_w, ada_b, norm_w, w_in, gm_ln_w, gm_ln_b, gm_ws, gm_bs, conv_w, conv_b,
              dt_bias, a_log, d_skip, ssm_norm_w, w_proj_a, w_proj_b, w_out, final_norm_w):
    for i in range(DEPTH):
        x = hybrid_layer(x, c, ada_w[i], ada_b[i], norm_w[i], w_in[i], gm_ln_w[i], gm_ln_b[i],
                         gm_ws[i], gm_bs[i], conv_w[i], conv_b[i], dt_bias[i], a_log[i], d_skip[i],
                         ssm_norm_w[i], w_proj_a[i], w_proj_b[i], w_out[i])
    return rms_norm(x, final_norm_w)


import jax as _jax
import jax.numpy as _jnp

TWIN_FORMAT = 'train_step'
FWD_PARAMS = ['x', 'c', 'ada_w', 'ada_b', 'norm_w', 'w_in', 'gm_ln_w', 'gm_ln_b', 'gm_ws', 'gm_bs', 'conv_w', 'conv_b', 'dt_bias', 'a_log', 'd_skip', 'ssm_norm_w', 'w_proj_a', 'w_proj_b', 'w_out', 'final_norm_w']
TWIN_WEIGHTS = ['ada_w', 'ada_b', 'norm_w', 'w_in', 'gm_ln_w', 'gm_ln_b', 'gm_ws', 'gm_bs', 'conv_w', 'conv_b', 'dt_bias', 'a_log', 'd_skip', 'ssm_norm_w', 'w_proj_a', 'w_proj_b', 'w_out', 'final_norm_w']
TWIN_DIFF_INPUT = 'x'
TWIN_INPUTS = ['x', 'c', 'ada_w', 'ada_b', 'norm_w', 'w_in', 'gm_ln_w', 'gm_ln_b', 'gm_ws', 'gm_bs', 'conv_w', 'conv_b', 'dt_bias', 'a_log', 'd_skip', 'ssm_norm_w', 'w_proj_a', 'w_proj_b', 'w_out', 'final_norm_w', 'loss_target', 'm_ada_w', 'm_ada_b', 'm_norm_w', 'm_w_in', 'm_gm_ln_w', 'm_gm_ln_b', 'm_gm_ws', 'm_gm_bs', 'm_conv_w', 'm_conv_b', 'm_dt_bias', 'm_a_log', 'm_d_skip', 'm_ssm_norm_w', 'm_w_proj_a', 'm_w_proj_b', 'm_w_out', 'm_final_norm_w', 'v_ada_w', 'v_ada_b', 'v_norm_w', 'v_w_in', 'v_gm_ln_w', 'v_gm_ln_b', 'v_gm_ws', 'v_gm_bs', 'v_conv_w', 'v_conv_b', 'v_dt_bias', 'v_a_log', 'v_d_skip', 'v_ssm_norm_w', 'v_w_proj_a', 'v_w_proj_b', 'v_w_out', 'v_final_norm_w']
TWIN_OUTPUTS = ['loss', 'grad_x', 'grad_ada_w', 'grad_ada_b', 'grad_norm_w', 'grad_w_in', 'grad_gm_ln_w', 'grad_gm_ln_b', 'grad_gm_ws', 'grad_gm_bs', 'grad_conv_w', 'grad_conv_b', 'grad_dt_bias', 'grad_a_log', 'grad_d_skip', 'grad_ssm_norm_w', 'grad_w_proj_a', 'grad_w_proj_b', 'grad_w_out', 'grad_final_norm_w', 'delta_ada_w', 'delta_ada_b', 'delta_norm_w', 'delta_w_in', 'delta_gm_ln_w', 'delta_gm_ln_b', 'delta_gm_ws', 'delta_gm_bs', 'delta_conv_w', 'delta_conv_b', 'delta_dt_bias', 'delta_a_log', 'delta_d_skip', 'delta_ssm_norm_w', 'delta_w_proj_a', 'delta_w_proj_b', 'delta_w_out', 'delta_final_norm_w', 'new_m_ada_w', 'new_m_ada_b', 'new_m_norm_w', 'new_m_w_in', 'new_m_gm_ln_w', 'new_m_gm_ln_b', 'new_m_gm_ws', 'new_m_gm_bs', 'new_m_conv_w', 'new_m_conv_b', 'new_m_dt_bias', 'new_m_a_log', 'new_m_d_skip', 'new_m_ssm_norm_w', 'new_m_w_proj_a', 'new_m_w_proj_b', 'new_m_w_out', 'new_m_final_norm_w', 'new_v_ada_w', 'new_v_ada_b', 'new_v_norm_w', 'new_v_w_in', 'new_v_gm_ln_w', 'new_v_gm_ln_b', 'new_v_gm_ws', 'new_v_gm_bs', 'new_v_conv_w', 'new_v_conv_b', 'new_v_dt_bias', 'new_v_a_log', 'new_v_d_skip', 'new_v_ssm_norm_w', 'new_v_w_proj_a', 'new_v_w_proj_b', 'new_v_w_out', 'new_v_final_norm_w']
TWIN_LEAF_KINDS = {'loss': 'loss', 'grad_x': 'grad_x', 'grad_ada_w': 'grad_w', 'grad_ada_b': 'grad_w', 'grad_norm_w': 'grad_w', 'grad_w_in': 'grad_w', 'grad_gm_ln_w': 'grad_w', 'grad_gm_ln_b': 'grad_w', 'grad_gm_ws': 'grad_w', 'grad_gm_bs': 'grad_w', 'grad_conv_w': 'grad_w', 'grad_conv_b': 'grad_w', 'grad_dt_bias': 'grad_w', 'grad_a_log': 'grad_w', 'grad_d_skip': 'grad_w', 'grad_ssm_norm_w': 'grad_w', 'grad_w_proj_a': 'grad_w', 'grad_w_proj_b': 'grad_w', 'grad_w_out': 'grad_w', 'grad_final_norm_w': 'grad_w', 'delta_ada_w': 'delta_w', 'delta_ada_b': 'delta_w', 'delta_norm_w': 'delta_w', 'delta_w_in': 'delta_w', 'delta_gm_ln_w': 'delta_w', 'delta_gm_ln_b': 'delta_w', 'delta_gm_ws': 'delta_w', 'delta_gm_bs': 'delta_w', 'delta_conv_w': 'delta_w', 'delta_conv_b': 'delta_w', 'delta_dt_bias': 'delta_w', 'delta_a_log': 'delta_w', 'delta_d_skip': 'delta_w', 'delta_ssm_norm_w': 'delta_w', 'delta_w_proj_a': 'delta_w', 'delta_w_proj_b': 'delta_w', 'delta_w_out': 'delta_w', 'delta_final_norm_w': 'delta_w', 'new_m_ada_w': 'new_m', 'new_m_ada_b': 'new_m', 'new_m_norm_w': 'new_m', 'new_m_w_in': 'new_m', 'new_m_gm_ln_w': 'new_m', 'new_m_gm_ln_b': 'new_m', 'new_m_gm_ws': 'new_m', 'new_m_gm_bs': 'new_m', 'new_m_conv_w': 'new_m', 'new_m_conv_b': 'new_m', 'new_m_dt_bias': 'new_m', 'new_m_a_log': 'new_m', 'new_m_d_skip': 'new_m', 'new_m_ssm_norm_w': 'new_m', 'new_m_w_proj_a': 'new_m', 'new_m_w_proj_b': 'new_m', 'new_m_w_out': 'new_m', 'new_m_final_norm_w': 'new_m', 'new_v_ada_w': 'new_v', 'new_v_ada_b': 'new_v', 'new_v_norm_w': 'new_v', 'new_v_w_in': 'new_v', 'new_v_gm_ln_w': 'new_v', 'new_v_gm_ln_b': 'new_v', 'new_v_gm_ws': 'new_v', 'new_v_gm_bs': 'new_v', 'new_v_conv_w': 'new_v', 'new_v_conv_b': 'new_v', 'new_v_dt_bias': 'new_v', 'new_v_a_log': 'new_v', 'new_v_d_skip': 'new_v', 'new_v_ssm_norm_w': 'new_v', 'new_v_w_proj_a': 'new_v', 'new_v_w_proj_b': 'new_v', 'new_v_w_out': 'new_v', 'new_v_final_norm_w': 'new_v'}


def _forward(args):
    return _fwd_reference(*[args[k] for k in FWD_PARAMS])


def _output_shape():
    def fwd():
        inp = _fwd_setup_inputs(0)
        return _fwd_reference(*[inp[k] for k in FWD_PARAMS])
    out = _jax.eval_shape(fwd)
    return out.shape, out.dtype

N_MICROBATCH = 1
ADAM_LR = 0.001
ADAM_B1 = 0.9
ADAM_B2 = 0.999
ADAM_EPS = 1e-08
ADAM_WD = 0.01
ADAM_STEP = 10
PER_EXAMPLE_BATCH_AXIS = {'x': 0, 'c': 0, 'loss_target': 0}
SHARED_INPUTS = []
_WEIGHT_DTYPES = {'ada_w': _jnp.float32, 'ada_b': _jnp.float32, 'norm_w': _jnp.float32, 'w_in': _jnp.float32, 'gm_ln_w': _jnp.float32, 'gm_ln_b': _jnp.float32, 'gm_ws': _jnp.float32, 'gm_bs': _jnp.float32, 'conv_w': _jnp.float32, 'conv_b': _jnp.float32, 'dt_bias': _jnp.float32, 'a_log': _jnp.float32, 'd_skip': _jnp.float32, 'ssm_norm_w': _jnp.float32, 'w_proj_a': _jnp.float32, 'w_proj_b': _jnp.float32, 'w_out': _jnp.float32, 'final_norm_w': _jnp.float32}
MOMENT_SCALE = {'ada_w': 2.971616e-01, 'ada_b': 6.525716e-01, 'norm_w': 1.557037e-01, 'w_in': 5.830260e-02, 'gm_ln_w': 4.323821e-02, 'gm_ln_b': 4.718897e-02, 'gm_ws': 4.126610e-02, 'gm_bs': 5.800489e-02, 'conv_w': 5.372165e-02, 'conv_b': 6.178022e-02, 'dt_bias': 1.363026e-01, 'a_log': 2.429426e-01, 'd_skip': 1.827782e-01, 'ssm_norm_w': 8.073361e-02, 'w_proj_a': 8.357595e-02, 'w_proj_b': 1.122209e-01, 'w_out': 1.418969e-01, 'final_norm_w': 6.445441e+01}


def _to_microbatches(a, axis):
    t = _jnp.moveaxis(a, axis, 0)
    t = t.reshape((N_MICROBATCH, t.shape[0] // N_MICROBATCH) + t.shape[1:])
    return _jnp.moveaxis(t, 1, axis + 1)


def setup_inputs(seed: int = 0) -> dict:
    inp = _fwd_setup_inputs(seed)
    key = _jax.random.fold_in(_jax.random.key(seed), 7919)
    shape, _ = _output_shape()
    out = dict(inp)
    out["loss_target"] = _jax.random.normal(_jax.random.fold_in(key, 0), shape, _jnp.float32)
    for i, name in enumerate(TWIN_WEIGHTS):
        w = inp[name].astype(_jnp.float32)
        if MOMENT_SCALE is None:
            s = _jnp.sqrt(_jnp.mean(_jnp.square(w)) + 1e-30)
        else:
            s = MOMENT_SCALE[name]
        km, kv = _jax.random.split(_jax.random.fold_in(key, i + 1))
        out[name] = w
        out["m_" + name] = s * _jax.random.normal(km, w.shape, _jnp.float32)
        out["v_" + name] = (s * s) * _jax.random.uniform(kv, w.shape, _jnp.float32, 0.5, 1.5)
    if N_MICROBATCH > 1:
        for name, axis in PER_EXAMPLE_BATCH_AXIS.items():
            out[name] = _to_microbatches(out[name], axis)
    return {'x': out['x'], 'c': out['c'], 'ada_w': out['ada_w'], 'ada_b': out['ada_b'], 'norm_w': out['norm_w'], 'w_in': out['w_in'], 'gm_ln_w': out['gm_ln_w'], 'gm_ln_b': out['gm_ln_b'], 'gm_ws': out['gm_ws'], 'gm_bs': out['gm_bs'], 'conv_w': out['conv_w'], 'conv_b': out['conv_b'], 'dt_bias': out['dt_bias'], 'a_log': out['a_log'], 'd_skip': out['d_skip'], 'ssm_norm_w': out['ssm_norm_w'], 'w_proj_a': out['w_proj_a'], 'w_proj_b': out['w_proj_b'], 'w_out': out['w_out'], 'final_norm_w': out['final_norm_w'], 'loss_target': out['loss_target'], 'm_ada_w': out['m_ada_w'], 'm_ada_b': out['m_ada_b'], 'm_norm_w': out['m_norm_w'], 'm_w_in': out['m_w_in'], 'm_gm_ln_w': out['m_gm_ln_w'], 'm_gm_ln_b': out['m_gm_ln_b'], 'm_gm_ws': out['m_gm_ws'], 'm_gm_bs': out['m_gm_bs'], 'm_conv_w': out['m_conv_w'], 'm_conv_b': out['m_conv_b'], 'm_dt_bias': out['m_dt_bias'], 'm_a_log': out['m_a_log'], 'm_d_skip': out['m_d_skip'], 'm_ssm_norm_w': out['m_ssm_norm_w'], 'm_w_proj_a': out['m_w_proj_a'], 'm_w_proj_b': out['m_w_proj_b'], 'm_w_out': out['m_w_out'], 'm_final_norm_w': out['m_final_norm_w'], 'v_ada_w': out['v_ada_w'], 'v_ada_b': out['v_ada_b'], 'v_norm_w': out['v_norm_w'], 'v_w_in': out['v_w_in'], 'v_gm_ln_w': out['v_gm_ln_w'], 'v_gm_ln_b': out['v_gm_ln_b'], 'v_gm_ws': out['v_gm_ws'], 'v_gm_bs': out['v_gm_bs'], 'v_conv_w': out['v_conv_w'], 'v_conv_b': out['v_conv_b'], 'v_dt_bias': out['v_dt_bias'], 'v_a_log': out['v_a_log'], 'v_d_skip': out['v_d_skip'], 'v_ssm_norm_w': out['v_ssm_norm_w'], 'v_w_proj_a': out['v_w_proj_a'], 'v_w_proj_b': out['v_w_proj_b'], 'v_w_out': out['v_w_out'], 'v_final_norm_w': out['v_final_norm_w']}


def _loss(weights, diff, rest, loss_target):
    with _jax.named_scope("forward"):
        args = {**rest, TWIN_DIFF_INPUT: diff, **{k: w.astype(_WEIGHT_DTYPES[k]) for k, w in weights.items()}}
        y = _forward(args)
    with _jax.named_scope("loss_head"):
        err = _jnp.square(y.astype(_jnp.float32) - loss_target)
        return 0.5 * _jnp.sum(_jnp.mean(err, axis=-1)) if err.ndim else 0.5 * err


def _adamw(w, g, m, v):
    m = ADAM_B1 * m + (1.0 - ADAM_B1) * g
    v = ADAM_B2 * v + (1.0 - ADAM_B2) * _jnp.square(g)
    m_hat = m / (1.0 - ADAM_B1 ** ADAM_STEP)
    v_hat = v / (1.0 - ADAM_B2 ** ADAM_STEP)
    delta = -ADAM_LR * (m_hat / (_jnp.sqrt(v_hat) + ADAM_EPS) + ADAM_WD * w)
    return delta, m, v


def reference(x, c, ada_w, ada_b, norm_w, w_in, gm_ln_w, gm_ln_b, gm_ws, gm_bs, conv_w, conv_b, dt_bias, a_log, d_skip, ssm_norm_w, w_proj_a, w_proj_b, w_out, final_norm_w, loss_target, m_ada_w, m_ada_b, m_norm_w, m_w_in, m_gm_ln_w, m_gm_ln_b, m_gm_ws, m_gm_bs, m_conv_w, m_conv_b, m_dt_bias, m_a_log, m_d_skip, m_ssm_norm_w, m_w_proj_a, m_w_proj_b, m_w_out, m_final_norm_w, v_ada_w, v_ada_b, v_norm_w, v_w_in, v_gm_ln_w, v_gm_ln_b, v_gm_ws, v_gm_bs, v_conv_w, v_conv_b, v_dt_bias, v_a_log, v_d_skip, v_ssm_norm_w, v_w_proj_a, v_w_proj_b, v_w_out, v_final_norm_w):
    given = dict(x=x, c=c, ada_w=ada_w, ada_b=ada_b, norm_w=norm_w, w_in=w_in, gm_ln_w=gm_ln_w, gm_ln_b=gm_ln_b, gm_ws=gm_ws, gm_bs=gm_bs, conv_w=conv_w, conv_b=conv_b, dt_bias=dt_bias, a_log=a_log, d_skip=d_skip, ssm_norm_w=ssm_norm_w, w_proj_a=w_proj_a, w_proj_b=w_proj_b, w_out=w_out, final_norm_w=final_norm_w, loss_target=loss_target, m_ada_w=m_ada_w, m_ada_b=m_ada_b, m_norm_w=m_norm_w, m_w_in=m_w_in, m_gm_ln_w=m_gm_ln_w, m_gm_ln_b=m_gm_ln_b, m_gm_ws=m_gm_ws, m_gm_bs=m_gm_bs, m_conv_w=m_conv_w, m_conv_b=m_conv_b, m_dt_bias=m_dt_bias, m_a_log=m_a_log, m_d_skip=m_d_skip, m_ssm_norm_w=m_ssm_norm_w, m_w_proj_a=m_w_proj_a, m_w_proj_b=m_w_proj_b, m_w_out=m_w_out, m_final_norm_w=m_final_norm_w, v_ada_w=v_ada_w, v_ada_b=v_ada_b, v_norm_w=v_norm_w, v_w_in=v_w_in, v_gm_ln_w=v_gm_ln_w, v_gm_ln_b=v_gm_ln_b, v_gm_ws=v_gm_ws, v_gm_bs=v_gm_bs, v_conv_w=v_conv_w, v_conv_b=v_conv_b, v_dt_bias=v_dt_bias, v_a_log=v_a_log, v_d_skip=v_d_skip, v_ssm_norm_w=v_ssm_norm_w, v_w_proj_a=v_w_proj_a, v_w_proj_b=v_w_proj_b, v_w_out=v_w_out, v_final_norm_w=v_final_norm_w)
    weights = {n: given[n] for n in TWIN_WEIGHTS}
    shared = {n: given[n] for n in SHARED_INPUTS}
    per_example = {n: given[n] for n in ['x', 'c']}
    grad_fn = _jax.value_and_grad(_loss, argnums=(0, 1))

    def one_microbatch(ex, loss_target):
        ex = dict(ex)
        diff = ex.pop(TWIN_DIFF_INPUT)
        return grad_fn(weights, diff, {**shared, **ex}, loss_target)

    if N_MICROBATCH == 1:
        loss, (grad_w, grad_x) = one_microbatch(per_example, given["loss_target"])
    else:
        def body(carry, xs):
            loss_sum, grad_sum = carry
            l_k, (gw_k, gx_k) = one_microbatch(xs[0], xs[1])
            with _jax.named_scope("update"):
                return (loss_sum + l_k, _jax.tree.map(_jnp.add, grad_sum, gw_k)), gx_k

        init = (_jnp.zeros((), _jnp.float32), _jax.tree.map(_jnp.zeros_like, weights))
        (loss, grad_w), grad_x = _jax.lax.scan(body, init, (per_example, given["loss_target"]))
    with _jax.named_scope("update"):
        delta_w, new_m, new_v = {}, {}, {}
        for n in TWIN_WEIGHTS:
            delta_w[n], new_m[n], new_v[n] = _adamw(weights[n], grad_w[n], given["m_" + n], given["v_" + n])
    return (loss, grad_x, *[grad_w[n] for n in TWIN_WEIGHTS], *[delta_w[n] for n in TWIN_WEIGHTS],
            *[new_m[n] for n in TWIN_WEIGHTS], *[new_v[n] for n in TWIN_WEIGHTS])
```

```python
import functools

import jax
import jax.numpy as jnp
from jax import lax
from jax.experimental import pallas as pl
from jax.experimental.pallas import tpu as pltpu

f32 = jnp.float32
_MXU = jnp.bfloat16
_WIRE = jnp.bfloat16

D = 1024
Q = 128
NG = 8
DI = 2048
NH = 32
P = 64
HPG = 4
NS = 128
KC = 4
CD = 4096
GRP = DI // NG
EPS = 1e-6
NDEV = 8
WA = 3 * D
WB = DI + CD + 256
WG = 2 * D
OFF_XBC = DI
OFF_DT = DI + CD
NIN = 11296
VMEM_LIMIT = 56 * 1024 * 1024
NEG = -1e30

ADAM_LR, ADAM_B1, ADAM_B2, ADAM_EPS, ADAM_WD, ADAM_STEP = 0.001, 0.9, 0.999, 1e-08, 0.01, 10


def _bf(x):
    return x.astype(_MXU)


def _dot(a, b):
    return jnp.dot(a, b, preferred_element_type=f32)


def _dot_nt(a, b):
    return lax.dot_general(a, b, (((1,), (1,)), ((), ())), preferred_element_type=f32)


def _dot_tn(a, b):
    return lax.dot_general(a, b, (((0,), (0,)), ((), ())), preferred_element_type=f32)


def _dot_exact(a, b):
    return jnp.dot(a, b, preferred_element_type=f32, precision=lax.Precision.HIGHEST)


def _sigmoid(x):
    return jax.nn.sigmoid(x)


def _silu(x):
    return x * _sigmoid(x)


def _dsilu(x):
    s = _sigmoid(x)
    return s * (1.0 + x * (1.0 - s))


_GK = 0.7978845608028654
_GC = 0.044715


def _gelu(x):
    return 0.5 * x * (1.0 + jnp.tanh(_GK * (x + _GC * x * x * x)))


def _dgelu(x):
    t = jnp.tanh(_GK * (x + _GC * x * x * x))
    return 0.5 * (1.0 + t) + 0.5 * x * (1.0 - t * t) * _GK * (1.0 + 3.0 * _GC * x * x)


def _softplus(x):
    return jnp.maximum(x, 0.0) + jnp.log1p(jnp.exp(-jnp.abs(x)))


def _tile(n, cap):
    if n <= cap:
        return n
    best = None
    for t in range(128, cap + 1, 128):
        if n % t == 0:
            best = t
    assert best is not None, (n, cap)
    return best


def _params(sem):
    return pltpu.CompilerParams(dimension_semantics=sem, vmem_limit_bytes=VMEM_LIMIT)


def _mm(a, b, mode, name, acc=None, out_dtype=f32, tm_cap=512, tn_cap=1280, tk_cap=1280):
    if mode == "nn":
        (M, K), (K2, N) = a.shape, b.shape
    elif mode == "nt":
        (M, K), (N, K2) = a.shape, b.shape
    else:
        (K, M), (K2, N) = a.shape, b.shape
        tk_cap = min(tk_cap, 512)
    assert K == K2, (a.shape, b.shape, mode)
    tm, tn, tk = _tile(M, tm_cap), _tile(N, tn_cap), _tile(K, tk_cap)
    nk = K // tk
    if mode == "nn":
        a_spec = pl.BlockSpec((tm, tk), lambda i, j, k: (i, k))
        b_spec = pl.BlockSpec((tk, tn), lambda i, j, k: (k, j))
        dot = _dot
    elif mode == "nt":
        a_spec = pl.BlockSpec((tm, tk), lambda i, j, k: (i, k))
        b_spec = pl.BlockSpec((tn, tk), lambda i, j, k: (j, k))
        dot = _dot_nt
    else:
        a_spec = pl.BlockSpec((tk, tm), lambda i, j, k: (k, i))
        b_spec = pl.BlockSpec((tk, tn), lambda i, j, k: (k, j))
        dot = _dot_tn
    o_spec = pl.BlockSpec((tm, tn), lambda i, j, k: (i, j))
    has_acc = acc is not None

    def body(*refs):
        if has_acc:
            a_ref, b_ref, c_ref, o_ref, acc_ref = refs
        else:
            a_ref, b_ref, o_ref, acc_ref = refs
        k = pl.program_id(2)

        @pl.when(k == 0)
        def _():
            if has_acc:
                acc_ref[...] = c_ref[...]
            else:
                acc_ref[...] = jnp.zeros_like(acc_ref)

        acc_ref[...] += dot(_bf(a_ref[...]), _bf(b_ref[...]))

        @pl.when(k == nk - 1)
        def _():
            o_ref[...] = acc_ref[...].astype(out_dtype)

    in_specs = [a_spec, b_spec] + ([o_spec] if has_acc else [])
    args = (a, b) + ((acc,) if has_acc else ())
    return pl.pallas_call(
        body, name=name, grid=(M // tm, N // tn, nk), in_specs=in_specs, out_specs=o_spec,
        out_shape=jax.ShapeDtypeStruct((M, N), out_dtype), scratch_shapes=[pltpu.VMEM((tm, tn), f32)],
        compiler_params=_params(("arbitrary", "arbitrary", "arbitrary")))(*args)


def _row_spec(ts, w, col=0):
    return pl.BlockSpec((ts, w), lambda i: (i, col))


def _full_spec(shape):
    nd = len(shape)
    return pl.BlockSpec(shape, lambda i: (0,) * nd)


def _modulate(x, nw, shift, scale, name):
    T = x.shape[0]
    ts = _tile(T, 512)

    def body(x_ref, nw_ref, sh_ref, sc_ref, h_ref):
        xv = x_ref[...]
        r = lax.rsqrt(jnp.mean(xv * xv, axis=-1, keepdims=True) + EPS)
        h_ref[...] = ((xv * r * nw_ref[...]) * (1.0 + sc_ref[...]) + sh_ref[...]).astype(h_ref.dtype)

    return pl.pallas_call(
        body, name=name, grid=(T // ts,),
        in_specs=[_row_spec(ts, D), _full_spec((1, D)), _full_spec((1, D)), _full_spec((1, D))],
        out_specs=_row_spec(ts, D), out_shape=jax.ShapeDtypeStruct((T, D), _MXU),
        compiler_params=_params(("arbitrary",)))(x, nw, shift, scale)


def _modulate_bwd(dh, dxout, x, nw, scale, name):
    T = x.shape[0]
    ts = _tile(T, 512)

    def body(dh_ref, dxo_ref, x_ref, nw_ref, sc_ref, dx_ref, acc_ref):
        @pl.when(pl.program_id(0) == 0)
        def _():
            acc_ref[...] = jnp.zeros_like(acc_ref)

        xv, dh_v = x_ref[...], dh_ref[...]
        r = lax.rsqrt(jnp.mean(xv * xv, axis=-1, keepdims=True) + EPS)
        xn = xv * r
        hn = xn * nw_ref[...]
        dhn = dh_v * (1.0 + sc_ref[...])
        acc_ref[0:1, :] += jnp.sum(dh_v, axis=0, keepdims=True)
        acc_ref[1:2, :] += jnp.sum(dh_v * hn, axis=0, keepdims=True)
        acc_ref[2:3, :] += jnp.sum(dhn * xn, axis=0, keepdims=True)
        dxn = dhn * nw_ref[...]
        dx_ref[...] = dxo_ref[...] + r * (dxn - xn * jnp.mean(dxn * xn, axis=-1, keepdims=True))

    return pl.pallas_call(
        body, name=name, grid=(T // ts,),
        in_specs=[_row_spec(ts, D), _row_spec(ts, D), _row_spec(ts, D), _full_spec((1, D)), _full_spec((1, D))],
        out_specs=[_row_spec(ts, D), _full_spec((8, D))],
        out_shape=[jax.ShapeDtypeStruct((T, D), f32), jax.ShapeDtypeStruct((8, D), f32)],
        compiler_params=_params(("arbitrary",)))(dh, dxout, x, nw, scale)


def _final_loss(x, tgt, fw, name):
    T = x.shape[0]
    ts = _tile(T, 512)

    def body(x_ref, t_ref, fw_ref, dx_ref, acc_ref):
        @pl.when(pl.program_id(0) == 0)
        def _():
            acc_ref[...] = jnp.zeros_like(acc_ref)

        xv = x_ref[...]
        r = lax.rsqrt(jnp.mean(xv * xv, axis=-1, keepdims=True) + EPS)
        xn = xv * r
        e = xn * fw_ref[...] - t_ref[...]
        dy = e * (1.0 / D)
        acc_ref[0:1, :] += jnp.sum(dy * xn, axis=0, keepdims=True)
        acc_ref[1:2, :] += jnp.sum(jnp.sum(e * e, axis=0, keepdims=True), axis=1, keepdims=True)
        dxn = dy * fw_ref[...]
        dx_ref[...] = r * (dxn - xn * jnp.mean(dxn * xn, axis=-1, keepdims=True))

    return pl.pallas_call(
        body, name=name, grid=(T // ts,),
        in_specs=[_row_spec(ts, D), _row_spec(ts, D), _full_spec((1, D))],
        out_specs=[_row_spec(ts, D), _full_spec((8, D))],
        out_shape=[jax.ShapeDtypeStruct((T, D), f32), jax.ShapeDtypeStruct((8, D), f32)],
        compiler_params=_params(("arbitrary",)))(x, tgt, fw)


def _tril(n):
    return lax.broadcasted_iota(jnp.int32, (n, n), 0) >= lax.broadcasted_iota(jnp.int32, (n, n), 1)


def _gm_chunk_fwd(u, v, z, lw, lb, ws_ref, bsx):
    gu, gv = _gelu(u), _gelu(v)
    mu = jnp.mean(gv, axis=-1, keepdims=True)
    cen = gv - mu
    rstd = lax.rsqrt(jnp.mean(cen * cen, axis=-1, keepdims=True) + EPS)
    vhat = cen * rstd
    vn = _bf(vhat * lw + lb)
    tri = _tril(Q)
    mixed = jnp.concatenate(
        [_dot(_bf(jnp.where(tri, ws_ref[g], 0.0)), vn[:, g * Q:(g + 1) * Q]) for g in range(NG)], axis=1) + bsx
    return gu, vhat, rstd, vn, mixed


def _gmlp_fwd(pA, lw, lb, ws, bsx, name):
    T = pA.shape[0]
    ts = _tile(T, 512)

    def body(u_ref, v_ref, z_ref, lw_ref, lb_ref, ws_ref, bsx_ref, y_ref):
        def chunk(ci, carry):
            rows = pl.ds(pl.multiple_of(ci * Q, Q), Q)
            gu, _, _, _, mixed = _gm_chunk_fwd(u_ref[rows, :], v_ref[rows, :], z_ref[rows, :], lw_ref[...], lb_ref[...],
                                               ws_ref, bsx_ref[...])
            y_ref[rows, :] = (gu * mixed * _silu(z_ref[rows, :])).astype(y_ref.dtype)
            return carry

        lax.fori_loop(0, ts // Q, chunk, 0)

    return pl.pallas_call(
        body, name=name, grid=(T // ts,),
        in_specs=[_row_spec(ts, D, 0), _row_spec(ts, D, 1), _row_spec(ts, D, 2), _full_spec((1, D)), _full_spec((1, D)),
                  _full_spec((NG, Q, Q)), _full_spec((Q, D))],
        out_specs=_row_spec(ts, D), out_shape=jax.ShapeDtypeStruct((T, D), _MXU),
        compiler_params=_params(("arbitrary",)))(pA, pA, pA, lw, lb, ws, bsx)


def _gmlp_bwd(dya, pA, lw, lb, ws, bsx, name):
    T = pA.shape[0]
    ts = _tile(T, 512)

    def body(dy_ref, u_ref, v_ref, z_ref, lw_ref, lb_ref, ws_ref, bsx_ref, dp_ref, acc_ref, dws_ref, dbs_ref):
        @pl.when(pl.program_id(0) == 0)
        def _():
            acc_ref[...] = jnp.zeros_like(acc_ref)
            dws_ref[...] = jnp.zeros_like(dws_ref)
            dbs_ref[...] = jnp.zeros_like(dbs_ref)

        tri = _tril(Q)

        def chunk(ci, carry):
            rows = pl.ds(pl.multiple_of(ci * Q, Q), Q)
            u, v, z, dy = u_ref[rows, :], v_ref[rows, :], z_ref[rows, :], dy_ref[rows, :]
            gu, vhat, rstd, vn, mixed = _gm_chunk_fwd(u, v, z, lw_ref[...], lb_ref[...], ws_ref, bsx_ref[...])
            sz = _silu(z)
            dp_ref[rows, 0:D] = (dy * mixed * sz * _dgelu(u)).astype(dp_ref.dtype)
            dp_ref[rows, 2 * D:3 * D] = (dy * gu * mixed * _dsilu(z)).astype(dp_ref.dtype)
            dmixed = dy * gu * sz
            dbs_ref[...] += dmixed
            dmb = _bf(dmixed)
            dvn_parts = []
            for g in range(NG):
                cols = slice(g * Q, (g + 1) * Q)
                wg = _bf(jnp.where(tri, ws_ref[g], 0.0))
                dvn_parts.append(_dot_tn(wg, dmb[:, cols]))
                dws_ref[g] += jnp.where(tri, _dot_nt(dmb[:, cols], vn[:, cols]), 0.0)
            dvn = jnp.concatenate(dvn_parts, axis=1)
            acc_ref[0:1, :] += jnp.sum(dvn * vhat, axis=0, keepdims=True)
            acc_ref[1:2, :] += jnp.sum(dvn, axis=0, keepdims=True)
            dvh = dvn * lw_ref[...]
            dgv = rstd * (dvh - jnp.mean(dvh, axis=-1, keepdims=True) - vhat * jnp.mean(dvh * vhat, axis=-1, keepdims=True))
            dp_ref[rows, D:2 * D] = (dgv * _dgelu(v)).astype(dp_ref.dtype)
            return carry

        lax.fori_loop(0, ts // Q, chunk, 0)

    return pl.pallas_call(
        body, name=name, grid=(T // ts,),
        in_specs=[_row_spec(ts, D), _row_spec(ts, D, 0), _row_spec(ts, D, 1), _row_spec(ts, D, 2), _full_spec((1, D)),
                  _full_spec((1, D)), _full_spec((NG, Q, Q)), _full_spec((Q, D))],
        out_specs=[_row_spec(ts, WA), _full_spec((8, D)), _full_spec((NG, Q, Q)), _full_spec((Q, D))],
        out_shape=[jax.ShapeDtypeStruct((T, WA), _MXU), jax.ShapeDtypeStruct((8, D), f32),
                   jax.ShapeDtypeStruct((NG, Q, Q), f32), jax.ShapeDtypeStruct((Q, D), f32)],
        compiler_params=_params(("arbitrary",)))(dya, pA, pA, pA, lw, lb, ws, bsx)


def _expand4(v, g):
    rows = v.shape[0]
    return jnp.concatenate([jnp.broadcast_to(v[:, HPG * g + r:HPG * g + r + 1], (rows, P)) for r in range(HPG)], axis=1)


def _ssd_conv(pb_ref, ext_ref, xc_ref, xa_ref, cw_ref, cb_ref):
    ext_ref[8:8 + Q, :] = pb_ref[:, OFF_XBC:OFF_XBC + CD]
    for j in range(CD // 512):
        cols = slice(j * 512, (j + 1) * 512)
        xc = cb_ref[:, cols]
        for k in range(KC):
            xc = xc + cw_ref[k:k + 1, cols] * ext_ref[5 + k:5 + k + Q, cols]
        xc_ref[:, cols] = xc
        xa_ref[:, cols] = _silu(xc)


def _ssd_time(pb_ref, dtb_ref, alog_ref):
    xdt = pb_ref[:, OFF_DT:OFF_DT + 128] + dtb_ref[...]
    dt = _softplus(xdt)
    a = -jnp.exp(alog_ref[...])
    cs = _dot_exact(_tril(Q).astype(f32), dt * a)
    return xdt, dt, a, cs


def _ssd_group_fwd(g, xa_ref, s_prev, dt, cs, cs_t, ecs, dte, ecl, dsk_ref):
    xs = xa_ref[:, g * GRP:(g + 1) * GRP]
    bb = _bf(xa_ref[:, DI + g * NS:DI + (g + 1) * NS])
    cb = _bf(xa_ref[:, DI + NG * NS + g * NS:DI + NG * NS + (g + 1) * NS])
    gm = _dot_nt(cb, bb)
    xd = xs * _expand4(dt, g)
    tri = _tril(Q)
    ms, ys = [], []
    for r in range(HPG):
        h = HPG * g + r
        lm = jnp.exp(jnp.where(tri, cs[:, h:h + 1] - cs_t[h:h + 1, :], NEG))
        m = gm * lm
        ms.append((m, lm))
        ys.append(_dot(_bf(m), _bf(xd[:, r * P:(r + 1) * P])))
    wv = _dot(cb, _bf(s_prev))
    yoff = _expand4(ecs, g) * wv
    y = jnp.concatenate(ys, axis=1) + yoff + xs * dsk_ref[:, g * GRP:(g + 1) * GRP]
    xdd = xd * _expand4(dte, g)
    s_new = s_prev * _expand4(ecl, g) + _dot_tn(bb, _bf(xdd))
    return y, s_new, (xs, bb, cb, gm, xd, ms, yoff, xdd)


def _ssd_fwd(pB, cw, cb, dtb, alog, dsk, snw, name):
    T = pB.shape[0]
    nc = T // Q

    def body(pb_ref, cw_ref, cb_ref, dtb_ref, alog_ref, dsk_ref, snw_ref, y_ref, st_ref, s_ref, ext_ref, xc_ref, xa_ref):
        @pl.when(pl.program_id(0) == 0)
        def _():
            s_ref[...] = jnp.zeros_like(s_ref)
            ext_ref[...] = jnp.zeros_like(ext_ref)

        _ssd_conv(pb_ref, ext_ref, xc_ref, xa_ref, cw_ref, cb_ref)
        ext_ref[0:8, :] = ext_ref[Q:Q + 8, :]
        _, dt, _, cs = _ssd_time(pb_ref, dtb_ref, alog_ref)
        cs_t = cs.T
        cs_last = cs[Q - 1:Q, :]
        ecs, dte, ecl = jnp.exp(cs), jnp.exp(cs_last - cs), jnp.exp(cs_last)
        for g in range(NG):
            s_prev = s_ref[g]
            st_ref[0, g] = s_prev
            y, s_new, _ = _ssd_group_fwd(g, xa_ref, s_prev, dt, cs, cs_t, ecs, dte, ecl, dsk_ref)
            s_ref[g] = s_new
            cols = slice(g * GRP, (g + 1) * GRP)
            yz = y * _silu(pb_ref[:, cols])
            rr = lax.rsqrt(jnp.mean(yz * yz, axis=-1, keepdims=True) + EPS)
            y_ref[:, cols] = (yz * rr * snw_ref[:, cols]).astype(y_ref.dtype)

    return pl.pallas_call(
        body, name=name, grid=(nc,),
        in_specs=[_row_spec(Q, WB), _full_spec((8, CD)), _full_spec((1, CD)), _full_spec((1, 128)), _full_spec((1, 128)),
                  _full_spec((1, DI)), _full_spec((1, DI))],
        out_specs=[_row_spec(Q, DI), pl.BlockSpec((1, NG, NS, GRP), lambda i: (i, 0, 0, 0))],
        out_shape=[jax.ShapeDtypeStruct((T, DI), _MXU), jax.ShapeDtypeStruct((nc, NG, NS, GRP), f32)],
        scratch_shapes=[pltpu.VMEM((NG, NS, GRP), f32), pltpu.VMEM((Q + 8, CD), f32), pltpu.VMEM((Q, CD), f32),
                        pltpu.VMEM((Q, CD), f32)],
        compiler_params=_params(("arbitrary",)))(pB, cw, cb, dtb, alog, dsk, snw)


def _ssd_bwd(dyb, pB, states, cw, cb, dtb, alog, dsk, snw, name):
    T = pB.shape[0]
    nc = T // Q

    def body(dy_ref, pb_ref, prev_ref, st_ref, cw_ref, cb_ref, dtb_ref, alog_ref, dsk_ref, snw_ref,
             dp_ref, dcw_ref, dcb_ref, dhd_ref, dcol_ref,
             ds_ref, ext_ref, xc_ref, xa_ref, dxa_ref, dxe_ref, dcs_ref, dcst_ref, ddt_ref):
        i = pl.program_id(0)
        c = nc - 1 - i

        @pl.when(i == 0)
        def _():
            ds_ref[...] = jnp.zeros_like(ds_ref)
            dxe_ref[...] = jnp.zeros_like(dxe_ref)
            dcw_ref[...] = jnp.zeros_like(dcw_ref)
            dcb_ref[...] = jnp.zeros_like(dcb_ref)
            dhd_ref[...] = jnp.zeros_like(dhd_ref)
            dcol_ref[...] = jnp.zeros_like(dcol_ref)

        ext_ref[0:8, :] = jnp.where(c > 0, prev_ref[:, OFF_XBC:OFF_XBC + CD], 0.0)
        _ssd_conv(pb_ref, ext_ref, xc_ref, xa_ref, cw_ref, cb_ref)
        xdt, dt, a, cs = _ssd_time(pb_ref, dtb_ref, alog_ref)
        cs_t = cs.T
        cs_last = cs[Q - 1:Q, :]
        ecs, dte, ecl = jnp.exp(cs), jnp.exp(cs_last - cs), jnp.exp(cs_last)
        dcs_ref[...] = jnp.zeros_like(dcs_ref)
        dcst_ref[...] = jnp.zeros_like(dcst_ref)
        ddt_ref[...] = jnp.zeros_like(ddt_ref)

        for g in range(NG):
            cols = slice(g * GRP, (g + 1) * GRP)
            s_prev = st_ref[0, g]
            sb = _bf(s_prev)
            y, _, (xs, bb, cbm, gm, xd, ms, yoff, xdd) = _ssd_group_fwd(g, xa_ref, s_prev, dt, cs, cs_t, ecs, dte, ecl, dsk_ref)
            z = pb_ref[:, cols]
            sz = _silu(z)
            yz = y * sz
            rr = lax.rsqrt(jnp.mean(yz * yz, axis=-1, keepdims=True) + EPS)
            nrm = yz * rr
            dyb_g = dy_ref[:, cols]
            dcol_ref[1:2, cols] += jnp.sum(dyb_g * nrm, axis=0, keepdims=True)
            dn = dyb_g * snw_ref[:, cols]
            dyz = rr * (dn - nrm * jnp.mean(dn * nrm, axis=-1, keepdims=True))
            dyv = dyz * sz
            dp_ref[:, cols] = (dyz * y * _dsilu(z)).astype(dp_ref.dtype)
            dcol_ref[0:1, cols] += jnp.sum(dyv * xs, axis=0, keepdims=True)
            dxs = dyv * dsk_ref[:, cols]
            dyb16 = _bf(dyv)
            dsn = ds_ref[g]
            dsn16 = _bf(dsn)
            dxdd = _dot(bb, dsn16)
            dxd_state = dxdd * _expand4(dte, g)
            t_dte = dxdd * xdd
            t_off = dyv * yoff
            sds = jnp.sum(s_prev * dsn, axis=0, keepdims=True)
            dxd_parts = []
            dg = jnp.zeros((Q, Q), f32)
            for r in range(HPG):
                h = HPG * g + r
                hc = slice(r * P, (r + 1) * P)
                m, lm = ms[r]
                dm = _dot_nt(dyb16[:, hc], _bf(xd[:, hc]))
                dxd_parts.append(_dot_tn(_bf(m), dyb16[:, hc]))
                e = dm * m
                dg = dg + dm * lm
                s_dte = jnp.sum(t_dte[:, hc], axis=1, keepdims=True)
                dcs_ref[:, h:h + 1] += (jnp.sum(e, axis=1, keepdims=True) + jnp.sum(t_off[:, hc], axis=1, keepdims=True) - s_dte)
                dcst_ref[h:h + 1, :] += jnp.sum(e, axis=0, keepdims=True)
                dcs_ref[Q - 1:Q, h:h + 1] += (jnp.sum(s_dte, axis=0, keepdims=True)
                                              + ecl[:, h:h + 1] * jnp.sum(sds[:, hc], axis=1, keepdims=True))
            dxd = jnp.concatenate(dxd_parts, axis=1) + dxd_state
            dxa_ref[:, cols] = dxs + dxd * _expand4(dt, g)
            t_dt = dxd * xs
            for r in range(HPG):
                h = HPG * g + r
                ddt_ref[:, h:h + 1] += jnp.sum(t_dt[:, r * P:(r + 1) * P], axis=1, keepdims=True)
            dg16 = _bf(dg)
            dw16 = _bf(dyv * _expand4(ecs, g))
            dxa_ref[:, DI + NG * NS + g * NS:DI + NG * NS + (g + 1) * NS] = _dot(dg16, bb) + _dot_nt(dw16, sb)
            dxa_ref[:, DI + g * NS:DI + (g + 1) * NS] = _dot_tn(dg16, cbm) + _dot_nt(_bf(xdd), dsn16)
            ds_ref[g] = dsn * _expand4(ecl, g) + _dot_tn(cbm, dw16)

        dcs = dcs_ref[...] - dcst_ref[...].T
        dadt = _dot_exact((lax.broadcasted_iota(jnp.int32, (Q, Q), 0) <= lax.broadcasted_iota(jnp.int32, (Q, Q), 1)).astype(f32), dcs)
        valid = lax.broadcasted_iota(jnp.int32, (Q, 128), 1) < NH
        ddt = jnp.where(valid, ddt_ref[...] + dadt * a, 0.0)
        ddtr = ddt * _sigmoid(xdt)
        dhd_ref[0:1, :] += jnp.sum(ddtr, axis=0, keepdims=True)
        dhd_ref[1:2, :] += jnp.sum(jnp.where(valid, dadt * dt * a, 0.0), axis=0, keepdims=True)
        dp_ref[:, OFF_DT:OFF_DT + 128] = ddtr.astype(dp_ref.dtype)
        dp_ref[:, OFF_DT + 128:WB] = jnp.zeros((Q, WB - OFF_DT - 128), dp_ref.dtype)
        for j in range(CD // 512):
            cj = slice(j * 512, (j + 1) * 512)
            dxc = dxa_ref[:, cj] * _dsilu(xc_ref[:, cj])
            dxe_ref[0:Q, cj] = dxc
            dcb_ref[0:1, cj] += jnp.sum(dxc, axis=0, keepdims=True)
            dxb = jnp.zeros((Q, 512), f32)
            for k in range(KC):
                dcw_ref[k:k + 1, cj] += jnp.sum(dxc * ext_ref[5 + k:5 + k + Q, cj], axis=0, keepdims=True)
                dxb = dxb + cw_ref[k:k + 1, cj] * dxe_ref[3 - k:3 - k + Q, cj]
            dp_ref[:, OFF_XBC + j * 512:OFF_XBC + (j + 1) * 512] = dxb.astype(dp_ref.dtype)
        dxe_ref[Q:Q + 8, :] = dxe_ref[0:8, :]

    rev = lambda i: (nc - 1 - i, 0)
    return pl.pallas_call(
        body, name=name, grid=(nc,),
        in_specs=[pl.BlockSpec((Q, DI), rev), pl.BlockSpec((Q, WB), rev),
                  pl.BlockSpec((8, WB), lambda i: (jnp.maximum((nc - 1 - i) * (Q // 8) - 1, 0), 0)),
                  pl.BlockSpec((1, NG, NS, GRP), lambda i: (nc - 1 - i, 0, 0, 0)),
                  _full_spec((8, CD)), _full_spec((1, CD)), _full_spec((1, 128)), _full_spec((1, 128)),
                  _full_spec((1, DI)), _full_spec((1, DI))],
        out_specs=[pl.BlockSpec((Q, WB), rev), _full_spec((8, CD)), _full_spec((8, CD)), _full_spec((8, 128)),
                   _full_spec((8, DI))],
        out_shape=[jax.ShapeDtypeStruct((T, WB), _MXU), jax.ShapeDtypeStruct((8, CD), f32), jax.ShapeDtypeStruct((8, CD), f32),
                   jax.ShapeDtypeStruct((8, 128), f32), jax.ShapeDtypeStruct((8, DI), f32)],
        scratch_shapes=[pltpu.VMEM((NG, NS, GRP), f32), pltpu.VMEM((Q + 8, CD), f32), pltpu.VMEM((Q, CD), f32),
                        pltpu.VMEM((Q, CD), f32), pltpu.VMEM((Q, CD), f32), pltpu.VMEM((Q + 8, CD), f32),
                        pltpu.VMEM((Q, 128), f32), pltpu.VMEM((128, Q), f32), pltpu.VMEM((Q, 128), f32)],
        compiler_params=_params(("arbitrary",)))(dyb, pB, pB, states, cw, cb, dtb, alog, dsk, snw)


def _merge_fwd(ya, yb, pG, x, gate, wa, wb, wo, name):
    T = x.shape[0]
    ts = _tile(T, 256)

    def body(ya_ref, yb_ref, g_ref, x_ref, gate_ref, wa_ref, wb_ref, wo_ref, xo_ref, mg_ref, pa_ref, pb_ref):
        pa = _dot(ya_ref[...], wa_ref[...])
        pb = _dot(yb_ref[...], wb_ref[...])
        merged = _sigmoid(g_ref[:, 0:D]) * pa + _sigmoid(g_ref[:, D:2 * D]) * pb
        mg = _bf(merged)
        xo_ref[...] = x_ref[...] + gate_ref[...] * _dot(mg, wo_ref[...])
        mg_ref[...] = mg
        pa_ref[...] = pa
        pb_ref[...] = pb

    return pl.pallas_call(
        body, name=name, grid=(T // ts,),
        in_specs=[_row_spec(ts, D), _row_spec(ts, DI), _row_spec(ts, WG), _row_spec(ts, D), _full_spec((1, D)),
                  _full_spec((D, D)), _full_spec((DI, D)), _full_spec((D, D))],
        out_specs=[_row_spec(ts, D)] * 4,
        out_shape=[jax.ShapeDtypeStruct((T, D), f32), jax.ShapeDtypeStruct((T, D), _MXU), jax.ShapeDtypeStruct((T, D), f32),
                   jax.ShapeDtypeStruct((T, D), f32)],
        compiler_params=_params(("arbitrary",)))(ya, yb, pG, x, gate, wa, wb, wo)


def _merge_bwd(dxout, merged, pa, pb, pG, gate, wo, name):
    T = dxout.shape[0]
    ts = _tile(T, 256)

    def body(dx_ref, mg_ref, pa_ref, pb_ref, g_ref, gate_ref, wo_ref, do_ref, dpa_ref, dpb_ref, dg_ref, acc_ref):
        @pl.when(pl.program_id(0) == 0)
        def _():
            acc_ref[...] = jnp.zeros_like(acc_ref)

        dxo = dx_ref[...]
        acc_ref[0:1, :] += jnp.sum(dxo * _dot(mg_ref[...], wo_ref[...]), axis=0, keepdims=True)
        do = _bf(dxo * gate_ref[...])
        do_ref[...] = do
        dmerged = _dot_nt(do, wo_ref[...])
        sa, sb = _sigmoid(g_ref[:, 0:D]), _sigmoid(g_ref[:, D:2 * D])
        dpa_ref[...] = (dmerged * sa).astype(dpa_ref.dtype)
        dpb_ref[...] = (dmerged * sb).astype(dpb_ref.dtype)
        dg_ref[:, 0:D] = (dmerged * pa_ref[...] * sa * (1.0 - sa)).astype(dg_ref.dtype)
        dg_ref[:, D:2 * D] = (dmerged * pb_ref[...] * sb * (1.0 - sb)).astype(dg_ref.dtype)

    return pl.pallas_call(
        body, name=name, grid=(T // ts,),
        in_specs=[_row_spec(ts, D), _row_spec(ts, D), _row_spec(ts, D), _row_spec(ts, D), _row_spec(ts, WG),
                  _full_spec((1, D)), _full_spec((D, D))],
        out_specs=[_row_spec(ts, D), _row_spec(ts, D), _row_spec(ts, D), _row_spec(ts, WG), _full_spec((8, D))],
        out_shape=[jax.ShapeDtypeStruct((T, D), _MXU), jax.ShapeDtypeStruct((T, D), _MXU), jax.ShapeDtypeStruct((T, D), _MXU),
                   jax.ShapeDtypeStruct((T, WG), _MXU), jax.ShapeDtypeStruct((8, D), f32)],
        compiler_params=_params(("arbitrary",)))(dxout, merged, pa, pb, pG, gate, wo)


def _layer_fwd(x, mod, lp, tag):
    h = _modulate(x, lp["nw"], mod[0:1], mod[1:2], f"modulate_{tag}")
    pA = _mm(h, lp["w_gm"], "nn", f"proj_gm_{tag}")
    pB = _mm(h, lp["w_ssd"], "nn", f"proj_ssd_{tag}")
    pG = _mm(h, lp["w_g"], "nn", f"proj_gate_{tag}")
    ya = _gmlp_fwd(pA, lp["lw"], lp["lb"], lp["ws"], lp["bsx"], f"gmlp_fwd_{tag}")
    yb, states = _ssd_fwd(pB, lp["cw"], lp["cb"], lp["dtb"], lp["alog"], lp["dsk"], lp["snw"], f"ssd_fwd_{tag}")
    xo, merged, pa, pb = _merge_fwd(ya, yb, pG, x, mod[2:3], lp["wa"], lp["wb"], lp["wo"], f"merge_fwd_{tag}")
    return xo, dict(x=x, h=h, pA=pA, pB=pB, pG=pG, ya=ya, yb=yb, states=states, merged=merged, pa=pa, pb=pb)


def _layer_bwd(dxo, sv, mod, lp, tag):
    do, dpa, dpb, dpG, s_gate = _merge_bwd(dxo, sv["merged"], sv["pa"], sv["pb"], sv["pG"], mod[2:3], lp["wo"], f"merge_bwd_{tag}")
    g = {}
    g["wo"] = _mm(sv["merged"], do, "tn", f"dw_out_{tag}", out_dtype=_WIRE)
    g["wa"] = _mm(sv["ya"], dpa, "tn", f"dw_proj_a_{tag}", out_dtype=_WIRE)
    g["wb"] = _mm(sv["yb"], dpb, "tn", f"dw_proj_b_{tag}", out_dtype=_WIRE)
    dya = _mm(dpa, lp["wa"], "nt", f"dy_a_{tag}")
    dyb = _mm(dpb, lp["wb"], "nt", f"dy_b_{tag}")
    dpA, s_ln, g["ws"], dbsx = _gmlp_bwd(dya, sv["pA"], lp["lw"], lp["lb"], lp["ws"], lp["bsx"], f"gmlp_bwd_{tag}")
    dpB, g["cw"], s_cb, s_hd, s_col = _ssd_bwd(dyb, sv["pB"], sv["states"], lp["cw"], lp["cb"], lp["dtb"], lp["alog"],
                                               lp["dsk"], lp["snw"], f"ssd_bwd_{tag}")
    dh = _mm(dpG, lp["w_g"], "nt", f"dh_gate_{tag}")
    dh = _mm(dpA, lp["w_gm"], "nt", f"dh_gm_{tag}", acc=dh)
    dh = _mm(dpB, lp["w_ssd"], "nt", f"dh_ssd_{tag}", acc=dh)
    g["w_g"] = _mm(sv["h"], dpG, "tn", f"dw_gate_{tag}", out_dtype=_WIRE)
    g["w_gm"] = _mm(sv["h"], dpA, "tn", f"dw_gm_{tag}", out_dtype=_WIRE)
    g["w_ssd"] = _mm(sv["h"], dpB, "tn", f"dw_ssd_{tag}", out_dtype=_WIRE)
    dx, s_mod = _modulate_bwd(dh, dxo, sv["x"], lp["nw"], mod[1:2], f"modulate_bwd_{tag}")
    g["mod"] = jnp.concatenate([s_mod[0], s_mod[1], s_gate[0]])
    g["nw"] = s_mod[2]
    g["lw"], g["lb"] = s_ln[0], s_ln[1]
    g["bs"] = dbsx.reshape(Q, NG, Q).sum(-1).T
    g["cb"] = s_cb[0]
    g["dtb"], g["alog"] = s_hd[0, :NH], s_hd[1, :NH]
    g["dsk"] = s_col[0].reshape(NH, P).sum(-1)
    g["snw"] = s_col[1]
    g["cw"] = g["cw"][0:KC]
    return dx, g


def _prep_layer(nw, w_in_full, lw, lb, ws, bs, cw_full, cb, dtb, alog, dsk, snw, wa, wb, wo):
    z = jnp.zeros((D, WB - (DI + CD + NH)), w_in_full.dtype)
    pad_h = lambda v: jnp.pad(v, (0, 128 - NH)).reshape(1, 128)
    return dict(
        nw=nw.reshape(1, D),
        w_gm=w_in_full[:, 0:WA],
        w_ssd=jnp.concatenate([w_in_full[:, WA:WA + DI + CD], w_in_full[:, WA + DI + CD:WA + DI + CD + NH], z], axis=1),
        w_g=w_in_full[:, WA + DI + CD + NH:NIN],
        lw=lw.reshape(1, D), lb=lb.reshape(1, D), ws=ws, bsx=jnp.repeat(bs.T, Q, axis=1),
        cw=jnp.pad(cw_full, ((0, 8 - KC), (0, 0))), cb=cb.reshape(1, CD), dtb=pad_h(dtb), alog=pad_h(alog),
        dsk=jnp.repeat(dsk, P).reshape(1, DI), snw=snw.reshape(1, DI), wa=wa, wb=wb, wo=wo)


def _exchange(src, gather, name):
    blk = src.shape if gather else src.shape[1:]
    if not gather:
        assert src.shape[0] == NDEV

    def body(src_ref, out_ref, send_sems, recv_sems, local_sem):
        x, y, c = lax.axis_index("x"), lax.axis_index("y"), lax.axis_index("c")
        me = 4 * x + 2 * y + c

        def peer(j):
            px = 1 - x if (j >> 2) & 1 else x
            py = 1 - y if (j >> 1) & 1 else y
            pc = 1 - c if j & 1 else c
            return (px, py, pc), 4 * px + 2 * py + pc

        def send(j):
            dev, idx = peer(j)
            return pltpu.make_async_remote_copy(
                src_ref=src_ref if gather else src_ref.at[idx], dst_ref=out_ref.at[me],
                send_sem=send_sems.at[j - 1], recv_sem=recv_sems.at[j - 1], device_id=dev, device_id_type=pl.DeviceIdType.MESH)

        def recv(j):
            dev, idx = peer(j)
            return pltpu.make_async_remote_copy(
                src_ref=src_ref if gather else src_ref.at[idx], dst_ref=out_ref.at[idx],
                send_sem=send_sems.at[j - 1], recv_sem=recv_sems.at[j - 1], device_id=dev, device_id_type=pl.DeviceIdType.MESH)

        mine = pltpu.make_async_copy(src_ref if gather else src_ref.at[me], out_ref.at[me], local_sem)
        mine.start()
        sends = [send(j) for j in range(1, NDEV)]
        for cp in sends:
            cp.start()
        for j in range(1, NDEV):
            recv(j).wait_recv()
        for cp in sends:
            cp.wait_send()
        mine.wait()

    return pl.pallas_call(
        body, name=name, out_shape=jax.ShapeDtypeStruct((NDEV,) + tuple(blk), src.dtype),
        in_specs=[pl.BlockSpec(memory_space=pl.ANY)], out_specs=pl.BlockSpec(memory_space=pl.ANY),
        scratch_shapes=[pltpu.SemaphoreType.DMA((NDEV - 1,)), pltpu.SemaphoreType.DMA((NDEV - 1,)), pltpu.SemaphoreType.DMA],
    )(src)


def _mod_fwd(c8, ada_full, ada_b, name):
    L = ada_full.shape[0]

    def body(c_ref, w_ref, b_ref, o_ref, sc_ref):
        sc = _silu(c_ref[...])
        sc_ref[...] = sc
        o_ref[0] = _dot(_bf(sc), w_ref[0]) + b_ref[0]

    return pl.pallas_call(
        body, name=name, grid=(L,),
        in_specs=[_full_spec((8, D)), pl.BlockSpec((1, D, 3 * D), lambda l: (l, 0, 0)), pl.BlockSpec((1, 1, 3 * D), lambda l: (l, 0, 0))],
        out_specs=[pl.BlockSpec((1, 8, 3 * D), lambda l: (l, 0, 0)), _full_spec((8, D))],
        out_shape=[jax.ShapeDtypeStruct((L, 8, 3 * D), f32), jax.ShapeDtypeStruct((8, D), f32)],
        compiler_params=_params(("arbitrary",)))(c8, ada_full, ada_b)


def _ada_w_grad(sc_all, dmod_cols, name):
    W = dmod_cols.shape[1]

    def body(s_ref, d_ref, o_ref):
        o_ref[...] = lax.dot_general(s_ref[...], d_ref[...], (((0,), (0,)), ((), ())), preferred_element_type=f32,
                                     precision=lax.Precision.HIGHEST)

    return pl.pallas_call(body, name=name, out_shape=jax.ShapeDtypeStruct((D, W), f32))(sc_all, dmod_cols)


def _adamw_math(w, g, m, v):
    m = ADAM_B1 * m + (1.0 - ADAM_B1) * g
    v = ADAM_B2 * v + (1.0 - ADAM_B2) * (g * g)
    m_hat = m / (1.0 - ADAM_B1 ** ADAM_STEP)
    v_hat = v / (1.0 - ADAM_B2 ** ADAM_STEP)
    delta = -ADAM_LR * (m_hat / (jnp.sqrt(v_hat) + ADAM_EPS) + ADAM_WD * w)
    return delta, m, v


def _sum_adamw(recv, w, m, v, name):
    n, R, C = recv.shape
    tr = _tile(R, 128 if C > 1024 else 256)

    def body(r_ref, w_ref, m_ref, v_ref, g_ref, d_ref, nm_ref, nv_ref):
        g = r_ref[0].astype(f32)
        for k in range(1, n):
            g = g + r_ref[k].astype(f32)
        g_ref[...] = g
        d_ref[...], nm_ref[...], nv_ref[...] = _adamw_math(w_ref[...], g, m_ref[...], v_ref[...])

    spec = pl.BlockSpec((tr, C), lambda i: (i, 0))
    return pl.pallas_call(
        body, name=name, grid=(R // tr,),
        in_specs=[pl.BlockSpec((n, tr, C), lambda i: (0, i, 0)), spec, spec, spec], out_specs=[spec] * 4,
        out_shape=[jax.ShapeDtypeStruct((R, C), f32)] * 4, compiler_params=_params(("arbitrary",)))(recv, w, m, v)


def _pack(arrays, rows):
    flat = []
    for a in arrays:
        a = a.reshape(-1).astype(f32)
        flat.append(jnp.pad(a, (0, (-a.shape[0]) % 128)))
    flat = jnp.concatenate(flat)
    return jnp.pad(flat, (0, rows * 128 - flat.shape[0])).reshape(rows, 128)


def _unpack(slab, shapes):
    flat = slab.reshape(-1)
    out, off = [], 0
    for s in shapes:
        n = 1
        for d in s:
            n *= d
        out.append(flat[off:off + n].reshape(s))
        off += n + (-n) % 128
    return out


_SMALL = ["ada_b", "norm_w", "gm_ln_w", "gm_ln_b", "gm_ws", "gm_bs", "conv_b", "dt_bias", "a_log", "d_skip", "ssm_norm_w",
          "final_norm_w"]


def kernel(x, c, ada_w, ada_b, norm_w, w_in, gm_ln_w, gm_ln_b, gm_ws, gm_bs, conv_w, conv_b, dt_bias, a_log, d_skip, ssm_norm_w, w_proj_a, w_proj_b, w_out, final_norm_w, loss_target, m_ada_w, m_ada_b, m_norm_w, m_w_in, m_gm_ln_w, m_gm_ln_b, m_gm_ws, m_gm_bs, m_conv_w, m_conv_b, m_dt_bias, m_a_log, m_d_skip, m_ssm_norm_w, m_w_proj_a, m_w_proj_b, m_w_out, m_final_norm_w, v_ada_w, v_ada_b, v_norm_w, v_w_in, v_gm_ln_w, v_gm_ln_b, v_gm_ws, v_gm_bs, v_conv_w, v_conv_b, v_dt_bias, v_a_log, v_d_skip, v_ssm_norm_w, v_w_proj_a, v_w_proj_b, v_w_out, v_final_norm_w):
    L = 2
    me = 4 * lax.axis_index("x") + 2 * lax.axis_index("y") + lax.axis_index("c")
    W = dict(ada_w=ada_w, ada_b=ada_b, norm_w=norm_w, w_in=w_in, gm_ln_w=gm_ln_w, gm_ln_b=gm_ln_b, gm_ws=gm_ws, gm_bs=gm_bs,
             conv_w=conv_w, conv_b=conv_b, dt_bias=dt_bias, a_log=a_log, d_skip=d_skip, ssm_norm_w=ssm_norm_w, w_proj_a=w_proj_a,
             w_proj_b=w_proj_b, w_out=w_out, final_norm_w=final_norm_w)
    M = dict(ada_w=m_ada_w, ada_b=m_ada_b, norm_w=m_norm_w, w_in=m_w_in, gm_ln_w=m_gm_ln_w, gm_ln_b=m_gm_ln_b, gm_ws=m_gm_ws,
             gm_bs=m_gm_bs, conv_w=m_conv_w, conv_b=m_conv_b, dt_bias=m_dt_bias, a_log=m_a_log, d_skip=m_d_skip,
             ssm_norm_w=m_ssm_norm_w, w_proj_a=m_w_proj_a, w_proj_b=m_w_proj_b, w_out=m_w_out, final_norm_w=m_final_norm_w)
    V = dict(ada_w=v_ada_w, ada_b=v_ada_b, norm_w=v_norm_w, w_in=v_w_in, gm_ln_w=v_gm_ln_w, gm_ln_b=v_gm_ln_b, gm_ws=v_gm_ws,
             gm_bs=v_gm_bs, conv_w=v_conv_w, conv_b=v_conv_b, dt_bias=v_dt_bias, a_log=v_a_log, d_skip=v_d_skip,
             ssm_norm_w=v_ssm_norm_w, w_proj_a=v_w_proj_a, w_proj_b=v_w_proj_b, w_out=v_w_out, final_norm_w=v_final_norm_w)
    SW = NIN // NDEV
    AW = 3 * D // NDEV
    CW = CD // NDEV
    RA, RB = D // NDEV, DI // NDEV

    g_win = _exchange(w_in.astype(_WIRE).reshape(L * D, SW), True, "gather_w_in").reshape(NDEV, L, D, SW)
    g_ada = _exchange(ada_w.astype(_WIRE).reshape(L * D, AW), True, "gather_ada_w").reshape(NDEV, L, D, AW)
    rows = jnp.concatenate([w_proj_a.reshape(L * RA, D), w_proj_b.reshape(L * RB, D), w_out.reshape(L * RA, D)], axis=0)
    g_rows = _exchange(rows.astype(_WIRE), True, "gather_w_rows")
    g_pa = g_rows[:, 0:L * RA].reshape(NDEV, L, RA, D)
    g_pb = g_rows[:, L * RA:L * (RA + RB)].reshape(NDEV, L, RB, D)
    g_po = g_rows[:, L * (RA + RB):].reshape(NDEV, L, RA, D)
    g_cw = _exchange(conv_w.reshape(L * KC, CW), True, "gather_conv_w").reshape(NDEV, L, KC, CW)
    cols = lambda g, l: jnp.transpose(g[:, l], (1, 0, 2)).reshape(g.shape[2], -1)
    ada_full = jnp.stack([cols(g_ada, l) for l in range(L)])

    mod8, sc8 = _mod_fwd(jnp.broadcast_to(c, (8, D)), ada_full, ada_b.reshape(L, 1, 3 * D), "mod_fwd")
    lps, mods = [], []
    for l in range(L):
        lps.append(_prep_layer(norm_w[l], cols(g_win, l), gm_ln_w[l], gm_ln_b[l], gm_ws[l], gm_bs[l], cols(g_cw, l), conv_b[l],
                               dt_bias[l], a_log[l], d_skip[l], ssm_norm_w[l], g_pa[:, l].reshape(D, D), g_pb[:, l].reshape(DI, D),
                               g_po[:, l].reshape(D, D)))
        mods.append(mod8[l, 0].reshape(3, D))

    h = x[0]
    saved = []
    for l in range(L):
        h, sv = _layer_fwd(h, mods[l], lps[l], f"l{l}")
        saved.append(sv)
    dh, s_fin = _final_loss(h, loss_target[0], final_norm_w.reshape(1, D), "final_loss")
    loss = lax.psum(0.5 / D * s_fin[1, 0], ("x", "y", "c"))
    grads = [None] * L
    for l in reversed(range(L)):
        dh, grads[l] = _layer_bwd(dh, saved[l], mods[l], lps[l], f"l{l}")
    grad_x = dh[None]

    def win_blocks(g):
        full = jnp.concatenate([g["w_gm"], g["w_ssd"][:, 0:DI + CD], g["w_ssd"][:, DI + CD:DI + CD + NH], g["w_g"]], axis=1)
        return jnp.transpose(full.reshape(D, NDEV, SW), (1, 0, 2))
    send_win = jnp.stack([win_blocks(g) for g in grads], axis=1).reshape(NDEV, L * D, SW)
    recv_win = _exchange(send_win, False, "scatter_w_in")
    o_win = _sum_adamw(recv_win, w_in.reshape(L * D, SW), m_w_in.reshape(L * D, SW), v_w_in.reshape(L * D, SW), "adamw_w_in")
    o_win = [o.reshape(L, D, SW) for o in o_win]

    rb = lambda key, r: jnp.stack([g[key].reshape(NDEV, r, D) for g in grads], axis=1).reshape(NDEV, L * r, D)
    send_rows = jnp.concatenate([rb("wa", RA), rb("wb", RB), rb("wo", RA)], axis=1)
    recv_rows = _exchange(send_rows, False, "scatter_w_rows")
    stack_rows = lambda d: jnp.concatenate([d["w_proj_a"].reshape(L * RA, D), d["w_proj_b"].reshape(L * RB, D),
                                            d["w_out"].reshape(L * RA, D)], axis=0)
    o_rows = _sum_adamw(recv_rows, stack_rows(W), stack_rows(M), stack_rows(V), "adamw_w_rows")
    o_pa = [o[0:L * RA].reshape(L, RA, D) for o in o_rows]
    o_pb = [o[L * RA:L * (RA + RB)].reshape(L, RB, D) for o in o_rows]
    o_po = [o[L * (RA + RB):].reshape(L, RA, D) for o in o_rows]

    st = lambda key: jnp.stack([g[key] for g in grads])
    small_g = dict(ada_b=st("mod"), norm_w=st("nw"), gm_ln_w=st("lw"), gm_ln_b=st("lb"), gm_ws=st("ws"), gm_bs=st("bs"),
                   conv_b=st("cb"), dt_bias=st("dtb"), a_log=st("alog"), d_skip=st("dsk"), ssm_norm_w=st("snw"),
                   final_norm_w=s_fin[0])
    extra = [st("cw"), sc8[0]]
    shapes = [W[n].shape for n in _SMALL] + [(L, KC, CD), (D,)]
    n_rows = sum(-(-functools.reduce(lambda a, b: a * b, s, 1) // 128) for s in shapes)
    n_rows = -(-n_rows // 256) * 256
    zeros_extra = [jnp.zeros((L, KC, CD), f32), jnp.zeros((D,), f32)]
    slab_g = _pack([small_g[n] for n in _SMALL] + extra, n_rows)
    slab_all = _exchange(slab_g, True, "gather_small")
    o_small = _sum_adamw(slab_all, _pack([W[n] for n in _SMALL] + zeros_extra, n_rows),
                         _pack([M[n] for n in _SMALL] + zeros_extra, n_rows),
                         _pack([V[n] for n in _SMALL] + zeros_extra, n_rows), "adamw_small")
    o_small = [_unpack(o, shapes) for o in o_small]
    small_out = {n: [o_small[k][i] for k in range(4)] for i, n in enumerate(_SMALL)}

    g_cw_mine = lax.dynamic_slice_in_dim(o_small[0][len(_SMALL)], me * CW, CW, axis=2)
    o_cw = _sum_adamw(g_cw_mine.reshape(1, L * KC, CW), conv_w.reshape(L * KC, CW), m_conv_w.reshape(L * KC, CW),
                      v_conv_w.reshape(L * KC, CW), "adamw_conv_w")
    o_cw = [o.reshape(L, KC, CW) for o in o_cw]

    per_dev = [_unpack(slab_all[k], shapes) for k in range(NDEV)]
    sc_all = jnp.stack([p[len(_SMALL) + 1] for p in per_dev])
    dmod_all = jnp.stack([p[0] for p in per_dev])
    dmod_cols = lax.dynamic_slice_in_dim(dmod_all, me * AW, AW, axis=2).reshape(NDEV, L * AW)
    g_ada_w = jnp.transpose(_ada_w_grad(sc_all, dmod_cols, "ada_w_grad").reshape(D, L, AW), (1, 0, 2))
    o_ada = _sum_adamw(g_ada_w.reshape(1, L * D, AW), ada_w.reshape(L * D, AW), m_ada_w.reshape(L * D, AW),
                       v_ada_w.reshape(L * D, AW), "adamw_ada_w")
    o_ada = [o.reshape(L, D, AW) for o in o_ada]

    big = dict(ada_w=o_ada, w_in=o_win, conv_w=o_cw, w_proj_a=o_pa, w_proj_b=o_pb, w_out=o_po)
    order = ["ada_w", "ada_b", "norm_w", "w_in", "gm_ln_w", "gm_ln_b", "gm_ws", "gm_bs", "conv_w", "conv_b", "dt_bias", "a_log",
             "d_skip", "ssm_norm_w", "w_proj_a", "w_proj_b", "w_out", "final_norm_w"]
    pick = lambda n, k: big[n][k] if n in big else small_out[n][k]
    return (loss, grad_x, *[pick(n, 0) for n in order], *[pick(n, 1) for n in order], *[pick(n, 2) for n in order],
            *[pick(n, 3) for n in order])
```

```python
import functools

import jax
import jax.numpy as jnp
from jax import lax
from jax.experimental import pallas as pl
from jax.experimental.pallas import tpu as pltpu

f32 = jnp.float32
_MXU = jnp.bfloat16
_WIRE = jnp.bfloat16

D = 1024
Q = 128
NG = 8
DI = 2048
NH = 32
P = 64
HPG = 4
NS = 128
KC = 4
CD = 4096
GRP = DI // NG
EPS = 1e-6
NDEV = 8
WA = 3 * D
WB = DI + CD + 256
WG = 2 * D
OFF_XBC = DI
OFF_DT = DI + CD
NIN = 11296
VMEM_LIMIT = 56 * 1024 * 1024
NEG = -1e30

ADAM_LR, ADAM_B1, ADAM_B2, ADAM_EPS, ADAM_WD, ADAM_STEP = 0.001, 0.9, 0.999, 1e-08, 0.01, 10


def _bf(x):
    return x.astype(_MXU)


def _dot(a, b):
    return jnp.dot(a, b, preferred_element_type=f32)


def _dot_nt(a, b):
    return lax.dot_general(a, b, (((1,), (1,)), ((), ())), preferred_element_type=f32)


def _dot_tn(a, b):
    return lax.dot_general(a, b, (((0,), (0,)), ((), ())), preferred_element_type=f32)


def _dot_exact(a, b):
    return jnp.dot(a, b, preferred_element_type=f32, precision=lax.Precision.HIGHEST)


def _sigmoid(x):
    return jax.nn.sigmoid(x)


def _silu(x):
    return x * _sigmoid(x)


def _dsilu(x):
    s = _sigmoid(x)
    return s * (1.0 + x * (1.0 - s))


_GK = 0.7978845608028654
_GC = 0.044715


def _gelu(x):
    return 0.5 * x * (1.0 + jnp.tanh(_GK * (x + _GC * x * x * x)))


def _dgelu(x):
    t = jnp.tanh(_GK * (x + _GC * x * x * x))
    return 0.5 * (1.0 + t) + 0.5 * x * (1.0 - t * t) * _GK * (1.0 + 3.0 * _GC * x * x)


def _softplus(x):
    return jnp.maximum(x, 0.0) + jnp.log1p(jnp.exp(-jnp.abs(x)))


def _tile(n, cap):
    if n <= cap:
        return n
    best = None
    for t in range(128, cap + 1, 128):
        if n % t == 0:
            best = t
    assert best is not None, (n, cap)
    return best


def _params(sem):
    return pltpu.CompilerParams(dimension_semantics=sem, vmem_limit_bytes=VMEM_LIMIT)


def _mm(a, b, mode, name, acc=None, out_dtype=f32, tm_cap=512, tn_cap=1280, tk_cap=1280):
    if mode == "nn":
        (M, K), (K2, N) = a.shape, b.shape
    elif mode == "nt":
        (M, K), (N, K2) = a.shape, b.shape
    else:
        (K, M), (K2, N) = a.shape, b.shape
        tk_cap = min(tk_cap, 512)
    assert K == K2, (a.shape, b.shape, mode)
    tm, tn, tk = _tile(M, tm_cap), _tile(N, tn_cap), _tile(K, tk_cap)
    nk = K // tk
    if mode == "nn":
        a_spec = pl.BlockSpec((tm, tk), lambda i, j, k: (i, k))
        b_spec = pl.BlockSpec((tk, tn), lambda i, j, k: (k, j))
        dot = _dot
    elif mode == "nt":
        a_spec = pl.BlockSpec((tm, tk), lambda i, j, k: (i, k))
        b_spec = pl.BlockSpec((tn, tk), lambda i, j, k: (j, k))
        dot = _dot_nt
    else:
        a_spec = pl.BlockSpec((tk, tm), lambda i, j, k: (k, i))
        b_spec = pl.BlockSpec((tk, tn), lambda i, j, k: (k, j))
        dot = _dot_tn
    o_spec = pl.BlockSpec((tm, tn), lambda i, j, k: (i, j))
    has_acc = acc is not None

    def body(*refs):
        if has_acc:
            a_ref, b_ref, c_ref, o_ref, acc_ref = refs
        else:
            a_ref, b_ref, o_ref, acc_ref = refs
        k = pl.program_id(2)

        @pl.when(k == 0)
        def _():
            if has_acc:
                acc_ref[...] = c_ref[...]
            else:
                acc_ref[...] = jnp.zeros_like(acc_ref)

        acc_ref[...] += dot(_bf(a_ref[...]), _bf(b_ref[...]))

        @pl.when(k == nk - 1)
        def _():
            o_ref[...] = acc_ref[...].astype(out_dtype)

    in_specs = [a_spec, b_spec] + ([o_spec] if has_acc else [])
    args = (a, b) + ((acc,) if has_acc else ())
    return pl.pallas_call(
        body, name=name, grid=(M // tm, N // tn, nk), in_specs=in_specs, out_specs=o_spec,
        out_shape=jax.ShapeDtypeStruct((M, N), out_dtype), scratch_shapes=[pltpu.VMEM((tm, tn), f32)],
        compiler_params=_params(("arbitrary", "arbitrary", "arbitrary")))(*args)


def _row_spec(ts, w, col=0):
    return pl.BlockSpec((ts, w), lambda i: (i, col))


def _full_spec(shape):
    nd = len(shape)
    return pl.BlockSpec(shape, lambda i: (0,) * nd)


def _modulate(x, nw, shift, scale, name):
    T = x.shape[0]
    ts = _tile(T, 512)

    def body(x_ref, nw_ref, sh_ref, sc_ref, h_ref):
        xv = x_ref[...]
        r = lax.rsqrt(jnp.mean(xv * xv, axis=-1, keepdims=True) + EPS)
        h_ref[...] = ((xv * r * nw_ref[...]) * (1.0 + sc_ref[...]) + sh_ref[...]).astype(h_ref.dtype)

    return pl.pallas_call(
        body, name=name, grid=(T // ts,),
        in_specs=[_row_spec(ts, D), _full_spec((1, D)), _full_spec((1, D)), _full_spec((1, D))],
        out_specs=_row_spec(ts, D), out_shape=jax.ShapeDtypeStruct((T, D), _MXU),
        compiler_params=_params(("arbitrary",)))(x, nw, shift, scale)


def _modulate_bwd(dh, dxout, x, nw, scale, name):
    T = x.shape[0]
    ts = _tile(T, 512)

    def body(dh_ref, dxo_ref, x_ref, nw_ref, sc_ref, dx_ref, acc_ref):
        @pl.when(pl.program_id(0) == 0)
        def _():
            acc_ref[...] = jnp.zeros_like(acc_ref)

        xv, dh_v = x_ref[...], dh_ref[...]
        r = lax.rsqrt(jnp.mean(xv * xv, axis=-1, keepdims=True) + EPS)
        xn = xv * r
        hn = xn * nw_ref[...]
        dhn = dh_v * (1.0 + sc_ref[...])
        acc_ref[0:1, :] += jnp.sum(dh_v, axis=0, keepdims=True)
        acc_ref[1:2, :] += jnp.sum(dh_v * hn, axis=0, keepdims=True)
        acc_ref[2:3, :] += jnp.sum(dhn * xn, axis=0, keepdims=True)
        dxn = dhn * nw_ref[...]
        dx_ref[...] = dxo_ref[...] + r * (dxn - xn * jnp.mean(dxn * xn, axis=-1, keepdims=True))

    return pl.pallas_call(
        body, name=name, grid=(T // ts,),
        in_specs=[_row_spec(ts, D), _row_spec(ts, D), _row_spec(ts, D), _full_spec((1, D)), _full_spec((1, D))],
        out_specs=[_row_spec(ts, D), _full_spec((8, D))],
        out_shape=[jax.ShapeDtypeStruct((T, D), f32), jax.ShapeDtypeStruct((8, D), f32)],
        compiler_params=_params(("arbitrary",)))(dh, dxout, x, nw, scale)


def _final_loss(x, tgt, fw, name):
    T = x.shape[0]
    ts = _tile(T, 512)

    def body(x_ref, t_ref, fw_ref, dx_ref, acc_ref):
        @pl.when(pl.program_id(0) == 0)
        def _():
            acc_ref[...] = jnp.zeros_like(acc_ref)

        xv = x_ref[...]
        r = lax.rsqrt(jnp.mean(xv * xv, axis=-1, keepdims=True) + EPS)
        xn = xv * r
        e = xn * fw_ref[...] - t_ref[...]
        dy = e * (1.0 / D)
        acc_ref[0:1, :] += jnp.sum(dy * xn, axis=0, keepdims=True)
        acc_ref[1:2, :] += jnp.sum(jnp.sum(e * e, axis=0, keepdims=True), axis=1, keepdims=True)
        dxn = dy * fw_ref[...]
        dx_ref[...] = r * (dxn - xn * jnp.mean(dxn * xn, axis=-1, keepdims=True))

    return pl.pallas_call(
        body, name=name, grid=(T // ts,),
        in_specs=[_row_spec(ts, D), _row_spec(ts, D), _full_spec((1, D))],
        out_specs=[_row_spec(ts, D), _full_spec((8, D))],
        out_shape=[jax.ShapeDtypeStruct((T, D), f32), jax.ShapeDtypeStruct((8, D), f32)],
        compiler_params=_params(("arbitrary",)))(x, tgt, fw)


def _tril(n):
    return lax.broadcasted_iota(jnp.int32, (n, n), 0) >= lax.broadcasted_iota(jnp.int32, (n, n), 1)


def _gm_chunk_fwd(u, v, z, lw, lb, ws_ref, bsx):
    gu, gv = _gelu(u), _gelu(v)
    mu = jnp.mean(gv, axis=-1, keepdims=True)
    cen = gv - mu
    rstd = lax.rsqrt(jnp.mean(cen * cen, axis=-1, keepdims=True) + EPS)
    vhat = cen * rstd
    vn = _bf(vhat * lw + lb)
    tri = _tril(Q)
    mixed = jnp.concatenate(
        [_dot(_bf(jnp.where(tri, ws_ref[g], 0.0)), vn[:, g * Q:(g + 1) * Q]) for g in range(NG)], axis=1) + bsx
    return gu, vhat, rstd, vn, mixed


def _gmlp_fwd(pA, lw, lb, ws, bsx, name):
    T = pA.shape[0]
    ts = _tile(T, 512)

    def body(u_ref, v_ref, z_ref, lw_ref, lb_ref, ws_ref, bsx_ref, y_ref):
        def chunk(ci, carry):
            rows = pl.ds(pl.multiple_of(ci * Q, Q), Q)
            gu, _, _, _, mixed = _gm_chunk_fwd(u_ref[rows, :], v_ref[rows, :], z_ref[rows, :], lw_ref[...], lb_ref[...],
                                               ws_ref, bsx_ref[...])
            y_ref[rows, :] = (gu * mixed * _silu(z_ref[rows, :])).astype(y_ref.dtype)
            return carry

        lax.fori_loop(0, ts // Q, chunk, 0)

    return pl.pallas_call(
        body, name=name, grid=(T // ts,),
        in_specs=[_row_spec(ts, D, 0), _row_spec(ts, D, 1), _row_spec(ts, D, 2), _full_spec((1, D)), _full_spec((1, D)),
                  _full_spec((NG, Q, Q)), _full_spec((Q, D))],
        out_specs=_row_spec(ts, D), out_shape=jax.ShapeDtypeStruct((T, D), _MXU),
        compiler_params=_params(("arbitrary",)))(pA, pA, pA, lw, lb, ws, bsx)


def _gmlp_bwd(dya, pA, lw, lb, ws, bsx, name):
    T = pA.shape[0]
    ts = _tile(T, 512)

    def body(dy_ref, u_ref, v_ref, z_ref, lw_ref, lb_ref, ws_ref, bsx_ref, dp_ref, acc_ref, dws_ref, dbs_ref):
        @pl.when(pl.program_id(0) == 0)
        def _():
            acc_ref[...] = jnp.zeros_like(acc_ref)
            dws_ref[...] = jnp.zeros_like(dws_ref)
            dbs_ref[...] = jnp.zeros_like(dbs_ref)

        tri = _tril(Q)

        def chunk(ci, carry):
            rows = pl.ds(pl.multiple_of(ci * Q, Q), Q)
            u, v, z, dy = u_ref[rows, :], v_ref[rows, :], z_ref[rows, :], dy_ref[rows, :]
            gu, vhat, rstd, vn, mixed = _gm_chunk_fwd(u, v, z, lw_ref[...], lb_ref[...], ws_ref, bsx_ref[...])
            sz = _silu(z)
            dp_ref[rows, 0:D] = (dy * mixed * sz * _dgelu(u)).astype(dp_ref.dtype)
            dp_ref[rows, 2 * D:3 * D] = (dy * gu * mixed * _dsilu(z)).astype(dp_ref.dtype)
            dmixed = dy * gu * sz
            dbs_ref[...] += dmixed
            dmb = _bf(dmixed)
            dvn_parts = []
            for g in range(NG):
                cols = slice(g * Q, (g + 1) * Q)
                wg = _bf(jnp.where(tri, ws_ref[g], 0.0))
                dvn_parts.append(_dot_tn(wg, dmb[:, cols]))
                dws_ref[g] += jnp.where(tri, _dot_nt(dmb[:, cols], vn[:, cols]), 0.0)
            dvn = jnp.concatenate(dvn_parts, axis=1)
            acc_ref[0:1, :] += jnp.sum(dvn * vhat, axis=0, keepdims=True)
            acc_ref[1:2, :] += jnp.sum(dvn, axis=0, keepdims=True)
            dvh = dvn * lw_ref[...]
            dgv = rstd * (dvh - jnp.mean(dvh, axis=-1, keepdims=True) - vhat * jnp.mean(dvh * vhat, axis=-1, keepdims=True))
            dp_ref[rows, D:2 * D] = (dgv * _dgelu(v)).astype(dp_ref.dtype)
            return carry

        lax.fori_loop(0, ts // Q, chunk, 0)

    return pl.pallas_call(
        body, name=name, grid=(T // ts,),
        in_specs=[_row_spec(ts, D), _row_spec(ts, D, 0), _row_spec(ts, D, 1), _row_spec(ts, D, 2), _full_spec((1, D)),
                  _full_spec((1, D)), _full_spec((NG, Q, Q)), _full_spec((Q, D))],
        out_specs=[_row_spec(ts, WA), _full_spec((8, D)), _full_spec((NG, Q, Q)), _full_spec((Q, D))],
        out_shape=[jax.ShapeDtypeStruct((T, WA), _MXU), jax.ShapeDtypeStruct((8, D), f32),
                   jax.ShapeDtypeStruct((NG, Q, Q), f32), jax.ShapeDtypeStruct((Q, D), f32)],
        compiler_params=_params(("arbitrary",)))(dya, pA, pA, pA, lw, lb, ws, bsx)


def _head_maps():
    h = lax.broadcasted_iota(jnp.int32, (128, DI), 0)
    ch = lax.broadcasted_iota(jnp.int32, (128, DI), 1)
    ex = (ch // P == h).astype(_MXU)
    return ex, ex.T


def _split(v, parts):
    out = []
    for _ in range(parts - 1):
        p = _bf(v)
        out.append(p)
        v = v - p.astype(f32)
    out.append(_bf(v))
    return out


def _expand(v, ex_ref, parts):
    acc = None
    for p in _split(v, parts):
        t = _dot(p, ex_ref[...])
        acc = t if acc is None else acc + t
    return acc


def _reduce(v, rd_ref):
    hi, lo = _split(v, 2)
    return _dot(hi, rd_ref[...]) + _dot(lo, rd_ref[...])


def _ssd_time(pb_ref, dtb_ref, alog_ref):
    xdt = pb_ref[:, OFF_DT:OFF_DT + 128] + dtb_ref[...]
    dt = _softplus(xdt)
    a = -jnp.exp(alog_ref[...])
    cs = _dot_exact(_tril(Q).astype(f32), dt * a)
    return xdt, dt, a, cs


def _head_mask(r):
    return lax.broadcasted_iota(jnp.int32, (Q, GRP), 1) // P == r


def _ssd_group_fwd(g, xa_ref, s_prev, cs, cs_t, dtx_ref, csx_ref, dsk_ref):
    cols = slice(g * GRP, (g + 1) * GRP)
    xs = xa_ref[:, cols]
    bb = _bf(xa_ref[:, DI + g * NS:DI + (g + 1) * NS])
    cb = _bf(xa_ref[:, DI + NG * NS + g * NS:DI + NG * NS + (g + 1) * NS])
    gm = _dot_nt(cb, bb)
    xd = xs * dtx_ref[:, cols]
    csx = csx_ref[:, cols]
    csl = csx_ref[Q - 1:Q, cols]
    tri = _tril(Q)
    lms = [jnp.exp(jnp.where(tri, cs[:, HPG * g + r:HPG * g + r + 1] - cs_t[HPG * g + r:HPG * g + r + 1, :], NEG))
           for r in range(HPG)]
    mfs = [gm * lm for lm in lms]
    mcat = jnp.concatenate([_bf(m) for m in mfs], axis=1)
    xbd = jnp.concatenate([_bf(jnp.where(_head_mask(r), xd, 0.0)) for r in range(HPG)], axis=0)
    ydiag = _dot(mcat, xbd)
    yoff = jnp.exp(csx) * _dot(cb, _bf(s_prev))
    y = ydiag + yoff + xs * dsk_ref[:, cols]
    xdd = xd * jnp.exp(csl - csx)
    s_new = s_prev * jnp.exp(csl) + _dot_tn(bb, _bf(xdd))
    return y, s_new, (xs, bb, cb, xd, lms, mfs, mcat, xbd, yoff, xdd, csx, csl)


def _ssd_fwd(pB, cw, cb, dtb, alog, dsk, snw, name):
    T = pB.shape[0]
    nc = T // Q
    ex, _ = _head_maps()

    def body(pb_ref, cw_ref, cb_ref, dtb_ref, alog_ref, dsk_ref, snw_ref, ex_ref, y_ref, xc_ref, st_ref,
             s_ref, ext_ref, xa_ref, dtx_ref, csx_ref):
        @pl.when(pl.program_id(0) == 0)
        def _():
            s_ref[...] = jnp.zeros_like(s_ref)
            ext_ref[...] = jnp.zeros_like(ext_ref)

        ext_ref[8:8 + Q, :] = pb_ref[:, OFF_XBC:OFF_XBC + CD]
        for j in range(CD // 512):
            cj = slice(j * 512, (j + 1) * 512)
            e = ext_ref[:, cj]
            xc = cb_ref[:, cj] + cw_ref[KC - 1:KC, cj] * e[8:8 + Q]
            for s in range(1, KC):
                xc = xc + cw_ref[KC - 1 - s:KC - s, cj] * pltpu.roll(e, s, 0)[8:8 + Q]
            xc_ref[:, cj] = xc
            xa_ref[:, cj] = _silu(xc)
        ext_ref[0:8, :] = ext_ref[Q:Q + 8, :]

        _, dt, _, cs = _ssd_time(pb_ref, dtb_ref, alog_ref)
        cs_t = cs.T
        dtx_ref[...] = _expand(dt, ex_ref, 2)
        csx_ref[...] = _expand(cs, ex_ref, 3)
        for g in range(NG):
            s_prev = s_ref[g]
            st_ref[0, g] = s_prev
            y, s_new, _ = _ssd_group_fwd(g, xa_ref, s_prev, cs, cs_t, dtx_ref, csx_ref, dsk_ref)
            s_ref[g] = s_new
            cols = slice(g * GRP, (g + 1) * GRP)
            yz = y * _silu(pb_ref[:, cols])
            rr = lax.rsqrt(jnp.mean(yz * yz, axis=-1, keepdims=True) + EPS)
            y_ref[:, cols] = (yz * rr * snw_ref[:, cols]).astype(y_ref.dtype)

    return pl.pallas_call(
        body, name=name, grid=(nc,),
        in_specs=[_row_spec(Q, WB), _full_spec((8, CD)), _full_spec((1, CD)), _full_spec((1, 128)), _full_spec((1, 128)),
                  _full_spec((1, DI)), _full_spec((1, DI)), _full_spec((128, DI))],
        out_specs=[_row_spec(Q, DI), _row_spec(Q, CD), pl.BlockSpec((1, NG, NS, GRP), lambda i: (i, 0, 0, 0))],
        out_shape=[jax.ShapeDtypeStruct((T, DI), _MXU), jax.ShapeDtypeStruct((T, CD), f32),
                   jax.ShapeDtypeStruct((nc, NG, NS, GRP), f32)],
        scratch_shapes=[pltpu.VMEM((NG, NS, GRP), f32), pltpu.VMEM((Q + 8, CD), f32), pltpu.VMEM((Q, CD), f32),
                        pltpu.VMEM((Q, DI), f32), pltpu.VMEM((Q, DI), f32)],
        compiler_params=_params(("arbitrary",)))(pB, cw, cb, dtb, alog, dsk, snw, ex)


def _ssd_bwd(dyb, pB, xc, states, cw, dtb, alog, dsk, snw, name):
    T = pB.shape[0]
    nc = T // Q
    ex, rd = _head_maps()
    selr = (lax.broadcasted_iota(jnp.int32, (HPG * Q, 128), 0) // Q == lax.broadcasted_iota(jnp.int32, (HPG * Q, 128), 1)).astype(_MXU)

    def body(dy_ref, pb_ref, xc_ref, st_ref, cw_ref, dtb_ref, alog_ref, dsk_ref, snw_ref, ex_ref, rd_ref, selr_ref,
             dp_ref, dcw_ref, dcb_ref, dhd_ref, dcol_ref,
             ds_ref, xa_ref, dxa_ref, dxe_ref, dtx_ref, csx_ref, rcs_ref, rdt_ref, rl_ref, dcs_ref, dcst_ref):
        @pl.when(pl.program_id(0) == 0)
        def _():
            ds_ref[...] = jnp.zeros_like(ds_ref)
            dxe_ref[...] = jnp.zeros_like(dxe_ref)
            dcw_ref[...] = jnp.zeros_like(dcw_ref)
            dcb_ref[...] = jnp.zeros_like(dcb_ref)
            dhd_ref[...] = jnp.zeros_like(dhd_ref)
            dcol_ref[...] = jnp.zeros_like(dcol_ref)
            rl_ref[...] = jnp.zeros_like(rl_ref)
            dcst_ref[...] = jnp.zeros_like(dcst_ref)

        dcs_ref[...] = jnp.zeros_like(dcs_ref)
        for j in range(CD // 512):
            cj = slice(j * 512, (j + 1) * 512)
            xa_ref[:, cj] = _silu(xc_ref[:, cj])
        xdt, dt, a, cs = _ssd_time(pb_ref, dtb_ref, alog_ref)
        cs_t = cs.T
        dtx_ref[...] = _expand(dt, ex_ref, 2)
        csx_ref[...] = _expand(cs, ex_ref, 3)

        for g in range(NG):
            cols = slice(g * GRP, (g + 1) * GRP)
            s_prev = st_ref[0, g]
            sb = _bf(s_prev)
            y, _, (xs, bb, cbm, xd, lms, mfs, mcat, xbd, yoff, xdd, csx, csl) = _ssd_group_fwd(
                g, xa_ref, s_prev, cs, cs_t, dtx_ref, csx_ref, dsk_ref)
            z = pb_ref[:, cols]
            sz = _silu(z)
            yz = y * sz
            rr = lax.rsqrt(jnp.mean(yz * yz, axis=-1, keepdims=True) + EPS)
            nrm = yz * rr
            dyb_g = dy_ref[:, cols]
            dcol_ref[1:2, cols] += jnp.sum(dyb_g * nrm, axis=0, keepdims=True)
            dn = dyb_g * snw_ref[:, cols]
            dyz = rr * (dn - nrm * jnp.mean(dn * nrm, axis=-1, keepdims=True))
            dyv = dyz * sz
            dp_ref[:, cols] = (dyz * y * _dsilu(z)).astype(dp_ref.dtype)
            dcol_ref[0:1, cols] += jnp.sum(dyv * xs, axis=0, keepdims=True)
            dy16 = _bf(dyv)
            dmcat = _dot_nt(dy16, xbd)
            dg = dmcat[:, 0:Q] * lms[0]
            for r in range(1, HPG):
                dg = dg + dmcat[:, r * Q:(r + 1) * Q] * lms[r]
            e_hi, e_lo = _split(jnp.concatenate([dmcat[:, r * Q:(r + 1) * Q] * mfs[r] for r in range(HPG)], axis=1), 2)
            rows = _dot(e_hi, selr_ref[...]) + _dot(e_lo, selr_ref[...])
            dcs_ref[...] += rows if g == 0 else pltpu.roll(rows, HPG * g, 1)
            ones8 = jnp.ones((8, Q), _MXU)
            csum = _dot(ones8, e_hi) + _dot(ones8, e_lo)
            for r in range(HPG):
                dcst_ref[HPG * g + r:HPG * g + r + 1, :] = csum[0:1, r * Q:(r + 1) * Q]
            big = _dot_tn(mcat, dy16)
            dxd_diag = jnp.where(_head_mask(0), big[0:Q], 0.0)
            for r in range(1, HPG):
                dxd_diag = dxd_diag + jnp.where(_head_mask(r), big[r * Q:(r + 1) * Q], 0.0)
            dsn = ds_ref[g]
            dsn16 = _bf(dsn)
            dxdd = _dot(bb, dsn16)
            t_state = dxdd * xdd
            dxd = dxd_diag + dxdd * jnp.exp(csl - csx)
            rcs_ref[:, cols] = dyv * yoff - t_state
            rl_ref[0:1, cols] = (jnp.sum(t_state, axis=0, keepdims=True)
                                 + jnp.exp(csl) * jnp.sum(s_prev * dsn, axis=0, keepdims=True))
            rdt_ref[:, cols] = dxd * xs
            dxa_ref[:, cols] = dyv * dsk_ref[:, cols] + dxd * dtx_ref[:, cols]
            dg16 = _bf(dg)
            dw16 = _bf(dyv * jnp.exp(csx))
            dxa_ref[:, DI + NG * NS + g * NS:DI + NG * NS + (g + 1) * NS] = _dot(dg16, bb) + _dot_nt(dw16, sb)
            dxa_ref[:, DI + g * NS:DI + (g + 1) * NS] = _dot_tn(dg16, cbm) + _dot_nt(_bf(xdd), dsn16)
            ds_ref[g] = dsn * jnp.exp(csl) + _dot_tn(cbm, dw16)

        row = lax.broadcasted_iota(jnp.int32, (Q, 128), 0)
        dcs = (dcs_ref[...] - dcst_ref[...].T + _reduce(rcs_ref[...], rd_ref)
               + jnp.where(row == Q - 1, _reduce(rl_ref[...], rd_ref)[0:1, :], 0.0))
        upper = lax.broadcasted_iota(jnp.int32, (Q, Q), 0) <= lax.broadcasted_iota(jnp.int32, (Q, Q), 1)
        dadt = _dot_exact(upper.astype(f32), dcs)
        valid = lax.broadcasted_iota(jnp.int32, (Q, 128), 1) < NH
        ddt = jnp.where(valid, _reduce(rdt_ref[...], rd_ref) + dadt * a, 0.0)
        ddtr = ddt * _sigmoid(xdt)
        dhd_ref[0:1, :] += jnp.sum(ddtr, axis=0, keepdims=True)
        dhd_ref[1:2, :] += jnp.sum(jnp.where(valid, dadt * dt * a, 0.0), axis=0, keepdims=True)
        dp_ref[:, OFF_DT:OFF_DT + 128] = ddtr.astype(dp_ref.dtype)
        dp_ref[:, OFF_DT + 128:WB] = jnp.zeros((Q, WB - OFF_DT - 128), dp_ref.dtype)
        for j in range(CD // 512):
            cj = slice(j * 512, (j + 1) * 512)
            dxc = dxa_ref[:, cj] * _dsilu(xc_ref[:, cj])
            dxe_ref[0:Q, cj] = dxc
            dcb_ref[0:1, cj] += jnp.sum(dxc, axis=0, keepdims=True)
            raw = pb_ref[:, OFF_XBC + j * 512:OFF_XBC + (j + 1) * 512]
            e = dxe_ref[:, cj]
            dcw_ref[KC - 1:KC, cj] += jnp.sum(dxc * raw, axis=0, keepdims=True)
            dxb = cw_ref[KC - 1:KC, cj] * dxc
            for s in range(1, KC):
                sh = pltpu.roll(e, Q + 8 - s, 0)[0:Q]
                dcw_ref[KC - 1 - s:KC - s, cj] += jnp.sum(sh * raw, axis=0, keepdims=True)
                dxb = dxb + cw_ref[KC - 1 - s:KC - s, cj] * sh
            dp_ref[:, OFF_XBC + j * 512:OFF_XBC + (j + 1) * 512] = dxb.astype(dp_ref.dtype)
        dxe_ref[Q:Q + 8, :] = dxe_ref[0:8, :]

    rev = lambda i: (nc - 1 - i, 0)
    return pl.pallas_call(
        body, name=name, grid=(nc,),
        in_specs=[pl.BlockSpec((Q, DI), rev), pl.BlockSpec((Q, WB), rev), pl.BlockSpec((Q, CD), rev),
                  pl.BlockSpec((1, NG, NS, GRP), lambda i: (nc - 1 - i, 0, 0, 0)),
                  _full_spec((8, CD)), _full_spec((1, 128)), _full_spec((1, 128)),
                  _full_spec((1, DI)), _full_spec((1, DI)), _full_spec((128, DI)), _full_spec((DI, 128)),
                  _full_spec((HPG * Q, 128))],
        out_specs=[pl.BlockSpec((Q, WB), rev), _full_spec((8, CD)), _full_spec((8, CD)), _full_spec((8, 128)),
                   _full_spec((8, DI))],
        out_shape=[jax.ShapeDtypeStruct((T, WB), _MXU), jax.ShapeDtypeStruct((8, CD), f32), jax.ShapeDtypeStruct((8, CD), f32),
                   jax.ShapeDtypeStruct((8, 128), f32), jax.ShapeDtypeStruct((8, DI), f32)],
        scratch_shapes=[pltpu.VMEM((NG, NS, GRP), f32), pltpu.VMEM((Q, CD), f32), pltpu.VMEM((Q, CD), f32),
                        pltpu.VMEM((Q + 8, CD), f32), pltpu.VMEM((Q, DI), f32), pltpu.VMEM((Q, DI), f32),
                        pltpu.VMEM((Q, DI), f32), pltpu.VMEM((Q, DI), f32), pltpu.VMEM((8, DI), f32),
                        pltpu.VMEM((Q, 128), f32), pltpu.VMEM((128, Q), f32)],
        compiler_params=_params(("arbitrary",)))(dyb, pB, xc, states, cw, dtb, alog, dsk, snw, ex, rd, selr)


def _merge_fwd(ya, yb, pG, x, gate, wa, wb, wo, name):
    T = x.shape[0]
    ts = _tile(T, 256)

    def body(ya_ref, yb_ref, g_ref, x_ref, gate_ref, wa_ref, wb_ref, wo_ref, xo_ref, mg_ref, pa_ref, pb_ref):
        pa = _dot(ya_ref[...], wa_ref[...])
        pb = _dot(yb_ref[...], wb_ref[...])
        merged = _sigmoid(g_ref[:, 0:D]) * pa + _sigmoid(g_ref[:, D:2 * D]) * pb
        mg = _bf(merged)
        xo_ref[...] = x_ref[...] + gate_ref[...] * _dot(mg, wo_ref[...])
        mg_ref[...] = mg
        pa_ref[...] = pa
        pb_ref[...] = pb

    return pl.pallas_call(
        body, name=name, grid=(T // ts,),
        in_specs=[_row_spec(ts, D), _row_spec(ts, DI), _row_spec(ts, WG), _row_spec(ts, D), _full_spec((1, D)),
                  _full_spec((D, D)), _full_spec((DI, D)), _full_spec((D, D))],
        out_specs=[_row_spec(ts, D)] * 4,
        out_shape=[jax.ShapeDtypeStruct((T, D), f32), jax.ShapeDtypeStruct((T, D), _MXU), jax.ShapeDtypeStruct((T, D), f32),
                   jax.ShapeDtypeStruct((T, D), f32)],
        compiler_params=_params(("arbitrary",)))(ya, yb, pG, x, gate, wa, wb, wo)


def _merge_bwd(dxout, merged, pa, pb, pG, gate, wo, name):
    T = dxout.shape[0]
    ts = _tile(T, 256)

    def body(dx_ref, mg_ref, pa_ref, pb_ref, g_ref, gate_ref, wo_ref, do_ref, dpa_ref, dpb_ref, dg_ref, acc_ref):
        @pl.when(pl.program_id(0) == 0)
        def _():
            acc_ref[...] = jnp.zeros_like(acc_ref)

        dxo = dx_ref[...]
        acc_ref[0:1, :] += jnp.sum(dxo * _dot(mg_ref[...], wo_ref[...]), axis=0, keepdims=True)
        do = _bf(dxo * gate_ref[...])
        do_ref[...] = do
        dmerged = _dot_nt(do, wo_ref[...])
        sa, sb = _sigmoid(g_ref[:, 0:D]), _sigmoid(g_ref[:, D:2 * D])
        dpa_ref[...] = (dmerged * sa).astype(dpa_ref.dtype)
        dpb_ref[...] = (dmerged * sb).astype(dpb_ref.dtype)
        dg_ref[:, 0:D] = (dmerged * pa_ref[...] * sa * (1.0 - sa)).astype(dg_ref.dtype)
        dg_ref[:, D:2 * D] = (dmerged * pb_ref[...] * sb * (1.0 - sb)).astype(dg_ref.dtype)

    return pl.pallas_call(
        body, name=name, grid=(T // ts,),
        in_specs=[_row_spec(ts, D), _row_spec(ts, D), _row_spec(ts, D), _row_spec(ts, D), _row_spec(ts, WG),
                  _full_spec((1, D)), _full_spec((D, D))],
        out_specs=[_row_spec(ts, D), _row_spec(ts, D), _row_spec(ts, D), _row_spec(ts, WG), _full_spec((8, D))],
        out_shape=[jax.ShapeDtypeStruct((T, D), _MXU), jax.ShapeDtypeStruct((T, D), _MXU), jax.ShapeDtypeStruct((T, D), _MXU),
                   jax.ShapeDtypeStruct((T, WG), _MXU), jax.ShapeDtypeStruct((8, D), f32)],
        compiler_params=_params(("arbitrary",)))(dxout, merged, pa, pb, pG, gate, wo)


def _layer_fwd(x, mod, lp, tag):
    h = _modulate(x, lp["nw"], mod[0:1], mod[1:2], f"modulate_{tag}")
    pA = _mm(h, lp["w_gm"], "nn", f"proj_gm_{tag}")
    pB = _mm(h, lp["w_ssd"], "nn", f"proj_ssd_{tag}")
    pG = _mm(h, lp["w_g"], "nn", f"proj_gate_{tag}")
    ya = _gmlp_fwd(pA, lp["lw"], lp["lb"], lp["ws"], lp["bsx"], f"gmlp_fwd_{tag}")
    yb, xc, states = _ssd_fwd(pB, lp["cw"], lp["cb"], lp["dtb"], lp["alog"], lp["dsk"], lp["snw"], f"ssd_fwd_{tag}")
    xo, merged, pa, pb = _merge_fwd(ya, yb, pG, x, mod[2:3], lp["wa"], lp["wb"], lp["wo"], f"merge_fwd_{tag}")
    return xo, dict(x=x, h=h, pA=pA, pB=pB, pG=pG, ya=ya, yb=yb, xc=xc, states=states, merged=merged, pa=pa, pb=pb)


def _layer_bwd(dxo, sv, mod, lp, tag):
    do, dpa, dpb, dpG, s_gate = _merge_bwd(dxo, sv["merged"], sv["pa"], sv["pb"], sv["pG"], mod[2:3], lp["wo"], f"merge_bwd_{tag}")
    g = {}
    g["wo"] = _mm(sv["merged"], do, "tn", f"dw_out_{tag}", out_dtype=_WIRE)
    g["wa"] = _mm(sv["ya"], dpa, "tn", f"dw_proj_a_{tag}", out_dtype=_WIRE)
    g["wb"] = _mm(sv["yb"], dpb, "tn", f"dw_proj_b_{tag}", out_dtype=_WIRE)
    dya = _mm(dpa, lp["wa"], "nt", f"dy_a_{tag}")
    dyb = _mm(dpb, lp["wb"], "nt", f"dy_b_{tag}")
    dpA, s_ln, g["ws"], dbsx = _gmlp_bwd(dya, sv["pA"], lp["lw"], lp["lb"], lp["ws"], lp["bsx"], f"gmlp_bwd_{tag}")
    dpB, g["cw"], s_cb, s_hd, s_col = _ssd_bwd(dyb, sv["pB"], sv["xc"], sv["states"], lp["cw"], lp["dtb"], lp["alog"],
                                               lp["dsk"], lp["snw"], f"ssd_bwd_{tag}")
    dh = _mm(dpG, lp["w_g"], "nt", f"dh_gate_{tag}")
    dh = _mm(dpA, lp["w_gm"], "nt", f"dh_gm_{tag}", acc=dh)
    dh = _mm(dpB, lp["w_ssd"], "nt", f"dh_ssd_{tag}", acc=dh)
    g["w_g"] = _mm(sv["h"], dpG, "tn", f"dw_gate_{tag}", out_dtype=_WIRE)
    g["w_gm"] = _mm(sv["h"], dpA, "tn", f"dw_gm_{tag}", out_dtype=_WIRE)
    g["w_ssd"] = _mm(sv["h"], dpB, "tn", f"dw_ssd_{tag}", out_dtype=_WIRE)
    dx, s_mod = _modulate_bwd(dh, dxo, sv["x"], lp["nw"], mod[1:2], f"modulate_bwd_{tag}")
    g["mod"] = jnp.concatenate([s_mod[0], s_mod[1], s_gate[0]])
    g["nw"] = s_mod[2]
    g["lw"], g["lb"] = s_ln[0], s_ln[1]
    g["bs"] = dbsx.reshape(Q, NG, Q).sum(-1).T
    g["cb"] = s_cb[0]
    g["dtb"], g["alog"] = s_hd[0, :NH], s_hd[1, :NH]
    g["dsk"] = s_col[0].reshape(NH, P).sum(-1)
    g["snw"] = s_col[1]
    g["cw"] = g["cw"][0:KC]
    return dx, g


def _prep_layer(nw, w_in_full, lw, lb, ws, bs, cw_full, cb, dtb, alog, dsk, snw, wa, wb, wo):
    z = jnp.zeros((D, WB - (DI + CD + NH)), w_in_full.dtype)
    pad_h = lambda v: jnp.pad(v, (0, 128 - NH)).reshape(1, 128)
    return dict(
        nw=nw.reshape(1, D),
        w_gm=w_in_full[:, 0:WA],
        w_ssd=jnp.concatenate([w_in_full[:, WA:WA + DI + CD], w_in_full[:, WA + DI + CD:WA + DI + CD + NH], z], axis=1),
        w_g=w_in_full[:, WA + DI + CD + NH:NIN],
        lw=lw.reshape(1, D), lb=lb.reshape(1, D), ws=ws, bsx=jnp.repeat(bs.T, Q, axis=1),
        cw=jnp.pad(cw_full, ((0, 8 - KC), (0, 0))), cb=cb.reshape(1, CD), dtb=pad_h(dtb), alog=pad_h(alog),
        dsk=jnp.repeat(dsk, P).reshape(1, DI), snw=snw.reshape(1, DI), wa=wa, wb=wb, wo=wo)


def _exchange(src, gather, name):
    blk = src.shape if gather else src.shape[1:]
    if not gather:
        assert src.shape[0] == NDEV

    def body(src_ref, out_ref, send_sems, recv_sems, local_sem):
        x, y, c = lax.axis_index("x"), lax.axis_index("y"), lax.axis_index("c")
        me = 4 * x + 2 * y + c

        def peer(j):
            px = 1 - x if (j >> 2) & 1 else x
            py = 1 - y if (j >> 1) & 1 else y
            pc = 1 - c if j & 1 else c
            return (px, py, pc), 4 * px + 2 * py + pc

        def send(j):
            dev, idx = peer(j)
            return pltpu.make_async_remote_copy(
                src_ref=src_ref if gather else src_ref.at[idx], dst_ref=out_ref.at[me],
                send_sem=send_sems.at[j - 1], recv_sem=recv_sems.at[j - 1], device_id=dev, device_id_type=pl.DeviceIdType.MESH)

        def recv(j):
            dev, idx = peer(j)
            return pltpu.make_async_remote_copy(
                src_ref=src_ref if gather else src_ref.at[idx], dst_ref=out_ref.at[idx],
                send_sem=send_sems.at[j - 1], recv_sem=recv_sems.at[j - 1], device_id=dev, device_id_type=pl.DeviceIdType.MESH)

        mine = pltpu.make_async_copy(src_ref if gather else src_ref.at[me], out_ref.at[me], local_sem)
        mine.start()
        sends = [send(j) for j in range(1, NDEV)]
        for cp in sends:
            cp.start()
        for j in range(1, NDEV):
            recv(j).wait_recv()
        for cp in sends:
            cp.wait_send()
        mine.wait()

    return pl.pallas_call(
        body, name=name, out_shape=jax.ShapeDtypeStruct((NDEV,) + tuple(blk), src.dtype),
        in_specs=[pl.BlockSpec(memory_space=pl.ANY)], out_specs=pl.BlockSpec(memory_space=pl.ANY),
        scratch_shapes=[pltpu.SemaphoreType.DMA((NDEV - 1,)), pltpu.SemaphoreType.DMA((NDEV - 1,)), pltpu.SemaphoreType.DMA],
    )(src)


def _mod_fwd(c8, ada_full, ada_b, name):
    L = ada_full.shape[0]

    def body(c_ref, w_ref, b_ref, o_ref, sc_ref):
        sc = _silu(c_ref[...])
        sc_ref[...] = sc
        o_ref[0] = _dot(_bf(sc), w_ref[0]) + b_ref[0]

    return pl.pallas_call(
        body, name=name, grid=(L,),
        in_specs=[_full_spec((8, D)), pl.BlockSpec((1, D, 3 * D), lambda l: (l, 0, 0)), pl.BlockSpec((1, 1, 3 * D), lambda l: (l, 0, 0))],
        out_specs=[pl.BlockSpec((1, 8, 3 * D), lambda l: (l, 0, 0)), _full_spec((8, D))],
        out_shape=[jax.ShapeDtypeStruct((L, 8, 3 * D), f32), jax.ShapeDtypeStruct((8, D), f32)],
        compiler_params=_params(("arbitrary",)))(c8, ada_full, ada_b)


def _ada_w_grad(sc_all, dmod_cols, name):
    W = dmod_cols.shape[1]

    def body(s_ref, d_ref, o_ref):
        o_ref[...] = lax.dot_general(s_ref[...], d_ref[...], (((0,), (0,)), ((), ())), preferred_element_type=f32,
                                     precision=lax.Precision.HIGHEST)

    return pl.pallas_call(body, name=name, out_shape=jax.ShapeDtypeStruct((D, W), f32))(sc_all, dmod_cols)


def _adamw_math(w, g, m, v):
    m = ADAM_B1 * m + (1.0 - ADAM_B1) * g
    v = ADAM_B2 * v + (1.0 - ADAM_B2) * (g * g)
    m_hat = m / (1.0 - ADAM_B1 ** ADAM_STEP)
    v_hat = v / (1.0 - ADAM_B2 ** ADAM_STEP)
    delta = -ADAM_LR * (m_hat / (jnp.sqrt(v_hat) + ADAM_EPS) + ADAM_WD * w)
    return delta, m, v


def _sum_adamw(recv, w, m, v, name):
    n, R, C = recv.shape
    tr = _tile(R, 128 if C > 1024 else 256)

    def body(r_ref, w_ref, m_ref, v_ref, g_ref, d_ref, nm_ref, nv_ref):
        g = r_ref[0].astype(f32)
        for k in range(1, n):
            g = g + r_ref[k].astype(f32)
        g_ref[...] = g
        d_ref[...], nm_ref[...], nv_ref[...] = _adamw_math(w_ref[...], g, m_ref[...], v_ref[...])

    spec = pl.BlockSpec((tr, C), lambda i: (i, 0))
    return pl.pallas_call(
        body, name=name, grid=(R // tr,),
        in_specs=[pl.BlockSpec((n, tr, C), lambda i: (0, i, 0)), spec, spec, spec], out_specs=[spec] * 4,
        out_shape=[jax.ShapeDtypeStruct((R, C), f32)] * 4, compiler_params=_params(("arbitrary",)))(recv, w, m, v)


def _pack(arrays, rows):
    flat = []
    for a in arrays:
        a = a.reshape(-1).astype(f32)
        flat.append(jnp.pad(a, (0, (-a.shape[0]) % 128)))
    flat = jnp.concatenate(flat)
    return jnp.pad(flat, (0, rows * 128 - flat.shape[0])).reshape(rows, 128)


def _unpack(slab, shapes):
    flat = slab.reshape(-1)
    out, off = [], 0
    for s in shapes:
        n = 1
        for d in s:
            n *= d
        out.append(flat[off:off + n].reshape(s))
        off += n + (-n) % 128
    return out


_SMALL = ["ada_b", "norm_w", "gm_ln_w", "gm_ln_b", "gm_ws", "gm_bs", "conv_b", "dt_bias", "a_log", "d_skip", "ssm_norm_w",
          "final_norm_w"]


def kernel(x, c, ada_w, ada_b, norm_w, w_in, gm_ln_w, gm_ln_b, gm_ws, gm_bs, conv_w, conv_b, dt_bias, a_log, d_skip, ssm_norm_w, w_proj_a, w_proj_b, w_out, final_norm_w, loss_target, m_ada_w, m_ada_b, m_norm_w, m_w_in, m_gm_ln_w, m_gm_ln_b, m_gm_ws, m_gm_bs, m_conv_w, m_conv_b, m_dt_bias, m_a_log, m_d_skip, m_ssm_norm_w, m_w_proj_a, m_w_proj_b, m_w_out, m_final_norm_w, v_ada_w, v_ada_b, v_norm_w, v_w_in, v_gm_ln_w, v_gm_ln_b, v_gm_ws, v_gm_bs, v_conv_w, v_conv_b, v_dt_bias, v_a_log, v_d_skip, v_ssm_norm_w, v_w_proj_a, v_w_proj_b, v_w_out, v_final_norm_w):
    L = 2
    me = 4 * lax.axis_index("x") + 2 * lax.axis_index("y") + lax.axis_index("c")
    W = dict(ada_w=ada_w, ada_b=ada_b, norm_w=norm_w, w_in=w_in, gm_ln_w=gm_ln_w, gm_ln_b=gm_ln_b, gm_ws=gm_ws, gm_bs=gm_bs,
             conv_w=conv_w, conv_b=conv_b, dt_bias=dt_bias, a_log=a_log, d_skip=d_skip, ssm_norm_w=ssm_norm_w, w_proj_a=w_proj_a,
             w_proj_b=w_proj_b, w_out=w_out, final_norm_w=final_norm_w)
    M = dict(ada_w=m_ada_w, ada_b=m_ada_b, norm_w=m_norm_w, w_in=m_w_in, gm_ln_w=m_gm_ln_w, gm_ln_b=m_gm_ln_b, gm_ws=m_gm_ws,
             gm_bs=m_gm_bs, conv_w=m_conv_w, conv_b=m_conv_b, dt_bias=m_dt_bias, a_log=m_a_log, d_skip=m_d_skip,
             ssm_norm_w=m_ssm_norm_w, w_proj_a=m_w_proj_a, w_proj_b=m_w_proj_b, w_out=m_w_out, final_norm_w=m_final_norm_w)
    V = dict(ada_w=v_ada_w, ada_b=v_ada_b, norm_w=v_norm_w, w_in=v_w_in, gm_ln_w=v_gm_ln_w, gm_ln_b=v_gm_ln_b, gm_ws=v_gm_ws,
             gm_bs=v_gm_bs, conv_w=v_conv_w, conv_b=v_conv_b, dt_bias=v_dt_bias, a_log=v_a_log, d_skip=v_d_skip,
             ssm_norm_w=v_ssm_norm_w, w_proj_a=v_w_proj_a, w_proj_b=v_w_proj_b, w_out=v_w_out, final_norm_w=v_final_norm_w)
    SW = NIN // NDEV
    AW = 3 * D // NDEV
    CW = CD // NDEV
    RA, RB = D // NDEV, DI // NDEV

    g_win = _exchange(w_in.astype(_WIRE).reshape(L * D, SW), True, "gather_w_in").reshape(NDEV, L, D, SW)
    g_ada = _exchange(ada_w.astype(_WIRE).reshape(L * D, AW), True, "gather_ada_w").reshape(NDEV, L, D, AW)
    rows = jnp.concatenate([w_proj_a.reshape(L * RA, D), w_proj_b.reshape(L * RB, D), w_out.reshape(L * RA, D)], axis=0)
    g_rows = _exchange(rows.astype(_WIRE), True, "gather_w_rows")
    g_pa = g_rows[:, 0:L * RA].reshape(NDEV, L, RA, D)
    g_pb = g_rows[:, L * RA:L * (RA + RB)].reshape(NDEV, L, RB, D)
    g_po = g_rows[:, L * (RA + RB):].reshape(NDEV, L, RA, D)
    g_cw = _exchange(conv_w.reshape(L * KC, CW), True, "gather_conv_w").reshape(NDEV, L, KC, CW)
    cols = lambda g, l: jnp.transpose(g[:, l], (1, 0, 2)).reshape(g.shape[2], -1)
    ada_full = jnp.stack([cols(g_ada, l) for l in range(L)])

    mod8, sc8 = _mod_fwd(jnp.broadcast_to(c, (8, D)), ada_full, ada_b.reshape(L, 1, 3 * D), "mod_fwd")
    lps, mods = [], []
    for l in range(L):
        lps.append(_prep_layer(norm_w[l], cols(g_win, l), gm_ln_w[l], gm_ln_b[l], gm_ws[l], gm_bs[l], cols(g_cw, l), conv_b[l],
                               dt_bias[l], a_log[l], d_skip[l], ssm_norm_w[l], g_pa[:, l].reshape(D, D), g_pb[:, l].reshape(DI, D),
                               g_po[:, l].reshape(D, D)))
        mods.append(mod8[l, 0].reshape(3, D))

    h = x[0]
    saved = []
    for l in range(L):
        h, sv = _layer_fwd(h, mods[l], lps[l], f"l{l}")
        saved.append(sv)
    dh, s_fin = _final_loss(h, loss_target[0], final_norm_w.reshape(1, D), "final_loss")
    loss = lax.psum(0.5 / D * s_fin[1, 0], ("x", "y", "c"))
    grads = [None] * L
    for l in reversed(range(L)):
        dh, grads[l] = _layer_bwd(dh, saved[l], mods[l], lps[l], f"l{l}")
    grad_x = dh[None]

    def win_blocks(g):
        full = jnp.concatenate([g["w_gm"], g["w_ssd"][:, 0:DI + CD], g["w_ssd"][:, DI + CD:DI + CD + NH], g["w_g"]], axis=1)
        return jnp.transpose(full.reshape(D, NDEV, SW), (1, 0, 2))
    send_win = jnp.stack([win_blocks(g) for g in grads], axis=1).reshape(NDEV, L * D, SW)
    recv_win = _exchange(send_win, False, "scatter_w_in")
    o_win = _sum_adamw(recv_win, w_in.reshape(L * D, SW), m_w_in.reshape(L * D, SW), v_w_in.reshape(L * D, SW), "adamw_w_in")
    o_win = [o.reshape(L, D, SW) for o in o_win]

    rb = lambda key, r: jnp.stack([g[key].reshape(NDEV, r, D) for g in grads], axis=1).reshape(NDEV, L * r, D)
    send_rows = jnp.concatenate([rb("wa", RA), rb("wb", RB), rb("wo", RA)], axis=1)
    recv_rows = _exchange(send_rows, False, "scatter_w_rows")
    stack_rows = lambda d: jnp.concatenate([d["w_proj_a"].reshape(L * RA, D), d["w_proj_b"].reshape(L * RB, D),
                                            d["w_out"].reshape(L * RA, D)], axis=0)
    o_rows = _sum_adamw(recv_rows, stack_rows(W), stack_rows(M), stack_rows(V), "adamw_w_rows")
    o_pa = [o[0:L * RA].reshape(L, RA, D) for o in o_rows]
    o_pb = [o[L * RA:L * (RA + RB)].reshape(L, RB, D) for o in o_rows]
    o_po = [o[L * (RA + RB):].reshape(L, RA, D) for o in o_rows]

    st = lambda key: jnp.stack([g[key] for g in grads])
    small_g = dict(ada_b=st("mod"), norm_w=st("nw"), gm_ln_w=st("lw"), gm_ln_b=st("lb"), gm_ws=st("ws"), gm_bs=st("bs"),
                   conv_b=st("cb"), dt_bias=st("dtb"), a_log=st("alog"), d_skip=st("dsk"), ssm_norm_w=st("snw"),
                   final_norm_w=s_fin[0])
    extra = [st("cw"), sc8[0]]
    shapes = [W[n].shape for n in _SMALL] + [(L, KC, CD), (D,)]
    n_rows = sum(-(-functools.reduce(lambda a, b: a * b, s, 1) // 128) for s in shapes)
    n_rows = -(-n_rows // 256) * 256
    zeros_extra = [jnp.zeros((L, KC, CD), f32), jnp.zeros((D,), f32)]
    slab_g = _pack([small_g[n] for n in _SMALL] + extra, n_rows)
    slab_all = _exchange(slab_g, True, "gather_small")
    o_small = _sum_adamw(slab_all, _pack([W[n] for n in _SMALL] + zeros_extra, n_rows),
                         _pack([M[n] for n in _SMALL] + zeros_extra, n_rows),
                         _pack([V[n] for n in _SMALL] + zeros_extra, n_rows), "adamw_small")
    o_small = [_unpack(o, shapes) for o in o_small]
    small_out = {n: [o_small[k][i] for k in range(4)] for i, n in enumerate(_SMALL)}

    g_cw_mine = lax.dynamic_slice_in_dim(o_small[0][len(_SMALL)], me * CW, CW, axis=2)
    o_cw = _sum_adamw(g_cw_mine.reshape(1, L * KC, CW), conv_w.reshape(L * KC, CW), m_conv_w.reshape(L * KC, CW),
                      v_conv_w.reshape(L * KC, CW), "adamw_conv_w")
    o_cw = [o.reshape(L, KC, CW) for o in o_cw]

    per_dev = [_unpack(slab_all[k], shapes) for k in range(NDEV)]
    sc_all = jnp.stack([p[len(_SMALL) + 1] for p in per_dev])
    dmod_all = jnp.stack([p[0] for p in per_dev])
    dmod_cols = lax.dynamic_slice_in_dim(dmod_all, me * AW, AW, axis=2).reshape(NDEV, L * AW)
    g_ada_w = jnp.transpose(_ada_w_grad(sc_all, dmod_cols, "ada_w_grad").reshape(D, L, AW), (1, 0, 2))
    o_ada = _sum_adamw(g_ada_w.reshape(1, L * D, AW), ada_w.reshape(L * D, AW), m_ada_w.reshape(L * D, AW),
                       v_ada_w.reshape(L * D, AW), "adamw_ada_w")
    o_ada = [o.reshape(L, D, AW) for o in o_ada]

    big = dict(ada_w=o_ada, w_in=o_win, conv_w=o_cw, w_proj_a=o_pa, w_proj_b=o_pb, w_out=o_po)
    order = ["ada_w", "ada_b", "norm_w", "w_in", "gm_ln_w", "gm_ln_b", "gm_ws", "gm_bs", "conv_w", "conv_b", "dt_bias", "a_log",
             "d_skip", "ssm_norm_w", "w_proj_a", "w_proj_b", "w_out", "final_norm_w"]
    pick = lambda n, k: big[n][k] if n in big else small_out[n][k]
    return (loss, grad_x, *[pick(n, 0) for n in order], *[pick(n, 1) for n in order], *[pick(n, 2) for n in order],
            *[pick(n, 3) for n in order])
```

```python
import functools

import jax
import jax.numpy as jnp
from jax import lax
from jax.experimental import pallas as pl
from jax.experimental.pallas import tpu as pltpu

f32 = jnp.float32
_MXU = jnp.bfloat16
_WIRE = jnp.bfloat16

D = 1024
Q = 128
NG = 8
DI = 2048
NH = 32
P = 64
HPG = 4
NS = 128
KC = 4
CD = 4096
GRP = DI // NG
EPS = 1e-6
NDEV = 8
WA = 3 * D
WB = DI + CD + 256
WG = 2 * D
OFF_XBC = DI
OFF_DT = DI + CD
NIN = 11296
VMEM_LIMIT = 56 * 1024 * 1024
NEG = -1e30

ADAM_LR, ADAM_B1, ADAM_B2, ADAM_EPS, ADAM_WD, ADAM_STEP = 0.001, 0.9, 0.999, 1e-08, 0.01, 10


def _bf(x):
    return x.astype(_MXU)


def _dot(a, b):
    return jnp.dot(a, b, preferred_element_type=f32)


def _dot_nt(a, b):
    return lax.dot_general(a, b, (((1,), (1,)), ((), ())), preferred_element_type=f32)


def _dot_tn(a, b):
    return lax.dot_general(a, b, (((0,), (0,)), ((), ())), preferred_element_type=f32)


def _dot_exact(a, b):
    return jnp.dot(a, b, preferred_element_type=f32, precision=lax.Precision.HIGHEST)


def _sigmoid(x):
    return jax.nn.sigmoid(x)


def _silu(x):
    return x * _sigmoid(x)


def _dsilu(x):
    s = _sigmoid(x)
    return s * (1.0 + x * (1.0 - s))


_GK = 0.7978845608028654
_GC = 0.044715


def _gelu(x):
    return 0.5 * x * (1.0 + jnp.tanh(_GK * (x + _GC * x * x * x)))


def _dgelu(x):
    t = jnp.tanh(_GK * (x + _GC * x * x * x))
    return 0.5 * (1.0 + t) + 0.5 * x * (1.0 - t * t) * _GK * (1.0 + 3.0 * _GC * x * x)


def _softplus(x):
    return jnp.maximum(x, 0.0) + jnp.log1p(jnp.exp(-jnp.abs(x)))


def _tile(n, cap):
    if n <= cap:
        return n
    best = None
    for t in range(128, cap + 1, 128):
        if n % t == 0:
            best = t
    assert best is not None, (n, cap)
    return best


def _params(sem):
    return pltpu.CompilerParams(dimension_semantics=sem, vmem_limit_bytes=VMEM_LIMIT)


def _exch_ops(src_ref, out_ref, send_sems, recv_sems, local_sem, gather):
    x, y, c = lax.axis_index("x"), lax.axis_index("y"), lax.axis_index("c")
    me = 4 * x + 2 * y + c

    def peer(j):
        px = 1 - x if (j >> 2) & 1 else x
        py = 1 - y if (j >> 1) & 1 else y
        pc = 1 - c if j & 1 else c
        return (px, py, pc), 4 * px + 2 * py + pc

    def copy(j, landing):
        dev, idx = peer(j)
        return pltpu.make_async_remote_copy(
            src_ref=src_ref if gather else src_ref.at[idx], dst_ref=out_ref.at[idx] if landing else out_ref.at[me],
            send_sem=send_sems.at[j - 1], recv_sem=recv_sems.at[j - 1], device_id=dev, device_id_type=pl.DeviceIdType.MESH)

    mine = pltpu.make_async_copy(src_ref if gather else src_ref.at[me], out_ref.at[me], local_sem)

    def start():
        mine.start()
        for j in range(1, NDEV):
            copy(j, False).start()

    def finish():
        for j in range(1, NDEV):
            copy(j, True).wait_recv()
        for j in range(1, NDEV):
            copy(j, False).wait_send()
        mine.wait()

    return start, finish


def _exch_shape(src, gather):
    return jax.ShapeDtypeStruct((NDEV,) + tuple(src.shape if gather else src.shape[1:]), src.dtype)


_EXCH_SEMS = [pltpu.SemaphoreType.DMA((NDEV - 1,)), pltpu.SemaphoreType.DMA((NDEV - 1,)), pltpu.SemaphoreType.DMA]


def _carry_call(body, name, grid, in_specs, out_specs, out_shape, scratch_shapes, args, carry=()):
    n_in, n_out, n_sc, nx = len(in_specs), len(out_specs), len(scratch_shapes), len(carry)
    sem = ("arbitrary",) * len(grid)
    if nx == 0:
        outs = pl.pallas_call(body, name=name, grid=grid, in_specs=in_specs, out_specs=out_specs, out_shape=out_shape,
                              scratch_shapes=scratch_shapes, compiler_params=_params(sem))(*args)
        return list(outs), []

    def wrapped(*refs):
        ins, srcs = refs[:n_in], refs[n_in:n_in + nx]
        outs, dsts = refs[n_in + nx:n_in + nx + n_out], refs[n_in + nx + n_out:n_in + 2 * nx + n_out]
        scratch, sems = refs[n_in + 2 * nx + n_out:n_in + 2 * nx + n_out + n_sc], refs[n_in + 2 * nx + n_out + n_sc:]
        ops = [_exch_ops(srcs[i], dsts[i], sems[3 * i], sems[3 * i + 1], sems[3 * i + 2], carry[i][1]) for i in range(nx)]
        first = functools.reduce(jnp.logical_and, [pl.program_id(d) == 0 for d in range(len(grid))])
        last = functools.reduce(jnp.logical_and, [pl.program_id(d) == grid[d] - 1 for d in range(len(grid))])

        @pl.when(first)
        def _():
            for start, _ in ops:
                start()

        body(*ins, *outs, *scratch)

        @pl.when(last)
        def _():
            for _, finish in ops:
                finish()

    hbm = pl.BlockSpec(memory_space=pl.ANY)
    outs = pl.pallas_call(
        wrapped, name=name, grid=grid, in_specs=list(in_specs) + [hbm] * nx, out_specs=list(out_specs) + [hbm] * nx,
        out_shape=list(out_shape) + [_exch_shape(s, g) for s, g in carry], scratch_shapes=list(scratch_shapes) + _EXCH_SEMS * nx,
        compiler_params=_params(sem))(*args, *[s for s, _ in carry])
    return list(outs[:n_out]), list(outs[n_out:])


def _mm(a, b, mode, name, acc=None, out_dtype=f32, tm_cap=512, tn_cap=1280, tk_cap=1280, carry=()):
    if mode == "nn":
        (M, K), (K2, N) = a.shape, b.shape
    elif mode == "nt":
        (M, K), (N, K2) = a.shape, b.shape
    else:
        (K, M), (K2, N) = a.shape, b.shape
        tk_cap = min(tk_cap, 512)
    assert K == K2, (a.shape, b.shape, mode)
    tm, tn, tk = _tile(M, tm_cap), _tile(N, tn_cap), _tile(K, tk_cap)
    nk = K // tk
    if mode == "nn":
        a_spec = pl.BlockSpec((tm, tk), lambda i, j, k: (i, k))
        b_spec = pl.BlockSpec((tk, tn), lambda i, j, k: (k, j))
        dot = _dot
    elif mode == "nt":
        a_spec = pl.BlockSpec((tm, tk), lambda i, j, k: (i, k))
        b_spec = pl.BlockSpec((tn, tk), lambda i, j, k: (j, k))
        dot = _dot_nt
    else:
        a_spec = pl.BlockSpec((tk, tm), lambda i, j, k: (k, i))
        b_spec = pl.BlockSpec((tk, tn), lambda i, j, k: (k, j))
        dot = _dot_tn
    o_spec = pl.BlockSpec((tm, tn), lambda i, j, k: (i, j))
    has_acc = acc is not None

    def body(*refs):
        if has_acc:
            a_ref, b_ref, c_ref, o_ref, acc_ref = refs
        else:
            a_ref, b_ref, o_ref, acc_ref = refs
        k = pl.program_id(2)

        @pl.when(k == 0)
        def _():
            if has_acc:
                acc_ref[...] = c_ref[...]
            else:
                acc_ref[...] = jnp.zeros_like(acc_ref)

        acc_ref[...] += dot(_bf(a_ref[...]), _bf(b_ref[...]))

        @pl.when(k == nk - 1)
        def _():
            o_ref[...] = acc_ref[...].astype(out_dtype)

    in_specs = [a_spec, b_spec] + ([o_spec] if has_acc else [])
    args = (a, b) + ((acc,) if has_acc else ())
    (out,), exchanged = _carry_call(body, name, (M // tm, N // tn, nk), in_specs, [o_spec], [jax.ShapeDtypeStruct((M, N), out_dtype)],
                                    [pltpu.VMEM((tm, tn), f32)], args, carry)
    return (out, exchanged) if carry else out


def _row_spec(ts, w, col=0):
    return pl.BlockSpec((ts, w), lambda i: (i, col))


def _full_spec(shape):
    nd = len(shape)
    return pl.BlockSpec(shape, lambda i: (0,) * nd)


def _modulate(x, nw, shift, scale, name):
    T = x.shape[0]
    ts = _tile(T, 512)

    def body(x_ref, nw_ref, sh_ref, sc_ref, h_ref):
        xv = x_ref[...]
        r = lax.rsqrt(jnp.mean(xv * xv, axis=-1, keepdims=True) + EPS)
        h_ref[...] = ((xv * r * nw_ref[...]) * (1.0 + sc_ref[...]) + sh_ref[...]).astype(h_ref.dtype)

    return pl.pallas_call(
        body, name=name, grid=(T // ts,),
        in_specs=[_row_spec(ts, D), _full_spec((1, D)), _full_spec((1, D)), _full_spec((1, D))],
        out_specs=_row_spec(ts, D), out_shape=jax.ShapeDtypeStruct((T, D), _MXU),
        compiler_params=_params(("arbitrary",)))(x, nw, shift, scale)


def _modulate_bwd(dh, dxout, x, nw, scale, name):
    T = x.shape[0]
    ts = _tile(T, 512)

    def body(dh_ref, dxo_ref, x_ref, nw_ref, sc_ref, dx_ref, acc_ref):
        @pl.when(pl.program_id(0) == 0)
        def _():
            acc_ref[...] = jnp.zeros_like(acc_ref)

        xv, dh_v = x_ref[...], dh_ref[...]
        r = lax.rsqrt(jnp.mean(xv * xv, axis=-1, keepdims=True) + EPS)
        xn = xv * r
        hn = xn * nw_ref[...]
        dhn = dh_v * (1.0 + sc_ref[...])
        acc_ref[0:1, :] += jnp.sum(dh_v, axis=0, keepdims=True)
        acc_ref[1:2, :] += jnp.sum(dh_v * hn, axis=0, keepdims=True)
        acc_ref[2:3, :] += jnp.sum(dhn * xn, axis=0, keepdims=True)
        dxn = dhn * nw_ref[...]
        dx_ref[...] = dxo_ref[...] + r * (dxn - xn * jnp.mean(dxn * xn, axis=-1, keepdims=True))

    return pl.pallas_call(
        body, name=name, grid=(T // ts,),
        in_specs=[_row_spec(ts, D), _row_spec(ts, D), _row_spec(ts, D), _full_spec((1, D)), _full_spec((1, D))],
        out_specs=[_row_spec(ts, D), _full_spec((8, D))],
        out_shape=[jax.ShapeDtypeStruct((T, D), f32), jax.ShapeDtypeStruct((8, D), f32)],
        compiler_params=_params(("arbitrary",)))(dh, dxout, x, nw, scale)


def _final_loss(x, tgt, fw, name):
    T = x.shape[0]
    ts = _tile(T, 512)

    def body(x_ref, t_ref, fw_ref, dx_ref, acc_ref):
        @pl.when(pl.program_id(0) == 0)
        def _():
            acc_ref[...] = jnp.zeros_like(acc_ref)

        xv = x_ref[...]
        r = lax.rsqrt(jnp.mean(xv * xv, axis=-1, keepdims=True) + EPS)
        xn = xv * r
        e = xn * fw_ref[...] - t_ref[...]
        dy = e * (1.0 / D)
        acc_ref[0:1, :] += jnp.sum(dy * xn, axis=0, keepdims=True)
        acc_ref[1:2, :] += jnp.sum(jnp.sum(e * e, axis=0, keepdims=True), axis=1, keepdims=True)
        dxn = dy * fw_ref[...]
        dx_ref[...] = r * (dxn - xn * jnp.mean(dxn * xn, axis=-1, keepdims=True))

    return pl.pallas_call(
        body, name=name, grid=(T // ts,),
        in_specs=[_row_spec(ts, D), _row_spec(ts, D), _full_spec((1, D))],
        out_specs=[_row_spec(ts, D), _full_spec((8, D))],
        out_shape=[jax.ShapeDtypeStruct((T, D), f32), jax.ShapeDtypeStruct((8, D), f32)],
        compiler_params=_params(("arbitrary",)))(x, tgt, fw)


def _tril(n):
    return lax.broadcasted_iota(jnp.int32, (n, n), 0) >= lax.broadcasted_iota(jnp.int32, (n, n), 1)


def _gm_chunk_fwd(u, v, z, lw, lb, ws_ref, bsx):
    gu, gv = _gelu(u), _gelu(v)
    mu = jnp.mean(gv, axis=-1, keepdims=True)
    cen = gv - mu
    rstd = lax.rsqrt(jnp.mean(cen * cen, axis=-1, keepdims=True) + EPS)
    vhat = cen * rstd
    vn = _bf(vhat * lw + lb)
    tri = _tril(Q)
    mixed = jnp.concatenate(
        [_dot(_bf(jnp.where(tri, ws_ref[g], 0.0)), vn[:, g * Q:(g + 1) * Q]) for g in range(NG)], axis=1) + bsx
    return gu, vhat, rstd, vn, mixed


def _gmlp_fwd(pA, lw, lb, ws, bsx, name):
    T = pA.shape[0]
    ts = _tile(T, 512)

    def body(u_ref, v_ref, z_ref, lw_ref, lb_ref, ws_ref, bsx_ref, y_ref):
        def chunk(ci, carry):
            rows = pl.ds(pl.multiple_of(ci * Q, Q), Q)
            gu, _, _, _, mixed = _gm_chunk_fwd(u_ref[rows, :], v_ref[rows, :], z_ref[rows, :], lw_ref[...], lb_ref[...],
                                               ws_ref, bsx_ref[...])
            y_ref[rows, :] = (gu * mixed * _silu(z_ref[rows, :])).astype(y_ref.dtype)
            return carry

        lax.fori_loop(0, ts // Q, chunk, 0)

    return pl.pallas_call(
        body, name=name, grid=(T // ts,),
        in_specs=[_row_spec(ts, D, 0), _row_spec(ts, D, 1), _row_spec(ts, D, 2), _full_spec((1, D)), _full_spec((1, D)),
                  _full_spec((NG, Q, Q)), _full_spec((Q, D))],
        out_specs=_row_spec(ts, D), out_shape=jax.ShapeDtypeStruct((T, D), _MXU),
        compiler_params=_params(("arbitrary",)))(pA, pA, pA, lw, lb, ws, bsx)


def _gmlp_bwd(dya, pA, lw, lb, ws, bsx, name):
    T = pA.shape[0]
    ts = _tile(T, 512)

    def body(dy_ref, u_ref, v_ref, z_ref, lw_ref, lb_ref, ws_ref, bsx_ref, dp_ref, acc_ref, dws_ref, dbs_ref):
        @pl.when(pl.program_id(0) == 0)
        def _():
            acc_ref[...] = jnp.zeros_like(acc_ref)
            dws_ref[...] = jnp.zeros_like(dws_ref)
            dbs_ref[...] = jnp.zeros_like(dbs_ref)

        tri = _tril(Q)

        def chunk(ci, carry):
            rows = pl.ds(pl.multiple_of(ci * Q, Q), Q)
            u, v, z, dy = u_ref[rows, :], v_ref[rows, :], z_ref[rows, :], dy_ref[rows, :]
            gu, vhat, rstd, vn, mixed = _gm_chunk_fwd(u, v, z, lw_ref[...], lb_ref[...], ws_ref, bsx_ref[...])
            sz = _silu(z)
            dp_ref[rows, 0:D] = (dy * mixed * sz * _dgelu(u)).astype(dp_ref.dtype)
            dp_ref[rows, 2 * D:3 * D] = (dy * gu * mixed * _dsilu(z)).astype(dp_ref.dtype)
            dmixed = dy * gu * sz
            dbs_ref[...] += dmixed
            dmb = _bf(dmixed)
            dvn_parts = []
            for g in range(NG):
                cols = slice(g * Q, (g + 1) * Q)
                wg = _bf(jnp.where(tri, ws_ref[g], 0.0))
                dvn_parts.append(_dot_tn(wg, dmb[:, cols]))
                dws_ref[g] += jnp.where(tri, _dot_nt(dmb[:, cols], vn[:, cols]), 0.0)
            dvn = jnp.concatenate(dvn_parts, axis=1)
            acc_ref[0:1, :] += jnp.sum(dvn * vhat, axis=0, keepdims=True)
            acc_ref[1:2, :] += jnp.sum(dvn, axis=0, keepdims=True)
            dvh = dvn * lw_ref[...]
            dgv = rstd * (dvh - jnp.mean(dvh, axis=-1, keepdims=True) - vhat * jnp.mean(dvh * vhat, axis=-1, keepdims=True))
            dp_ref[rows, D:2 * D] = (dgv * _dgelu(v)).astype(dp_ref.dtype)
            return carry

        lax.fori_loop(0, ts // Q, chunk, 0)

    return pl.pallas_call(
        body, name=name, grid=(T // ts,),
        in_specs=[_row_spec(ts, D), _row_spec(ts, D, 0), _row_spec(ts, D, 1), _row_spec(ts, D, 2), _full_spec((1, D)),
                  _full_spec((1, D)), _full_spec((NG, Q, Q)), _full_spec((Q, D))],
        out_specs=[_row_spec(ts, WA), _full_spec((8, D)), _full_spec((NG, Q, Q)), _full_spec((Q, D))],
        out_shape=[jax.ShapeDtypeStruct((T, WA), _MXU), jax.ShapeDtypeStruct((8, D), f32),
                   jax.ShapeDtypeStruct((NG, Q, Q), f32), jax.ShapeDtypeStruct((Q, D), f32)],
        compiler_params=_params(("arbitrary",)))(dya, pA, pA, pA, lw, lb, ws, bsx)


def _head_maps():
    h = lax.broadcasted_iota(jnp.int32, (128, DI), 0)
    ch = lax.broadcasted_iota(jnp.int32, (128, DI), 1)
    ex = (ch // P == h).astype(_MXU)
    return ex, ex.T


def _split(v, parts):
    out = []
    for _ in range(parts - 1):
        p = _bf(v)
        out.append(p)
        v = v - p.astype(f32)
    out.append(_bf(v))
    return out


def _expand(v, ex_ref, parts):
    acc = None
    for p in _split(v, parts):
        t = _dot(p, ex_ref[...])
        acc = t if acc is None else acc + t
    return acc


def _reduce(v, rd_ref):
    hi, lo = _split(v, 2)
    return _dot(hi, rd_ref[...]) + _dot(lo, rd_ref[...])


def _ssd_time(pb_ref, dtb_ref, alog_ref):
    xdt = pb_ref[:, OFF_DT:OFF_DT + 128] + dtb_ref[...]
    dt = _softplus(xdt)
    a = -jnp.exp(alog_ref[...])
    cs = _dot_exact(_tril(Q).astype(f32), dt * a)
    return xdt, dt, a, cs


def _head_mask(r):
    return lax.broadcasted_iota(jnp.int32, (Q, GRP), 1) // P == r


def _ssd_group_fwd(g, xa_ref, s_prev, cs, cs_t, dtx_ref, csx_ref, dsk_ref):
    cols = slice(g * GRP, (g + 1) * GRP)
    xs = xa_ref[:, cols]
    bb = _bf(xa_ref[:, DI + g * NS:DI + (g + 1) * NS])
    cb = _bf(xa_ref[:, DI + NG * NS + g * NS:DI + NG * NS + (g + 1) * NS])
    gm = _dot_nt(cb, bb)
    xd = xs * dtx_ref[:, cols]
    csx = csx_ref[:, cols]
    csl = csx_ref[Q - 1:Q, cols]
    tri = _tril(Q)
    lms = [jnp.exp(jnp.where(tri, cs[:, HPG * g + r:HPG * g + r + 1] - cs_t[HPG * g + r:HPG * g + r + 1, :], NEG))
           for r in range(HPG)]
    mfs = [gm * lm for lm in lms]
    mcat = jnp.concatenate([_bf(m) for m in mfs], axis=1)
    xbd = jnp.concatenate([_bf(jnp.where(_head_mask(r), xd, 0.0)) for r in range(HPG)], axis=0)
    ydiag = _dot(mcat, xbd)
    yoff = jnp.exp(csx) * _dot(cb, _bf(s_prev))
    y = ydiag + yoff + xs * dsk_ref[:, cols]
    xdd = xd * jnp.exp(csl - csx)
    s_new = s_prev * jnp.exp(csl) + _dot_tn(bb, _bf(xdd))
    return y, s_new, (xs, bb, cb, xd, lms, mfs, mcat, xbd, yoff, xdd, csx, csl)


def _ssd_fwd(pB, cw, cb, dtb, alog, dsk, snw, name, carry=()):
    T = pB.shape[0]
    nc = T // Q
    ex, _ = _head_maps()

    def body(pb_ref, cw_ref, cb_ref, dtb_ref, alog_ref, dsk_ref, snw_ref, ex_ref, y_ref, xc_ref, st_ref,
             s_ref, ext_ref, xa_ref, dtx_ref, csx_ref):
        @pl.when(pl.program_id(0) == 0)
        def _():
            s_ref[...] = jnp.zeros_like(s_ref)
            ext_ref[...] = jnp.zeros_like(ext_ref)

        ext_ref[8:8 + Q, :] = pb_ref[:, OFF_XBC:OFF_XBC + CD]
        for j in range(CD // 512):
            cj = slice(j * 512, (j + 1) * 512)
            e = ext_ref[:, cj]
            xc = cb_ref[:, cj] + cw_ref[KC - 1:KC, cj] * e[8:8 + Q]
            for s in range(1, KC):
                xc = xc + cw_ref[KC - 1 - s:KC - s, cj] * pltpu.roll(e, s, 0)[8:8 + Q]
            xc_ref[:, cj] = xc
            xa_ref[:, cj] = _silu(xc)
        ext_ref[0:8, :] = ext_ref[Q:Q + 8, :]

        _, dt, _, cs = _ssd_time(pb_ref, dtb_ref, alog_ref)
        cs_t = cs.T
        dtx_ref[...] = _expand(dt, ex_ref, 2)
        csx_ref[...] = _expand(cs, ex_ref, 3)
        for g in range(NG):
            s_prev = s_ref[g]
            st_ref[0, g] = s_prev
            y, s_new, _ = _ssd_group_fwd(g, xa_ref, s_prev, cs, cs_t, dtx_ref, csx_ref, dsk_ref)
            s_ref[g] = s_new
            cols = slice(g * GRP, (g + 1) * GRP)
            yz = y * _silu(pb_ref[:, cols])
            rr = lax.rsqrt(jnp.mean(yz * yz, axis=-1, keepdims=True) + EPS)
            y_ref[:, cols] = (yz * rr * snw_ref[:, cols]).astype(y_ref.dtype)

    outs, exchanged = _carry_call(
        body, name, (nc,),
        [_row_spec(Q, WB), _full_spec((8, CD)), _full_spec((1, CD)), _full_spec((1, 128)), _full_spec((1, 128)),
         _full_spec((1, DI)), _full_spec((1, DI)), _full_spec((128, DI))],
        [_row_spec(Q, DI), _row_spec(Q, CD), pl.BlockSpec((1, NG, NS, GRP), lambda i: (i, 0, 0, 0))],
        [jax.ShapeDtypeStruct((T, DI), _MXU), jax.ShapeDtypeStruct((T, CD), f32), jax.ShapeDtypeStruct((nc, NG, NS, GRP), f32)],
        [pltpu.VMEM((NG, NS, GRP), f32), pltpu.VMEM((Q + 8, CD), f32), pltpu.VMEM((Q, CD), f32),
         pltpu.VMEM((Q, DI), f32), pltpu.VMEM((Q, DI), f32)],
        (pB, cw, cb, dtb, alog, dsk, snw, ex), carry)
    return outs + [exchanged]


def _ssd_bwd(dyb, pB, xc, states, cw, dtb, alog, dsk, snw, name, carry=()):
    T = pB.shape[0]
    nc = T // Q
    ex, rd = _head_maps()
    selr = (lax.broadcasted_iota(jnp.int32, (HPG * Q, 128), 0) // Q == lax.broadcasted_iota(jnp.int32, (HPG * Q, 128), 1)).astype(_MXU)

    def body(dy_ref, pb_ref, xc_ref, st_ref, cw_ref, dtb_ref, alog_ref, dsk_ref, snw_ref, ex_ref, rd_ref, selr_ref,
             dp_ref, dcw_ref, dcb_ref, dhd_ref, dcol_ref,
             ds_ref, xa_ref, dxa_ref, dxe_ref, dtx_ref, csx_ref, rcs_ref, rdt_ref, rl_ref, dcs_ref, dcst_ref):
        @pl.when(pl.program_id(0) == 0)
        def _():
            ds_ref[...] = jnp.zeros_like(ds_ref)
            dxe_ref[...] = jnp.zeros_like(dxe_ref)
            dcw_ref[...] = jnp.zeros_like(dcw_ref)
            dcb_ref[...] = jnp.zeros_like(dcb_ref)
            dhd_ref[...] = jnp.zeros_like(dhd_ref)
            dcol_ref[...] = jnp.zeros_like(dcol_ref)
            rl_ref[...] = jnp.zeros_like(rl_ref)
            dcst_ref[...] = jnp.zeros_like(dcst_ref)

        dcs_ref[...] = jnp.zeros_like(dcs_ref)
        for j in range(CD // 512):
            cj = slice(j * 512, (j + 1) * 512)
            xa_ref[:, cj] = _silu(xc_ref[:, cj])
        xdt, dt, a, cs = _ssd_time(pb_ref, dtb_ref, alog_ref)
        cs_t = cs.T
        dtx_ref[...] = _expand(dt, ex_ref, 2)
        csx_ref[...] = _expand(cs, ex_ref, 3)

        for g in range(NG):
            cols = slice(g * GRP, (g + 1) * GRP)
            s_prev = st_ref[0, g]
            sb = _bf(s_prev)
            y, _, (xs, bb, cbm, xd, lms, mfs, mcat, xbd, yoff, xdd, csx, csl) = _ssd_group_fwd(
                g, xa_ref, s_prev, cs, cs_t, dtx_ref, csx_ref, dsk_ref)
            z = pb_ref[:, cols]
            sz = _silu(z)
            yz = y * sz
            rr = lax.rsqrt(jnp.mean(yz * yz, axis=-1, keepdims=True) + EPS)
            nrm = yz * rr
            dyb_g = dy_ref[:, cols]
            dcol_ref[1:2, cols] += jnp.sum(dyb_g * nrm, axis=0, keepdims=True)
            dn = dyb_g * snw_ref[:, cols]
            dyz = rr * (dn - nrm * jnp.mean(dn * nrm, axis=-1, keepdims=True))
            dyv = dyz * sz
            dp_ref[:, cols] = (dyz * y * _dsilu(z)).astype(dp_ref.dtype)
            dcol_ref[0:1, cols] += jnp.sum(dyv * xs, axis=0, keepdims=True)
            dy16 = _bf(dyv)
            dmcat = _dot_nt(dy16, xbd)
            dg = dmcat[:, 0:Q] * lms[0]
            for r in range(1, HPG):
                dg = dg + dmcat[:, r * Q:(r + 1) * Q] * lms[r]
            e_hi, e_lo = _split(jnp.concatenate([dmcat[:, r * Q:(r + 1) * Q] * mfs[r] for r in range(HPG)], axis=1), 2)
            rows = _dot(e_hi, selr_ref[...]) + _dot(e_lo, selr_ref[...])
            dcs_ref[...] += rows if g == 0 else pltpu.roll(rows, HPG * g, 1)
            ones8 = jnp.ones((8, Q), _MXU)
            csum = _dot(ones8, e_hi) + _dot(ones8, e_lo)
            for r in range(HPG):
                dcst_ref[HPG * g + r:HPG * g + r + 1, :] = csum[0:1, r * Q:(r + 1) * Q]
            big = _dot_tn(mcat, dy16)
            dxd_diag = jnp.where(_head_mask(0), big[0:Q], 0.0)
            for r in range(1, HPG):
                dxd_diag = dxd_diag + jnp.where(_head_mask(r), big[r * Q:(r + 1) * Q], 0.0)
            dsn = ds_ref[g]
            dsn16 = _bf(dsn)
            dxdd = _dot(bb, dsn16)
            t_state = dxdd * xdd
            dxd = dxd_diag + dxdd * jnp.exp(csl - csx)
            rcs_ref[:, cols] = dyv * yoff - t_state
            rl_ref[0:1, cols] = (jnp.sum(t_state, axis=0, keepdims=True)
                                 + jnp.exp(csl) * jnp.sum(s_prev * dsn, axis=0, keepdims=True))
            rdt_ref[:, cols] = dxd * xs
            dxa_ref[:, cols] = dyv * dsk_ref[:, cols] + dxd * dtx_ref[:, cols]
            dg16 = _bf(dg)
            dw16 = _bf(dyv * jnp.exp(csx))
            dxa_ref[:, DI + NG * NS + g * NS:DI + NG * NS + (g + 1) * NS] = _dot(dg16, bb) + _dot_nt(dw16, sb)
            dxa_ref[:, DI + g * NS:DI + (g + 1) * NS] = _dot_tn(dg16, cbm) + _dot_nt(_bf(xdd), dsn16)
            ds_ref[g] = dsn * jnp.exp(csl) + _dot_tn(cbm, dw16)

        row = lax.broadcasted_iota(jnp.int32, (Q, 128), 0)
        dcs = (dcs_ref[...] - dcst_ref[...].T + _reduce(rcs_ref[...], rd_ref)
               + jnp.where(row == Q - 1, _reduce(rl_ref[...], rd_ref)[0:1, :], 0.0))
        upper = lax.broadcasted_iota(jnp.int32, (Q, Q), 0) <= lax.broadcasted_iota(jnp.int32, (Q, Q), 1)
        dadt = _dot_exact(upper.astype(f32), dcs)
        valid = lax.broadcasted_iota(jnp.int32, (Q, 128), 1) < NH
        ddt = jnp.where(valid, _reduce(rdt_ref[...], rd_ref) + dadt * a, 0.0)
        ddtr = ddt * _sigmoid(xdt)
        dhd_ref[0:1, :] += jnp.sum(ddtr, axis=0, keepdims=True)
        dhd_ref[1:2, :] += jnp.sum(jnp.where(valid, dadt * dt * a, 0.0), axis=0, keepdims=True)
        dp_ref[:, OFF_DT:OFF_DT + 128] = ddtr.astype(dp_ref.dtype)
        dp_ref[:, OFF_DT + 128:WB] = jnp.zeros((Q, WB - OFF_DT - 128), dp_ref.dtype)
        for j in range(CD // 512):
            cj = slice(j * 512, (j + 1) * 512)
            dxc = dxa_ref[:, cj] * _dsilu(xc_ref[:, cj])
            dxe_ref[0:Q, cj] = dxc
            dcb_ref[0:1, cj] += jnp.sum(dxc, axis=0, keepdims=True)
            raw = pb_ref[:, OFF_XBC + j * 512:OFF_XBC + (j + 1) * 512]
            e = dxe_ref[:, cj]
            dcw_ref[KC - 1:KC, cj] += jnp.sum(dxc * raw, axis=0, keepdims=True)
            dxb = cw_ref[KC - 1:KC, cj] * dxc
            for s in range(1, KC):
                sh = pltpu.roll(e, Q + 8 - s, 0)[0:Q]
                dcw_ref[KC - 1 - s:KC - s, cj] += jnp.sum(sh * raw, axis=0, keepdims=True)
                dxb = dxb + cw_ref[KC - 1 - s:KC - s, cj] * sh
            dp_ref[:, OFF_XBC + j * 512:OFF_XBC + (j + 1) * 512] = dxb.astype(dp_ref.dtype)
        dxe_ref[Q:Q + 8, :] = dxe_ref[0:8, :]

    rev = lambda i: (nc - 1 - i, 0)
    outs, exchanged = _carry_call(
        body, name, (nc,),
        [pl.BlockSpec((Q, DI), rev), pl.BlockSpec((Q, WB), rev), pl.BlockSpec((Q, CD), rev),
         pl.BlockSpec((1, NG, NS, GRP), lambda i: (nc - 1 - i, 0, 0, 0)),
         _full_spec((8, CD)), _full_spec((1, 128)), _full_spec((1, 128)),
         _full_spec((1, DI)), _full_spec((1, DI)), _full_spec((128, DI)), _full_spec((DI, 128)), _full_spec((HPG * Q, 128))],
        [pl.BlockSpec((Q, WB), rev), _full_spec((8, CD)), _full_spec((8, CD)), _full_spec((8, 128)), _full_spec((8, DI))],
        [jax.ShapeDtypeStruct((T, WB), _MXU), jax.ShapeDtypeStruct((8, CD), f32), jax.ShapeDtypeStruct((8, CD), f32),
         jax.ShapeDtypeStruct((8, 128), f32), jax.ShapeDtypeStruct((8, DI), f32)],
        [pltpu.VMEM((NG, NS, GRP), f32), pltpu.VMEM((Q, CD), f32), pltpu.VMEM((Q, CD), f32),
         pltpu.VMEM((Q + 8, CD), f32), pltpu.VMEM((Q, DI), f32), pltpu.VMEM((Q, DI), f32),
         pltpu.VMEM((Q, DI), f32), pltpu.VMEM((Q, DI), f32), pltpu.VMEM((8, DI), f32),
         pltpu.VMEM((Q, 128), f32), pltpu.VMEM((128, Q), f32)],
        (dyb, pB, xc, states, cw, dtb, alog, dsk, snw, ex, rd, selr), carry)
    return outs + [exchanged]


def _merge_fwd(ya, yb, pG, x, gate, wa, wb, wo, name, carry=()):
    T = x.shape[0]
    ts = _tile(T, 256)

    def body(ya_ref, yb_ref, g_ref, x_ref, gate_ref, wa_ref, wb_ref, wo_ref, xo_ref, mg_ref, pa_ref, pb_ref):
        pa = _dot(ya_ref[...], wa_ref[...])
        pb = _dot(yb_ref[...], wb_ref[...])
        merged = _sigmoid(g_ref[:, 0:D]) * pa + _sigmoid(g_ref[:, D:2 * D]) * pb
        mg = _bf(merged)
        xo_ref[...] = x_ref[...] + gate_ref[...] * _dot(mg, wo_ref[...])
        mg_ref[...] = mg
        pa_ref[...] = pa
        pb_ref[...] = pb

    outs, exchanged = _carry_call(
        body, name, (T // ts,),
        [_row_spec(ts, D), _row_spec(ts, DI), _row_spec(ts, WG), _row_spec(ts, D), _full_spec((1, D)),
         _full_spec((D, D)), _full_spec((DI, D)), _full_spec((D, D))],
        [_row_spec(ts, D)] * 4,
        [jax.ShapeDtypeStruct((T, D), f32), jax.ShapeDtypeStruct((T, D), _MXU), jax.ShapeDtypeStruct((T, D), f32),
         jax.ShapeDtypeStruct((T, D), f32)],
        [], (ya, yb, pG, x, gate, wa, wb, wo), carry)
    return outs + [exchanged]


def _merge_bwd(dxout, merged, pa, pb, pG, gate, wo, name):
    T = dxout.shape[0]
    ts = _tile(T, 256)

    def body(dx_ref, mg_ref, pa_ref, pb_ref, g_ref, gate_ref, wo_ref, do_ref, dpa_ref, dpb_ref, dg_ref, acc_ref):
        @pl.when(pl.program_id(0) == 0)
        def _():
            acc_ref[...] = jnp.zeros_like(acc_ref)

        dxo = dx_ref[...]
        acc_ref[0:1, :] += jnp.sum(dxo * _dot(mg_ref[...], wo_ref[...]), axis=0, keepdims=True)
        do = _bf(dxo * gate_ref[...])
        do_ref[...] = do
        dmerged = _dot_nt(do, wo_ref[...])
        sa, sb = _sigmoid(g_ref[:, 0:D]), _sigmoid(g_ref[:, D:2 * D])
        dpa_ref[...] = (dmerged * sa).astype(dpa_ref.dtype)
        dpb_ref[...] = (dmerged * sb).astype(dpb_ref.dtype)
        dg_ref[:, 0:D] = (dmerged * pa_ref[...] * sa * (1.0 - sa)).astype(dg_ref.dtype)
        dg_ref[:, D:2 * D] = (dmerged * pb_ref[...] * sb * (1.0 - sb)).astype(dg_ref.dtype)

    return pl.pallas_call(
        body, name=name, grid=(T // ts,),
        in_specs=[_row_spec(ts, D), _row_spec(ts, D), _row_spec(ts, D), _row_spec(ts, D), _row_spec(ts, WG),
                  _full_spec((1, D)), _full_spec((D, D))],
        out_specs=[_row_spec(ts, D), _row_spec(ts, D), _row_spec(ts, D), _row_spec(ts, WG), _full_spec((8, D))],
        out_shape=[jax.ShapeDtypeStruct((T, D), _MXU), jax.ShapeDtypeStruct((T, D), _MXU), jax.ShapeDtypeStruct((T, D), _MXU),
                   jax.ShapeDtypeStruct((T, WG), _MXU), jax.ShapeDtypeStruct((8, D), f32)],
        compiler_params=_params(("arbitrary",)))(dxout, merged, pa, pb, pG, gate, wo)


def _layer_fwd(x, mod, lp, tag, carry=None):
    carry = carry or {}
    got = {}
    h = _modulate(x, lp["nw"], mod[0:1], mod[1:2], f"modulate_{tag}")
    pA = _mm(h, lp["w_gm"], "nn", f"proj_gm_{tag}")
    if carry.get("proj_ssd"):
        pB, got["proj_ssd"] = _mm(h, lp["w_ssd"], "nn", f"proj_ssd_{tag}", carry=carry["proj_ssd"])
    else:
        pB = _mm(h, lp["w_ssd"], "nn", f"proj_ssd_{tag}")
    pG = _mm(h, lp["w_g"], "nn", f"proj_gate_{tag}")
    ya = _gmlp_fwd(pA, lp["lw"], lp["lb"], lp["ws"], lp["bsx"], f"gmlp_fwd_{tag}")
    yb, xc, states, got["ssd_fwd"] = _ssd_fwd(pB, lp["cw"], lp["cb"], lp["dtb"], lp["alog"], lp["dsk"], lp["snw"], f"ssd_fwd_{tag}",
                                              carry.get("ssd_fwd", ()))
    xo, merged, pa, pb, got["merge_fwd"] = _merge_fwd(ya, yb, pG, x, mod[2:3], lp["wa"], lp["wb"], lp["wo"], f"merge_fwd_{tag}",
                                                      carry.get("merge_fwd", ()))
    return xo, dict(x=x, h=h, pA=pA, pB=pB, pG=pG, ya=ya, yb=yb, xc=xc, states=states, merged=merged, pa=pa, pb=pb), got


def _layer_bwd(dxo, sv, mod, lp, tag, carry=()):
    do, dpa, dpb, dpG, s_gate = _merge_bwd(dxo, sv["merged"], sv["pa"], sv["pb"], sv["pG"], mod[2:3], lp["wo"], f"merge_bwd_{tag}")
    g = {}
    g["wo"] = _mm(sv["merged"], do, "tn", f"dw_out_{tag}", out_dtype=_WIRE)
    g["wa"] = _mm(sv["ya"], dpa, "tn", f"dw_proj_a_{tag}", out_dtype=_WIRE)
    g["wb"] = _mm(sv["yb"], dpb, "tn", f"dw_proj_b_{tag}", out_dtype=_WIRE)
    dya = _mm(dpa, lp["wa"], "nt", f"dy_a_{tag}")
    dyb = _mm(dpb, lp["wb"], "nt", f"dy_b_{tag}")
    dpA, s_ln, g["ws"], dbsx = _gmlp_bwd(dya, sv["pA"], lp["lw"], lp["lb"], lp["ws"], lp["bsx"], f"gmlp_bwd_{tag}")
    dpB, g["cw"], s_cb, s_hd, s_col, got = _ssd_bwd(dyb, sv["pB"], sv["xc"], sv["states"], lp["cw"], lp["dtb"], lp["alog"],
                                                    lp["dsk"], lp["snw"], f"ssd_bwd_{tag}", carry)
    dh = _mm(dpG, lp["w_g"], "nt", f"dh_gate_{tag}")
    dh = _mm(dpA, lp["w_gm"], "nt", f"dh_gm_{tag}", acc=dh)
    dh = _mm(dpB, lp["w_ssd"], "nt", f"dh_ssd_{tag}", acc=dh)
    g["w_g"] = _mm(sv["h"], dpG, "tn", f"dw_gate_{tag}", out_dtype=_WIRE)
    g["w_gm"] = _mm(sv["h"], dpA, "tn", f"dw_gm_{tag}", out_dtype=_WIRE)
    g["w_ssd"] = _mm(sv["h"], dpB, "tn", f"dw_ssd_{tag}", out_dtype=_WIRE)
    dx, s_mod = _modulate_bwd(dh, dxo, sv["x"], lp["nw"], mod[1:2], f"modulate_bwd_{tag}")
    g["mod"] = jnp.concatenate([s_mod[0], s_mod[1], s_gate[0]])
    g["nw"] = s_mod[2]
    g["lw"], g["lb"] = s_ln[0], s_ln[1]
    g["bs"] = dbsx.reshape(Q, NG, Q).sum(-1).T
    g["cb"] = s_cb[0]
    g["dtb"], g["alog"] = s_hd[0, :NH], s_hd[1, :NH]
    g["dsk"] = s_col[0].reshape(NH, P).sum(-1)
    g["snw"] = s_col[1]
    g["cw"] = g["cw"][0:KC]
    return dx, g, got


def _prep_layer(nw, w_in_full, lw, lb, ws, bs, cw_full, cb, dtb, alog, dsk, snw, wa, wb, wo):
    z = jnp.zeros((D, WB - (DI + CD + NH)), w_in_full.dtype)
    pad_h = lambda v: jnp.pad(v, (0, 128 - NH)).reshape(1, 128)
    return dict(
        nw=nw.reshape(1, D),
        w_gm=w_in_full[:, 0:WA],
        w_ssd=jnp.concatenate([w_in_full[:, WA:WA + DI + CD], w_in_full[:, WA + DI + CD:WA + DI + CD + NH], z], axis=1),
        w_g=w_in_full[:, WA + DI + CD + NH:NIN],
        lw=lw.reshape(1, D), lb=lb.reshape(1, D), ws=ws, bsx=jnp.repeat(bs.T, Q, axis=1),
        cw=jnp.pad(cw_full, ((0, 8 - KC), (0, 0))), cb=cb.reshape(1, CD), dtb=pad_h(dtb), alog=pad_h(alog),
        dsk=jnp.repeat(dsk, P).reshape(1, DI), snw=snw.reshape(1, DI), wa=wa, wb=wb, wo=wo)


def _exchange(src, gather, name):
    if not gather:
        assert src.shape[0] == NDEV

    def body(src_ref, out_ref, send_sems, recv_sems, local_sem):
        start, finish = _exch_ops(src_ref, out_ref, send_sems, recv_sems, local_sem, gather)
        start()
        finish()

    return pl.pallas_call(
        body, name=name, out_shape=_exch_shape(src, gather),
        in_specs=[pl.BlockSpec(memory_space=pl.ANY)], out_specs=pl.BlockSpec(memory_space=pl.ANY), scratch_shapes=_EXCH_SEMS)(src)


def _mod_dist(c8, ada_w, ada_b_cols, name):
    L, _, AW = ada_w.shape

    def body(c_ref, w_ref, b_ref, parts_ref, sc_ref, call_ref, mine_ref, send1, recv1, send2, recv2):
        x, y, c = lax.axis_index("x"), lax.axis_index("y"), lax.axis_index("c")
        me = 4 * x + 2 * y + c

        def peer(j):
            px = 1 - x if (j >> 2) & 1 else x
            py = 1 - y if (j >> 1) & 1 else y
            pc = 1 - c if j & 1 else c
            return (px, py, pc), 4 * px + 2 * py + pc

        def copy(j, src, dst, sems, landing):
            dev, idx = peer(j)
            return pltpu.make_async_remote_copy(
                src_ref=src, dst_ref=dst.at[idx] if landing else dst.at[me], send_sem=sems[0].at[j - 1], recv_sem=sems[1].at[j - 1],
                device_id=dev, device_id_type=pl.DeviceIdType.MESH)

        def all_to_all(src, dst, sems):
            for j in range(1, NDEV):
                copy(j, src, dst, sems, False).start()
            for j in range(1, NDEV):
                copy(j, src, dst, sems, True).wait_recv()
            for j in range(1, NDEV):
                copy(j, src, dst, sems, False).wait_send()

        call_ref[me] = c_ref[...]
        all_to_all(c_ref, call_ref, (send1, recv1))
        row = lax.broadcasted_iota(jnp.int32, (8, D), 0)
        cm = jnp.zeros((8, D), f32)
        for k in range(NDEV):
            cm = jnp.where(row == k, call_ref[k], cm)
        sc = _silu(cm)
        sc_ref[...] = sc
        for l in range(L):
            mine_ref[l] = _dot(_bf(sc), w_ref[l]) + b_ref[l]
        parts_ref[me] = mine_ref[...]
        all_to_all(mine_ref, parts_ref, (send2, recv2))

    vmem = pl.BlockSpec(memory_space=pltpu.VMEM)
    sems = pltpu.SemaphoreType.DMA((NDEV - 1,))
    return pl.pallas_call(
        body, name=name, in_specs=[vmem, vmem, vmem], out_specs=[vmem, vmem],
        out_shape=[jax.ShapeDtypeStruct((NDEV, L, 8, AW), f32), jax.ShapeDtypeStruct((8, D), f32)],
        scratch_shapes=[pltpu.VMEM((NDEV, 8, D), f32), pltpu.VMEM((L, 8, AW), f32), sems, sems, sems, sems])(c8, ada_w, ada_b_cols)


def _ada_w_grad(sc_all, dmod_cols, name):
    W = dmod_cols.shape[1]

    def body(s_ref, d_ref, o_ref):
        o_ref[...] = lax.dot_general(s_ref[...], d_ref[...], (((0,), (0,)), ((), ())), preferred_element_type=f32,
                                     precision=lax.Precision.HIGHEST)

    return pl.pallas_call(body, name=name, out_shape=jax.ShapeDtypeStruct((D, W), f32))(sc_all, dmod_cols)


def _adamw_math(w, g, m, v):
    m = ADAM_B1 * m + (1.0 - ADAM_B1) * g
    v = ADAM_B2 * v + (1.0 - ADAM_B2) * (g * g)
    m_hat = m / (1.0 - ADAM_B1 ** ADAM_STEP)
    v_hat = v / (1.0 - ADAM_B2 ** ADAM_STEP)
    delta = -ADAM_LR * (m_hat / (jnp.sqrt(v_hat) + ADAM_EPS) + ADAM_WD * w)
    return delta, m, v


def _sum_adamw(recv, w, m, v, name):
    n, R, C = recv.shape
    tr = _tile(R, 128 if C > 1024 else 256)

    def body(r_ref, w_ref, m_ref, v_ref, g_ref, d_ref, nm_ref, nv_ref):
        g = r_ref[0].astype(f32)
        for k in range(1, n):
            g = g + r_ref[k].astype(f32)
        g_ref[...] = g
        d_ref[...], nm_ref[...], nv_ref[...] = _adamw_math(w_ref[...], g, m_ref[...], v_ref[...])

    spec = pl.BlockSpec((tr, C), lambda i: (i, 0))
    return pl.pallas_call(
        body, name=name, grid=(R // tr,),
        in_specs=[pl.BlockSpec((n, tr, C), lambda i: (0, i, 0)), spec, spec, spec], out_specs=[spec] * 4,
        out_shape=[jax.ShapeDtypeStruct((R, C), f32)] * 4, compiler_params=_params(("arbitrary",)))(recv, w, m, v)


def _pack(arrays, rows):
    flat = []
    for a in arrays:
        a = a.reshape(-1).astype(f32)
        flat.append(jnp.pad(a, (0, (-a.shape[0]) % 128)))
    flat = jnp.concatenate(flat)
    return jnp.pad(flat, (0, rows * 128 - flat.shape[0])).reshape(rows, 128)


def _unpack(slab, shapes):
    flat = slab.reshape(-1)
    out, off = [], 0
    for s in shapes:
        n = 1
        for d in s:
            n *= d
        out.append(flat[off:off + n].reshape(s))
        off += n + (-n) % 128
    return out


_SMALL = ["ada_b", "norm_w", "gm_ln_w", "gm_ln_b", "gm_ws", "gm_bs", "conv_b", "dt_bias", "a_log", "d_skip", "ssm_norm_w",
          "final_norm_w"]


def kernel(x, c, ada_w, ada_b, norm_w, w_in, gm_ln_w, gm_ln_b, gm_ws, gm_bs, conv_w, conv_b, dt_bias, a_log, d_skip, ssm_norm_w, w_proj_a, w_proj_b, w_out, final_norm_w, loss_target, m_ada_w, m_ada_b, m_norm_w, m_w_in, m_gm_ln_w, m_gm_ln_b, m_gm_ws, m_gm_bs, m_conv_w, m_conv_b, m_dt_bias, m_a_log, m_d_skip, m_ssm_norm_w, m_w_proj_a, m_w_proj_b, m_w_out, m_final_norm_w, v_ada_w, v_ada_b, v_norm_w, v_w_in, v_gm_ln_w, v_gm_ln_b, v_gm_ws, v_gm_bs, v_conv_w, v_conv_b, v_dt_bias, v_a_log, v_d_skip, v_ssm_norm_w, v_w_proj_a, v_w_proj_b, v_w_out, v_final_norm_w):
    L = 2
    me = 4 * lax.axis_index("x") + 2 * lax.axis_index("y") + lax.axis_index("c")
    W = dict(ada_w=ada_w, ada_b=ada_b, norm_w=norm_w, w_in=w_in, gm_ln_w=gm_ln_w, gm_ln_b=gm_ln_b, gm_ws=gm_ws, gm_bs=gm_bs,
             conv_w=conv_w, conv_b=conv_b, dt_bias=dt_bias, a_log=a_log, d_skip=d_skip, ssm_norm_w=ssm_norm_w, w_proj_a=w_proj_a,
             w_proj_b=w_proj_b, w_out=w_out, final_norm_w=final_norm_w)
    M = dict(ada_w=m_ada_w, ada_b=m_ada_b, norm_w=m_norm_w, w_in=m_w_in, gm_ln_w=m_gm_ln_w, gm_ln_b=m_gm_ln_b, gm_ws=m_gm_ws,
             gm_bs=m_gm_bs, conv_w=m_conv_w, conv_b=m_conv_b, dt_bias=m_dt_bias, a_log=m_a_log, d_skip=m_d_skip,
             ssm_norm_w=m_ssm_norm_w, w_proj_a=m_w_proj_a, w_proj_b=m_w_proj_b, w_out=m_w_out, final_norm_w=m_final_norm_w)
    V = dict(ada_w=v_ada_w, ada_b=v_ada_b, norm_w=v_norm_w, w_in=v_w_in, gm_ln_w=v_gm_ln_w, gm_ln_b=v_gm_ln_b, gm_ws=v_gm_ws,
             gm_bs=v_gm_bs, conv_w=v_conv_w, conv_b=v_conv_b, dt_bias=v_dt_bias, a_log=v_a_log, d_skip=v_d_skip,
             ssm_norm_w=v_ssm_norm_w, w_proj_a=v_w_proj_a, w_proj_b=v_w_proj_b, w_out=v_w_out, final_norm_w=v_final_norm_w)
    SW = NIN // NDEV
    AW = 3 * D // NDEV
    CW = CD // NDEV
    RA, RB = D // NDEV, DI // NDEV

    wire = lambda a: a.astype(_WIRE)
    rows_of = lambda d, l: jnp.concatenate([d["w_proj_a"][l], d["w_proj_b"][l], d["w_out"][l]], axis=0)
    g_win, g_rows = [None] * L, [None] * L
    g_win[0] = _exchange(wire(w_in[0]), True, "gather_w_in_l0")
    g_rows[0] = _exchange(wire(rows_of(W, 0)), True, "gather_w_rows_l0")
    g_cw = _exchange(conv_w.reshape(L * KC, CW), True, "gather_conv_w").reshape(NDEV, L, KC, CW)
    parts, sc_all = _mod_dist(jnp.broadcast_to(c, (8, D)), wire(ada_w),
                              lax.dynamic_slice_in_dim(ada_b, me * AW, AW, axis=1).reshape(L, 1, AW), "mod_dist")
    mods = [lax.dynamic_index_in_dim(parts[:, l], me, axis=1, keepdims=False).reshape(3, D) for l in range(L)]
    cols = lambda g: jnp.transpose(g, (1, 0, 2)).reshape(g.shape[1], -1)

    def prep(l):
        r = g_rows[l]
        return _prep_layer(norm_w[l], cols(g_win[l]), gm_ln_w[l], gm_ln_b[l], gm_ws[l], gm_bs[l], cols(g_cw[:, l]), conv_b[l],
                           dt_bias[l], a_log[l], d_skip[l], ssm_norm_w[l], r[:, 0:RA].reshape(D, D),
                           r[:, RA:RA + RB].reshape(DI, D), r[:, RA + RB:].reshape(D, D))

    half = D // 2
    lp0 = prep(0)
    h, sv0, got = _layer_fwd(x[0], mods[0], lp0, "l0", dict(proj_ssd=[(wire(w_in[1][:half]), True)],
                                                           ssd_fwd=[(wire(w_in[1][half:]), True)],
                                                           merge_fwd=[(wire(rows_of(W, 1)), True)]))
    g_win[1] = jnp.concatenate([got["proj_ssd"][0], got["ssd_fwd"][0]], axis=1)
    g_rows[1] = got["merge_fwd"][0]
    lp1 = prep(1)
    h, sv1, _ = _layer_fwd(h, mods[1], lp1, "l1")
    dh, s_fin = _final_loss(h, loss_target[0], final_norm_w.reshape(1, D), "final_loss")
    loss = lax.psum(0.5 / D * s_fin[1, 0], ("x", "y", "c"))

    def win_blocks(g):
        full = jnp.concatenate([g["w_gm"], g["w_ssd"][:, 0:DI + CD], g["w_ssd"][:, DI + CD:DI + CD + NH], g["w_g"]], axis=1)
        return jnp.transpose(full.reshape(D, NDEV, SW), (1, 0, 2))

    def row_blocks(g):
        return jnp.concatenate([g["wa"].reshape(NDEV, RA, D), g["wb"].reshape(NDEV, RB, D), g["wo"].reshape(NDEV, RA, D)], axis=1)

    dh, g1, _ = _layer_bwd(dh, sv1, mods[1], lp1, "l1")
    dh, g0, got = _layer_bwd(dh, sv0, mods[0], lp0, "l0", [(win_blocks(g1), False), (row_blocks(g1), False)])
    grads = [g0, g1]
    grad_x = dh[None]
    recv_win = [_exchange(win_blocks(g0), False, "scatter_w_in_l0"), got[0]]
    recv_rows = [_exchange(row_blocks(g0), False, "scatter_w_rows_l0"), got[1]]

    o_win_l = [_sum_adamw(recv_win[l], w_in[l], m_w_in[l], v_w_in[l], f"adamw_w_in_l{l}") for l in range(L)]
    o_win = [jnp.stack([o_win_l[l][k] for l in range(L)]) for k in range(4)]
    o_rows_l = [_sum_adamw(recv_rows[l], rows_of(W, l), rows_of(M, l), rows_of(V, l), f"adamw_w_rows_l{l}") for l in range(L)]
    o_pa = [jnp.stack([o_rows_l[l][k][0:RA] for l in range(L)]) for k in range(4)]
    o_pb = [jnp.stack([o_rows_l[l][k][RA:RA + RB] for l in range(L)]) for k in range(4)]
    o_po = [jnp.stack([o_rows_l[l][k][RA + RB:] for l in range(L)]) for k in range(4)]

    st = lambda key: jnp.stack([g[key] for g in grads])
    small_g = dict(ada_b=st("mod"), norm_w=st("nw"), gm_ln_w=st("lw"), gm_ln_b=st("lb"), gm_ws=st("ws"), gm_bs=st("bs"),
                   conv_b=st("cb"), dt_bias=st("dtb"), a_log=st("alog"), d_skip=st("dsk"), ssm_norm_w=st("snw"),
                   final_norm_w=s_fin[0])
    extra = [st("cw")]
    shapes = [W[n].shape for n in _SMALL] + [(L, KC, CD)]
    n_rows = sum(-(-functools.reduce(lambda a, b: a * b, s, 1) // 128) for s in shapes)
    n_rows = -(-n_rows // 256) * 256
    zeros_extra = [jnp.zeros((L, KC, CD), f32)]
    slab_g = _pack([small_g[n] for n in _SMALL] + extra, n_rows)
    slab_all = _exchange(slab_g, True, "gather_small")
    o_small = _sum_adamw(slab_all, _pack([W[n] for n in _SMALL] + zeros_extra, n_rows),
                         _pack([M[n] for n in _SMALL] + zeros_extra, n_rows),
                         _pack([V[n] for n in _SMALL] + zeros_extra, n_rows), "adamw_small")
    o_small = [_unpack(o, shapes) for o in o_small]
    small_out = {n: [o_small[k][i] for k in range(4)] for i, n in enumerate(_SMALL)}

    g_cw_mine = lax.dynamic_slice_in_dim(o_small[0][len(_SMALL)], me * CW, CW, axis=2)
    o_cw = _sum_adamw(g_cw_mine.reshape(1, L * KC, CW), conv_w.reshape(L * KC, CW), m_conv_w.reshape(L * KC, CW),
                      v_conv_w.reshape(L * KC, CW), "adamw_conv_w")
    o_cw = [o.reshape(L, KC, CW) for o in o_cw]

    dmod_all = jnp.stack([_unpack(slab_all[k], shapes)[0] for k in range(NDEV)])
    dmod_cols = lax.dynamic_slice_in_dim(dmod_all, me * AW, AW, axis=2).reshape(NDEV, L * AW)
    g_ada_w = jnp.transpose(_ada_w_grad(sc_all, dmod_cols, "ada_w_grad").reshape(D, L, AW), (1, 0, 2))
    o_ada = _sum_adamw(g_ada_w.reshape(1, L * D, AW), ada_w.reshape(L * D, AW), m_ada_w.reshape(L * D, AW),
                       v_ada_w.reshape(L * D, AW), "adamw_ada_w")
    o_ada = [o.reshape(L, D, AW) for o in o_ada]

    big = dict(ada_w=o_ada, w_in=o_win, conv_w=o_cw, w_proj_a=o_pa, w_proj_b=o_pb, w_out=o_po)
    order = ["ada_w", "ada_b", "norm_w", "w_in", "gm_ln_w", "gm_ln_b", "gm_ws", "gm_bs", "conv_w", "conv_b", "dt_bias", "a_log",
             "d_skip", "ssm_norm_w", "w_proj_a", "w_proj_b", "w_out", "final_norm_w"]
    pick = lambda n, k: big[n][k] if n in big else small_out[n][k]
    return (loss, grad_x, *[pick(n, 0) for n in order], *[pick(n, 1) for n in order], *[pick(n, 2) for n in order],
            *[pick(n, 3) for n in order])
```

```python
import functools

import jax
import jax.numpy as jnp
from jax import lax
from jax.experimental import pallas as pl
from jax.experimental.pallas import tpu as pltpu

f32 = jnp.float32
_MXU = jnp.bfloat16
_WIRE = jnp.bfloat16

D = 1024
Q = 128
NG = 8
DI = 2048
NH = 32
P = 64
HPG = 4
NS = 128
KC = 4
CD = 4096
GRP = DI // NG
EPS = 1e-6
NDEV = 8
WA = 3 * D
WB = DI + CD + 256
WG = 2 * D
OFF_XBC = DI
OFF_DT = DI + CD
NIN = 11296
VMEM_LIMIT = 56 * 1024 * 1024
NEG = -1e30

ADAM_LR, ADAM_B1, ADAM_B2, ADAM_EPS, ADAM_WD, ADAM_STEP = 0.001, 0.9, 0.999, 1e-08, 0.01, 10


def _bf(x):
    return x.astype(_MXU)


def _dot(a, b):
    return jnp.dot(a, b, preferred_element_type=f32)


def _dot_nt(a, b):
    return lax.dot_general(a, b, (((1,), (1,)), ((), ())), preferred_element_type=f32)


def _dot_tn(a, b):
    return lax.dot_general(a, b, (((0,), (0,)), ((), ())), preferred_element_type=f32)


def _dot_exact(a, b):
    return jnp.dot(a, b, preferred_element_type=f32, precision=lax.Precision.HIGHEST)


def _sigmoid(x):
    return jax.nn.sigmoid(x)


def _silu(x):
    return x * _sigmoid(x)


def _dsilu(x):
    s = _sigmoid(x)
    return s * (1.0 + x * (1.0 - s))


_GK = 0.7978845608028654
_GC = 0.044715


def _gelu(x):
    return 0.5 * x * (1.0 + jnp.tanh(_GK * (x + _GC * x * x * x)))


def _dgelu(x):
    t = jnp.tanh(_GK * (x + _GC * x * x * x))
    return 0.5 * (1.0 + t) + 0.5 * x * (1.0 - t * t) * _GK * (1.0 + 3.0 * _GC * x * x)


def _softplus(x):
    return jnp.maximum(x, 0.0) + jnp.log1p(jnp.exp(-jnp.abs(x)))


def _tile(n, cap):
    if n <= cap:
        return n
    best = None
    for t in range(128, cap + 1, 128):
        if n % t == 0:
            best = t
    assert best is not None, (n, cap)
    return best


def _params(sem):
    return pltpu.CompilerParams(dimension_semantics=sem, vmem_limit_bytes=VMEM_LIMIT)


def _exch_ops(src_ref, out_ref, send_sems, recv_sems, local_sem, gather):
    x, y, c = lax.axis_index("x"), lax.axis_index("y"), lax.axis_index("c")
    me = 4 * x + 2 * y + c

    def peer(j):
        px = 1 - x if (j >> 2) & 1 else x
        py = 1 - y if (j >> 1) & 1 else y
        pc = 1 - c if j & 1 else c
        return (px, py, pc), 4 * px + 2 * py + pc

    def copy(j, landing):
        dev, idx = peer(j)
        return pltpu.make_async_remote_copy(
            src_ref=src_ref if gather else src_ref.at[idx], dst_ref=out_ref.at[idx] if landing else out_ref.at[me],
            send_sem=send_sems.at[j - 1], recv_sem=recv_sems.at[j - 1], device_id=dev, device_id_type=pl.DeviceIdType.MESH)

    mine = pltpu.make_async_copy(src_ref if gather else src_ref.at[me], out_ref.at[me], local_sem)

    def start():
        mine.start()
        for j in range(1, NDEV):
            copy(j, False).start()

    def finish():
        for j in range(1, NDEV):
            copy(j, True).wait_recv()
        for j in range(1, NDEV):
            copy(j, False).wait_send()
        mine.wait()

    return start, finish


def _exch_shape(src, gather):
    return jax.ShapeDtypeStruct((NDEV,) + tuple(src.shape if gather else src.shape[1:]), src.dtype)


_EXCH_SEMS = [pltpu.SemaphoreType.DMA((NDEV - 1,)), pltpu.SemaphoreType.DMA((NDEV - 1,)), pltpu.SemaphoreType.DMA]


def _carry_call(body, name, grid, in_specs, out_specs, out_shape, scratch_shapes, args, carry=()):
    n_in, n_out, n_sc, nx = len(in_specs), len(out_specs), len(scratch_shapes), len(carry)
    sem = ("arbitrary",) * len(grid)
    if nx == 0:
        outs = pl.pallas_call(body, name=name, grid=grid, in_specs=in_specs, out_specs=out_specs, out_shape=out_shape,
                              scratch_shapes=scratch_shapes, compiler_params=_params(sem))(*args)
        return list(outs), []

    def wrapped(*refs):
        ins, srcs = refs[:n_in], refs[n_in:n_in + nx]
        outs, dsts = refs[n_in + nx:n_in + nx + n_out], refs[n_in + nx + n_out:n_in + 2 * nx + n_out]
        scratch, sems = refs[n_in + 2 * nx + n_out:n_in + 2 * nx + n_out + n_sc], refs[n_in + 2 * nx + n_out + n_sc:]
        ops = [_exch_ops(srcs[i], dsts[i], sems[3 * i], sems[3 * i + 1], sems[3 * i + 2], carry[i][1]) for i in range(nx)]
        first = functools.reduce(jnp.logical_and, [pl.program_id(d) == 0 for d in range(len(grid))])
        last = functools.reduce(jnp.logical_and, [pl.program_id(d) == grid[d] - 1 for d in range(len(grid))])

        @pl.when(first)
        def _():
            for start, _ in ops:
                start()

        body(*ins, *outs, *scratch)

        @pl.when(last)
        def _():
            for _, finish in ops:
                finish()

    hbm = pl.BlockSpec(memory_space=pl.ANY)
    outs = pl.pallas_call(
        wrapped, name=name, grid=grid, in_specs=list(in_specs) + [hbm] * nx, out_specs=list(out_specs) + [hbm] * nx,
        out_shape=list(out_shape) + [_exch_shape(s, g) for s, g in carry], scratch_shapes=list(scratch_shapes) + _EXCH_SEMS * nx,
        compiler_params=_params(sem))(*args, *[s for s, _ in carry])
    return list(outs[:n_out]), list(outs[n_out:])


def _mm(a, b, mode, name, acc=None, out_dtype=f32, tm_cap=512, tn_cap=1280, tk_cap=1280, carry=()):
    if mode == "nn":
        (M, K), (K2, N) = a.shape, b.shape
    elif mode == "nt":
        (M, K), (N, K2) = a.shape, b.shape
    else:
        (K, M), (K2, N) = a.shape, b.shape
        tk_cap = min(tk_cap, 512)
    assert K == K2, (a.shape, b.shape, mode)
    tm, tn, tk = _tile(M, tm_cap), _tile(N, tn_cap), _tile(K, tk_cap)
    nk = K // tk
    if mode == "nn":
        a_spec = pl.BlockSpec((tm, tk), lambda i, j, k: (i, k))
        b_spec = pl.BlockSpec((tk, tn), lambda i, j, k: (k, j))
        dot = _dot
    elif mode == "nt":
        a_spec = pl.BlockSpec((tm, tk), lambda i, j, k: (i, k))
        b_spec = pl.BlockSpec((tn, tk), lambda i, j, k: (j, k))
        dot = _dot_nt
    else:
        a_spec = pl.BlockSpec((tk, tm), lambda i, j, k: (k, i))
        b_spec = pl.BlockSpec((tk, tn), lambda i, j, k: (k, j))
        dot = _dot_tn
    o_spec = pl.BlockSpec((tm, tn), lambda i, j, k: (i, j))
    has_acc = acc is not None

    def body(*refs):
        if has_acc:
            a_ref, b_ref, c_ref, o_ref, acc_ref = refs
        else:
            a_ref, b_ref, o_ref, acc_ref = refs
        k = pl.program_id(2)

        @pl.when(k == 0)
        def _():
            if has_acc:
                acc_ref[...] = c_ref[...]
            else:
                acc_ref[...] = jnp.zeros_like(acc_ref)

        acc_ref[...] += dot(_bf(a_ref[...]), _bf(b_ref[...]))

        @pl.when(k == nk - 1)
        def _():
            o_ref[...] = acc_ref[...].astype(out_dtype)

    in_specs = [a_spec, b_spec] + ([o_spec] if has_acc else [])
    args = (a, b) + ((acc,) if has_acc else ())
    (out,), exchanged = _carry_call(body, name, (M // tm, N // tn, nk), in_specs, [o_spec], [jax.ShapeDtypeStruct((M, N), out_dtype)],
                                    [pltpu.VMEM((tm, tn), f32)], args, carry)
    return (out, exchanged) if carry else out


def _row_spec(ts, w, col=0):
    return pl.BlockSpec((ts, w), lambda i: (i, col))


def _full_spec(shape):
    nd = len(shape)
    return pl.BlockSpec(shape, lambda i: (0,) * nd)


def _modulate(x, nw, shift, scale, name):
    T = x.shape[0]
    ts = _tile(T, 512)

    def body(x_ref, nw_ref, sh_ref, sc_ref, h_ref):
        xv = x_ref[...]
        r = lax.rsqrt(jnp.mean(xv * xv, axis=-1, keepdims=True) + EPS)
        h_ref[...] = ((xv * r * nw_ref[...]) * (1.0 + sc_ref[...]) + sh_ref[...]).astype(h_ref.dtype)

    return pl.pallas_call(
        body, name=name, grid=(T // ts,),
        in_specs=[_row_spec(ts, D), _full_spec((1, D)), _full_spec((1, D)), _full_spec((1, D))],
        out_specs=_row_spec(ts, D), out_shape=jax.ShapeDtypeStruct((T, D), _MXU),
        compiler_params=_params(("arbitrary",)))(x, nw, shift, scale)


def _modulate_bwd(dh, dxout, x, nw, scale, name):
    T = x.shape[0]
    ts = _tile(T, 512)

    def body(dh_ref, dxo_ref, x_ref, nw_ref, sc_ref, dx_ref, acc_ref):
        @pl.when(pl.program_id(0) == 0)
        def _():
            acc_ref[...] = jnp.zeros_like(acc_ref)

        xv, dh_v = x_ref[...], dh_ref[...]
        r = lax.rsqrt(jnp.mean(xv * xv, axis=-1, keepdims=True) + EPS)
        xn = xv * r
        hn = xn * nw_ref[...]
        dhn = dh_v * (1.0 + sc_ref[...])
        acc_ref[0:1, :] += jnp.sum(dh_v, axis=0, keepdims=True)
        acc_ref[1:2, :] += jnp.sum(dh_v * hn, axis=0, keepdims=True)
        acc_ref[2:3, :] += jnp.sum(dhn * xn, axis=0, keepdims=True)
        dxn = dhn * nw_ref[...]
        dx_ref[...] = dxo_ref[...] + r * (dxn - xn * jnp.mean(dxn * xn, axis=-1, keepdims=True))

    return pl.pallas_call(
        body, name=name, grid=(T // ts,),
        in_specs=[_row_spec(ts, D), _row_spec(ts, D), _row_spec(ts, D), _full_spec((1, D)), _full_spec((1, D))],
        out_specs=[_row_spec(ts, D), _full_spec((8, D))],
        out_shape=[jax.ShapeDtypeStruct((T, D), f32), jax.ShapeDtypeStruct((8, D), f32)],
        compiler_params=_params(("arbitrary",)))(dh, dxout, x, nw, scale)


def _final_loss(x, tgt, fw, name):
    T = x.shape[0]
    ts = _tile(T, 512)

    def body(x_ref, t_ref, fw_ref, dx_ref, acc_ref):
        @pl.when(pl.program_id(0) == 0)
        def _():
            acc_ref[...] = jnp.zeros_like(acc_ref)

        xv = x_ref[...]
        r = lax.rsqrt(jnp.mean(xv * xv, axis=-1, keepdims=True) + EPS)
        xn = xv * r
        e = xn * fw_ref[...] - t_ref[...]
        dy = e * (1.0 / D)
        acc_ref[0:1, :] += jnp.sum(dy * xn, axis=0, keepdims=True)
        acc_ref[1:2, :] += jnp.sum(jnp.sum(e * e, axis=0, keepdims=True), axis=1, keepdims=True)
        dxn = dy * fw_ref[...]
        dx_ref[...] = r * (dxn - xn * jnp.mean(dxn * xn, axis=-1, keepdims=True))

    return pl.pallas_call(
        body, name=name, grid=(T // ts,),
        in_specs=[_row_spec(ts, D), _row_spec(ts, D), _full_spec((1, D))],
        out_specs=[_row_spec(ts, D), _full_spec((8, D))],
        out_shape=[jax.ShapeDtypeStruct((T, D), f32), jax.ShapeDtypeStruct((8, D), f32)],
        compiler_params=_params(("arbitrary",)))(x, tgt, fw)


def _tril(n):
    return lax.broadcasted_iota(jnp.int32, (n, n), 0) >= lax.broadcasted_iota(jnp.int32, (n, n), 1)


def _gm_chunk_fwd(u, v, z, lw, lb, ws_ref, bsx):
    gu, gv = _gelu(u), _gelu(v)
    mu = jnp.mean(gv, axis=-1, keepdims=True)
    cen = gv - mu
    rstd = lax.rsqrt(jnp.mean(cen * cen, axis=-1, keepdims=True) + EPS)
    vhat = cen * rstd
    vn = _bf(vhat * lw + lb)
    tri = _tril(Q)
    mixed = jnp.concatenate(
        [_dot(_bf(jnp.where(tri, ws_ref[g], 0.0)), vn[:, g * Q:(g + 1) * Q]) for g in range(NG)], axis=1) + bsx
    return gu, vhat, rstd, vn, mixed


def _gmlp_fwd(pA, lw, lb, ws, bsx, name):
    T = pA.shape[0]
    ts = _tile(T, 512)

    def body(u_ref, v_ref, z_ref, lw_ref, lb_ref, ws_ref, bsx_ref, y_ref):
        def chunk(ci, carry):
            rows = pl.ds(pl.multiple_of(ci * Q, Q), Q)
            gu, _, _, _, mixed = _gm_chunk_fwd(u_ref[rows, :], v_ref[rows, :], z_ref[rows, :], lw_ref[...], lb_ref[...],
                                               ws_ref, bsx_ref[...])
            y_ref[rows, :] = (gu * mixed * _silu(z_ref[rows, :])).astype(y_ref.dtype)
            return carry

        lax.fori_loop(0, ts // Q, chunk, 0)

    return pl.pallas_call(
        body, name=name, grid=(T // ts,),
        in_specs=[_row_spec(ts, D, 0), _row_spec(ts, D, 1), _row_spec(ts, D, 2), _full_spec((1, D)), _full_spec((1, D)),
                  _full_spec((NG, Q, Q)), _full_spec((Q, D))],
        out_specs=_row_spec(ts, D), out_shape=jax.ShapeDtypeStruct((T, D), _MXU),
        compiler_params=_params(("arbitrary",)))(pA, pA, pA, lw, lb, ws, bsx)


def _gmlp_bwd(dya, pA, lw, lb, ws, bsx, name):
    T = pA.shape[0]
    ts = _tile(T, 512)

    def body(dy_ref, u_ref, v_ref, z_ref, lw_ref, lb_ref, ws_ref, bsx_ref, dp_ref, acc_ref, dws_ref, dbs_ref):
        @pl.when(pl.program_id(0) == 0)
        def _():
            acc_ref[...] = jnp.zeros_like(acc_ref)
            dws_ref[...] = jnp.zeros_like(dws_ref)
            dbs_ref[...] = jnp.zeros_like(dbs_ref)

        tri = _tril(Q)

        def chunk(ci, carry):
            rows = pl.ds(pl.multiple_of(ci * Q, Q), Q)
            u, v, z, dy = u_ref[rows, :], v_ref[rows, :], z_ref[rows, :], dy_ref[rows, :]
            gu, vhat, rstd, vn, mixed = _gm_chunk_fwd(u, v, z, lw_ref[...], lb_ref[...], ws_ref, bsx_ref[...])
            sz = _silu(z)
            dp_ref[rows, 0:D] = (dy * mixed * sz * _dgelu(u)).astype(dp_ref.dtype)
            dp_ref[rows, 2 * D:3 * D] = (dy * gu * mixed * _dsilu(z)).astype(dp_ref.dtype)
            dmixed = dy * gu * sz
            dbs_ref[...] += dmixed
            dmb = _bf(dmixed)
            dvn_parts = []
            for g in range(NG):
                cols = slice(g * Q, (g + 1) * Q)
                wg = _bf(jnp.where(tri, ws_ref[g], 0.0))
                dvn_parts.append(_dot_tn(wg, dmb[:, cols]))
                dws_ref[g] += jnp.where(tri, _dot_nt(dmb[:, cols], vn[:, cols]), 0.0)
            dvn = jnp.concatenate(dvn_parts, axis=1)
            acc_ref[0:1, :] += jnp.sum(dvn * vhat, axis=0, keepdims=True)
            acc_ref[1:2, :] += jnp.sum(dvn, axis=0, keepdims=True)
            dvh = dvn * lw_ref[...]
            dgv = rstd * (dvh - jnp.mean(dvh, axis=-1, keepdims=True) - vhat * jnp.mean(dvh * vhat, axis=-1, keepdims=True))
            dp_ref[rows, D:2 * D] = (dgv * _dgelu(v)).astype(dp_ref.dtype)
            return carry

        lax.fori_loop(0, ts // Q, chunk, 0)

    return pl.pallas_call(
        body, name=name, grid=(T // ts,),
        in_specs=[_row_spec(ts, D), _row_spec(ts, D, 0), _row_spec(ts, D, 1), _row_spec(ts, D, 2), _full_spec((1, D)),
                  _full_spec((1, D)), _full_spec((NG, Q, Q)), _full_spec((Q, D))],
        out_specs=[_row_spec(ts, WA), _full_spec((8, D)), _full_spec((NG, Q, Q)), _full_spec((Q, D))],
        out_shape=[jax.ShapeDtypeStruct((T, WA), _MXU), jax.ShapeDtypeStruct((8, D), f32),
                   jax.ShapeDtypeStruct((NG, Q, Q), f32), jax.ShapeDtypeStruct((Q, D), f32)],
        compiler_params=_params(("arbitrary",)))(dya, pA, pA, pA, lw, lb, ws, bsx)


def _head_maps():
    h = lax.broadcasted_iota(jnp.int32, (128, DI), 0)
    ch = lax.broadcasted_iota(jnp.int32, (128, DI), 1)
    ex = (ch // P == h).astype(_MXU)
    return ex, ex.T


def _split(v, parts):
    out = []
    for _ in range(parts - 1):
        p = _bf(v)
        out.append(p)
        v = v - p.astype(f32)
    out.append(_bf(v))
    return out


def _expand(v, ex_ref, parts):
    acc = None
    for p in _split(v, parts):
        t = _dot(p, ex_ref[...])
        acc = t if acc is None else acc + t
    return acc


def _reduce(v, rd_ref):
    hi, lo = _split(v, 2)
    return _dot(hi, rd_ref[...]) + _dot(lo, rd_ref[...])


def _ssd_time(pb_ref, dtb_ref, alog_ref):
    xdt = pb_ref[:, OFF_DT:OFF_DT + 128] + dtb_ref[...]
    dt = _softplus(xdt)
    a = -jnp.exp(alog_ref[...])
    cs = _dot_exact(_tril(Q).astype(f32), dt * a)
    return xdt, dt, a, cs


def _head_mask(r):
    return lax.broadcasted_iota(jnp.int32, (Q, GRP), 1) // P == r


def _ssd_group_fwd(g, xa_ref, s_prev, cs, cs_t, dtx_ref, csx_ref, dsk_ref):
    cols = slice(g * GRP, (g + 1) * GRP)
    xs = xa_ref[:, cols]
    bb = _bf(xa_ref[:, DI + g * NS:DI + (g + 1) * NS])
    cb = _bf(xa_ref[:, DI + NG * NS + g * NS:DI + NG * NS + (g + 1) * NS])
    gm = _dot_nt(cb, bb)
    xd = xs * dtx_ref[:, cols]
    csx = csx_ref[:, cols]
    csl = csx_ref[Q - 1:Q, cols]
    tri = _tril(Q)
    lms = [jnp.exp(jnp.where(tri, cs[:, HPG * g + r:HPG * g + r + 1] - cs_t[HPG * g + r:HPG * g + r + 1, :], NEG))
           for r in range(HPG)]
    mfs = [gm * lm for lm in lms]
    mcat = jnp.concatenate([_bf(m) for m in mfs], axis=1)
    xbd = jnp.concatenate([_bf(jnp.where(_head_mask(r), xd, 0.0)) for r in range(HPG)], axis=0)
    ydiag = _dot(mcat, xbd)
    yoff = jnp.exp(csx) * _dot(cb, _bf(s_prev))
    y = ydiag + yoff + xs * dsk_ref[:, cols]
    xdd = xd * jnp.exp(csl - csx)
    s_new = s_prev * jnp.exp(csl) + _dot_tn(bb, _bf(xdd))
    return y, s_new, (xs, bb, cb, xd, lms, mfs, mcat, xbd, yoff, xdd, csx, csl)


def _ssd_fwd(pB, cw, cb, dtb, alog, dsk, snw, name, carry=()):
    T = pB.shape[0]
    nc = T // Q
    ex, _ = _head_maps()

    def body(pb_ref, cw_ref, cb_ref, dtb_ref, alog_ref, dsk_ref, snw_ref, ex_ref, y_ref, xc_ref, st_ref,
             s_ref, ext_ref, xa_ref, dtx_ref, csx_ref):
        @pl.when(pl.program_id(0) == 0)
        def _():
            s_ref[...] = jnp.zeros_like(s_ref)
            ext_ref[...] = jnp.zeros_like(ext_ref)

        ext_ref[8:8 + Q, :] = pb_ref[:, OFF_XBC:OFF_XBC + CD]
        for j in range(CD // 512):
            cj = slice(j * 512, (j + 1) * 512)
            e = ext_ref[:, cj]
            xc = cb_ref[:, cj] + cw_ref[KC - 1:KC, cj] * e[8:8 + Q]
            for s in range(1, KC):
                xc = xc + cw_ref[KC - 1 - s:KC - s, cj] * pltpu.roll(e, s, 0)[8:8 + Q]
            xc_ref[:, cj] = xc
            xa_ref[:, cj] = _silu(xc)
        ext_ref[0:8, :] = ext_ref[Q:Q + 8, :]

        _, dt, _, cs = _ssd_time(pb_ref, dtb_ref, alog_ref)
        cs_t = cs.T
        dtx_ref[...] = _expand(dt, ex_ref, 2)
        csx_ref[...] = _expand(cs, ex_ref, 3)
        for g in range(NG):
            s_prev = s_ref[g]
            st_ref[0, g] = s_prev
            y, s_new, _ = _ssd_group_fwd(g, xa_ref, s_prev, cs, cs_t, dtx_ref, csx_ref, dsk_ref)
            s_ref[g] = s_new
            cols = slice(g * GRP, (g + 1) * GRP)
            yz = y * _silu(pb_ref[:, cols])
            rr = lax.rsqrt(jnp.mean(yz * yz, axis=-1, keepdims=True) + EPS)
            y_ref[:, cols] = (yz * rr * snw_ref[:, cols]).astype(y_ref.dtype)

    outs, exchanged = _carry_call(
        body, name, (nc,),
        [_row_spec(Q, WB), _full_spec((8, CD)), _full_spec((1, CD)), _full_spec((1, 128)), _full_spec((1, 128)),
         _full_spec((1, DI)), _full_spec((1, DI)), _full_spec((128, DI))],
        [_row_spec(Q, DI), _row_spec(Q, CD), pl.BlockSpec((1, NG, NS, GRP), lambda i: (i, 0, 0, 0))],
        [jax.ShapeDtypeStruct((T, DI), _MXU), jax.ShapeDtypeStruct((T, CD), f32), jax.ShapeDtypeStruct((nc, NG, NS, GRP), f32)],
        [pltpu.VMEM((NG, NS, GRP), f32), pltpu.VMEM((Q + 8, CD), f32), pltpu.VMEM((Q, CD), f32),
         pltpu.VMEM((Q, DI), f32), pltpu.VMEM((Q, DI), f32)],
        (pB, cw, cb, dtb, alog, dsk, snw, ex), carry)
    return outs + [exchanged]


def _ssd_bwd(dyb, pB, xc, states, cw, dtb, alog, dsk, snw, name, carry=()):
    T = pB.shape[0]
    nc = T // Q
    ex, rd = _head_maps()
    selr = (lax.broadcasted_iota(jnp.int32, (HPG * Q, 128), 0) // Q == lax.broadcasted_iota(jnp.int32, (HPG * Q, 128), 1)).astype(_MXU)

    def body(dy_ref, pb_ref, xc_ref, st_ref, cw_ref, dtb_ref, alog_ref, dsk_ref, snw_ref, ex_ref, rd_ref, selr_ref,
             dp_ref, dcw_ref, dcb_ref, dhd_ref, dcol_ref,
             ds_ref, xa_ref, dxa_ref, dxe_ref, dtx_ref, csx_ref, rcs_ref, rdt_ref, rl_ref, dcs_ref, dcst_ref):
        @pl.when(pl.program_id(0) == 0)
        def _():
            ds_ref[...] = jnp.zeros_like(ds_ref)
            dxe_ref[...] = jnp.zeros_like(dxe_ref)
            dcw_ref[...] = jnp.zeros_like(dcw_ref)
            dcb_ref[...] = jnp.zeros_like(dcb_ref)
            dhd_ref[...] = jnp.zeros_like(dhd_ref)
            dcol_ref[...] = jnp.zeros_like(dcol_ref)
            rl_ref[...] = jnp.zeros_like(rl_ref)
            dcst_ref[...] = jnp.zeros_like(dcst_ref)

        dcs_ref[...] = jnp.zeros_like(dcs_ref)
        for j in range(CD // 512):
            cj = slice(j * 512, (j + 1) * 512)
            xa_ref[:, cj] = _silu(xc_ref[:, cj])
        xdt, dt, a, cs = _ssd_time(pb_ref, dtb_ref, alog_ref)
        cs_t = cs.T
        dtx_ref[...] = _expand(dt, ex_ref, 2)
        csx_ref[...] = _expand(cs, ex_ref, 3)

        for g in range(NG):
            cols = slice(g * GRP, (g + 1) * GRP)
            s_prev = st_ref[0, g]
            sb = _bf(s_prev)
            y, _, (xs, bb, cbm, xd, lms, mfs, mcat, xbd, yoff, xdd, csx, csl) = _ssd_group_fwd(
                g, xa_ref, s_prev, cs, cs_t, dtx_ref, csx_ref, dsk_ref)
            z = pb_ref[:, cols]
            sz = _silu(z)
            yz = y * sz
            rr = lax.rsqrt(jnp.mean(yz * yz, axis=-1, keepdims=True) + EPS)
            nrm = yz * rr
            dyb_g = dy_ref[:, cols]
            dcol_ref[1:2, cols] += jnp.sum(dyb_g * nrm, axis=0, keepdims=True)
            dn = dyb_g * snw_ref[:, cols]
            dyz = rr * (dn - nrm * jnp.mean(dn * nrm, axis=-1, keepdims=True))
            dyv = dyz * sz
            dp_ref[:, cols] = (dyz * y * _dsilu(z)).astype(dp_ref.dtype)
            dcol_ref[0:1, cols] += jnp.sum(dyv * xs, axis=0, keepdims=True)
            dy16 = _bf(dyv)
            dmcat = _dot_nt(dy16, xbd)
            dg = dmcat[:, 0:Q] * lms[0]
            for r in range(1, HPG):
                dg = dg + dmcat[:, r * Q:(r + 1) * Q] * lms[r]
            e_hi, e_lo = _split(jnp.concatenate([dmcat[:, r * Q:(r + 1) * Q] * mfs[r] for r in range(HPG)], axis=1), 2)
            rows = _dot(e_hi, selr_ref[...]) + _dot(e_lo, selr_ref[...])
            dcs_ref[...] += rows if g == 0 else pltpu.roll(rows, HPG * g, 1)
            ones8 = jnp.ones((8, Q), _MXU)
            csum = _dot(ones8, e_hi) + _dot(ones8, e_lo)
            for r in range(HPG):
                dcst_ref[HPG * g + r:HPG * g + r + 1, :] = csum[0:1, r * Q:(r + 1) * Q]
            big = _dot_tn(mcat, dy16)
            dxd_diag = jnp.where(_head_mask(0), big[0:Q], 0.0)
            for r in range(1, HPG):
                dxd_diag = dxd_diag + jnp.where(_head_mask(r), big[r * Q:(r + 1) * Q], 0.0)
            dsn = ds_ref[g]
            dsn16 = _bf(dsn)
            dxdd = _dot(bb, dsn16)
            t_state = dxdd * xdd
            dxd = dxd_diag + dxdd * jnp.exp(csl - csx)
            rcs_ref[:, cols] = dyv * yoff - t_state
            rl_ref[0:1, cols] = (jnp.sum(t_state, axis=0, keepdims=True)
                                 + jnp.exp(csl) * jnp.sum(s_prev * dsn, axis=0, keepdims=True))
            rdt_ref[:, cols] = dxd * xs
            dxa_ref[:, cols] = dyv * dsk_ref[:, cols] + dxd * dtx_ref[:, cols]
            dg16 = _bf(dg)
            dw16 = _bf(dyv * jnp.exp(csx))
            dxa_ref[:, DI + NG * NS + g * NS:DI + NG * NS + (g + 1) * NS] = _dot(dg16, bb) + _dot_nt(dw16, sb)
            dxa_ref[:, DI + g * NS:DI + (g + 1) * NS] = _dot_tn(dg16, cbm) + _dot_nt(_bf(xdd), dsn16)
            ds_ref[g] = dsn * jnp.exp(csl) + _dot_tn(cbm, dw16)

        row = lax.broadcasted_iota(jnp.int32, (Q, 128), 0)
        dcs = (dcs_ref[...] - dcst_ref[...].T + _reduce(rcs_ref[...], rd_ref)
               + jnp.where(row == Q - 1, _reduce(rl_ref[...], rd_ref)[0:1, :], 0.0))
        upper = lax.broadcasted_iota(jnp.int32, (Q, Q), 0) <= lax.broadcasted_iota(jnp.int32, (Q, Q), 1)
        dadt = _dot_exact(upper.astype(f32), dcs)
        valid = lax.broadcasted_iota(jnp.int32, (Q, 128), 1) < NH
        ddt = jnp.where(valid, _reduce(rdt_ref[...], rd_ref) + dadt * a, 0.0)
        ddtr = ddt * _sigmoid(xdt)
        dhd_ref[0:1, :] += jnp.sum(ddtr, axis=0, keepdims=True)
        dhd_ref[1:2, :] += jnp.sum(jnp.where(valid, dadt * dt * a, 0.0), axis=0, keepdims=True)
        dp_ref[:, OFF_DT:OFF_DT + 128] = ddtr.astype(dp_ref.dtype)
        dp_ref[:, OFF_DT + 128:WB] = jnp.zeros((Q, WB - OFF_DT - 128), dp_ref.dtype)
        for j in range(CD // 512):
            cj = slice(j * 512, (j + 1) * 512)
            dxc = dxa_ref[:, cj] * _dsilu(xc_ref[:, cj])
            dxe_ref[0:Q, cj] = dxc
            dcb_ref[0:1, cj] += jnp.sum(dxc, axis=0, keepdims=True)
            raw = pb_ref[:, OFF_XBC + j * 512:OFF_XBC + (j + 1) * 512]
            e = dxe_ref[:, cj]
            dcw_ref[KC - 1:KC, cj] += jnp.sum(dxc * raw, axis=0, keepdims=True)
            dxb = cw_ref[KC - 1:KC, cj] * dxc
            for s in range(1, KC):
                sh = pltpu.roll(e, Q + 8 - s, 0)[0:Q]
                dcw_ref[KC - 1 - s:KC - s, cj] += jnp.sum(sh * raw, axis=0, keepdims=True)
                dxb = dxb + cw_ref[KC - 1 - s:KC - s, cj] * sh
            dp_ref[:, OFF_XBC + j * 512:OFF_XBC + (j + 1) * 512] = dxb.astype(dp_ref.dtype)
        dxe_ref[Q:Q + 8, :] = dxe_ref[0:8, :]

    rev = lambda i: (nc - 1 - i, 0)
    outs, exchanged = _carry_call(
        body, name, (nc,),
        [pl.BlockSpec((Q, DI), rev), pl.BlockSpec((Q, WB), rev), pl.BlockSpec((Q, CD), rev),
         pl.BlockSpec((1, NG, NS, GRP), lambda i: (nc - 1 - i, 0, 0, 0)),
         _full_spec((8, CD)), _full_spec((1, 128)), _full_spec((1, 128)),
         _full_spec((1, DI)), _full_spec((1, DI)), _full_spec((128, DI)), _full_spec((DI, 128)), _full_spec((HPG * Q, 128))],
        [pl.BlockSpec((Q, WB), rev), _full_spec((8, CD)), _full_spec((8, CD)), _full_spec((8, 128)), _full_spec((8, DI))],
        [jax.ShapeDtypeStruct((T, WB), _MXU), jax.ShapeDtypeStruct((8, CD), f32), jax.ShapeDtypeStruct((8, CD), f32),
         jax.ShapeDtypeStruct((8, 128), f32), jax.ShapeDtypeStruct((8, DI), f32)],
        [pltpu.VMEM((NG, NS, GRP), f32), pltpu.VMEM((Q, CD), f32), pltpu.VMEM((Q, CD), f32),
         pltpu.VMEM((Q + 8, CD), f32), pltpu.VMEM((Q, DI), f32), pltpu.VMEM((Q, DI), f32),
         pltpu.VMEM((Q, DI), f32), pltpu.VMEM((Q, DI), f32), pltpu.VMEM((8, DI), f32),
         pltpu.VMEM((Q, 128), f32), pltpu.VMEM((128, Q), f32)],
        (dyb, pB, xc, states, cw, dtb, alog, dsk, snw, ex, rd, selr), carry)
    return outs + [exchanged]


def _merge_fwd(ya, yb, pG, x, gate, wa, wb, wo, name, carry=()):
    T = x.shape[0]
    ts = _tile(T, 256)

    def body(ya_ref, yb_ref, g_ref, x_ref, gate_ref, wa_ref, wb_ref, wo_ref, xo_ref, mg_ref, pa_ref, pb_ref):
        pa = _dot(ya_ref[...], wa_ref[...])
        pb = _dot(yb_ref[...], wb_ref[...])
        merged = _sigmoid(g_ref[:, 0:D]) * pa + _sigmoid(g_ref[:, D:2 * D]) * pb
        mg = _bf(merged)
        xo_ref[...] = x_ref[...] + gate_ref[...] * _dot(mg, wo_ref[...])
        mg_ref[...] = mg
        pa_ref[...] = pa
        pb_ref[...] = pb

    outs, exchanged = _carry_call(
        body, name, (T // ts,),
        [_row_spec(ts, D), _row_spec(ts, DI), _row_spec(ts, WG), _row_spec(ts, D), _full_spec((1, D)),
         _full_spec((D, D)), _full_spec((DI, D)), _full_spec((D, D))],
        [_row_spec(ts, D)] * 4,
        [jax.ShapeDtypeStruct((T, D), f32), jax.ShapeDtypeStruct((T, D), _MXU), jax.ShapeDtypeStruct((T, D), f32),
         jax.ShapeDtypeStruct((T, D), f32)],
        [], (ya, yb, pG, x, gate, wa, wb, wo), carry)
    return outs + [exchanged]


def _merge_bwd(dxout, merged, pa, pb, pG, gate, wo, name):
    T = dxout.shape[0]
    ts = _tile(T, 256)

    def body(dx_ref, mg_ref, pa_ref, pb_ref, g_ref, gate_ref, wo_ref, do_ref, dpa_ref, dpb_ref, dg_ref, acc_ref):
        @pl.when(pl.program_id(0) == 0)
        def _():
            acc_ref[...] = jnp.zeros_like(acc_ref)

        dxo = dx_ref[...]
        acc_ref[0:1, :] += jnp.sum(dxo * _dot(mg_ref[...], wo_ref[...]), axis=0, keepdims=True)
        do = _bf(dxo * gate_ref[...])
        do_ref[...] = do
        dmerged = _dot_nt(do, wo_ref[...])
        sa, sb = _sigmoid(g_ref[:, 0:D]), _sigmoid(g_ref[:, D:2 * D])
        dpa_ref[...] = (dmerged * sa).astype(dpa_ref.dtype)
        dpb_ref[...] = (dmerged * sb).astype(dpb_ref.dtype)
        dg_ref[:, 0:D] = (dmerged * pa_ref[...] * sa * (1.0 - sa)).astype(dg_ref.dtype)
        dg_ref[:, D:2 * D] = (dmerged * pb_ref[...] * sb * (1.0 - sb)).astype(dg_ref.dtype)

    return pl.pallas_call(
        body, name=name, grid=(T // ts,),
        in_specs=[_row_spec(ts, D), _row_spec(ts, D), _row_spec(ts, D), _row_spec(ts, D), _row_spec(ts, WG),
                  _full_spec((1, D)), _full_spec((D, D))],
        out_specs=[_row_spec(ts, D), _row_spec(ts, D), _row_spec(ts, D), _row_spec(ts, WG), _full_spec((8, D))],
        out_shape=[jax.ShapeDtypeStruct((T, D), _MXU), jax.ShapeDtypeStruct((T, D), _MXU), jax.ShapeDtypeStruct((T, D), _MXU),
                   jax.ShapeDtypeStruct((T, WG), _MXU), jax.ShapeDtypeStruct((8, D), f32)],
        compiler_params=_params(("arbitrary",)))(dxout, merged, pa, pb, pG, gate, wo)


def _layer_fwd(x, mod, lp, tag, carry=None):
    carry = carry or {}
    got = {}
    h = _modulate(x, lp["nw"], mod[0:1], mod[1:2], f"modulate_{tag}")
    pA = _mm(h, lp["w_gm"], "nn", f"proj_gm_{tag}")
    if carry.get("proj_ssd"):
        pB, got["proj_ssd"] = _mm(h, lp["w_ssd"], "nn", f"proj_ssd_{tag}", carry=carry["proj_ssd"])
    else:
        pB = _mm(h, lp["w_ssd"], "nn", f"proj_ssd_{tag}")
    pG = _mm(h, lp["w_g"], "nn", f"proj_gate_{tag}")
    ya = _gmlp_fwd(pA, lp["lw"], lp["lb"], lp["ws"], lp["bsx"], f"gmlp_fwd_{tag}")
    yb, xc, states, got["ssd_fwd"] = _ssd_fwd(pB, lp["cw"], lp["cb"], lp["dtb"], lp["alog"], lp["dsk"], lp["snw"], f"ssd_fwd_{tag}",
                                              carry.get("ssd_fwd", ()))
    xo, merged, pa, pb, got["merge_fwd"] = _merge_fwd(ya, yb, pG, x, mod[2:3], lp["wa"], lp["wb"], lp["wo"], f"merge_fwd_{tag}",
                                                      carry.get("merge_fwd", ()))
    return xo, dict(x=x, h=h, pA=pA, pB=pB, pG=pG, ya=ya, yb=yb, xc=xc, states=states, merged=merged, pa=pa, pb=pb), got


def _dh_modulate_bwd(dpG, dpA, dpB, w_g, w_gm, w_ssd, dxout, x, nw, scale, name, carry=()):
    T = x.shape[0]
    tm = _tile(T, 512)
    fam = [(WG, _tile(WG, 1280)), (WA, _tile(WA, 1280)), (WB, _tile(WB, 1280))]
    steps = [w // k for w, k in fam]
    first = [0, steps[0], steps[0] + steps[1]]
    ns = sum(steps)

    def chunk(f):
        return lambda i, s: jnp.clip(s - first[f], 0, steps[f] - 1)

    a_specs = [pl.BlockSpec((tm, fam[f][1]), lambda i, s, c=chunk(f): (i, c(i, s))) for f in range(3)]
    b_specs = [pl.BlockSpec((D, fam[f][1]), lambda i, s, c=chunk(f): (0, c(i, s))) for f in range(3)]
    tok = pl.BlockSpec((tm, D), lambda i, s: (i, 0))
    vec = pl.BlockSpec((1, D), lambda i, s: (0, 0))

    def body(ag, aa, ab, bg, ba, bb, dxo_ref, x_ref, nw_ref, sc_ref, dx_ref, sum_ref, acc_ref):
        i, s = pl.program_id(0), pl.program_id(1)

        @pl.when(jnp.logical_and(i == 0, s == 0))
        def _():
            sum_ref[...] = jnp.zeros_like(sum_ref)

        @pl.when(s == 0)
        def _():
            acc_ref[...] = jnp.zeros_like(acc_ref)

        @pl.when(s < first[1])
        def _():
            acc_ref[...] += _dot_nt(ag[...], bg[...])

        @pl.when(jnp.logical_and(s >= first[1], s < first[2]))
        def _():
            acc_ref[...] += _dot_nt(aa[...], ba[...])

        @pl.when(s >= first[2])
        def _():
            acc_ref[...] += _dot_nt(ab[...], bb[...])

        @pl.when(s == ns - 1)
        def _():
            xv, dh_v = x_ref[...], acc_ref[...]
            r = lax.rsqrt(jnp.mean(xv * xv, axis=-1, keepdims=True) + EPS)
            xn = xv * r
            hn = xn * nw_ref[...]
            dhn = dh_v * (1.0 + sc_ref[...])
            sum_ref[0:1, :] += jnp.sum(dh_v, axis=0, keepdims=True)
            sum_ref[1:2, :] += jnp.sum(dh_v * hn, axis=0, keepdims=True)
            sum_ref[2:3, :] += jnp.sum(dhn * xn, axis=0, keepdims=True)
            dxn = dhn * nw_ref[...]
            dx_ref[...] = dxo_ref[...] + r * (dxn - xn * jnp.mean(dxn * xn, axis=-1, keepdims=True))

    outs, exchanged = _carry_call(
        body, name, (T // tm, ns), a_specs + b_specs + [tok, tok, vec, vec], [tok, pl.BlockSpec((8, D), lambda i, s: (0, 0))],
        [jax.ShapeDtypeStruct((T, D), f32), jax.ShapeDtypeStruct((8, D), f32)], [pltpu.VMEM((tm, D), f32)],
        (dpG, dpA, dpB, w_g, w_gm, w_ssd, dxout, x, nw, scale), carry)
    return outs + [exchanged]


def _win_blocks(g):
    full = jnp.concatenate([g["w_gm"], g["w_ssd"][:, 0:DI + CD], g["w_ssd"][:, DI + CD:DI + CD + NH], g["w_g"]], axis=1)
    return jnp.transpose(full.reshape(D, NDEV, NIN // NDEV), (1, 0, 2))


def _row_blocks(g):
    return jnp.concatenate([g["wa"].reshape(NDEV, D // NDEV, D), g["wb"].reshape(NDEV, DI // NDEV, D),
                            g["wo"].reshape(NDEV, D // NDEV, D)], axis=1)


def _layer_bwd(dxo, sv, mod, lp, tag, carry=(), scatter_own=False):
    do, dpa, dpb, dpG, s_gate = _merge_bwd(dxo, sv["merged"], sv["pa"], sv["pb"], sv["pG"], mod[2:3], lp["wo"], f"merge_bwd_{tag}")
    g = {}
    g["wo"] = _mm(sv["merged"], do, "tn", f"dw_out_{tag}", out_dtype=_WIRE)
    g["wa"] = _mm(sv["ya"], dpa, "tn", f"dw_proj_a_{tag}", out_dtype=_WIRE)
    g["wb"] = _mm(sv["yb"], dpb, "tn", f"dw_proj_b_{tag}", out_dtype=_WIRE)
    dya = _mm(dpa, lp["wa"], "nt", f"dy_a_{tag}")
    dyb = _mm(dpb, lp["wb"], "nt", f"dy_b_{tag}")
    dpA, s_ln, g["ws"], dbsx = _gmlp_bwd(dya, sv["pA"], lp["lw"], lp["lb"], lp["ws"], lp["bsx"], f"gmlp_bwd_{tag}")
    carry = list(carry) + ([(_row_blocks(g), False)] if scatter_own else [])
    dpB, g["cw"], s_cb, s_hd, s_col, got = _ssd_bwd(dyb, sv["pB"], sv["xc"], sv["states"], lp["cw"], lp["dtb"], lp["alog"],
                                                    lp["dsk"], lp["snw"], f"ssd_bwd_{tag}", carry)
    g["w_g"] = _mm(sv["h"], dpG, "tn", f"dw_gate_{tag}", out_dtype=_WIRE)
    g["w_gm"] = _mm(sv["h"], dpA, "tn", f"dw_gm_{tag}", out_dtype=_WIRE)
    g["w_ssd"] = _mm(sv["h"], dpB, "tn", f"dw_ssd_{tag}", out_dtype=_WIRE)
    dx, s_mod, got_win = _dh_modulate_bwd(dpG, dpA, dpB, lp["w_g"], lp["w_gm"], lp["w_ssd"], dxo, sv["x"], lp["nw"], mod[1:2],
                                          f"dh_{tag}", [(_win_blocks(g), False)] if scatter_own else ())
    if scatter_own:
        got, got_own = got[:-1], [got[-1], got_win[0]]
    else:
        got_own = []
    g["mod"] = jnp.concatenate([s_mod[0], s_mod[1], s_gate[0]])
    g["nw"] = s_mod[2]
    g["lw"], g["lb"] = s_ln[0], s_ln[1]
    g["bs"] = dbsx.reshape(Q, NG, Q).sum(-1).T
    g["cb"] = s_cb[0]
    g["dtb"], g["alog"] = s_hd[0, :NH], s_hd[1, :NH]
    g["dsk"] = s_col[0].reshape(NH, P).sum(-1)
    g["snw"] = s_col[1]
    g["cw"] = g["cw"][0:KC]
    return dx, g, got, got_own


def _prep_layer(nw, w_in_full, lw, lb, ws, bs, cw_full, cb, dtb, alog, dsk, snw, wa, wb, wo):
    z = jnp.zeros((D, WB - (DI + CD + NH)), w_in_full.dtype)
    pad_h = lambda v: jnp.pad(v, (0, 128 - NH)).reshape(1, 128)
    return dict(
        nw=nw.reshape(1, D),
        w_gm=w_in_full[:, 0:WA],
        w_ssd=jnp.concatenate([w_in_full[:, WA:WA + DI + CD], w_in_full[:, WA + DI + CD:WA + DI + CD + NH], z], axis=1),
        w_g=w_in_full[:, WA + DI + CD + NH:NIN],
        lw=lw.reshape(1, D), lb=lb.reshape(1, D), ws=ws, bsx=jnp.repeat(bs.T, Q, axis=1),
        cw=jnp.pad(cw_full, ((0, 8 - KC), (0, 0))), cb=cb.reshape(1, CD), dtb=pad_h(dtb), alog=pad_h(alog),
        dsk=jnp.repeat(dsk, P).reshape(1, DI), snw=snw.reshape(1, DI), wa=wa, wb=wb, wo=wo)


def _exchange(src, gather, name):
    if not gather:
        assert src.shape[0] == NDEV

    def body(src_ref, out_ref, send_sems, recv_sems, local_sem):
        start, finish = _exch_ops(src_ref, out_ref, send_sems, recv_sems, local_sem, gather)
        start()
        finish()

    return pl.pallas_call(
        body, name=name, out_shape=_exch_shape(src, gather),
        in_specs=[pl.BlockSpec(memory_space=pl.ANY)], out_specs=pl.BlockSpec(memory_space=pl.ANY), scratch_shapes=_EXCH_SEMS)(src)


def _gather2(src, name):
    def body(src_ref, out_ref, send_sems, recv_sems, local_sem):
        x, y, c = lax.axis_index("x"), lax.axis_index("y"), lax.axis_index("c")
        me, sibling = (x, y, c), (x, y, 1 - c)
        chips = [(1 - x, y), (x, 1 - y), (1 - x, 1 - y)]

        def slot(px, py, pc):
            return out_ref.at[4 * px + 2 * py + pc]

        def copy(k, block, to, src=None):
            return pltpu.make_async_remote_copy(
                src_ref=slot(*block) if src is None else src, dst_ref=slot(*block), send_sem=send_sems.at[k],
                recv_sem=recv_sems.at[k], device_id=to, device_id_type=pl.DeviceIdType.MESH)

        mine = pltpu.make_async_copy(src_ref, slot(*me), local_sem)
        mine.start()
        first = [copy(0, me, sibling, src=src_ref)] + [copy(1 + j, me, (*chip, c), src=src_ref) for j, chip in enumerate(chips)]
        for cp in first:
            cp.start()
        passed = [copy(4 + j, (*chip, c), sibling) for j, chip in enumerate(chips)]
        for j, chip in enumerate(chips):
            copy(1 + j, (*chip, c), me).wait_recv()
            passed[j].start()
        copy(0, sibling, me).wait_recv()
        for j, chip in enumerate(chips):
            copy(4 + j, (*chip, 1 - c), me).wait_recv()
        for cp in first + passed:
            cp.wait_send()
        mine.wait()

    return pl.pallas_call(
        body, name=name, out_shape=_exch_shape(src, True),
        in_specs=[pl.BlockSpec(memory_space=pl.ANY)], out_specs=pl.BlockSpec(memory_space=pl.ANY), scratch_shapes=_EXCH_SEMS)(src)


def _mod_dist(c8, ada_w, ada_b_cols, name):
    L, _, AW = ada_w.shape

    def body(c_ref, w_ref, b_ref, parts_ref, sc_ref, call_ref, mine_ref, send1, recv1, send2, recv2):
        x, y, c = lax.axis_index("x"), lax.axis_index("y"), lax.axis_index("c")
        me = 4 * x + 2 * y + c

        def peer(j):
            px = 1 - x if (j >> 2) & 1 else x
            py = 1 - y if (j >> 1) & 1 else y
            pc = 1 - c if j & 1 else c
            return (px, py, pc), 4 * px + 2 * py + pc

        def copy(j, src, dst, sems, landing):
            dev, idx = peer(j)
            return pltpu.make_async_remote_copy(
                src_ref=src, dst_ref=dst.at[idx] if landing else dst.at[me], send_sem=sems[0].at[j - 1], recv_sem=sems[1].at[j - 1],
                device_id=dev, device_id_type=pl.DeviceIdType.MESH)

        def all_to_all(src, dst, sems):
            for j in range(1, NDEV):
                copy(j, src, dst, sems, False).start()
            for j in range(1, NDEV):
                copy(j, src, dst, sems, True).wait_recv()
            for j in range(1, NDEV):
                copy(j, src, dst, sems, False).wait_send()

        call_ref[me] = c_ref[...]
        all_to_all(c_ref, call_ref, (send1, recv1))
        row = lax.broadcasted_iota(jnp.int32, (8, D), 0)
        cm = jnp.zeros((8, D), f32)
        for k in range(NDEV):
            cm = jnp.where(row == k, call_ref[k], cm)
        sc = _silu(cm)
        sc_ref[...] = sc
        for l in range(L):
            mine_ref[l] = _dot(_bf(sc), w_ref[l]) + b_ref[l]
        parts_ref[me] = mine_ref[...]
        all_to_all(mine_ref, parts_ref, (send2, recv2))

    vmem = pl.BlockSpec(memory_space=pltpu.VMEM)
    sems = pltpu.SemaphoreType.DMA((NDEV - 1,))
    return pl.pallas_call(
        body, name=name, in_specs=[vmem, vmem, vmem], out_specs=[vmem, vmem],
        out_shape=[jax.ShapeDtypeStruct((NDEV, L, 8, AW), f32), jax.ShapeDtypeStruct((8, D), f32)],
        scratch_shapes=[pltpu.VMEM((NDEV, 8, D), f32), pltpu.VMEM((L, 8, AW), f32), sems, sems, sems, sems])(c8, ada_w, ada_b_cols)


def _ada_w_grad(sc_all, dmod_cols, name):
    W = dmod_cols.shape[1]

    def body(s_ref, d_ref, o_ref):
        o_ref[...] = lax.dot_general(s_ref[...], d_ref[...], (((0,), (0,)), ((), ())), preferred_element_type=f32,
                                     precision=lax.Precision.HIGHEST)

    return pl.pallas_call(body, name=name, out_shape=jax.ShapeDtypeStruct((D, W), f32))(sc_all, dmod_cols)


def _adamw_math(w, g, m, v):
    m = ADAM_B1 * m + (1.0 - ADAM_B1) * g
    v = ADAM_B2 * v + (1.0 - ADAM_B2) * (g * g)
    m_hat = m / (1.0 - ADAM_B1 ** ADAM_STEP)
    v_hat = v / (1.0 - ADAM_B2 ** ADAM_STEP)
    delta = -ADAM_LR * (m_hat / (jnp.sqrt(v_hat) + ADAM_EPS) + ADAM_WD * w)
    return delta, m, v


def _sum_adamw(recv, w, m, v, name):
    n, R, C = recv.shape
    tr = _tile(R, 128 if C > 1024 else 256)

    def body(r_ref, w_ref, m_ref, v_ref, g_ref, d_ref, nm_ref, nv_ref):
        g = r_ref[0].astype(f32)
        for k in range(1, n):
            g = g + r_ref[k].astype(f32)
        g_ref[...] = g
        d_ref[...], nm_ref[...], nv_ref[...] = _adamw_math(w_ref[...], g, m_ref[...], v_ref[...])

    spec = pl.BlockSpec((tr, C), lambda i: (i, 0))
    return pl.pallas_call(
        body, name=name, grid=(R // tr,),
        in_specs=[pl.BlockSpec((n, tr, C), lambda i: (0, i, 0)), spec, spec, spec], out_specs=[spec] * 4,
        out_shape=[jax.ShapeDtypeStruct((R, C), f32)] * 4, compiler_params=_params(("arbitrary",)))(recv, w, m, v)


def _pack(arrays, rows):
    flat = []
    for a in arrays:
        a = a.reshape(-1).astype(f32)
        flat.append(jnp.pad(a, (0, (-a.shape[0]) % 128)))
    flat = jnp.concatenate(flat)
    return jnp.pad(flat, (0, rows * 128 - flat.shape[0])).reshape(rows, 128)


def _unpack(slab, shapes):
    flat = slab.reshape(-1)
    out, off = [], 0
    for s in shapes:
        n = 1
        for d in s:
            n *= d
        out.append(flat[off:off + n].reshape(s))
        off += n + (-n) % 128
    return out


def kernel(x, c, ada_w, ada_b, norm_w, w_in, gm_ln_w, gm_ln_b, gm_ws, gm_bs, conv_w, conv_b, dt_bias, a_log, d_skip, ssm_norm_w, w_proj_a, w_proj_b, w_out, final_norm_w, loss_target, m_ada_w, m_ada_b, m_norm_w, m_w_in, m_gm_ln_w, m_gm_ln_b, m_gm_ws, m_gm_bs, m_conv_w, m_conv_b, m_dt_bias, m_a_log, m_d_skip, m_ssm_norm_w, m_w_proj_a, m_w_proj_b, m_w_out, m_final_norm_w, v_ada_w, v_ada_b, v_norm_w, v_w_in, v_gm_ln_w, v_gm_ln_b, v_gm_ws, v_gm_bs, v_conv_w, v_conv_b, v_dt_bias, v_a_log, v_d_skip, v_ssm_norm_w, v_w_proj_a, v_w_proj_b, v_w_out, v_final_norm_w):
    L = 2
    me = 4 * lax.axis_index("x") + 2 * lax.axis_index("y") + lax.axis_index("c")
    W = dict(ada_w=ada_w, ada_b=ada_b, norm_w=norm_w, w_in=w_in, gm_ln_w=gm_ln_w, gm_ln_b=gm_ln_b, gm_ws=gm_ws, gm_bs=gm_bs,
             conv_w=conv_w, conv_b=conv_b, dt_bias=dt_bias, a_log=a_log, d_skip=d_skip, ssm_norm_w=ssm_norm_w, w_proj_a=w_proj_a,
             w_proj_b=w_proj_b, w_out=w_out, final_norm_w=final_norm_w)
    M = dict(ada_w=m_ada_w, ada_b=m_ada_b, norm_w=m_norm_w, w_in=m_w_in, gm_ln_w=m_gm_ln_w, gm_ln_b=m_gm_ln_b, gm_ws=m_gm_ws,
             gm_bs=m_gm_bs, conv_w=m_conv_w, conv_b=m_conv_b, dt_bias=m_dt_bias, a_log=m_a_log, d_skip=m_d_skip,
             ssm_norm_w=m_ssm_norm_w, w_proj_a=m_w_proj_a, w_proj_b=m_w_proj_b, w_out=m_w_out, final_norm_w=m_final_norm_w)
    V = dict(ada_w=v_ada_w, ada_b=v_ada_b, norm_w=v_norm_w, w_in=v_w_in, gm_ln_w=v_gm_ln_w, gm_ln_b=v_gm_ln_b, gm_ws=v_gm_ws,
             gm_bs=v_gm_bs, conv_w=v_conv_w, conv_b=v_conv_b, dt_bias=v_dt_bias, a_log=v_a_log, d_skip=v_d_skip,
             ssm_norm_w=v_ssm_norm_w, w_proj_a=v_w_proj_a, w_proj_b=v_w_proj_b, w_out=v_w_out, final_norm_w=v_final_norm_w)
    SW = NIN // NDEV
    AW = 3 * D // NDEV
    CW = CD // NDEV
    RA, RB = D // NDEV, DI // NDEV

    wire = lambda a: a.astype(_WIRE)
    rows_of = lambda d, l: jnp.concatenate([d["w_proj_a"][l], d["w_proj_b"][l], d["w_out"][l]], axis=0)
    g_win, g_rows = [None] * L, [None] * L
    g_win[0] = _gather2(wire(w_in[0]), "gather_w_in_l0")
    g_rows[0] = _gather2(wire(rows_of(W, 0)), "gather_w_rows_l0")
    g_cw = _exchange(conv_w.reshape(L * KC, CW), True, "gather_conv_w").reshape(NDEV, L, KC, CW)
    parts, sc_all = _mod_dist(jnp.broadcast_to(c, (8, D)), wire(ada_w),
                              lax.dynamic_slice_in_dim(ada_b, me * AW, AW, axis=1).reshape(L, 1, AW), "mod_dist")
    mods = [lax.dynamic_index_in_dim(parts[:, l], me, axis=1, keepdims=False).reshape(3, D) for l in range(L)]
    cols = lambda g: jnp.transpose(g, (1, 0, 2)).reshape(g.shape[1], -1)

    def prep(l):
        r = g_rows[l]
        return _prep_layer(norm_w[l], cols(g_win[l]), gm_ln_w[l], gm_ln_b[l], gm_ws[l], gm_bs[l], cols(g_cw[:, l]), conv_b[l],
                           dt_bias[l], a_log[l], d_skip[l], ssm_norm_w[l], r[:, 0:RA].reshape(D, D),
                           r[:, RA:RA + RB].reshape(DI, D), r[:, RA + RB:].reshape(D, D))

    half = D // 2
    lp0 = prep(0)
    h, sv0, got = _layer_fwd(x[0], mods[0], lp0, "l0", dict(proj_ssd=[(wire(w_in[1][:half]), True)],
                                                           ssd_fwd=[(wire(w_in[1][half:]), True)],
                                                           merge_fwd=[(wire(rows_of(W, 1)), True)]))
    g_win[1] = jnp.concatenate([got["proj_ssd"][0], got["ssd_fwd"][0]], axis=1)
    g_rows[1] = got["merge_fwd"][0]
    lp1 = prep(1)
    h, sv1, _ = _layer_fwd(h, mods[1], lp1, "l1")
    dh, s_fin = _final_loss(h, loss_target[0], final_norm_w.reshape(1, D), "final_loss")
    loss = lax.psum(0.5 / D * s_fin[1, 0], ("x", "y", "c"))

    small_l = ["ada_b", "norm_w", "gm_ln_w", "gm_ln_b", "gm_ws", "gm_bs", "conv_b", "dt_bias", "a_log", "d_skip", "ssm_norm_w"]
    small_key = dict(ada_b="mod", norm_w="nw", gm_ln_w="lw", gm_ln_b="lb", gm_ws="ws", gm_bs="bs", conv_b="cb", dt_bias="dtb",
                     a_log="alog", d_skip="dsk", ssm_norm_w="snw")
    shapes_l = [W[n].shape[1:] for n in small_l] + [(KC, CD)]
    shapes = [shapes_l, shapes_l + [(D,)]]
    n_rows = [-(-sum(-(-functools.reduce(lambda a, b: a * b, s, 1) // 128) for s in sh) // 256) * 256 for sh in shapes]
    slab_of = lambda g, l, tail: _pack([g[small_key[n]] for n in small_l] + [g["cw"]] + tail, n_rows[l])

    dh, g1, _, _ = _layer_bwd(dh, sv1, mods[1], lp1, "l1")
    dh, g0, got, got_own = _layer_bwd(dh, sv0, mods[0], lp0, "l0",
                                      [(_win_blocks(g1), False), (_row_blocks(g1), False), (slab_of(g1, 1, [s_fin[0]]), True)], True)
    grads = [g0, g1]
    grad_x = dh[None]
    recv_win = [got_own[1], got[0]]
    recv_rows = [got_own[0], got[1]]
    slab_all = [_exchange(slab_of(g0, 0, []), True, "gather_small_l0"), got[2]]

    o_win_l = [_sum_adamw(recv_win[l], w_in[l], m_w_in[l], v_w_in[l], f"adamw_w_in_l{l}") for l in range(L)]
    o_win = [jnp.stack([o_win_l[l][k] for l in range(L)]) for k in range(4)]
    o_rows_l = [_sum_adamw(recv_rows[l], rows_of(W, l), rows_of(M, l), rows_of(V, l), f"adamw_w_rows_l{l}") for l in range(L)]
    o_pa = [jnp.stack([o_rows_l[l][k][0:RA] for l in range(L)]) for k in range(4)]
    o_pb = [jnp.stack([o_rows_l[l][k][RA:RA + RB] for l in range(L)]) for k in range(4)]
    o_po = [jnp.stack([o_rows_l[l][k][RA + RB:] for l in range(L)]) for k in range(4)]

    def slab_params(d, l):
        tail = [jnp.zeros((KC, CD), f32)] + ([d["final_norm_w"]] if l == L - 1 else [])
        return _pack([d[n][l] for n in small_l] + tail, n_rows[l])

    o_small = [[_unpack(o, shapes[l]) for o in _sum_adamw(slab_all[l], slab_params(W, l), slab_params(M, l), slab_params(V, l),
                                                          f"adamw_small_l{l}")] for l in range(L)]
    small_out = {n: [jnp.stack([o_small[l][k][i] for l in range(L)]) for k in range(4)] for i, n in enumerate(small_l)}
    small_out["final_norm_w"] = [o_small[L - 1][k][len(small_l) + 1] for k in range(4)]

    g_cw_mine = jnp.stack([lax.dynamic_slice_in_dim(o_small[l][0][len(small_l)], me * CW, CW, axis=1) for l in range(L)])
    o_cw = _sum_adamw(g_cw_mine.reshape(1, L * KC, CW), conv_w.reshape(L * KC, CW), m_conv_w.reshape(L * KC, CW),
                      v_conv_w.reshape(L * KC, CW), "adamw_conv_w")
    o_cw = [o.reshape(L, KC, CW) for o in o_cw]

    dmod_all = jnp.stack([jnp.stack([_unpack(slab_all[l][k], shapes[l])[0] for l in range(L)]) for k in range(NDEV)])
    dmod_cols = lax.dynamic_slice_in_dim(dmod_all, me * AW, AW, axis=2).reshape(NDEV, L * AW)
    g_ada_w = jnp.transpose(_ada_w_grad(sc_all, dmod_cols, "ada_w_grad").reshape(D, L, AW), (1, 0, 2))
    o_ada = _sum_adamw(g_ada_w.reshape(1, L * D, AW), ada_w.reshape(L * D, AW), m_ada_w.reshape(L * D, AW),
                       v_ada_w.reshape(L * D, AW), "adamw_ada_w")
    o_ada = [o.reshape(L, D, AW) for o in o_ada]

    big = dict(ada_w=o_ada, w_in=o_win, conv_w=o_cw, w_proj_a=o_pa, w_proj_b=o_pb, w_out=o_po)
    order = ["ada_w", "ada_b", "norm_w", "w_in", "gm_ln_w", "gm_ln_b", "gm_ws", "gm_bs", "conv_w", "conv_b", "dt_bias", "a_log",
             "d_skip", "ssm_norm_w", "w_proj_a", "w_proj_b", "w_out", "final_norm_w"]
    pick = lambda n, k: big[n][k] if n in big else small_out[n][k]
    return (loss, grad_x, *[pick(n, 0) for n in order], *[pick(n, 1) for n in order], *[pick(n, 2) for n in order],
            *[pick(n, 3) for n in order])
```

```python
import functools

import jax
import jax.numpy as jnp
from jax import lax
from jax.experimental import pallas as pl
from jax.experimental.pallas import tpu as pltpu

f32 = jnp.float32
_MXU = jnp.bfloat16
_WIRE = jnp.bfloat16

D = 1024
Q = 128
NG = 8
DI = 2048
NH = 32
P = 64
HPG = 4
NS = 128
KC = 4
CD = 4096
GRP = DI // NG
EPS = 1e-6
NDEV = 8
WA = 3 * D
WB = DI + CD + 256
WG = 2 * D
OFF_XBC = DI
OFF_DT = DI + CD
NIN = 11296
VMEM_LIMIT = 56 * 1024 * 1024
NEG = -1e30

ADAM_LR, ADAM_B1, ADAM_B2, ADAM_EPS, ADAM_WD, ADAM_STEP = 0.001, 0.9, 0.999, 1e-08, 0.01, 10


def _bf(x):
    return x.astype(_MXU)


def _dot(a, b):
    return jnp.dot(a, b, preferred_element_type=f32)


def _dot_nt(a, b):
    return lax.dot_general(a, b, (((1,), (1,)), ((), ())), preferred_element_type=f32)


def _dot_tn(a, b):
    return lax.dot_general(a, b, (((0,), (0,)), ((), ())), preferred_element_type=f32)


def _dot_exact(a, b):
    return jnp.dot(a, b, preferred_element_type=f32, precision=lax.Precision.HIGHEST)


def _sigmoid(x):
    return jax.nn.sigmoid(x)


def _silu(x):
    return x * _sigmoid(x)


def _dsilu(x):
    s = _sigmoid(x)
    return s * (1.0 + x * (1.0 - s))


_GK = 0.7978845608028654
_GC = 0.044715


def _gelu(x):
    return 0.5 * x * (1.0 + jnp.tanh(_GK * (x + _GC * x * x * x)))


def _dgelu(x):
    t = jnp.tanh(_GK * (x + _GC * x * x * x))
    return 0.5 * (1.0 + t) + 0.5 * x * (1.0 - t * t) * _GK * (1.0 + 3.0 * _GC * x * x)


def _softplus(x):
    return jnp.maximum(x, 0.0) + jnp.log1p(jnp.exp(-jnp.abs(x)))


def _tile(n, cap):
    if n <= cap:
        return n
    best = None
    for t in range(128, cap + 1, 128):
        if n % t == 0:
            best = t
    assert best is not None, (n, cap)
    return best


def _params(sem):
    return pltpu.CompilerParams(dimension_semantics=sem, vmem_limit_bytes=VMEM_LIMIT)


def _exch_ops(src_ref, out_ref, send_sems, recv_sems, local_sem, gather):
    x, y, c = lax.axis_index("x"), lax.axis_index("y"), lax.axis_index("c")
    me = 4 * x + 2 * y + c

    def peer(j):
        px = 1 - x if (j >> 2) & 1 else x
        py = 1 - y if (j >> 1) & 1 else y
        pc = 1 - c if j & 1 else c
        return (px, py, pc), 4 * px + 2 * py + pc

    def copy(j, landing):
        dev, idx = peer(j)
        return pltpu.make_async_remote_copy(
            src_ref=src_ref if gather else src_ref.at[idx], dst_ref=out_ref.at[idx] if landing else out_ref.at[me],
            send_sem=send_sems.at[j - 1], recv_sem=recv_sems.at[j - 1], device_id=dev, device_id_type=pl.DeviceIdType.MESH)

    mine = pltpu.make_async_copy(src_ref if gather else src_ref.at[me], out_ref.at[me], local_sem)

    def start():
        mine.start()
        for j in range(1, NDEV):
            copy(j, False).start()

    def finish():
        for j in range(1, NDEV):
            copy(j, True).wait_recv()
        for j in range(1, NDEV):
            copy(j, False).wait_send()
        mine.wait()

    return start, finish


def _exch_shape(src, gather):
    return jax.ShapeDtypeStruct((NDEV,) + tuple(src.shape if gather else src.shape[1:]), src.dtype)


_EXCH_SEMS = [pltpu.SemaphoreType.DMA((NDEV - 1,)), pltpu.SemaphoreType.DMA((NDEV - 1,)), pltpu.SemaphoreType.DMA]


def _carry_call(body, name, grid, in_specs, out_specs, out_shape, scratch_shapes, args, carry=()):
    n_in, n_out, n_sc, nx = len(in_specs), len(out_specs), len(scratch_shapes), len(carry)
    sem = ("arbitrary",) * len(grid)
    if nx == 0:
        outs = pl.pallas_call(body, name=name, grid=grid, in_specs=in_specs, out_specs=out_specs, out_shape=out_shape,
                              scratch_shapes=scratch_shapes, compiler_params=_params(sem))(*args)
        return list(outs), []

    def wrapped(*refs):
        ins, srcs = refs[:n_in], refs[n_in:n_in + nx]
        outs, dsts = refs[n_in + nx:n_in + nx + n_out], refs[n_in + nx + n_out:n_in + 2 * nx + n_out]
        scratch, sems = refs[n_in + 2 * nx + n_out:n_in + 2 * nx + n_out + n_sc], refs[n_in + 2 * nx + n_out + n_sc:]
        ops = [_exch_ops(srcs[i], dsts[i], sems[3 * i], sems[3 * i + 1], sems[3 * i + 2], carry[i][1]) for i in range(nx)]
        first = functools.reduce(jnp.logical_and, [pl.program_id(d) == 0 for d in range(len(grid))])
        last = functools.reduce(jnp.logical_and, [pl.program_id(d) == grid[d] - 1 for d in range(len(grid))])

        @pl.when(first)
        def _():
            for start, _ in ops:
                start()

        body(*ins, *outs, *scratch)

        @pl.when(last)
        def _():
            for _, finish in ops:
                finish()

    hbm = pl.BlockSpec(memory_space=pl.ANY)
    outs = pl.pallas_call(
        wrapped, name=name, grid=grid, in_specs=list(in_specs) + [hbm] * nx, out_specs=list(out_specs) + [hbm] * nx,
        out_shape=list(out_shape) + [_exch_shape(s, g) for s, g in carry], scratch_shapes=list(scratch_shapes) + _EXCH_SEMS * nx,
        compiler_params=_params(sem))(*args, *[s for s, _ in carry])
    return list(outs[:n_out]), list(outs[n_out:])


def _mm(a, b, mode, name, acc=None, out_dtype=f32, tm_cap=1024, tn_cap=1280, tk_cap=1280, carry=()):
    if mode == "nn":
        (M, K), (K2, N) = a.shape, b.shape
    elif mode == "nt":
        (M, K), (N, K2) = a.shape, b.shape
    else:
        (K, M), (K2, N) = a.shape, b.shape
        tk_cap = min(tk_cap, 512)
    assert K == K2, (a.shape, b.shape, mode)
    tm, tn, tk = _tile(M, tm_cap), _tile(N, tn_cap), _tile(K, tk_cap)
    nk = K // tk
    if mode == "nn":
        a_spec = pl.BlockSpec((tm, tk), lambda i, j, k: (i, k))
        b_spec = pl.BlockSpec((tk, tn), lambda i, j, k: (k, j))
        dot = _dot
    elif mode == "nt":
        a_spec = pl.BlockSpec((tm, tk), lambda i, j, k: (i, k))
        b_spec = pl.BlockSpec((tn, tk), lambda i, j, k: (j, k))
        dot = _dot_nt
    else:
        a_spec = pl.BlockSpec((tk, tm), lambda i, j, k: (k, i))
        b_spec = pl.BlockSpec((tk, tn), lambda i, j, k: (k, j))
        dot = _dot_tn
    o_spec = pl.BlockSpec((tm, tn), lambda i, j, k: (i, j))
    has_acc = acc is not None

    def body(*refs):
        if has_acc:
            a_ref, b_ref, c_ref, o_ref, acc_ref = refs
        else:
            a_ref, b_ref, o_ref, acc_ref = refs
        k = pl.program_id(2)

        @pl.when(k == 0)
        def _():
            if has_acc:
                acc_ref[...] = c_ref[...]
            else:
                acc_ref[...] = jnp.zeros_like(acc_ref)

        acc_ref[...] += dot(_bf(a_ref[...]), _bf(b_ref[...]))

        @pl.when(k == nk - 1)
        def _():
            o_ref[...] = acc_ref[...].astype(out_dtype)

    in_specs = [a_spec, b_spec] + ([o_spec] if has_acc else [])
    args = (a, b) + ((acc,) if has_acc else ())
    (out,), exchanged = _carry_call(body, name, (M // tm, N // tn, nk), in_specs, [o_spec], [jax.ShapeDtypeStruct((M, N), out_dtype)],
                                    [pltpu.VMEM((tm, tn), f32)], args, carry)
    return (out, exchanged) if carry else out


def _row_spec(ts, w, col=0):
    return pl.BlockSpec((ts, w), lambda i: (i, col))


def _full_spec(shape):
    nd = len(shape)
    return pl.BlockSpec(shape, lambda i: (0,) * nd)


def _modulate(x, nw, shift, scale, name):
    T = x.shape[0]
    ts = _tile(T, 512)

    def body(x_ref, nw_ref, sh_ref, sc_ref, h_ref):
        xv = x_ref[...]
        r = lax.rsqrt(jnp.mean(xv * xv, axis=-1, keepdims=True) + EPS)
        h_ref[...] = ((xv * r * nw_ref[...]) * (1.0 + sc_ref[...]) + sh_ref[...]).astype(h_ref.dtype)

    return pl.pallas_call(
        body, name=name, grid=(T // ts,),
        in_specs=[_row_spec(ts, D), _full_spec((1, D)), _full_spec((1, D)), _full_spec((1, D))],
        out_specs=_row_spec(ts, D), out_shape=jax.ShapeDtypeStruct((T, D), _MXU),
        compiler_params=_params(("arbitrary",)))(x, nw, shift, scale)


def _modulate_bwd(dh, dxout, x, nw, scale, name):
    T = x.shape[0]
    ts = _tile(T, 512)

    def body(dh_ref, dxo_ref, x_ref, nw_ref, sc_ref, dx_ref, acc_ref):
        @pl.when(pl.program_id(0) == 0)
        def _():
            acc_ref[...] = jnp.zeros_like(acc_ref)

        xv, dh_v = x_ref[...], dh_ref[...]
        r = lax.rsqrt(jnp.mean(xv * xv, axis=-1, keepdims=True) + EPS)
        xn = xv * r
        hn = xn * nw_ref[...]
        dhn = dh_v * (1.0 + sc_ref[...])
        acc_ref[0:1, :] += jnp.sum(dh_v, axis=0, keepdims=True)
        acc_ref[1:2, :] += jnp.sum(dh_v * hn, axis=0, keepdims=True)
        acc_ref[2:3, :] += jnp.sum(dhn * xn, axis=0, keepdims=True)
        dxn = dhn * nw_ref[...]
        dx_ref[...] = dxo_ref[...] + r * (dxn - xn * jnp.mean(dxn * xn, axis=-1, keepdims=True))

    return pl.pallas_call(
        body, name=name, grid=(T // ts,),
        in_specs=[_row_spec(ts, D), _row_spec(ts, D), _row_spec(ts, D), _full_spec((1, D)), _full_spec((1, D))],
        out_specs=[_row_spec(ts, D), _full_spec((8, D))],
        out_shape=[jax.ShapeDtypeStruct((T, D), f32), jax.ShapeDtypeStruct((8, D), f32)],
        compiler_params=_params(("arbitrary",)))(dh, dxout, x, nw, scale)


def _final_loss(x, tgt, fw, name):
    T = x.shape[0]
    ts = _tile(T, 512)

    def body(x_ref, t_ref, fw_ref, dx_ref, acc_ref):
        @pl.when(pl.program_id(0) == 0)
        def _():
            acc_ref[...] = jnp.zeros_like(acc_ref)

        xv = x_ref[...]
        r = lax.rsqrt(jnp.mean(xv * xv, axis=-1, keepdims=True) + EPS)
        xn = xv * r
        e = xn * fw_ref[...] - t_ref[...]
        dy = e * (1.0 / D)
        acc_ref[0:1, :] += jnp.sum(dy * xn, axis=0, keepdims=True)
        acc_ref[1:2, :] += jnp.sum(jnp.sum(e * e, axis=0, keepdims=True), axis=1, keepdims=True)
        dxn = dy * fw_ref[...]
        dx_ref[...] = r * (dxn - xn * jnp.mean(dxn * xn, axis=-1, keepdims=True))

    return pl.pallas_call(
        body, name=name, grid=(T // ts,),
        in_specs=[_row_spec(ts, D), _row_spec(ts, D), _full_spec((1, D))],
        out_specs=[_row_spec(ts, D), _full_spec((8, D))],
        out_shape=[jax.ShapeDtypeStruct((T, D), f32), jax.ShapeDtypeStruct((8, D), f32)],
        compiler_params=_params(("arbitrary",)))(x, tgt, fw)


def _tril(n):
    return lax.broadcasted_iota(jnp.int32, (n, n), 0) >= lax.broadcasted_iota(jnp.int32, (n, n), 1)


def _gm_chunk_fwd(u, v, z, lw, lb, ws_ref, bsx):
    gu, gv = _gelu(u), _gelu(v)
    mu = jnp.mean(gv, axis=-1, keepdims=True)
    cen = gv - mu
    rstd = lax.rsqrt(jnp.mean(cen * cen, axis=-1, keepdims=True) + EPS)
    vhat = cen * rstd
    vn = _bf(vhat * lw + lb)
    tri = _tril(Q)
    mixed = jnp.concatenate(
        [_dot(_bf(jnp.where(tri, ws_ref[g], 0.0)), vn[:, g * Q:(g + 1) * Q]) for g in range(NG)], axis=1) + bsx
    return gu, vhat, rstd, vn, mixed


def _gmlp_fwd(pA, lw, lb, ws, bsx, name):
    T = pA.shape[0]
    ts = _tile(T, 512)

    def body(u_ref, v_ref, z_ref, lw_ref, lb_ref, ws_ref, bsx_ref, y_ref):
        def chunk(ci, carry):
            rows = pl.ds(pl.multiple_of(ci * Q, Q), Q)
            u, v, z = u_ref[rows, :].astype(f32), v_ref[rows, :].astype(f32), z_ref[rows, :].astype(f32)
            gu, _, _, _, mixed = _gm_chunk_fwd(u, v, z, lw_ref[...], lb_ref[...], ws_ref, bsx_ref[...])
            y_ref[rows, :] = (gu * mixed * _silu(z)).astype(y_ref.dtype)
            return carry

        lax.fori_loop(0, ts // Q, chunk, 0)

    return pl.pallas_call(
        body, name=name, grid=(T // ts,),
        in_specs=[_row_spec(ts, D, 0), _row_spec(ts, D, 1), _row_spec(ts, D, 2), _full_spec((1, D)), _full_spec((1, D)),
                  _full_spec((NG, Q, Q)), _full_spec((Q, D))],
        out_specs=_row_spec(ts, D), out_shape=jax.ShapeDtypeStruct((T, D), _MXU),
        compiler_params=_params(("arbitrary",)))(pA, pA, pA, lw, lb, ws, bsx)


def _gmlp_bwd(dya, pA, lw, lb, ws, bsx, name):
    T = pA.shape[0]
    ts = _tile(T, 512)

    def body(dy_ref, u_ref, v_ref, z_ref, lw_ref, lb_ref, ws_ref, bsx_ref, dp_ref, acc_ref, dws_ref, dbs_ref):
        @pl.when(pl.program_id(0) == 0)
        def _():
            acc_ref[...] = jnp.zeros_like(acc_ref)
            dws_ref[...] = jnp.zeros_like(dws_ref)
            dbs_ref[...] = jnp.zeros_like(dbs_ref)

        tri = _tril(Q)

        def chunk(ci, carry):
            rows = pl.ds(pl.multiple_of(ci * Q, Q), Q)
            u, v, z, dy = u_ref[rows, :].astype(f32), v_ref[rows, :].astype(f32), z_ref[rows, :].astype(f32), dy_ref[rows, :]
            gu, vhat, rstd, vn, mixed = _gm_chunk_fwd(u, v, z, lw_ref[...], lb_ref[...], ws_ref, bsx_ref[...])
            sz = _silu(z)
            dp_ref[rows, 0:D] = (dy * mixed * sz * _dgelu(u)).astype(dp_ref.dtype)
            dp_ref[rows, 2 * D:3 * D] = (dy * gu * mixed * _dsilu(z)).astype(dp_ref.dtype)
            dmixed = dy * gu * sz
            dbs_ref[...] += dmixed
            dmb = _bf(dmixed)
            dvn_parts = []
            for g in range(NG):
                cols = slice(g * Q, (g + 1) * Q)
                wg = _bf(jnp.where(tri, ws_ref[g], 0.0))
                dvn_parts.append(_dot_tn(wg, dmb[:, cols]))
                dws_ref[g] += jnp.where(tri, _dot_nt(dmb[:, cols], vn[:, cols]), 0.0)
            dvn = jnp.concatenate(dvn_parts, axis=1)
            acc_ref[0:1, :] += jnp.sum(dvn * vhat, axis=0, keepdims=True)
            acc_ref[1:2, :] += jnp.sum(dvn, axis=0, keepdims=True)
            dvh = dvn * lw_ref[...]
            dgv = rstd * (dvh - jnp.mean(dvh, axis=-1, keepdims=True) - vhat * jnp.mean(dvh * vhat, axis=-1, keepdims=True))
            dp_ref[rows, D:2 * D] = (dgv * _dgelu(v)).astype(dp_ref.dtype)
            return carry

        lax.fori_loop(0, ts // Q, chunk, 0)

    return pl.pallas_call(
        body, name=name, grid=(T // ts,),
        in_specs=[_row_spec(ts, D), _row_spec(ts, D, 0), _row_spec(ts, D, 1), _row_spec(ts, D, 2), _full_spec((1, D)),
                  _full_spec((1, D)), _full_spec((NG, Q, Q)), _full_spec((Q, D))],
        out_specs=[_row_spec(ts, WA), _full_spec((8, D)), _full_spec((NG, Q, Q)), _full_spec((Q, D))],
        out_shape=[jax.ShapeDtypeStruct((T, WA), _MXU), jax.ShapeDtypeStruct((8, D), f32),
                   jax.ShapeDtypeStruct((NG, Q, Q), f32), jax.ShapeDtypeStruct((Q, D), f32)],
        compiler_params=_params(("arbitrary",)))(dya, pA, pA, pA, lw, lb, ws, bsx)


def _head_maps():
    h = lax.broadcasted_iota(jnp.int32, (128, DI), 0)
    ch = lax.broadcasted_iota(jnp.int32, (128, DI), 1)
    ex = (ch // P == h).astype(_MXU)
    return ex, ex.T


def _split(v, parts):
    out = []
    for _ in range(parts - 1):
        p = _bf(v)
        out.append(p)
        v = v - p.astype(f32)
    out.append(_bf(v))
    return out


def _expand(v, ex_ref, parts):
    acc = None
    for p in _split(v, parts):
        t = _dot(p, ex_ref[...])
        acc = t if acc is None else acc + t
    return acc


def _reduce(v, rd_ref):
    hi, lo = _split(v, 2)
    return _dot(hi, rd_ref[...]) + _dot(lo, rd_ref[...])


def _ssd_time(pb_ref, dtb_ref, alog_ref):
    xdt = pb_ref[:, OFF_DT:OFF_DT + 128] + dtb_ref[...]
    dt = _softplus(xdt)
    a = -jnp.exp(alog_ref[...])
    cs = _dot_exact(_tril(Q).astype(f32), dt * a)
    return xdt, dt, a, cs


def _head_mask(r):
    return lax.broadcasted_iota(jnp.int32, (Q, GRP), 1) // P == r


def _ssd_group_fwd(g, xa_ref, s_prev, cs, cs_t, dtx_ref, csx_ref, dsk_ref):
    cols = slice(g * GRP, (g + 1) * GRP)
    xs = xa_ref[:, cols]
    bb = _bf(xa_ref[:, DI + g * NS:DI + (g + 1) * NS])
    cb = _bf(xa_ref[:, DI + NG * NS + g * NS:DI + NG * NS + (g + 1) * NS])
    gm = _dot_nt(cb, bb)
    xd = xs * dtx_ref[:, cols]
    csx = csx_ref[:, cols]
    csl = csx_ref[Q - 1:Q, cols]
    tri = _tril(Q)
    lms = [jnp.exp(jnp.where(tri, cs[:, HPG * g + r:HPG * g + r + 1] - cs_t[HPG * g + r:HPG * g + r + 1, :], NEG))
           for r in range(HPG)]
    mfs = [gm * lm for lm in lms]
    mcat = jnp.concatenate([_bf(m) for m in mfs], axis=1)
    xbd = jnp.concatenate([_bf(jnp.where(_head_mask(r), xd, 0.0)) for r in range(HPG)], axis=0)
    ydiag = _dot(mcat, xbd)
    yoff = jnp.exp(csx) * _dot(cb, _bf(s_prev))
    y = ydiag + yoff + xs * dsk_ref[:, cols]
    xdd = xd * jnp.exp(csl - csx)
    s_new = s_prev * jnp.exp(csl) + _dot_tn(bb, _bf(xdd))
    return y, s_new, (xs, bb, cb, xd, lms, mfs, mcat, xbd, yoff, xdd, csx, csl)


def _ssd_fwd(pB, cw, cb, dtb, alog, dsk, snw, name, carry=()):
    T = pB.shape[0]
    nc = T // Q
    ex, _ = _head_maps()

    def body(pb_ref, cw_ref, cb_ref, dtb_ref, alog_ref, dsk_ref, snw_ref, ex_ref, y_ref, xc_ref, st_ref,
             s_ref, ext_ref, xa_ref, dtx_ref, csx_ref):
        @pl.when(pl.program_id(0) == 0)
        def _():
            s_ref[...] = jnp.zeros_like(s_ref)
            ext_ref[...] = jnp.zeros_like(ext_ref)

        ext_ref[8:8 + Q, :] = pb_ref[:, OFF_XBC:OFF_XBC + CD]
        for j in range(CD // 512):
            cj = slice(j * 512, (j + 1) * 512)
            e = ext_ref[:, cj]
            xc = cb_ref[:, cj] + cw_ref[KC - 1:KC, cj] * e[8:8 + Q]
            for s in range(1, KC):
                xc = xc + cw_ref[KC - 1 - s:KC - s, cj] * pltpu.roll(e, s, 0)[8:8 + Q]
            xc_ref[:, cj] = xc
            xa_ref[:, cj] = _silu(xc)
        ext_ref[0:8, :] = ext_ref[Q:Q + 8, :]

        _, dt, _, cs = _ssd_time(pb_ref, dtb_ref, alog_ref)
        cs_t = cs.T
        dtx_ref[...] = _expand(dt, ex_ref, 2)
        csx_ref[...] = _expand(cs, ex_ref, 3)
        for g in range(NG):
            s_prev = s_ref[g]
            st_ref[0, g] = s_prev
            y, s_new, _ = _ssd_group_fwd(g, xa_ref, s_prev, cs, cs_t, dtx_ref, csx_ref, dsk_ref)
            s_ref[g] = s_new
            cols = slice(g * GRP, (g + 1) * GRP)
            yz = y * _silu(pb_ref[:, cols])
            rr = lax.rsqrt(jnp.mean(yz * yz, axis=-1, keepdims=True) + EPS)
            y_ref[:, cols] = (yz * rr * snw_ref[:, cols]).astype(y_ref.dtype)

    outs, exchanged = _carry_call(
        body, name, (nc,),
        [_row_spec(Q, WB), _full_spec((8, CD)), _full_spec((1, CD)), _full_spec((1, 128)), _full_spec((1, 128)),
         _full_spec((1, DI)), _full_spec((1, DI)), _full_spec((128, DI))],
        [_row_spec(Q, DI), _row_spec(Q, CD), pl.BlockSpec((1, NG, NS, GRP), lambda i: (i, 0, 0, 0))],
        [jax.ShapeDtypeStruct((T, DI), _MXU), jax.ShapeDtypeStruct((T, CD), f32), jax.ShapeDtypeStruct((nc, NG, NS, GRP), f32)],
        [pltpu.VMEM((NG, NS, GRP), f32), pltpu.VMEM((Q + 8, CD), f32), pltpu.VMEM((Q, CD), f32),
         pltpu.VMEM((Q, DI), f32), pltpu.VMEM((Q, DI), f32)],
        (pB, cw, cb, dtb, alog, dsk, snw, ex), carry)
    return outs + [exchanged]


def _ssd_bwd(dyb, pB, xc, states, cw, dtb, alog, dsk, snw, name, carry=()):
    T = pB.shape[0]
    nc = T // Q
    ex, rd = _head_maps()
    selr = (lax.broadcasted_iota(jnp.int32, (HPG * Q, 128), 0) // Q == lax.broadcasted_iota(jnp.int32, (HPG * Q, 128), 1)).astype(_MXU)

    def body(dy_ref, pb_ref, xc_ref, st_ref, cw_ref, dtb_ref, alog_ref, dsk_ref, snw_ref, ex_ref, rd_ref, selr_ref,
             dp_ref, dcw_ref, dcb_ref, dhd_ref, dcol_ref,
             ds_ref, xa_ref, dxa_ref, dxe_ref, dtx_ref, csx_ref, rcs_ref, rdt_ref, rl_ref, dcs_ref, dcst_ref):
        @pl.when(pl.program_id(0) == 0)
        def _():
            ds_ref[...] = jnp.zeros_like(ds_ref)
            dxe_ref[...] = jnp.zeros_like(dxe_ref)
            dcw_ref[...] = jnp.zeros_like(dcw_ref)
            dcb_ref[...] = jnp.zeros_like(dcb_ref)
            dhd_ref[...] = jnp.zeros_like(dhd_ref)
            dcol_ref[...] = jnp.zeros_like(dcol_ref)
            rl_ref[...] = jnp.zeros_like(rl_ref)
            dcst_ref[...] = jnp.zeros_like(dcst_ref)

        dcs_ref[...] = jnp.zeros_like(dcs_ref)
        for j in range(CD // 512):
            cj = slice(j * 512, (j + 1) * 512)
            xa_ref[:, cj] = _silu(xc_ref[:, cj])
        xdt, dt, a, cs = _ssd_time(pb_ref, dtb_ref, alog_ref)
        cs_t = cs.T
        dtx_ref[...] = _expand(dt, ex_ref, 2)
        csx_ref[...] = _expand(cs, ex_ref, 3)

        for g in range(NG):
            cols = slice(g * GRP, (g + 1) * GRP)
            s_prev = st_ref[0, g]
            sb = _bf(s_prev)
            y, _, (xs, bb, cbm, xd, lms, mfs, mcat, xbd, yoff, xdd, csx, csl) = _ssd_group_fwd(
                g, xa_ref, s_prev, cs, cs_t, dtx_ref, csx_ref, dsk_ref)
            z = pb_ref[:, cols]
            sz = _silu(z)
            yz = y * sz
            rr = lax.rsqrt(jnp.mean(yz * yz, axis=-1, keepdims=True) + EPS)
            nrm = yz * rr
            dyb_g = dy_ref[:, cols]
            dcol_ref[1:2, cols] += jnp.sum(dyb_g * nrm, axis=0, keepdims=True)
            dn = dyb_g * snw_ref[:, cols]
            dyz = rr * (dn - nrm * jnp.mean(dn * nrm, axis=-1, keepdims=True))
            dyv = dyz * sz
            dp_ref[:, cols] = (dyz * y * _dsilu(z)).astype(dp_ref.dtype)
            dcol_ref[0:1, cols] += jnp.sum(dyv * xs, axis=0, keepdims=True)
            dy16 = _bf(dyv)
            dmcat = _dot_nt(dy16, xbd)
            dg = dmcat[:, 0:Q] * lms[0]
            for r in range(1, HPG):
                dg = dg + dmcat[:, r * Q:(r + 1) * Q] * lms[r]
            e_hi, e_lo = _split(jnp.concatenate([dmcat[:, r * Q:(r + 1) * Q] * mfs[r] for r in range(HPG)], axis=1), 2)
            rows = _dot(e_hi, selr_ref[...]) + _dot(e_lo, selr_ref[...])
            dcs_ref[...] += rows if g == 0 else pltpu.roll(rows, HPG * g, 1)
            ones8 = jnp.ones((8, Q), _MXU)
            csum = _dot(ones8, e_hi) + _dot(ones8, e_lo)
            for r in range(HPG):
                dcst_ref[HPG * g + r:HPG * g + r + 1, :] = csum[0:1, r * Q:(r + 1) * Q]
            big = _dot_tn(mcat, dy16)
            dxd_diag = jnp.where(_head_mask(0), big[0:Q], 0.0)
            for r in range(1, HPG):
                dxd_diag = dxd_diag + jnp.where(_head_mask(r), big[r * Q:(r + 1) * Q], 0.0)
            dsn = ds_ref[g]
            dsn16 = _bf(dsn)
            dxdd = _dot(bb, dsn16)
            t_state = dxdd * xdd
            dxd = dxd_diag + dxdd * jnp.exp(csl - csx)
            rcs_ref[:, cols] = dyv * yoff - t_state
            rl_ref[0:1, cols] = (jnp.sum(t_state, axis=0, keepdims=True)
                                 + jnp.exp(csl) * jnp.sum(s_prev * dsn, axis=0, keepdims=True))
            rdt_ref[:, cols] = dxd * xs
            dxa_ref[:, cols] = dyv * dsk_ref[:, cols] + dxd * dtx_ref[:, cols]
            dg16 = _bf(dg)
            dw16 = _bf(dyv * jnp.exp(csx))
            dxa_ref[:, DI + NG * NS + g * NS:DI + NG * NS + (g + 1) * NS] = _dot(dg16, bb) + _dot_nt(dw16, sb)
            dxa_ref[:, DI + g * NS:DI + (g + 1) * NS] = _dot_tn(dg16, cbm) + _dot_nt(_bf(xdd), dsn16)
            ds_ref[g] = dsn * jnp.exp(csl) + _dot_tn(cbm, dw16)

        row = lax.broadcasted_iota(jnp.int32, (Q, 128), 0)
        dcs = (dcs_ref[...] - dcst_ref[...].T + _reduce(rcs_ref[...], rd_ref)
               + jnp.where(row == Q - 1, _reduce(rl_ref[...], rd_ref)[0:1, :], 0.0))
        upper = lax.broadcasted_iota(jnp.int32, (Q, Q), 0) <= lax.broadcasted_iota(jnp.int32, (Q, Q), 1)
        dadt = _dot_exact(upper.astype(f32), dcs)
        valid = lax.broadcasted_iota(jnp.int32, (Q, 128), 1) < NH
        ddt = jnp.where(valid, _reduce(rdt_ref[...], rd_ref) + dadt * a, 0.0)
        ddtr = ddt * _sigmoid(xdt)
        dhd_ref[0:1, :] += jnp.sum(ddtr, axis=0, keepdims=True)
        dhd_ref[1:2, :] += jnp.sum(jnp.where(valid, dadt * dt * a, 0.0), axis=0, keepdims=True)
        dp_ref[:, OFF_DT:OFF_DT + 128] = ddtr.astype(dp_ref.dtype)
        dp_ref[:, OFF_DT + 128:WB] = jnp.zeros((Q, WB - OFF_DT - 128), dp_ref.dtype)
        for j in range(CD // 512):
            cj = slice(j * 512, (j + 1) * 512)
            dxc = dxa_ref[:, cj] * _dsilu(xc_ref[:, cj])
            dxe_ref[0:Q, cj] = dxc
            dcb_ref[0:1, cj] += jnp.sum(dxc, axis=0, keepdims=True)
            raw = pb_ref[:, OFF_XBC + j * 512:OFF_XBC + (j + 1) * 512]
            e = dxe_ref[:, cj]
            dcw_ref[KC - 1:KC, cj] += jnp.sum(dxc * raw, axis=0, keepdims=True)
            dxb = cw_ref[KC - 1:KC, cj] * dxc
            for s in range(1, KC):
                sh = pltpu.roll(e, Q + 8 - s, 0)[0:Q]
                dcw_ref[KC - 1 - s:KC - s, cj] += jnp.sum(sh * raw, axis=0, keepdims=True)
                dxb = dxb + cw_ref[KC - 1 - s:KC - s, cj] * sh
            dp_ref[:, OFF_XBC + j * 512:OFF_XBC + (j + 1) * 512] = dxb.astype(dp_ref.dtype)
        dxe_ref[Q:Q + 8, :] = dxe_ref[0:8, :]

    rev = lambda i: (nc - 1 - i, 0)
    outs, exchanged = _carry_call(
        body, name, (nc,),
        [pl.BlockSpec((Q, DI), rev), pl.BlockSpec((Q, WB), rev), pl.BlockSpec((Q, CD), rev),
         pl.BlockSpec((1, NG, NS, GRP), lambda i: (nc - 1 - i, 0, 0, 0)),
         _full_spec((8, CD)), _full_spec((1, 128)), _full_spec((1, 128)),
         _full_spec((1, DI)), _full_spec((1, DI)), _full_spec((128, DI)), _full_spec((DI, 128)), _full_spec((HPG * Q, 128))],
        [pl.BlockSpec((Q, WB), rev), _full_spec((8, CD)), _full_spec((8, CD)), _full_spec((8, 128)), _full_spec((8, DI))],
        [jax.ShapeDtypeStruct((T, WB), _MXU), jax.ShapeDtypeStruct((8, CD), f32), jax.ShapeDtypeStruct((8, CD), f32),
         jax.ShapeDtypeStruct((8, 128), f32), jax.ShapeDtypeStruct((8, DI), f32)],
        [pltpu.VMEM((NG, NS, GRP), f32), pltpu.VMEM((Q, CD), f32), pltpu.VMEM((Q, CD), f32),
         pltpu.VMEM((Q + 8, CD), f32), pltpu.VMEM((Q, DI), f32), pltpu.VMEM((Q, DI), f32),
         pltpu.VMEM((Q, DI), f32), pltpu.VMEM((Q, DI), f32), pltpu.VMEM((8, DI), f32),
         pltpu.VMEM((Q, 128), f32), pltpu.VMEM((128, Q), f32)],
        (dyb, pB, xc, states, cw, dtb, alog, dsk, snw, ex, rd, selr), carry)
    return outs + [exchanged]


def _merge_fwd(ya, yb, pG, x, gate, wa, wb, wo, name, carry=()):
    T = x.shape[0]
    ts = _tile(T, 256)

    def body(ya_ref, yb_ref, g_ref, x_ref, gate_ref, wa_ref, wb_ref, wo_ref, xo_ref, mg_ref, pa_ref, pb_ref):
        pa = _dot(ya_ref[...], wa_ref[...])
        pb = _dot(yb_ref[...], wb_ref[...])
        merged = _sigmoid(g_ref[:, 0:D].astype(f32)) * pa + _sigmoid(g_ref[:, D:2 * D].astype(f32)) * pb
        mg = _bf(merged)
        xo_ref[...] = x_ref[...] + gate_ref[...] * _dot(mg, wo_ref[...])
        mg_ref[...] = mg
        pa_ref[...] = pa.astype(pa_ref.dtype)
        pb_ref[...] = pb.astype(pb_ref.dtype)

    outs, exchanged = _carry_call(
        body, name, (T // ts,),
        [_row_spec(ts, D), _row_spec(ts, DI), _row_spec(ts, WG), _row_spec(ts, D), _full_spec((1, D)),
         _full_spec((D, D)), _full_spec((DI, D)), _full_spec((D, D))],
        [_row_spec(ts, D)] * 4,
        [jax.ShapeDtypeStruct((T, D), f32), jax.ShapeDtypeStruct((T, D), _MXU), jax.ShapeDtypeStruct((T, D), _MXU),
         jax.ShapeDtypeStruct((T, D), _MXU)],
        [], (ya, yb, pG, x, gate, wa, wb, wo), carry)
    return outs + [exchanged]


def _merge_bwd(dxout, merged, pa, pb, pG, gate, wo, name):
    T = dxout.shape[0]
    ts = _tile(T, 256)

    def body(dx_ref, mg_ref, pa_ref, pb_ref, g_ref, gate_ref, wo_ref, do_ref, dpa_ref, dpb_ref, dg_ref, acc_ref):
        @pl.when(pl.program_id(0) == 0)
        def _():
            acc_ref[...] = jnp.zeros_like(acc_ref)

        dxo = dx_ref[...]
        acc_ref[0:1, :] += jnp.sum(dxo * _dot(mg_ref[...], wo_ref[...]), axis=0, keepdims=True)
        do = _bf(dxo * gate_ref[...])
        do_ref[...] = do
        dmerged = _dot_nt(do, wo_ref[...])
        sa, sb = _sigmoid(g_ref[:, 0:D].astype(f32)), _sigmoid(g_ref[:, D:2 * D].astype(f32))
        dpa_ref[...] = (dmerged * sa).astype(dpa_ref.dtype)
        dpb_ref[...] = (dmerged * sb).astype(dpb_ref.dtype)
        dg_ref[:, 0:D] = (dmerged * pa_ref[...].astype(f32) * sa * (1.0 - sa)).astype(dg_ref.dtype)
        dg_ref[:, D:2 * D] = (dmerged * pb_ref[...].astype(f32) * sb * (1.0 - sb)).astype(dg_ref.dtype)

    return pl.pallas_call(
        body, name=name, grid=(T // ts,),
        in_specs=[_row_spec(ts, D), _row_spec(ts, D), _row_spec(ts, D), _row_spec(ts, D), _row_spec(ts, WG),
                  _full_spec((1, D)), _full_spec((D, D))],
        out_specs=[_row_spec(ts, D), _row_spec(ts, D), _row_spec(ts, D), _row_spec(ts, WG), _full_spec((8, D))],
        out_shape=[jax.ShapeDtypeStruct((T, D), _MXU), jax.ShapeDtypeStruct((T, D), _MXU), jax.ShapeDtypeStruct((T, D), _MXU),
                   jax.ShapeDtypeStruct((T, WG), _MXU), jax.ShapeDtypeStruct((8, D), f32)],
        compiler_params=_params(("arbitrary",)))(dxout, merged, pa, pb, pG, gate, wo)


def _layer_fwd(x, mod, lp, tag, carry=None):
    carry = carry or {}
    got = {}
    h = _modulate(x, lp["nw"], mod[0:1], mod[1:2], f"modulate_{tag}")
    pA = _mm(h, lp["w_gm"], "nn", f"proj_gm_{tag}", out_dtype=_MXU)
    if carry.get("proj_ssd"):
        pB, got["proj_ssd"] = _mm(h, lp["w_ssd"], "nn", f"proj_ssd_{tag}", carry=carry["proj_ssd"])
    else:
        pB = _mm(h, lp["w_ssd"], "nn", f"proj_ssd_{tag}")
    pG = _mm(h, lp["w_g"], "nn", f"proj_gate_{tag}", out_dtype=_MXU)
    ya = _gmlp_fwd(pA, lp["lw"], lp["lb"], lp["ws"], lp["bsx"], f"gmlp_fwd_{tag}")
    yb, xc, states, got["ssd_fwd"] = _ssd_fwd(pB, lp["cw"], lp["cb"], lp["dtb"], lp["alog"], lp["dsk"], lp["snw"], f"ssd_fwd_{tag}",
                                              carry.get("ssd_fwd", ()))
    xo, merged, pa, pb, got["merge_fwd"] = _merge_fwd(ya, yb, pG, x, mod[2:3], lp["wa"], lp["wb"], lp["wo"], f"merge_fwd_{tag}",
                                                      carry.get("merge_fwd", ()))
    return xo, dict(x=x, h=h, pA=pA, pB=pB, pG=pG, ya=ya, yb=yb, xc=xc, states=states, merged=merged, pa=pa, pb=pb), got


def _dh_modulate_bwd(dpG, dpA, dpB, w_g, w_gm, w_ssd, dxout, x, nw, scale, name, carry=()):
    T = x.shape[0]
    tm = _tile(T, 512)
    fam = [(WG, _tile(WG, 1280)), (WA, _tile(WA, 1280)), (WB, _tile(WB, 1280))]
    steps = [w // k for w, k in fam]
    first = [0, steps[0], steps[0] + steps[1]]
    ns = sum(steps)

    def chunk(f):
        return lambda i, s: jnp.clip(s - first[f], 0, steps[f] - 1)

    a_specs = [pl.BlockSpec((tm, fam[f][1]), lambda i, s, c=chunk(f): (i, c(i, s))) for f in range(3)]
    b_specs = [pl.BlockSpec((D, fam[f][1]), lambda i, s, c=chunk(f): (0, c(i, s))) for f in range(3)]
    tok = pl.BlockSpec((tm, D), lambda i, s: (i, 0))
    vec = pl.BlockSpec((1, D), lambda i, s: (0, 0))

    def body(ag, aa, ab, bg, ba, bb, dxo_ref, x_ref, nw_ref, sc_ref, dx_ref, sum_ref, acc_ref):
        i, s = pl.program_id(0), pl.program_id(1)

        @pl.when(jnp.logical_and(i == 0, s == 0))
        def _():
            sum_ref[...] = jnp.zeros_like(sum_ref)

        @pl.when(s == 0)
        def _():
            acc_ref[...] = jnp.zeros_like(acc_ref)

        @pl.when(s < first[1])
        def _():
            acc_ref[...] += _dot_nt(ag[...], bg[...])

        @pl.when(jnp.logical_and(s >= first[1], s < first[2]))
        def _():
            acc_ref[...] += _dot_nt(aa[...], ba[...])

        @pl.when(s >= first[2])
        def _():
            acc_ref[...] += _dot_nt(ab[...], bb[...])

        @pl.when(s == ns - 1)
        def _():
            xv, dh_v = x_ref[...], acc_ref[...]
            r = lax.rsqrt(jnp.mean(xv * xv, axis=-1, keepdims=True) + EPS)
            xn = xv * r
            hn = xn * nw_ref[...]
            dhn = dh_v * (1.0 + sc_ref[...])
            sum_ref[0:1, :] += jnp.sum(dh_v, axis=0, keepdims=True)
            sum_ref[1:2, :] += jnp.sum(dh_v * hn, axis=0, keepdims=True)
            sum_ref[2:3, :] += jnp.sum(dhn * xn, axis=0, keepdims=True)
            dxn = dhn * nw_ref[...]
            dx_ref[...] = dxo_ref[...] + r * (dxn - xn * jnp.mean(dxn * xn, axis=-1, keepdims=True))

    outs, exchanged = _carry_call(
        body, name, (T // tm, ns), a_specs + b_specs + [tok, tok, vec, vec], [tok, pl.BlockSpec((8, D), lambda i, s: (0, 0))],
        [jax.ShapeDtypeStruct((T, D), f32), jax.ShapeDtypeStruct((8, D), f32)], [pltpu.VMEM((tm, D), f32)],
        (dpG, dpA, dpB, w_g, w_gm, w_ssd, dxout, x, nw, scale), carry)
    return outs + [exchanged]


def _win_blocks(g):
    full = jnp.concatenate([g["w_gm"], g["w_ssd"][:, 0:DI + CD], g["w_ssd"][:, DI + CD:DI + CD + NH], g["w_g"]], axis=1)
    return jnp.transpose(full.reshape(D, NDEV, NIN // NDEV), (1, 0, 2))


def _row_blocks(g):
    return jnp.concatenate([g["wa"].reshape(NDEV, D // NDEV, D), g["wb"].reshape(NDEV, DI // NDEV, D),
                            g["wo"].reshape(NDEV, D // NDEV, D)], axis=1)


def _layer_bwd(dxo, sv, mod, lp, tag, carry=(), scatter_own=False):
    do, dpa, dpb, dpG, s_gate = _merge_bwd(dxo, sv["merged"], sv["pa"], sv["pb"], sv["pG"], mod[2:3], lp["wo"], f"merge_bwd_{tag}")
    g = {}
    g["wo"] = _mm(sv["merged"], do, "tn", f"dw_out_{tag}", out_dtype=_WIRE)
    g["wa"] = _mm(sv["ya"], dpa, "tn", f"dw_proj_a_{tag}", out_dtype=_WIRE)
    g["wb"] = _mm(sv["yb"], dpb, "tn", f"dw_proj_b_{tag}", out_dtype=_WIRE)
    dya = _mm(dpa, lp["wa"], "nt", f"dy_a_{tag}")
    dyb = _mm(dpb, lp["wb"], "nt", f"dy_b_{tag}")
    dpA, s_ln, g["ws"], dbsx = _gmlp_bwd(dya, sv["pA"], lp["lw"], lp["lb"], lp["ws"], lp["bsx"], f"gmlp_bwd_{tag}")
    carry = list(carry) + ([(_row_blocks(g), False)] if scatter_own else [])
    dpB, g["cw"], s_cb, s_hd, s_col, got = _ssd_bwd(dyb, sv["pB"], sv["xc"], sv["states"], lp["cw"], lp["dtb"], lp["alog"],
                                                    lp["dsk"], lp["snw"], f"ssd_bwd_{tag}", carry)
    g["w_g"] = _mm(sv["h"], dpG, "tn", f"dw_gate_{tag}", out_dtype=_WIRE)
    g["w_gm"] = _mm(sv["h"], dpA, "tn", f"dw_gm_{tag}", out_dtype=_WIRE)
    g["w_ssd"] = _mm(sv["h"], dpB, "tn", f"dw_ssd_{tag}", out_dtype=_WIRE)
    dx, s_mod, got_win = _dh_modulate_bwd(dpG, dpA, dpB, lp["w_g"], lp["w_gm"], lp["w_ssd"], dxo, sv["x"], lp["nw"], mod[1:2],
                                          f"dh_{tag}", [(_win_blocks(g), False)] if scatter_own else ())
    if scatter_own:
        got, got_own = got[:-1], [got[-1], got_win[0]]
    else:
        got_own = []
    g["mod"] = jnp.concatenate([s_mod[0], s_mod[1], s_gate[0]])
    g["nw"] = s_mod[2]
    g["lw"], g["lb"] = s_ln[0], s_ln[1]
    g["bs"] = dbsx.reshape(Q, NG, Q).sum(-1).T
    g["cb"] = s_cb[0]
    g["dtb"], g["alog"] = s_hd[0, :NH], s_hd[1, :NH]
    g["dsk"] = s_col[0].reshape(NH, P).sum(-1)
    g["snw"] = s_col[1]
    g["cw"] = g["cw"][0:KC]
    return dx, g, got, got_own


def _prep_layer(nw, w_in_full, lw, lb, ws, bs, cw_full, cb, dtb, alog, dsk, snw, wa, wb, wo):
    z = jnp.zeros((D, WB - (DI + CD + NH)), w_in_full.dtype)
    pad_h = lambda v: jnp.pad(v, (0, 128 - NH)).reshape(1, 128)
    return dict(
        nw=nw.reshape(1, D),
        w_gm=w_in_full[:, 0:WA],
        w_ssd=jnp.concatenate([w_in_full[:, WA:WA + DI + CD], w_in_full[:, WA + DI + CD:WA + DI + CD + NH], z], axis=1),
        w_g=w_in_full[:, WA + DI + CD + NH:NIN],
        lw=lw.reshape(1, D), lb=lb.reshape(1, D), ws=ws, bsx=jnp.repeat(bs.T, Q, axis=1),
        cw=jnp.pad(cw_full, ((0, 8 - KC), (0, 0))), cb=cb.reshape(1, CD), dtb=pad_h(dtb), alog=pad_h(alog),
        dsk=jnp.repeat(dsk, P).reshape(1, DI), snw=snw.reshape(1, DI), wa=wa, wb=wb, wo=wo)


def _exchange(src, gather, name):
    if not gather:
        assert src.shape[0] == NDEV

    def body(src_ref, out_ref, send_sems, recv_sems, local_sem):
        start, finish = _exch_ops(src_ref, out_ref, send_sems, recv_sems, local_sem, gather)
        start()
        finish()

    return pl.pallas_call(
        body, name=name, out_shape=_exch_shape(src, gather),
        in_specs=[pl.BlockSpec(memory_space=pl.ANY)], out_specs=pl.BlockSpec(memory_space=pl.ANY), scratch_shapes=_EXCH_SEMS)(src)


def _gather2(src, name):
    def body(src_ref, out_ref, send_sems, recv_sems, local_sem):
        x, y, c = lax.axis_index("x"), lax.axis_index("y"), lax.axis_index("c")
        me, sibling = (x, y, c), (x, y, 1 - c)
        chips = [(1 - x, y), (x, 1 - y), (1 - x, 1 - y)]

        def slot(px, py, pc):
            return out_ref.at[4 * px + 2 * py + pc]

        def copy(k, block, to, src=None):
            return pltpu.make_async_remote_copy(
                src_ref=slot(*block) if src is None else src, dst_ref=slot(*block), send_sem=send_sems.at[k],
                recv_sem=recv_sems.at[k], device_id=to, device_id_type=pl.DeviceIdType.MESH)

        mine = pltpu.make_async_copy(src_ref, slot(*me), local_sem)
        mine.start()
        first = [copy(0, me, sibling, src=src_ref)] + [copy(1 + j, me, (*chip, c), src=src_ref) for j, chip in enumerate(chips)]
        for cp in first:
            cp.start()
        passed = [copy(4 + j, (*chip, c), sibling) for j, chip in enumerate(chips)]
        for j, chip in enumerate(chips):
            copy(1 + j, (*chip, c), me).wait_recv()
            passed[j].start()
        copy(0, sibling, me).wait_recv()
        for j, chip in enumerate(chips):
            copy(4 + j, (*chip, 1 - c), me).wait_recv()
        for cp in first + passed:
            cp.wait_send()
        mine.wait()

    return pl.pallas_call(
        body, name=name, out_shape=_exch_shape(src, True),
        in_specs=[pl.BlockSpec(memory_space=pl.ANY)], out_specs=pl.BlockSpec(memory_space=pl.ANY), scratch_shapes=_EXCH_SEMS)(src)


def _mod_dist(c8, ada_w, ada_b_cols, name):
    L, _, AW = ada_w.shape

    def body(c_ref, w_ref, b_ref, parts_ref, sc_ref, call_ref, mine_ref, send1, recv1, send2, recv2):
        x, y, c = lax.axis_index("x"), lax.axis_index("y"), lax.axis_index("c")
        me = 4 * x + 2 * y + c

        def peer(j):
            px = 1 - x if (j >> 2) & 1 else x
            py = 1 - y if (j >> 1) & 1 else y
            pc = 1 - c if j & 1 else c
            return (px, py, pc), 4 * px + 2 * py + pc

        def copy(j, src, dst, sems, landing):
            dev, idx = peer(j)
            return pltpu.make_async_remote_copy(
                src_ref=src, dst_ref=dst.at[idx] if landing else dst.at[me], send_sem=sems[0].at[j - 1], recv_sem=sems[1].at[j - 1],
                device_id=dev, device_id_type=pl.DeviceIdType.MESH)

        def all_to_all(src, dst, sems):
            for j in range(1, NDEV):
                copy(j, src, dst, sems, False).start()
            for j in range(1, NDEV):
                copy(j, src, dst, sems, True).wait_recv()
            for j in range(1, NDEV):
                copy(j, src, dst, sems, False).wait_send()

        call_ref[me] = c_ref[...]
        all_to_all(c_ref, call_ref, (send1, recv1))
        row = lax.broadcasted_iota(jnp.int32, (8, D), 0)
        cm = jnp.zeros((8, D), f32)
        for k in range(NDEV):
            cm = jnp.where(row == k, call_ref[k], cm)
        sc = _silu(cm)
        sc_ref[...] = sc
        for l in range(L):
            mine_ref[l] = _dot(_bf(sc), w_ref[l]) + b_ref[l]
        parts_ref[me] = mine_ref[...]
        all_to_all(mine_ref, parts_ref, (send2, recv2))

    vmem = pl.BlockSpec(memory_space=pltpu.VMEM)
    sems = pltpu.SemaphoreType.DMA((NDEV - 1,))
    return pl.pallas_call(
        body, name=name, in_specs=[vmem, vmem, vmem], out_specs=[vmem, vmem],
        out_shape=[jax.ShapeDtypeStruct((NDEV, L, 8, AW), f32), jax.ShapeDtypeStruct((8, D), f32)],
        scratch_shapes=[pltpu.VMEM((NDEV, 8, D), f32), pltpu.VMEM((L, 8, AW), f32), sems, sems, sems, sems])(c8, ada_w, ada_b_cols)


def _ada_w_grad(sc_all, dmod_cols, name):
    W = dmod_cols.shape[1]

    def body(s_ref, d_ref, o_ref):
        o_ref[...] = lax.dot_general(s_ref[...], d_ref[...], (((0,), (0,)), ((), ())), preferred_element_type=f32,
                                     precision=lax.Precision.HIGHEST)

    return pl.pallas_call(body, name=name, out_shape=jax.ShapeDtypeStruct((D, W), f32))(sc_all, dmod_cols)


def _adamw_math(w, g, m, v):
    m = ADAM_B1 * m + (1.0 - ADAM_B1) * g
    v = ADAM_B2 * v + (1.0 - ADAM_B2) * (g * g)
    m_hat = m / (1.0 - ADAM_B1 ** ADAM_STEP)
    v_hat = v / (1.0 - ADAM_B2 ** ADAM_STEP)
    delta = -ADAM_LR * (m_hat / (jnp.sqrt(v_hat) + ADAM_EPS) + ADAM_WD * w)
    return delta, m, v


def _sum_adamw(recv, w, m, v, name):
    n, R, C = recv.shape
    tr = _tile(R, 128 if C > 1024 else 256)

    def body(r_ref, w_ref, m_ref, v_ref, g_ref, d_ref, nm_ref, nv_ref):
        g = r_ref[0].astype(f32)
        for k in range(1, n):
            g = g + r_ref[k].astype(f32)
        g_ref[...] = g
        d_ref[...], nm_ref[...], nv_ref[...] = _adamw_math(w_ref[...], g, m_ref[...], v_ref[...])

    spec = pl.BlockSpec((tr, C), lambda i: (i, 0))
    return pl.pallas_call(
        body, name=name, grid=(R // tr,),
        in_specs=[pl.BlockSpec((n, tr, C), lambda i: (0, i, 0)), spec, spec, spec], out_specs=[spec] * 4,
        out_shape=[jax.ShapeDtypeStruct((R, C), f32)] * 4, compiler_params=_params(("arbitrary",)))(recv, w, m, v)


def _pack(arrays, rows):
    flat = []
    for a in arrays:
        a = a.reshape(-1).astype(f32)
        flat.append(jnp.pad(a, (0, (-a.shape[0]) % 128)))
    flat = jnp.concatenate(flat)
    return jnp.pad(flat, (0, rows * 128 - flat.shape[0])).reshape(rows, 128)


def _unpack(slab, shapes):
    flat = slab.reshape(-1)
    out, off = [], 0
    for s in shapes:
        n = 1
        for d in s:
            n *= d
        out.append(flat[off:off + n].reshape(s))
        off += n + (-n) % 128
    return out


def kernel(x, c, ada_w, ada_b, norm_w, w_in, gm_ln_w, gm_ln_b, gm_ws, gm_bs, conv_w, conv_b, dt_bias, a_log, d_skip, ssm_norm_w, w_proj_a, w_proj_b, w_out, final_norm_w, loss_target, m_ada_w, m_ada_b, m_norm_w, m_w_in, m_gm_ln_w, m_gm_ln_b, m_gm_ws, m_gm_bs, m_conv_w, m_conv_b, m_dt_bias, m_a_log, m_d_skip, m_ssm_norm_w, m_w_proj_a, m_w_proj_b, m_w_out, m_final_norm_w, v_ada_w, v_ada_b, v_norm_w, v_w_in, v_gm_ln_w, v_gm_ln_b, v_gm_ws, v_gm_bs, v_conv_w, v_conv_b, v_dt_bias, v_a_log, v_d_skip, v_ssm_norm_w, v_w_proj_a, v_w_proj_b, v_w_out, v_final_norm_w):
    L = 2
    me = 4 * lax.axis_index("x") + 2 * lax.axis_index("y") + lax.axis_index("c")
    W = dict(ada_w=ada_w, ada_b=ada_b, norm_w=norm_w, w_in=w_in, gm_ln_w=gm_ln_w, gm_ln_b=gm_ln_b, gm_ws=gm_ws, gm_bs=gm_bs,
             conv_w=conv_w, conv_b=conv_b, dt_bias=dt_bias, a_log=a_log, d_skip=d_skip, ssm_norm_w=ssm_norm_w, w_proj_a=w_proj_a,
             w_proj_b=w_proj_b, w_out=w_out, final_norm_w=final_norm_w)
    M = dict(ada_w=m_ada_w, ada_b=m_ada_b, norm_w=m_norm_w, w_in=m_w_in, gm_ln_w=m_gm_ln_w, gm_ln_b=m_gm_ln_b, gm_ws=m_gm_ws,
             gm_bs=m_gm_bs, conv_w=m_conv_w, conv_b=m_conv_b, dt_bias=m_dt_bias, a_log=m_a_log, d_skip=m_d_skip,
             ssm_norm_w=m_ssm_norm_w, w_proj_a=m_w_proj_a, w_proj_b=m_w_proj_b, w_out=m_w_out, final_norm_w=m_final_norm_w)
    V = dict(ada_w=v_ada_w, ada_b=v_ada_b, norm_w=v_norm_w, w_in=v_w_in, gm_ln_w=v_gm_ln_w, gm_ln_b=v_gm_ln_b, gm_ws=v_gm_ws,
             gm_bs=v_gm_bs, conv_w=v_conv_w, conv_b=v_conv_b, dt_bias=v_dt_bias, a_log=v_a_log, d_skip=v_d_skip,
             ssm_norm_w=v_ssm_norm_w, w_proj_a=v_w_proj_a, w_proj_b=v_w_proj_b, w_out=v_w_out, final_norm_w=v_final_norm_w)
    SW = NIN // NDEV
    AW = 3 * D // NDEV
    CW = CD // NDEV
    RA, RB = D // NDEV, DI // NDEV

    wire = lambda a: a.astype(_WIRE)
    rows_of = lambda d, l: jnp.concatenate([d["w_proj_a"][l], d["w_proj_b"][l], d["w_out"][l]], axis=0)
    g_win, g_rows = [None] * L, [None] * L
    g_win[0] = _gather2(wire(w_in[0]), "gather_w_in_l0")
    g_rows[0] = _gather2(wire(rows_of(W, 0)), "gather_w_rows_l0")
    g_cw = _exchange(conv_w.reshape(L * KC, CW), True, "gather_conv_w").reshape(NDEV, L, KC, CW)
    parts, sc_all = _mod_dist(jnp.broadcast_to(c, (8, D)), wire(ada_w),
                              lax.dynamic_slice_in_dim(ada_b, me * AW, AW, axis=1).reshape(L, 1, AW), "mod_dist")
    mods = [lax.dynamic_index_in_dim(parts[:, l], me, axis=1, keepdims=False).reshape(3, D) for l in range(L)]
    cols = lambda g: jnp.transpose(g, (1, 0, 2)).reshape(g.shape[1], -1)

    def prep(l):
        r = g_rows[l]
        return _prep_layer(norm_w[l], cols(g_win[l]), gm_ln_w[l], gm_ln_b[l], gm_ws[l], gm_bs[l], cols(g_cw[:, l]), conv_b[l],
                           dt_bias[l], a_log[l], d_skip[l], ssm_norm_w[l], r[:, 0:RA].reshape(D, D),
                           r[:, RA:RA + RB].reshape(DI, D), r[:, RA + RB:].reshape(D, D))

    half = D // 2
    lp0 = prep(0)
    h, sv0, got = _layer_fwd(x[0], mods[0], lp0, "l0", dict(proj_ssd=[(wire(w_in[1][:half]), True)],
                                                           ssd_fwd=[(wire(w_in[1][half:]), True)],
                                                           merge_fwd=[(wire(rows_of(W, 1)), True)]))
    g_win[1] = jnp.concatenate([got["proj_ssd"][0], got["ssd_fwd"][0]], axis=1)
    g_rows[1] = got["merge_fwd"][0]
    lp1 = prep(1)
    h, sv1, _ = _layer_fwd(h, mods[1], lp1, "l1")
    dh, s_fin = _final_loss(h, loss_target[0], final_norm_w.reshape(1, D), "final_loss")
    loss = lax.psum(0.5 / D * s_fin[1, 0], ("x", "y", "c"))

    small_l = ["ada_b", "norm_w", "gm_ln_w", "gm_ln_b", "gm_ws", "gm_bs", "conv_b", "dt_bias", "a_log", "d_skip", "ssm_norm_w"]
    small_key = dict(ada_b="mod", norm_w="nw", gm_ln_w="lw", gm_ln_b="lb", gm_ws="ws", gm_bs="bs", conv_b="cb", dt_bias="dtb",
                     a_log="alog", d_skip="dsk", ssm_norm_w="snw")
    shapes_l = [W[n].shape[1:] for n in small_l] + [(KC, CD)]
    shapes = [shapes_l, shapes_l + [(D,)]]
    n_rows = [-(-sum(-(-functools.reduce(lambda a, b: a * b, s, 1) // 128) for s in sh) // 256) * 256 for sh in shapes]
    slab_of = lambda g, l, tail: _pack([g[small_key[n]] for n in small_l] + [g["cw"]] + tail, n_rows[l])

    dh, g1, _, _ = _layer_bwd(dh, sv1, mods[1], lp1, "l1")
    dh, g0, got, got_own = _layer_bwd(dh, sv0, mods[0], lp0, "l0",
                                      [(_win_blocks(g1), False), (_row_blocks(g1), False), (slab_of(g1, 1, [s_fin[0]]), True)], True)
    grads = [g0, g1]
    grad_x = dh[None]
    recv_win = [got_own[1], got[0]]
    recv_rows = [got_own[0], got[1]]
    slab_all = [_exchange(slab_of(g0, 0, []), True, "gather_small_l0"), got[2]]

    o_win_l = [_sum_adamw(recv_win[l], w_in[l], m_w_in[l], v_w_in[l], f"adamw_w_in_l{l}") for l in range(L)]
    o_win = [jnp.stack([o_win_l[l][k] for l in range(L)]) for k in range(4)]
    o_rows_l = [_sum_adamw(recv_rows[l], rows_of(W, l), rows_of(M, l), rows_of(V, l), f"adamw_w_rows_l{l}") for l in range(L)]
    o_pa = [jnp.stack([o_rows_l[l][k][0:RA] for l in range(L)]) for k in range(4)]
    o_pb = [jnp.stack([o_rows_l[l][k][RA:RA + RB] for l in range(L)]) for k in range(4)]
    o_po = [jnp.stack([o_rows_l[l][k][RA + RB:] for l in range(L)]) for k in range(4)]

    def slab_params(d, l):
        tail = [jnp.zeros((KC, CD), f32)] + ([d["final_norm_w"]] if l == L - 1 else [])
        return _pack([d[n][l] for n in small_l] + tail, n_rows[l])

    o_small = [[_unpack(o, shapes[l]) for o in _sum_adamw(slab_all[l], slab_params(W, l), slab_params(M, l), slab_params(V, l),
                                                          f"adamw_small_l{l}")] for l in range(L)]
    small_out = {n: [jnp.stack([o_small[l][k][i] for l in range(L)]) for k in range(4)] for i, n in enumerate(small_l)}
    small_out["final_norm_w"] = [o_small[L - 1][k][len(small_l) + 1] for k in range(4)]

    g_cw_mine = jnp.stack([lax.dynamic_slice_in_dim(o_small[l][0][len(small_l)], me * CW, CW, axis=1) for l in range(L)])
    o_cw = _sum_adamw(g_cw_mine.reshape(1, L * KC, CW), conv_w.reshape(L * KC, CW), m_conv_w.reshape(L * KC, CW),
                      v_conv_w.reshape(L * KC, CW), "adamw_conv_w")
    o_cw = [o.reshape(L, KC, CW) for o in o_cw]

    dmod_all = jnp.stack([jnp.stack([_unpack(slab_all[l][k], shapes[l])[0] for l in range(L)]) for k in range(NDEV)])
    dmod_cols = lax.dynamic_slice_in_dim(dmod_all, me * AW, AW, axis=2).reshape(NDEV, L * AW)
    g_ada_w = jnp.transpose(_ada_w_grad(sc_all, dmod_cols, "ada_w_grad").reshape(D, L, AW), (1, 0, 2))
    o_ada = _sum_adamw(g_ada_w.reshape(1, L * D, AW), ada_w.reshape(L * D, AW), m_ada_w.reshape(L * D, AW),
                       v_ada_w.reshape(L * D, AW), "adamw_ada_w")
    o_ada = [o.reshape(L, D, AW) for o in o_ada]

    big = dict(ada_w=o_ada, w_in=o_win, conv_w=o_cw, w_proj_a=o_pa, w_proj_b=o_pb, w_out=o_po)
    order = ["ada_w", "ada_b", "norm_w", "w_in", "gm_ln_w", "gm_ln_b", "gm_ws", "gm_bs", "conv_w", "conv_b", "dt_bias", "a_log",
             "d_skip", "ssm_norm_w", "w_proj_a", "w_proj_b", "w_out", "final_norm_w"]
    pick = lambda n, k: big[n][k] if n in big else small_out[n][k]
    return (loss, grad_x, *[pick(n, 0) for n in order], *[pick(n, 1) for n in order], *[pick(n, 2) for n in order],
            *[pick(n, 3) for n in order])
```

```python
import functools

import jax
import jax.numpy as jnp
from jax import lax
from jax.experimental import pallas as pl
from jax.experimental.pallas import tpu as pltpu

f32 = jnp.float32
_MXU = jnp.bfloat16
_WIRE = jnp.bfloat16

D = 1024
Q = 128
NG = 8
DI = 2048
NH = 32
P = 64
HPG = 4
NS = 128
KC = 4
CD = 4096
GRP = DI // NG
EPS = 1e-6
NDEV = 8
WA = 3 * D
WB = DI + CD + 256
WG = 2 * D
OFF_XBC = DI
OFF_DT = DI + CD
NIN = 11296
VMEM_LIMIT = 56 * 1024 * 1024
NEG = -1e30

ADAM_LR, ADAM_B1, ADAM_B2, ADAM_EPS, ADAM_WD, ADAM_STEP = 0.001, 0.9, 0.999, 1e-08, 0.01, 10


def _bf(x):
    return x.astype(_MXU)


def _dot(a, b):
    return jnp.dot(a, b, preferred_element_type=f32)


def _dot_nt(a, b):
    return lax.dot_general(a, b, (((1,), (1,)), ((), ())), preferred_element_type=f32)


def _dot_tn(a, b):
    return lax.dot_general(a, b, (((0,), (0,)), ((), ())), preferred_element_type=f32)


def _dot_exact(a, b):
    return jnp.dot(a, b, preferred_element_type=f32, precision=lax.Precision.HIGHEST)


def _sigmoid(x):
    return jax.nn.sigmoid(x)


def _silu(x):
    return x * _sigmoid(x)


def _dsilu(x):
    s = _sigmoid(x)
    return s * (1.0 + x * (1.0 - s))


_GK = 0.7978845608028654
_GC = 0.044715


def _gelu(x):
    return 0.5 * x * (1.0 + jnp.tanh(_GK * (x + _GC * x * x * x)))


def _dgelu(x):
    t = jnp.tanh(_GK * (x + _GC * x * x * x))
    return 0.5 * (1.0 + t) + 0.5 * x * (1.0 - t * t) * _GK * (1.0 + 3.0 * _GC * x * x)


def _softplus(x):
    return jnp.maximum(x, 0.0) + jnp.log1p(jnp.exp(-jnp.abs(x)))


def _tile(n, cap):
    if n <= cap:
        return n
    best = None
    for t in range(128, cap + 1, 128):
        if n % t == 0:
            best = t
    assert best is not None, (n, cap)
    return best


def _params(sem):
    return pltpu.CompilerParams(dimension_semantics=sem, vmem_limit_bytes=VMEM_LIMIT)


def _exch_ops(src_ref, out_ref, send_sems, recv_sems, local_sem, gather):
    x, y, c = lax.axis_index("x"), lax.axis_index("y"), lax.axis_index("c")
    me = 4 * x + 2 * y + c

    def peer(j):
        px = 1 - x if (j >> 2) & 1 else x
        py = 1 - y if (j >> 1) & 1 else y
        pc = 1 - c if j & 1 else c
        return (px, py, pc), 4 * px + 2 * py + pc

    def copy(j, landing):
        dev, idx = peer(j)
        return pltpu.make_async_remote_copy(
            src_ref=src_ref if gather else src_ref.at[idx], dst_ref=out_ref.at[idx] if landing else out_ref.at[me],
            send_sem=send_sems.at[j - 1], recv_sem=recv_sems.at[j - 1], device_id=dev, device_id_type=pl.DeviceIdType.MESH)

    mine = pltpu.make_async_copy(src_ref if gather else src_ref.at[me], out_ref.at[me], local_sem)

    def start():
        mine.start()
        for j in range(1, NDEV):
            copy(j, False).start()

    def finish():
        for j in range(1, NDEV):
            copy(j, True).wait_recv()
        for j in range(1, NDEV):
            copy(j, False).wait_send()
        mine.wait()

    return start, finish


def _exch_shape(src, gather):
    return jax.ShapeDtypeStruct((NDEV,) + tuple(src.shape if gather else src.shape[1:]), src.dtype)


_EXCH_SEMS = [pltpu.SemaphoreType.DMA((NDEV - 1,)), pltpu.SemaphoreType.DMA((NDEV - 1,)), pltpu.SemaphoreType.DMA]


def _carry_call(body, name, grid, in_specs, out_specs, out_shape, scratch_shapes, args, carry=()):
    n_in, n_out, n_sc, nx = len(in_specs), len(out_specs), len(scratch_shapes), len(carry)
    sem = ("arbitrary",) * len(grid)
    if nx == 0:
        outs = pl.pallas_call(body, name=name, grid=grid, in_specs=in_specs, out_specs=out_specs, out_shape=out_shape,
                              scratch_shapes=scratch_shapes, compiler_params=_params(sem))(*args)
        return list(outs), []

    def wrapped(*refs):
        ins, srcs = refs[:n_in], refs[n_in:n_in + nx]
        outs, dsts = refs[n_in + nx:n_in + nx + n_out], refs[n_in + nx + n_out:n_in + 2 * nx + n_out]
        scratch, sems = refs[n_in + 2 * nx + n_out:n_in + 2 * nx + n_out + n_sc], refs[n_in + 2 * nx + n_out + n_sc:]
        ops = [_exch_ops(srcs[i], dsts[i], sems[3 * i], sems[3 * i + 1], sems[3 * i + 2], carry[i][1]) for i in range(nx)]
        first = functools.reduce(jnp.logical_and, [pl.program_id(d) == 0 for d in range(len(grid))])
        last = functools.reduce(jnp.logical_and, [pl.program_id(d) == grid[d] - 1 for d in range(len(grid))])

        @pl.when(first)
        def _():
            for start, _ in ops:
                start()

        body(*ins, *outs, *scratch)

        @pl.when(last)
        def _():
            for _, finish in ops:
                finish()

    hbm = pl.BlockSpec(memory_space=pl.ANY)
    outs = pl.pallas_call(
        wrapped, name=name, grid=grid, in_specs=list(in_specs) + [hbm] * nx, out_specs=list(out_specs) + [hbm] * nx,
        out_shape=list(out_shape) + [_exch_shape(s, g) for s, g in carry], scratch_shapes=list(scratch_shapes) + _EXCH_SEMS * nx,
        compiler_params=_params(sem))(*args, *[s for s, _ in carry])
    return list(outs[:n_out]), list(outs[n_out:])


def _mm(a, b, mode, name, acc=None, out_dtype=f32, tm_cap=1024, tn_cap=1280, tk_cap=1280, carry=()):
    if mode == "nn":
        (M, K), (K2, N) = a.shape, b.shape
    elif mode == "nt":
        (M, K), (N, K2) = a.shape, b.shape
    else:
        (K, M), (K2, N) = a.shape, b.shape
        tk_cap = min(tk_cap, 512)
    assert K == K2, (a.shape, b.shape, mode)
    tm, tn, tk = _tile(M, tm_cap), _tile(N, tn_cap), _tile(K, tk_cap)
    nk = K // tk
    if mode == "nn":
        a_spec = pl.BlockSpec((tm, tk), lambda i, j, k: (i, k))
        b_spec = pl.BlockSpec((tk, tn), lambda i, j, k: (k, j))
        dot = _dot
    elif mode == "nt":
        a_spec = pl.BlockSpec((tm, tk), lambda i, j, k: (i, k))
        b_spec = pl.BlockSpec((tn, tk), lambda i, j, k: (j, k))
        dot = _dot_nt
    else:
        a_spec = pl.BlockSpec((tk, tm), lambda i, j, k: (k, i))
        b_spec = pl.BlockSpec((tk, tn), lambda i, j, k: (k, j))
        dot = _dot_tn
    o_spec = pl.BlockSpec((tm, tn), lambda i, j, k: (i, j))
    has_acc = acc is not None

    def body(*refs):
        if has_acc:
            a_ref, b_ref, c_ref, o_ref, acc_ref = refs
        else:
            a_ref, b_ref, o_ref, acc_ref = refs
        k = pl.program_id(2)

        @pl.when(k == 0)
        def _():
            if has_acc:
                acc_ref[...] = c_ref[...]
            else:
                acc_ref[...] = jnp.zeros_like(acc_ref)

        acc_ref[...] += dot(_bf(a_ref[...]), _bf(b_ref[...]))

        @pl.when(k == nk - 1)
        def _():
            o_ref[...] = acc_ref[...].astype(out_dtype)

    in_specs = [a_spec, b_spec] + ([o_spec] if has_acc else [])
    args = (a, b) + ((acc,) if has_acc else ())
    (out,), exchanged = _carry_call(body, name, (M // tm, N // tn, nk), in_specs, [o_spec], [jax.ShapeDtypeStruct((M, N), out_dtype)],
                                    [pltpu.VMEM((tm, tn), f32)], args, carry)
    return (out, exchanged) if carry else out


def _row_spec(ts, w, col=0):
    return pl.BlockSpec((ts, w), lambda i: (i, col))


def _full_spec(shape):
    nd = len(shape)
    return pl.BlockSpec(shape, lambda i: (0,) * nd)


def _modulate(x, nw, shift, scale, name):
    T = x.shape[0]
    ts = _tile(T, 512)

    def body(x_ref, nw_ref, sh_ref, sc_ref, h_ref):
        xv = x_ref[...]
        r = lax.rsqrt(jnp.mean(xv * xv, axis=-1, keepdims=True) + EPS)
        h_ref[...] = ((xv * r * nw_ref[...]) * (1.0 + sc_ref[...]) + sh_ref[...]).astype(h_ref.dtype)

    return pl.pallas_call(
        body, name=name, grid=(T // ts,),
        in_specs=[_row_spec(ts, D), _full_spec((1, D)), _full_spec((1, D)), _full_spec((1, D))],
        out_specs=_row_spec(ts, D), out_shape=jax.ShapeDtypeStruct((T, D), _MXU),
        compiler_params=_params(("arbitrary",)))(x, nw, shift, scale)


def _modulate_bwd(dh, dxout, x, nw, scale, name):
    T = x.shape[0]
    ts = _tile(T, 512)

    def body(dh_ref, dxo_ref, x_ref, nw_ref, sc_ref, dx_ref, acc_ref):
        @pl.when(pl.program_id(0) == 0)
        def _():
            acc_ref[...] = jnp.zeros_like(acc_ref)

        xv, dh_v = x_ref[...], dh_ref[...]
        r = lax.rsqrt(jnp.mean(xv * xv, axis=-1, keepdims=True) + EPS)
        xn = xv * r
        hn = xn * nw_ref[...]
        dhn = dh_v * (1.0 + sc_ref[...])
        acc_ref[0:1, :] += jnp.sum(dh_v, axis=0, keepdims=True)
        acc_ref[1:2, :] += jnp.sum(dh_v * hn, axis=0, keepdims=True)
        acc_ref[2:3, :] += jnp.sum(dhn * xn, axis=0, keepdims=True)
        dxn = dhn * nw_ref[...]
        dx_ref[...] = dxo_ref[...] + r * (dxn - xn * jnp.mean(dxn * xn, axis=-1, keepdims=True))

    return pl.pallas_call(
        body, name=name, grid=(T // ts,),
        in_specs=[_row_spec(ts, D), _row_spec(ts, D), _row_spec(ts, D), _full_spec((1, D)), _full_spec((1, D))],
        out_specs=[_row_spec(ts, D), _full_spec((8, D))],
        out_shape=[jax.ShapeDtypeStruct((T, D), f32), jax.ShapeDtypeStruct((8, D), f32)],
        compiler_params=_params(("arbitrary",)))(dh, dxout, x, nw, scale)


def _final_loss(x, tgt, fw, name):
    T = x.shape[0]
    ts = _tile(T, 512)

    def body(x_ref, t_ref, fw_ref, dx_ref, acc_ref):
        @pl.when(pl.program_id(0) == 0)
        def _():
            acc_ref[...] = jnp.zeros_like(acc_ref)

        xv = x_ref[...]
        r = lax.rsqrt(jnp.mean(xv * xv, axis=-1, keepdims=True) + EPS)
        xn = xv * r
        e = xn * fw_ref[...] - t_ref[...]
        dy = e * (1.0 / D)
        acc_ref[0:1, :] += jnp.sum(dy * xn, axis=0, keepdims=True)
        acc_ref[1:2, :] += jnp.sum(jnp.sum(e * e, axis=0, keepdims=True), axis=1, keepdims=True)
        dxn = dy * fw_ref[...]
        dx_ref[...] = r * (dxn - xn * jnp.mean(dxn * xn, axis=-1, keepdims=True))

    return pl.pallas_call(
        body, name=name, grid=(T // ts,),
        in_specs=[_row_spec(ts, D), _row_spec(ts, D), _full_spec((1, D))],
        out_specs=[_row_spec(ts, D), _full_spec((8, D))],
        out_shape=[jax.ShapeDtypeStruct((T, D), f32), jax.ShapeDtypeStruct((8, D), f32)],
        compiler_params=_params(("arbitrary",)))(x, tgt, fw)


def _tril(n):
    return lax.broadcasted_iota(jnp.int32, (n, n), 0) >= lax.broadcasted_iota(jnp.int32, (n, n), 1)


def _gm_chunk_fwd(u, v, z, lw, lb, ws_ref, bsx):
    gu, gv = _gelu(u), _gelu(v)
    mu = jnp.mean(gv, axis=-1, keepdims=True)
    cen = gv - mu
    rstd = lax.rsqrt(jnp.mean(cen * cen, axis=-1, keepdims=True) + EPS)
    vhat = cen * rstd
    vn = _bf(vhat * lw + lb)
    tri = _tril(Q)
    mixed = jnp.concatenate(
        [_dot(_bf(jnp.where(tri, ws_ref[g], 0.0)), vn[:, g * Q:(g + 1) * Q]) for g in range(NG)], axis=1) + bsx
    return gu, vhat, rstd, vn, mixed


def _gmlp_fwd(pA, lw, lb, ws, bsx, name):
    T = pA.shape[0]
    ts = _tile(T, 512)

    def body(u_ref, v_ref, z_ref, lw_ref, lb_ref, ws_ref, bsx_ref, y_ref):
        def chunk(ci, carry):
            rows = pl.ds(pl.multiple_of(ci * Q, Q), Q)
            u, v, z = u_ref[rows, :].astype(f32), v_ref[rows, :].astype(f32), z_ref[rows, :].astype(f32)
            gu, _, _, _, mixed = _gm_chunk_fwd(u, v, z, lw_ref[...], lb_ref[...], ws_ref, bsx_ref[...])
            y_ref[rows, :] = (gu * mixed * _silu(z)).astype(y_ref.dtype)
            return carry

        lax.fori_loop(0, ts // Q, chunk, 0)

    return pl.pallas_call(
        body, name=name, grid=(T // ts,),
        in_specs=[_row_spec(ts, D, 0), _row_spec(ts, D, 1), _row_spec(ts, D, 2), _full_spec((1, D)), _full_spec((1, D)),
                  _full_spec((NG, Q, Q)), _full_spec((Q, D))],
        out_specs=_row_spec(ts, D), out_shape=jax.ShapeDtypeStruct((T, D), _MXU),
        compiler_params=_params(("arbitrary",)))(pA, pA, pA, lw, lb, ws, bsx)


def _gmlp_bwd(dya, pA, lw, lb, ws, bsx, name):
    T = pA.shape[0]
    ts = _tile(T, 512)

    def body(dy_ref, u_ref, v_ref, z_ref, lw_ref, lb_ref, ws_ref, bsx_ref, dp_ref, acc_ref, dws_ref, dbs_ref):
        @pl.when(pl.program_id(0) == 0)
        def _():
            acc_ref[...] = jnp.zeros_like(acc_ref)
            dws_ref[...] = jnp.zeros_like(dws_ref)
            dbs_ref[...] = jnp.zeros_like(dbs_ref)

        tri = _tril(Q)

        def chunk(ci, carry):
            rows = pl.ds(pl.multiple_of(ci * Q, Q), Q)
            u, v, z, dy = u_ref[rows, :].astype(f32), v_ref[rows, :].astype(f32), z_ref[rows, :].astype(f32), dy_ref[rows, :]
            gu, vhat, rstd, vn, mixed = _gm_chunk_fwd(u, v, z, lw_ref[...], lb_ref[...], ws_ref, bsx_ref[...])
            sz = _silu(z)
            dp_ref[rows, 0:D] = (dy * mixed * sz * _dgelu(u)).astype(dp_ref.dtype)
            dp_ref[rows, 2 * D:3 * D] = (dy * gu * mixed * _dsilu(z)).astype(dp_ref.dtype)
            dmixed = dy * gu * sz
            dbs_ref[...] += dmixed
            dmb = _bf(dmixed)
            dvn_parts = []
            for g in range(NG):
                cols = slice(g * Q, (g + 1) * Q)
                wg = _bf(jnp.where(tri, ws_ref[g], 0.0))
                dvn_parts.append(_dot_tn(wg, dmb[:, cols]))
                dws_ref[g] += jnp.where(tri, _dot_nt(dmb[:, cols], vn[:, cols]), 0.0)
            dvn = jnp.concatenate(dvn_parts, axis=1)
            acc_ref[0:1, :] += jnp.sum(dvn * vhat, axis=0, keepdims=True)
            acc_ref[1:2, :] += jnp.sum(dvn, axis=0, keepdims=True)
            dvh = dvn * lw_ref[...]
            dgv = rstd * (dvh - jnp.mean(dvh, axis=-1, keepdims=True) - vhat * jnp.mean(dvh * vhat, axis=-1, keepdims=True))
            dp_ref[rows, D:2 * D] = (dgv * _dgelu(v)).astype(dp_ref.dtype)
            return carry

        lax.fori_loop(0, ts // Q, chunk, 0)

    return pl.pallas_call(
        body, name=name, grid=(T // ts,),
        in_specs=[_row_spec(ts, D), _row_spec(ts, D, 0), _row_spec(ts, D, 1), _row_spec(ts, D, 2), _full_spec((1, D)),
                  _full_spec((1, D)), _full_spec((NG, Q, Q)), _full_spec((Q, D))],
        out_specs=[_row_spec(ts, WA), _full_spec((8, D)), _full_spec((NG, Q, Q)), _full_spec((Q, D))],
        out_shape=[jax.ShapeDtypeStruct((T, WA), _MXU), jax.ShapeDtypeStruct((8, D), f32),
                   jax.ShapeDtypeStruct((NG, Q, Q), f32), jax.ShapeDtypeStruct((Q, D), f32)],
        compiler_params=_params(("arbitrary",)))(dya, pA, pA, pA, lw, lb, ws, bsx)


def _head_maps():
    h = lax.broadcasted_iota(jnp.int32, (128, DI), 0)
    ch = lax.broadcasted_iota(jnp.int32, (128, DI), 1)
    ex = (ch // P == h).astype(_MXU)
    return ex, ex.T


def _split(v, parts):
    out = []
    for _ in range(parts - 1):
        p = _bf(v)
        out.append(p)
        v = v - p.astype(f32)
    out.append(_bf(v))
    return out


def _expand(v, ex_ref, parts):
    acc = None
    for p in _split(v, parts):
        t = _dot(p, ex_ref[...])
        acc = t if acc is None else acc + t
    return acc


def _reduce(v, rd_ref):
    hi, lo = _split(v, 2)
    return _dot(hi, rd_ref[...]) + _dot(lo, rd_ref[...])


def _ssd_time(pb_ref, dtb_ref, alog_ref):
    xdt = pb_ref[:, OFF_DT:OFF_DT + 128] + dtb_ref[...]
    dt = _softplus(xdt)
    a = -jnp.exp(alog_ref[...])
    cs = _dot_exact(_tril(Q).astype(f32), dt * a)
    return xdt, dt, a, cs


def _head_mask(r):
    return lax.broadcasted_iota(jnp.int32, (Q, GRP), 1) // P == r


def _ssd_group_fwd(g, xa_ref, s_prev, cs, cs_t, dtx_ref, csx_ref, dsk_ref):
    cols = slice(g * GRP, (g + 1) * GRP)
    xs = xa_ref[:, cols]
    bb = _bf(xa_ref[:, DI + g * NS:DI + (g + 1) * NS])
    cb = _bf(xa_ref[:, DI + NG * NS + g * NS:DI + NG * NS + (g + 1) * NS])
    gm = _dot_nt(cb, bb)
    xd = xs * dtx_ref[:, cols]
    csx = csx_ref[:, cols]
    csl = csx_ref[Q - 1:Q, cols]
    tri = _tril(Q)
    lms = [jnp.exp(jnp.where(tri, cs[:, HPG * g + r:HPG * g + r + 1] - cs_t[HPG * g + r:HPG * g + r + 1, :], NEG))
           for r in range(HPG)]
    mfs = [gm * lm for lm in lms]
    mcat = jnp.concatenate([_bf(m) for m in mfs], axis=1)
    xbd = jnp.concatenate([_bf(jnp.where(_head_mask(r), xd, 0.0)) for r in range(HPG)], axis=0)
    ydiag = _dot(mcat, xbd)
    yoff = jnp.exp(csx) * _dot(cb, _bf(s_prev))
    y = ydiag + yoff + xs * dsk_ref[:, cols]
    xdd = xd * jnp.exp(csl - csx)
    s_new = s_prev * jnp.exp(csl) + _dot_tn(bb, _bf(xdd))
    return y, s_new, (xs, bb, cb, xd, lms, mfs, mcat, xbd, yoff, xdd, csx, csl)


def _ssd_fwd(pB, cw, cb, dtb, alog, dsk, snw, name, carry=()):
    T = pB.shape[0]
    nc = T // Q
    ex, _ = _head_maps()

    def body(pb_ref, cw_ref, cb_ref, dtb_ref, alog_ref, dsk_ref, snw_ref, ex_ref, y_ref, xc_ref, st_ref,
             s_ref, ext_ref, xa_ref, dtx_ref, csx_ref):
        @pl.when(pl.program_id(0) == 0)
        def _():
            s_ref[...] = jnp.zeros_like(s_ref)
            ext_ref[...] = jnp.zeros_like(ext_ref)

        ext_ref[8:8 + Q, :] = pb_ref[:, OFF_XBC:OFF_XBC + CD]
        for j in range(CD // 512):
            cj = slice(j * 512, (j + 1) * 512)
            e = ext_ref[:, cj]
            xc = cb_ref[:, cj] + cw_ref[KC - 1:KC, cj] * e[8:8 + Q]
            for s in range(1, KC):
                xc = xc + cw_ref[KC - 1 - s:KC - s, cj] * pltpu.roll(e, s, 0)[8:8 + Q]
            xc_ref[:, cj] = xc
            xa_ref[:, cj] = _silu(xc)
        ext_ref[0:8, :] = ext_ref[Q:Q + 8, :]

        _, dt, _, cs = _ssd_time(pb_ref, dtb_ref, alog_ref)
        cs_t = cs.T
        dtx_ref[...] = _expand(dt, ex_ref, 2)
        csx_ref[...] = _expand(cs, ex_ref, 3)
        for g in range(NG):
            s_prev = s_ref[g]
            st_ref[0, g] = s_prev
            y, s_new, _ = _ssd_group_fwd(g, xa_ref, s_prev, cs, cs_t, dtx_ref, csx_ref, dsk_ref)
            s_ref[g] = s_new
            cols = slice(g * GRP, (g + 1) * GRP)
            yz = y * _silu(pb_ref[:, cols])
            rr = lax.rsqrt(jnp.mean(yz * yz, axis=-1, keepdims=True) + EPS)
            y_ref[:, cols] = (yz * rr * snw_ref[:, cols]).astype(y_ref.dtype)

    outs, exchanged = _carry_call(
        body, name, (nc,),
        [_row_spec(Q, WB), _full_spec((8, CD)), _full_spec((1, CD)), _full_spec((1, 128)), _full_spec((1, 128)),
         _full_spec((1, DI)), _full_spec((1, DI)), _full_spec((128, DI))],
        [_row_spec(Q, DI), _row_spec(Q, CD), pl.BlockSpec((1, NG, NS, GRP), lambda i: (i, 0, 0, 0))],
        [jax.ShapeDtypeStruct((T, DI), _MXU), jax.ShapeDtypeStruct((T, CD), f32), jax.ShapeDtypeStruct((nc, NG, NS, GRP), f32)],
        [pltpu.VMEM((NG, NS, GRP), f32), pltpu.VMEM((Q + 8, CD), f32), pltpu.VMEM((Q, CD), f32),
         pltpu.VMEM((Q, DI), f32), pltpu.VMEM((Q, DI), f32)],
        (pB, cw, cb, dtb, alog, dsk, snw, ex), carry)
    return outs + [exchanged]


def _ssd_bwd(dyb, pB, xc, states, cw, dtb, alog, dsk, snw, name, carry=()):
    T = pB.shape[0]
    nc = T // Q
    ex, rd = _head_maps()
    selr = (lax.broadcasted_iota(jnp.int32, (HPG * Q, 128), 0) // Q == lax.broadcasted_iota(jnp.int32, (HPG * Q, 128), 1)).astype(_MXU)

    def body(dy_ref, pb_ref, xc_ref, st_ref, cw_ref, dtb_ref, alog_ref, dsk_ref, snw_ref, ex_ref, rd_ref, selr_ref,
             dp_ref, dcw_ref, dcb_ref, dhd_ref, dcol_ref,
             ds_ref, xa_ref, dxa_ref, dxe_ref, dtx_ref, csx_ref, rcs_ref, rdt_ref, rl_ref, dcs_ref, dcst_ref):
        @pl.when(pl.program_id(0) == 0)
        def _():
            ds_ref[...] = jnp.zeros_like(ds_ref)
            dxe_ref[...] = jnp.zeros_like(dxe_ref)
            dcw_ref[...] = jnp.zeros_like(dcw_ref)
            dcb_ref[...] = jnp.zeros_like(dcb_ref)
            dhd_ref[...] = jnp.zeros_like(dhd_ref)
            dcol_ref[...] = jnp.zeros_like(dcol_ref)
            rl_ref[...] = jnp.zeros_like(rl_ref)
            dcst_ref[...] = jnp.zeros_like(dcst_ref)

        dcs_ref[...] = jnp.zeros_like(dcs_ref)
        for j in range(CD // 512):
            cj = slice(j * 512, (j + 1) * 512)
            xa_ref[:, cj] = _silu(xc_ref[:, cj])
        xdt, dt, a, cs = _ssd_time(pb_ref, dtb_ref, alog_ref)
        cs_t = cs.T
        dtx_ref[...] = _expand(dt, ex_ref, 2)
        csx_ref[...] = _expand(cs, ex_ref, 3)

        for g in range(NG):
            cols = slice(g * GRP, (g + 1) * GRP)
            s_prev = st_ref[0, g]
            sb = _bf(s_prev)
            y, _, (xs, bb, cbm, xd, lms, mfs, mcat, xbd, yoff, xdd, csx, csl) = _ssd_group_fwd(
                g, xa_ref, s_prev, cs, cs_t, dtx_ref, csx_ref, dsk_ref)
            z = pb_ref[:, cols]
            sz = _silu(z)
            yz = y * sz
            rr = lax.rsqrt(jnp.mean(yz * yz, axis=-1, keepdims=True) + EPS)
            nrm = yz * rr
            dyb_g = dy_ref[:, cols]
            dcol_ref[1:2, cols] += jnp.sum(dyb_g * nrm, axis=0, keepdims=True)
            dn = dyb_g * snw_ref[:, cols]
            dyz = rr * (dn - nrm * jnp.mean(dn * nrm, axis=-1, keepdims=True))
            dyv = dyz * sz
            dp_ref[:, cols] = (dyz * y * _dsilu(z)).astype(dp_ref.dtype)
            dcol_ref[0:1, cols] += jnp.sum(dyv * xs, axis=0, keepdims=True)
            dy16 = _bf(dyv)
            dmcat = _dot_nt(dy16, xbd)
            dg = dmcat[:, 0:Q] * lms[0]
            for r in range(1, HPG):
                dg = dg + dmcat[:, r * Q:(r + 1) * Q] * lms[r]
            e_hi, e_lo = _split(jnp.concatenate([dmcat[:, r * Q:(r + 1) * Q] * mfs[r] for r in range(HPG)], axis=1), 2)
            rows = _dot(e_hi, selr_ref[...]) + _dot(e_lo, selr_ref[...])
            dcs_ref[...] += rows if g == 0 else pltpu.roll(rows, HPG * g, 1)
            ones8 = jnp.ones((8, Q), _MXU)
            csum = _dot(ones8, e_hi) + _dot(ones8, e_lo)
            for r in range(HPG):
                dcst_ref[HPG * g + r:HPG * g + r + 1, :] = csum[0:1, r * Q:(r + 1) * Q]
            big = _dot_tn(mcat, dy16)
            dxd_diag = jnp.where(_head_mask(0), big[0:Q], 0.0)
            for r in range(1, HPG):
                dxd_diag = dxd_diag + jnp.where(_head_mask(r), big[r * Q:(r + 1) * Q], 0.0)
            dsn = ds_ref[g]
            dsn16 = _bf(dsn)
            dxdd = _dot(bb, dsn16)
            t_state = dxdd * xdd
            dxd = dxd_diag + dxdd * jnp.exp(csl - csx)
            rcs_ref[:, cols] = dyv * yoff - t_state
            rl_ref[0:1, cols] = (jnp.sum(t_state, axis=0, keepdims=True)
                                 + jnp.exp(csl) * jnp.sum(s_prev * dsn, axis=0, keepdims=True))
            rdt_ref[:, cols] = dxd * xs
            dxa_ref[:, cols] = dyv * dsk_ref[:, cols] + dxd * dtx_ref[:, cols]
            dg16 = _bf(dg)
            dw16 = _bf(dyv * jnp.exp(csx))
            dxa_ref[:, DI + NG * NS + g * NS:DI + NG * NS + (g + 1) * NS] = _dot(dg16, bb) + _dot_nt(dw16, sb)
            dxa_ref[:, DI + g * NS:DI + (g + 1) * NS] = _dot_tn(dg16, cbm) + _dot_nt(_bf(xdd), dsn16)
            ds_ref[g] = dsn * jnp.exp(csl) + _dot_tn(cbm, dw16)

        row = lax.broadcasted_iota(jnp.int32, (Q, 128), 0)
        dcs = (dcs_ref[...] - dcst_ref[...].T + _reduce(rcs_ref[...], rd_ref)
               + jnp.where(row == Q - 1, _reduce(rl_ref[...], rd_ref)[0:1, :], 0.0))
        upper = lax.broadcasted_iota(jnp.int32, (Q, Q), 0) <= lax.broadcasted_iota(jnp.int32, (Q, Q), 1)
        dadt = _dot_exact(upper.astype(f32), dcs)
        valid = lax.broadcasted_iota(jnp.int32, (Q, 128), 1) < NH
        ddt = jnp.where(valid, _reduce(rdt_ref[...], rd_ref) + dadt * a, 0.0)
        ddtr = ddt * _sigmoid(xdt)
        dhd_ref[0:1, :] += jnp.sum(ddtr, axis=0, keepdims=True)
        dhd_ref[1:2, :] += jnp.sum(jnp.where(valid, dadt * dt * a, 0.0), axis=0, keepdims=True)
        dp_ref[:, OFF_DT:OFF_DT + 128] = ddtr.astype(dp_ref.dtype)
        dp_ref[:, OFF_DT + 128:WB] = jnp.zeros((Q, WB - OFF_DT - 128), dp_ref.dtype)
        for j in range(CD // 512):
            cj = slice(j * 512, (j + 1) * 512)
            dxc = dxa_ref[:, cj] * _dsilu(xc_ref[:, cj])
            dxe_ref[0:Q, cj] = dxc
            dcb_ref[0:1, cj] += jnp.sum(dxc, axis=0, keepdims=True)
            raw = pb_ref[:, OFF_XBC + j * 512:OFF_XBC + (j + 1) * 512]
            e = dxe_ref[:, cj]
            dcw_ref[KC - 1:KC, cj] += jnp.sum(dxc * raw, axis=0, keepdims=True)
            dxb = cw_ref[KC - 1:KC, cj] * dxc
            for s in range(1, KC):
                sh = pltpu.roll(e, Q + 8 - s, 0)[0:Q]
                dcw_ref[KC - 1 - s:KC - s, cj] += jnp.sum(sh * raw, axis=0, keepdims=True)
                dxb = dxb + cw_ref[KC - 1 - s:KC - s, cj] * sh
            dp_ref[:, OFF_XBC + j * 512:OFF_XBC + (j + 1) * 512] = dxb.astype(dp_ref.dtype)
        dxe_ref[Q:Q + 8, :] = dxe_ref[0:8, :]

    rev = lambda i: (nc - 1 - i, 0)
    outs, exchanged = _carry_call(
        body, name, (nc,),
        [pl.BlockSpec((Q, DI), rev), pl.BlockSpec((Q, WB), rev), pl.BlockSpec((Q, CD), rev),
         pl.BlockSpec((1, NG, NS, GRP), lambda i: (nc - 1 - i, 0, 0, 0)),
         _full_spec((8, CD)), _full_spec((1, 128)), _full_spec((1, 128)),
         _full_spec((1, DI)), _full_spec((1, DI)), _full_spec((128, DI)), _full_spec((DI, 128)), _full_spec((HPG * Q, 128))],
        [pl.BlockSpec((Q, WB), rev), _full_spec((8, CD)), _full_spec((8, CD)), _full_spec((8, 128)), _full_spec((8, DI))],
        [jax.ShapeDtypeStruct((T, WB), _MXU), jax.ShapeDtypeStruct((8, CD), f32), jax.ShapeDtypeStruct((8, CD), f32),
         jax.ShapeDtypeStruct((8, 128), f32), jax.ShapeDtypeStruct((8, DI), f32)],
        [pltpu.VMEM((NG, NS, GRP), f32), pltpu.VMEM((Q, CD), f32), pltpu.VMEM((Q, CD), f32),
         pltpu.VMEM((Q + 8, CD), f32), pltpu.VMEM((Q, DI), f32), pltpu.VMEM((Q, DI), f32),
         pltpu.VMEM((Q, DI), f32), pltpu.VMEM((Q, DI), f32), pltpu.VMEM((8, DI), f32),
         pltpu.VMEM((Q, 128), f32), pltpu.VMEM((128, Q), f32)],
        (dyb, pB, xc, states, cw, dtb, alog, dsk, snw, ex, rd, selr), carry)
    return outs + [exchanged]


def _merge_fwd(ya, yb, pG, x, gate, wa, wb, wo, name, carry=()):
    T = x.shape[0]
    ts = _tile(T, 256)

    def body(ya_ref, yb_ref, g_ref, x_ref, gate_ref, wa_ref, wb_ref, wo_ref, xo_ref, mg_ref, pa_ref, pb_ref):
        pa = _dot(ya_ref[...], wa_ref[...])
        pb = _dot(yb_ref[...], wb_ref[...])
        merged = _sigmoid(g_ref[:, 0:D].astype(f32)) * pa + _sigmoid(g_ref[:, D:2 * D].astype(f32)) * pb
        mg = _bf(merged)
        xo_ref[...] = x_ref[...] + gate_ref[...] * _dot(mg, wo_ref[...])
        mg_ref[...] = mg
        pa_ref[...] = pa.astype(pa_ref.dtype)
        pb_ref[...] = pb.astype(pb_ref.dtype)

    outs, exchanged = _carry_call(
        body, name, (T // ts,),
        [_row_spec(ts, D), _row_spec(ts, DI), _row_spec(ts, WG), _row_spec(ts, D), _full_spec((1, D)),
         _full_spec((D, D)), _full_spec((DI, D)), _full_spec((D, D))],
        [_row_spec(ts, D)] * 4,
        [jax.ShapeDtypeStruct((T, D), f32), jax.ShapeDtypeStruct((T, D), _MXU), jax.ShapeDtypeStruct((T, D), _MXU),
         jax.ShapeDtypeStruct((T, D), _MXU)],
        [], (ya, yb, pG, x, gate, wa, wb, wo), carry)
    return outs + [exchanged]


def _merge_bwd(dxout, merged, pa, pb, pG, gate, wo, name):
    T = dxout.shape[0]
    ts = _tile(T, 256)

    def body(dx_ref, mg_ref, pa_ref, pb_ref, g_ref, gate_ref, wo_ref, do_ref, dpa_ref, dpb_ref, dg_ref, acc_ref):
        @pl.when(pl.program_id(0) == 0)
        def _():
            acc_ref[...] = jnp.zeros_like(acc_ref)

        dxo = dx_ref[...]
        acc_ref[0:1, :] += jnp.sum(dxo * _dot(mg_ref[...], wo_ref[...]), axis=0, keepdims=True)
        do = _bf(dxo * gate_ref[...])
        do_ref[...] = do
        dmerged = _dot_nt(do, wo_ref[...])
        sa, sb = _sigmoid(g_ref[:, 0:D].astype(f32)), _sigmoid(g_ref[:, D:2 * D].astype(f32))
        dpa_ref[...] = (dmerged * sa).astype(dpa_ref.dtype)
        dpb_ref[...] = (dmerged * sb).astype(dpb_ref.dtype)
        dg_ref[:, 0:D] = (dmerged * pa_ref[...].astype(f32) * sa * (1.0 - sa)).astype(dg_ref.dtype)
        dg_ref[:, D:2 * D] = (dmerged * pb_ref[...].astype(f32) * sb * (1.0 - sb)).astype(dg_ref.dtype)

    return pl.pallas_call(
        body, name=name, grid=(T // ts,),
        in_specs=[_row_spec(ts, D), _row_spec(ts, D), _row_spec(ts, D), _row_spec(ts, D), _row_spec(ts, WG),
                  _full_spec((1, D)), _full_spec((D, D))],
        out_specs=[_row_spec(ts, D), _row_spec(ts, D), _row_spec(ts, D), _row_spec(ts, WG), _full_spec((8, D))],
        out_shape=[jax.ShapeDtypeStruct((T, D), _MXU), jax.ShapeDtypeStruct((T, D), _MXU), jax.ShapeDtypeStruct((T, D), _MXU),
                   jax.ShapeDtypeStruct((T, WG), _MXU), jax.ShapeDtypeStruct((8, D), f32)],
        compiler_params=_params(("arbitrary",)))(dxout, merged, pa, pb, pG, gate, wo)


def _rows_split(r):
    ra, rb = D // NDEV, DI // NDEV
    return r[:, 0:ra].reshape(D, D), r[:, ra:ra + rb].reshape(DI, D), r[:, ra + rb:].reshape(D, D)


def _layer_fwd(x, mod, lp, tag, carry=None):
    carry = carry or {}
    got = {}

    def mm(key, w, **kw):
        if carry.get(key):
            out, got[key] = _mm(h, w, "nn", f"{key}_{tag}", carry=carry[key], **kw)
            return out
        return _mm(h, w, "nn", f"{key}_{tag}", **kw)

    h = _modulate(x, lp["nw"], mod[0:1], mod[1:2], f"modulate_{tag}")
    pA = mm("proj_gm", lp["w_gm"], out_dtype=_MXU)
    pB = mm("proj_ssd", lp["w_ssd"])
    pG = mm("proj_gate", lp["w_g"], out_dtype=_MXU)
    if "wa" not in lp:
        lp = dict(lp)
        lp["wa"], lp["wb"], lp["wo"] = _rows_split(jnp.concatenate([got["proj_gm"][0], got["proj_gate"][0]], axis=1))
    ya = _gmlp_fwd(pA, lp["lw"], lp["lb"], lp["ws"], lp["bsx"], f"gmlp_fwd_{tag}")
    yb, xc, states, got["ssd_fwd"] = _ssd_fwd(pB, lp["cw"], lp["cb"], lp["dtb"], lp["alog"], lp["dsk"], lp["snw"], f"ssd_fwd_{tag}",
                                              carry.get("ssd_fwd", ()))
    xo, merged, pa, pb, _ = _merge_fwd(ya, yb, pG, x, mod[2:3], lp["wa"], lp["wb"], lp["wo"], f"merge_fwd_{tag}")
    return xo, dict(x=x, h=h, pA=pA, pB=pB, pG=pG, ya=ya, yb=yb, xc=xc, states=states, merged=merged, pa=pa, pb=pb), got, lp


def _dh_modulate_bwd(dpG, dpA, dpB, w_g, w_gm, w_ssd, dxout, x, nw, scale, name, carry=()):
    T = x.shape[0]
    tm = _tile(T, 1024)
    fam = [(WG, _tile(WG, 1024)), (WA, _tile(WA, 1024)), (WB, _tile(WB, 640))]
    steps = [w // k for w, k in fam]
    first = [0, steps[0], steps[0] + steps[1]]
    ns = sum(steps)

    def chunk(f):
        return lambda i, s: jnp.clip(s - first[f], 0, steps[f] - 1)

    a_specs = [pl.BlockSpec((tm, fam[f][1]), lambda i, s, c=chunk(f): (i, c(i, s))) for f in range(3)]
    b_specs = [pl.BlockSpec((D, fam[f][1]), lambda i, s, c=chunk(f): (0, c(i, s))) for f in range(3)]
    tok = pl.BlockSpec((tm, D), lambda i, s: (i, 0))
    vec = pl.BlockSpec((1, D), lambda i, s: (0, 0))

    def body(ag, aa, ab, bg, ba, bb, dxo_ref, x_ref, nw_ref, sc_ref, dx_ref, sum_ref, acc_ref):
        i, s = pl.program_id(0), pl.program_id(1)

        @pl.when(jnp.logical_and(i == 0, s == 0))
        def _():
            sum_ref[...] = jnp.zeros_like(sum_ref)

        @pl.when(s == 0)
        def _():
            acc_ref[...] = jnp.zeros_like(acc_ref)

        @pl.when(s < first[1])
        def _():
            acc_ref[...] += _dot_nt(ag[...], bg[...])

        @pl.when(jnp.logical_and(s >= first[1], s < first[2]))
        def _():
            acc_ref[...] += _dot_nt(aa[...], ba[...])

        @pl.when(s >= first[2])
        def _():
            acc_ref[...] += _dot_nt(ab[...], bb[...])

        @pl.when(s == ns - 1)
        def _():
            xv, dh_v = x_ref[...], acc_ref[...]
            r = lax.rsqrt(jnp.mean(xv * xv, axis=-1, keepdims=True) + EPS)
            xn = xv * r
            hn = xn * nw_ref[...]
            dhn = dh_v * (1.0 + sc_ref[...])
            sum_ref[0:1, :] += jnp.sum(dh_v, axis=0, keepdims=True)
            sum_ref[1:2, :] += jnp.sum(dh_v * hn, axis=0, keepdims=True)
            sum_ref[2:3, :] += jnp.sum(dhn * xn, axis=0, keepdims=True)
            dxn = dhn * nw_ref[...]
            dx_ref[...] = dxo_ref[...] + r * (dxn - xn * jnp.mean(dxn * xn, axis=-1, keepdims=True))

    outs, exchanged = _carry_call(
        body, name, (T // tm, ns), a_specs + b_specs + [tok, tok, vec, vec], [tok, pl.BlockSpec((8, D), lambda i, s: (0, 0))],
        [jax.ShapeDtypeStruct((T, D), f32), jax.ShapeDtypeStruct((8, D), f32)], [pltpu.VMEM((tm, D), f32)],
        (dpG, dpA, dpB, w_g, w_gm, w_ssd, dxout, x, nw, scale), carry)
    return outs + [exchanged]


def _win_blocks(g):
    full = jnp.concatenate([g["w_gm"], g["w_ssd"][:, 0:DI + CD], g["w_ssd"][:, DI + CD:DI + CD + NH], g["w_g"]], axis=1)
    return jnp.transpose(full.reshape(D, NDEV, NIN // NDEV), (1, 0, 2))


def _row_blocks(g):
    return jnp.concatenate([g["wa"].reshape(NDEV, D // NDEV, D), g["wb"].reshape(NDEV, DI // NDEV, D),
                            g["wo"].reshape(NDEV, D // NDEV, D)], axis=1)


def _layer_bwd(dxo, sv, mod, lp, tag, carry=(), scatter_own=None):
    do, dpa, dpb, dpG, s_gate = _merge_bwd(dxo, sv["merged"], sv["pa"], sv["pb"], sv["pG"], mod[2:3], lp["wo"], f"merge_bwd_{tag}")
    g = {}
    g["wo"] = _mm(sv["merged"], do, "tn", f"dw_out_{tag}", out_dtype=_WIRE)
    g["wa"] = _mm(sv["ya"], dpa, "tn", f"dw_proj_a_{tag}", out_dtype=_WIRE)
    g["wb"] = _mm(sv["yb"], dpb, "tn", f"dw_proj_b_{tag}", out_dtype=_WIRE)
    dya = _mm(dpa, lp["wa"], "nt", f"dy_a_{tag}")
    dyb = _mm(dpb, lp["wb"], "nt", f"dy_b_{tag}")
    dpA, s_ln, g["ws"], dbsx = _gmlp_bwd(dya, sv["pA"], lp["lw"], lp["lb"], lp["ws"], lp["bsx"], f"gmlp_bwd_{tag}")
    carry = list(carry) + ([(_row_blocks(g), False)] if scatter_own else [])
    dpB, g["cw"], s_cb, s_hd, s_col, got = _ssd_bwd(dyb, sv["pB"], sv["xc"], sv["states"], lp["cw"], lp["dtb"], lp["alog"],
                                                    lp["dsk"], lp["snw"], f"ssd_bwd_{tag}", carry)
    g["lw"], g["lb"] = s_ln[0], s_ln[1]
    g["bs"] = dbsx.reshape(Q, NG, Q).sum(-1).T
    g["cb"] = s_cb[0]
    g["dtb"], g["alog"] = s_hd[0, :NH], s_hd[1, :NH]
    g["dsk"] = s_col[0].reshape(NH, P).sum(-1)
    g["snw"] = s_col[1]
    g["cw"] = g["cw"][0:KC]
    g["w_g"] = _mm(sv["h"], dpG, "tn", f"dw_gate_{tag}", out_dtype=_WIRE)
    g["w_gm"] = _mm(sv["h"], dpA, "tn", f"dw_gm_{tag}", out_dtype=_WIRE)
    g["w_ssd"] = _mm(sv["h"], dpB, "tn", f"dw_ssd_{tag}", out_dtype=_WIRE)
    dx, s_mod, got_dh = _dh_modulate_bwd(dpG, dpA, dpB, lp["w_g"], lp["w_gm"], lp["w_ssd"], dxo, sv["x"], lp["nw"], mod[1:2],
                                         f"dh_{tag}", [(_win_blocks(g), False), (scatter_own(g), True)] if scatter_own else ())
    if scatter_own:
        got, got_own = got[:-1], [got[-1], got_dh[0], got_dh[1]]
    else:
        got_own = []
    g["mod"] = jnp.concatenate([s_mod[0], s_mod[1], s_gate[0]])
    g["nw"] = s_mod[2]
    return dx, g, got, got_own


def _prep_layer(nw, w_in_full, lw, lb, ws, bs, cw_full, cb, dtb, alog, dsk, snw, rows=None):
    z = jnp.zeros((D, WB - (DI + CD + NH)), w_in_full.dtype)
    pad_h = lambda v: jnp.pad(v, (0, 128 - NH)).reshape(1, 128)
    extra = dict(zip(("wa", "wb", "wo"), _rows_split(rows))) if rows is not None else {}
    return dict(
        **extra,
        nw=nw.reshape(1, D),
        w_gm=w_in_full[:, 0:WA],
        w_ssd=jnp.concatenate([w_in_full[:, WA:WA + DI + CD], w_in_full[:, WA + DI + CD:WA + DI + CD + NH], z], axis=1),
        w_g=w_in_full[:, WA + DI + CD + NH:NIN],
        lw=lw.reshape(1, D), lb=lb.reshape(1, D), ws=ws, bsx=jnp.repeat(bs.T, Q, axis=1),
        cw=jnp.pad(cw_full, ((0, 8 - KC), (0, 0))), cb=cb.reshape(1, CD), dtb=pad_h(dtb), alog=pad_h(alog),
        dsk=jnp.repeat(dsk, P).reshape(1, DI), snw=snw.reshape(1, DI))


def _exchange(src, gather, name):
    if not gather:
        assert src.shape[0] == NDEV

    def body(src_ref, out_ref, send_sems, recv_sems, local_sem):
        start, finish = _exch_ops(src_ref, out_ref, send_sems, recv_sems, local_sem, gather)
        start()
        finish()

    return pl.pallas_call(
        body, name=name, out_shape=_exch_shape(src, gather),
        in_specs=[pl.BlockSpec(memory_space=pl.ANY)], out_specs=pl.BlockSpec(memory_space=pl.ANY), scratch_shapes=_EXCH_SEMS)(src)


def _gather2(src, name):
    def body(src_ref, out_ref, send_sems, recv_sems, local_sem):
        x, y, c = lax.axis_index("x"), lax.axis_index("y"), lax.axis_index("c")
        me, sibling = (x, y, c), (x, y, 1 - c)
        chips = [(1 - x, y), (x, 1 - y), (1 - x, 1 - y)]

        def slot(px, py, pc):
            return out_ref.at[4 * px + 2 * py + pc]

        def copy(k, block, to, src=None):
            return pltpu.make_async_remote_copy(
                src_ref=slot(*block) if src is None else src, dst_ref=slot(*block), send_sem=send_sems.at[k],
                recv_sem=recv_sems.at[k], device_id=to, device_id_type=pl.DeviceIdType.MESH)

        mine = pltpu.make_async_copy(src_ref, slot(*me), local_sem)
        mine.start()
        first = [copy(0, me, sibling, src=src_ref)] + [copy(1 + j, me, (*chip, c), src=src_ref) for j, chip in enumerate(chips)]
        for cp in first:
            cp.start()
        passed = [copy(4 + j, (*chip, c), sibling) for j, chip in enumerate(chips)]
        for j, chip in enumerate(chips):
            copy(1 + j, (*chip, c), me).wait_recv()
            passed[j].start()
        copy(0, sibling, me).wait_recv()
        for j, chip in enumerate(chips):
            copy(4 + j, (*chip, 1 - c), me).wait_recv()
        for cp in first + passed:
            cp.wait_send()
        mine.wait()

    return pl.pallas_call(
        body, name=name, out_shape=_exch_shape(src, True),
        in_specs=[pl.BlockSpec(memory_space=pl.ANY)], out_specs=pl.BlockSpec(memory_space=pl.ANY), scratch_shapes=_EXCH_SEMS)(src)


def _mod_dist(c8, ada_w, ada_b_cols, name):
    L, _, AW = ada_w.shape

    def body(c_ref, w_ref, b_ref, parts_ref, sc_ref, call_ref, mine_ref, send1, recv1, send2, recv2):
        x, y, c = lax.axis_index("x"), lax.axis_index("y"), lax.axis_index("c")
        me = 4 * x + 2 * y + c

        def peer(j):
            px = 1 - x if (j >> 2) & 1 else x
            py = 1 - y if (j >> 1) & 1 else y
            pc = 1 - c if j & 1 else c
            return (px, py, pc), 4 * px + 2 * py + pc

        def copy(j, src, dst, sems, landing):
            dev, idx = peer(j)
            return pltpu.make_async_remote_copy(
                src_ref=src, dst_ref=dst.at[idx] if landing else dst.at[me], send_sem=sems[0].at[j - 1], recv_sem=sems[1].at[j - 1],
                device_id=dev, device_id_type=pl.DeviceIdType.MESH)

        def all_to_all(src, dst, sems):
            for j in range(1, NDEV):
                copy(j, src, dst, sems, False).start()
            for j in range(1, NDEV):
                copy(j, src, dst, sems, True).wait_recv()
            for j in range(1, NDEV):
                copy(j, src, dst, sems, False).wait_send()

        call_ref[me] = c_ref[...]
        all_to_all(c_ref, call_ref, (send1, recv1))
        row = lax.broadcasted_iota(jnp.int32, (8, D), 0)
        cm = jnp.zeros((8, D), f32)
        for k in range(NDEV):
            cm = jnp.where(row == k, call_ref[k], cm)
        sc = _silu(cm)
        sc_ref[...] = sc
        for l in range(L):
            mine_ref[l] = _dot(_bf(sc), w_ref[l]) + b_ref[l]
        parts_ref[me] = mine_ref[...]
        all_to_all(mine_ref, parts_ref, (send2, recv2))

    vmem = pl.BlockSpec(memory_space=pltpu.VMEM)
    sems = pltpu.SemaphoreType.DMA((NDEV - 1,))
    return pl.pallas_call(
        body, name=name, in_specs=[vmem, vmem, vmem], out_specs=[vmem, vmem],
        out_shape=[jax.ShapeDtypeStruct((NDEV, L, 8, AW), f32), jax.ShapeDtypeStruct((8, D), f32)],
        scratch_shapes=[pltpu.VMEM((NDEV, 8, D), f32), pltpu.VMEM((L, 8, AW), f32), sems, sems, sems, sems])(c8, ada_w, ada_b_cols)


def _ada_w_grad(sc_all, dmod_cols, name):
    W = dmod_cols.shape[1]

    def body(s_ref, d_ref, o_ref):
        o_ref[...] = lax.dot_general(s_ref[...], d_ref[...], (((0,), (0,)), ((), ())), preferred_element_type=f32,
                                     precision=lax.Precision.HIGHEST)

    return pl.pallas_call(body, name=name, out_shape=jax.ShapeDtypeStruct((D, W), f32))(sc_all, dmod_cols)


def _adamw_math(w, g, m, v):
    m = ADAM_B1 * m + (1.0 - ADAM_B1) * g
    v = ADAM_B2 * v + (1.0 - ADAM_B2) * (g * g)
    m_hat = m / (1.0 - ADAM_B1 ** ADAM_STEP)
    v_hat = v / (1.0 - ADAM_B2 ** ADAM_STEP)
    delta = -ADAM_LR * (m_hat / (jnp.sqrt(v_hat) + ADAM_EPS) + ADAM_WD * w)
    return delta, m, v


def _sum_adamw(recvs, w, m, v, name):
    nl = len(recvs)
    n, R, C = recvs[0].shape
    tr = _tile(R, 128 if C > 1024 else 256)
    nr = R // tr

    def body(*refs):
        r_refs, (w_ref, m_ref, v_ref, g_ref, d_ref, nm_ref, nv_ref) = refs[:nl], refs[nl:]
        for li in range(nl):
            @pl.when(pl.program_id(0) == li)
            def _(r_ref=r_refs[li]):
                g = r_ref[0].astype(f32)
                for k in range(1, n):
                    g = g + r_ref[k].astype(f32)
                g_ref[...] = g
                d_ref[...], nm_ref[...], nv_ref[...] = _adamw_math(w_ref[...], g, m_ref[...], v_ref[...])

    spec = pl.BlockSpec((tr, C), lambda l, i: (l * nr + i, 0))
    r_specs = [pl.BlockSpec((n, tr, C), lambda l, i, li=li: (0, jnp.clip(i + (l - li) * nr, 0, nr - 1), 0)) for li in range(nl)]
    return pl.pallas_call(
        body, name=name, grid=(nl, nr), in_specs=r_specs + [spec, spec, spec], out_specs=[spec] * 4,
        out_shape=[jax.ShapeDtypeStruct((nl * R, C), f32)] * 4, compiler_params=_params(("arbitrary", "arbitrary")))(*recvs, w, m, v)


def _pack(arrays, rows):
    flat = []
    for a in arrays:
        a = a.reshape(-1).astype(f32)
        flat.append(jnp.pad(a, (0, (-a.shape[0]) % 128)))
    flat = jnp.concatenate(flat)
    return jnp.pad(flat, (0, rows * 128 - flat.shape[0])).reshape(rows, 128)


def _unpack(slab, shapes):
    flat = slab.reshape(-1)
    out, off = [], 0
    for s in shapes:
        n = 1
        for d in s:
            n *= d
        out.append(flat[off:off + n].reshape(s))
        off += n + (-n) % 128
    return out


def kernel(x, c, ada_w, ada_b, norm_w, w_in, gm_ln_w, gm_ln_b, gm_ws, gm_bs, conv_w, conv_b, dt_bias, a_log, d_skip, ssm_norm_w, w_proj_a, w_proj_b, w_out, final_norm_w, loss_target, m_ada_w, m_ada_b, m_norm_w, m_w_in, m_gm_ln_w, m_gm_ln_b, m_gm_ws, m_gm_bs, m_conv_w, m_conv_b, m_dt_bias, m_a_log, m_d_skip, m_ssm_norm_w, m_w_proj_a, m_w_proj_b, m_w_out, m_final_norm_w, v_ada_w, v_ada_b, v_norm_w, v_w_in, v_gm_ln_w, v_gm_ln_b, v_gm_ws, v_gm_bs, v_conv_w, v_conv_b, v_dt_bias, v_a_log, v_d_skip, v_ssm_norm_w, v_w_proj_a, v_w_proj_b, v_w_out, v_final_norm_w):
    L = 2
    me = 4 * lax.axis_index("x") + 2 * lax.axis_index("y") + lax.axis_index("c")
    W = dict(ada_w=ada_w, ada_b=ada_b, norm_w=norm_w, w_in=w_in, gm_ln_w=gm_ln_w, gm_ln_b=gm_ln_b, gm_ws=gm_ws, gm_bs=gm_bs,
             conv_w=conv_w, conv_b=conv_b, dt_bias=dt_bias, a_log=a_log, d_skip=d_skip, ssm_norm_w=ssm_norm_w, w_proj_a=w_proj_a,
             w_proj_b=w_proj_b, w_out=w_out, final_norm_w=final_norm_w)
    M = dict(ada_w=m_ada_w, ada_b=m_ada_b, norm_w=m_norm_w, w_in=m_w_in, gm_ln_w=m_gm_ln_w, gm_ln_b=m_gm_ln_b, gm_ws=m_gm_ws,
             gm_bs=m_gm_bs, conv_w=m_conv_w, conv_b=m_conv_b, dt_bias=m_dt_bias, a_log=m_a_log, d_skip=m_d_skip,
             ssm_norm_w=m_ssm_norm_w, w_proj_a=m_w_proj_a, w_proj_b=m_w_proj_b, w_out=m_w_out, final_norm_w=m_final_norm_w)
    V = dict(ada_w=v_ada_w, ada_b=v_ada_b, norm_w=v_norm_w, w_in=v_w_in, gm_ln_w=v_gm_ln_w, gm_ln_b=v_gm_ln_b, gm_ws=v_gm_ws,
             gm_bs=v_gm_bs, conv_w=v_conv_w, conv_b=v_conv_b, dt_bias=v_dt_bias, a_log=v_a_log, d_skip=v_d_skip,
             ssm_norm_w=v_ssm_norm_w, w_proj_a=v_w_proj_a, w_proj_b=v_w_proj_b, w_out=v_w_out, final_norm_w=v_final_norm_w)
    SW = NIN // NDEV
    AW = 3 * D // NDEV
    CW = CD // NDEV
    RA, RB = D // NDEV, DI // NDEV

    wire = lambda a: a.astype(_WIRE)
    rows_of = lambda d, l: jnp.concatenate([d["w_proj_a"][l], d["w_proj_b"][l], d["w_out"][l]], axis=0)
    g_win0 = _gather2(wire(w_in[0]), "gather_w_in_l0")
    g_cw = _exchange(conv_w.reshape(L * KC, CW), True, "gather_conv_w").reshape(NDEV, L, KC, CW)
    parts, sc_all = _mod_dist(jnp.broadcast_to(c, (8, D)), wire(ada_w),
                              lax.dynamic_slice_in_dim(ada_b, me * AW, AW, axis=1).reshape(L, 1, AW), "mod_dist")
    mods = [lax.dynamic_index_in_dim(parts[:, l], me, axis=1, keepdims=False).reshape(3, D) for l in range(L)]
    cols = lambda g: jnp.transpose(g, (1, 0, 2)).reshape(g.shape[1], -1)

    def prep(l, g_win, g_rows):
        return _prep_layer(norm_w[l], cols(g_win), gm_ln_w[l], gm_ln_b[l], gm_ws[l], gm_bs[l], cols(g_cw[:, l]), conv_b[l],
                           dt_bias[l], a_log[l], d_skip[l], ssm_norm_w[l], g_rows)

    half = D // 2
    rows0, rows1 = wire(rows_of(W, 0)), wire(rows_of(W, 1))
    hr = rows0.shape[0] // 2
    h, sv0, got, lp0 = _layer_fwd(x[0], mods[0], prep(0, g_win0, None), "l0",
                                  dict(proj_gm=[(rows0[:hr], True)], proj_gate=[(rows0[hr:], True)],
                                       proj_ssd=[(wire(w_in[1][:half]), True)],
                                       ssd_fwd=[(wire(w_in[1][half:]), True), (rows1, True)]))
    lp1 = prep(1, jnp.concatenate([got["proj_ssd"][0], got["ssd_fwd"][0]], axis=1), got["ssd_fwd"][1])
    h, sv1, _, _ = _layer_fwd(h, mods[1], lp1, "l1")
    dh, s_fin = _final_loss(h, loss_target[0], final_norm_w.reshape(1, D), "final_loss")
    loss = lax.psum(0.5 / D * s_fin[1, 0], ("x", "y", "c"))

    small_l = ["gm_ln_w", "gm_ln_b", "gm_ws", "gm_bs", "conv_b", "dt_bias", "a_log", "d_skip", "ssm_norm_w"]
    small_key = dict(gm_ln_w="lw", gm_ln_b="lb", gm_ws="ws", gm_bs="bs", conv_b="cb", dt_bias="dtb", a_log="alog", d_skip="dsk",
                     ssm_norm_w="snw")
    shapes_l = [W[n].shape[1:] for n in small_l] + [(KC, CD)]
    shapes_t = [(L, 3 * D), (L, D), (D,)]
    rows_for = lambda sh: -(-sum(-(-functools.reduce(lambda a, b: a * b, s, 1) // 128) for s in sh) // 256) * 256
    n_rows_l, n_rows_t = rows_for(shapes_l), rows_for(shapes_t)
    slab_of = lambda g: _pack([g[small_key[n]] for n in small_l] + [g["cw"]], n_rows_l)

    dh, g1, _, _ = _layer_bwd(dh, sv1, mods[1], lp1, "l1")
    dh, g0, got, got_own = _layer_bwd(dh, sv0, mods[0], lp0, "l0",
                                      [(_win_blocks(g1), False), (_row_blocks(g1), False), (slab_of(g1), True)], slab_of)
    grads = [g0, g1]
    grad_x = dh[None]
    recv_win = [got_own[1], got[0]]
    recv_rows = [got_own[0], got[1]]
    slab_all = [got_own[2], got[2]]
    st = lambda key: jnp.stack([g[key] for g in grads])
    slab_t = _exchange(_pack([st("mod"), st("nw"), s_fin[0]], n_rows_t), True, "gather_small_tail")

    o_win = [o.reshape(L, D, SW) for o in _sum_adamw(recv_win, w_in.reshape(L * D, SW), m_w_in.reshape(L * D, SW),
                                                     v_w_in.reshape(L * D, SW), "adamw_w_in")]
    rows_all = lambda d: jnp.concatenate([rows_of(d, l) for l in range(L)], axis=0)
    o_rows = [o.reshape(L, 2 * RA + RB, D) for o in _sum_adamw(recv_rows, rows_all(W), rows_all(M), rows_all(V), "adamw_w_rows")]
    o_pa = [o[:, 0:RA] for o in o_rows]
    o_pb = [o[:, RA:RA + RB] for o in o_rows]
    o_po = [o[:, RA + RB:] for o in o_rows]

    def slab_params(d):
        return jnp.concatenate([_pack([d[n][l] for n in small_l] + [jnp.zeros((KC, CD), f32)], n_rows_l) for l in range(L)], axis=0)

    o_small = [[_unpack(o[l * n_rows_l:(l + 1) * n_rows_l], shapes_l) for l in range(L)]
               for o in _sum_adamw(slab_all, slab_params(W), slab_params(M), slab_params(V), "adamw_small")]
    small_out = {n: [jnp.stack([o_small[k][l][i] for l in range(L)]) for k in range(4)] for i, n in enumerate(small_l)}
    tail_params = lambda d: _pack([d["ada_b"], d["norm_w"], d["final_norm_w"]], n_rows_t)
    o_tail = [_unpack(o, shapes_t) for o in _sum_adamw([slab_t], tail_params(W), tail_params(M), tail_params(V), "adamw_small_tail")]
    for i, n in enumerate(["ada_b", "norm_w", "final_norm_w"]):
        small_out[n] = [o_tail[k][i] for k in range(4)]

    g_cw_mine = jnp.stack([lax.dynamic_slice_in_dim(o_small[0][l][len(small_l)], me * CW, CW, axis=1) for l in range(L)])
    o_cw = _sum_adamw([g_cw_mine.reshape(1, L * KC, CW)], conv_w.reshape(L * KC, CW), m_conv_w.reshape(L * KC, CW),
                      v_conv_w.reshape(L * KC, CW), "adamw_conv_w")
    o_cw = [o.reshape(L, KC, CW) for o in o_cw]

    dmod_all = jnp.stack([_unpack(slab_t[k], shapes_t)[0] for k in range(NDEV)])
    dmod_cols = lax.dynamic_slice_in_dim(dmod_all, me * AW, AW, axis=2).reshape(NDEV, L * AW)
    g_ada_w = jnp.transpose(_ada_w_grad(sc_all, dmod_cols, "ada_w_grad").reshape(D, L, AW), (1, 0, 2))
    o_ada = _sum_adamw([g_ada_w.reshape(1, L * D, AW)], ada_w.reshape(L * D, AW), m_ada_w.reshape(L * D, AW),
                       v_ada_w.reshape(L * D, AW), "adamw_ada_w")
    o_ada = [o.reshape(L, D, AW) for o in o_ada]

    big = dict(ada_w=o_ada, w_in=o_win, conv_w=o_cw, w_proj_a=o_pa, w_proj_b=o_pb, w_out=o_po)
    order = ["ada_w", "ada_b", "norm_w", "w_in", "gm_ln_w", "gm_ln_b", "gm_ws", "gm_bs", "conv_w", "conv_b", "dt_bias", "a_log",
             "d_skip", "ssm_norm_w", "w_proj_a", "w_proj_b", "w_out", "final_norm_w"]
    pick = lambda n, k: big[n][k] if n in big else small_out[n][k]
    return (loss, grad_x, *[pick(n, 0) for n in order], *[pick(n, 1) for n in order], *[pick(n, 2) for n in order],
            *[pick(n, 3) for n in order])
```

```python
import functools

import jax
import jax.numpy as jnp
from jax import lax
from jax.experimental import pallas as pl
from jax.experimental.pallas import tpu as pltpu

f32 = jnp.float32
_MXU = jnp.bfloat16
_WIRE = jnp.bfloat16

D = 1024
Q = 128
NG = 8
DI = 2048
NH = 32
P = 64
HPG = 4
NS = 128
KC = 4
CD = 4096
GRP = DI // NG
EPS = 1e-6
NDEV = 8
WA = 3 * D
WB = DI + CD + 256
WG = 2 * D
OFF_XBC = DI
OFF_DT = DI + CD
NIN = 11296
VMEM_LIMIT = 56 * 1024 * 1024
NEG = -1e30

ADAM_LR, ADAM_B1, ADAM_B2, ADAM_EPS, ADAM_WD, ADAM_STEP = 0.001, 0.9, 0.999, 1e-08, 0.01, 10


def _bf(x):
    return x.astype(_MXU)


def _dot(a, b):
    return jnp.dot(a, b, preferred_element_type=f32)


def _dot_nt(a, b):
    return lax.dot_general(a, b, (((1,), (1,)), ((), ())), preferred_element_type=f32)


def _dot_tn(a, b):
    return lax.dot_general(a, b, (((0,), (0,)), ((), ())), preferred_element_type=f32)


def _dot_exact(a, b):
    return jnp.dot(a, b, preferred_element_type=f32, precision=lax.Precision.HIGHEST)


def _sigmoid(x):
    return jax.nn.sigmoid(x)


def _silu(x):
    return x * _sigmoid(x)


def _dsilu(x):
    s = _sigmoid(x)
    return s * (1.0 + x * (1.0 - s))


_GK = 0.7978845608028654
_GC = 0.044715


def _gelu(x):
    return 0.5 * x * (1.0 + jnp.tanh(_GK * (x + _GC * x * x * x)))


def _dgelu(x):
    t = jnp.tanh(_GK * (x + _GC * x * x * x))
    return 0.5 * (1.0 + t) + 0.5 * x * (1.0 - t * t) * _GK * (1.0 + 3.0 * _GC * x * x)


def _softplus(x):
    return jnp.maximum(x, 0.0) + jnp.log1p(jnp.exp(-jnp.abs(x)))


def _tile(n, cap):
    if n <= cap:
        return n
    best = None
    for t in range(128, cap + 1, 128):
        if n % t == 0:
            best = t
    assert best is not None, (n, cap)
    return best


def _params(sem):
    return pltpu.CompilerParams(dimension_semantics=sem, vmem_limit_bytes=VMEM_LIMIT)


def _exch_ops(src_ref, out_ref, send_sems, recv_sems, local_sem, gather):
    x, y, c = lax.axis_index("x"), lax.axis_index("y"), lax.axis_index("c")
    me = 4 * x + 2 * y + c

    def peer(j):
        px = 1 - x if (j >> 2) & 1 else x
        py = 1 - y if (j >> 1) & 1 else y
        pc = 1 - c if j & 1 else c
        return (px, py, pc), 4 * px + 2 * py + pc

    def copy(j, landing):
        dev, idx = peer(j)
        return pltpu.make_async_remote_copy(
            src_ref=src_ref if gather else src_ref.at[idx], dst_ref=out_ref.at[idx] if landing else out_ref.at[me],
            send_sem=send_sems.at[j - 1], recv_sem=recv_sems.at[j - 1], device_id=dev, device_id_type=pl.DeviceIdType.MESH)

    mine = pltpu.make_async_copy(src_ref if gather else src_ref.at[me], out_ref.at[me], local_sem)

    def start():
        mine.start()
        for j in range(1, NDEV):
            copy(j, False).start()

    def finish():
        for j in range(1, NDEV):
            copy(j, True).wait_recv()
        for j in range(1, NDEV):
            copy(j, False).wait_send()
        mine.wait()

    return start, finish


def _exch_shape(src, gather):
    return jax.ShapeDtypeStruct((NDEV,) + tuple(src.shape if gather else src.shape[1:]), src.dtype)


_EXCH_SEMS = [pltpu.SemaphoreType.DMA((NDEV - 1,)), pltpu.SemaphoreType.DMA((NDEV - 1,)), pltpu.SemaphoreType.DMA]


def _carry_call(body, name, grid, in_specs, out_specs, out_shape, scratch_shapes, args, carry=()):
    n_in, n_out, n_sc, nx = len(in_specs), len(out_specs), len(scratch_shapes), len(carry)
    sem = ("arbitrary",) * len(grid)
    if nx == 0:
        outs = pl.pallas_call(body, name=name, grid=grid, in_specs=in_specs, out_specs=out_specs, out_shape=out_shape,
                              scratch_shapes=scratch_shapes, compiler_params=_params(sem))(*args)
        return list(outs), []

    def wrapped(*refs):
        ins, srcs = refs[:n_in], refs[n_in:n_in + nx]
        outs, dsts = refs[n_in + nx:n_in + nx + n_out], refs[n_in + nx + n_out:n_in + 2 * nx + n_out]
        scratch, sems = refs[n_in + 2 * nx + n_out:n_in + 2 * nx + n_out + n_sc], refs[n_in + 2 * nx + n_out + n_sc:]
        ops = [_exch_ops(srcs[i], dsts[i], sems[3 * i], sems[3 * i + 1], sems[3 * i + 2], carry[i][1]) for i in range(nx)]
        first = functools.reduce(jnp.logical_and, [pl.program_id(d) == 0 for d in range(len(grid))])
        last = functools.reduce(jnp.logical_and, [pl.program_id(d) == grid[d] - 1 for d in range(len(grid))])

        @pl.when(first)
        def _():
            for start, _ in ops:
                start()

        body(*ins, *outs, *scratch)

        @pl.when(last)
        def _():
            for _, finish in ops:
                finish()

    hbm = pl.BlockSpec(memory_space=pl.ANY)
    outs = pl.pallas_call(
        wrapped, name=name, grid=grid, in_specs=list(in_specs) + [hbm] * nx, out_specs=list(out_specs) + [hbm] * nx,
        out_shape=list(out_shape) + [_exch_shape(s, g) for s, g in carry], scratch_shapes=list(scratch_shapes) + _EXCH_SEMS * nx,
        compiler_params=_params(sem))(*args, *[s for s, _ in carry])
    return list(outs[:n_out]), list(outs[n_out:])


def _mm(a, b, mode, name, acc=None, out_dtype=f32, tm_cap=1024, tn_cap=1280, tk_cap=1280, carry=()):
    if mode == "nn":
        (M, K), (K2, N) = a.shape, b.shape
    elif mode == "nt":
        (M, K), (N, K2) = a.shape, b.shape
    else:
        (K, M), (K2, N) = a.shape, b.shape
        tk_cap = min(tk_cap, 512)
    assert K == K2, (a.shape, b.shape, mode)
    tm, tn, tk = _tile(M, tm_cap), _tile(N, tn_cap), _tile(K, tk_cap)
    nk = K // tk
    if mode == "nn":
        a_spec = pl.BlockSpec((tm, tk), lambda i, j, k: (i, k))
        b_spec = pl.BlockSpec((tk, tn), lambda i, j, k: (k, j))
        dot = _dot
    elif mode == "nt":
        a_spec = pl.BlockSpec((tm, tk), lambda i, j, k: (i, k))
        b_spec = pl.BlockSpec((tn, tk), lambda i, j, k: (j, k))
        dot = _dot_nt
    else:
        a_spec = pl.BlockSpec((tk, tm), lambda i, j, k: (k, i))
        b_spec = pl.BlockSpec((tk, tn), lambda i, j, k: (k, j))
        dot = _dot_tn
    o_spec = pl.BlockSpec((tm, tn), lambda i, j, k: (i, j))
    has_acc = acc is not None

    def body(*refs):
        if has_acc:
            a_ref, b_ref, c_ref, o_ref, acc_ref = refs
        else:
            a_ref, b_ref, o_ref, acc_ref = refs
        k = pl.program_id(2)

        @pl.when(k == 0)
        def _():
            if has_acc:
                acc_ref[...] = c_ref[...]
            else:
                acc_ref[...] = jnp.zeros_like(acc_ref)

        acc_ref[...] += dot(_bf(a_ref[...]), _bf(b_ref[...]))

        @pl.when(k == nk - 1)
        def _():
            o_ref[...] = acc_ref[...].astype(out_dtype)

    in_specs = [a_spec, b_spec] + ([o_spec] if has_acc else [])
    args = (a, b) + ((acc,) if has_acc else ())
    (out,), exchanged = _carry_call(body, name, (M // tm, N // tn, nk), in_specs, [o_spec], [jax.ShapeDtypeStruct((M, N), out_dtype)],
                                    [pltpu.VMEM((tm, tn), f32)], args, carry)
    return (out, exchanged) if carry else out


def _row_spec(ts, w, col=0):
    return pl.BlockSpec((ts, w), lambda i: (i, col))


def _full_spec(shape):
    nd = len(shape)
    return pl.BlockSpec(shape, lambda i: (0,) * nd)


def _modulate(x, nw, shift, scale, name):
    T = x.shape[0]
    ts = _tile(T, 512)

    def body(x_ref, nw_ref, sh_ref, sc_ref, h_ref):
        xv = x_ref[...]
        r = lax.rsqrt(jnp.mean(xv * xv, axis=-1, keepdims=True) + EPS)
        h_ref[...] = ((xv * r * nw_ref[...]) * (1.0 + sc_ref[...]) + sh_ref[...]).astype(h_ref.dtype)

    return pl.pallas_call(
        body, name=name, grid=(T // ts,),
        in_specs=[_row_spec(ts, D), _full_spec((1, D)), _full_spec((1, D)), _full_spec((1, D))],
        out_specs=_row_spec(ts, D), out_shape=jax.ShapeDtypeStruct((T, D), _MXU),
        compiler_params=_params(("arbitrary",)))(x, nw, shift, scale)


def _modulate_bwd(dh, dxout, x, nw, scale, name):
    T = x.shape[0]
    ts = _tile(T, 512)

    def body(dh_ref, dxo_ref, x_ref, nw_ref, sc_ref, dx_ref, acc_ref):
        @pl.when(pl.program_id(0) == 0)
        def _():
            acc_ref[...] = jnp.zeros_like(acc_ref)

        xv, dh_v = x_ref[...], dh_ref[...]
        r = lax.rsqrt(jnp.mean(xv * xv, axis=-1, keepdims=True) + EPS)
        xn = xv * r
        hn = xn * nw_ref[...]
        dhn = dh_v * (1.0 + sc_ref[...])
        acc_ref[0:1, :] += jnp.sum(dh_v, axis=0, keepdims=True)
        acc_ref[1:2, :] += jnp.sum(dh_v * hn, axis=0, keepdims=True)
        acc_ref[2:3, :] += jnp.sum(dhn * xn, axis=0, keepdims=True)
        dxn = dhn * nw_ref[...]
        dx_ref[...] = dxo_ref[...] + r * (dxn - xn * jnp.mean(dxn * xn, axis=-1, keepdims=True))

    return pl.pallas_call(
        body, name=name, grid=(T // ts,),
        in_specs=[_row_spec(ts, D), _row_spec(ts, D), _row_spec(ts, D), _full_spec((1, D)), _full_spec((1, D))],
        out_specs=[_row_spec(ts, D), _full_spec((8, D))],
        out_shape=[jax.ShapeDtypeStruct((T, D), f32), jax.ShapeDtypeStruct((8, D), f32)],
        compiler_params=_params(("arbitrary",)))(dh, dxout, x, nw, scale)


def _final_loss(x, tgt, fw, name):
    T = x.shape[0]
    ts = _tile(T, 512)

    def body(x_ref, t_ref, fw_ref, dx_ref, acc_ref):
        @pl.when(pl.program_id(0) == 0)
        def _():
            acc_ref[...] = jnp.zeros_like(acc_ref)

        xv = x_ref[...]
        r = lax.rsqrt(jnp.mean(xv * xv, axis=-1, keepdims=True) + EPS)
        xn = xv * r
        e = xn * fw_ref[...] - t_ref[...]
        dy = e * (1.0 / D)
        acc_ref[0:1, :] += jnp.sum(dy * xn, axis=0, keepdims=True)
        acc_ref[1:2, :] += jnp.sum(jnp.sum(e * e, axis=0, keepdims=True), axis=1, keepdims=True)
        dxn = dy * fw_ref[...]
        dx_ref[...] = r * (dxn - xn * jnp.mean(dxn * xn, axis=-1, keepdims=True))

    return pl.pallas_call(
        body, name=name, grid=(T // ts,),
        in_specs=[_row_spec(ts, D), _row_spec(ts, D), _full_spec((1, D))],
        out_specs=[_row_spec(ts, D), _full_spec((8, D))],
        out_shape=[jax.ShapeDtypeStruct((T, D), f32), jax.ShapeDtypeStruct((8, D), f32)],
        compiler_params=_params(("arbitrary",)))(x, tgt, fw)


def _tril(n):
    return lax.broadcasted_iota(jnp.int32, (n, n), 0) >= lax.broadcasted_iota(jnp.int32, (n, n), 1)


def _gm_chunk_fwd(u, v, z, lw, lb, ws_ref, bsx):
    gu, gv = _gelu(u), _gelu(v)
    mu = jnp.mean(gv, axis=-1, keepdims=True)
    cen = gv - mu
    rstd = lax.rsqrt(jnp.mean(cen * cen, axis=-1, keepdims=True) + EPS)
    vhat = cen * rstd
    vn = _bf(vhat * lw + lb)
    tri = _tril(Q)
    mixed = jnp.concatenate(
        [_dot(_bf(jnp.where(tri, ws_ref[g], 0.0)), vn[:, g * Q:(g + 1) * Q]) for g in range(NG)], axis=1) + bsx
    return gu, vhat, rstd, vn, mixed


def _gmlp_fwd(pA, lw, lb, ws, bsx, name):
    T = pA.shape[0]
    ts = _tile(T, 512)

    def body(u_ref, v_ref, z_ref, lw_ref, lb_ref, ws_ref, bsx_ref, y_ref):
        def chunk(ci, carry):
            rows = pl.ds(pl.multiple_of(ci * Q, Q), Q)
            u, v, z = u_ref[rows, :].astype(f32), v_ref[rows, :].astype(f32), z_ref[rows, :].astype(f32)
            gu, _, _, _, mixed = _gm_chunk_fwd(u, v, z, lw_ref[...], lb_ref[...], ws_ref, bsx_ref[...])
            y_ref[rows, :] = (gu * mixed * _silu(z)).astype(y_ref.dtype)
            return carry

        lax.fori_loop(0, ts // Q, chunk, 0)

    return pl.pallas_call(
        body, name=name, grid=(T // ts,),
        in_specs=[_row_spec(ts, D, 0), _row_spec(ts, D, 1), _row_spec(ts, D, 2), _full_spec((1, D)), _full_spec((1, D)),
                  _full_spec((NG, Q, Q)), _full_spec((Q, D))],
        out_specs=_row_spec(ts, D), out_shape=jax.ShapeDtypeStruct((T, D), _MXU),
        compiler_params=_params(("arbitrary",)))(pA, pA, pA, lw, lb, ws, bsx)


def _gmlp_bwd(dya, pA, lw, lb, ws, bsx, name):
    T = pA.shape[0]
    ts = _tile(T, 512)

    def body(dy_ref, u_ref, v_ref, z_ref, lw_ref, lb_ref, ws_ref, bsx_ref, dp_ref, acc_ref, dws_ref, dbs_ref):
        @pl.when(pl.program_id(0) == 0)
        def _():
            acc_ref[...] = jnp.zeros_like(acc_ref)
            dws_ref[...] = jnp.zeros_like(dws_ref)
            dbs_ref[...] = jnp.zeros_like(dbs_ref)

        tri = _tril(Q)

        def chunk(ci, carry):
            rows = pl.ds(pl.multiple_of(ci * Q, Q), Q)
            u, v, z, dy = u_ref[rows, :].astype(f32), v_ref[rows, :].astype(f32), z_ref[rows, :].astype(f32), dy_ref[rows, :].astype(f32)
            gu, vhat, rstd, vn, mixed = _gm_chunk_fwd(u, v, z, lw_ref[...], lb_ref[...], ws_ref, bsx_ref[...])
            sz = _silu(z)
            dp_ref[rows, 0:D] = (dy * mixed * sz * _dgelu(u)).astype(dp_ref.dtype)
            dp_ref[rows, 2 * D:3 * D] = (dy * gu * mixed * _dsilu(z)).astype(dp_ref.dtype)
            dmixed = dy * gu * sz
            dbs_ref[...] += dmixed
            dmb = _bf(dmixed)
            dvn_parts = []
            for g in range(NG):
                cols = slice(g * Q, (g + 1) * Q)
                wg = _bf(jnp.where(tri, ws_ref[g], 0.0))
                dvn_parts.append(_dot_tn(wg, dmb[:, cols]))
                dws_ref[g] += jnp.where(tri, _dot_nt(dmb[:, cols], vn[:, cols]), 0.0)
            dvn = jnp.concatenate(dvn_parts, axis=1)
            acc_ref[0:1, :] += jnp.sum(dvn * vhat, axis=0, keepdims=True)
            acc_ref[1:2, :] += jnp.sum(dvn, axis=0, keepdims=True)
            dvh = dvn * lw_ref[...]
            dgv = rstd * (dvh - jnp.mean(dvh, axis=-1, keepdims=True) - vhat * jnp.mean(dvh * vhat, axis=-1, keepdims=True))
            dp_ref[rows, D:2 * D] = (dgv * _dgelu(v)).astype(dp_ref.dtype)
            return carry

        lax.fori_loop(0, ts // Q, chunk, 0)

    return pl.pallas_call(
        body, name=name, grid=(T // ts,),
        in_specs=[_row_spec(ts, D), _row_spec(ts, D, 0), _row_spec(ts, D, 1), _row_spec(ts, D, 2), _full_spec((1, D)),
                  _full_spec((1, D)), _full_spec((NG, Q, Q)), _full_spec((Q, D))],
        out_specs=[_row_spec(ts, WA), _full_spec((8, D)), _full_spec((NG, Q, Q)), _full_spec((Q, D))],
        out_shape=[jax.ShapeDtypeStruct((T, WA), _MXU), jax.ShapeDtypeStruct((8, D), f32),
                   jax.ShapeDtypeStruct((NG, Q, Q), f32), jax.ShapeDtypeStruct((Q, D), f32)],
        compiler_params=_params(("arbitrary",)))(dya, pA, pA, pA, lw, lb, ws, bsx)


def _head_maps():
    h = lax.broadcasted_iota(jnp.int32, (128, DI), 0)
    ch = lax.broadcasted_iota(jnp.int32, (128, DI), 1)
    ex = (ch // P == h).astype(_MXU)
    return ex, ex.T


def _split(v, parts):
    out = []
    for _ in range(parts - 1):
        p = _bf(v)
        out.append(p)
        v = v - p.astype(f32)
    out.append(_bf(v))
    return out


def _expand(v, ex_ref, parts):
    acc = None
    for p in _split(v, parts):
        t = _dot(p, ex_ref[...])
        acc = t if acc is None else acc + t
    return acc


def _reduce(v, rd_ref, parts=2):
    acc = None
    for p in _split(v, parts):
        t = _dot(p, rd_ref[...])
        acc = t if acc is None else acc + t
    return acc


def _ssd_time(pb_ref, dtb_ref, alog_ref):
    xdt = pb_ref[:, OFF_DT:OFF_DT + 128] + dtb_ref[...]
    dt = _softplus(xdt)
    a = -jnp.exp(alog_ref[...])
    cs = _dot_exact(_tril(Q).astype(f32), dt * a)
    return xdt, dt, a, cs


def _head_mask(r):
    return lax.broadcasted_iota(jnp.int32, (Q, GRP), 1) // P == r


def _ssd_group_fwd(g, xa_ref, s_prev, cs, cs_t, dtx_ref, csx_ref, dsk_ref):
    cols = slice(g * GRP, (g + 1) * GRP)
    xs = xa_ref[:, cols]
    bb = _bf(xa_ref[:, DI + g * NS:DI + (g + 1) * NS])
    cb = _bf(xa_ref[:, DI + NG * NS + g * NS:DI + NG * NS + (g + 1) * NS])
    gm = _dot_nt(cb, bb)
    xd = xs * dtx_ref[:, cols]
    csx = csx_ref[:, cols]
    csl = csx_ref[Q - 1:Q, cols]
    tri = _tril(Q)
    lms = [jnp.exp(jnp.where(tri, cs[:, HPG * g + r:HPG * g + r + 1] - cs_t[HPG * g + r:HPG * g + r + 1, :], NEG))
           for r in range(HPG)]
    mfs = [gm * lm for lm in lms]
    mcat = jnp.concatenate([_bf(m) for m in mfs], axis=1)
    xbd = jnp.concatenate([_bf(jnp.where(_head_mask(r), xd, 0.0)) for r in range(HPG)], axis=0)
    ydiag = _dot(mcat, xbd)
    yoff = jnp.exp(csx) * _dot(cb, _bf(s_prev))
    y = ydiag + yoff + xs * dsk_ref[:, cols]
    xdd = xd * jnp.exp(csl - csx)
    s_new = s_prev * jnp.exp(csl) + _dot_tn(bb, _bf(xdd))
    return y, s_new, (xs, bb, cb, xd, lms, mfs, mcat, xbd, yoff, xdd, csx, csl)


def _ssd_fwd(pB, cw, cb, dtb, alog, dsk, snw, name, carry=()):
    T = pB.shape[0]
    nc = T // Q
    ex, _ = _head_maps()

    def body(pb_ref, cw_ref, cb_ref, dtb_ref, alog_ref, dsk_ref, snw_ref, ex_ref, y_ref, xc_ref, st_ref,
             s_ref, ext_ref, xa_ref, dtx_ref, csx_ref):
        @pl.when(pl.program_id(0) == 0)
        def _():
            s_ref[...] = jnp.zeros_like(s_ref)
            ext_ref[...] = jnp.zeros_like(ext_ref)

        ext_ref[8:8 + Q, :] = pb_ref[:, OFF_XBC:OFF_XBC + CD]
        for j in range(CD // 512):
            cj = slice(j * 512, (j + 1) * 512)
            e = ext_ref[:, cj]
            xc = cb_ref[:, cj] + cw_ref[KC - 1:KC, cj] * e[8:8 + Q]
            for s in range(1, KC):
                xc = xc + cw_ref[KC - 1 - s:KC - s, cj] * pltpu.roll(e, s, 0)[8:8 + Q]
            xc_ref[:, cj] = xc
            xa_ref[:, cj] = _silu(xc)
        ext_ref[0:8, :] = ext_ref[Q:Q + 8, :]

        _, dt, _, cs = _ssd_time(pb_ref, dtb_ref, alog_ref)
        cs_t = cs.T
        dtx_ref[...] = _expand(dt, ex_ref, 2)
        csx_ref[...] = _expand(cs, ex_ref, 3)
        for g in range(NG):
            s_prev = s_ref[g]
            st_ref[0, g] = s_prev
            y, s_new, _ = _ssd_group_fwd(g, xa_ref, s_prev, cs, cs_t, dtx_ref, csx_ref, dsk_ref)
            s_ref[g] = s_new
            cols = slice(g * GRP, (g + 1) * GRP)
            yz = y * _silu(pb_ref[:, cols])
            rr = lax.rsqrt(jnp.mean(yz * yz, axis=-1, keepdims=True) + EPS)
            y_ref[:, cols] = (yz * rr * snw_ref[:, cols]).astype(y_ref.dtype)

    outs, exchanged = _carry_call(
        body, name, (nc,),
        [_row_spec(Q, WB), _full_spec((8, CD)), _full_spec((1, CD)), _full_spec((1, 128)), _full_spec((1, 128)),
         _full_spec((1, DI)), _full_spec((1, DI)), _full_spec((128, DI))],
        [_row_spec(Q, DI), _row_spec(Q, CD), pl.BlockSpec((1, NG, NS, GRP), lambda i: (i, 0, 0, 0))],
        [jax.ShapeDtypeStruct((T, DI), _MXU), jax.ShapeDtypeStruct((T, CD), f32), jax.ShapeDtypeStruct((nc, NG, NS, GRP), f32)],
        [pltpu.VMEM((NG, NS, GRP), f32), pltpu.VMEM((Q + 8, CD), f32), pltpu.VMEM((Q, CD), f32),
         pltpu.VMEM((Q, DI), f32), pltpu.VMEM((Q, DI), f32)],
        (pB, cw, cb, dtb, alog, dsk, snw, ex), carry)
    return outs + [exchanged]


def _ssd_bwd(dyb, pB, xc, states, cw, dtb, alog, dsk, snw, name, carry=()):
    T = pB.shape[0]
    nc = T // Q
    ex, rd = _head_maps()
    selr = (lax.broadcasted_iota(jnp.int32, (HPG * Q, 128), 0) // Q == lax.broadcasted_iota(jnp.int32, (HPG * Q, 128), 1)).astype(_MXU)

    def body(dy_ref, pb_ref, xc_ref, st_ref, cw_ref, dtb_ref, alog_ref, dsk_ref, snw_ref, ex_ref, rd_ref, selr_ref,
             dp_ref, dcw_ref, dcb_ref, dhd_ref, dcol_ref,
             ds_ref, xa_ref, sg_ref, dxa_ref, dxe_ref, dtx_ref, csx_ref, rcs_ref, rdt_ref, rl_ref, dcs_ref, dcst_ref):
        @pl.when(pl.program_id(0) == 0)
        def _():
            ds_ref[...] = jnp.zeros_like(ds_ref)
            dxe_ref[...] = jnp.zeros_like(dxe_ref)
            dcw_ref[...] = jnp.zeros_like(dcw_ref)
            dcb_ref[...] = jnp.zeros_like(dcb_ref)
            dhd_ref[...] = jnp.zeros_like(dhd_ref)
            dcol_ref[...] = jnp.zeros_like(dcol_ref)
            rl_ref[...] = jnp.zeros_like(rl_ref)
            dcst_ref[...] = jnp.zeros_like(dcst_ref)

        dcs_ref[...] = jnp.zeros_like(dcs_ref)
        for j in range(CD // 512):
            cj = slice(j * 512, (j + 1) * 512)
            sg_ref[:, cj] = _sigmoid(xc_ref[:, cj])
            xa_ref[:, cj] = xc_ref[:, cj] * sg_ref[:, cj]
        xdt, dt, a, cs = _ssd_time(pb_ref, dtb_ref, alog_ref)
        cs_t = cs.T
        dtx_ref[...] = _expand(dt, ex_ref, 2)
        csx_ref[...] = _expand(cs, ex_ref, 3)

        for g in range(NG):
            cols = slice(g * GRP, (g + 1) * GRP)
            s_prev = st_ref[0, g]
            sb = _bf(s_prev)
            y, _, (xs, bb, cbm, xd, lms, mfs, mcat, xbd, yoff, xdd, csx, csl) = _ssd_group_fwd(
                g, xa_ref, s_prev, cs, cs_t, dtx_ref, csx_ref, dsk_ref)
            z = pb_ref[:, cols]
            sz = _silu(z)
            yz = y * sz
            rr = lax.rsqrt(jnp.mean(yz * yz, axis=-1, keepdims=True) + EPS)
            nrm = yz * rr
            dyb_g = dy_ref[:, cols].astype(f32)
            dcol_ref[1:2, cols] += jnp.sum(dyb_g * nrm, axis=0, keepdims=True)
            dn = dyb_g * snw_ref[:, cols]
            dyz = rr * (dn - nrm * jnp.mean(dn * nrm, axis=-1, keepdims=True))
            dyv = dyz * sz
            dp_ref[:, cols] = (dyz * y * _dsilu(z)).astype(dp_ref.dtype)
            dcol_ref[0:1, cols] += jnp.sum(dyv * xs, axis=0, keepdims=True)
            dy16 = _bf(dyv)
            dmcat = _dot_nt(dy16, xbd)
            dg = dmcat[:, 0:Q] * lms[0]
            for r in range(1, HPG):
                dg = dg + dmcat[:, r * Q:(r + 1) * Q] * lms[r]
            e16 = _bf(jnp.concatenate([dmcat[:, r * Q:(r + 1) * Q] * mfs[r] for r in range(HPG)], axis=1))
            rows = _dot(e16, selr_ref[...])
            dcs_ref[...] += rows if g == 0 else pltpu.roll(rows, HPG * g, 1)
            csum = _dot(jnp.ones((8, Q), _MXU), e16)
            for r in range(HPG):
                dcst_ref[HPG * g + r:HPG * g + r + 1, :] = csum[0:1, r * Q:(r + 1) * Q]
            big = _dot_tn(mcat, dy16)
            dxd_diag = jnp.where(_head_mask(0), big[0:Q], 0.0)
            for r in range(1, HPG):
                dxd_diag = dxd_diag + jnp.where(_head_mask(r), big[r * Q:(r + 1) * Q], 0.0)
            dsn = ds_ref[g]
            dsn16 = _bf(dsn)
            dxdd = _dot(bb, dsn16)
            t_state = dxdd * xdd
            dxd = dxd_diag + dxdd * jnp.exp(csl - csx)
            rcs_ref[:, cols] = dyv * yoff - t_state
            rl_ref[0:1, cols] = (jnp.sum(t_state, axis=0, keepdims=True)
                                 + jnp.exp(csl) * jnp.sum(s_prev * dsn, axis=0, keepdims=True))
            rdt_ref[:, cols] = dxd * xs
            dxa_ref[:, cols] = dyv * dsk_ref[:, cols] + dxd * dtx_ref[:, cols]
            dg16 = _bf(dg)
            dw16 = _bf(dyv * jnp.exp(csx))
            dxa_ref[:, DI + NG * NS + g * NS:DI + NG * NS + (g + 1) * NS] = _dot(dg16, bb) + _dot_nt(dw16, sb)
            dxa_ref[:, DI + g * NS:DI + (g + 1) * NS] = _dot_tn(dg16, cbm) + _dot_nt(_bf(xdd), dsn16)
            ds_ref[g] = dsn * jnp.exp(csl) + _dot_tn(cbm, dw16)

        row = lax.broadcasted_iota(jnp.int32, (Q, 128), 0)
        dcs = (dcs_ref[...] - dcst_ref[...].T + _reduce(rcs_ref[...], rd_ref)
               + jnp.where(row == Q - 1, _reduce(rl_ref[...], rd_ref)[0:1, :], 0.0))
        upper = lax.broadcasted_iota(jnp.int32, (Q, Q), 0) <= lax.broadcasted_iota(jnp.int32, (Q, Q), 1)
        dadt = _dot_exact(upper.astype(f32), dcs)
        valid = lax.broadcasted_iota(jnp.int32, (Q, 128), 1) < NH
        ddt = jnp.where(valid, _reduce(rdt_ref[...], rd_ref, 1) + dadt * a, 0.0)
        ddtr = ddt * _sigmoid(xdt)
        dhd_ref[0:1, :] += jnp.sum(ddtr, axis=0, keepdims=True)
        dhd_ref[1:2, :] += jnp.sum(jnp.where(valid, dadt * dt * a, 0.0), axis=0, keepdims=True)
        dp_ref[:, OFF_DT:OFF_DT + 128] = ddtr.astype(dp_ref.dtype)
        dp_ref[:, OFF_DT + 128:WB] = jnp.zeros((Q, WB - OFF_DT - 128), dp_ref.dtype)
        for j in range(CD // 512):
            cj = slice(j * 512, (j + 1) * 512)
            sg = sg_ref[:, cj]
            dxc = dxa_ref[:, cj] * (sg + xa_ref[:, cj] * (1.0 - sg))
            dxe_ref[0:Q, cj] = dxc
            dcb_ref[0:1, cj] += jnp.sum(dxc, axis=0, keepdims=True)
            raw = pb_ref[:, OFF_XBC + j * 512:OFF_XBC + (j + 1) * 512]
            e = dxe_ref[:, cj]
            dcw_ref[KC - 1:KC, cj] += jnp.sum(dxc * raw, axis=0, keepdims=True)
            dxb = cw_ref[KC - 1:KC, cj] * dxc
            for s in range(1, KC):
                sh = pltpu.roll(e, Q + 8 - s, 0)[0:Q]
                dcw_ref[KC - 1 - s:KC - s, cj] += jnp.sum(sh * raw, axis=0, keepdims=True)
                dxb = dxb + cw_ref[KC - 1 - s:KC - s, cj] * sh
            dp_ref[:, OFF_XBC + j * 512:OFF_XBC + (j + 1) * 512] = dxb.astype(dp_ref.dtype)
        dxe_ref[Q:Q + 8, :] = dxe_ref[0:8, :]

    rev = lambda i: (nc - 1 - i, 0)
    outs, exchanged = _carry_call(
        body, name, (nc,),
        [pl.BlockSpec((Q, DI), rev), pl.BlockSpec((Q, WB), rev), pl.BlockSpec((Q, CD), rev),
         pl.BlockSpec((1, NG, NS, GRP), lambda i: (nc - 1 - i, 0, 0, 0)),
         _full_spec((8, CD)), _full_spec((1, 128)), _full_spec((1, 128)),
         _full_spec((1, DI)), _full_spec((1, DI)), _full_spec((128, DI)), _full_spec((DI, 128)), _full_spec((HPG * Q, 128))],
        [pl.BlockSpec((Q, WB), rev), _full_spec((8, CD)), _full_spec((8, CD)), _full_spec((8, 128)), _full_spec((8, DI))],
        [jax.ShapeDtypeStruct((T, WB), _MXU), jax.ShapeDtypeStruct((8, CD), f32), jax.ShapeDtypeStruct((8, CD), f32),
         jax.ShapeDtypeStruct((8, 128), f32), jax.ShapeDtypeStruct((8, DI), f32)],
        [pltpu.VMEM((NG, NS, GRP), f32), pltpu.VMEM((Q, CD), f32), pltpu.VMEM((Q, CD), f32), pltpu.VMEM((Q, CD), f32),
         pltpu.VMEM((Q + 8, CD), f32), pltpu.VMEM((Q, DI), f32), pltpu.VMEM((Q, DI), f32),
         pltpu.VMEM((Q, DI), f32), pltpu.VMEM((Q, DI), f32), pltpu.VMEM((8, DI), f32),
         pltpu.VMEM((Q, 128), f32), pltpu.VMEM((128, Q), f32)],
        (dyb, pB, xc, states, cw, dtb, alog, dsk, snw, ex, rd, selr), carry)
    return outs + [exchanged]


def _merge_fwd(ya, yb, pG, x, gate, wa, wb, wo, name, carry=()):
    T = x.shape[0]
    ts = _tile(T, 256)

    def body(ya_ref, yb_ref, g_ref, x_ref, gate_ref, wa_ref, wb_ref, wo_ref, xo_ref, mg_ref, pa_ref, pb_ref):
        pa = _dot(ya_ref[...], wa_ref[...])
        pb = _dot(yb_ref[...], wb_ref[...])
        merged = _sigmoid(g_ref[:, 0:D].astype(f32)) * pa + _sigmoid(g_ref[:, D:2 * D].astype(f32)) * pb
        mg = _bf(merged)
        xo_ref[...] = x_ref[...] + gate_ref[...] * _dot(mg, wo_ref[...])
        mg_ref[...] = mg
        pa_ref[...] = pa.astype(pa_ref.dtype)
        pb_ref[...] = pb.astype(pb_ref.dtype)

    outs, exchanged = _carry_call(
        body, name, (T // ts,),
        [_row_spec(ts, D), _row_spec(ts, DI), _row_spec(ts, WG), _row_spec(ts, D), _full_spec((1, D)),
         _full_spec((D, D)), _full_spec((DI, D)), _full_spec((D, D))],
        [_row_spec(ts, D)] * 4,
        [jax.ShapeDtypeStruct((T, D), f32), jax.ShapeDtypeStruct((T, D), _MXU), jax.ShapeDtypeStruct((T, D), _MXU),
         jax.ShapeDtypeStruct((T, D), _MXU)],
        [], (ya, yb, pG, x, gate, wa, wb, wo), carry)
    return outs + [exchanged]


def _merge_bwd(dxout, merged, pa, pb, pG, gate, wo, name):
    T = dxout.shape[0]
    ts = _tile(T, 256)

    def body(dx_ref, mg_ref, pa_ref, pb_ref, g_ref, gate_ref, wo_ref, do_ref, dpa_ref, dpb_ref, dg_ref, acc_ref):
        @pl.when(pl.program_id(0) == 0)
        def _():
            acc_ref[...] = jnp.zeros_like(acc_ref)

        dxo = dx_ref[...]
        acc_ref[0:1, :] += jnp.sum(dxo * _dot(mg_ref[...], wo_ref[...]), axis=0, keepdims=True)
        do = _bf(dxo * gate_ref[...])
        do_ref[...] = do
        dmerged = _dot_nt(do, wo_ref[...])
        sa, sb = _sigmoid(g_ref[:, 0:D].astype(f32)), _sigmoid(g_ref[:, D:2 * D].astype(f32))
        dpa_ref[...] = (dmerged * sa).astype(dpa_ref.dtype)
        dpb_ref[...] = (dmerged * sb).astype(dpb_ref.dtype)
        dg_ref[:, 0:D] = (dmerged * pa_ref[...].astype(f32) * sa * (1.0 - sa)).astype(dg_ref.dtype)
        dg_ref[:, D:2 * D] = (dmerged * pb_ref[...].astype(f32) * sb * (1.0 - sb)).astype(dg_ref.dtype)

    return pl.pallas_call(
        body, name=name, grid=(T // ts,),
        in_specs=[_row_spec(ts, D), _row_spec(ts, D), _row_spec(ts, D), _row_spec(ts, D), _row_spec(ts, WG),
                  _full_spec((1, D)), _full_spec((D, D))],
        out_specs=[_row_spec(ts, D), _row_spec(ts, D), _row_spec(ts, D), _row_spec(ts, WG), _full_spec((8, D))],
        out_shape=[jax.ShapeDtypeStruct((T, D), _MXU), jax.ShapeDtypeStruct((T, D), _MXU), jax.ShapeDtypeStruct((T, D), _MXU),
                   jax.ShapeDtypeStruct((T, WG), _MXU), jax.ShapeDtypeStruct((8, D), f32)],
        compiler_params=_params(("arbitrary",)))(dxout, merged, pa, pb, pG, gate, wo)


def _rows_split(r):
    ra, rb = D // NDEV, DI // NDEV
    return r[:, 0:ra].reshape(D, D), r[:, ra:ra + rb].reshape(DI, D), r[:, ra + rb:].reshape(D, D)


def _layer_fwd(x, mod, lp, tag, carry=None):
    carry = carry or {}
    got = {}

    def mm(key, w, **kw):
        if carry.get(key):
            out, got[key] = _mm(h, w, "nn", f"{key}_{tag}", carry=carry[key], **kw)
            return out
        return _mm(h, w, "nn", f"{key}_{tag}", **kw)

    h = _modulate(x, lp["nw"], mod[0:1], mod[1:2], f"modulate_{tag}")
    pA = mm("proj_gm", lp["w_gm"], out_dtype=_MXU)
    pB = mm("proj_ssd", lp["w_ssd"])
    pG = mm("proj_gate", lp["w_g"], out_dtype=_MXU)
    if "wa" not in lp:
        lp = dict(lp)
        lp["wa"], lp["wb"], lp["wo"] = _rows_split(jnp.concatenate([got["proj_gm"][0], got["proj_gate"][0]], axis=1))
    ya = _gmlp_fwd(pA, lp["lw"], lp["lb"], lp["ws"], lp["bsx"], f"gmlp_fwd_{tag}")
    yb, xc, states, got["ssd_fwd"] = _ssd_fwd(pB, lp["cw"], lp["cb"], lp["dtb"], lp["alog"], lp["dsk"], lp["snw"], f"ssd_fwd_{tag}",
                                              carry.get("ssd_fwd", ()))
    xo, merged, pa, pb, _ = _merge_fwd(ya, yb, pG, x, mod[2:3], lp["wa"], lp["wb"], lp["wo"], f"merge_fwd_{tag}")
    return xo, dict(x=x, h=h, pA=pA, pB=pB, pG=pG, ya=ya, yb=yb, xc=xc, states=states, merged=merged, pa=pa, pb=pb), got, lp


def _dh_modulate_bwd(dpG, dpA, dpB, w_g, w_gm, w_ssd, dxout, x, nw, scale, name, carry=()):
    T = x.shape[0]
    tm = _tile(T, 1024)
    fam = [(WG, _tile(WG, 1024)), (WA, _tile(WA, 1024)), (WB, _tile(WB, 640))]
    steps = [w // k for w, k in fam]
    first = [0, steps[0], steps[0] + steps[1]]
    ns = sum(steps)

    def chunk(f):
        return lambda i, s: jnp.clip(s - first[f], 0, steps[f] - 1)

    a_specs = [pl.BlockSpec((tm, fam[f][1]), lambda i, s, c=chunk(f): (i, c(i, s))) for f in range(3)]
    b_specs = [pl.BlockSpec((D, fam[f][1]), lambda i, s, c=chunk(f): (0, c(i, s))) for f in range(3)]
    tok = pl.BlockSpec((tm, D), lambda i, s: (i, 0))
    vec = pl.BlockSpec((1, D), lambda i, s: (0, 0))

    def body(ag, aa, ab, bg, ba, bb, dxo_ref, x_ref, nw_ref, sc_ref, dx_ref, sum_ref, acc_ref):
        i, s = pl.program_id(0), pl.program_id(1)

        @pl.when(jnp.logical_and(i == 0, s == 0))
        def _():
            sum_ref[...] = jnp.zeros_like(sum_ref)

        @pl.when(s == 0)
        def _():
            acc_ref[...] = jnp.zeros_like(acc_ref)

        @pl.when(s < first[1])
        def _():
            acc_ref[...] += _dot_nt(ag[...], bg[...])

        @pl.when(jnp.logical_and(s >= first[1], s < first[2]))
        def _():
            acc_ref[...] += _dot_nt(aa[...], ba[...])

        @pl.when(s >= first[2])
        def _():
            acc_ref[...] += _dot_nt(ab[...], bb[...])

        @pl.when(s == ns - 1)
        def _():
            xv, dh_v = x_ref[...], acc_ref[...]
            r = lax.rsqrt(jnp.mean(xv * xv, axis=-1, keepdims=True) + EPS)
            xn = xv * r
            hn = xn * nw_ref[...]
            dhn = dh_v * (1.0 + sc_ref[...])
            sum_ref[0:1, :] += jnp.sum(dh_v, axis=0, keepdims=True)
            sum_ref[1:2, :] += jnp.sum(dh_v * hn, axis=0, keepdims=True)
            sum_ref[2:3, :] += jnp.sum(dhn * xn, axis=0, keepdims=True)
            dxn = dhn * nw_ref[...]
            dx_ref[...] = dxo_ref[...] + r * (dxn - xn * jnp.mean(dxn * xn, axis=-1, keepdims=True))

    outs, exchanged = _carry_call(
        body, name, (T // tm, ns), a_specs + b_specs + [tok, tok, vec, vec], [tok, pl.BlockSpec((8, D), lambda i, s: (0, 0))],
        [jax.ShapeDtypeStruct((T, D), f32), jax.ShapeDtypeStruct((8, D), f32)], [pltpu.VMEM((tm, D), f32)],
        (dpG, dpA, dpB, w_g, w_gm, w_ssd, dxout, x, nw, scale), carry)
    return outs + [exchanged]


def _win_blocks(g):
    full = jnp.concatenate([g["w_gm"], g["w_ssd"][:, 0:DI + CD], g["w_ssd"][:, DI + CD:DI + CD + NH], g["w_g"]], axis=1)
    return jnp.transpose(full.reshape(D, NDEV, NIN // NDEV), (1, 0, 2))


def _row_blocks(g):
    return jnp.concatenate([g["wa"].reshape(NDEV, D // NDEV, D), g["wb"].reshape(NDEV, DI // NDEV, D),
                            g["wo"].reshape(NDEV, D // NDEV, D)], axis=1)


def _layer_bwd(dxo, sv, mod, lp, tag, carry=(), scatter_own=None):
    do, dpa, dpb, dpG, s_gate = _merge_bwd(dxo, sv["merged"], sv["pa"], sv["pb"], sv["pG"], mod[2:3], lp["wo"], f"merge_bwd_{tag}")
    g = {}
    g["wo"] = _mm(sv["merged"], do, "tn", f"dw_out_{tag}", out_dtype=_WIRE)
    g["wa"] = _mm(sv["ya"], dpa, "tn", f"dw_proj_a_{tag}", out_dtype=_WIRE)
    g["wb"] = _mm(sv["yb"], dpb, "tn", f"dw_proj_b_{tag}", out_dtype=_WIRE)
    dya = _mm(dpa, lp["wa"], "nt", f"dy_a_{tag}", out_dtype=_MXU)
    dyb = _mm(dpb, lp["wb"], "nt", f"dy_b_{tag}", out_dtype=_MXU)
    dpA, s_ln, g["ws"], dbsx = _gmlp_bwd(dya, sv["pA"], lp["lw"], lp["lb"], lp["ws"], lp["bsx"], f"gmlp_bwd_{tag}")
    carry = list(carry) + ([(_row_blocks(g), False)] if scatter_own else [])
    dpB, g["cw"], s_cb, s_hd, s_col, got = _ssd_bwd(dyb, sv["pB"], sv["xc"], sv["states"], lp["cw"], lp["dtb"], lp["alog"],
                                                    lp["dsk"], lp["snw"], f"ssd_bwd_{tag}", carry)
    g["lw"], g["lb"] = s_ln[0], s_ln[1]
    g["bs"] = dbsx.reshape(Q, NG, Q).sum(-1).T
    g["cb"] = s_cb[0]
    g["dtb"], g["alog"] = s_hd[0, :NH], s_hd[1, :NH]
    g["dsk"] = s_col[0].reshape(NH, P).sum(-1)
    g["snw"] = s_col[1]
    g["cw"] = g["cw"][0:KC]
    g["w_g"] = _mm(sv["h"], dpG, "tn", f"dw_gate_{tag}", out_dtype=_WIRE)
    g["w_gm"] = _mm(sv["h"], dpA, "tn", f"dw_gm_{tag}", out_dtype=_WIRE)
    g["w_ssd"] = _mm(sv["h"], dpB, "tn", f"dw_ssd_{tag}", out_dtype=_WIRE)
    dx, s_mod, got_dh = _dh_modulate_bwd(dpG, dpA, dpB, lp["w_g"], lp["w_gm"], lp["w_ssd"], dxo, sv["x"], lp["nw"], mod[1:2],
                                         f"dh_{tag}", [(_win_blocks(g), False), (scatter_own(g), True)] if scatter_own else ())
    if scatter_own:
        got, got_own = got[:-1], [got[-1], got_dh[0], got_dh[1]]
    else:
        got_own = []
    g["mod"] = jnp.concatenate([s_mod[0], s_mod[1], s_gate[0]])
    g["nw"] = s_mod[2]
    return dx, g, got, got_own


def _prep_layer(nw, w_in_full, lw, lb, ws, bs, cw_full, cb, dtb, alog, dsk, snw, rows=None):
    z = jnp.zeros((D, WB - (DI + CD + NH)), w_in_full.dtype)
    pad_h = lambda v: jnp.pad(v, (0, 128 - NH)).reshape(1, 128)
    extra = dict(zip(("wa", "wb", "wo"), _rows_split(rows))) if rows is not None else {}
    return dict(
        **extra,
        nw=nw.reshape(1, D),
        w_gm=w_in_full[:, 0:WA],
        w_ssd=jnp.concatenate([w_in_full[:, WA:WA + DI + CD], w_in_full[:, WA + DI + CD:WA + DI + CD + NH], z], axis=1),
        w_g=w_in_full[:, WA + DI + CD + NH:NIN],
        lw=lw.reshape(1, D), lb=lb.reshape(1, D), ws=ws, bsx=jnp.repeat(bs.T, Q, axis=1),
        cw=jnp.pad(cw_full, ((0, 8 - KC), (0, 0))), cb=cb.reshape(1, CD), dtb=pad_h(dtb), alog=pad_h(alog),
        dsk=jnp.repeat(dsk, P).reshape(1, DI), snw=snw.reshape(1, DI))


def _exchange(src, gather, name):
    if not gather:
        assert src.shape[0] == NDEV

    def body(src_ref, out_ref, send_sems, recv_sems, local_sem):
        start, finish = _exch_ops(src_ref, out_ref, send_sems, recv_sems, local_sem, gather)
        start()
        finish()

    return pl.pallas_call(
        body, name=name, out_shape=_exch_shape(src, gather),
        in_specs=[pl.BlockSpec(memory_space=pl.ANY)], out_specs=pl.BlockSpec(memory_space=pl.ANY), scratch_shapes=_EXCH_SEMS)(src)


def _gather2(src, name):
    def body(src_ref, out_ref, send_sems, recv_sems, local_sem):
        x, y, c = lax.axis_index("x"), lax.axis_index("y"), lax.axis_index("c")
        me, sibling = (x, y, c), (x, y, 1 - c)
        chips = [(1 - x, y), (x, 1 - y), (1 - x, 1 - y)]

        def slot(px, py, pc):
            return out_ref.at[4 * px + 2 * py + pc]

        def copy(k, block, to, src=None):
            return pltpu.make_async_remote_copy(
                src_ref=slot(*block) if src is None else src, dst_ref=slot(*block), send_sem=send_sems.at[k],
                recv_sem=recv_sems.at[k], device_id=to, device_id_type=pl.DeviceIdType.MESH)

        mine = pltpu.make_async_copy(src_ref, slot(*me), local_sem)
        mine.start()
        first = [copy(0, me, sibling, src=src_ref)] + [copy(1 + j, me, (*chip, c), src=src_ref) for j, chip in enumerate(chips)]
        for cp in first:
            cp.start()
        passed = [copy(4 + j, (*chip, c), sibling) for j, chip in enumerate(chips)]
        for j, chip in enumerate(chips):
            copy(1 + j, (*chip, c), me).wait_recv()
            passed[j].start()
        copy(0, sibling, me).wait_recv()
        for j, chip in enumerate(chips):
            copy(4 + j, (*chip, 1 - c), me).wait_recv()
        for cp in first + passed:
            cp.wait_send()
        mine.wait()

    return pl.pallas_call(
        body, name=name, out_shape=_exch_shape(src, True),
        in_specs=[pl.BlockSpec(memory_space=pl.ANY)], out_specs=pl.BlockSpec(memory_space=pl.ANY), scratch_shapes=_EXCH_SEMS)(src)


def _mod_dist(c8, ada_w, ada_b_cols, name):
    L, _, AW = ada_w.shape

    def body(c_ref, w_ref, b_ref, parts_ref, sc_ref, call_ref, mine_ref, send1, recv1, send2, recv2):
        x, y, c = lax.axis_index("x"), lax.axis_index("y"), lax.axis_index("c")
        me = 4 * x + 2 * y + c

        def peer(j):
            px = 1 - x if (j >> 2) & 1 else x
            py = 1 - y if (j >> 1) & 1 else y
            pc = 1 - c if j & 1 else c
            return (px, py, pc), 4 * px + 2 * py + pc

        def copy(j, src, dst, sems, landing):
            dev, idx = peer(j)
            return pltpu.make_async_remote_copy(
                src_ref=src, dst_ref=dst.at[idx] if landing else dst.at[me], send_sem=sems[0].at[j - 1], recv_sem=sems[1].at[j - 1],
                device_id=dev, device_id_type=pl.DeviceIdType.MESH)

        def all_to_all(src, dst, sems):
            for j in range(1, NDEV):
                copy(j, src, dst, sems, False).start()
            for j in range(1, NDEV):
                copy(j, src, dst, sems, True).wait_recv()
            for j in range(1, NDEV):
                copy(j, src, dst, sems, False).wait_send()

        call_ref[me] = c_ref[...]
        all_to_all(c_ref, call_ref, (send1, recv1))
        row = lax.broadcasted_iota(jnp.int32, (8, D), 0)
        cm = jnp.zeros((8, D), f32)
        for k in range(NDEV):
            cm = jnp.where(row == k, call_ref[k], cm)
        sc = _silu(cm)
        sc_ref[...] = sc
        for l in range(L):
            mine_ref[l] = _dot(_bf(sc), w_ref[l]) + b_ref[l]
        parts_ref[me] = mine_ref[...]
        all_to_all(mine_ref, parts_ref, (send2, recv2))

    vmem = pl.BlockSpec(memory_space=pltpu.VMEM)
    sems = pltpu.SemaphoreType.DMA((NDEV - 1,))
    return pl.pallas_call(
        body, name=name, in_specs=[vmem, vmem, vmem], out_specs=[vmem, vmem],
        out_shape=[jax.ShapeDtypeStruct((NDEV, L, 8, AW), f32), jax.ShapeDtypeStruct((8, D), f32)],
        scratch_shapes=[pltpu.VMEM((NDEV, 8, D), f32), pltpu.VMEM((L, 8, AW), f32), sems, sems, sems, sems])(c8, ada_w, ada_b_cols)


def _ada_w_grad(sc_all, dmod_cols, name):
    W = dmod_cols.shape[1]

    def body(s_ref, d_ref, o_ref):
        o_ref[...] = lax.dot_general(s_ref[...], d_ref[...], (((0,), (0,)), ((), ())), preferred_element_type=f32,
                                     precision=lax.Precision.HIGHEST)

    return pl.pallas_call(body, name=name, out_shape=jax.ShapeDtypeStruct((D, W), f32))(sc_all, dmod_cols)


def _adamw_math(w, g, m, v):
    m = ADAM_B1 * m + (1.0 - ADAM_B1) * g
    v = ADAM_B2 * v + (1.0 - ADAM_B2) * (g * g)
    m_hat = m / (1.0 - ADAM_B1 ** ADAM_STEP)
    v_hat = v / (1.0 - ADAM_B2 ** ADAM_STEP)
    delta = -ADAM_LR * (m_hat / (jnp.sqrt(v_hat) + ADAM_EPS) + ADAM_WD * w)
    return delta, m, v


def _sum_adamw(recvs, w, m, v, name):
    nl = len(recvs)
    n, R, C = recvs[0].shape
    tr = _tile(R, 128 if C > 1024 else 256)
    nr = R // tr

    def body(*refs):
        r_refs, (w_ref, m_ref, v_ref, g_ref, d_ref, nm_ref, nv_ref) = refs[:nl], refs[nl:]
        for li in range(nl):
            @pl.when(pl.program_id(0) == li)
            def _(r_ref=r_refs[li]):
                g = r_ref[0].astype(f32)
                for k in range(1, n):
                    g = g + r_ref[k].astype(f32)
                g_ref[...] = g
                d_ref[...], nm_ref[...], nv_ref[...] = _adamw_math(w_ref[...], g, m_ref[...], v_ref[...])

    spec = pl.BlockSpec((tr, C), lambda l, i: (l * nr + i, 0))
    r_specs = [pl.BlockSpec((n, tr, C), lambda l, i, li=li: (0, jnp.clip(i + (l - li) * nr, 0, nr - 1), 0)) for li in range(nl)]
    return pl.pallas_call(
        body, name=name, grid=(nl, nr), in_specs=r_specs + [spec, spec, spec], out_specs=[spec] * 4,
        out_shape=[jax.ShapeDtypeStruct((nl * R, C), f32)] * 4, compiler_params=_params(("arbitrary", "arbitrary")))(*recvs, w, m, v)


def _pack(arrays, rows):
    flat = []
    for a in arrays:
        a = a.reshape(-1).astype(f32)
        flat.append(jnp.pad(a, (0, (-a.shape[0]) % 128)))
    flat = jnp.concatenate(flat)
    return jnp.pad(flat, (0, rows * 128 - flat.shape[0])).reshape(rows, 128)


def _unpack(slab, shapes):
    flat = slab.reshape(-1)
    out, off = [], 0
    for s in shapes:
        n = 1
        for d in s:
            n *= d
        out.append(flat[off:off + n].reshape(s))
        off += n + (-n) % 128
    return out


def kernel(x, c, ada_w, ada_b, norm_w, w_in, gm_ln_w, gm_ln_b, gm_ws, gm_bs, conv_w, conv_b, dt_bias, a_log, d_skip, ssm_norm_w, w_proj_a, w_proj_b, w_out, final_norm_w, loss_target, m_ada_w, m_ada_b, m_norm_w, m_w_in, m_gm_ln_w, m_gm_ln_b, m_gm_ws, m_gm_bs, m_conv_w, m_conv_b, m_dt_bias, m_a_log, m_d_skip, m_ssm_norm_w, m_w_proj_a, m_w_proj_b, m_w_out, m_final_norm_w, v_ada_w, v_ada_b, v_norm_w, v_w_in, v_gm_ln_w, v_gm_ln_b, v_gm_ws, v_gm_bs, v_conv_w, v_conv_b, v_dt_bias, v_a_log, v_d_skip, v_ssm_norm_w, v_w_proj_a, v_w_proj_b, v_w_out, v_final_norm_w):
    L = 2
    me = 4 * lax.axis_index("x") + 2 * lax.axis_index("y") + lax.axis_index("c")
    W = dict(ada_w=ada_w, ada_b=ada_b, norm_w=norm_w, w_in=w_in, gm_ln_w=gm_ln_w, gm_ln_b=gm_ln_b, gm_ws=gm_ws, gm_bs=gm_bs,
             conv_w=conv_w, conv_b=conv_b, dt_bias=dt_bias, a_log=a_log, d_skip=d_skip, ssm_norm_w=ssm_norm_w, w_proj_a=w_proj_a,
             w_proj_b=w_proj_b, w_out=w_out, final_norm_w=final_norm_w)
    M = dict(ada_w=m_ada_w, ada_b=m_ada_b, norm_w=m_norm_w, w_in=m_w_in, gm_ln_w=m_gm_ln_w, gm_ln_b=m_gm_ln_b, gm_ws=m_gm_ws,
             gm_bs=m_gm_bs, conv_w=m_conv_w, conv_b=m_conv_b, dt_bias=m_dt_bias, a_log=m_a_log, d_skip=m_d_skip,
             ssm_norm_w=m_ssm_norm_w, w_proj_a=m_w_proj_a, w_proj_b=m_w_proj_b, w_out=m_w_out, final_norm_w=m_final_norm_w)
    V = dict(ada_w=v_ada_w, ada_b=v_ada_b, norm_w=v_norm_w, w_in=v_w_in, gm_ln_w=v_gm_ln_w, gm_ln_b=v_gm_ln_b, gm_ws=v_gm_ws,
             gm_bs=v_gm_bs, conv_w=v_conv_w, conv_b=v_conv_b, dt_bias=v_dt_bias, a_log=v_a_log, d_skip=v_d_skip,
             ssm_norm_w=v_ssm_norm_w, w_proj_a=v_w_proj_a, w_proj_b=v_w_proj_b, w_out=v_w_out, final_norm_w=v_final_norm_w)
    SW = NIN // NDEV
    AW = 3 * D // NDEV
    CW = CD // NDEV
    RA, RB = D // NDEV, DI // NDEV

    wire = lambda a: a.astype(_WIRE)
    rows_of = lambda d, l: jnp.concatenate([d["w_proj_a"][l], d["w_proj_b"][l], d["w_out"][l]], axis=0)
    g_win0 = _gather2(wire(w_in[0]), "gather_w_in_l0")
    g_cw = _exchange(conv_w.reshape(L * KC, CW), True, "gather_conv_w").reshape(NDEV, L, KC, CW)
    parts, sc_all = _mod_dist(jnp.broadcast_to(c, (8, D)), wire(ada_w),
                              lax.dynamic_slice_in_dim(ada_b, me * AW, AW, axis=1).reshape(L, 1, AW), "mod_dist")
    mods = [lax.dynamic_index_in_dim(parts[:, l], me, axis=1, keepdims=False).reshape(3, D) for l in range(L)]
    cols = lambda g: jnp.transpose(g, (1, 0, 2)).reshape(g.shape[1], -1)

    def prep(l, g_win, g_rows):
        return _prep_layer(norm_w[l], cols(g_win), gm_ln_w[l], gm_ln_b[l], gm_ws[l], gm_bs[l], cols(g_cw[:, l]), conv_b[l],
                           dt_bias[l], a_log[l], d_skip[l], ssm_norm_w[l], g_rows)

    half = D // 2
    rows0, rows1 = wire(rows_of(W, 0)), wire(rows_of(W, 1))
    hr = rows0.shape[0] // 2
    h, sv0, got, lp0 = _layer_fwd(x[0], mods[0], prep(0, g_win0, None), "l0",
                                  dict(proj_gm=[(rows0[:hr], True)], proj_gate=[(rows0[hr:], True)],
                                       proj_ssd=[(wire(w_in[1][:half]), True)],
                                       ssd_fwd=[(wire(w_in[1][half:]), True), (rows1, True)]))
    lp1 = prep(1, jnp.concatenate([got["proj_ssd"][0], got["ssd_fwd"][0]], axis=1), got["ssd_fwd"][1])
    h, sv1, _, _ = _layer_fwd(h, mods[1], lp1, "l1")
    dh, s_fin = _final_loss(h, loss_target[0], final_norm_w.reshape(1, D), "final_loss")
    loss = lax.psum(0.5 / D * s_fin[1, 0], ("x", "y", "c"))

    small_l = ["gm_ln_w", "gm_ln_b", "gm_ws", "gm_bs", "conv_b", "dt_bias", "a_log", "d_skip", "ssm_norm_w"]
    small_key = dict(gm_ln_w="lw", gm_ln_b="lb", gm_ws="ws", gm_bs="bs", conv_b="cb", dt_bias="dtb", a_log="alog", d_skip="dsk",
                     ssm_norm_w="snw")
    shapes_l = [W[n].shape[1:] for n in small_l] + [(KC, CD)]
    shapes_t = [(L, 3 * D), (L, D), (D,)]
    rows_for = lambda sh: -(-sum(-(-functools.reduce(lambda a, b: a * b, s, 1) // 128) for s in sh) // 256) * 256
    n_rows_l, n_rows_t = rows_for(shapes_l), rows_for(shapes_t)
    slab_of = lambda g: _pack([g[small_key[n]] for n in small_l] + [g["cw"]], n_rows_l)

    dh, g1, _, _ = _layer_bwd(dh, sv1, mods[1], lp1, "l1")
    dh, g0, got, got_own = _layer_bwd(dh, sv0, mods[0], lp0, "l0",
                                      [(_win_blocks(g1), False), (_row_blocks(g1), False), (slab_of(g1), True)], slab_of)
    grads = [g0, g1]
    grad_x = dh[None]
    recv_win = [got_own[1], got[0]]
    recv_rows = [got_own[0], got[1]]
    slab_all = [got_own[2], got[2]]
    st = lambda key: jnp.stack([g[key] for g in grads])
    slab_t = _exchange(_pack([st("mod"), st("nw"), s_fin[0]], n_rows_t), True, "gather_small_tail")

    o_win = [o.reshape(L, D, SW) for o in _sum_adamw(recv_win, w_in.reshape(L * D, SW), m_w_in.reshape(L * D, SW),
                                                     v_w_in.reshape(L * D, SW), "adamw_w_in")]
    rows_all = lambda d: jnp.concatenate([rows_of(d, l) for l in range(L)], axis=0)
    o_rows = [o.reshape(L, 2 * RA + RB, D) for o in _sum_adamw(recv_rows, rows_all(W), rows_all(M), rows_all(V), "adamw_w_rows")]
    o_pa = [o[:, 0:RA] for o in o_rows]
    o_pb = [o[:, RA:RA + RB] for o in o_rows]
    o_po = [o[:, RA + RB:] for o in o_rows]

    def slab_params(d):
        return jnp.concatenate([_pack([d[n][l] for n in small_l] + [jnp.zeros((KC, CD), f32)], n_rows_l) for l in range(L)], axis=0)

    o_small = [[_unpack(o[l * n_rows_l:(l + 1) * n_rows_l], shapes_l) for l in range(L)]
               for o in _sum_adamw(slab_all, slab_params(W), slab_params(M), slab_params(V), "adamw_small")]
    small_out = {n: [jnp.stack([o_small[k][l][i] for l in range(L)]) for k in range(4)] for i, n in enumerate(small_l)}
    tail_params = lambda d: _pack([d["ada_b"], d["norm_w"], d["final_norm_w"]], n_rows_t)
    o_tail = [_unpack(o, shapes_t) for o in _sum_adamw([slab_t], tail_params(W), tail_params(M), tail_params(V), "adamw_small_tail")]
    for i, n in enumerate(["ada_b", "norm_w", "final_norm_w"]):
        small_out[n] = [o_tail[k][i] for k in range(4)]

    g_cw_mine = jnp.stack([lax.dynamic_slice_in_dim(o_small[0][l][len(small_l)], me * CW, CW, axis=1) for l in range(L)])
    o_cw = _sum_adamw([g_cw_mine.reshape(1, L * KC, CW)], conv_w.reshape(L * KC, CW), m_conv_w.reshape(L * KC, CW),
                      v_conv_w.reshape(L * KC, CW), "adamw_conv_w")
    o_cw = [o.reshape(L, KC, CW) for o in o_cw]

    dmod_all = jnp.stack([_unpack(slab_t[k], shapes_t)[0] for k in range(NDEV)])
    dmod_cols = lax.dynamic_slice_in_dim(dmod_all, me * AW, AW, axis=2).reshape(NDEV, L * AW)
    g_ada_w = jnp.transpose(_ada_w_grad(sc_all, dmod_cols, "ada_w_grad").reshape(D, L, AW), (1, 0, 2))
    o_ada = _sum_adamw([g_ada_w.reshape(1, L * D, AW)], ada_w.reshape(L * D, AW), m_ada_w.reshape(L * D, AW),
                       v_ada_w.reshape(L * D, AW), "adamw_ada_w")
    o_ada = [o.reshape(L, D, AW) for o in o_ada]

    big = dict(ada_w=o_ada, w_in=o_win, conv_w=o_cw, w_proj_a=o_pa, w_proj_b=o_pb, w_out=o_po)
    order = ["ada_w", "ada_b", "norm_w", "w_in", "gm_ln_w", "gm_ln_b", "gm_ws", "gm_bs", "conv_w", "conv_b", "dt_bias", "a_log",
             "d_skip", "ssm_norm_w", "w_proj_a", "w_proj_b", "w_out", "final_norm_w"]
    pick = lambda n, k: big[n][k] if n in big else small_out[n][k]
    return (loss, grad_x, *[pick(n, 0) for n in order], *[pick(n, 1) for n in order], *[pick(n, 2) for n in order],
            *[pick(n, 3) for n in order])
```

```python
import functools

import jax
import jax.numpy as jnp
from jax import lax
from jax.experimental import pallas as pl
from jax.experimental.pallas import tpu as pltpu

f32 = jnp.float32
_MXU = jnp.bfloat16
_WIRE = jnp.bfloat16

D = 1024
Q = 128
NG = 8
DI = 2048
NH = 32
P = 64
HPG = 4
NS = 128
KC = 4
CD = 4096
GRP = DI // NG
EPS = 1e-6
NDEV = 8
WA = 3 * D
WB = DI + CD + 256
WG = 2 * D
OFF_XBC = DI
OFF_DT = DI + CD
NIN = 11296
VMEM_LIMIT = 56 * 1024 * 1024
NEG = -1e30

ADAM_LR, ADAM_B1, ADAM_B2, ADAM_EPS, ADAM_WD, ADAM_STEP = 0.001, 0.9, 0.999, 1e-08, 0.01, 10


def _bf(x):
    return x.astype(_MXU)


def _dot(a, b):
    return jnp.dot(a, b, preferred_element_type=f32)


def _dot_nt(a, b):
    return lax.dot_general(a, b, (((1,), (1,)), ((), ())), preferred_element_type=f32)


def _dot_tn(a, b):
    return lax.dot_general(a, b, (((0,), (0,)), ((), ())), preferred_element_type=f32)


def _dot_exact(a, b):
    return jnp.dot(a, b, preferred_element_type=f32, precision=lax.Precision.HIGHEST)


def _sigmoid(x):
    return jax.nn.sigmoid(x)


def _silu(x):
    return x * _sigmoid(x)


def _dsilu(x):
    s = _sigmoid(x)
    return s * (1.0 + x * (1.0 - s))


_GK = 0.7978845608028654
_GC = 0.044715


def _gelu(x):
    return 0.5 * x * (1.0 + jnp.tanh(_GK * (x + _GC * x * x * x)))


def _dgelu(x):
    t = jnp.tanh(_GK * (x + _GC * x * x * x))
    return 0.5 * (1.0 + t) + 0.5 * x * (1.0 - t * t) * _GK * (1.0 + 3.0 * _GC * x * x)


def _softplus(x):
    return jnp.maximum(x, 0.0) + jnp.log1p(jnp.exp(-jnp.abs(x)))


def _tile(n, cap):
    if n <= cap:
        return n
    best = None
    for t in range(128, cap + 1, 128):
        if n % t == 0:
            best = t
    assert best is not None, (n, cap)
    return best


def _params(sem):
    return pltpu.CompilerParams(dimension_semantics=sem, vmem_limit_bytes=VMEM_LIMIT)


def _exch_ops(src_ref, out_ref, send_sems, recv_sems, local_sem, gather):
    x, y, c = lax.axis_index("x"), lax.axis_index("y"), lax.axis_index("c")
    me = 4 * x + 2 * y + c

    def peer(j):
        px = 1 - x if (j >> 2) & 1 else x
        py = 1 - y if (j >> 1) & 1 else y
        pc = 1 - c if j & 1 else c
        return (px, py, pc), 4 * px + 2 * py + pc

    def copy(j, landing):
        dev, idx = peer(j)
        return pltpu.make_async_remote_copy(
            src_ref=src_ref if gather else src_ref.at[idx], dst_ref=out_ref.at[idx] if landing else out_ref.at[me],
            send_sem=send_sems.at[j - 1], recv_sem=recv_sems.at[j - 1], device_id=dev, device_id_type=pl.DeviceIdType.MESH)

    mine = pltpu.make_async_copy(src_ref if gather else src_ref.at[me], out_ref.at[me], local_sem)

    def start():
        mine.start()
        for j in range(1, NDEV):
            copy(j, False).start()

    def finish():
        for j in range(1, NDEV):
            copy(j, True).wait_recv()
        for j in range(1, NDEV):
            copy(j, False).wait_send()
        mine.wait()

    return start, finish


def _exch_shape(src, gather):
    return jax.ShapeDtypeStruct((NDEV,) + tuple(src.shape if gather else src.shape[1:]), src.dtype)


_EXCH_SEMS = [pltpu.SemaphoreType.DMA((NDEV - 1,)), pltpu.SemaphoreType.DMA((NDEV - 1,)), pltpu.SemaphoreType.DMA]


def _carry_call(body, name, grid, in_specs, out_specs, out_shape, scratch_shapes, args, carry=()):
    n_in, n_out, n_sc, nx = len(in_specs), len(out_specs), len(scratch_shapes), len(carry)
    sem = ("arbitrary",) * len(grid)
    if nx == 0:
        outs = pl.pallas_call(body, name=name, grid=grid, in_specs=in_specs, out_specs=out_specs, out_shape=out_shape,
                              scratch_shapes=scratch_shapes, compiler_params=_params(sem))(*args)
        return list(outs), []

    def wrapped(*refs):
        ins, srcs = refs[:n_in], refs[n_in:n_in + nx]
        outs, dsts = refs[n_in + nx:n_in + nx + n_out], refs[n_in + nx + n_out:n_in + 2 * nx + n_out]
        scratch, sems = refs[n_in + 2 * nx + n_out:n_in + 2 * nx + n_out + n_sc], refs[n_in + 2 * nx + n_out + n_sc:]
        ops = [_exch_ops(srcs[i], dsts[i], sems[3 * i], sems[3 * i + 1], sems[3 * i + 2], carry[i][1]) for i in range(nx)]
        first = functools.reduce(jnp.logical_and, [pl.program_id(d) == 0 for d in range(len(grid))])
        last = functools.reduce(jnp.logical_and, [pl.program_id(d) == grid[d] - 1 for d in range(len(grid))])

        @pl.when(first)
        def _():
            for start, _ in ops:
                start()

        body(*ins, *outs, *scratch)

        @pl.when(last)
        def _():
            for _, finish in ops:
                finish()

    hbm = pl.BlockSpec(memory_space=pl.ANY)
    outs = pl.pallas_call(
        wrapped, name=name, grid=grid, in_specs=list(in_specs) + [hbm] * nx, out_specs=list(out_specs) + [hbm] * nx,
        out_shape=list(out_shape) + [_exch_shape(s, g) for s, g in carry], scratch_shapes=list(scratch_shapes) + _EXCH_SEMS * nx,
        compiler_params=_params(sem))(*args, *[s for s, _ in carry])
    return list(outs[:n_out]), list(outs[n_out:])


def _mm(a, b, mode, name, out_dtype=f32, tm_cap=1024, tn_cap=1280, tk_cap=1280, carry=()):
    if mode == "nn":
        (M, K), (K2, N) = a.shape, b.shape
    elif mode == "nt":
        (M, K), (N, K2) = a.shape, b.shape
    else:
        (K, M), (K2, N) = a.shape, b.shape
        tk_cap = min(tk_cap, 1024)
    assert K == K2, (a.shape, b.shape, mode)
    tm, tn, tk = _tile(M, tm_cap), _tile(N, tn_cap), _tile(K, tk_cap)
    nk = K // tk
    if mode == "nn":
        a_spec = pl.BlockSpec((tm, tk), lambda i, j, k: (i, k))
        b_spec = pl.BlockSpec((tk, tn), lambda i, j, k: (k, j))
        dot = _dot
    elif mode == "nt":
        a_spec = pl.BlockSpec((tm, tk), lambda i, j, k: (i, k))
        b_spec = pl.BlockSpec((tn, tk), lambda i, j, k: (j, k))
        dot = _dot_nt
    else:
        a_spec = pl.BlockSpec((tk, tm), lambda i, j, k: (k, i))
        b_spec = pl.BlockSpec((tk, tn), lambda i, j, k: (k, j))
        dot = _dot_tn
    o_spec = pl.BlockSpec((tm, tn), lambda i, j, k: (i, j))

    def body_one(a_ref, b_ref, o_ref):
        o_ref[...] = dot(_bf(a_ref[...]), _bf(b_ref[...])).astype(out_dtype)

    def body_acc(a_ref, b_ref, o_ref, acc_ref):
        k = pl.program_id(2)

        @pl.when(k == 0)
        def _():
            acc_ref[...] = jnp.zeros_like(acc_ref)

        acc_ref[...] += dot(_bf(a_ref[...]), _bf(b_ref[...]))

        @pl.when(k == nk - 1)
        def _():
            o_ref[...] = acc_ref[...].astype(out_dtype)

    (out,), exchanged = _carry_call(body_one if nk == 1 else body_acc, name, (M // tm, N // tn, nk), [a_spec, b_spec], [o_spec],
                                    [jax.ShapeDtypeStruct((M, N), out_dtype)], [] if nk == 1 else [pltpu.VMEM((tm, tn), f32)],
                                    (a, b), carry)
    return (out, exchanged) if carry else out


def _row_spec(ts, w, col=0):
    return pl.BlockSpec((ts, w), lambda i: (i, col))


def _full_spec(shape):
    nd = len(shape)
    return pl.BlockSpec(shape, lambda i: (0,) * nd)


def _modulate(x, nw, shift, scale, name):
    T = x.shape[0]
    ts = _tile(T, 512)

    def body(x_ref, nw_ref, sh_ref, sc_ref, h_ref):
        xv = x_ref[...]
        r = lax.rsqrt(jnp.mean(xv * xv, axis=-1, keepdims=True) + EPS)
        h_ref[...] = ((xv * r * nw_ref[...]) * (1.0 + sc_ref[...]) + sh_ref[...]).astype(h_ref.dtype)

    return pl.pallas_call(
        body, name=name, grid=(T // ts,),
        in_specs=[_row_spec(ts, D), _full_spec((1, D)), _full_spec((1, D)), _full_spec((1, D))],
        out_specs=_row_spec(ts, D), out_shape=jax.ShapeDtypeStruct((T, D), _MXU),
        compiler_params=_params(("arbitrary",)))(x, nw, shift, scale)


def _final_loss(x, tgt, fw, name):
    T = x.shape[0]
    ts = _tile(T, 512)

    def body(x_ref, t_ref, fw_ref, dx_ref, acc_ref):
        @pl.when(pl.program_id(0) == 0)
        def _():
            acc_ref[...] = jnp.zeros_like(acc_ref)

        xv = x_ref[...]
        r = lax.rsqrt(jnp.mean(xv * xv, axis=-1, keepdims=True) + EPS)
        xn = xv * r
        e = xn * fw_ref[...] - t_ref[...]
        dy = e * (1.0 / D)
        acc_ref[0:1, :] += jnp.sum(dy * xn, axis=0, keepdims=True)
        acc_ref[1:2, :] += jnp.sum(jnp.sum(e * e, axis=0, keepdims=True), axis=1, keepdims=True)
        dxn = dy * fw_ref[...]
        dx_ref[...] = r * (dxn - xn * jnp.mean(dxn * xn, axis=-1, keepdims=True))

    return pl.pallas_call(
        body, name=name, grid=(T // ts,),
        in_specs=[_row_spec(ts, D), _row_spec(ts, D), _full_spec((1, D))],
        out_specs=[_row_spec(ts, D), _full_spec((8, D))],
        out_shape=[jax.ShapeDtypeStruct((T, D), f32), jax.ShapeDtypeStruct((8, D), f32)],
        compiler_params=_params(("arbitrary",)))(x, tgt, fw)


def _tril(n):
    return lax.broadcasted_iota(jnp.int32, (n, n), 0) >= lax.broadcasted_iota(jnp.int32, (n, n), 1)


def _gm_chunk_fwd(u, v, z, lw, lb, ws_ref, bsx):
    gu, gv = _gelu(u), _gelu(v)
    mu = jnp.mean(gv, axis=-1, keepdims=True)
    cen = gv - mu
    rstd = lax.rsqrt(jnp.mean(cen * cen, axis=-1, keepdims=True) + EPS)
    vhat = cen * rstd
    vn = _bf(vhat * lw + lb)
    tri = _tril(Q)
    mixed = jnp.concatenate(
        [_dot(_bf(jnp.where(tri, ws_ref[g], 0.0)), vn[:, g * Q:(g + 1) * Q]) for g in range(NG)], axis=1) + bsx
    return gu, vhat, rstd, vn, mixed


def _gmlp_fwd(pA, lw, lb, ws, bsx, name):
    T = pA.shape[0]
    ts = _tile(T, 512)

    def body(u_ref, v_ref, z_ref, lw_ref, lb_ref, ws_ref, bsx_ref, y_ref):
        def chunk(ci, carry):
            rows = pl.ds(pl.multiple_of(ci * Q, Q), Q)
            u, v, z = u_ref[rows, :].astype(f32), v_ref[rows, :].astype(f32), z_ref[rows, :].astype(f32)
            gu, _, _, _, mixed = _gm_chunk_fwd(u, v, z, lw_ref[...], lb_ref[...], ws_ref, bsx_ref[...])
            y_ref[rows, :] = (gu * mixed * _silu(z)).astype(y_ref.dtype)
            return carry

        lax.fori_loop(0, ts // Q, chunk, 0)

    return pl.pallas_call(
        body, name=name, grid=(T // ts,),
        in_specs=[_row_spec(ts, D, 0), _row_spec(ts, D, 1), _row_spec(ts, D, 2), _full_spec((1, D)), _full_spec((1, D)),
                  _full_spec((NG, Q, Q)), _full_spec((Q, D))],
        out_specs=_row_spec(ts, D), out_shape=jax.ShapeDtypeStruct((T, D), _MXU),
        compiler_params=_params(("arbitrary",)))(pA, pA, pA, lw, lb, ws, bsx)


def _gmlp_bwd(dya, pA, lw, lb, ws, bsx, name):
    T = pA.shape[0]
    ts = _tile(T, 512)

    def body(dy_ref, u_ref, v_ref, z_ref, lw_ref, lb_ref, ws_ref, bsx_ref, dp_ref, acc_ref, dws_ref, dbs_ref):
        @pl.when(pl.program_id(0) == 0)
        def _():
            acc_ref[...] = jnp.zeros_like(acc_ref)
            dws_ref[...] = jnp.zeros_like(dws_ref)
            dbs_ref[...] = jnp.zeros_like(dbs_ref)

        tri = _tril(Q)

        def chunk(ci, carry):
            rows = pl.ds(pl.multiple_of(ci * Q, Q), Q)
            u, v, z, dy = u_ref[rows, :].astype(f32), v_ref[rows, :].astype(f32), z_ref[rows, :].astype(f32), dy_ref[rows, :].astype(f32)
            gu, vhat, rstd, vn, mixed = _gm_chunk_fwd(u, v, z, lw_ref[...], lb_ref[...], ws_ref, bsx_ref[...])
            sz = _silu(z)
            dp_ref[rows, 0:D] = (dy * mixed * sz * _dgelu(u)).astype(dp_ref.dtype)
            dp_ref[rows, 2 * D:3 * D] = (dy * gu * mixed * _dsilu(z)).astype(dp_ref.dtype)
            dmixed = dy * gu * sz
            dbs_ref[...] += dmixed
            dmb = _bf(dmixed)
            dvn_parts = []
            for g in range(NG):
                cols = slice(g * Q, (g + 1) * Q)
                wg = _bf(jnp.where(tri, ws_ref[g], 0.0))
                dvn_parts.append(_dot_tn(wg, dmb[:, cols]))
                dws_ref[g] += jnp.where(tri, _dot_nt(dmb[:, cols], vn[:, cols]), 0.0)
            dvn = jnp.concatenate(dvn_parts, axis=1)
            acc_ref[0:1, :] += jnp.sum(dvn * vhat, axis=0, keepdims=True)
            acc_ref[1:2, :] += jnp.sum(dvn, axis=0, keepdims=True)
            dvh = dvn * lw_ref[...]
            dgv = rstd * (dvh - jnp.mean(dvh, axis=-1, keepdims=True) - vhat * jnp.mean(dvh * vhat, axis=-1, keepdims=True))
            dp_ref[rows, D:2 * D] = (dgv * _dgelu(v)).astype(dp_ref.dtype)
            return carry

        lax.fori_loop(0, ts // Q, chunk, 0)

    return pl.pallas_call(
        body, name=name, grid=(T // ts,),
        in_specs=[_row_spec(ts, D), _row_spec(ts, D, 0), _row_spec(ts, D, 1), _row_spec(ts, D, 2), _full_spec((1, D)),
                  _full_spec((1, D)), _full_spec((NG, Q, Q)), _full_spec((Q, D))],
        out_specs=[_row_spec(ts, WA), _full_spec((8, D)), _full_spec((NG, Q, Q)), _full_spec((Q, D))],
        out_shape=[jax.ShapeDtypeStruct((T, WA), _MXU), jax.ShapeDtypeStruct((8, D), f32),
                   jax.ShapeDtypeStruct((NG, Q, Q), f32), jax.ShapeDtypeStruct((Q, D), f32)],
        compiler_params=_params(("arbitrary",)))(dya, pA, pA, pA, lw, lb, ws, bsx)


def _head_maps():
    h = lax.broadcasted_iota(jnp.int32, (128, DI), 0)
    ch = lax.broadcasted_iota(jnp.int32, (128, DI), 1)
    ex = (ch // P == h).astype(_MXU)
    return ex, ex.T


def _split(v, parts):
    out = []
    for _ in range(parts - 1):
        p = _bf(v)
        out.append(p)
        v = v - p.astype(f32)
    out.append(_bf(v))
    return out


def _expand(v, ex_ref, parts):
    acc = None
    for p in _split(v, parts):
        t = _dot(p, ex_ref[...])
        acc = t if acc is None else acc + t
    return acc


def _reduce(v, rd_ref, parts=2):
    acc = None
    for p in _split(v, parts):
        t = _dot(p, rd_ref[...])
        acc = t if acc is None else acc + t
    return acc


def _ssd_time(pb_ref, dtb_ref, alog_ref):
    xdt = pb_ref[:, OFF_DT:OFF_DT + 128] + dtb_ref[...]
    dt = _softplus(xdt)
    a = -jnp.exp(alog_ref[...])
    cs = _dot_exact(_tril(Q).astype(f32), dt * a)
    return xdt, dt, a, cs


def _head_mask(r):
    return lax.broadcasted_iota(jnp.int32, (Q, GRP), 1) // P == r


def _ssd_group_fwd(g, xa_ref, s_prev, cs, cs_t, dtx_ref, csx_ref, dsk_ref):
    cols = slice(g * GRP, (g + 1) * GRP)
    xs = xa_ref[:, cols]
    bb = _bf(xa_ref[:, DI + g * NS:DI + (g + 1) * NS])
    cb = _bf(xa_ref[:, DI + NG * NS + g * NS:DI + NG * NS + (g + 1) * NS])
    gm = _dot_nt(cb, bb)
    xd = xs * dtx_ref[:, cols]
    csx = csx_ref[:, cols]
    csl = csx_ref[Q - 1:Q, cols]
    tri = _tril(Q)
    lms = [jnp.exp(jnp.where(tri, cs[:, HPG * g + r:HPG * g + r + 1] - cs_t[HPG * g + r:HPG * g + r + 1, :], NEG))
           for r in range(HPG)]
    mfs = [gm * lm for lm in lms]
    mcat = jnp.concatenate([_bf(m) for m in mfs], axis=1)
    xbd = jnp.concatenate([_bf(jnp.where(_head_mask(r), xd, 0.0)) for r in range(HPG)], axis=0)
    ydiag = _dot(mcat, xbd)
    yoff = jnp.exp(csx) * _dot(cb, _bf(s_prev))
    y = ydiag + yoff + xs * dsk_ref[:, cols]
    xdd = xd * jnp.exp(csl - csx)
    s_new = s_prev * jnp.exp(csl) + _dot_tn(bb, _bf(xdd))
    return y, s_new, (xs, bb, cb, xd, lms, mfs, mcat, xbd, yoff, xdd, csx, csl)


def _ssd_fwd(pB, cw, cb, dtb, alog, dsk, snw, name, carry=()):
    T = pB.shape[0]
    nc = T // Q
    ex, _ = _head_maps()

    def body(pb_ref, cw_ref, cb_ref, dtb_ref, alog_ref, dsk_ref, snw_ref, ex_ref, y_ref, xc_ref, st_ref,
             s_ref, ext_ref, xa_ref, dtx_ref, csx_ref):
        @pl.when(pl.program_id(0) == 0)
        def _():
            s_ref[...] = jnp.zeros_like(s_ref)
            ext_ref[...] = jnp.zeros_like(ext_ref)

        ext_ref[8:8 + Q, :] = pb_ref[:, OFF_XBC:OFF_XBC + CD]
        for j in range(CD // 512):
            cj = slice(j * 512, (j + 1) * 512)
            e = ext_ref[:, cj]
            xc = cb_ref[:, cj] + cw_ref[KC - 1:KC, cj] * e[8:8 + Q]
            for s in range(1, KC):
                xc = xc + cw_ref[KC - 1 - s:KC - s, cj] * pltpu.roll(e, s, 0)[8:8 + Q]
            xc_ref[:, cj] = xc
            xa_ref[:, cj] = _silu(xc)
        ext_ref[0:8, :] = ext_ref[Q:Q + 8, :]

        _, dt, _, cs = _ssd_time(pb_ref, dtb_ref, alog_ref)
        cs_t = cs.T
        dtx_ref[...] = _expand(dt, ex_ref, 2)
        csx_ref[...] = _expand(cs, ex_ref, 3)
        for g in range(NG):
            s_prev = s_ref[g]
            st_ref[0, g] = s_prev
            y, s_new, _ = _ssd_group_fwd(g, xa_ref, s_prev, cs, cs_t, dtx_ref, csx_ref, dsk_ref)
            s_ref[g] = s_new
            cols = slice(g * GRP, (g + 1) * GRP)
            yz = y * _silu(pb_ref[:, cols])
            rr = lax.rsqrt(jnp.mean(yz * yz, axis=-1, keepdims=True) + EPS)
            y_ref[:, cols] = (yz * rr * snw_ref[:, cols]).astype(y_ref.dtype)

    outs, exchanged = _carry_call(
        body, name, (nc,),
        [_row_spec(Q, WB), _full_spec((8, CD)), _full_spec((1, CD)), _full_spec((1, 128)), _full_spec((1, 128)),
         _full_spec((1, DI)), _full_spec((1, DI)), _full_spec((128, DI))],
        [_row_spec(Q, DI), _row_spec(Q, CD), pl.BlockSpec((1, NG, NS, GRP), lambda i: (i, 0, 0, 0))],
        [jax.ShapeDtypeStruct((T, DI), _MXU), jax.ShapeDtypeStruct((T, CD), f32), jax.ShapeDtypeStruct((nc, NG, NS, GRP), f32)],
        [pltpu.VMEM((NG, NS, GRP), f32), pltpu.VMEM((Q + 8, CD), f32), pltpu.VMEM((Q, CD), f32),
         pltpu.VMEM((Q, DI), f32), pltpu.VMEM((Q, DI), f32)],
        (pB, cw, cb, dtb, alog, dsk, snw, ex), carry)
    return outs + [exchanged]


def _ssd_bwd(dyb, pB, xc, states, cw, dtb, alog, dsk, snw, name, carry=()):
    T = pB.shape[0]
    nc = T // Q
    ex, rd = _head_maps()
    selr = (lax.broadcasted_iota(jnp.int32, (HPG * Q, 128), 0) // Q == lax.broadcasted_iota(jnp.int32, (HPG * Q, 128), 1)).astype(_MXU)

    def body(dy_ref, pb_ref, xc_ref, st_ref, cw_ref, dtb_ref, alog_ref, dsk_ref, snw_ref, ex_ref, rd_ref, selr_ref,
             dp_ref, dcw_ref, dcb_ref, dhd_ref, dcol_ref,
             ds_ref, xa_ref, sg_ref, dxa_ref, dxe_ref, dtx_ref, csx_ref, rcs_ref, rdt_ref, rl_ref, dcs_ref, dcst_ref):
        @pl.when(pl.program_id(0) == 0)
        def _():
            ds_ref[...] = jnp.zeros_like(ds_ref)
            dxe_ref[...] = jnp.zeros_like(dxe_ref)
            dcw_ref[...] = jnp.zeros_like(dcw_ref)
            dcb_ref[...] = jnp.zeros_like(dcb_ref)
            dhd_ref[...] = jnp.zeros_like(dhd_ref)
            dcol_ref[...] = jnp.zeros_like(dcol_ref)
            rl_ref[...] = jnp.zeros_like(rl_ref)
            dcst_ref[...] = jnp.zeros_like(dcst_ref)

        dcs_ref[...] = jnp.zeros_like(dcs_ref)
        for j in range(CD // 512):
            cj = slice(j * 512, (j + 1) * 512)
            sg_ref[:, cj] = _sigmoid(xc_ref[:, cj])
            xa_ref[:, cj] = xc_ref[:, cj] * sg_ref[:, cj]
        xdt, dt, a, cs = _ssd_time(pb_ref, dtb_ref, alog_ref)
        cs_t = cs.T
        dtx_ref[...] = _expand(dt, ex_ref, 2)
        csx_ref[...] = _expand(cs, ex_ref, 3)

        for g in range(NG):
            cols = slice(g * GRP, (g + 1) * GRP)
            s_prev = st_ref[0, g]
            sb = _bf(s_prev)
            y, _, (xs, bb, cbm, xd, lms, mfs, mcat, xbd, yoff, xdd, csx, csl) = _ssd_group_fwd(
                g, xa_ref, s_prev, cs, cs_t, dtx_ref, csx_ref, dsk_ref)
            z = pb_ref[:, cols]
            sz = _silu(z)
            yz = y * sz
            rr = lax.rsqrt(jnp.mean(yz * yz, axis=-1, keepdims=True) + EPS)
            nrm = yz * rr
            dyb_g = dy_ref[:, cols].astype(f32)
            dcol_ref[1:2, cols] += jnp.sum(dyb_g * nrm, axis=0, keepdims=True)
            dn = dyb_g * snw_ref[:, cols]
            dyz = rr * (dn - nrm * jnp.mean(dn * nrm, axis=-1, keepdims=True))
            dyv = dyz * sz
            dp_ref[:, cols] = (dyz * y * _dsilu(z)).astype(dp_ref.dtype)
            dcol_ref[0:1, cols] += jnp.sum(dyv * xs, axis=0, keepdims=True)
            dy16 = _bf(dyv)
            dmcat = _dot_nt(dy16, xbd)
            dg = dmcat[:, 0:Q] * lms[0]
            for r in range(1, HPG):
                dg = dg + dmcat[:, r * Q:(r + 1) * Q] * lms[r]
            e16 = _bf(jnp.concatenate([dmcat[:, r * Q:(r + 1) * Q] * mfs[r] for r in range(HPG)], axis=1))
            rows = _dot(e16, selr_ref[...])
            dcs_ref[...] += rows if g == 0 else pltpu.roll(rows, HPG * g, 1)
            csum = _dot(jnp.ones((8, Q), _MXU), e16)
            for r in range(HPG):
                dcst_ref[HPG * g + r:HPG * g + r + 1, :] = csum[0:1, r * Q:(r + 1) * Q]
            big = _dot_tn(mcat, dy16)
            dxd_diag = jnp.where(_head_mask(0), big[0:Q], 0.0)
            for r in range(1, HPG):
                dxd_diag = dxd_diag + jnp.where(_head_mask(r), big[r * Q:(r + 1) * Q], 0.0)
            dsn = ds_ref[g]
            dsn16 = _bf(dsn)
            dxdd = _dot(bb, dsn16)
            t_state = dxdd * xdd
            dxd = dxd_diag + dxdd * jnp.exp(csl - csx)
            rcs_ref[:, cols] = dyv * yoff - t_state
            rl_ref[0:1, cols] = (jnp.sum(t_state, axis=0, keepdims=True)
                                 + jnp.exp(csl) * jnp.sum(s_prev * dsn, axis=0, keepdims=True))
            rdt_ref[:, cols] = dxd * xs
            dxa_ref[:, cols] = dyv * dsk_ref[:, cols] + dxd * dtx_ref[:, cols]
            dg16 = _bf(dg)
            dw16 = _bf(dyv * jnp.exp(csx))
            dxa_ref[:, DI + NG * NS + g * NS:DI + NG * NS + (g + 1) * NS] = _dot(dg16, bb) + _dot_nt(dw16, sb)
            dxa_ref[:, DI + g * NS:DI + (g + 1) * NS] = _dot_tn(dg16, cbm) + _dot_nt(_bf(xdd), dsn16)
            ds_ref[g] = dsn * jnp.exp(csl) + _dot_tn(cbm, dw16)

        row = lax.broadcasted_iota(jnp.int32, (Q, 128), 0)
        dcs = (dcs_ref[...] - dcst_ref[...].T + _reduce(rcs_ref[...], rd_ref)
               + jnp.where(row == Q - 1, _reduce(rl_ref[...], rd_ref)[0:1, :], 0.0))
        upper = lax.broadcasted_iota(jnp.int32, (Q, Q), 0) <= lax.broadcasted_iota(jnp.int32, (Q, Q), 1)
        dadt = _dot_exact(upper.astype(f32), dcs)
        valid = lax.broadcasted_iota(jnp.int32, (Q, 128), 1) < NH
        ddt = jnp.where(valid, _reduce(rdt_ref[...], rd_ref, 1) + dadt * a, 0.0)
        ddtr = ddt * _sigmoid(xdt)
        dhd_ref[0:1, :] += jnp.sum(ddtr, axis=0, keepdims=True)
        dhd_ref[1:2, :] += jnp.sum(jnp.where(valid, dadt * dt * a, 0.0), axis=0, keepdims=True)
        dp_ref[:, OFF_DT:OFF_DT + 128] = ddtr.astype(dp_ref.dtype)
        dp_ref[:, OFF_DT + 128:WB] = jnp.zeros((Q, WB - OFF_DT - 128), dp_ref.dtype)
        for j in range(CD // 512):
            cj = slice(j * 512, (j + 1) * 512)
            sg = sg_ref[:, cj]
            dxc = dxa_ref[:, cj] * (sg + xa_ref[:, cj] * (1.0 - sg))
            dxe_ref[0:Q, cj] = dxc
            dcb_ref[0:1, cj] += jnp.sum(dxc, axis=0, keepdims=True)
            raw = pb_ref[:, OFF_XBC + j * 512:OFF_XBC + (j + 1) * 512]
            e = dxe_ref[:, cj]
            dcw_ref[KC - 1:KC, cj] += jnp.sum(dxc * raw, axis=0, keepdims=True)
            dxb = cw_ref[KC - 1:KC, cj] * dxc
            for s in range(1, KC):
                sh = pltpu.roll(e, Q + 8 - s, 0)[0:Q]
                dcw_ref[KC - 1 - s:KC - s, cj] += jnp.sum(sh * raw, axis=0, keepdims=True)
                dxb = dxb + cw_ref[KC - 1 - s:KC - s, cj] * sh
            dp_ref[:, OFF_XBC + j * 512:OFF_XBC + (j + 1) * 512] = dxb.astype(dp_ref.dtype)
        dxe_ref[Q:Q + 8, :] = dxe_ref[0:8, :]

    rev = lambda i: (nc - 1 - i, 0)
    outs, exchanged = _carry_call(
        body, name, (nc,),
        [pl.BlockSpec((Q, DI), rev), pl.BlockSpec((Q, WB), rev), pl.BlockSpec((Q, CD), rev),
         pl.BlockSpec((1, NG, NS, GRP), lambda i: (nc - 1 - i, 0, 0, 0)),
         _full_spec((8, CD)), _full_spec((1, 128)), _full_spec((1, 128)),
         _full_spec((1, DI)), _full_spec((1, DI)), _full_spec((128, DI)), _full_spec((DI, 128)), _full_spec((HPG * Q, 128))],
        [pl.BlockSpec((Q, WB), rev), _full_spec((8, CD)), _full_spec((8, CD)), _full_spec((8, 128)), _full_spec((8, DI))],
        [jax.ShapeDtypeStruct((T, WB), _MXU), jax.ShapeDtypeStruct((8, CD), f32), jax.ShapeDtypeStruct((8, CD), f32),
         jax.ShapeDtypeStruct((8, 128), f32), jax.ShapeDtypeStruct((8, DI), f32)],
        [pltpu.VMEM((NG, NS, GRP), f32), pltpu.VMEM((Q, CD), f32), pltpu.VMEM((Q, CD), f32), pltpu.VMEM((Q, CD), f32),
         pltpu.VMEM((Q + 8, CD), f32), pltpu.VMEM((Q, DI), f32), pltpu.VMEM((Q, DI), f32),
         pltpu.VMEM((Q, DI), f32), pltpu.VMEM((Q, DI), f32), pltpu.VMEM((8, DI), f32),
         pltpu.VMEM((Q, 128), f32), pltpu.VMEM((128, Q), f32)],
        (dyb, pB, xc, states, cw, dtb, alog, dsk, snw, ex, rd, selr), carry)
    return outs + [exchanged]


def _merge_fwd(ya, yb, pG, x, gate, wa, wb, wo, name, carry=()):
    T = x.shape[0]
    ts = _tile(T, 256)

    def body(ya_ref, yb_ref, g_ref, x_ref, gate_ref, wa_ref, wb_ref, wo_ref, xo_ref, mg_ref, pa_ref, pb_ref):
        pa = _dot(ya_ref[...], wa_ref[...])
        pb = _dot(yb_ref[...], wb_ref[...])
        merged = _sigmoid(g_ref[:, 0:D].astype(f32)) * pa + _sigmoid(g_ref[:, D:2 * D].astype(f32)) * pb
        mg = _bf(merged)
        xo_ref[...] = x_ref[...] + gate_ref[...] * _dot(mg, wo_ref[...])
        mg_ref[...] = mg
        pa_ref[...] = pa.astype(pa_ref.dtype)
        pb_ref[...] = pb.astype(pb_ref.dtype)

    outs, exchanged = _carry_call(
        body, name, (T // ts,),
        [_row_spec(ts, D), _row_spec(ts, DI), _row_spec(ts, WG), _row_spec(ts, D), _full_spec((1, D)),
         _full_spec((D, D)), _full_spec((DI, D)), _full_spec((D, D))],
        [_row_spec(ts, D)] * 4,
        [jax.ShapeDtypeStruct((T, D), f32), jax.ShapeDtypeStruct((T, D), _MXU), jax.ShapeDtypeStruct((T, D), _MXU),
         jax.ShapeDtypeStruct((T, D), _MXU)],
        [], (ya, yb, pG, x, gate, wa, wb, wo), carry)
    return outs + [exchanged]


def _merge_bwd(dxout, merged, pa, pb, pG, gate, wo, name):
    T = dxout.shape[0]
    ts = _tile(T, 256)

    def body(dx_ref, mg_ref, pa_ref, pb_ref, g_ref, gate_ref, wo_ref, do_ref, dpa_ref, dpb_ref, dg_ref, acc_ref):
        @pl.when(pl.program_id(0) == 0)
        def _():
            acc_ref[...] = jnp.zeros_like(acc_ref)

        dxo = dx_ref[...]
        acc_ref[0:1, :] += jnp.sum(dxo * _dot(mg_ref[...], wo_ref[...]), axis=0, keepdims=True)
        do = _bf(dxo * gate_ref[...])
        do_ref[...] = do
        dmerged = _dot_nt(do, wo_ref[...])
        sa, sb = _sigmoid(g_ref[:, 0:D].astype(f32)), _sigmoid(g_ref[:, D:2 * D].astype(f32))
        dpa_ref[...] = (dmerged * sa).astype(dpa_ref.dtype)
        dpb_ref[...] = (dmerged * sb).astype(dpb_ref.dtype)
        dg_ref[:, 0:D] = (dmerged * pa_ref[...].astype(f32) * sa * (1.0 - sa)).astype(dg_ref.dtype)
        dg_ref[:, D:2 * D] = (dmerged * pb_ref[...].astype(f32) * sb * (1.0 - sb)).astype(dg_ref.dtype)

    return pl.pallas_call(
        body, name=name, grid=(T // ts,),
        in_specs=[_row_spec(ts, D), _row_spec(ts, D), _row_spec(ts, D), _row_spec(ts, D), _row_spec(ts, WG),
                  _full_spec((1, D)), _full_spec((D, D))],
        out_specs=[_row_spec(ts, D), _row_spec(ts, D), _row_spec(ts, D), _row_spec(ts, WG), _full_spec((8, D))],
        out_shape=[jax.ShapeDtypeStruct((T, D), _MXU), jax.ShapeDtypeStruct((T, D), _MXU), jax.ShapeDtypeStruct((T, D), _MXU),
                   jax.ShapeDtypeStruct((T, WG), _MXU), jax.ShapeDtypeStruct((8, D), f32)],
        compiler_params=_params(("arbitrary",)))(dxout, merged, pa, pb, pG, gate, wo)


def _rows_split(r):
    ra, rb = D // NDEV, DI // NDEV
    return r[:, 0:ra].reshape(D, D), r[:, ra:ra + rb].reshape(DI, D), r[:, ra + rb:].reshape(D, D)


def _layer_fwd(x, mod, lp, tag, carry=None):
    carry = carry or {}
    got = {}

    def mm(key, w, **kw):
        if carry.get(key):
            out, got[key] = _mm(h, w, "nn", f"{key}_{tag}", carry=carry[key], **kw)
            return out
        return _mm(h, w, "nn", f"{key}_{tag}", **kw)

    h = _modulate(x, lp["nw"], mod[0:1], mod[1:2], f"modulate_{tag}")
    pA = mm("proj_gm", lp["w_gm"], out_dtype=_MXU)
    pB = mm("proj_ssd", lp["w_ssd"])
    pG = mm("proj_gate", lp["w_g"], out_dtype=_MXU)
    if "wa" not in lp:
        lp = dict(lp)
        lp["wa"], lp["wb"], lp["wo"] = _rows_split(jnp.concatenate([got["proj_gm"][0], got["proj_gate"][0]], axis=1))
    ya = _gmlp_fwd(pA, lp["lw"], lp["lb"], lp["ws"], lp["bsx"], f"gmlp_fwd_{tag}")
    yb, xc, states, got["ssd_fwd"] = _ssd_fwd(pB, lp["cw"], lp["cb"], lp["dtb"], lp["alog"], lp["dsk"], lp["snw"], f"ssd_fwd_{tag}",
                                              carry.get("ssd_fwd", ()))
    xo, merged, pa, pb, _ = _merge_fwd(ya, yb, pG, x, mod[2:3], lp["wa"], lp["wb"], lp["wo"], f"merge_fwd_{tag}")
    return xo, dict(x=x, h=h, pA=pA, pB=pB, pG=pG, ya=ya, yb=yb, xc=xc, states=states, merged=merged, pa=pa, pb=pb), got, lp


def _dh_modulate_bwd(dpG, dpA, dpB, w_g, w_gm, w_ssd, dxout, x, nw, scale, name, carry=()):
    T = x.shape[0]
    tm = _tile(T, 1024)
    fam = [(WG, _tile(WG, 1024)), (WA, _tile(WA, 1024)), (WB, _tile(WB, 640))]
    steps = [w // k for w, k in fam]
    first = [0, steps[0], steps[0] + steps[1]]
    ns = sum(steps)

    def chunk(f):
        return lambda i, s: jnp.clip(s - first[f], 0, steps[f] - 1)

    a_specs = [pl.BlockSpec((tm, fam[f][1]), lambda i, s, c=chunk(f): (i, c(i, s))) for f in range(3)]
    b_specs = [pl.BlockSpec((D, fam[f][1]), lambda i, s, c=chunk(f): (0, c(i, s))) for f in range(3)]
    tok = pl.BlockSpec((tm, D), lambda i, s: (i, 0))
    vec = pl.BlockSpec((1, D), lambda i, s: (0, 0))

    def body(ag, aa, ab, bg, ba, bb, dxo_ref, x_ref, nw_ref, sc_ref, dx_ref, sum_ref, acc_ref):
        i, s = pl.program_id(0), pl.program_id(1)

        @pl.when(jnp.logical_and(i == 0, s == 0))
        def _():
            sum_ref[...] = jnp.zeros_like(sum_ref)

        @pl.when(s == 0)
        def _():
            acc_ref[...] = jnp.zeros_like(acc_ref)

        @pl.when(s < first[1])
        def _():
            acc_ref[...] += _dot_nt(ag[...], bg[...])

        @pl.when(jnp.logical_and(s >= first[1], s < first[2]))
        def _():
            acc_ref[...] += _dot_nt(aa[...], ba[...])

        @pl.when(s >= first[2])
        def _():
            acc_ref[...] += _dot_nt(ab[...], bb[...])

        @pl.when(s == ns - 1)
        def _():
            xv, dh_v = x_ref[...], acc_ref[...]
            r = lax.rsqrt(jnp.mean(xv * xv, axis=-1, keepdims=True) + EPS)
            xn = xv * r
            hn = xn * nw_ref[...]
            dhn = dh_v * (1.0 + sc_ref[...])
            sum_ref[0:1, :] += jnp.sum(dh_v, axis=0, keepdims=True)
            sum_ref[1:2, :] += jnp.sum(dh_v * hn, axis=0, keepdims=True)
            sum_ref[2:3, :] += jnp.sum(dhn * xn, axis=0, keepdims=True)
            dxn = dhn * nw_ref[...]
            dx_ref[...] = dxo_ref[...] + r * (dxn - xn * jnp.mean(dxn * xn, axis=-1, keepdims=True))

    outs, exchanged = _carry_call(
        body, name, (T // tm, ns), a_specs + b_specs + [tok, tok, vec, vec], [tok, pl.BlockSpec((8, D), lambda i, s: (0, 0))],
        [jax.ShapeDtypeStruct((T, D), f32), jax.ShapeDtypeStruct((8, D), f32)], [pltpu.VMEM((tm, D), f32)],
        (dpG, dpA, dpB, w_g, w_gm, w_ssd, dxout, x, nw, scale), carry)
    return outs + [exchanged]


def _win_blocks(g):
    full = jnp.concatenate([g["w_gm"], g["w_ssd"][:, 0:DI + CD], g["w_ssd"][:, DI + CD:DI + CD + NH], g["w_g"]], axis=1)
    return jnp.transpose(full.reshape(D, NDEV, NIN // NDEV), (1, 0, 2))


def _row_blocks(g):
    return jnp.concatenate([g["wa"].reshape(NDEV, D // NDEV, D), g["wb"].reshape(NDEV, DI // NDEV, D),
                            g["wo"].reshape(NDEV, D // NDEV, D)], axis=1)


def _layer_bwd(dxo, sv, mod, lp, tag, carry=(), scatter_own=None):
    do, dpa, dpb, dpG, s_gate = _merge_bwd(dxo, sv["merged"], sv["pa"], sv["pb"], sv["pG"], mod[2:3], lp["wo"], f"merge_bwd_{tag}")
    g = {}
    g["wo"] = _mm(sv["merged"], do, "tn", f"dw_out_{tag}", out_dtype=_WIRE)
    g["wa"] = _mm(sv["ya"], dpa, "tn", f"dw_proj_a_{tag}", out_dtype=_WIRE)
    g["wb"] = _mm(sv["yb"], dpb, "tn", f"dw_proj_b_{tag}", out_dtype=_WIRE)
    dya = _mm(dpa, lp["wa"], "nt", f"dy_a_{tag}", out_dtype=_MXU)
    dyb = _mm(dpb, lp["wb"], "nt", f"dy_b_{tag}", out_dtype=_MXU)
    dpA, s_ln, g["ws"], dbsx = _gmlp_bwd(dya, sv["pA"], lp["lw"], lp["lb"], lp["ws"], lp["bsx"], f"gmlp_bwd_{tag}")
    carry = list(carry) + ([(_row_blocks(g), False)] if scatter_own else [])
    dpB, g["cw"], s_cb, s_hd, s_col, got = _ssd_bwd(dyb, sv["pB"], sv["xc"], sv["states"], lp["cw"], lp["dtb"], lp["alog"],
                                                    lp["dsk"], lp["snw"], f"ssd_bwd_{tag}", carry)
    g["lw"], g["lb"] = s_ln[0], s_ln[1]
    g["bs"] = dbsx.reshape(Q, NG, Q).sum(-1).T
    g["cb"] = s_cb[0]
    g["dtb"], g["alog"] = s_hd[0, :NH], s_hd[1, :NH]
    g["dsk"] = s_col[0].reshape(NH, P).sum(-1)
    g["snw"] = s_col[1]
    g["cw"] = g["cw"][0:KC]
    g["w_g"] = _mm(sv["h"], dpG, "tn", f"dw_gate_{tag}", out_dtype=_WIRE)
    g["w_gm"] = _mm(sv["h"], dpA, "tn", f"dw_gm_{tag}", out_dtype=_WIRE)
    g["w_ssd"] = _mm(sv["h"], dpB, "tn", f"dw_ssd_{tag}", out_dtype=_WIRE)
    dx, s_mod, got_dh = _dh_modulate_bwd(dpG, dpA, dpB, lp["w_g"], lp["w_gm"], lp["w_ssd"], dxo, sv["x"], lp["nw"], mod[1:2],
                                         f"dh_{tag}", [(_win_blocks(g), False), (scatter_own(g), True)] if scatter_own else ())
    if scatter_own:
        got, got_own = got[:-1], [got[-1], got_dh[0], got_dh[1]]
    else:
        got_own = []
    g["mod"] = jnp.concatenate([s_mod[0], s_mod[1], s_gate[0]])
    g["nw"] = s_mod[2]
    return dx, g, got, got_own


def _prep_layer(nw, w_in_full, lw, lb, ws, bs, cw_full, cb, dtb, alog, dsk, snw, rows=None):
    z = jnp.zeros((D, WB - (DI + CD + NH)), w_in_full.dtype)
    pad_h = lambda v: jnp.pad(v, (0, 128 - NH)).reshape(1, 128)
    extra = dict(zip(("wa", "wb", "wo"), _rows_split(rows))) if rows is not None else {}
    return dict(
        **extra,
        nw=nw.reshape(1, D),
        w_gm=w_in_full[:, 0:WA],
        w_ssd=jnp.concatenate([w_in_full[:, WA:WA + DI + CD], w_in_full[:, WA + DI + CD:WA + DI + CD + NH], z], axis=1),
        w_g=w_in_full[:, WA + DI + CD + NH:NIN],
        lw=lw.reshape(1, D), lb=lb.reshape(1, D), ws=ws, bsx=jnp.repeat(bs.T, Q, axis=1),
        cw=jnp.pad(cw_full, ((0, 8 - KC), (0, 0))), cb=cb.reshape(1, CD), dtb=pad_h(dtb), alog=pad_h(alog),
        dsk=jnp.repeat(dsk, P).reshape(1, DI), snw=snw.reshape(1, DI))


def _exchange(src, gather, name):
    if not gather:
        assert src.shape[0] == NDEV

    def body(src_ref, out_ref, send_sems, recv_sems, local_sem):
        start, finish = _exch_ops(src_ref, out_ref, send_sems, recv_sems, local_sem, gather)
        start()
        finish()

    return pl.pallas_call(
        body, name=name, out_shape=_exch_shape(src, gather),
        in_specs=[pl.BlockSpec(memory_space=pl.ANY)], out_specs=pl.BlockSpec(memory_space=pl.ANY), scratch_shapes=_EXCH_SEMS)(src)


def _gather2(src, name):
    def body(src_ref, out_ref, send_sems, recv_sems, local_sem):
        x, y, c = lax.axis_index("x"), lax.axis_index("y"), lax.axis_index("c")
        me, sibling = (x, y, c), (x, y, 1 - c)
        chips = [(1 - x, y), (x, 1 - y), (1 - x, 1 - y)]

        def slot(px, py, pc):
            return out_ref.at[4 * px + 2 * py + pc]

        def copy(k, block, to, src=None):
            return pltpu.make_async_remote_copy(
                src_ref=slot(*block) if src is None else src, dst_ref=slot(*block), send_sem=send_sems.at[k],
                recv_sem=recv_sems.at[k], device_id=to, device_id_type=pl.DeviceIdType.MESH)

        mine = pltpu.make_async_copy(src_ref, slot(*me), local_sem)
        mine.start()
        first = [copy(0, me, sibling, src=src_ref)] + [copy(1 + j, me, (*chip, c), src=src_ref) for j, chip in enumerate(chips)]
        for cp in first:
            cp.start()
        passed = [copy(4 + j, (*chip, c), sibling) for j, chip in enumerate(chips)]
        for j, chip in enumerate(chips):
            copy(1 + j, (*chip, c), me).wait_recv()
            passed[j].start()
        copy(0, sibling, me).wait_recv()
        for j, chip in enumerate(chips):
            copy(4 + j, (*chip, 1 - c), me).wait_recv()
        for cp in first + passed:
            cp.wait_send()
        mine.wait()

    return pl.pallas_call(
        body, name=name, out_shape=_exch_shape(src, True),
        in_specs=[pl.BlockSpec(memory_space=pl.ANY)], out_specs=pl.BlockSpec(memory_space=pl.ANY), scratch_shapes=_EXCH_SEMS)(src)


def _mod_dist(c8, ada_w, ada_b_cols, name):
    L, _, AW = ada_w.shape

    def body(c_ref, w_ref, b_ref, parts_ref, sc_ref, call_ref, mine_ref, send1, recv1, send2, recv2):
        x, y, c = lax.axis_index("x"), lax.axis_index("y"), lax.axis_index("c")
        me = 4 * x + 2 * y + c

        def peer(j):
            px = 1 - x if (j >> 2) & 1 else x
            py = 1 - y if (j >> 1) & 1 else y
            pc = 1 - c if j & 1 else c
            return (px, py, pc), 4 * px + 2 * py + pc

        def copy(j, src, dst, sems, landing):
            dev, idx = peer(j)
            return pltpu.make_async_remote_copy(
                src_ref=src, dst_ref=dst.at[idx] if landing else dst.at[me], send_sem=sems[0].at[j - 1], recv_sem=sems[1].at[j - 1],
                device_id=dev, device_id_type=pl.DeviceIdType.MESH)

        def all_to_all(src, dst, sems):
            for j in range(1, NDEV):
                copy(j, src, dst, sems, False).start()
            for j in range(1, NDEV):
                copy(j, src, dst, sems, True).wait_recv()
            for j in range(1, NDEV):
                copy(j, src, dst, sems, False).wait_send()

        call_ref[me] = c_ref[...]
        all_to_all(c_ref, call_ref, (send1, recv1))
        row = lax.broadcasted_iota(jnp.int32, (8, D), 0)
        cm = jnp.zeros((8, D), f32)
        for k in range(NDEV):
            cm = jnp.where(row == k, call_ref[k], cm)
        sc = _silu(cm)
        sc_ref[...] = sc
        for l in range(L):
            mine_ref[l] = _dot(_bf(sc), w_ref[l]) + b_ref[l]
        parts_ref[me] = mine_ref[...]
        all_to_all(mine_ref, parts_ref, (send2, recv2))

    vmem = pl.BlockSpec(memory_space=pltpu.VMEM)
    sems = pltpu.SemaphoreType.DMA((NDEV - 1,))
    return pl.pallas_call(
        body, name=name, in_specs=[vmem, vmem, vmem], out_specs=[vmem, vmem],
        out_shape=[jax.ShapeDtypeStruct((NDEV, L, 8, AW), f32), jax.ShapeDtypeStruct((8, D), f32)],
        scratch_shapes=[pltpu.VMEM((NDEV, 8, D), f32), pltpu.VMEM((L, 8, AW), f32), sems, sems, sems, sems])(c8, ada_w, ada_b_cols)


def _ada_w_grad(sc_all, dmod_cols, name):
    W = dmod_cols.shape[1]

    def body(s_ref, d_ref, o_ref):
        o_ref[...] = lax.dot_general(s_ref[...], d_ref[...], (((0,), (0,)), ((), ())), preferred_element_type=f32,
                                     precision=lax.Precision.HIGHEST)

    return pl.pallas_call(body, name=name, out_shape=jax.ShapeDtypeStruct((D, W), f32))(sc_all, dmod_cols)


def _adamw_math(w, g, m, v):
    m = ADAM_B1 * m + (1.0 - ADAM_B1) * g
    v = ADAM_B2 * v + (1.0 - ADAM_B2) * (g * g)
    m_hat = m / (1.0 - ADAM_B1 ** ADAM_STEP)
    v_hat = v / (1.0 - ADAM_B2 ** ADAM_STEP)
    delta = -ADAM_LR * (m_hat / (jnp.sqrt(v_hat) + ADAM_EPS) + ADAM_WD * w)
    return delta, m, v


def _sum_adamw(recvs, w, m, v, name):
    nl = len(recvs)
    n, R, C = recvs[0].shape
    tr = _tile(R, 128 if C > 1024 else 256)
    nr = R // tr
    stacked = w.ndim == 3
    at = (lambda ref: ref.at[0]) if stacked else (lambda ref: ref)

    def body(*refs):
        r_refs, (w_ref, m_ref, v_ref, g_ref, d_ref, nm_ref, nv_ref) = refs[:nl], [at(r) for r in refs[nl:]]
        for li in range(nl):
            @pl.when(pl.program_id(0) == li)
            def _(r_ref=r_refs[li]):
                g = r_ref[0].astype(f32)
                for k in range(1, n):
                    g = g + r_ref[k].astype(f32)
                g_ref[...] = g
                d_ref[...], nm_ref[...], nv_ref[...] = _adamw_math(w_ref[...], g, m_ref[...], v_ref[...])

    if stacked:
        spec = pl.BlockSpec((1, tr, C), lambda l, i: (l, i, 0))
    else:
        spec = pl.BlockSpec((tr, C), lambda l, i: (l * nr + i, 0))
    r_specs = [pl.BlockSpec((n, tr, C), lambda l, i, li=li: (0, jnp.clip(i + (l - li) * nr, 0, nr - 1), 0)) for li in range(nl)]
    return pl.pallas_call(
        body, name=name, grid=(nl, nr), in_specs=r_specs + [spec, spec, spec], out_specs=[spec] * 4,
        out_shape=[jax.ShapeDtypeStruct(w.shape, f32)] * 4, compiler_params=_params(("arbitrary", "arbitrary")))(*recvs, w, m, v)


def _pack(arrays, rows):
    flat = []
    for a in arrays:
        a = a.reshape(-1).astype(f32)
        flat.append(jnp.pad(a, (0, (-a.shape[0]) % 128)))
    flat = jnp.concatenate(flat)
    return jnp.pad(flat, (0, rows * 128 - flat.shape[0])).reshape(rows, 128)


def _unpack(slab, shapes):
    flat = slab.reshape(-1)
    out, off = [], 0
    for s in shapes:
        n = 1
        for d in s:
            n *= d
        out.append(flat[off:off + n].reshape(s))
        off += n + (-n) % 128
    return out


def kernel(x, c, ada_w, ada_b, norm_w, w_in, gm_ln_w, gm_ln_b, gm_ws, gm_bs, conv_w, conv_b, dt_bias, a_log, d_skip, ssm_norm_w, w_proj_a, w_proj_b, w_out, final_norm_w, loss_target, m_ada_w, m_ada_b, m_norm_w, m_w_in, m_gm_ln_w, m_gm_ln_b, m_gm_ws, m_gm_bs, m_conv_w, m_conv_b, m_dt_bias, m_a_log, m_d_skip, m_ssm_norm_w, m_w_proj_a, m_w_proj_b, m_w_out, m_final_norm_w, v_ada_w, v_ada_b, v_norm_w, v_w_in, v_gm_ln_w, v_gm_ln_b, v_gm_ws, v_gm_bs, v_conv_w, v_conv_b, v_dt_bias, v_a_log, v_d_skip, v_ssm_norm_w, v_w_proj_a, v_w_proj_b, v_w_out, v_final_norm_w):
    L = 2
    me = 4 * lax.axis_index("x") + 2 * lax.axis_index("y") + lax.axis_index("c")
    W = dict(ada_w=ada_w, ada_b=ada_b, norm_w=norm_w, w_in=w_in, gm_ln_w=gm_ln_w, gm_ln_b=gm_ln_b, gm_ws=gm_ws, gm_bs=gm_bs,
             conv_w=conv_w, conv_b=conv_b, dt_bias=dt_bias, a_log=a_log, d_skip=d_skip, ssm_norm_w=ssm_norm_w, w_proj_a=w_proj_a,
             w_proj_b=w_proj_b, w_out=w_out, final_norm_w=final_norm_w)
    M = dict(ada_w=m_ada_w, ada_b=m_ada_b, norm_w=m_norm_w, w_in=m_w_in, gm_ln_w=m_gm_ln_w, gm_ln_b=m_gm_ln_b, gm_ws=m_gm_ws,
             gm_bs=m_gm_bs, conv_w=m_conv_w, conv_b=m_conv_b, dt_bias=m_dt_bias, a_log=m_a_log, d_skip=m_d_skip,
             ssm_norm_w=m_ssm_norm_w, w_proj_a=m_w_proj_a, w_proj_b=m_w_proj_b, w_out=m_w_out, final_norm_w=m_final_norm_w)
    V = dict(ada_w=v_ada_w, ada_b=v_ada_b, norm_w=v_norm_w, w_in=v_w_in, gm_ln_w=v_gm_ln_w, gm_ln_b=v_gm_ln_b, gm_ws=v_gm_ws,
             gm_bs=v_gm_bs, conv_w=v_conv_w, conv_b=v_conv_b, dt_bias=v_dt_bias, a_log=v_a_log, d_skip=v_d_skip,
             ssm_norm_w=v_ssm_norm_w, w_proj_a=v_w_proj_a, w_proj_b=v_w_proj_b, w_out=v_w_out, final_norm_w=v_final_norm_w)
    SW = NIN // NDEV
    AW = 3 * D // NDEV
    CW = CD // NDEV
    RA, RB = D // NDEV, DI // NDEV

    wire = lambda a: a.astype(_WIRE)
    rows_of = lambda d, l: jnp.concatenate([d["w_proj_a"][l], d["w_proj_b"][l], d["w_out"][l]], axis=0)
    g_win0 = _gather2(wire(w_in[0]), "gather_w_in_l0")
    g_cw = _exchange(conv_w.reshape(L * KC, CW), True, "gather_conv_w").reshape(NDEV, L, KC, CW)
    parts, sc_all = _mod_dist(jnp.broadcast_to(c, (8, D)), wire(ada_w),
                              lax.dynamic_slice_in_dim(ada_b, me * AW, AW, axis=1).reshape(L, 1, AW), "mod_dist")
    mods = [lax.dynamic_index_in_dim(parts[:, l], me, axis=1, keepdims=False).reshape(3, D) for l in range(L)]
    cols = lambda g: jnp.transpose(g, (1, 0, 2)).reshape(g.shape[1], -1)

    def prep(l, g_win, g_rows):
        return _prep_layer(norm_w[l], cols(g_win), gm_ln_w[l], gm_ln_b[l], gm_ws[l], gm_bs[l], cols(g_cw[:, l]), conv_b[l],
                           dt_bias[l], a_log[l], d_skip[l], ssm_norm_w[l], g_rows)

    half = D // 2
    rows0, rows1 = wire(rows_of(W, 0)), wire(rows_of(W, 1))
    hr = rows0.shape[0] // 2
    h, sv0, got, lp0 = _layer_fwd(x[0], mods[0], prep(0, g_win0, None), "l0",
                                  dict(proj_gm=[(rows0[:hr], True)], proj_gate=[(rows0[hr:], True)],
                                       proj_ssd=[(wire(w_in[1][:half]), True)],
                                       ssd_fwd=[(wire(w_in[1][half:]), True), (rows1, True)]))
    lp1 = prep(1, jnp.concatenate([got["proj_ssd"][0], got["ssd_fwd"][0]], axis=1), got["ssd_fwd"][1])
    h, sv1, _, _ = _layer_fwd(h, mods[1], lp1, "l1")
    dh, s_fin = _final_loss(h, loss_target[0], final_norm_w.reshape(1, D), "final_loss")
    loss = lax.psum(0.5 / D * s_fin[1, 0], ("x", "y", "c"))

    small_l = ["gm_ln_w", "gm_ln_b", "gm_ws", "gm_bs", "conv_b", "dt_bias", "a_log", "d_skip", "ssm_norm_w"]
    small_key = dict(gm_ln_w="lw", gm_ln_b="lb", gm_ws="ws", gm_bs="bs", conv_b="cb", dt_bias="dtb", a_log="alog", d_skip="dsk",
                     ssm_norm_w="snw")
    shapes_l = [W[n].shape[1:] for n in small_l] + [(KC, CD)]
    shapes_t = [(L, 3 * D), (L, D), (D,)]
    rows_for = lambda sh: -(-sum(-(-functools.reduce(lambda a, b: a * b, s, 1) // 128) for s in sh) // 256) * 256
    n_rows_l, n_rows_t = rows_for(shapes_l), rows_for(shapes_t)
    slab_of = lambda g: _pack([g[small_key[n]] for n in small_l] + [g["cw"]], n_rows_l)

    dh, g1, _, _ = _layer_bwd(dh, sv1, mods[1], lp1, "l1")
    dh, g0, got, got_own = _layer_bwd(dh, sv0, mods[0], lp0, "l0",
                                      [(_win_blocks(g1), False), (_row_blocks(g1), False), (slab_of(g1), True)], slab_of)
    grads = [g0, g1]
    grad_x = dh[None]
    recv_win = [got_own[1], got[0]]
    recv_rows = [got_own[0], got[1]]
    slab_all = [got_own[2], got[2]]
    st = lambda key: jnp.stack([g[key] for g in grads])
    slab_t = _exchange(_pack([st("mod"), st("nw"), s_fin[0]], n_rows_t), True, "gather_small_tail")

    o_win = _sum_adamw(recv_win, w_in, m_w_in, v_w_in, "adamw_w_in")
    rows_all = lambda d: jnp.concatenate([rows_of(d, l) for l in range(L)], axis=0)
    o_rows = [o.reshape(L, 2 * RA + RB, D) for o in _sum_adamw(recv_rows, rows_all(W), rows_all(M), rows_all(V), "adamw_w_rows")]
    o_pa = [o[:, 0:RA] for o in o_rows]
    o_pb = [o[:, RA:RA + RB] for o in o_rows]
    o_po = [o[:, RA + RB:] for o in o_rows]

    def slab_params(d):
        return jnp.concatenate([_pack([d[n][l] for n in small_l] + [jnp.zeros((KC, CD), f32)], n_rows_l) for l in range(L)], axis=0)

    o_small = [[_unpack(o[l * n_rows_l:(l + 1) * n_rows_l], shapes_l) for l in range(L)]
               for o in _sum_adamw(slab_all, slab_params(W), slab_params(M), slab_params(V), "adamw_small")]
    small_out = {n: [jnp.stack([o_small[k][l][i] for l in range(L)]) for k in range(4)] for i, n in enumerate(small_l)}
    tail_params = lambda d: _pack([d["ada_b"], d["norm_w"], d["final_norm_w"]], n_rows_t)
    o_tail = [_unpack(o, shapes_t) for o in _sum_adamw([slab_t], tail_params(W), tail_params(M), tail_params(V), "adamw_small_tail")]
    for i, n in enumerate(["ada_b", "norm_w", "final_norm_w"]):
        small_out[n] = [o_tail[k][i] for k in range(4)]

    g_cw_mine = jnp.stack([lax.dynamic_slice_in_dim(o_small[0][l][len(small_l)], me * CW, CW, axis=1) for l in range(L)])
    o_cw = _sum_adamw([g_cw_mine.reshape(1, L * KC, CW)], conv_w.reshape(L * KC, CW), m_conv_w.reshape(L * KC, CW),
                      v_conv_w.reshape(L * KC, CW), "adamw_conv_w")
    o_cw = [o.reshape(L, KC, CW) for o in o_cw]

    dmod_all = jnp.stack([_unpack(slab_t[k], shapes_t)[0] for k in range(NDEV)])
    dmod_cols = lax.dynamic_slice_in_dim(dmod_all, me * AW, AW, axis=2).reshape(NDEV, L * AW)
    g_ada_w = jnp.transpose(_ada_w_grad(sc_all, dmod_cols, "ada_w_grad").reshape(D, L, AW), (1, 0, 2))
    o_ada = _sum_adamw([g_ada_w[l][None] for l in range(L)], ada_w, m_ada_w, v_ada_w, "adamw_ada_w")

    big = dict(ada_w=o_ada, w_in=o_win, conv_w=o_cw, w_proj_a=o_pa, w_proj_b=o_pb, w_out=o_po)
    order = ["ada_w", "ada_b", "norm_w", "w_in", "gm_ln_w", "gm_ln_b", "gm_ws", "gm_bs", "conv_w", "conv_b", "dt_bias", "a_log",
             "d_skip", "ssm_norm_w", "w_proj_a", "w_proj_b", "w_out", "final_norm_w"]
    pick = lambda n, k: big[n][k] if n in big else small_out[n][k]
    return (loss, grad_x, *[pick(n, 0) for n in order], *[pick(n, 1) for n in order], *[pick(n, 2) for n in order],
            *[pick(n, 3) for n in order])
```

```python
import functools

import jax
import jax.numpy as jnp
from jax import lax
from jax.experimental import pallas as pl
from jax.experimental.pallas import tpu as pltpu

f32 = jnp.float32
_MXU = jnp.bfloat16
_WIRE = jnp.bfloat16

D = 1024
Q = 128
NG = 8
DI = 2048
NH = 32
P = 64
HPG = 4
NS = 128
KC = 4
CD = 4096
GRP = DI // NG
EPS = 1e-6
NDEV = 8
WA = 3 * D
WB = DI + CD + 256
WG = 2 * D
OFF_XBC = DI
OFF_DT = DI + CD
NIN = 11296
VMEM_LIMIT = 56 * 1024 * 1024
NEG = -1e30

ADAM_LR, ADAM_B1, ADAM_B2, ADAM_EPS, ADAM_WD, ADAM_STEP = 0.001, 0.9, 0.999, 1e-08, 0.01, 10


def _bf(x):
    return x.astype(_MXU)


def _dot(a, b):
    return jnp.dot(a, b, preferred_element_type=f32)


def _dot_nt(a, b):
    return lax.dot_general(a, b, (((1,), (1,)), ((), ())), preferred_element_type=f32)


def _dot_tn(a, b):
    return lax.dot_general(a, b, (((0,), (0,)), ((), ())), preferred_element_type=f32)


def _dot_exact(a, b):
    return jnp.dot(a, b, preferred_element_type=f32, precision=lax.Precision.HIGHEST)


def _sigmoid(x):
    return jax.nn.sigmoid(x)


def _silu(x):
    return x * _sigmoid(x)


def _dsilu(x):
    s = _sigmoid(x)
    return s * (1.0 + x * (1.0 - s))


_GK = 0.7978845608028654
_GC = 0.044715


def _gelu(x):
    return 0.5 * x * (1.0 + jnp.tanh(_GK * (x + _GC * x * x * x)))


def _dgelu(x):
    t = jnp.tanh(_GK * (x + _GC * x * x * x))
    return 0.5 * (1.0 + t) + 0.5 * x * (1.0 - t * t) * _GK * (1.0 + 3.0 * _GC * x * x)


def _softplus(x):
    return jnp.maximum(x, 0.0) + jnp.log1p(jnp.exp(-jnp.abs(x)))


def _tile(n, cap):
    if n <= cap:
        return n
    best = None
    for t in range(128, cap + 1, 128):
        if n % t == 0:
            best = t
    assert best is not None, (n, cap)
    return best


def _params(sem):
    return pltpu.CompilerParams(dimension_semantics=sem, vmem_limit_bytes=VMEM_LIMIT)


def _exch_ops(src_ref, out_ref, send_sems, recv_sems, local_sem, gather):
    x, y, c = lax.axis_index("x"), lax.axis_index("y"), lax.axis_index("c")
    me = 4 * x + 2 * y + c

    def peer(j):
        px = 1 - x if (j >> 2) & 1 else x
        py = 1 - y if (j >> 1) & 1 else y
        pc = 1 - c if j & 1 else c
        return (px, py, pc), 4 * px + 2 * py + pc

    def copy(j, landing):
        dev, idx = peer(j)
        return pltpu.make_async_remote_copy(
            src_ref=src_ref if gather else src_ref.at[idx], dst_ref=out_ref.at[idx] if landing else out_ref.at[me],
            send_sem=send_sems.at[j - 1], recv_sem=recv_sems.at[j - 1], device_id=dev, device_id_type=pl.DeviceIdType.MESH)

    mine = pltpu.make_async_copy(src_ref if gather else src_ref.at[me], out_ref.at[me], local_sem)

    def start():
        mine.start()
        for j in range(1, NDEV):
            copy(j, False).start()

    def finish():
        for j in range(1, NDEV):
            copy(j, True).wait_recv()
        for j in range(1, NDEV):
            copy(j, False).wait_send()
        mine.wait()

    return start, finish


def _exch_shape(src, gather):
    return jax.ShapeDtypeStruct((NDEV,) + tuple(src.shape if gather else src.shape[1:]), src.dtype)


_EXCH_SEMS = [pltpu.SemaphoreType.DMA((NDEV - 1,)), pltpu.SemaphoreType.DMA((NDEV - 1,)), pltpu.SemaphoreType.DMA]


def _carry_call(body, name, grid, in_specs, out_specs, out_shape, scratch_shapes, args, carry=()):
    n_in, n_out, n_sc, nx = len(in_specs), len(out_specs), len(scratch_shapes), len(carry)
    sem = ("arbitrary",) * len(grid)
    if nx == 0:
        outs = pl.pallas_call(body, name=name, grid=grid, in_specs=in_specs, out_specs=out_specs, out_shape=out_shape,
                              scratch_shapes=scratch_shapes, compiler_params=_params(sem))(*args)
        return list(outs), []

    def wrapped(*refs):
        ins, srcs = refs[:n_in], refs[n_in:n_in + nx]
        outs, dsts = refs[n_in + nx:n_in + nx + n_out], refs[n_in + nx + n_out:n_in + 2 * nx + n_out]
        scratch, sems = refs[n_in + 2 * nx + n_out:n_in + 2 * nx + n_out + n_sc], refs[n_in + 2 * nx + n_out + n_sc:]
        ops = [_exch_ops(srcs[i], dsts[i], sems[3 * i], sems[3 * i + 1], sems[3 * i + 2], carry[i][1]) for i in range(nx)]
        first = functools.reduce(jnp.logical_and, [pl.program_id(d) == 0 for d in range(len(grid))])
        last = functools.reduce(jnp.logical_and, [pl.program_id(d) == grid[d] - 1 for d in range(len(grid))])

        @pl.when(first)
        def _():
            for start, _ in ops:
                start()

        body(*ins, *outs, *scratch)

        @pl.when(last)
        def _():
            for _, finish in ops:
                finish()

    hbm = pl.BlockSpec(memory_space=pl.ANY)
    outs = pl.pallas_call(
        wrapped, name=name, grid=grid, in_specs=list(in_specs) + [hbm] * nx, out_specs=list(out_specs) + [hbm] * nx,
        out_shape=list(out_shape) + [_exch_shape(s, g) for s, g in carry], scratch_shapes=list(scratch_shapes) + _EXCH_SEMS * nx,
        compiler_params=_params(sem))(*args, *[s for s, _ in carry])
    return list(outs[:n_out]), list(outs[n_out:])


def _mm(a, b, mode, name, out_dtype=f32, tm_cap=1024, tn_cap=1280, tk_cap=1280, carry=()):
    if mode == "nn":
        (M, K), (K2, N) = a.shape, b.shape
    elif mode == "nt":
        (M, K), (N, K2) = a.shape, b.shape
    else:
        (K, M), (K2, N) = a.shape, b.shape
        tk_cap = min(tk_cap, 1024)
    assert K == K2, (a.shape, b.shape, mode)
    tm, tn, tk = _tile(M, tm_cap), _tile(N, tn_cap), _tile(K, tk_cap)
    nk = K // tk
    if mode == "nn":
        a_spec = pl.BlockSpec((tm, tk), lambda i, j, k: (i, k))
        b_spec = pl.BlockSpec((tk, tn), lambda i, j, k: (k, j))
        dot = _dot
    elif mode == "nt":
        a_spec = pl.BlockSpec((tm, tk), lambda i, j, k: (i, k))
        b_spec = pl.BlockSpec((tn, tk), lambda i, j, k: (j, k))
        dot = _dot_nt
    else:
        a_spec = pl.BlockSpec((tk, tm), lambda i, j, k: (k, i))
        b_spec = pl.BlockSpec((tk, tn), lambda i, j, k: (k, j))
        dot = _dot_tn
    o_spec = pl.BlockSpec((tm, tn), lambda i, j, k: (i, j))

    def body_one(a_ref, b_ref, o_ref):
        o_ref[...] = dot(_bf(a_ref[...]), _bf(b_ref[...])).astype(out_dtype)

    def body_acc(a_ref, b_ref, o_ref, acc_ref):
        k = pl.program_id(2)

        @pl.when(k == 0)
        def _():
            acc_ref[...] = jnp.zeros_like(acc_ref)

        acc_ref[...] += dot(_bf(a_ref[...]), _bf(b_ref[...]))

        @pl.when(k == nk - 1)
        def _():
            o_ref[...] = acc_ref[...].astype(out_dtype)

    (out,), exchanged = _carry_call(body_one if nk == 1 else body_acc, name, (M // tm, N // tn, nk), [a_spec, b_spec], [o_spec],
                                    [jax.ShapeDtypeStruct((M, N), out_dtype)], [] if nk == 1 else [pltpu.VMEM((tm, tn), f32)],
                                    (a, b), carry)
    return (out, exchanged) if carry else out


def _row_spec(ts, w, col=0):
    return pl.BlockSpec((ts, w), lambda i: (i, col))


def _full_spec(shape):
    nd = len(shape)
    return pl.BlockSpec(shape, lambda i: (0,) * nd)


def _modulate(x, nw, shift, scale, name):
    T = x.shape[0]
    ts = _tile(T, 512)

    def body(x_ref, nw_ref, sh_ref, sc_ref, h_ref):
        xv = x_ref[...]
        r = lax.rsqrt(jnp.mean(xv * xv, axis=-1, keepdims=True) + EPS)
        h_ref[...] = ((xv * r * nw_ref[...]) * (1.0 + sc_ref[...]) + sh_ref[...]).astype(h_ref.dtype)

    return pl.pallas_call(
        body, name=name, grid=(T // ts,),
        in_specs=[_row_spec(ts, D), _full_spec((1, D)), _full_spec((1, D)), _full_spec((1, D))],
        out_specs=_row_spec(ts, D), out_shape=jax.ShapeDtypeStruct((T, D), _MXU),
        compiler_params=_params(("arbitrary",)))(x, nw, shift, scale)


def _final_loss(x, tgt, fw, name):
    T = x.shape[0]
    ts = _tile(T, 512)

    def body(x_ref, t_ref, fw_ref, dx_ref, acc_ref):
        @pl.when(pl.program_id(0) == 0)
        def _():
            acc_ref[...] = jnp.zeros_like(acc_ref)

        xv = x_ref[...]
        r = lax.rsqrt(jnp.mean(xv * xv, axis=-1, keepdims=True) + EPS)
        xn = xv * r
        e = xn * fw_ref[...] - t_ref[...]
        dy = e * (1.0 / D)
        acc_ref[0:1, :] += jnp.sum(dy * xn, axis=0, keepdims=True)
        acc_ref[1:2, :] += jnp.sum(jnp.sum(e * e, axis=0, keepdims=True), axis=1, keepdims=True)
        dxn = dy * fw_ref[...]
        dx_ref[...] = r * (dxn - xn * jnp.mean(dxn * xn, axis=-1, keepdims=True))

    return pl.pallas_call(
        body, name=name, grid=(T // ts,),
        in_specs=[_row_spec(ts, D), _row_spec(ts, D), _full_spec((1, D))],
        out_specs=[_row_spec(ts, D), _full_spec((8, D))],
        out_shape=[jax.ShapeDtypeStruct((T, D), f32), jax.ShapeDtypeStruct((8, D), f32)],
        compiler_params=_params(("arbitrary",)))(x, tgt, fw)


def _tril(n):
    return lax.broadcasted_iota(jnp.int32, (n, n), 0) >= lax.broadcasted_iota(jnp.int32, (n, n), 1)


def _gm_chunk_fwd(u, v, z, lw, lb, ws_ref, bsx):
    gu, gv = _gelu(u), _gelu(v)
    mu = jnp.mean(gv, axis=-1, keepdims=True)
    cen = gv - mu
    rstd = lax.rsqrt(jnp.mean(cen * cen, axis=-1, keepdims=True) + EPS)
    vhat = cen * rstd
    vn = _bf(vhat * lw + lb)
    tri = _tril(Q)
    mixed = jnp.concatenate(
        [_dot(_bf(jnp.where(tri, ws_ref[g], 0.0)), vn[:, g * Q:(g + 1) * Q]) for g in range(NG)], axis=1) + bsx
    return gu, vhat, rstd, vn, mixed


def _gmlp_fwd(pA, lw, lb, ws, bsx, name):
    T = pA.shape[0]
    ts = _tile(T, 512)

    def body(u_ref, v_ref, z_ref, lw_ref, lb_ref, ws_ref, bsx_ref, y_ref):
        def chunk(ci, carry):
            rows = pl.ds(pl.multiple_of(ci * Q, Q), Q)
            u, v, z = u_ref[rows, :].astype(f32), v_ref[rows, :].astype(f32), z_ref[rows, :].astype(f32)
            gu, _, _, _, mixed = _gm_chunk_fwd(u, v, z, lw_ref[...], lb_ref[...], ws_ref, bsx_ref[...])
            y_ref[rows, :] = (gu * mixed * _silu(z)).astype(y_ref.dtype)
            return carry

        lax.fori_loop(0, ts // Q, chunk, 0)

    return pl.pallas_call(
        body, name=name, grid=(T // ts,),
        in_specs=[_row_spec(ts, D, 0), _row_spec(ts, D, 1), _row_spec(ts, D, 2), _full_spec((1, D)), _full_spec((1, D)),
                  _full_spec((NG, Q, Q)), _full_spec((Q, D))],
        out_specs=_row_spec(ts, D), out_shape=jax.ShapeDtypeStruct((T, D), _MXU),
        compiler_params=_params(("arbitrary",)))(pA, pA, pA, lw, lb, ws, bsx)


def _gmlp_bwd(dya, pA, lw, lb, ws, bsx, name):
    T = pA.shape[0]
    ts = _tile(T, 512)

    def body(dy_ref, u_ref, v_ref, z_ref, lw_ref, lb_ref, ws_ref, bsx_ref, dp_ref, acc_ref, dws_ref, dbs_ref):
        @pl.when(pl.program_id(0) == 0)
        def _():
            acc_ref[...] = jnp.zeros_like(acc_ref)
            dws_ref[...] = jnp.zeros_like(dws_ref)
            dbs_ref[...] = jnp.zeros_like(dbs_ref)

        tri = _tril(Q)

        def chunk(ci, carry):
            rows = pl.ds(pl.multiple_of(ci * Q, Q), Q)
            u, v, z, dy = u_ref[rows, :].astype(f32), v_ref[rows, :].astype(f32), z_ref[rows, :].astype(f32), dy_ref[rows, :].astype(f32)
            gu, vhat, rstd, vn, mixed = _gm_chunk_fwd(u, v, z, lw_ref[...], lb_ref[...], ws_ref, bsx_ref[...])
            sz = _silu(z)
            dp_ref[rows, 0:D] = (dy * mixed * sz * _dgelu(u)).astype(dp_ref.dtype)
            dp_ref[rows, 2 * D:3 * D] = (dy * gu * mixed * _dsilu(z)).astype(dp_ref.dtype)
            dmixed = dy * gu * sz
            dbs_ref[...] += dmixed
            dmb = _bf(dmixed)
            dvn_parts = []
            for g in range(NG):
                cols = slice(g * Q, (g + 1) * Q)
                wg = _bf(jnp.where(tri, ws_ref[g], 0.0))
                dvn_parts.append(_dot_tn(wg, dmb[:, cols]))
                dws_ref[g] += jnp.where(tri, _dot_nt(dmb[:, cols], vn[:, cols]), 0.0)
            dvn = jnp.concatenate(dvn_parts, axis=1)
            acc_ref[0:1, :] += jnp.sum(dvn * vhat, axis=0, keepdims=True)
            acc_ref[1:2, :] += jnp.sum(dvn, axis=0, keepdims=True)
            dvh = dvn * lw_ref[...]
            dgv = rstd * (dvh - jnp.mean(dvh, axis=-1, keepdims=True) - vhat * jnp.mean(dvh * vhat, axis=-1, keepdims=True))
            dp_ref[rows, D:2 * D] = (dgv * _dgelu(v)).astype(dp_ref.dtype)
            return carry

        lax.fori_loop(0, ts // Q, chunk, 0)

    return pl.pallas_call(
        body, name=name, grid=(T // ts,),
        in_specs=[_row_spec(ts, D), _row_spec(ts, D, 0), _row_spec(ts, D, 1), _row_spec(ts, D, 2), _full_spec((1, D)),
                  _full_spec((1, D)), _full_spec((NG, Q, Q)), _full_spec((Q, D))],
        out_specs=[_row_spec(ts, WA), _full_spec((8, D)), _full_spec((NG, Q, Q)), _full_spec((Q, D))],
        out_shape=[jax.ShapeDtypeStruct((T, WA), _MXU), jax.ShapeDtypeStruct((8, D), f32),
                   jax.ShapeDtypeStruct((NG, Q, Q), f32), jax.ShapeDtypeStruct((Q, D), f32)],
        compiler_params=_params(("arbitrary",)))(dya, pA, pA, pA, lw, lb, ws, bsx)


def _head_maps():
    h = lax.broadcasted_iota(jnp.int32, (128, DI), 0)
    ch = lax.broadcasted_iota(jnp.int32, (128, DI), 1)
    ex = (ch // P == h).astype(_MXU)
    return ex, ex.T


def _split(v, parts):
    out = []
    for _ in range(parts - 1):
        p = _bf(v)
        out.append(p)
        v = v - p.astype(f32)
    out.append(_bf(v))
    return out


def _expand(v, ex_ref, parts):
    acc = None
    for p in _split(v, parts):
        t = _dot(p, ex_ref[...])
        acc = t if acc is None else acc + t
    return acc


def _reduce(v, rd_ref, parts=2):
    acc = None
    for p in _split(v, parts):
        t = _dot(p, rd_ref[...])
        acc = t if acc is None else acc + t
    return acc


def _ssd_time(pb_ref, dtb_ref, alog_ref):
    xdt = pb_ref[:, OFF_DT:OFF_DT + 128] + dtb_ref[...]
    dt = _softplus(xdt)
    a = -jnp.exp(alog_ref[...])
    cs = _dot_exact(_tril(Q).astype(f32), dt * a)
    return xdt, dt, a, cs


def _head_mask(r):
    return lax.broadcasted_iota(jnp.int32, (Q, GRP), 1) // P == r


def _ssd_group_fwd(g, xa_ref, s_prev, cs, cs_t, dtx_ref, csx_ref, dsk_ref):
    cols = slice(g * GRP, (g + 1) * GRP)
    xs = xa_ref[:, cols]
    bb = _bf(xa_ref[:, DI + g * NS:DI + (g + 1) * NS])
    cb = _bf(xa_ref[:, DI + NG * NS + g * NS:DI + NG * NS + (g + 1) * NS])
    gm = _dot_nt(cb, bb)
    xd = xs * dtx_ref[:, cols]
    csx = csx_ref[:, cols]
    csl = csx_ref[Q - 1:Q, cols]
    tri = _tril(Q)
    lms = [jnp.exp(jnp.where(tri, cs[:, HPG * g + r:HPG * g + r + 1] - cs_t[HPG * g + r:HPG * g + r + 1, :], NEG))
           for r in range(HPG)]
    mfs = [gm * lm for lm in lms]
    mcat = jnp.concatenate([_bf(m) for m in mfs], axis=1)
    xbd = jnp.concatenate([_bf(jnp.where(_head_mask(r), xd, 0.0)) for r in range(HPG)], axis=0)
    ydiag = _dot(mcat, xbd)
    ecs, dte, ecl = jnp.exp(csx), jnp.exp(csl - csx), jnp.exp(csl)
    yoff = ecs * _dot(cb, _bf(s_prev))
    y = ydiag + yoff + xs * dsk_ref[:, cols]
    xdd = xd * dte
    s_new = s_prev * ecl + _dot_tn(bb, _bf(xdd))
    return y, s_new, (xs, bb, cb, xd, lms, mfs, mcat, xbd, yoff, xdd, ecs, dte, ecl)


def _ssd_fwd(pB, cw, cb, dtb, alog, dsk, snw, name, carry=()):
    T = pB.shape[0]
    nc = T // Q
    ex, _ = _head_maps()

    def body(pb_ref, cw_ref, cb_ref, dtb_ref, alog_ref, dsk_ref, snw_ref, ex_ref, y_ref, xc_ref, st_ref,
             s_ref, ext_ref, xa_ref, dtx_ref, csx_ref):
        @pl.when(pl.program_id(0) == 0)
        def _():
            s_ref[...] = jnp.zeros_like(s_ref)
            ext_ref[...] = jnp.zeros_like(ext_ref)

        ext_ref[8:8 + Q, :] = pb_ref[:, OFF_XBC:OFF_XBC + CD]
        for j in range(CD // 512):
            cj = slice(j * 512, (j + 1) * 512)
            e = ext_ref[:, cj]
            xc = cb_ref[:, cj] + cw_ref[KC - 1:KC, cj] * e[8:8 + Q]
            for s in range(1, KC):
                xc = xc + cw_ref[KC - 1 - s:KC - s, cj] * pltpu.roll(e, s, 0)[8:8 + Q]
            xc_ref[:, cj] = xc
            xa_ref[:, cj] = _silu(xc)
        ext_ref[0:8, :] = ext_ref[Q:Q + 8, :]

        _, dt, _, cs = _ssd_time(pb_ref, dtb_ref, alog_ref)
        cs_t = cs.T
        dtx_ref[...] = _expand(dt, ex_ref, 2)
        csx_ref[...] = _expand(cs, ex_ref, 3)
        for g in range(NG):
            s_prev = s_ref[g]
            st_ref[0, g] = s_prev
            y, s_new, _ = _ssd_group_fwd(g, xa_ref, s_prev, cs, cs_t, dtx_ref, csx_ref, dsk_ref)
            s_ref[g] = s_new
            cols = slice(g * GRP, (g + 1) * GRP)
            yz = y * _silu(pb_ref[:, cols])
            rr = lax.rsqrt(jnp.mean(yz * yz, axis=-1, keepdims=True) + EPS)
            y_ref[:, cols] = (yz * rr * snw_ref[:, cols]).astype(y_ref.dtype)

    outs, exchanged = _carry_call(
        body, name, (nc,),
        [_row_spec(Q, WB), _full_spec((8, CD)), _full_spec((1, CD)), _full_spec((1, 128)), _full_spec((1, 128)),
         _full_spec((1, DI)), _full_spec((1, DI)), _full_spec((128, DI))],
        [_row_spec(Q, DI), _row_spec(Q, CD), pl.BlockSpec((1, NG, NS, GRP), lambda i: (i, 0, 0, 0))],
        [jax.ShapeDtypeStruct((T, DI), _MXU), jax.ShapeDtypeStruct((T, CD), f32), jax.ShapeDtypeStruct((nc, NG, NS, GRP), f32)],
        [pltpu.VMEM((NG, NS, GRP), f32), pltpu.VMEM((Q + 8, CD), f32), pltpu.VMEM((Q, CD), f32),
         pltpu.VMEM((Q, DI), f32), pltpu.VMEM((Q, DI), f32)],
        (pB, cw, cb, dtb, alog, dsk, snw, ex), carry)
    return outs + [exchanged]


def _ssd_bwd(dyb, pB, xc, states, cw, dtb, alog, dsk, snw, name, carry=()):
    T = pB.shape[0]
    nc = T // Q
    ex, rd = _head_maps()
    selr = (lax.broadcasted_iota(jnp.int32, (HPG * Q, 128), 0) // Q == lax.broadcasted_iota(jnp.int32, (HPG * Q, 128), 1)).astype(_MXU)

    def body(dy_ref, pb_ref, xc_ref, st_ref, cw_ref, dtb_ref, alog_ref, dsk_ref, snw_ref, ex_ref, rd_ref, selr_ref,
             dp_ref, dcw_ref, dcb_ref, dhd_ref, dcol_ref,
             ds_ref, xa_ref, sg_ref, dxa_ref, dxe_ref, dtx_ref, csx_ref, rcs_ref, rdt_ref, rl_ref, dcs_ref, dcst_ref):
        @pl.when(pl.program_id(0) == 0)
        def _():
            ds_ref[...] = jnp.zeros_like(ds_ref)
            dxe_ref[...] = jnp.zeros_like(dxe_ref)
            dcw_ref[...] = jnp.zeros_like(dcw_ref)
            dcb_ref[...] = jnp.zeros_like(dcb_ref)
            dhd_ref[...] = jnp.zeros_like(dhd_ref)
            dcol_ref[...] = jnp.zeros_like(dcol_ref)
            rl_ref[...] = jnp.zeros_like(rl_ref)
            dcst_ref[...] = jnp.zeros_like(dcst_ref)

        dcs_ref[...] = jnp.zeros_like(dcs_ref)
        for j in range(CD // 512):
            cj = slice(j * 512, (j + 1) * 512)
            sg_ref[:, cj] = _sigmoid(xc_ref[:, cj])
            xa_ref[:, cj] = xc_ref[:, cj] * sg_ref[:, cj]
        xdt, dt, a, cs = _ssd_time(pb_ref, dtb_ref, alog_ref)
        cs_t = cs.T
        dtx_ref[...] = _expand(dt, ex_ref, 2)
        csx_ref[...] = _expand(cs, ex_ref, 3)

        for g in range(NG):
            cols = slice(g * GRP, (g + 1) * GRP)
            s_prev = st_ref[0, g]
            sb = _bf(s_prev)
            y, _, (xs, bb, cbm, xd, lms, mfs, mcat, xbd, yoff, xdd, ecs, dte, ecl) = _ssd_group_fwd(
                g, xa_ref, s_prev, cs, cs_t, dtx_ref, csx_ref, dsk_ref)
            z = pb_ref[:, cols]
            sz = _silu(z)
            yz = y * sz
            rr = lax.rsqrt(jnp.mean(yz * yz, axis=-1, keepdims=True) + EPS)
            nrm = yz * rr
            dyb_g = dy_ref[:, cols].astype(f32)
            dcol_ref[1:2, cols] += jnp.sum(dyb_g * nrm, axis=0, keepdims=True)
            dn = dyb_g * snw_ref[:, cols]
            dyz = rr * (dn - nrm * jnp.mean(dn * nrm, axis=-1, keepdims=True))
            dyv = dyz * sz
            dp_ref[:, cols] = (dyz * y * _dsilu(z)).astype(dp_ref.dtype)
            dcol_ref[0:1, cols] += jnp.sum(dyv * xs, axis=0, keepdims=True)
            dy16 = _bf(dyv)
            dmcat = _dot_nt(dy16, xbd)
            dg = dmcat[:, 0:Q] * lms[0]
            for r in range(1, HPG):
                dg = dg + dmcat[:, r * Q:(r + 1) * Q] * lms[r]
            e16 = _bf(jnp.concatenate([dmcat[:, r * Q:(r + 1) * Q] * mfs[r] for r in range(HPG)], axis=1))
            rows = _dot(e16, selr_ref[...])
            dcs_ref[...] += rows if g == 0 else pltpu.roll(rows, HPG * g, 1)
            csum = _dot(jnp.ones((8, Q), _MXU), e16)
            for r in range(HPG):
                dcst_ref[HPG * g + r:HPG * g + r + 1, :] = csum[0:1, r * Q:(r + 1) * Q]
            big = _dot_tn(mcat, dy16)
            dxd_diag = jnp.where(_head_mask(0), big[0:Q], 0.0)
            for r in range(1, HPG):
                dxd_diag = dxd_diag + jnp.where(_head_mask(r), big[r * Q:(r + 1) * Q], 0.0)
            dsn = ds_ref[g]
            dsn16 = _bf(dsn)
            dxdd = _dot(bb, dsn16)
            t_state = dxdd * xdd
            dxd = dxd_diag + dxdd * dte
            rcs_ref[:, cols] = dyv * yoff - t_state
            rl_ref[0:1, cols] = (jnp.sum(t_state, axis=0, keepdims=True)
                                 + ecl * jnp.sum(s_prev * dsn, axis=0, keepdims=True))
            rdt_ref[:, cols] = dxd * xs
            dxa_ref[:, cols] = dyv * dsk_ref[:, cols] + dxd * dtx_ref[:, cols]
            dg16 = _bf(dg)
            dw16 = _bf(dyv * ecs)
            dxa_ref[:, DI + NG * NS + g * NS:DI + NG * NS + (g + 1) * NS] = _dot(dg16, bb) + _dot_nt(dw16, sb)
            dxa_ref[:, DI + g * NS:DI + (g + 1) * NS] = _dot_tn(dg16, cbm) + _dot_nt(_bf(xdd), dsn16)
            ds_ref[g] = dsn * ecl + _dot_tn(cbm, dw16)

        row = lax.broadcasted_iota(jnp.int32, (Q, 128), 0)
        dcs = (dcs_ref[...] - dcst_ref[...].T + _reduce(rcs_ref[...], rd_ref)
               + jnp.where(row == Q - 1, _reduce(rl_ref[...], rd_ref)[0:1, :], 0.0))
        upper = lax.broadcasted_iota(jnp.int32, (Q, Q), 0) <= lax.broadcasted_iota(jnp.int32, (Q, Q), 1)
        dadt = _dot_exact(upper.astype(f32), dcs)
        valid = lax.broadcasted_iota(jnp.int32, (Q, 128), 1) < NH
        ddt = jnp.where(valid, _reduce(rdt_ref[...], rd_ref, 1) + dadt * a, 0.0)
        ddtr = ddt * _sigmoid(xdt)
        dhd_ref[0:1, :] += jnp.sum(ddtr, axis=0, keepdims=True)
        dhd_ref[1:2, :] += jnp.sum(jnp.where(valid, dadt * dt * a, 0.0), axis=0, keepdims=True)
        dp_ref[:, OFF_DT:OFF_DT + 128] = ddtr.astype(dp_ref.dtype)
        dp_ref[:, OFF_DT + 128:WB] = jnp.zeros((Q, WB - OFF_DT - 128), dp_ref.dtype)
        for j in range(CD // 512):
            cj = slice(j * 512, (j + 1) * 512)
            sg = sg_ref[:, cj]
            dxc = dxa_ref[:, cj] * (sg + xa_ref[:, cj] * (1.0 - sg))
            dxe_ref[0:Q, cj] = dxc
            dcb_ref[0:1, cj] += jnp.sum(dxc, axis=0, keepdims=True)
            raw = pb_ref[:, OFF_XBC + j * 512:OFF_XBC + (j + 1) * 512]
            e = dxe_ref[:, cj]
            dcw_ref[KC - 1:KC, cj] += jnp.sum(dxc * raw, axis=0, keepdims=True)
            dxb = cw_ref[KC - 1:KC, cj] * dxc
            for s in range(1, KC):
                sh = pltpu.roll(e, Q + 8 - s, 0)[0:Q]
                dcw_ref[KC - 1 - s:KC - s, cj] += jnp.sum(sh * raw, axis=0, keepdims=True)
                dxb = dxb + cw_ref[KC - 1 - s:KC - s, cj] * sh
            dp_ref[:, OFF_XBC + j * 512:OFF_XBC + (j + 1) * 512] = dxb.astype(dp_ref.dtype)
        dxe_ref[Q:Q + 8, :] = dxe_ref[0:8, :]

    rev = lambda i: (nc - 1 - i, 0)
    outs, exchanged = _carry_call(
        body, name, (nc,),
        [pl.BlockSpec((Q, DI), rev), pl.BlockSpec((Q, WB), rev), pl.BlockSpec((Q, CD), rev),
         pl.BlockSpec((1, NG, NS, GRP), lambda i: (nc - 1 - i, 0, 0, 0)),
         _full_spec((8, CD)), _full_spec((1, 128)), _full_spec((1, 128)),
         _full_spec((1, DI)), _full_spec((1, DI)), _full_spec((128, DI)), _full_spec((DI, 128)), _full_spec((HPG * Q, 128))],
        [pl.BlockSpec((Q, WB), rev), _full_spec((8, CD)), _full_spec((8, CD)), _full_spec((8, 128)), _full_spec((8, DI))],
        [jax.ShapeDtypeStruct((T, WB), _MXU), jax.ShapeDtypeStruct((8, CD), f32), jax.ShapeDtypeStruct((8, CD), f32),
         jax.ShapeDtypeStruct((8, 128), f32), jax.ShapeDtypeStruct((8, DI), f32)],
        [pltpu.VMEM((NG, NS, GRP), f32), pltpu.VMEM((Q, CD), f32), pltpu.VMEM((Q, CD), f32), pltpu.VMEM((Q, CD), f32),
         pltpu.VMEM((Q + 8, CD), f32), pltpu.VMEM((Q, DI), f32), pltpu.VMEM((Q, DI), f32),
         pltpu.VMEM((Q, DI), f32), pltpu.VMEM((Q, DI), f32), pltpu.VMEM((8, DI), f32),
         pltpu.VMEM((Q, 128), f32), pltpu.VMEM((128, Q), f32)],
        (dyb, pB, xc, states, cw, dtb, alog, dsk, snw, ex, rd, selr), carry)
    return outs + [exchanged]


def _merge_fwd(ya, yb, pG, x, gate, wa, wb, wo, name, carry=()):
    T = x.shape[0]
    ts = _tile(T, 256)

    def body(ya_ref, yb_ref, g_ref, x_ref, gate_ref, wa_ref, wb_ref, wo_ref, xo_ref, mg_ref, pa_ref, pb_ref):
        pa = _dot(ya_ref[...], wa_ref[...])
        pb = _dot(yb_ref[...], wb_ref[...])
        merged = _sigmoid(g_ref[:, 0:D].astype(f32)) * pa + _sigmoid(g_ref[:, D:2 * D].astype(f32)) * pb
        mg = _bf(merged)
        xo_ref[...] = x_ref[...] + gate_ref[...] * _dot(mg, wo_ref[...])
        mg_ref[...] = mg
        pa_ref[...] = pa.astype(pa_ref.dtype)
        pb_ref[...] = pb.astype(pb_ref.dtype)

    outs, exchanged = _carry_call(
        body, name, (T // ts,),
        [_row_spec(ts, D), _row_spec(ts, DI), _row_spec(ts, WG), _row_spec(ts, D), _full_spec((1, D)),
         _full_spec((D, D)), _full_spec((DI, D)), _full_spec((D, D))],
        [_row_spec(ts, D)] * 4,
        [jax.ShapeDtypeStruct((T, D), f32), jax.ShapeDtypeStruct((T, D), _MXU), jax.ShapeDtypeStruct((T, D), _MXU),
         jax.ShapeDtypeStruct((T, D), _MXU)],
        [], (ya, yb, pG, x, gate, wa, wb, wo), carry)
    return outs + [exchanged]


def _merge_bwd(dxout, merged, pa, pb, pG, gate, wo, name):
    T = dxout.shape[0]
    ts = _tile(T, 256)

    def body(dx_ref, mg_ref, pa_ref, pb_ref, g_ref, gate_ref, wo_ref, do_ref, dpa_ref, dpb_ref, dg_ref, acc_ref):
        @pl.when(pl.program_id(0) == 0)
        def _():
            acc_ref[...] = jnp.zeros_like(acc_ref)

        dxo = dx_ref[...]
        acc_ref[0:1, :] += jnp.sum(dxo * _dot(mg_ref[...], wo_ref[...]), axis=0, keepdims=True)
        do = _bf(dxo * gate_ref[...])
        do_ref[...] = do
        dmerged = _dot_nt(do, wo_ref[...])
        sa, sb = _sigmoid(g_ref[:, 0:D].astype(f32)), _sigmoid(g_ref[:, D:2 * D].astype(f32))
        dpa_ref[...] = (dmerged * sa).astype(dpa_ref.dtype)
        dpb_ref[...] = (dmerged * sb).astype(dpb_ref.dtype)
        dg_ref[:, 0:D] = (dmerged * pa_ref[...].astype(f32) * sa * (1.0 - sa)).astype(dg_ref.dtype)
        dg_ref[:, D:2 * D] = (dmerged * pb_ref[...].astype(f32) * sb * (1.0 - sb)).astype(dg_ref.dtype)

    return pl.pallas_call(
        body, name=name, grid=(T // ts,),
        in_specs=[_row_spec(ts, D), _row_spec(ts, D), _row_spec(ts, D), _row_spec(ts, D), _row_spec(ts, WG),
                  _full_spec((1, D)), _full_spec((D, D))],
        out_specs=[_row_spec(ts, D), _row_spec(ts, D), _row_spec(ts, D), _row_spec(ts, WG), _full_spec((8, D))],
        out_shape=[jax.ShapeDtypeStruct((T, D), _MXU), jax.ShapeDtypeStruct((T, D), _MXU), jax.ShapeDtypeStruct((T, D), _MXU),
                   jax.ShapeDtypeStruct((T, WG), _MXU), jax.ShapeDtypeStruct((8, D), f32)],
        compiler_params=_params(("arbitrary",)))(dxout, merged, pa, pb, pG, gate, wo)


def _rows_split(r):
    ra, rb = D // NDEV, DI // NDEV
    return r[:, 0:ra].reshape(D, D), r[:, ra:ra + rb].reshape(DI, D), r[:, ra + rb:].reshape(D, D)


def _layer_fwd(x, mod, lp, tag, carry=None):
    carry = carry or {}
    got = {}

    def mm(key, w, **kw):
        if carry.get(key):
            out, got[key] = _mm(h, w, "nn", f"{key}_{tag}", carry=carry[key], **kw)
            return out
        return _mm(h, w, "nn", f"{key}_{tag}", **kw)

    h = _modulate(x, lp["nw"], mod[0:1], mod[1:2], f"modulate_{tag}")
    pA = mm("proj_gm", lp["w_gm"], out_dtype=_MXU)
    pB = mm("proj_ssd", lp["w_ssd"])
    pG = mm("proj_gate", lp["w_g"], out_dtype=_MXU)
    if "wa" not in lp:
        lp = dict(lp)
        lp["wa"], lp["wb"], lp["wo"] = _rows_split(jnp.concatenate([got["proj_gm"][0], got["proj_gate"][0]], axis=1))
    ya = _gmlp_fwd(pA, lp["lw"], lp["lb"], lp["ws"], lp["bsx"], f"gmlp_fwd_{tag}")
    yb, xc, states, got["ssd_fwd"] = _ssd_fwd(pB, lp["cw"], lp["cb"], lp["dtb"], lp["alog"], lp["dsk"], lp["snw"], f"ssd_fwd_{tag}",
                                              carry.get("ssd_fwd", ()))
    xo, merged, pa, pb, _ = _merge_fwd(ya, yb, pG, x, mod[2:3], lp["wa"], lp["wb"], lp["wo"], f"merge_fwd_{tag}")
    return xo, dict(x=x, h=h, pA=pA, pB=pB, pG=pG, ya=ya, yb=yb, xc=xc, states=states, merged=merged, pa=pa, pb=pb), got, lp


def _dh_modulate_bwd(dpG, dpA, dpB, w_g, w_gm, w_ssd, dxout, x, nw, scale, name, carry=()):
    T = x.shape[0]
    tm = _tile(T, 1024)
    fam = [(WG, _tile(WG, 1024)), (WA, _tile(WA, 1024)), (WB, _tile(WB, 1280))]
    steps = [w // k for w, k in fam]
    first = [0, steps[0], steps[0] + steps[1]]
    ns = sum(steps)

    def chunk(f):
        return lambda i, s: jnp.clip(s - first[f], 0, steps[f] - 1)

    a_specs = [pl.BlockSpec((tm, fam[f][1]), lambda i, s, c=chunk(f): (i, c(i, s))) for f in range(3)]
    b_specs = [pl.BlockSpec((D, fam[f][1]), lambda i, s, c=chunk(f): (0, c(i, s))) for f in range(3)]
    tok = pl.BlockSpec((tm, D), lambda i, s: (i, 0))
    tok_once = pl.BlockSpec((tm, D), lambda i, s: (i, 0), pipeline_mode=pl.Buffered(1))
    vec = pl.BlockSpec((1, D), lambda i, s: (0, 0))

    def body(ag, aa, ab, bg, ba, bb, dxo_ref, x_ref, nw_ref, sc_ref, dx_ref, sum_ref, acc_ref):
        i, s = pl.program_id(0), pl.program_id(1)

        @pl.when(jnp.logical_and(i == 0, s == 0))
        def _():
            sum_ref[...] = jnp.zeros_like(sum_ref)

        @pl.when(s == 0)
        def _():
            acc_ref[...] = jnp.zeros_like(acc_ref)

        @pl.when(s < first[1])
        def _():
            acc_ref[...] += _dot_nt(ag[...], bg[...])

        @pl.when(jnp.logical_and(s >= first[1], s < first[2]))
        def _():
            acc_ref[...] += _dot_nt(aa[...], ba[...])

        @pl.when(s >= first[2])
        def _():
            acc_ref[...] += _dot_nt(ab[...], bb[...])

        @pl.when(s == ns - 1)
        def _():
            xv, dh_v = x_ref[...], acc_ref[...]
            r = lax.rsqrt(jnp.mean(xv * xv, axis=-1, keepdims=True) + EPS)
            xn = xv * r
            hn = xn * nw_ref[...]
            dhn = dh_v * (1.0 + sc_ref[...])
            sum_ref[0:1, :] += jnp.sum(dh_v, axis=0, keepdims=True)
            sum_ref[1:2, :] += jnp.sum(dh_v * hn, axis=0, keepdims=True)
            sum_ref[2:3, :] += jnp.sum(dhn * xn, axis=0, keepdims=True)
            dxn = dhn * nw_ref[...]
            dx_ref[...] = dxo_ref[...] + r * (dxn - xn * jnp.mean(dxn * xn, axis=-1, keepdims=True))

    outs, exchanged = _carry_call(
        body, name, (T // tm, ns), a_specs + b_specs + [tok_once, tok_once, vec, vec], [tok, pl.BlockSpec((8, D), lambda i, s: (0, 0))],
        [jax.ShapeDtypeStruct((T, D), f32), jax.ShapeDtypeStruct((8, D), f32)], [pltpu.VMEM((tm, D), f32)],
        (dpG, dpA, dpB, w_g, w_gm, w_ssd, dxout, x, nw, scale), carry)
    return outs + [exchanged]


def _win_blocks(g):
    full = jnp.concatenate([g["w_gm"], g["w_ssd"][:, 0:DI + CD], g["w_ssd"][:, DI + CD:DI + CD + NH], g["w_g"]], axis=1)
    return jnp.transpose(full.reshape(D, NDEV, NIN // NDEV), (1, 0, 2))


def _row_blocks(g):
    return jnp.concatenate([g["wa"].reshape(NDEV, D // NDEV, D), g["wb"].reshape(NDEV, DI // NDEV, D),
                            g["wo"].reshape(NDEV, D // NDEV, D)], axis=1)


def _layer_bwd(dxo, sv, mod, lp, tag, carry=(), scatter_own=None):
    do, dpa, dpb, dpG, s_gate = _merge_bwd(dxo, sv["merged"], sv["pa"], sv["pb"], sv["pG"], mod[2:3], lp["wo"], f"merge_bwd_{tag}")
    g = {}
    g["wo"] = _mm(sv["merged"], do, "tn", f"dw_out_{tag}", out_dtype=_WIRE)
    g["wa"] = _mm(sv["ya"], dpa, "tn", f"dw_proj_a_{tag}", out_dtype=_WIRE)
    g["wb"] = _mm(sv["yb"], dpb, "tn", f"dw_proj_b_{tag}", out_dtype=_WIRE)
    dya = _mm(dpa, lp["wa"], "nt", f"dy_a_{tag}", out_dtype=_MXU)
    dyb = _mm(dpb, lp["wb"], "nt", f"dy_b_{tag}", out_dtype=_MXU)
    dpA, s_ln, g["ws"], dbsx = _gmlp_bwd(dya, sv["pA"], lp["lw"], lp["lb"], lp["ws"], lp["bsx"], f"gmlp_bwd_{tag}")
    carry = list(carry) + ([(_row_blocks(g), False)] if scatter_own else [])
    dpB, g["cw"], s_cb, s_hd, s_col, got = _ssd_bwd(dyb, sv["pB"], sv["xc"], sv["states"], lp["cw"], lp["dtb"], lp["alog"],
                                                    lp["dsk"], lp["snw"], f"ssd_bwd_{tag}", carry)
    g["lw"], g["lb"] = s_ln[0], s_ln[1]
    g["bs"] = dbsx.reshape(Q, NG, Q).sum(-1).T
    g["cb"] = s_cb[0]
    g["dtb"], g["alog"] = s_hd[0, :NH], s_hd[1, :NH]
    g["dsk"] = s_col[0].reshape(NH, P).sum(-1)
    g["snw"] = s_col[1]
    g["cw"] = g["cw"][0:KC]
    g["w_g"] = _mm(sv["h"], dpG, "tn", f"dw_gate_{tag}", out_dtype=_WIRE)
    g["w_gm"] = _mm(sv["h"], dpA, "tn", f"dw_gm_{tag}", out_dtype=_WIRE)
    g["w_ssd"] = _mm(sv["h"], dpB, "tn", f"dw_ssd_{tag}", out_dtype=_WIRE)
    dx, s_mod, got_dh = _dh_modulate_bwd(dpG, dpA, dpB, lp["w_g"], lp["w_gm"], lp["w_ssd"], dxo, sv["x"], lp["nw"], mod[1:2],
                                         f"dh_{tag}", [(_win_blocks(g), False), (scatter_own(g), True)] if scatter_own else ())
    if scatter_own:
        got, got_own = got[:-1], [got[-1], got_dh[0], got_dh[1]]
    else:
        got_own = []
    g["mod"] = jnp.concatenate([s_mod[0], s_mod[1], s_gate[0]])
    g["nw"] = s_mod[2]
    return dx, g, got, got_own


def _prep_layer(nw, w_in_full, lw, lb, ws, bs, cw_full, cb, dtb, alog, dsk, snw, rows=None):
    z = jnp.zeros((D, WB - (DI + CD + NH)), w_in_full.dtype)
    pad_h = lambda v: jnp.pad(v, (0, 128 - NH)).reshape(1, 128)
    extra = dict(zip(("wa", "wb", "wo"), _rows_split(rows))) if rows is not None else {}
    return dict(
        **extra,
        nw=nw.reshape(1, D),
        w_gm=w_in_full[:, 0:WA],
        w_ssd=jnp.concatenate([w_in_full[:, WA:WA + DI + CD], w_in_full[:, WA + DI + CD:WA + DI + CD + NH], z], axis=1),
        w_g=w_in_full[:, WA + DI + CD + NH:NIN],
        lw=lw.reshape(1, D), lb=lb.reshape(1, D), ws=ws, bsx=jnp.repeat(bs.T, Q, axis=1),
        cw=jnp.pad(cw_full, ((0, 8 - KC), (0, 0))), cb=cb.reshape(1, CD), dtb=pad_h(dtb), alog=pad_h(alog),
        dsk=jnp.repeat(dsk, P).reshape(1, DI), snw=snw.reshape(1, DI))


def _exchange(src, gather, name):
    if not gather:
        assert src.shape[0] == NDEV

    def body(src_ref, out_ref, send_sems, recv_sems, local_sem):
        start, finish = _exch_ops(src_ref, out_ref, send_sems, recv_sems, local_sem, gather)
        start()
        finish()

    return pl.pallas_call(
        body, name=name, out_shape=_exch_shape(src, gather),
        in_specs=[pl.BlockSpec(memory_space=pl.ANY)], out_specs=pl.BlockSpec(memory_space=pl.ANY), scratch_shapes=_EXCH_SEMS)(src)


def _gather2(src, name):
    def body(src_ref, out_ref, send_sems, recv_sems, local_sem):
        x, y, c = lax.axis_index("x"), lax.axis_index("y"), lax.axis_index("c")
        me, sibling = (x, y, c), (x, y, 1 - c)
        chips = [(1 - x, y), (x, 1 - y), (1 - x, 1 - y)]

        def slot(px, py, pc):
            return out_ref.at[4 * px + 2 * py + pc]

        def copy(k, block, to, src=None):
            return pltpu.make_async_remote_copy(
                src_ref=slot(*block) if src is None else src, dst_ref=slot(*block), send_sem=send_sems.at[k],
                recv_sem=recv_sems.at[k], device_id=to, device_id_type=pl.DeviceIdType.MESH)

        mine = pltpu.make_async_copy(src_ref, slot(*me), local_sem)
        mine.start()
        first = [copy(0, me, sibling, src=src_ref)] + [copy(1 + j, me, (*chip, c), src=src_ref) for j, chip in enumerate(chips)]
        for cp in first:
            cp.start()
        passed = [copy(4 + j, (*chip, c), sibling) for j, chip in enumerate(chips)]
        for j, chip in enumerate(chips):
            copy(1 + j, (*chip, c), me).wait_recv()
            passed[j].start()
        copy(0, sibling, me).wait_recv()
        for j, chip in enumerate(chips):
            copy(4 + j, (*chip, 1 - c), me).wait_recv()
        for cp in first + passed:
            cp.wait_send()
        mine.wait()

    return pl.pallas_call(
        body, name=name, out_shape=_exch_shape(src, True),
        in_specs=[pl.BlockSpec(memory_space=pl.ANY)], out_specs=pl.BlockSpec(memory_space=pl.ANY), scratch_shapes=_EXCH_SEMS)(src)


def _mod_dist(c8, ada_w, ada_b_cols, name):
    L, _, AW = ada_w.shape

    def body(c_ref, w_ref, b_ref, parts_ref, sc_ref, call_ref, mine_ref, send1, recv1, send2, recv2):
        x, y, c = lax.axis_index("x"), lax.axis_index("y"), lax.axis_index("c")
        me = 4 * x + 2 * y + c

        def peer(j):
            px = 1 - x if (j >> 2) & 1 else x
            py = 1 - y if (j >> 1) & 1 else y
            pc = 1 - c if j & 1 else c
            return (px, py, pc), 4 * px + 2 * py + pc

        def copy(j, src, dst, sems, landing):
            dev, idx = peer(j)
            return pltpu.make_async_remote_copy(
                src_ref=src, dst_ref=dst.at[idx] if landing else dst.at[me], send_sem=sems[0].at[j - 1], recv_sem=sems[1].at[j - 1],
                device_id=dev, device_id_type=pl.DeviceIdType.MESH)

        def all_to_all(src, dst, sems):
            for j in range(1, NDEV):
                copy(j, src, dst, sems, False).start()
            for j in range(1, NDEV):
                copy(j, src, dst, sems, True).wait_recv()
            for j in range(1, NDEV):
                copy(j, src, dst, sems, False).wait_send()

        call_ref[me] = c_ref[...]
        all_to_all(c_ref, call_ref, (send1, recv1))
        row = lax.broadcasted_iota(jnp.int32, (8, D), 0)
        cm = jnp.zeros((8, D), f32)
        for k in range(NDEV):
            cm = jnp.where(row == k, call_ref[k], cm)
        sc = _silu(cm)
        sc_ref[...] = sc
        for l in range(L):
            mine_ref[l] = _dot(_bf(sc), w_ref[l]) + b_ref[l]
        parts_ref[me] = mine_ref[...]
        all_to_all(mine_ref, parts_ref, (send2, recv2))

    vmem = pl.BlockSpec(memory_space=pltpu.VMEM)
    sems = pltpu.SemaphoreType.DMA((NDEV - 1,))
    return pl.pallas_call(
        body, name=name, in_specs=[vmem, vmem, vmem], out_specs=[vmem, vmem],
        out_shape=[jax.ShapeDtypeStruct((NDEV, L, 8, AW), f32), jax.ShapeDtypeStruct((8, D), f32)],
        scratch_shapes=[pltpu.VMEM((NDEV, 8, D), f32), pltpu.VMEM((L, 8, AW), f32), sems, sems, sems, sems])(c8, ada_w, ada_b_cols)


def _ada_w_grad(sc_all, dmod_cols, name):
    W = dmod_cols.shape[1]

    def body(s_ref, d_ref, o_ref):
        o_ref[...] = lax.dot_general(s_ref[...], d_ref[...], (((0,), (0,)), ((), ())), preferred_element_type=f32,
                                     precision=lax.Precision.HIGHEST)

    return pl.pallas_call(body, name=name, out_shape=jax.ShapeDtypeStruct((D, W), f32))(sc_all, dmod_cols)


def _adamw_math(w, g, m, v):
    m = ADAM_B1 * m + (1.0 - ADAM_B1) * g
    v = ADAM_B2 * v + (1.0 - ADAM_B2) * (g * g)
    m_hat = m / (1.0 - ADAM_B1 ** ADAM_STEP)
    v_hat = v / (1.0 - ADAM_B2 ** ADAM_STEP)
    delta = -ADAM_LR * (m_hat / (jnp.sqrt(v_hat) + ADAM_EPS) + ADAM_WD * w)
    return delta, m, v


def _sum_adamw(recvs, w, m, v, name):
    nl = len(recvs)
    n, R, C = recvs[0].shape
    tr = _tile(R, 128 if C > 1024 else 256)
    nr = R // tr
    stacked = w.ndim == 3
    at = (lambda ref: ref.at[0]) if stacked else (lambda ref: ref)

    def body(*refs):
        r_refs, (w_ref, m_ref, v_ref, g_ref, d_ref, nm_ref, nv_ref) = refs[:nl], [at(r) for r in refs[nl:]]
        for li in range(nl):
            @pl.when(pl.program_id(0) == li)
            def _(r_ref=r_refs[li]):
                g = r_ref[0].astype(f32)
                for k in range(1, n):
                    g = g + r_ref[k].astype(f32)
                g_ref[...] = g
                d_ref[...], nm_ref[...], nv_ref[...] = _adamw_math(w_ref[...], g, m_ref[...], v_ref[...])

    if stacked:
        spec = pl.BlockSpec((1, tr, C), lambda l, i: (l, i, 0))
    else:
        spec = pl.BlockSpec((tr, C), lambda l, i: (l * nr + i, 0))
    r_specs = [pl.BlockSpec((n, tr, C), lambda l, i, li=li: (0, jnp.clip(i + (l - li) * nr, 0, nr - 1), 0)) for li in range(nl)]
    return pl.pallas_call(
        body, name=name, grid=(nl, nr), in_specs=r_specs + [spec, spec, spec], out_specs=[spec] * 4,
        out_shape=[jax.ShapeDtypeStruct(w.shape, f32)] * 4, compiler_params=_params(("arbitrary", "arbitrary")))(*recvs, w, m, v)


def _pack(arrays, rows):
    flat = []
    for a in arrays:
        a = a.reshape(-1).astype(f32)
        flat.append(jnp.pad(a, (0, (-a.shape[0]) % 128)))
    flat = jnp.concatenate(flat)
    return jnp.pad(flat, (0, rows * 128 - flat.shape[0])).reshape(rows, 128)


def _unpack(slab, shapes):
    flat = slab.reshape(-1)
    out, off = [], 0
    for s in shapes:
        n = 1
        for d in s:
            n *= d
        out.append(flat[off:off + n].reshape(s))
        off += n + (-n) % 128
    return out


def kernel(x, c, ada_w, ada_b, norm_w, w_in, gm_ln_w, gm_ln_b, gm_ws, gm_bs, conv_w, conv_b, dt_bias, a_log, d_skip, ssm_norm_w, w_proj_a, w_proj_b, w_out, final_norm_w, loss_target, m_ada_w, m_ada_b, m_norm_w, m_w_in, m_gm_ln_w, m_gm_ln_b, m_gm_ws, m_gm_bs, m_conv_w, m_conv_b, m_dt_bias, m_a_log, m_d_skip, m_ssm_norm_w, m_w_proj_a, m_w_proj_b, m_w_out, m_final_norm_w, v_ada_w, v_ada_b, v_norm_w, v_w_in, v_gm_ln_w, v_gm_ln_b, v_gm_ws, v_gm_bs, v_conv_w, v_conv_b, v_dt_bias, v_a_log, v_d_skip, v_ssm_norm_w, v_w_proj_a, v_w_proj_b, v_w_out, v_final_norm_w):
    L = 2
    me = 4 * lax.axis_index("x") + 2 * lax.axis_index("y") + lax.axis_index("c")
    W = dict(ada_w=ada_w, ada_b=ada_b, norm_w=norm_w, w_in=w_in, gm_ln_w=gm_ln_w, gm_ln_b=gm_ln_b, gm_ws=gm_ws, gm_bs=gm_bs,
             conv_w=conv_w, conv_b=conv_b, dt_bias=dt_bias, a_log=a_log, d_skip=d_skip, ssm_norm_w=ssm_norm_w, w_proj_a=w_proj_a,
             w_proj_b=w_proj_b, w_out=w_out, final_norm_w=final_norm_w)
    M = dict(ada_w=m_ada_w, ada_b=m_ada_b, norm_w=m_norm_w, w_in=m_w_in, gm_ln_w=m_gm_ln_w, gm_ln_b=m_gm_ln_b, gm_ws=m_gm_ws,
             gm_bs=m_gm_bs, conv_w=m_conv_w, conv_b=m_conv_b, dt_bias=m_dt_bias, a_log=m_a_log, d_skip=m_d_skip,
             ssm_norm_w=m_ssm_norm_w, w_proj_a=m_w_proj_a, w_proj_b=m_w_proj_b, w_out=m_w_out, final_norm_w=m_final_norm_w)
    V = dict(ada_w=v_ada_w, ada_b=v_ada_b, norm_w=v_norm_w, w_in=v_w_in, gm_ln_w=v_gm_ln_w, gm_ln_b=v_gm_ln_b, gm_ws=v_gm_ws,
             gm_bs=v_gm_bs, conv_w=v_conv_w, conv_b=v_conv_b, dt_bias=v_dt_bias, a_log=v_a_log, d_skip=v_d_skip,
             ssm_norm_w=v_ssm_norm_w, w_proj_a=v_w_proj_a, w_proj_b=v_w_proj_b, w_out=v_w_out, final_norm_w=v_final_norm_w)
    SW = NIN // NDEV
    AW = 3 * D // NDEV
    CW = CD // NDEV
    RA, RB = D // NDEV, DI // NDEV

    wire = lambda a: a.astype(_WIRE)
    rows_of = lambda d, l: jnp.concatenate([d["w_proj_a"][l], d["w_proj_b"][l], d["w_out"][l]], axis=0)
    g_win0 = _gather2(wire(w_in[0]), "gather_w_in_l0")
    g_cw = _exchange(conv_w.reshape(L * KC, CW), True, "gather_conv_w").reshape(NDEV, L, KC, CW)
    parts, sc_all = _mod_dist(jnp.broadcast_to(c, (8, D)), wire(ada_w),
                              lax.dynamic_slice_in_dim(ada_b, me * AW, AW, axis=1).reshape(L, 1, AW), "mod_dist")
    mods = [lax.dynamic_index_in_dim(parts[:, l], me, axis=1, keepdims=False).reshape(3, D) for l in range(L)]
    cols = lambda g: jnp.transpose(g, (1, 0, 2)).reshape(g.shape[1], -1)

    def prep(l, g_win, g_rows):
        return _prep_layer(norm_w[l], cols(g_win), gm_ln_w[l], gm_ln_b[l], gm_ws[l], gm_bs[l], cols(g_cw[:, l]), conv_b[l],
                           dt_bias[l], a_log[l], d_skip[l], ssm_norm_w[l], g_rows)

    half = D // 2
    rows0, rows1 = wire(rows_of(W, 0)), wire(rows_of(W, 1))
    hr = rows0.shape[0] // 2
    h, sv0, got, lp0 = _layer_fwd(x[0], mods[0], prep(0, g_win0, None), "l0",
                                  dict(proj_gm=[(rows0[:hr], True)], proj_gate=[(rows0[hr:], True)],
                                       proj_ssd=[(wire(w_in[1][:half]), True)],
                                       ssd_fwd=[(wire(w_in[1][half:]), True), (rows1, True)]))
    lp1 = prep(1, jnp.concatenate([got["proj_ssd"][0], got["ssd_fwd"][0]], axis=1), got["ssd_fwd"][1])
    h, sv1, _, _ = _layer_fwd(h, mods[1], lp1, "l1")
    dh, s_fin = _final_loss(h, loss_target[0], final_norm_w.reshape(1, D), "final_loss")
    loss = lax.psum(0.5 / D * s_fin[1, 0], ("x", "y", "c"))

    small_l = ["gm_ln_w", "gm_ln_b", "gm_ws", "gm_bs", "conv_b", "dt_bias", "a_log", "d_skip", "ssm_norm_w"]
    small_key = dict(gm_ln_w="lw", gm_ln_b="lb", gm_ws="ws", gm_bs="bs", conv_b="cb", dt_bias="dtb", a_log="alog", d_skip="dsk",
                     ssm_norm_w="snw")
    shapes_l = [W[n].shape[1:] for n in small_l] + [(KC, CD)]
    shapes_t = [(L, 3 * D), (L, D), (D,)]
    rows_for = lambda sh: -(-sum(-(-functools.reduce(lambda a, b: a * b, s, 1) // 128) for s in sh) // 256) * 256
    n_rows_l, n_rows_t = rows_for(shapes_l), rows_for(shapes_t)
    slab_of = lambda g: _pack([g[small_key[n]] for n in small_l] + [g["cw"]], n_rows_l)

    dh, g1, _, _ = _layer_bwd(dh, sv1, mods[1], lp1, "l1")
    dh, g0, got, got_own = _layer_bwd(dh, sv0, mods[0], lp0, "l0",
                                      [(_win_blocks(g1), False), (_row_blocks(g1), False), (slab_of(g1), True)], slab_of)
    grads = [g0, g1]
    grad_x = dh[None]
    recv_win = [got_own[1], got[0]]
    recv_rows = [got_own[0], got[1]]
    slab_all = [got_own[2], got[2]]
    st = lambda key: jnp.stack([g[key] for g in grads])
    slab_t = _exchange(_pack([st("mod"), st("nw"), s_fin[0]], n_rows_t), True, "gather_small_tail")

    o_win = _sum_adamw(recv_win, w_in, m_w_in, v_w_in, "adamw_w_in")
    rows_all = lambda d: jnp.concatenate([rows_of(d, l) for l in range(L)], axis=0)
    o_rows = [o.reshape(L, 2 * RA + RB, D) for o in _sum_adamw(recv_rows, rows_all(W), rows_all(M), rows_all(V), "adamw_w_rows")]
    o_pa = [o[:, 0:RA] for o in o_rows]
    o_pb = [o[:, RA:RA + RB] for o in o_rows]
    o_po = [o[:, RA + RB:] for o in o_rows]

    def slab_params(d):
        return jnp.concatenate([_pack([d[n][l] for n in small_l] + [jnp.zeros((KC, CD), f32)], n_rows_l) for l in range(L)], axis=0)

    o_small = [[_unpack(o[l * n_rows_l:(l + 1) * n_rows_l], shapes_l) for l in range(L)]
               for o in _sum_adamw(slab_all, slab_params(W), slab_params(M), slab_params(V), "adamw_small")]
    small_out = {n: [jnp.stack([o_small[k][l][i] for l in range(L)]) for k in range(4)] for i, n in enumerate(small_l)}
    tail_params = lambda d: _pack([d["ada_b"], d["norm_w"], d["final_norm_w"]], n_rows_t)
    o_tail = [_unpack(o, shapes_t) for o in _sum_adamw([slab_t], tail_params(W), tail_params(M), tail_params(V), "adamw_small_tail")]
    for i, n in enumerate(["ada_b", "norm_w", "final_norm_w"]):
        small_out[n] = [o_tail[k][i] for k in range(4)]

    g_cw_mine = jnp.stack([lax.dynamic_slice_in_dim(o_small[0][l][len(small_l)], me * CW, CW, axis=1) for l in range(L)])
    o_cw = _sum_adamw([g_cw_mine.reshape(1, L * KC, CW)], conv_w.reshape(L * KC, CW), m_conv_w.reshape(L * KC, CW),
                      v_conv_w.reshape(L * KC, CW), "adamw_conv_w")
    o_cw = [o.reshape(L, KC, CW) for o in o_cw]

    dmod_all = jnp.stack([_unpack(slab_t[k], shapes_t)[0] for k in range(NDEV)])
    dmod_cols = lax.dynamic_slice_in_dim(dmod_all, me * AW, AW, axis=2).reshape(NDEV, L * AW)
    g_ada_w = jnp.transpose(_ada_w_grad(sc_all, dmod_cols, "ada_w_grad").reshape(D, L, AW), (1, 0, 2))
    o_ada = _sum_adamw([g_ada_w[l][None] for l in range(L)], ada_w, m_ada_w, v_ada_w, "adamw_ada_w")

    big = dict(ada_w=o_ada, w_in=o_win, conv_w=o_cw, w_proj_a=o_pa, w_proj_b=o_pb, w_out=o_po)
    order = ["ada_w", "ada_b", "norm_w", "w_in", "gm_ln_w", "gm_ln_b", "gm_ws", "gm_bs", "conv_w", "conv_b", "dt_bias", "a_log",
             "d_skip", "ssm_norm_w", "w_proj_a", "w_proj_b", "w_out", "final_norm_w"]
    pick = lambda n, k: big[n][k] if n in big else small_out[n][k]
    return (loss, grad_x, *[pick(n, 0) for n in order], *[pick(n, 1) for n in order], *[pick(n, 2) for n in order],
            *[pick(n, 3) for n in order])
```

```python
import functools

import jax
import jax.numpy as jnp
from jax import lax
from jax.experimental import pallas as pl
from jax.experimental.pallas import tpu as pltpu

f32 = jnp.float32
_MXU = jnp.bfloat16
_WIRE = jnp.bfloat16

D = 1024
Q = 128
NG = 8
DI = 2048
NH = 32
P = 64
HPG = 4
NS = 128
KC = 4
CD = 4096
GRP = DI // NG
EPS = 1e-6
NDEV = 8
WA = 3 * D
WB = DI + CD + 256
WG = 2 * D
OFF_XBC = DI
OFF_DT = DI + CD
NIN = 11296
VMEM_LIMIT = 56 * 1024 * 1024
NEG = -1e30

ADAM_LR, ADAM_B1, ADAM_B2, ADAM_EPS, ADAM_WD, ADAM_STEP = 0.001, 0.9, 0.999, 1e-08, 0.01, 10


def _bf(x):
    return x.astype(_MXU)


def _dot(a, b):
    return jnp.dot(a, b, preferred_element_type=f32)


def _dot_nt(a, b):
    return lax.dot_general(a, b, (((1,), (1,)), ((), ())), preferred_element_type=f32)


def _dot_tn(a, b):
    return lax.dot_general(a, b, (((0,), (0,)), ((), ())), preferred_element_type=f32)


def _dot_exact(a, b):
    return jnp.dot(a, b, preferred_element_type=f32, precision=lax.Precision.HIGHEST)


def _sigmoid(x):
    return jax.nn.sigmoid(x)


def _silu(x):
    return x * _sigmoid(x)


def _dsilu(x):
    s = _sigmoid(x)
    return s * (1.0 + x * (1.0 - s))


_GK = 0.7978845608028654
_GC = 0.044715


def _gelu(x):
    return 0.5 * x * (1.0 + jnp.tanh(_GK * (x + _GC * x * x * x)))


def _dgelu(x):
    t = jnp.tanh(_GK * (x + _GC * x * x * x))
    return 0.5 * (1.0 + t) + 0.5 * x * (1.0 - t * t) * _GK * (1.0 + 3.0 * _GC * x * x)


def _softplus(x):
    return jnp.maximum(x, 0.0) + jnp.log1p(jnp.exp(-jnp.abs(x)))


def _tile(n, cap):
    if n <= cap:
        return n
    best = None
    for t in range(128, cap + 1, 128):
        if n % t == 0:
            best = t
    assert best is not None, (n, cap)
    return best


def _params(sem):
    return pltpu.CompilerParams(dimension_semantics=sem, vmem_limit_bytes=VMEM_LIMIT)


def _exch_ops(src_ref, out_ref, send_sems, recv_sems, local_sem, gather):
    x, y, c = lax.axis_index("x"), lax.axis_index("y"), lax.axis_index("c")
    me = 4 * x + 2 * y + c

    def peer(j):
        px = 1 - x if (j >> 2) & 1 else x
        py = 1 - y if (j >> 1) & 1 else y
        pc = 1 - c if j & 1 else c
        return (px, py, pc), 4 * px + 2 * py + pc

    def copy(j, landing):
        dev, idx = peer(j)
        return pltpu.make_async_remote_copy(
            src_ref=src_ref if gather else src_ref.at[idx], dst_ref=out_ref.at[idx] if landing else out_ref.at[me],
            send_sem=send_sems.at[j - 1], recv_sem=recv_sems.at[j - 1], device_id=dev, device_id_type=pl.DeviceIdType.MESH)

    mine = pltpu.make_async_copy(src_ref if gather else src_ref.at[me], out_ref.at[me], local_sem)

    def start():
        mine.start()
        for j in range(1, NDEV):
            copy(j, False).start()

    def finish():
        for j in range(1, NDEV):
            copy(j, True).wait_recv()
        for j in range(1, NDEV):
            copy(j, False).wait_send()
        mine.wait()

    return start, finish


def _exch_shape(src, gather):
    return jax.ShapeDtypeStruct((NDEV,) + tuple(src.shape if gather else src.shape[1:]), src.dtype)


_EXCH_SEMS = [pltpu.SemaphoreType.DMA((NDEV - 1,)), pltpu.SemaphoreType.DMA((NDEV - 1,)), pltpu.SemaphoreType.DMA]


def _carry_call(body, name, grid, in_specs, out_specs, out_shape, scratch_shapes, args, carry=()):
    n_in, n_out, n_sc, nx = len(in_specs), len(out_specs), len(scratch_shapes), len(carry)
    sem = ("arbitrary",) * len(grid)
    if nx == 0:
        outs = pl.pallas_call(body, name=name, grid=grid, in_specs=in_specs, out_specs=out_specs, out_shape=out_shape,
                              scratch_shapes=scratch_shapes, compiler_params=_params(sem))(*args)
        return list(outs), []

    def wrapped(*refs):
        ins, srcs = refs[:n_in], refs[n_in:n_in + nx]
        outs, dsts = refs[n_in + nx:n_in + nx + n_out], refs[n_in + nx + n_out:n_in + 2 * nx + n_out]
        scratch, sems = refs[n_in + 2 * nx + n_out:n_in + 2 * nx + n_out + n_sc], refs[n_in + 2 * nx + n_out + n_sc:]
        ops = [_exch_ops(srcs[i], dsts[i], sems[3 * i], sems[3 * i + 1], sems[3 * i + 2], carry[i][1]) for i in range(nx)]
        first = functools.reduce(jnp.logical_and, [pl.program_id(d) == 0 for d in range(len(grid))])
        last = functools.reduce(jnp.logical_and, [pl.program_id(d) == grid[d] - 1 for d in range(len(grid))])

        @pl.when(first)
        def _():
            for start, _ in ops:
                start()

        body(*ins, *outs, *scratch)

        @pl.when(last)
        def _():
            for _, finish in ops:
                finish()

    hbm = pl.BlockSpec(memory_space=pl.ANY)
    outs = pl.pallas_call(
        wrapped, name=name, grid=grid, in_specs=list(in_specs) + [hbm] * nx, out_specs=list(out_specs) + [hbm] * nx,
        out_shape=list(out_shape) + [_exch_shape(s, g) for s, g in carry], scratch_shapes=list(scratch_shapes) + _EXCH_SEMS * nx,
        compiler_params=_params(sem))(*args, *[s for s, _ in carry])
    return list(outs[:n_out]), list(outs[n_out:])


def _mm(a, b, mode, name, out_dtype=f32, tm_cap=1024, tn_cap=1280, tk_cap=1280, carry=()):
    if mode == "nn":
        (M, K), (K2, N) = a.shape, b.shape
    elif mode == "nt":
        (M, K), (N, K2) = a.shape, b.shape
    else:
        (K, M), (K2, N) = a.shape, b.shape
        tk_cap = min(tk_cap, 1024)
    assert K == K2, (a.shape, b.shape, mode)
    tm, tn, tk = _tile(M, tm_cap), _tile(N, tn_cap), _tile(K, tk_cap)
    nk = K // tk
    if mode == "nn":
        a_spec = pl.BlockSpec((tm, tk), lambda i, j, k: (i, k))
        b_spec = pl.BlockSpec((tk, tn), lambda i, j, k: (k, j))
        dot = _dot
    elif mode == "nt":
        a_spec = pl.BlockSpec((tm, tk), lambda i, j, k: (i, k))
        b_spec = pl.BlockSpec((tn, tk), lambda i, j, k: (j, k))
        dot = _dot_nt
    else:
        a_spec = pl.BlockSpec((tk, tm), lambda i, j, k: (k, i))
        b_spec = pl.BlockSpec((tk, tn), lambda i, j, k: (k, j))
        dot = _dot_tn
    o_spec = pl.BlockSpec((tm, tn), lambda i, j, k: (i, j))

    def body_one(a_ref, b_ref, o_ref):
        o_ref[...] = dot(_bf(a_ref[...]), _bf(b_ref[...])).astype(out_dtype)

    def body_acc(a_ref, b_ref, o_ref, acc_ref):
        k = pl.program_id(2)

        @pl.when(k == 0)
        def _():
            acc_ref[...] = jnp.zeros_like(acc_ref)

        acc_ref[...] += dot(_bf(a_ref[...]), _bf(b_ref[...]))

        @pl.when(k == nk - 1)
        def _():
            o_ref[...] = acc_ref[...].astype(out_dtype)

    (out,), exchanged = _carry_call(body_one if nk == 1 else body_acc, name, (M // tm, N // tn, nk), [a_spec, b_spec], [o_spec],
                                    [jax.ShapeDtypeStruct((M, N), out_dtype)], [] if nk == 1 else [pltpu.VMEM((tm, tn), f32)],
                                    (a, b), carry)
    return (out, exchanged) if carry else out


def _row_spec(ts, w, col=0):
    return pl.BlockSpec((ts, w), lambda i: (i, col))


def _full_spec(shape):
    nd = len(shape)
    return pl.BlockSpec(shape, lambda i: (0,) * nd)


def _modulate(x, nw, shift, scale, name):
    T = x.shape[0]
    ts = _tile(T, 512)

    def body(x_ref, nw_ref, sh_ref, sc_ref, h_ref):
        xv = x_ref[...]
        r = lax.rsqrt(jnp.mean(xv * xv, axis=-1, keepdims=True) + EPS)
        h_ref[...] = ((xv * r * nw_ref[...]) * (1.0 + sc_ref[...]) + sh_ref[...]).astype(h_ref.dtype)

    return pl.pallas_call(
        body, name=name, grid=(T // ts,),
        in_specs=[_row_spec(ts, D), _full_spec((1, D)), _full_spec((1, D)), _full_spec((1, D))],
        out_specs=_row_spec(ts, D), out_shape=jax.ShapeDtypeStruct((T, D), _MXU),
        compiler_params=_params(("arbitrary",)))(x, nw, shift, scale)


def _final_loss(x, tgt, fw, name):
    T = x.shape[0]
    ts = _tile(T, 512)

    def body(x_ref, t_ref, fw_ref, dx_ref, acc_ref):
        @pl.when(pl.program_id(0) == 0)
        def _():
            acc_ref[...] = jnp.zeros_like(acc_ref)

        xv = x_ref[...]
        r = lax.rsqrt(jnp.mean(xv * xv, axis=-1, keepdims=True) + EPS)
        xn = xv * r
        e = xn * fw_ref[...] - t_ref[...]
        dy = e * (1.0 / D)
        acc_ref[0:1, :] += jnp.sum(dy * xn, axis=0, keepdims=True)
        acc_ref[1:2, :] += jnp.sum(jnp.sum(e * e, axis=0, keepdims=True), axis=1, keepdims=True)
        dxn = dy * fw_ref[...]
        dx_ref[...] = r * (dxn - xn * jnp.mean(dxn * xn, axis=-1, keepdims=True))

    return pl.pallas_call(
        body, name=name, grid=(T // ts,),
        in_specs=[_row_spec(ts, D), _row_spec(ts, D), _full_spec((1, D))],
        out_specs=[_row_spec(ts, D), _full_spec((8, D))],
        out_shape=[jax.ShapeDtypeStruct((T, D), f32), jax.ShapeDtypeStruct((8, D), f32)],
        compiler_params=_params(("arbitrary",)))(x, tgt, fw)


def _tril(n):
    return lax.broadcasted_iota(jnp.int32, (n, n), 0) >= lax.broadcasted_iota(jnp.int32, (n, n), 1)


def _gm_chunk_fwd(u, v, z, lw, lb, ws_ref, bsx):
    gu, gv = _gelu(u), _gelu(v)
    mu = jnp.mean(gv, axis=-1, keepdims=True)
    cen = gv - mu
    rstd = lax.rsqrt(jnp.mean(cen * cen, axis=-1, keepdims=True) + EPS)
    vhat = cen * rstd
    vn = _bf(vhat * lw + lb)
    tri = _tril(Q)
    mixed = jnp.concatenate(
        [_dot(_bf(jnp.where(tri, ws_ref[g], 0.0)), vn[:, g * Q:(g + 1) * Q]) for g in range(NG)], axis=1) + bsx
    return gu, vhat, rstd, vn, mixed


def _gmlp_fwd(pA, lw, lb, ws, bsx, name):
    T = pA.shape[0]
    ts = _tile(T, 512)

    def body(u_ref, v_ref, z_ref, lw_ref, lb_ref, ws_ref, bsx_ref, y_ref):
        def chunk(ci, carry):
            rows = pl.ds(pl.multiple_of(ci * Q, Q), Q)
            u, v, z = u_ref[rows, :].astype(f32), v_ref[rows, :].astype(f32), z_ref[rows, :].astype(f32)
            gu, _, _, _, mixed = _gm_chunk_fwd(u, v, z, lw_ref[...], lb_ref[...], ws_ref, bsx_ref[...])
            y_ref[rows, :] = (gu * mixed * _silu(z)).astype(y_ref.dtype)
            return carry

        lax.fori_loop(0, ts // Q, chunk, 0)

    return pl.pallas_call(
        body, name=name, grid=(T // ts,),
        in_specs=[_row_spec(ts, D, 0), _row_spec(ts, D, 1), _row_spec(ts, D, 2), _full_spec((1, D)), _full_spec((1, D)),
                  _full_spec((NG, Q, Q)), _full_spec((Q, D))],
        out_specs=_row_spec(ts, D), out_shape=jax.ShapeDtypeStruct((T, D), _MXU),
        compiler_params=_params(("arbitrary",)))(pA, pA, pA, lw, lb, ws, bsx)


def _gmlp_bwd(dya, pA, lw, lb, ws, bsx, name):
    T = pA.shape[0]
    ts = _tile(T, 512)

    def body(dy_ref, u_ref, v_ref, z_ref, lw_ref, lb_ref, ws_ref, bsx_ref, dp_ref, acc_ref, dws_ref, dbs_ref):
        @pl.when(pl.program_id(0) == 0)
        def _():
            acc_ref[...] = jnp.zeros_like(acc_ref)
            dws_ref[...] = jnp.zeros_like(dws_ref)
            dbs_ref[...] = jnp.zeros_like(dbs_ref)

        tri = _tril(Q)

        def chunk(ci, carry):
            rows = pl.ds(pl.multiple_of(ci * Q, Q), Q)
            u, v, z, dy = u_ref[rows, :].astype(f32), v_ref[rows, :].astype(f32), z_ref[rows, :].astype(f32), dy_ref[rows, :].astype(f32)
            gu, vhat, rstd, vn, mixed = _gm_chunk_fwd(u, v, z, lw_ref[...], lb_ref[...], ws_ref, bsx_ref[...])
            sz = _silu(z)
            dp_ref[rows, 0:D] = (dy * mixed * sz * _dgelu(u)).astype(dp_ref.dtype)
            dp_ref[rows, 2 * D:3 * D] = (dy * gu * mixed * _dsilu(z)).astype(dp_ref.dtype)
            dmixed = dy * gu * sz
            dbs_ref[...] += dmixed
            dmb = _bf(dmixed)
            dvn_parts = []
            for g in range(NG):
                cols = slice(g * Q, (g + 1) * Q)
                wg = _bf(jnp.where(tri, ws_ref[g], 0.0))
                dvn_parts.append(_dot_tn(wg, dmb[:, cols]))
                dws_ref[g] += jnp.where(tri, _dot_nt(dmb[:, cols], vn[:, cols]), 0.0)
            dvn = jnp.concatenate(dvn_parts, axis=1)
            acc_ref[0:1, :] += jnp.sum(dvn * vhat, axis=0, keepdims=True)
            acc_ref[1:2, :] += jnp.sum(dvn, axis=0, keepdims=True)
            dvh = dvn * lw_ref[...]
            dgv = rstd * (dvh - jnp.mean(dvh, axis=-1, keepdims=True) - vhat * jnp.mean(dvh * vhat, axis=-1, keepdims=True))
            dp_ref[rows, D:2 * D] = (dgv * _dgelu(v)).astype(dp_ref.dtype)
            return carry

        lax.fori_loop(0, ts // Q, chunk, 0)

    return pl.pallas_call(
        body, name=name, grid=(T // ts,),
        in_specs=[_row_spec(ts, D), _row_spec(ts, D, 0), _row_spec(ts, D, 1), _row_spec(ts, D, 2), _full_spec((1, D)),
                  _full_spec((1, D)), _full_spec((NG, Q, Q)), _full_spec((Q, D))],
        out_specs=[_row_spec(ts, WA), _full_spec((8, D)), _full_spec((NG, Q, Q)), _full_spec((Q, D))],
        out_shape=[jax.ShapeDtypeStruct((T, WA), _MXU), jax.ShapeDtypeStruct((8, D), f32),
                   jax.ShapeDtypeStruct((NG, Q, Q), f32), jax.ShapeDtypeStruct((Q, D), f32)],
        compiler_params=_params(("arbitrary",)))(dya, pA, pA, pA, lw, lb, ws, bsx)


def _head_maps():
    h = lax.broadcasted_iota(jnp.int32, (128, DI), 0)
    ch = lax.broadcasted_iota(jnp.int32, (128, DI), 1)
    ex = (ch // P == h).astype(_MXU)
    return ex, ex.T


def _split(v, parts):
    out = []
    for _ in range(parts - 1):
        p = _bf(v)
        out.append(p)
        v = v - p.astype(f32)
    out.append(_bf(v))
    return out


def _expand(v, ex_ref, parts):
    acc = None
    for p in _split(v, parts):
        t = _dot(p, ex_ref[...])
        acc = t if acc is None else acc + t
    return acc


def _reduce(v, rd_ref, parts=2):
    acc = None
    for p in _split(v, parts):
        t = _dot(p, rd_ref[...])
        acc = t if acc is None else acc + t
    return acc


def _ssd_time(pb_ref, dtb_ref, alog_ref):
    xdt = pb_ref[:, OFF_DT:OFF_DT + 128] + dtb_ref[...]
    dt = _softplus(xdt)
    a = -jnp.exp(alog_ref[...])
    cs = _dot_exact(_tril(Q).astype(f32), dt * a)
    return xdt, dt, a, cs


def _head_mask(r):
    return lax.broadcasted_iota(jnp.int32, (Q, GRP), 1) // P == r


def _ssd_group_fwd(g, xa_ref, s_prev, cs, cs_t, dtx_ref, csx_ref, dsk_ref):
    cols = slice(g * GRP, (g + 1) * GRP)
    xs = xa_ref[:, cols]
    bb = _bf(xa_ref[:, DI + g * NS:DI + (g + 1) * NS])
    cb = _bf(xa_ref[:, DI + NG * NS + g * NS:DI + NG * NS + (g + 1) * NS])
    gm = _dot_nt(cb, bb)
    xd = xs * dtx_ref[:, cols]
    csx = csx_ref[:, cols]
    csl = csx_ref[Q - 1:Q, cols]
    tri = _tril(Q)
    lms = [jnp.exp(jnp.where(tri, cs[:, HPG * g + r:HPG * g + r + 1] - cs_t[HPG * g + r:HPG * g + r + 1, :], NEG))
           for r in range(HPG)]
    mfs = [gm * lm for lm in lms]
    mcat = jnp.concatenate([_bf(m) for m in mfs], axis=1)
    xbd = jnp.concatenate([_bf(jnp.where(_head_mask(r), xd, 0.0)) for r in range(HPG)], axis=0)
    ydiag = _dot(mcat, xbd)
    ecs, dte, ecl = jnp.exp(csx), jnp.exp(csl - csx), jnp.exp(csl)
    yoff = ecs * _dot(cb, _bf(s_prev))
    y = ydiag + yoff + xs * dsk_ref[:, cols]
    xdd = xd * dte
    s_new = s_prev * ecl + _dot_tn(bb, _bf(xdd))
    return y, s_new, (xs, bb, cb, xd, lms, mfs, mcat, xbd, yoff, xdd, ecs, dte, ecl)


def _ssd_fwd(pB, cw, cb, dtb, alog, dsk, snw, name, carry=()):
    T = pB.shape[0]
    nc = T // Q
    ex, _ = _head_maps()

    def body(pb_ref, cw_ref, cb_ref, dtb_ref, alog_ref, dsk_ref, snw_ref, ex_ref, y_ref, xc_ref, st_ref,
             s_ref, ext_ref, xa_ref, dtx_ref, csx_ref):
        @pl.when(pl.program_id(0) == 0)
        def _():
            s_ref[...] = jnp.zeros_like(s_ref)
            ext_ref[...] = jnp.zeros_like(ext_ref)

        ext_ref[8:8 + Q, :] = pb_ref[:, OFF_XBC:OFF_XBC + CD]
        for j in range(CD // 512):
            cj = slice(j * 512, (j + 1) * 512)
            e = ext_ref[:, cj]
            xc = cb_ref[:, cj] + cw_ref[KC - 1:KC, cj] * e[8:8 + Q]
            for s in range(1, KC):
                xc = xc + cw_ref[KC - 1 - s:KC - s, cj] * pltpu.roll(e, s, 0)[8:8 + Q]
            xc_ref[:, cj] = xc
            xa_ref[:, cj] = _silu(xc)
        ext_ref[0:8, :] = ext_ref[Q:Q + 8, :]

        _, dt, _, cs = _ssd_time(pb_ref, dtb_ref, alog_ref)
        cs_t = cs.T
        dtx_ref[...] = _expand(dt, ex_ref, 2)
        csx_ref[...] = _expand(cs, ex_ref, 3)
        for g in range(NG):
            s_prev = s_ref[g]
            st_ref[0, g] = s_prev
            y, s_new, _ = _ssd_group_fwd(g, xa_ref, s_prev, cs, cs_t, dtx_ref, csx_ref, dsk_ref)
            s_ref[g] = s_new
            cols = slice(g * GRP, (g + 1) * GRP)
            yz = y * _silu(pb_ref[:, cols])
            rr = lax.rsqrt(jnp.mean(yz * yz, axis=-1, keepdims=True) + EPS)
            y_ref[:, cols] = (yz * rr * snw_ref[:, cols]).astype(y_ref.dtype)

    outs, exchanged = _carry_call(
        body, name, (nc,),
        [_row_spec(Q, WB), _full_spec((8, CD)), _full_spec((1, CD)), _full_spec((1, 128)), _full_spec((1, 128)),
         _full_spec((1, DI)), _full_spec((1, DI)), _full_spec((128, DI))],
        [_row_spec(Q, DI), _row_spec(Q, CD), pl.BlockSpec((1, NG, NS, GRP), lambda i: (i, 0, 0, 0))],
        [jax.ShapeDtypeStruct((T, DI), _MXU), jax.ShapeDtypeStruct((T, CD), f32), jax.ShapeDtypeStruct((nc, NG, NS, GRP), f32)],
        [pltpu.VMEM((NG, NS, GRP), f32), pltpu.VMEM((Q + 8, CD), f32), pltpu.VMEM((Q, CD), f32),
         pltpu.VMEM((Q, DI), f32), pltpu.VMEM((Q, DI), f32)],
        (pB, cw, cb, dtb, alog, dsk, snw, ex), carry)
    return outs + [exchanged]


def _ssd_bwd(dyb, pB, xc, states, cw, dtb, alog, dsk, snw, name, carry=()):
    T = pB.shape[0]
    nc = T // Q
    ex, rd = _head_maps()
    selr = (lax.broadcasted_iota(jnp.int32, (HPG * Q, 128), 0) // Q == lax.broadcasted_iota(jnp.int32, (HPG * Q, 128), 1)).astype(_MXU)

    def body(dy_ref, pb_ref, xc_ref, st_ref, cw_ref, dtb_ref, alog_ref, dsk_ref, snw_ref, ex_ref, rd_ref, selr_ref,
             dp_ref, dcw_ref, dcb_ref, dhd_ref, dcol_ref,
             ds_ref, xa_ref, sg_ref, dxa_ref, dxe_ref, dtx_ref, csx_ref, rcs_ref, rdt_ref, rl_ref, dcs_ref, dcst_ref):
        @pl.when(pl.program_id(0) == 0)
        def _():
            ds_ref[...] = jnp.zeros_like(ds_ref)
            dxe_ref[...] = jnp.zeros_like(dxe_ref)
            dcw_ref[...] = jnp.zeros_like(dcw_ref)
            dcb_ref[...] = jnp.zeros_like(dcb_ref)
            dhd_ref[...] = jnp.zeros_like(dhd_ref)
            dcol_ref[...] = jnp.zeros_like(dcol_ref)
            rl_ref[...] = jnp.zeros_like(rl_ref)
            dcst_ref[...] = jnp.zeros_like(dcst_ref)

        dcs_ref[...] = jnp.zeros_like(dcs_ref)
        for j in range(CD // 512):
            cj = slice(j * 512, (j + 1) * 512)
            sg_ref[:, cj] = _sigmoid(xc_ref[:, cj])
            xa_ref[:, cj] = xc_ref[:, cj] * sg_ref[:, cj]
        xdt, dt, a, cs = _ssd_time(pb_ref, dtb_ref, alog_ref)
        cs_t = cs.T
        dtx_ref[...] = _expand(dt, ex_ref, 2)
        csx_ref[...] = _expand(cs, ex_ref, 3)

        for g in range(NG):
            cols = slice(g * GRP, (g + 1) * GRP)
            s_prev = st_ref[0, g]
            sb = _bf(s_prev)
            y, _, (xs, bb, cbm, xd, lms, mfs, mcat, xbd, yoff, xdd, ecs, dte, ecl) = _ssd_group_fwd(
                g, xa_ref, s_prev, cs, cs_t, dtx_ref, csx_ref, dsk_ref)
            z = pb_ref[:, cols]
            sz = _silu(z)
            yz = y * sz
            rr = lax.rsqrt(jnp.mean(yz * yz, axis=-1, keepdims=True) + EPS)
            nrm = yz * rr
            dyb_g = dy_ref[:, cols].astype(f32)
            dcol_ref[1:2, cols] += jnp.sum(dyb_g * nrm, axis=0, keepdims=True)
            dn = dyb_g * snw_ref[:, cols]
            dyz = rr * (dn - nrm * jnp.mean(dn * nrm, axis=-1, keepdims=True))
            dyv = dyz * sz
            dp_ref[:, cols] = (dyz * y * _dsilu(z)).astype(dp_ref.dtype)
            dcol_ref[0:1, cols] += jnp.sum(dyv * xs, axis=0, keepdims=True)
            dy16 = _bf(dyv)
            dmcat = _dot_nt(dy16, xbd)
            dg = dmcat[:, 0:Q] * lms[0]
            for r in range(1, HPG):
                dg = dg + dmcat[:, r * Q:(r + 1) * Q] * lms[r]
            e16 = _bf(jnp.concatenate([dmcat[:, r * Q:(r + 1) * Q] * mfs[r] for r in range(HPG)], axis=1))
            rows = _dot(e16, selr_ref[...])
            dcs_ref[...] += rows if g == 0 else pltpu.roll(rows, HPG * g, 1)
            csum = _dot(jnp.ones((8, Q), _MXU), e16)
            for r in range(HPG):
                dcst_ref[HPG * g + r:HPG * g + r + 1, :] = csum[0:1, r * Q:(r + 1) * Q]
            big = _dot_tn(mcat, dy16)
            dxd_diag = jnp.where(_head_mask(0), big[0:Q], 0.0)
            for r in range(1, HPG):
                dxd_diag = dxd_diag + jnp.where(_head_mask(r), big[r * Q:(r + 1) * Q], 0.0)
            dsn = ds_ref[g]
            dsn16 = _bf(dsn)
            dxdd = _dot(bb, dsn16)
            t_state = dxdd * xdd
            dxd = dxd_diag + dxdd * dte
            rcs_ref[:, cols] = dyv * yoff - t_state
            rl_ref[0:1, cols] = (jnp.sum(t_state, axis=0, keepdims=True)
                                 + ecl * jnp.sum(s_prev * dsn, axis=0, keepdims=True))
            rdt_ref[:, cols] = dxd * xs
            dxa_ref[:, cols] = dyv * dsk_ref[:, cols] + dxd * dtx_ref[:, cols]
            dg16 = _bf(dg)
            dw16 = _bf(dyv * ecs)
            dxa_ref[:, DI + NG * NS + g * NS:DI + NG * NS + (g + 1) * NS] = _dot(dg16, bb) + _dot_nt(dw16, sb)
            dxa_ref[:, DI + g * NS:DI + (g + 1) * NS] = _dot_tn(dg16, cbm) + _dot_nt(_bf(xdd), dsn16)
            ds_ref[g] = dsn * ecl + _dot_tn(cbm, dw16)

        row = lax.broadcasted_iota(jnp.int32, (Q, 128), 0)
        dcs = (dcs_ref[...] - dcst_ref[...].T + _reduce(rcs_ref[...], rd_ref)
               + jnp.where(row == Q - 1, _reduce(rl_ref[...], rd_ref)[0:1, :], 0.0))
        upper = lax.broadcasted_iota(jnp.int32, (Q, Q), 0) <= lax.broadcasted_iota(jnp.int32, (Q, Q), 1)
        dadt = _dot_exact(upper.astype(f32), dcs)
        valid = lax.broadcasted_iota(jnp.int32, (Q, 128), 1) < NH
        ddt = jnp.where(valid, _reduce(rdt_ref[...], rd_ref, 1) + dadt * a, 0.0)
        ddtr = ddt * _sigmoid(xdt)
        dhd_ref[0:1, :] += jnp.sum(ddtr, axis=0, keepdims=True)
        dhd_ref[1:2, :] += jnp.sum(jnp.where(valid, dadt * dt * a, 0.0), axis=0, keepdims=True)
        dp_ref[:, OFF_DT:OFF_DT + 128] = ddtr.astype(dp_ref.dtype)
        dp_ref[:, OFF_DT + 128:WB] = jnp.zeros((Q, WB - OFF_DT - 128), dp_ref.dtype)
        for j in range(CD // 512):
            cj = slice(j * 512, (j + 1) * 512)
            sg = sg_ref[:, cj]
            dxc = dxa_ref[:, cj] * (sg + xa_ref[:, cj] * (1.0 - sg))
            dxe_ref[0:Q, cj] = dxc
            dcb_ref[0:1, cj] += jnp.sum(dxc, axis=0, keepdims=True)
            raw = pb_ref[:, OFF_XBC + j * 512:OFF_XBC + (j + 1) * 512]
            e = dxe_ref[:, cj]
            dcw_ref[KC - 1:KC, cj] += jnp.sum(dxc * raw, axis=0, keepdims=True)
            dxb = cw_ref[KC - 1:KC, cj] * dxc
            for s in range(1, KC):
                sh = pltpu.roll(e, Q + 8 - s, 0)[0:Q]
                dcw_ref[KC - 1 - s:KC - s, cj] += jnp.sum(sh * raw, axis=0, keepdims=True)
                dxb = dxb + cw_ref[KC - 1 - s:KC - s, cj] * sh
            dp_ref[:, OFF_XBC + j * 512:OFF_XBC + (j + 1) * 512] = dxb.astype(dp_ref.dtype)
        dxe_ref[Q:Q + 8, :] = dxe_ref[0:8, :]

    rev = lambda i: (nc - 1 - i, 0)
    outs, exchanged = _carry_call(
        body, name, (nc,),
        [pl.BlockSpec((Q, DI), rev), pl.BlockSpec((Q, WB), rev), pl.BlockSpec((Q, CD), rev),
         pl.BlockSpec((1, NG, NS, GRP), lambda i: (nc - 1 - i, 0, 0, 0)),
         _full_spec((8, CD)), _full_spec((1, 128)), _full_spec((1, 128)),
         _full_spec((1, DI)), _full_spec((1, DI)), _full_spec((128, DI)), _full_spec((DI, 128)), _full_spec((HPG * Q, 128))],
        [pl.BlockSpec((Q, WB), rev), _full_spec((8, CD)), _full_spec((8, CD)), _full_spec((8, 128)), _full_spec((8, DI))],
        [jax.ShapeDtypeStruct((T, WB), _MXU), jax.ShapeDtypeStruct((8, CD), f32), jax.ShapeDtypeStruct((8, CD), f32),
         jax.ShapeDtypeStruct((8, 128), f32), jax.ShapeDtypeStruct((8, DI), f32)],
        [pltpu.VMEM((NG, NS, GRP), f32), pltpu.VMEM((Q, CD), f32), pltpu.VMEM((Q, CD), f32), pltpu.VMEM((Q, CD), f32),
         pltpu.VMEM((Q + 8, CD), f32), pltpu.VMEM((Q, DI), f32), pltpu.VMEM((Q, DI), f32),
         pltpu.VMEM((Q, DI), f32), pltpu.VMEM((Q, DI), f32), pltpu.VMEM((8, DI), f32),
         pltpu.VMEM((Q, 128), f32), pltpu.VMEM((128, Q), f32)],
        (dyb, pB, xc, states, cw, dtb, alog, dsk, snw, ex, rd, selr), carry)
    return outs + [exchanged]


def _merge_fwd(ya, yb, pG, x, gate, wa, wb, wo, name, carry=()):
    T = x.shape[0]
    ts = _tile(T, 256)

    def body(ya_ref, yb_ref, g_ref, x_ref, gate_ref, wa_ref, wb_ref, wo_ref, xo_ref, mg_ref, pa_ref, pb_ref):
        pa = _dot(ya_ref[...], wa_ref[...])
        pb = _dot(yb_ref[...], wb_ref[...])
        merged = _sigmoid(g_ref[:, 0:D].astype(f32)) * pa + _sigmoid(g_ref[:, D:2 * D].astype(f32)) * pb
        mg = _bf(merged)
        xo_ref[...] = x_ref[...] + gate_ref[...] * _dot(mg, wo_ref[...])
        mg_ref[...] = mg
        pa_ref[...] = pa.astype(pa_ref.dtype)
        pb_ref[...] = pb.astype(pb_ref.dtype)

    outs, exchanged = _carry_call(
        body, name, (T // ts,),
        [_row_spec(ts, D), _row_spec(ts, DI), _row_spec(ts, WG), _row_spec(ts, D), _full_spec((1, D)),
         _full_spec((D, D)), _full_spec((DI, D)), _full_spec((D, D))],
        [_row_spec(ts, D)] * 4,
        [jax.ShapeDtypeStruct((T, D), f32), jax.ShapeDtypeStruct((T, D), _MXU), jax.ShapeDtypeStruct((T, D), _MXU),
         jax.ShapeDtypeStruct((T, D), _MXU)],
        [], (ya, yb, pG, x, gate, wa, wb, wo), carry)
    return outs + [exchanged]


def _merge_bwd(dxout, merged, pa, pb, pG, gate, wo, name):
    T = dxout.shape[0]
    ts = _tile(T, 256)

    def body(dx_ref, mg_ref, pa_ref, pb_ref, g_ref, gate_ref, wo_ref, do_ref, dpa_ref, dpb_ref, dg_ref, acc_ref):
        @pl.when(pl.program_id(0) == 0)
        def _():
            acc_ref[...] = jnp.zeros_like(acc_ref)

        dxo = dx_ref[...]
        acc_ref[0:1, :] += jnp.sum(dxo * _dot(mg_ref[...], wo_ref[...]), axis=0, keepdims=True)
        do = _bf(dxo * gate_ref[...])
        do_ref[...] = do
        dmerged = _dot_nt(do, wo_ref[...])
        sa, sb = _sigmoid(g_ref[:, 0:D].astype(f32)), _sigmoid(g_ref[:, D:2 * D].astype(f32))
        dpa_ref[...] = (dmerged * sa).astype(dpa_ref.dtype)
        dpb_ref[...] = (dmerged * sb).astype(dpb_ref.dtype)
        dg_ref[:, 0:D] = (dmerged * pa_ref[...].astype(f32) * sa * (1.0 - sa)).astype(dg_ref.dtype)
        dg_ref[:, D:2 * D] = (dmerged * pb_ref[...].astype(f32) * sb * (1.0 - sb)).astype(dg_ref.dtype)

    return pl.pallas_call(
        body, name=name, grid=(T // ts,),
        in_specs=[_row_spec(ts, D), _row_spec(ts, D), _row_spec(ts, D), _row_spec(ts, D), _row_spec(ts, WG),
                  _full_spec((1, D)), _full_spec((D, D))],
        out_specs=[_row_spec(ts, D), _row_spec(ts, D), _row_spec(ts, D), _row_spec(ts, WG), _full_spec((8, D))],
        out_shape=[jax.ShapeDtypeStruct((T, D), _MXU), jax.ShapeDtypeStruct((T, D), _MXU), jax.ShapeDtypeStruct((T, D), _MXU),
                   jax.ShapeDtypeStruct((T, WG), _MXU), jax.ShapeDtypeStruct((8, D), f32)],
        compiler_params=_params(("arbitrary",)))(dxout, merged, pa, pb, pG, gate, wo)


def _rows_split(r):
    ra, rb = D // NDEV, DI // NDEV
    return r[:, 0:ra].reshape(D, D), r[:, ra:ra + rb].reshape(DI, D), r[:, ra + rb:].reshape(D, D)


def _layer_fwd(x, mod, lp, tag, carry=None):
    carry = carry or {}
    got = {}

    def mm(key, w, **kw):
        if carry.get(key):
            out, got[key] = _mm(h, w, "nn", f"{key}_{tag}", carry=carry[key], **kw)
            return out
        return _mm(h, w, "nn", f"{key}_{tag}", **kw)

    h = _modulate(x, lp["nw"], mod[0:1], mod[1:2], f"modulate_{tag}")
    pA = mm("proj_gm", lp["w_gm"], out_dtype=_MXU)
    pB = mm("proj_ssd", lp["w_ssd"])
    pG = mm("proj_gate", lp["w_g"], out_dtype=_MXU)
    if "wa" not in lp:
        lp = dict(lp)
        lp["wa"], lp["wb"], lp["wo"] = _rows_split(jnp.concatenate([got["proj_gm"][0], got["proj_gate"][0]], axis=1))
    ya = _gmlp_fwd(pA, lp["lw"], lp["lb"], lp["ws"], lp["bsx"], f"gmlp_fwd_{tag}")
    yb, xc, states, got["ssd_fwd"] = _ssd_fwd(pB, lp["cw"], lp["cb"], lp["dtb"], lp["alog"], lp["dsk"], lp["snw"], f"ssd_fwd_{tag}",
                                              carry.get("ssd_fwd", ()))
    xo, merged, pa, pb, _ = _merge_fwd(ya, yb, pG, x, mod[2:3], lp["wa"], lp["wb"], lp["wo"], f"merge_fwd_{tag}")
    return xo, dict(x=x, h=h, pA=pA, pB=pB, pG=pG, ya=ya, yb=yb, xc=xc, states=states, merged=merged, pa=pa, pb=pb), got, lp


def _dh_modulate_bwd(dpG, dpA, dpB, w_g, w_gm, w_ssd, dxout, x, nw, scale, name, carry=()):
    T = x.shape[0]
    tm = _tile(T, 1024)
    fam = [(WG, _tile(WG, 1024)), (WA, _tile(WA, 1024)), (WB, _tile(WB, 1280))]
    steps = [w // k for w, k in fam]
    first = [0, steps[0], steps[0] + steps[1]]
    ns = sum(steps)

    def chunk(f):
        return lambda i, s: jnp.clip(s - first[f], 0, steps[f] - 1)

    a_specs = [pl.BlockSpec((tm, fam[f][1]), lambda i, s, c=chunk(f): (i, c(i, s))) for f in range(3)]
    b_specs = [pl.BlockSpec((D, fam[f][1]), lambda i, s, c=chunk(f): (0, c(i, s))) for f in range(3)]
    tok = pl.BlockSpec((tm, D), lambda i, s: (i, 0))
    tok_once = pl.BlockSpec((tm, D), lambda i, s: (i, 0), pipeline_mode=pl.Buffered(1))
    vec = pl.BlockSpec((1, D), lambda i, s: (0, 0))

    def body(ag, aa, ab, bg, ba, bb, dxo_ref, x_ref, nw_ref, sc_ref, dx_ref, sum_ref, acc_ref):
        i, s = pl.program_id(0), pl.program_id(1)

        @pl.when(jnp.logical_and(i == 0, s == 0))
        def _():
            sum_ref[...] = jnp.zeros_like(sum_ref)

        @pl.when(s == 0)
        def _():
            acc_ref[...] = jnp.zeros_like(acc_ref)

        @pl.when(s < first[1])
        def _():
            acc_ref[...] += _dot_nt(ag[...], bg[...])

        @pl.when(jnp.logical_and(s >= first[1], s < first[2]))
        def _():
            acc_ref[...] += _dot_nt(aa[...], ba[...])

        @pl.when(s >= first[2])
        def _():
            acc_ref[...] += _dot_nt(ab[...], bb[...])

        @pl.when(s == ns - 1)
        def _():
            xv, dh_v = x_ref[...], acc_ref[...]
            r = lax.rsqrt(jnp.mean(xv * xv, axis=-1, keepdims=True) + EPS)
            xn = xv * r
            hn = xn * nw_ref[...]
            dhn = dh_v * (1.0 + sc_ref[...])
            sum_ref[0:1, :] += jnp.sum(dh_v, axis=0, keepdims=True)
            sum_ref[1:2, :] += jnp.sum(dh_v * hn, axis=0, keepdims=True)
            sum_ref[2:3, :] += jnp.sum(dhn * xn, axis=0, keepdims=True)
            dxn = dhn * nw_ref[...]
            dx_ref[...] = dxo_ref[...] + r * (dxn - xn * jnp.mean(dxn * xn, axis=-1, keepdims=True))

    outs, exchanged = _carry_call(
        body, name, (T // tm, ns), a_specs + b_specs + [tok_once, tok_once, vec, vec], [tok, pl.BlockSpec((8, D), lambda i, s: (0, 0))],
        [jax.ShapeDtypeStruct((T, D), f32), jax.ShapeDtypeStruct((8, D), f32)], [pltpu.VMEM((tm, D), f32)],
        (dpG, dpA, dpB, w_g, w_gm, w_ssd, dxout, x, nw, scale), carry)
    return outs + [exchanged]


def _win_blocks(g):
    full = jnp.concatenate([g["w_gm"], g["w_ssd"][:, 0:DI + CD], g["w_ssd"][:, DI + CD:DI + CD + NH], g["w_g"]], axis=1)
    return jnp.transpose(full.reshape(D, NDEV, NIN // NDEV), (1, 0, 2))


def _row_blocks(g):
    return jnp.concatenate([g["wa"].reshape(NDEV, D // NDEV, D), g["wb"].reshape(NDEV, DI // NDEV, D),
                            g["wo"].reshape(NDEV, D // NDEV, D)], axis=1)


def _layer_bwd(dxo, sv, mod, lp, tag, carry=(), scatter_own=None):
    do, dpa, dpb, dpG, s_gate = _merge_bwd(dxo, sv["merged"], sv["pa"], sv["pb"], sv["pG"], mod[2:3], lp["wo"], f"merge_bwd_{tag}")
    g = {}
    g["wo"] = _mm(sv["merged"], do, "tn", f"dw_out_{tag}", out_dtype=_WIRE)
    g["wa"] = _mm(sv["ya"], dpa, "tn", f"dw_proj_a_{tag}", out_dtype=_WIRE)
    g["wb"] = _mm(sv["yb"], dpb, "tn", f"dw_proj_b_{tag}", out_dtype=_WIRE)
    dya = _mm(dpa, lp["wa"], "nt", f"dy_a_{tag}", out_dtype=_MXU)
    dyb = _mm(dpb, lp["wb"], "nt", f"dy_b_{tag}", out_dtype=_MXU)
    dpA, s_ln, g["ws"], dbsx = _gmlp_bwd(dya, sv["pA"], lp["lw"], lp["lb"], lp["ws"], lp["bsx"], f"gmlp_bwd_{tag}")
    carry = list(carry) + ([(_row_blocks(g), False)] if scatter_own else [])
    dpB, g["cw"], s_cb, s_hd, s_col, got = _ssd_bwd(dyb, sv["pB"], sv["xc"], sv["states"], lp["cw"], lp["dtb"], lp["alog"],
                                                    lp["dsk"], lp["snw"], f"ssd_bwd_{tag}", carry)
    g["lw"], g["lb"] = s_ln[0], s_ln[1]
    g["bs"] = dbsx.reshape(Q, NG, Q).sum(-1).T
    g["cb"] = s_cb[0]
    g["dtb"], g["alog"] = s_hd[0, :NH], s_hd[1, :NH]
    g["dsk"] = s_col[0].reshape(NH, P).sum(-1)
    g["snw"] = s_col[1]
    g["cw"] = g["cw"][0:KC]
    g["w_g"] = _mm(sv["h"], dpG, "tn", f"dw_gate_{tag}", out_dtype=_WIRE)
    g["w_gm"] = _mm(sv["h"], dpA, "tn", f"dw_gm_{tag}", out_dtype=_WIRE)
    if scatter_own:
        g["w_ssd"], got_slab = _mm(sv["h"], dpB, "tn", f"dw_ssd_{tag}", out_dtype=_WIRE, carry=[(scatter_own(g), True)])
    else:
        g["w_ssd"] = _mm(sv["h"], dpB, "tn", f"dw_ssd_{tag}", out_dtype=_WIRE)
    dx, s_mod, got_dh = _dh_modulate_bwd(dpG, dpA, dpB, lp["w_g"], lp["w_gm"], lp["w_ssd"], dxo, sv["x"], lp["nw"], mod[1:2],
                                         f"dh_{tag}", [(_win_blocks(g), False)] if scatter_own else ())
    if scatter_own:
        got, got_own = got[:-1], [got[-1], got_dh[0], got_slab[0]]
    else:
        got_own = []
    g["mod"] = jnp.concatenate([s_mod[0], s_mod[1], s_gate[0]])
    g["nw"] = s_mod[2]
    return dx, g, got, got_own


def _prep_layer(nw, w_in_full, lw, lb, ws, bs, cw_full, cb, dtb, alog, dsk, snw, rows=None):
    z = jnp.zeros((D, WB - (DI + CD + NH)), w_in_full.dtype)
    pad_h = lambda v: jnp.pad(v, (0, 128 - NH)).reshape(1, 128)
    extra = dict(zip(("wa", "wb", "wo"), _rows_split(rows))) if rows is not None else {}
    return dict(
        **extra,
        nw=nw.reshape(1, D),
        w_gm=w_in_full[:, 0:WA],
        w_ssd=jnp.concatenate([w_in_full[:, WA:WA + DI + CD], w_in_full[:, WA + DI + CD:WA + DI + CD + NH], z], axis=1),
        w_g=w_in_full[:, WA + DI + CD + NH:NIN],
        lw=lw.reshape(1, D), lb=lb.reshape(1, D), ws=ws, bsx=jnp.repeat(bs.T, Q, axis=1),
        cw=jnp.pad(cw_full, ((0, 8 - KC), (0, 0))), cb=cb.reshape(1, CD), dtb=pad_h(dtb), alog=pad_h(alog),
        dsk=jnp.repeat(dsk, P).reshape(1, DI), snw=snw.reshape(1, DI))


def _exchange(src, gather, name):
    if not gather:
        assert src.shape[0] == NDEV

    def body(src_ref, out_ref, send_sems, recv_sems, local_sem):
        start, finish = _exch_ops(src_ref, out_ref, send_sems, recv_sems, local_sem, gather)
        start()
        finish()

    return pl.pallas_call(
        body, name=name, out_shape=_exch_shape(src, gather),
        in_specs=[pl.BlockSpec(memory_space=pl.ANY)], out_specs=pl.BlockSpec(memory_space=pl.ANY), scratch_shapes=_EXCH_SEMS)(src)


def _gather2(src, name):
    def body(src_ref, out_ref, send_sems, recv_sems, local_sem):
        x, y, c = lax.axis_index("x"), lax.axis_index("y"), lax.axis_index("c")
        me, sibling = (x, y, c), (x, y, 1 - c)
        chips = [(1 - x, y), (x, 1 - y), (1 - x, 1 - y)]

        def slot(px, py, pc):
            return out_ref.at[4 * px + 2 * py + pc]

        def copy(k, block, to, src=None):
            return pltpu.make_async_remote_copy(
                src_ref=slot(*block) if src is None else src, dst_ref=slot(*block), send_sem=send_sems.at[k],
                recv_sem=recv_sems.at[k], device_id=to, device_id_type=pl.DeviceIdType.MESH)

        mine = pltpu.make_async_copy(src_ref, slot(*me), local_sem)
        mine.start()
        first = [copy(0, me, sibling, src=src_ref)] + [copy(1 + j, me, (*chip, c), src=src_ref) for j, chip in enumerate(chips)]
        for cp in first:
            cp.start()
        passed = [copy(4 + j, (*chip, c), sibling) for j, chip in enumerate(chips)]
        for j, chip in enumerate(chips):
            copy(1 + j, (*chip, c), me).wait_recv()
            passed[j].start()
        copy(0, sibling, me).wait_recv()
        for j, chip in enumerate(chips):
            copy(4 + j, (*chip, 1 - c), me).wait_recv()
        for cp in first + passed:
            cp.wait_send()
        mine.wait()

    return pl.pallas_call(
        body, name=name, out_shape=_exch_shape(src, True),
        in_specs=[pl.BlockSpec(memory_space=pl.ANY)], out_specs=pl.BlockSpec(memory_space=pl.ANY), scratch_shapes=_EXCH_SEMS)(src)


def _mod_dist(c8, ada_w, ada_b_cols, name):
    L, _, AW = ada_w.shape

    def body(c_ref, w_ref, b_ref, parts_ref, sc_ref, call_ref, mine_ref, send1, recv1, send2, recv2):
        x, y, c = lax.axis_index("x"), lax.axis_index("y"), lax.axis_index("c")
        me = 4 * x + 2 * y + c

        def peer(j):
            px = 1 - x if (j >> 2) & 1 else x
            py = 1 - y if (j >> 1) & 1 else y
            pc = 1 - c if j & 1 else c
            return (px, py, pc), 4 * px + 2 * py + pc

        def copy(j, src, dst, sems, landing):
            dev, idx = peer(j)
            return pltpu.make_async_remote_copy(
                src_ref=src, dst_ref=dst.at[idx] if landing else dst.at[me], send_sem=sems[0].at[j - 1], recv_sem=sems[1].at[j - 1],
                device_id=dev, device_id_type=pl.DeviceIdType.MESH)

        def all_to_all(src, dst, sems):
            for j in range(1, NDEV):
                copy(j, src, dst, sems, False).start()
            for j in range(1, NDEV):
                copy(j, src, dst, sems, True).wait_recv()
            for j in range(1, NDEV):
                copy(j, src, dst, sems, False).wait_send()

        call_ref[me] = c_ref[...]
        all_to_all(c_ref, call_ref, (send1, recv1))
        row = lax.broadcasted_iota(jnp.int32, (8, D), 0)
        cm = jnp.zeros((8, D), f32)
        for k in range(NDEV):
            cm = jnp.where(row == k, call_ref[k], cm)
        sc = _silu(cm)
        sc_ref[...] = sc
        for l in range(L):
            mine_ref[l] = _dot(_bf(sc), w_ref[l]) + b_ref[l]
        parts_ref[me] = mine_ref[...]
        all_to_all(mine_ref, parts_ref, (send2, recv2))

    vmem = pl.BlockSpec(memory_space=pltpu.VMEM)
    sems = pltpu.SemaphoreType.DMA((NDEV - 1,))
    return pl.pallas_call(
        body, name=name, in_specs=[vmem, vmem, vmem], out_specs=[vmem, vmem],
        out_shape=[jax.ShapeDtypeStruct((NDEV, L, 8, AW), f32), jax.ShapeDtypeStruct((8, D), f32)],
        scratch_shapes=[pltpu.VMEM((NDEV, 8, D), f32), pltpu.VMEM((L, 8, AW), f32), sems, sems, sems, sems])(c8, ada_w, ada_b_cols)


def _ada_w_grad(sc_all, dmod_cols, name):
    W = dmod_cols.shape[1]

    def body(s_ref, d_ref, o_ref):
        o_ref[...] = lax.dot_general(s_ref[...], d_ref[...], (((0,), (0,)), ((), ())), preferred_element_type=f32,
                                     precision=lax.Precision.HIGHEST)

    return pl.pallas_call(body, name=name, out_shape=jax.ShapeDtypeStruct((D, W), f32))(sc_all, dmod_cols)


def _adamw_math(w, g, m, v):
    m = ADAM_B1 * m + (1.0 - ADAM_B1) * g
    v = ADAM_B2 * v + (1.0 - ADAM_B2) * (g * g)
    m_hat = m / (1.0 - ADAM_B1 ** ADAM_STEP)
    v_hat = v / (1.0 - ADAM_B2 ** ADAM_STEP)
    delta = -ADAM_LR * (m_hat / (jnp.sqrt(v_hat) + ADAM_EPS) + ADAM_WD * w)
    return delta, m, v


def _sum_adamw(recvs, w, m, v, name):
    nl = len(recvs)
    n, R, C = recvs[0].shape
    tr = _tile(R, 128 if C > 1024 else 256)
    nr = R // tr
    stacked = w.ndim == 3
    at = (lambda ref: ref.at[0]) if stacked else (lambda ref: ref)

    def body(*refs):
        r_refs, (w_ref, m_ref, v_ref, g_ref, d_ref, nm_ref, nv_ref) = refs[:nl], [at(r) for r in refs[nl:]]
        for li in range(nl):
            @pl.when(pl.program_id(0) == li)
            def _(r_ref=r_refs[li]):
                g = r_ref[0].astype(f32)
                for k in range(1, n):
                    g = g + r_ref[k].astype(f32)
                g_ref[...] = g
                d_ref[...], nm_ref[...], nv_ref[...] = _adamw_math(w_ref[...], g, m_ref[...], v_ref[...])

    if stacked:
        spec = pl.BlockSpec((1, tr, C), lambda l, i: (l, i, 0))
    else:
        spec = pl.BlockSpec((tr, C), lambda l, i: (l * nr + i, 0))
    r_specs = [pl.BlockSpec((n, tr, C), lambda l, i, li=li: (0, jnp.clip(i + (l - li) * nr, 0, nr - 1), 0)) for li in range(nl)]
    return pl.pallas_call(
        body, name=name, grid=(nl, nr), in_specs=r_specs + [spec, spec, spec], out_specs=[spec] * 4,
        out_shape=[jax.ShapeDtypeStruct(w.shape, f32)] * 4, compiler_params=_params(("arbitrary", "arbitrary")))(*recvs, w, m, v)


def _pack(arrays, rows):
    flat = []
    for a in arrays:
        a = a.reshape(-1).astype(f32)
        flat.append(jnp.pad(a, (0, (-a.shape[0]) % 128)))
    flat = jnp.concatenate(flat)
    return jnp.pad(flat, (0, rows * 128 - flat.shape[0])).reshape(rows, 128)


def _unpack(slab, shapes):
    flat = slab.reshape(-1)
    out, off = [], 0
    for s in shapes:
        n = 1
        for d in s:
            n *= d
        out.append(flat[off:off + n].reshape(s))
        off += n + (-n) % 128
    return out


def kernel(x, c, ada_w, ada_b, norm_w, w_in, gm_ln_w, gm_ln_b, gm_ws, gm_bs, conv_w, conv_b, dt_bias, a_log, d_skip, ssm_norm_w, w_proj_a, w_proj_b, w_out, final_norm_w, loss_target, m_ada_w, m_ada_b, m_norm_w, m_w_in, m_gm_ln_w, m_gm_ln_b, m_gm_ws, m_gm_bs, m_conv_w, m_conv_b, m_dt_bias, m_a_log, m_d_skip, m_ssm_norm_w, m_w_proj_a, m_w_proj_b, m_w_out, m_final_norm_w, v_ada_w, v_ada_b, v_norm_w, v_w_in, v_gm_ln_w, v_gm_ln_b, v_gm_ws, v_gm_bs, v_conv_w, v_conv_b, v_dt_bias, v_a_log, v_d_skip, v_ssm_norm_w, v_w_proj_a, v_w_proj_b, v_w_out, v_final_norm_w):
    L = 2
    me = 4 * lax.axis_index("x") + 2 * lax.axis_index("y") + lax.axis_index("c")
    W = dict(ada_w=ada_w, ada_b=ada_b, norm_w=norm_w, w_in=w_in, gm_ln_w=gm_ln_w, gm_ln_b=gm_ln_b, gm_ws=gm_ws, gm_bs=gm_bs,
             conv_w=conv_w, conv_b=conv_b, dt_bias=dt_bias, a_log=a_log, d_skip=d_skip, ssm_norm_w=ssm_norm_w, w_proj_a=w_proj_a,
             w_proj_b=w_proj_b, w_out=w_out, final_norm_w=final_norm_w)
    M = dict(ada_w=m_ada_w, ada_b=m_ada_b, norm_w=m_norm_w, w_in=m_w_in, gm_ln_w=m_gm_ln_w, gm_ln_b=m_gm_ln_b, gm_ws=m_gm_ws,
             gm_bs=m_gm_bs, conv_w=m_conv_w, conv_b=m_conv_b, dt_bias=m_dt_bias, a_log=m_a_log, d_skip=m_d_skip,
             ssm_norm_w=m_ssm_norm_w, w_proj_a=m_w_proj_a, w_proj_b=m_w_proj_b, w_out=m_w_out, final_norm_w=m_final_norm_w)
    V = dict(ada_w=v_ada_w, ada_b=v_ada_b, norm_w=v_norm_w, w_in=v_w_in, gm_ln_w=v_gm_ln_w, gm_ln_b=v_gm_ln_b, gm_ws=v_gm_ws,
             gm_bs=v_gm_bs, conv_w=v_conv_w, conv_b=v_conv_b, dt_bias=v_dt_bias, a_log=v_a_log, d_skip=v_d_skip,
             ssm_norm_w=v_ssm_norm_w, w_proj_a=v_w_proj_a, w_proj_b=v_w_proj_b, w_out=v_w_out, final_norm_w=v_final_norm_w)
    SW = NIN // NDEV
    AW = 3 * D // NDEV
    CW = CD // NDEV
    RA, RB = D // NDEV, DI // NDEV

    wire = lambda a: a.astype(_WIRE)
    rows_of = lambda d, l: jnp.concatenate([d["w_proj_a"][l], d["w_proj_b"][l], d["w_out"][l]], axis=0)
    g_win0 = _gather2(wire(w_in[0]), "gather_w_in_l0")
    g_cw = _exchange(conv_w.reshape(L * KC, CW), True, "gather_conv_w").reshape(NDEV, L, KC, CW)
    parts, sc_all = _mod_dist(jnp.broadcast_to(c, (8, D)), wire(ada_w),
                              lax.dynamic_slice_in_dim(ada_b, me * AW, AW, axis=1).reshape(L, 1, AW), "mod_dist")
    mods = [lax.dynamic_index_in_dim(parts[:, l], me, axis=1, keepdims=False).reshape(3, D) for l in range(L)]
    cols = lambda g: jnp.transpose(g, (1, 0, 2)).reshape(g.shape[1], -1)

    def prep(l, g_win, g_rows):
        return _prep_layer(norm_w[l], cols(g_win), gm_ln_w[l], gm_ln_b[l], gm_ws[l], gm_bs[l], cols(g_cw[:, l]), conv_b[l],
                           dt_bias[l], a_log[l], d_skip[l], ssm_norm_w[l], g_rows)

    half = D // 2
    rows0, rows1 = wire(rows_of(W, 0)), wire(rows_of(W, 1))
    hr = rows0.shape[0] // 2
    h, sv0, got, lp0 = _layer_fwd(x[0], mods[0], prep(0, g_win0, None), "l0",
                                  dict(proj_gm=[(rows0[:hr], True)], proj_gate=[(rows0[hr:], True)],
                                       proj_ssd=[(wire(w_in[1][:half]), True)],
                                       ssd_fwd=[(wire(w_in[1][half:]), True), (rows1, True)]))
    lp1 = prep(1, jnp.concatenate([got["proj_ssd"][0], got["ssd_fwd"][0]], axis=1), got["ssd_fwd"][1])
    h, sv1, _, _ = _layer_fwd(h, mods[1], lp1, "l1")
    dh, s_fin = _final_loss(h, loss_target[0], final_norm_w.reshape(1, D), "final_loss")
    loss = lax.psum(0.5 / D * s_fin[1, 0], ("x", "y", "c"))

    small_l = ["gm_ln_w", "gm_ln_b", "gm_ws", "gm_bs", "conv_b", "dt_bias", "a_log", "d_skip", "ssm_norm_w"]
    small_key = dict(gm_ln_w="lw", gm_ln_b="lb", gm_ws="ws", gm_bs="bs", conv_b="cb", dt_bias="dtb", a_log="alog", d_skip="dsk",
                     ssm_norm_w="snw")
    shapes_l = [W[n].shape[1:] for n in small_l] + [(KC, CD)]
    shapes_t = [(L, 3 * D), (L, D), (D,)]
    rows_for = lambda sh: -(-sum(-(-functools.reduce(lambda a, b: a * b, s, 1) // 128) for s in sh) // 256) * 256
    n_rows_l, n_rows_t = rows_for(shapes_l), rows_for(shapes_t)
    slab_of = lambda g: _pack([g[small_key[n]] for n in small_l] + [g["cw"]], n_rows_l)

    dh, g1, _, _ = _layer_bwd(dh, sv1, mods[1], lp1, "l1")
    dh, g0, got, got_own = _layer_bwd(dh, sv0, mods[0], lp0, "l0",
                                      [(_win_blocks(g1), False), (_row_blocks(g1), False), (slab_of(g1), True)], slab_of)
    grads = [g0, g1]
    grad_x = dh[None]
    recv_win = [got_own[1], got[0]]
    recv_rows = [got_own[0], got[1]]
    slab_all = [got_own[2], got[2]]
    st = lambda key: jnp.stack([g[key] for g in grads])
    slab_t = _exchange(_pack([st("mod"), st("nw"), s_fin[0]], n_rows_t), True, "gather_small_tail")

    o_win = _sum_adamw(recv_win, w_in, m_w_in, v_w_in, "adamw_w_in")
    rows_all = lambda d: jnp.concatenate([rows_of(d, l) for l in range(L)], axis=0)
    o_rows = [o.reshape(L, 2 * RA + RB, D) for o in _sum_adamw(recv_rows, rows_all(W), rows_all(M), rows_all(V), "adamw_w_rows")]
    o_pa = [o[:, 0:RA] for o in o_rows]
    o_pb = [o[:, RA:RA + RB] for o in o_rows]
    o_po = [o[:, RA + RB:] for o in o_rows]

    def slab_params(d):
        return jnp.concatenate([_pack([d[n][l] for n in small_l] + [jnp.zeros((KC, CD), f32)], n_rows_l) for l in range(L)], axis=0)

    o_small = [[_unpack(o[l * n_rows_l:(l + 1) * n_rows_l], shapes_l) for l in range(L)]
               for o in _sum_adamw(slab_all, slab_params(W), slab_params(M), slab_params(V), "adamw_small")]
    small_out = {n: [jnp.stack([o_small[k][l][i] for l in range(L)]) for k in range(4)] for i, n in enumerate(small_l)}
    tail_params = lambda d: _pack([d["ada_b"], d["norm_w"], d["final_norm_w"]], n_rows_t)
    o_tail = [_unpack(o, shapes_t) for o in _sum_adamw([slab_t], tail_params(W), tail_params(M), tail_params(V), "adamw_small_tail")]
    for i, n in enumerate(["ada_b", "norm_w", "final_norm_w"]):
        small_out[n] = [o_tail[k][i] for k in range(4)]

    g_cw_mine = jnp.stack([lax.dynamic_slice_in_dim(o_small[0][l][len(small_l)], me * CW, CW, axis=1) for l in range(L)])
    o_cw = _sum_adamw([g_cw_mine.reshape(1, L * KC, CW)], conv_w.reshape(L * KC, CW), m_conv_w.reshape(L * KC, CW),
                      v_conv_w.reshape(L * KC, CW), "adamw_conv_w")
    o_cw = [o.reshape(L, KC, CW) for o in o_cw]

    dmod_all = jnp.stack([_unpack(slab_t[k], shapes_t)[0] for k in range(NDEV)])
    dmod_cols = lax.dynamic_slice_in_dim(dmod_all, me * AW, AW, axis=2).reshape(NDEV, L * AW)
    g_ada_w = jnp.transpose(_ada_w_grad(sc_all, dmod_cols, "ada_w_grad").reshape(D, L, AW), (1, 0, 2))
    o_ada = _sum_adamw([g_ada_w[l][None] for l in range(L)], ada_w, m_ada_w, v_ada_w, "adamw_ada_w")

    big = dict(ada_w=o_ada, w_in=o_win, conv_w=o_cw, w_proj_a=o_pa, w_proj_b=o_pb, w_out=o_po)
    order = ["ada_w", "ada_b", "norm_w", "w_in", "gm_ln_w", "gm_ln_b", "gm_ws", "gm_bs", "conv_w", "conv_b", "dt_bias", "a_log",
             "d_skip", "ssm_norm_w", "w_proj_a", "w_proj_b", "w_out", "final_norm_w"]
    pick = lambda n, k: big[n][k] if n in big else small_out[n][k]
    return (loss, grad_x, *[pick(n, 0) for n in order], *[pick(n, 1) for n in order], *[pick(n, 2) for n in order],
            *[pick(n, 3) for n in order])
```

```python
import functools

import jax
import jax.numpy as jnp
from jax import lax
from jax.experimental import pallas as pl
from jax.experimental.pallas import tpu as pltpu

f32 = jnp.float32
_MXU = jnp.bfloat16
_WIRE = jnp.bfloat16

D = 1024
Q = 128
NG = 8
DI = 2048
NH = 32
P = 64
HPG = 4
NS = 128
KC = 4
CD = 4096
GRP = DI // NG
EPS = 1e-6
NDEV = 8
WA = 3 * D
WB = DI + CD + 256
WG = 2 * D
OFF_XBC = DI
OFF_DT = DI + CD
NIN = 11296
VMEM_LIMIT = 56 * 1024 * 1024
NEG = -1e30

ADAM_LR, ADAM_B1, ADAM_B2, ADAM_EPS, ADAM_WD, ADAM_STEP = 0.001, 0.9, 0.999, 1e-08, 0.01, 10


def _bf(x):
    return x.astype(_MXU)


def _dot(a, b):
    return jnp.dot(a, b, preferred_element_type=f32)


def _dot_nt(a, b):
    return lax.dot_general(a, b, (((1,), (1,)), ((), ())), preferred_element_type=f32)


def _dot_tn(a, b):
    return lax.dot_general(a, b, (((0,), (0,)), ((), ())), preferred_element_type=f32)


def _dot_exact(a, b):
    return jnp.dot(a, b, preferred_element_type=f32, precision=lax.Precision.HIGHEST)


def _sigmoid(x):
    return jax.nn.sigmoid(x)


def _silu(x):
    return x * _sigmoid(x)


def _dsilu(x):
    s = _sigmoid(x)
    return s * (1.0 + x * (1.0 - s))


_GK = 0.7978845608028654
_GC = 0.044715


def _gelu_and_grad(x):
    x2, hx = x * x, 0.5 * x
    t = jnp.tanh(x * (_GK + (_GK * _GC) * x2))
    return hx * (1.0 + t), 0.5 * (1.0 + t) + hx * (1.0 - t * t) * (_GK + (3.0 * _GK * _GC) * x2)


def _gelu(x):
    return 0.5 * x * (1.0 + jnp.tanh(x * (_GK + (_GK * _GC) * (x * x))))


def _softplus(x):
    return jnp.maximum(x, 0.0) + jnp.log1p(jnp.exp(-jnp.abs(x)))


def _tile(n, cap):
    if n <= cap:
        return n
    best = None
    for t in range(128, cap + 1, 128):
        if n % t == 0:
            best = t
    assert best is not None, (n, cap)
    return best


def _params(sem):
    return pltpu.CompilerParams(dimension_semantics=sem, vmem_limit_bytes=VMEM_LIMIT)


def _exch_ops(src_ref, out_ref, send_sems, recv_sems, local_sem, gather):
    x, y, c = lax.axis_index("x"), lax.axis_index("y"), lax.axis_index("c")
    me = 4 * x + 2 * y + c

    def peer(j):
        px = 1 - x if (j >> 2) & 1 else x
        py = 1 - y if (j >> 1) & 1 else y
        pc = 1 - c if j & 1 else c
        return (px, py, pc), 4 * px + 2 * py + pc

    def copy(j, landing):
        dev, idx = peer(j)
        return pltpu.make_async_remote_copy(
            src_ref=src_ref if gather else src_ref.at[idx], dst_ref=out_ref.at[idx] if landing else out_ref.at[me],
            send_sem=send_sems.at[j - 1], recv_sem=recv_sems.at[j - 1], device_id=dev, device_id_type=pl.DeviceIdType.MESH)

    mine = pltpu.make_async_copy(src_ref if gather else src_ref.at[me], out_ref.at[me], local_sem)

    def start():
        mine.start()
        for j in range(1, NDEV):
            copy(j, False).start()

    def finish():
        for j in range(1, NDEV):
            copy(j, True).wait_recv()
        for j in range(1, NDEV):
            copy(j, False).wait_send()
        mine.wait()

    return start, finish


def _gather2_ops(src_ref, out_ref, send_sems, recv_sems, local_sem):
    x, y, c = lax.axis_index("x"), lax.axis_index("y"), lax.axis_index("c")
    me, sibling = (x, y, c), (x, y, 1 - c)
    chips = [(1 - x, y), (x, 1 - y), (1 - x, 1 - y)]

    def slot(px, py, pc):
        return out_ref.at[4 * px + 2 * py + pc]

    def copy(k, block, to, src=None):
        return pltpu.make_async_remote_copy(
            src_ref=slot(*block) if src is None else src, dst_ref=slot(*block), send_sem=send_sems.at[k],
            recv_sem=recv_sems.at[k], device_id=to, device_id_type=pl.DeviceIdType.MESH)

    mine = pltpu.make_async_copy(src_ref, slot(*me), local_sem)
    first = [copy(0, me, sibling, src=src_ref)] + [copy(1 + j, me, (*chip, c), src=src_ref) for j, chip in enumerate(chips)]
    passed = [copy(4 + j, (*chip, c), sibling) for j, chip in enumerate(chips)]

    def start():
        mine.start()
        for cp in first:
            cp.start()

    def finish():
        for j, chip in enumerate(chips):
            copy(1 + j, (*chip, c), me).wait_recv()
            passed[j].start()
        copy(0, sibling, me).wait_recv()
        for j, chip in enumerate(chips):
            copy(4 + j, (*chip, 1 - c), me).wait_recv()
        for cp in first + passed:
            cp.wait_send()
        mine.wait()

    return start, finish


def _exch_shape(src, gather):
    return jax.ShapeDtypeStruct((NDEV,) + tuple(src.shape if gather else src.shape[1:]), src.dtype)


_EXCH_SEMS = [pltpu.SemaphoreType.DMA((NDEV - 1,)), pltpu.SemaphoreType.DMA((NDEV - 1,)), pltpu.SemaphoreType.DMA]


def _carry_call(body, name, grid, in_specs, out_specs, out_shape, scratch_shapes, args, carry=()):
    n_in, n_out, n_sc, nx = len(in_specs), len(out_specs), len(scratch_shapes), len(carry)
    sem = ("arbitrary",) * len(grid)
    if nx == 0:
        outs = pl.pallas_call(body, name=name, grid=grid, in_specs=in_specs, out_specs=out_specs, out_shape=out_shape,
                              scratch_shapes=scratch_shapes, compiler_params=_params(sem))(*args)
        return list(outs), []

    def wrapped(*refs):
        ins, srcs = refs[:n_in], refs[n_in:n_in + nx]
        outs, dsts = refs[n_in + nx:n_in + nx + n_out], refs[n_in + nx + n_out:n_in + 2 * nx + n_out]
        scratch, sems = refs[n_in + 2 * nx + n_out:n_in + 2 * nx + n_out + n_sc], refs[n_in + 2 * nx + n_out + n_sc:]
        ops = [_gather2_ops(srcs[i], dsts[i], sems[3 * i], sems[3 * i + 1], sems[3 * i + 2]) if carry[i][1] == "two_level" else
               _exch_ops(srcs[i], dsts[i], sems[3 * i], sems[3 * i + 1], sems[3 * i + 2], carry[i][1]) for i in range(nx)]
        first = functools.reduce(jnp.logical_and, [pl.program_id(d) == 0 for d in range(len(grid))])
        last = functools.reduce(jnp.logical_and, [pl.program_id(d) == grid[d] - 1 for d in range(len(grid))])

        @pl.when(first)
        def _():
            for start, _ in ops:
                start()

        body(*ins, *outs, *scratch)

        @pl.when(last)
        def _():
            for _, finish in ops:
                finish()

    hbm = pl.BlockSpec(memory_space=pl.ANY)
    outs = pl.pallas_call(
        wrapped, name=name, grid=grid, in_specs=list(in_specs) + [hbm] * nx, out_specs=list(out_specs) + [hbm] * nx,
        out_shape=list(out_shape) + [_exch_shape(s, g) for s, g in carry], scratch_shapes=list(scratch_shapes) + _EXCH_SEMS * nx,
        compiler_params=_params(sem))(*args, *[s for s, _ in carry])
    return list(outs[:n_out]), list(outs[n_out:])


def _mm(a, b, mode, name, out_dtype=f32, tm_cap=1024, tn_cap=1280, tk_cap=1280, carry=()):
    if mode == "nn":
        (M, K), (K2, N) = a.shape, b.shape
    elif mode == "nt":
        (M, K), (N, K2) = a.shape, b.shape
    else:
        (K, M), (K2, N) = a.shape, b.shape
        tk_cap = min(tk_cap, 1024)
    assert K == K2, (a.shape, b.shape, mode)
    tm, tn, tk = _tile(M, tm_cap), _tile(N, tn_cap), _tile(K, tk_cap)
    nk = K // tk
    if mode == "nn":
        a_spec = pl.BlockSpec((tm, tk), lambda i, j, k: (i, k))
        b_spec = pl.BlockSpec((tk, tn), lambda i, j, k: (k, j))
        dot = _dot
    elif mode == "nt":
        a_spec = pl.BlockSpec((tm, tk), lambda i, j, k: (i, k))
        b_spec = pl.BlockSpec((tn, tk), lambda i, j, k: (j, k))
        dot = _dot_nt
    else:
        a_spec = pl.BlockSpec((tk, tm), lambda i, j, k: (k, i))
        b_spec = pl.BlockSpec((tk, tn), lambda i, j, k: (k, j))
        dot = _dot_tn
    o_spec = pl.BlockSpec((tm, tn), lambda i, j, k: (i, j))

    def body_one(a_ref, b_ref, o_ref):
        o_ref[...] = dot(_bf(a_ref[...]), _bf(b_ref[...])).astype(out_dtype)

    def body_acc(a_ref, b_ref, o_ref, acc_ref):
        k = pl.program_id(2)

        @pl.when(k == 0)
        def _():
            acc_ref[...] = jnp.zeros_like(acc_ref)

        acc_ref[...] += dot(_bf(a_ref[...]), _bf(b_ref[...]))

        @pl.when(k == nk - 1)
        def _():
            o_ref[...] = acc_ref[...].astype(out_dtype)

    (out,), exchanged = _carry_call(body_one if nk == 1 else body_acc, name, (M // tm, N // tn, nk), [a_spec, b_spec], [o_spec],
                                    [jax.ShapeDtypeStruct((M, N), out_dtype)], [] if nk == 1 else [pltpu.VMEM((tm, tn), f32)],
                                    (a, b), carry)
    return (out, exchanged) if carry else out


def _row_spec(ts, w, col=0):
    return pl.BlockSpec((ts, w), lambda i: (i, col))


def _full_spec(shape):
    nd = len(shape)
    return pl.BlockSpec(shape, lambda i: (0,) * nd)


def _modulate(x, nw, shift, scale, name, carry=()):
    T = x.shape[0]
    ts = _tile(T, 512)

    def body(x_ref, nw_ref, sh_ref, sc_ref, h_ref):
        xv = x_ref[...]
        r = lax.rsqrt(jnp.mean(xv * xv, axis=-1, keepdims=True) + EPS)
        h_ref[...] = ((xv * r * nw_ref[...]) * (1.0 + sc_ref[...]) + sh_ref[...]).astype(h_ref.dtype)

    (h,), exchanged = _carry_call(body, name, (T // ts,), [_row_spec(ts, D), _full_spec((1, D)), _full_spec((1, D)), _full_spec((1, D))],
                                  [_row_spec(ts, D)], [jax.ShapeDtypeStruct((T, D), _MXU)], [], (x, nw, shift, scale), carry)
    return (h, exchanged) if carry else h


def _final_loss(x, tgt, fw, name):
    T = x.shape[0]
    ts = _tile(T, 512)

    def body(x_ref, t_ref, fw_ref, dx_ref, acc_ref):
        @pl.when(pl.program_id(0) == 0)
        def _():
            acc_ref[...] = jnp.zeros_like(acc_ref)

        xv = x_ref[...]
        r = lax.rsqrt(jnp.mean(xv * xv, axis=-1, keepdims=True) + EPS)
        xn = xv * r
        e = xn * fw_ref[...] - t_ref[...]
        dy = e * (1.0 / D)
        acc_ref[0:1, :] += jnp.sum(dy * xn, axis=0, keepdims=True)
        acc_ref[1:2, :] += jnp.sum(jnp.sum(e * e, axis=0, keepdims=True), axis=1, keepdims=True)
        dxn = dy * fw_ref[...]
        dx_ref[...] = r * (dxn - xn * jnp.mean(dxn * xn, axis=-1, keepdims=True))

    return pl.pallas_call(
        body, name=name, grid=(T // ts,),
        in_specs=[_row_spec(ts, D), _row_spec(ts, D), _full_spec((1, D))],
        out_specs=[_row_spec(ts, D), _full_spec((8, D))],
        out_shape=[jax.ShapeDtypeStruct((T, D), f32), jax.ShapeDtypeStruct((8, D), f32)],
        compiler_params=_params(("arbitrary",)))(x, tgt, fw)


def _tril(n):
    return lax.broadcasted_iota(jnp.int32, (n, n), 0) >= lax.broadcasted_iota(jnp.int32, (n, n), 1)


def _gm_chunk_fwd(u, v, z, lw, lb, ws_ref, bsx, gu=None, gv=None):
    gu, gv = _gelu(u) if gu is None else gu, _gelu(v) if gv is None else gv
    mu = jnp.mean(gv, axis=-1, keepdims=True)
    cen = gv - mu
    rstd = lax.rsqrt(jnp.mean(cen * cen, axis=-1, keepdims=True) + EPS)
    vhat = cen * rstd
    vn = _bf(vhat * lw + lb)
    tri = _tril(Q)
    mixed = jnp.concatenate(
        [_dot(_bf(jnp.where(tri, ws_ref[g], 0.0)), vn[:, g * Q:(g + 1) * Q]) for g in range(NG)], axis=1) + bsx
    return gu, vhat, rstd, vn, mixed


def _gmlp_fwd(pA, lw, lb, ws, bsx, name):
    T = pA.shape[0]
    ts = _tile(T, 512)

    def body(u_ref, v_ref, z_ref, lw_ref, lb_ref, ws_ref, bsx_ref, y_ref):
        def chunk(ci, carry):
            rows = pl.ds(pl.multiple_of(ci * Q, Q), Q)
            u, v, z = u_ref[rows, :].astype(f32), v_ref[rows, :].astype(f32), z_ref[rows, :].astype(f32)
            gu, _, _, _, mixed = _gm_chunk_fwd(u, v, z, lw_ref[...], lb_ref[...], ws_ref, bsx_ref[...])
            y_ref[rows, :] = (gu * mixed * _silu(z)).astype(y_ref.dtype)
            return carry

        lax.fori_loop(0, ts // Q, chunk, 0)

    return pl.pallas_call(
        body, name=name, grid=(T // ts,),
        in_specs=[_row_spec(ts, D, 0), _row_spec(ts, D, 1), _row_spec(ts, D, 2), _full_spec((1, D)), _full_spec((1, D)),
                  _full_spec((NG, Q, Q)), _full_spec((Q, D))],
        out_specs=_row_spec(ts, D), out_shape=jax.ShapeDtypeStruct((T, D), _MXU),
        compiler_params=_params(("arbitrary",)))(pA, pA, pA, lw, lb, ws, bsx)


def _gmlp_bwd(dya, pA, lw, lb, ws, bsx, name):
    T = pA.shape[0]
    ts = _tile(T, 512)

    def body(dy_ref, u_ref, v_ref, z_ref, lw_ref, lb_ref, ws_ref, bsx_ref, dp_ref, acc_ref, dws_ref, dbs_ref):
        @pl.when(pl.program_id(0) == 0)
        def _():
            acc_ref[...] = jnp.zeros_like(acc_ref)
            dws_ref[...] = jnp.zeros_like(dws_ref)
            dbs_ref[...] = jnp.zeros_like(dbs_ref)

        tri = _tril(Q)

        def chunk(ci, carry):
            rows = pl.ds(pl.multiple_of(ci * Q, Q), Q)
            u, v, z, dy = u_ref[rows, :].astype(f32), v_ref[rows, :].astype(f32), z_ref[rows, :].astype(f32), dy_ref[rows, :].astype(f32)
            (gu, d_gu), (gv, d_gv) = _gelu_and_grad(u), _gelu_and_grad(v)
            gu, vhat, rstd, vn, mixed = _gm_chunk_fwd(u, v, z, lw_ref[...], lb_ref[...], ws_ref, bsx_ref[...], gu, gv)
            sz = _silu(z)
            dp_ref[rows, 0:D] = (dy * mixed * sz * d_gu).astype(dp_ref.dtype)
            dp_ref[rows, 2 * D:3 * D] = (dy * gu * mixed * _dsilu(z)).astype(dp_ref.dtype)
            dmixed = dy * gu * sz
            dbs_ref[...] += dmixed
            dmb = _bf(dmixed)
            dvn_parts = []
            for g in range(NG):
                cols = slice(g * Q, (g + 1) * Q)
                wg = _bf(jnp.where(tri, ws_ref[g], 0.0))
                dvn_parts.append(_dot_tn(wg, dmb[:, cols]))
                dws_ref[g] += jnp.where(tri, _dot_nt(dmb[:, cols], vn[:, cols]), 0.0)
            dvn = jnp.concatenate(dvn_parts, axis=1)
            acc_ref[0:1, :] += jnp.sum(dvn * vhat, axis=0, keepdims=True)
            acc_ref[1:2, :] += jnp.sum(dvn, axis=0, keepdims=True)
            dvh = dvn * lw_ref[...]
            dgv = rstd * (dvh - jnp.mean(dvh, axis=-1, keepdims=True) - vhat * jnp.mean(dvh * vhat, axis=-1, keepdims=True))
            dp_ref[rows, D:2 * D] = (dgv * d_gv).astype(dp_ref.dtype)
            return carry

        lax.fori_loop(0, ts // Q, chunk, 0)

    return pl.pallas_call(
        body, name=name, grid=(T // ts,),
        in_specs=[_row_spec(ts, D), _row_spec(ts, D, 0), _row_spec(ts, D, 1), _row_spec(ts, D, 2), _full_spec((1, D)),
                  _full_spec((1, D)), _full_spec((NG, Q, Q)), _full_spec((Q, D))],
        out_specs=[_row_spec(ts, WA), _full_spec((8, D)), _full_spec((NG, Q, Q)), _full_spec((Q, D))],
        out_shape=[jax.ShapeDtypeStruct((T, WA), _MXU), jax.ShapeDtypeStruct((8, D), f32),
                   jax.ShapeDtypeStruct((NG, Q, Q), f32), jax.ShapeDtypeStruct((Q, D), f32)],
        compiler_params=_params(("arbitrary",)))(dya, pA, pA, pA, lw, lb, ws, bsx)


def _head_maps():
    h = lax.broadcasted_iota(jnp.int32, (128, DI), 0)
    ch = lax.broadcasted_iota(jnp.int32, (128, DI), 1)
    ex = (ch // P == h).astype(_MXU)
    return ex, ex.T


def _split(v, parts):
    out = []
    for _ in range(parts - 1):
        p = _bf(v)
        out.append(p)
        v = v - p.astype(f32)
    out.append(_bf(v))
    return out


def _expand(v, ex_ref, parts):
    acc = None
    for p in _split(v, parts):
        t = _dot(p, ex_ref[...])
        acc = t if acc is None else acc + t
    return acc


def _reduce(v, rd_ref, parts=2):
    acc = None
    for p in _split(v, parts):
        t = _dot(p, rd_ref[...])
        acc = t if acc is None else acc + t
    return acc


def _ssd_time(pb_ref, dtb_ref, alog_ref):
    xdt = pb_ref[:, OFF_DT:OFF_DT + 128] + dtb_ref[...]
    dt = _softplus(xdt)
    a = -jnp.exp(alog_ref[...])
    cs = _dot_exact(_tril(Q).astype(f32), dt * a)
    return xdt, dt, a, cs


def _head_mask(r):
    return lax.broadcasted_iota(jnp.int32, (Q, GRP), 1) // P == r


def _ssd_group_fwd(g, xa_ref, s_prev, cs, cs_t, dtx_ref, csx_ref, dsk_ref):
    cols = slice(g * GRP, (g + 1) * GRP)
    xs = xa_ref[:, cols]
    bb = _bf(xa_ref[:, DI + g * NS:DI + (g + 1) * NS])
    cb = _bf(xa_ref[:, DI + NG * NS + g * NS:DI + NG * NS + (g + 1) * NS])
    gm = _dot_nt(cb, bb)
    xd = xs * dtx_ref[:, cols]
    csx = csx_ref[:, cols]
    csl = csx_ref[Q - 1:Q, cols]
    tri = _tril(Q)
    lms = [jnp.exp(jnp.where(tri, cs[:, HPG * g + r:HPG * g + r + 1] - cs_t[HPG * g + r:HPG * g + r + 1, :], NEG))
           for r in range(HPG)]
    mfs = [gm * lm for lm in lms]
    mcat = jnp.concatenate([_bf(m) for m in mfs], axis=1)
    xbd = jnp.concatenate([_bf(jnp.where(_head_mask(r), xd, 0.0)) for r in range(HPG)], axis=0)
    ydiag = _dot(mcat, xbd)
    ecs, dte, ecl = jnp.exp(csx), jnp.exp(csl - csx), jnp.exp(csl)
    yoff = ecs * _dot(cb, _bf(s_prev))
    y = ydiag + yoff + xs * dsk_ref[:, cols]
    xdd = xd * dte
    s_new = s_prev * ecl + _dot_tn(bb, _bf(xdd))
    return y, s_new, (xs, bb, cb, xd, lms, mfs, mcat, xbd, yoff, xdd, ecs, dte, ecl)


def _ssd_fwd(pB, cw, cb, dtb, alog, dsk, snw, name, carry=()):
    T = pB.shape[0]
    nc = T // Q
    ex, _ = _head_maps()

    def body(pb_ref, cw_ref, cb_ref, dtb_ref, alog_ref, dsk_ref, snw_ref, ex_ref, y_ref, xc_ref, st_ref,
             s_ref, ext_ref, xa_ref, dtx_ref, csx_ref):
        @pl.when(pl.program_id(0) == 0)
        def _():
            s_ref[...] = jnp.zeros_like(s_ref)
            ext_ref[...] = jnp.zeros_like(ext_ref)

        ext_ref[8:8 + Q, :] = pb_ref[:, OFF_XBC:OFF_XBC + CD]
        for j in range(CD // 512):
            cj = slice(j * 512, (j + 1) * 512)
            e = ext_ref[:, cj]
            xc = cb_ref[:, cj] + cw_ref[KC - 1:KC, cj] * e[8:8 + Q]
            for s in range(1, KC):
                xc = xc + cw_ref[KC - 1 - s:KC - s, cj] * pltpu.roll(e, s, 0)[8:8 + Q]
            xc_ref[:, cj] = xc
            xa_ref[:, cj] = _silu(xc)
        ext_ref[0:8, :] = ext_ref[Q:Q + 8, :]

        _, dt, _, cs = _ssd_time(pb_ref, dtb_ref, alog_ref)
        cs_t = cs.T
        dtx_ref[...] = _expand(dt, ex_ref, 2)
        csx_ref[...] = _expand(cs, ex_ref, 3)
        for g in range(NG):
            s_prev = s_ref[g]
            st_ref[0, g] = s_prev
            y, s_new, _ = _ssd_group_fwd(g, xa_ref, s_prev, cs, cs_t, dtx_ref, csx_ref, dsk_ref)
            s_ref[g] = s_new
            cols = slice(g * GRP, (g + 1) * GRP)
            yz = y * _silu(pb_ref[:, cols])
            rr = lax.rsqrt(jnp.mean(yz * yz, axis=-1, keepdims=True) + EPS)
            y_ref[:, cols] = (yz * rr * snw_ref[:, cols]).astype(y_ref.dtype)

    outs, exchanged = _carry_call(
        body, name, (nc,),
        [_row_spec(Q, WB), _full_spec((8, CD)), _full_spec((1, CD)), _full_spec((1, 128)), _full_spec((1, 128)),
         _full_spec((1, DI)), _full_spec((1, DI)), _full_spec((128, DI))],
        [_row_spec(Q, DI), _row_spec(Q, CD), pl.BlockSpec((1, NG, NS, GRP), lambda i: (i, 0, 0, 0))],
        [jax.ShapeDtypeStruct((T, DI), _MXU), jax.ShapeDtypeStruct((T, CD), f32), jax.ShapeDtypeStruct((nc, NG, NS, GRP), f32)],
        [pltpu.VMEM((NG, NS, GRP), f32), pltpu.VMEM((Q + 8, CD), f32), pltpu.VMEM((Q, CD), f32),
         pltpu.VMEM((Q, DI), f32), pltpu.VMEM((Q, DI), f32)],
        (pB, cw, cb, dtb, alog, dsk, snw, ex), carry)
    return outs + [exchanged]


def _ssd_bwd(dyb, pB, xc, states, cw, dtb, alog, dsk, snw, name, carry=()):
    T = pB.shape[0]
    nc = T // Q
    ex, rd = _head_maps()
    selr = (lax.broadcasted_iota(jnp.int32, (HPG * Q, 128), 0) // Q == lax.broadcasted_iota(jnp.int32, (HPG * Q, 128), 1)).astype(_MXU)

    def body(dy_ref, pb_ref, xc_ref, st_ref, cw_ref, dtb_ref, alog_ref, dsk_ref, snw_ref, ex_ref, rd_ref, selr_ref,
             dp_ref, dcw_ref, dcb_ref, dhd_ref, dcol_ref,
             ds_ref, xa_ref, sg_ref, dxa_ref, dxe_ref, dtx_ref, csx_ref, rcs_ref, rdt_ref, rl_ref, dcs_ref, dcst_ref):
        @pl.when(pl.program_id(0) == 0)
        def _():
            ds_ref[...] = jnp.zeros_like(ds_ref)
            dxe_ref[...] = jnp.zeros_like(dxe_ref)
            dcw_ref[...] = jnp.zeros_like(dcw_ref)
            dcb_ref[...] = jnp.zeros_like(dcb_ref)
            dhd_ref[...] = jnp.zeros_like(dhd_ref)
            dcol_ref[...] = jnp.zeros_like(dcol_ref)
            rl_ref[...] = jnp.zeros_like(rl_ref)
            dcst_ref[...] = jnp.zeros_like(dcst_ref)

        dcs_ref[...] = jnp.zeros_like(dcs_ref)
        for j in range(CD // 512):
            cj = slice(j * 512, (j + 1) * 512)
            sg_ref[:, cj] = _sigmoid(xc_ref[:, cj])
            xa_ref[:, cj] = xc_ref[:, cj] * sg_ref[:, cj]
        xdt, dt, a, cs = _ssd_time(pb_ref, dtb_ref, alog_ref)
        cs_t = cs.T
        dtx_ref[...] = _expand(dt, ex_ref, 2)
        csx_ref[...] = _expand(cs, ex_ref, 3)

        for g in range(NG):
            cols = slice(g * GRP, (g + 1) * GRP)
            s_prev = st_ref[0, g]
            sb = _bf(s_prev)
            y, _, (xs, bb, cbm, xd, lms, mfs, mcat, xbd, yoff, xdd, ecs, dte, ecl) = _ssd_group_fwd(
                g, xa_ref, s_prev, cs, cs_t, dtx_ref, csx_ref, dsk_ref)
            z = pb_ref[:, cols]
            sz = _silu(z)
            yz = y * sz
            rr = lax.rsqrt(jnp.mean(yz * yz, axis=-1, keepdims=True) + EPS)
            nrm = yz * rr
            dyb_g = dy_ref[:, cols].astype(f32)
            dcol_ref[1:2, cols] += jnp.sum(dyb_g * nrm, axis=0, keepdims=True)
            dn = dyb_g * snw_ref[:, cols]
            dyz = rr * (dn - nrm * jnp.mean(dn * nrm, axis=-1, keepdims=True))
            dyv = dyz * sz
            dp_ref[:, cols] = (dyz * y * _dsilu(z)).astype(dp_ref.dtype)
            dcol_ref[0:1, cols] += jnp.sum(dyv * xs, axis=0, keepdims=True)
            dy16 = _bf(dyv)
            dmcat = _dot_nt(dy16, xbd)
            dg = dmcat[:, 0:Q] * lms[0]
            for r in range(1, HPG):
                dg = dg + dmcat[:, r * Q:(r + 1) * Q] * lms[r]
            e16 = _bf(jnp.concatenate([dmcat[:, r * Q:(r + 1) * Q] * mfs[r] for r in range(HPG)], axis=1))
            rows = _dot(e16, selr_ref[...])
            dcs_ref[...] += rows if g == 0 else pltpu.roll(rows, HPG * g, 1)
            csum = _dot(jnp.ones((8, Q), _MXU), e16)
            for r in range(HPG):
                dcst_ref[HPG * g + r:HPG * g + r + 1, :] = csum[0:1, r * Q:(r + 1) * Q]
            big = _dot_tn(mcat, dy16)
            dxd_diag = jnp.where(_head_mask(0), big[0:Q], 0.0)
            for r in range(1, HPG):
                dxd_diag = dxd_diag + jnp.where(_head_mask(r), big[r * Q:(r + 1) * Q], 0.0)
            dsn = ds_ref[g]
            dsn16 = _bf(dsn)
            dxdd = _dot(bb, dsn16)
            t_state = dxdd * xdd
            dxd = dxd_diag + dxdd * dte
            rcs_ref[:, cols] = dyv * yoff - t_state
            rl_ref[0:1, cols] = (jnp.sum(t_state, axis=0, keepdims=True)
                                 + ecl * jnp.sum(s_prev * dsn, axis=0, keepdims=True))
            rdt_ref[:, cols] = dxd * xs
            dxa_ref[:, cols] = dyv * dsk_ref[:, cols] + dxd * dtx_ref[:, cols]
            dg16 = _bf(dg)
            dw16 = _bf(dyv * ecs)
            dxa_ref[:, DI + NG * NS + g * NS:DI + NG * NS + (g + 1) * NS] = _dot(dg16, bb) + _dot_nt(dw16, sb)
            dxa_ref[:, DI + g * NS:DI + (g + 1) * NS] = _dot_tn(dg16, cbm) + _dot_nt(_bf(xdd), dsn16)
            ds_ref[g] = dsn * ecl + _dot_tn(cbm, dw16)

        row = lax.broadcasted_iota(jnp.int32, (Q, 128), 0)
        dcs = (dcs_ref[...] - dcst_ref[...].T + _reduce(rcs_ref[...], rd_ref)
               + jnp.where(row == Q - 1, _reduce(rl_ref[...], rd_ref)[0:1, :], 0.0))
        upper = lax.broadcasted_iota(jnp.int32, (Q, Q), 0) <= lax.broadcasted_iota(jnp.int32, (Q, Q), 1)
        dadt = _dot_exact(upper.astype(f32), dcs)
        valid = lax.broadcasted_iota(jnp.int32, (Q, 128), 1) < NH
        ddt = jnp.where(valid, _reduce(rdt_ref[...], rd_ref, 1) + dadt * a, 0.0)
        ddtr = ddt * _sigmoid(xdt)
        dhd_ref[0:1, :] += jnp.sum(ddtr, axis=0, keepdims=True)
        dhd_ref[1:2, :] += jnp.sum(jnp.where(valid, dadt * dt * a, 0.0), axis=0, keepdims=True)
        dp_ref[:, OFF_DT:OFF_DT + 128] = ddtr.astype(dp_ref.dtype)
        dp_ref[:, OFF_DT + 128:WB] = jnp.zeros((Q, WB - OFF_DT - 128), dp_ref.dtype)
        for j in range(CD // 512):
            cj = slice(j * 512, (j + 1) * 512)
            sg = sg_ref[:, cj]
            dxc = dxa_ref[:, cj] * (sg + xa_ref[:, cj] * (1.0 - sg))
            dxe_ref[0:Q, cj] = dxc
            dcb_ref[0:1, cj] += jnp.sum(dxc, axis=0, keepdims=True)
            raw = pb_ref[:, OFF_XBC + j * 512:OFF_XBC + (j + 1) * 512]
            e = dxe_ref[:, cj]
            dcw_ref[KC - 1:KC, cj] += jnp.sum(dxc * raw, axis=0, keepdims=True)
            dxb = cw_ref[KC - 1:KC, cj] * dxc
            for s in range(1, KC):
                sh = pltpu.roll(e, Q + 8 - s, 0)[0:Q]
                dcw_ref[KC - 1 - s:KC - s, cj] += jnp.sum(sh * raw, axis=0, keepdims=True)
                dxb = dxb + cw_ref[KC - 1 - s:KC - s, cj] * sh
            dp_ref[:, OFF_XBC + j * 512:OFF_XBC + (j + 1) * 512] = dxb.astype(dp_ref.dtype)
        dxe_ref[Q:Q + 8, :] = dxe_ref[0:8, :]

    rev = lambda i: (nc - 1 - i, 0)
    outs, exchanged = _carry_call(
        body, name, (nc,),
        [pl.BlockSpec((Q, DI), rev), pl.BlockSpec((Q, WB), rev), pl.BlockSpec((Q, CD), rev),
         pl.BlockSpec((1, NG, NS, GRP), lambda i: (nc - 1 - i, 0, 0, 0)),
         _full_spec((8, CD)), _full_spec((1, 128)), _full_spec((1, 128)),
         _full_spec((1, DI)), _full_spec((1, DI)), _full_spec((128, DI)), _full_spec((DI, 128)), _full_spec((HPG * Q, 128))],
        [pl.BlockSpec((Q, WB), rev), _full_spec((8, CD)), _full_spec((8, CD)), _full_spec((8, 128)), _full_spec((8, DI))],
        [jax.ShapeDtypeStruct((T, WB), _MXU), jax.ShapeDtypeStruct((8, CD), f32), jax.ShapeDtypeStruct((8, CD), f32),
         jax.ShapeDtypeStruct((8, 128), f32), jax.ShapeDtypeStruct((8, DI), f32)],
        [pltpu.VMEM((NG, NS, GRP), f32), pltpu.VMEM((Q, CD), f32), pltpu.VMEM((Q, CD), f32), pltpu.VMEM((Q, CD), f32),
         pltpu.VMEM((Q + 8, CD), f32), pltpu.VMEM((Q, DI), f32), pltpu.VMEM((Q, DI), f32),
         pltpu.VMEM((Q, DI), f32), pltpu.VMEM((Q, DI), f32), pltpu.VMEM((8, DI), f32),
         pltpu.VMEM((Q, 128), f32), pltpu.VMEM((128, Q), f32)],
        (dyb, pB, xc, states, cw, dtb, alog, dsk, snw, ex, rd, selr), carry)
    return outs + [exchanged]


def _merge_fwd(ya, yb, pG, x, gate, wa, wb, wo, name, carry=()):
    T = x.shape[0]
    ts = _tile(T, 256)

    def body(ya_ref, yb_ref, g_ref, x_ref, gate_ref, wa_ref, wb_ref, wo_ref, xo_ref, mg_ref, pa_ref, pb_ref):
        pa = _dot(ya_ref[...], wa_ref[...])
        pb = _dot(yb_ref[...], wb_ref[...])
        merged = _sigmoid(g_ref[:, 0:D].astype(f32)) * pa + _sigmoid(g_ref[:, D:2 * D].astype(f32)) * pb
        mg = _bf(merged)
        xo_ref[...] = x_ref[...] + gate_ref[...] * _dot(mg, wo_ref[...])
        mg_ref[...] = mg
        pa_ref[...] = pa.astype(pa_ref.dtype)
        pb_ref[...] = pb.astype(pb_ref.dtype)

    outs, exchanged = _carry_call(
        body, name, (T // ts,),
        [_row_spec(ts, D), _row_spec(ts, DI), _row_spec(ts, WG), _row_spec(ts, D), _full_spec((1, D)),
         _full_spec((D, D)), _full_spec((DI, D)), _full_spec((D, D))],
        [_row_spec(ts, D)] * 4,
        [jax.ShapeDtypeStruct((T, D), f32), jax.ShapeDtypeStruct((T, D), _MXU), jax.ShapeDtypeStruct((T, D), _MXU),
         jax.ShapeDtypeStruct((T, D), _MXU)],
        [], (ya, yb, pG, x, gate, wa, wb, wo), carry)
    return outs + [exchanged]


def _merge_bwd(dxout, merged, pa, pb, pG, gate, wo, name):
    T = dxout.shape[0]
    ts = _tile(T, 256)

    def body(dx_ref, mg_ref, pa_ref, pb_ref, g_ref, gate_ref, wo_ref, do_ref, dpa_ref, dpb_ref, dg_ref, acc_ref):
        @pl.when(pl.program_id(0) == 0)
        def _():
            acc_ref[...] = jnp.zeros_like(acc_ref)

        dxo = dx_ref[...]
        acc_ref[0:1, :] += jnp.sum(dxo * _dot(mg_ref[...], wo_ref[...]), axis=0, keepdims=True)
        do = _bf(dxo * gate_ref[...])
        do_ref[...] = do
        dmerged = _dot_nt(do, wo_ref[...])
        sa, sb = _sigmoid(g_ref[:, 0:D].astype(f32)), _sigmoid(g_ref[:, D:2 * D].astype(f32))
        dpa_ref[...] = (dmerged * sa).astype(dpa_ref.dtype)
        dpb_ref[...] = (dmerged * sb).astype(dpb_ref.dtype)
        dg_ref[:, 0:D] = (dmerged * pa_ref[...].astype(f32) * sa * (1.0 - sa)).astype(dg_ref.dtype)
        dg_ref[:, D:2 * D] = (dmerged * pb_ref[...].astype(f32) * sb * (1.0 - sb)).astype(dg_ref.dtype)

    return pl.pallas_call(
        body, name=name, grid=(T // ts,),
        in_specs=[_row_spec(ts, D), _row_spec(ts, D), _row_spec(ts, D), _row_spec(ts, D), _row_spec(ts, WG),
                  _full_spec((1, D)), _full_spec((D, D))],
        out_specs=[_row_spec(ts, D), _row_spec(ts, D), _row_spec(ts, D), _row_spec(ts, WG), _full_spec((8, D))],
        out_shape=[jax.ShapeDtypeStruct((T, D), _MXU), jax.ShapeDtypeStruct((T, D), _MXU), jax.ShapeDtypeStruct((T, D), _MXU),
                   jax.ShapeDtypeStruct((T, WG), _MXU), jax.ShapeDtypeStruct((8, D), f32)],
        compiler_params=_params(("arbitrary",)))(dxout, merged, pa, pb, pG, gate, wo)


def _rows_split(r):
    ra, rb = D // NDEV, DI // NDEV
    return r[:, 0:ra].reshape(D, D), r[:, ra:ra + rb].reshape(DI, D), r[:, ra + rb:].reshape(D, D)


def _layer_fwd(x, mod, lp, tag, carry=None, h=None):
    carry = carry or {}
    got = {}

    def mm(key, w, **kw):
        if carry.get(key):
            out, got[key] = _mm(h, w, "nn", f"{key}_{tag}", carry=carry[key], **kw)
            return out
        return _mm(h, w, "nn", f"{key}_{tag}", **kw)

    if h is None:
        h = _modulate(x, lp["nw"], mod[0:1], mod[1:2], f"modulate_{tag}")
    pA = mm("proj_gm", lp["w_gm"], out_dtype=_MXU)
    pB = mm("proj_ssd", lp["w_ssd"])
    pG = mm("proj_gate", lp["w_g"], out_dtype=_MXU)
    if "wa" not in lp:
        lp = dict(lp)
        lp["wa"], lp["wb"], lp["wo"] = _rows_split(jnp.concatenate([got["proj_gm"][0], got["proj_gate"][0]], axis=1))
    ya = _gmlp_fwd(pA, lp["lw"], lp["lb"], lp["ws"], lp["bsx"], f"gmlp_fwd_{tag}")
    yb, xc, states, got["ssd_fwd"] = _ssd_fwd(pB, lp["cw"], lp["cb"], lp["dtb"], lp["alog"], lp["dsk"], lp["snw"], f"ssd_fwd_{tag}",
                                              carry.get("ssd_fwd", ()))
    xo, merged, pa, pb, _ = _merge_fwd(ya, yb, pG, x, mod[2:3], lp["wa"], lp["wb"], lp["wo"], f"merge_fwd_{tag}")
    return xo, dict(x=x, h=h, pA=pA, pB=pB, pG=pG, ya=ya, yb=yb, xc=xc, states=states, merged=merged, pa=pa, pb=pb), got, lp


def _dh_modulate_bwd(dpG, dpA, dpB, w_g, w_gm, w_ssd, dxout, x, nw, scale, name, carry=()):
    T = x.shape[0]
    tm = _tile(T, 1024)
    fam = [(WG, _tile(WG, 1024)), (WA, _tile(WA, 1024)), (WB, _tile(WB, 1280))]
    steps = [w // k for w, k in fam]
    first = [0, steps[0], steps[0] + steps[1]]
    ns = sum(steps)

    def chunk(f):
        return lambda i, s: jnp.clip(s - first[f], 0, steps[f] - 1)

    a_specs = [pl.BlockSpec((tm, fam[f][1]), lambda i, s, c=chunk(f): (i, c(i, s))) for f in range(3)]
    b_specs = [pl.BlockSpec((D, fam[f][1]), lambda i, s, c=chunk(f): (0, c(i, s))) for f in range(3)]
    tok = pl.BlockSpec((tm, D), lambda i, s: (i, 0))
    tok_once = pl.BlockSpec((tm, D), lambda i, s: (i, 0), pipeline_mode=pl.Buffered(1))
    vec = pl.BlockSpec((1, D), lambda i, s: (0, 0))

    def body(ag, aa, ab, bg, ba, bb, dxo_ref, x_ref, nw_ref, sc_ref, dx_ref, sum_ref, acc_ref):
        i, s = pl.program_id(0), pl.program_id(1)

        @pl.when(jnp.logical_and(i == 0, s == 0))
        def _():
            sum_ref[...] = jnp.zeros_like(sum_ref)

        @pl.when(s == 0)
        def _():
            acc_ref[...] = jnp.zeros_like(acc_ref)

        @pl.when(s < first[1])
        def _():
            acc_ref[...] += _dot_nt(ag[...], bg[...])

        @pl.when(jnp.logical_and(s >= first[1], s < first[2]))
        def _():
            acc_ref[...] += _dot_nt(aa[...], ba[...])

        @pl.when(s >= first[2])
        def _():
            acc_ref[...] += _dot_nt(ab[...], bb[...])

        @pl.when(s == ns - 1)
        def _():
            xv, dh_v = x_ref[...], acc_ref[...]
            r = lax.rsqrt(jnp.mean(xv * xv, axis=-1, keepdims=True) + EPS)
            xn = xv * r
            hn = xn * nw_ref[...]
            dhn = dh_v * (1.0 + sc_ref[...])
            sum_ref[0:1, :] += jnp.sum(dh_v, axis=0, keepdims=True)
            sum_ref[1:2, :] += jnp.sum(dh_v * hn, axis=0, keepdims=True)
            sum_ref[2:3, :] += jnp.sum(dhn * xn, axis=0, keepdims=True)
            dxn = dhn * nw_ref[...]
            dx_ref[...] = dxo_ref[...] + r * (dxn - xn * jnp.mean(dxn * xn, axis=-1, keepdims=True))

    outs, exchanged = _carry_call(
        body, name, (T // tm, ns), a_specs + b_specs + [tok_once, tok_once, vec, vec], [tok, pl.BlockSpec((8, D), lambda i, s: (0, 0))],
        [jax.ShapeDtypeStruct((T, D), f32), jax.ShapeDtypeStruct((8, D), f32)], [pltpu.VMEM((tm, D), f32)],
        (dpG, dpA, dpB, w_g, w_gm, w_ssd, dxout, x, nw, scale), carry)
    return outs + [exchanged]


def _win_blocks(g):
    full = jnp.concatenate([g["w_gm"], g["w_ssd"][:, 0:DI + CD], g["w_ssd"][:, DI + CD:DI + CD + NH], g["w_g"]], axis=1)
    return jnp.transpose(full.reshape(D, NDEV, NIN // NDEV), (1, 0, 2))


def _row_blocks(g):
    return jnp.concatenate([g["wa"].reshape(NDEV, D // NDEV, D), g["wb"].reshape(NDEV, DI // NDEV, D),
                            g["wo"].reshape(NDEV, D // NDEV, D)], axis=1)


def _layer_bwd(dxo, sv, mod, lp, tag, carry=(), scatter_own=None):
    do, dpa, dpb, dpG, s_gate = _merge_bwd(dxo, sv["merged"], sv["pa"], sv["pb"], sv["pG"], mod[2:3], lp["wo"], f"merge_bwd_{tag}")
    g = {}
    g["wo"] = _mm(sv["merged"], do, "tn", f"dw_out_{tag}", out_dtype=_WIRE)
    g["wa"] = _mm(sv["ya"], dpa, "tn", f"dw_proj_a_{tag}", out_dtype=_WIRE)
    g["wb"] = _mm(sv["yb"], dpb, "tn", f"dw_proj_b_{tag}", out_dtype=_WIRE)
    dya = _mm(dpa, lp["wa"], "nt", f"dy_a_{tag}", out_dtype=_MXU)
    dyb = _mm(dpb, lp["wb"], "nt", f"dy_b_{tag}", out_dtype=_MXU)
    dpA, s_ln, g["ws"], dbsx = _gmlp_bwd(dya, sv["pA"], lp["lw"], lp["lb"], lp["ws"], lp["bsx"], f"gmlp_bwd_{tag}")
    carry = list(carry) + ([(_row_blocks(g), False)] if scatter_own else [])
    dpB, g["cw"], s_cb, s_hd, s_col, got = _ssd_bwd(dyb, sv["pB"], sv["xc"], sv["states"], lp["cw"], lp["dtb"], lp["alog"],
                                                    lp["dsk"], lp["snw"], f"ssd_bwd_{tag}", carry)
    g["lw"], g["lb"] = s_ln[0], s_ln[1]
    g["bs"] = dbsx.reshape(Q, NG, Q).sum(-1).T
    g["cb"] = s_cb[0]
    g["dtb"], g["alog"] = s_hd[0, :NH], s_hd[1, :NH]
    g["dsk"] = s_col[0].reshape(NH, P).sum(-1)
    g["snw"] = s_col[1]
    g["cw"] = g["cw"][0:KC]
    g["w_g"] = _mm(sv["h"], dpG, "tn", f"dw_gate_{tag}", out_dtype=_WIRE)
    g["w_gm"] = _mm(sv["h"], dpA, "tn", f"dw_gm_{tag}", out_dtype=_WIRE)
    if scatter_own:
        g["w_ssd"], got_slab = _mm(sv["h"], dpB, "tn", f"dw_ssd_{tag}", out_dtype=_WIRE, carry=[(scatter_own(g), True)])
    else:
        g["w_ssd"] = _mm(sv["h"], dpB, "tn", f"dw_ssd_{tag}", out_dtype=_WIRE)
    dx, s_mod, got_dh = _dh_modulate_bwd(dpG, dpA, dpB, lp["w_g"], lp["w_gm"], lp["w_ssd"], dxo, sv["x"], lp["nw"], mod[1:2],
                                         f"dh_{tag}", [(_win_blocks(g), False)] if scatter_own else ())
    if scatter_own:
        got, got_own = got[:-1], [got[-1], got_dh[0], got_slab[0]]
    else:
        got_own = []
    g["mod"] = jnp.concatenate([s_mod[0], s_mod[1], s_gate[0]])
    g["nw"] = s_mod[2]
    return dx, g, got, got_own


def _prep_layer(nw, w_in_full, lw, lb, ws, bs, cw_full, cb, dtb, alog, dsk, snw, rows=None):
    z = jnp.zeros((D, WB - (DI + CD + NH)), w_in_full.dtype)
    pad_h = lambda v: jnp.pad(v, (0, 128 - NH)).reshape(1, 128)
    extra = dict(zip(("wa", "wb", "wo"), _rows_split(rows))) if rows is not None else {}
    return dict(
        **extra,
        nw=nw.reshape(1, D),
        w_gm=w_in_full[:, 0:WA],
        w_ssd=jnp.concatenate([w_in_full[:, WA:WA + DI + CD], w_in_full[:, WA + DI + CD:WA + DI + CD + NH], z], axis=1),
        w_g=w_in_full[:, WA + DI + CD + NH:NIN],
        lw=lw.reshape(1, D), lb=lb.reshape(1, D), ws=ws, bsx=jnp.repeat(bs.T, Q, axis=1),
        cw=jnp.pad(cw_full, ((0, 8 - KC), (0, 0))), cb=cb.reshape(1, CD), dtb=pad_h(dtb), alog=pad_h(alog),
        dsk=jnp.repeat(dsk, P).reshape(1, DI), snw=snw.reshape(1, DI))


def _exchange(src, gather, name):
    if not gather:
        assert src.shape[0] == NDEV

    def body(src_ref, out_ref, send_sems, recv_sems, local_sem):
        start, finish = _exch_ops(src_ref, out_ref, send_sems, recv_sems, local_sem, gather)
        start()
        finish()

    return pl.pallas_call(
        body, name=name, out_shape=_exch_shape(src, gather),
        in_specs=[pl.BlockSpec(memory_space=pl.ANY)], out_specs=pl.BlockSpec(memory_space=pl.ANY), scratch_shapes=_EXCH_SEMS)(src)


def _gather2(src, name):
    def body(src_ref, out_ref, send_sems, recv_sems, local_sem):
        start, finish = _gather2_ops(src_ref, out_ref, send_sems, recv_sems, local_sem)
        start()
        finish()

    return pl.pallas_call(
        body, name=name, out_shape=_exch_shape(src, True),
        in_specs=[pl.BlockSpec(memory_space=pl.ANY)], out_specs=pl.BlockSpec(memory_space=pl.ANY), scratch_shapes=_EXCH_SEMS)(src)


def _mod_dist(c8, ada_w, ada_b_cols, name):
    L, _, AW = ada_w.shape

    def body(c_ref, w_ref, b_ref, parts_ref, sc_ref, call_ref, mine_ref, send1, recv1, send2, recv2):
        x, y, c = lax.axis_index("x"), lax.axis_index("y"), lax.axis_index("c")
        me = 4 * x + 2 * y + c

        def peer(j):
            px = 1 - x if (j >> 2) & 1 else x
            py = 1 - y if (j >> 1) & 1 else y
            pc = 1 - c if j & 1 else c
            return (px, py, pc), 4 * px + 2 * py + pc

        def copy(j, src, dst, sems, landing):
            dev, idx = peer(j)
            return pltpu.make_async_remote_copy(
                src_ref=src, dst_ref=dst.at[idx] if landing else dst.at[me], send_sem=sems[0].at[j - 1], recv_sem=sems[1].at[j - 1],
                device_id=dev, device_id_type=pl.DeviceIdType.MESH)

        def all_to_all(src, dst, sems):
            for j in range(1, NDEV):
                copy(j, src, dst, sems, False).start()
            for j in range(1, NDEV):
                copy(j, src, dst, sems, True).wait_recv()
            for j in range(1, NDEV):
                copy(j, src, dst, sems, False).wait_send()

        call_ref[me] = c_ref[...]
        all_to_all(c_ref, call_ref, (send1, recv1))
        row = lax.broadcasted_iota(jnp.int32, (8, D), 0)
        cm = jnp.zeros((8, D), f32)
        for k in range(NDEV):
            cm = jnp.where(row == k, call_ref[k], cm)
        sc = _silu(cm)
        sc_ref[...] = sc
        for l in range(L):
            mine_ref[l] = _dot(_bf(sc), w_ref[l]) + b_ref[l]
        parts_ref[me] = mine_ref[...]
        all_to_all(mine_ref, parts_ref, (send2, recv2))

    vmem = pl.BlockSpec(memory_space=pltpu.VMEM)
    sems = pltpu.SemaphoreType.DMA((NDEV - 1,))
    return pl.pallas_call(
        body, name=name, in_specs=[vmem, vmem, vmem], out_specs=[vmem, vmem],
        out_shape=[jax.ShapeDtypeStruct((NDEV, L, 8, AW), f32), jax.ShapeDtypeStruct((8, D), f32)],
        scratch_shapes=[pltpu.VMEM((NDEV, 8, D), f32), pltpu.VMEM((L, 8, AW), f32), sems, sems, sems, sems])(c8, ada_w, ada_b_cols)


def _ada_w_grad(sc_all, dmod_cols, name):
    W = dmod_cols.shape[1]

    def body(s_ref, d_ref, o_ref):
        o_ref[...] = lax.dot_general(s_ref[...], d_ref[...], (((0,), (0,)), ((), ())), preferred_element_type=f32,
                                     precision=lax.Precision.HIGHEST)

    return pl.pallas_call(body, name=name, out_shape=jax.ShapeDtypeStruct((D, W), f32))(sc_all, dmod_cols)


def _adamw_math(w, g, m, v):
    m = ADAM_B1 * m + (1.0 - ADAM_B1) * g
    v = ADAM_B2 * v + (1.0 - ADAM_B2) * (g * g)
    m_hat = m / (1.0 - ADAM_B1 ** ADAM_STEP)
    v_hat = v / (1.0 - ADAM_B2 ** ADAM_STEP)
    delta = -ADAM_LR * (m_hat / (jnp.sqrt(v_hat) + ADAM_EPS) + ADAM_WD * w)
    return delta, m, v


def _sum_adamw(recvs, w, m, v, name):
    nl = len(recvs)
    n, R, C = recvs[0].shape
    tr = _tile(R, 128 if C > 1024 else 256)
    nr = R // tr
    stacked = w.ndim == 3
    at = (lambda ref: ref.at[0]) if stacked else (lambda ref: ref)

    def body(*refs):
        r_refs, (w_ref, m_ref, v_ref, g_ref, d_ref, nm_ref, nv_ref) = refs[:nl], [at(r) for r in refs[nl:]]
        for li in range(nl):
            @pl.when(pl.program_id(0) == li)
            def _(r_ref=r_refs[li]):
                g = r_ref[0].astype(f32)
                for k in range(1, n):
                    g = g + r_ref[k].astype(f32)
                g_ref[...] = g
                d_ref[...], nm_ref[...], nv_ref[...] = _adamw_math(w_ref[...], g, m_ref[...], v_ref[...])

    if stacked:
        spec = pl.BlockSpec((1, tr, C), lambda l, i: (l, i, 0))
    else:
        spec = pl.BlockSpec((tr, C), lambda l, i: (l * nr + i, 0))
    r_specs = [pl.BlockSpec((n, tr, C), lambda l, i, li=li: (0, jnp.clip(i + (l - li) * nr, 0, nr - 1), 0)) for li in range(nl)]
    return pl.pallas_call(
        body, name=name, grid=(nl, nr), in_specs=r_specs + [spec, spec, spec], out_specs=[spec] * 4,
        out_shape=[jax.ShapeDtypeStruct(w.shape, f32)] * 4, compiler_params=_params(("arbitrary", "arbitrary")))(*recvs, w, m, v)


def _pack(arrays, rows):
    flat = []
    for a in arrays:
        a = a.reshape(-1).astype(f32)
        flat.append(jnp.pad(a, (0, (-a.shape[0]) % 128)))
    flat = jnp.concatenate(flat)
    return jnp.pad(flat, (0, rows * 128 - flat.shape[0])).reshape(rows, 128)


def _unpack(slab, shapes):
    flat = slab.reshape(-1)
    out, off = [], 0
    for s in shapes:
        n = 1
        for d in s:
            n *= d
        out.append(flat[off:off + n].reshape(s))
        off += n + (-n) % 128
    return out


def kernel(x, c, ada_w, ada_b, norm_w, w_in, gm_ln_w, gm_ln_b, gm_ws, gm_bs, conv_w, conv_b, dt_bias, a_log, d_skip, ssm_norm_w, w_proj_a, w_proj_b, w_out, final_norm_w, loss_target, m_ada_w, m_ada_b, m_norm_w, m_w_in, m_gm_ln_w, m_gm_ln_b, m_gm_ws, m_gm_bs, m_conv_w, m_conv_b, m_dt_bias, m_a_log, m_d_skip, m_ssm_norm_w, m_w_proj_a, m_w_proj_b, m_w_out, m_final_norm_w, v_ada_w, v_ada_b, v_norm_w, v_w_in, v_gm_ln_w, v_gm_ln_b, v_gm_ws, v_gm_bs, v_conv_w, v_conv_b, v_dt_bias, v_a_log, v_d_skip, v_ssm_norm_w, v_w_proj_a, v_w_proj_b, v_w_out, v_final_norm_w):
    L = 2
    me = 4 * lax.axis_index("x") + 2 * lax.axis_index("y") + lax.axis_index("c")
    W = dict(ada_w=ada_w, ada_b=ada_b, norm_w=norm_w, w_in=w_in, gm_ln_w=gm_ln_w, gm_ln_b=gm_ln_b, gm_ws=gm_ws, gm_bs=gm_bs,
             conv_w=conv_w, conv_b=conv_b, dt_bias=dt_bias, a_log=a_log, d_skip=d_skip, ssm_norm_w=ssm_norm_w, w_proj_a=w_proj_a,
             w_proj_b=w_proj_b, w_out=w_out, final_norm_w=final_norm_w)
    M = dict(ada_w=m_ada_w, ada_b=m_ada_b, norm_w=m_norm_w, w_in=m_w_in, gm_ln_w=m_gm_ln_w, gm_ln_b=m_gm_ln_b, gm_ws=m_gm_ws,
             gm_bs=m_gm_bs, conv_w=m_conv_w, conv_b=m_conv_b, dt_bias=m_dt_bias, a_log=m_a_log, d_skip=m_d_skip,
             ssm_norm_w=m_ssm_norm_w, w_proj_a=m_w_proj_a, w_proj_b=m_w_proj_b, w_out=m_w_out, final_norm_w=m_final_norm_w)
    V = dict(ada_w=v_ada_w, ada_b=v_ada_b, norm_w=v_norm_w, w_in=v_w_in, gm_ln_w=v_gm_ln_w, gm_ln_b=v_gm_ln_b, gm_ws=v_gm_ws,
             gm_bs=v_gm_bs, conv_w=v_conv_w, conv_b=v_conv_b, dt_bias=v_dt_bias, a_log=v_a_log, d_skip=v_d_skip,
             ssm_norm_w=v_ssm_norm_w, w_proj_a=v_w_proj_a, w_proj_b=v_w_proj_b, w_out=v_w_out, final_norm_w=v_final_norm_w)
    SW = NIN // NDEV
    AW = 3 * D // NDEV
    CW = CD // NDEV
    RA, RB = D // NDEV, DI // NDEV

    wire = lambda a: a.astype(_WIRE)
    rows_of = lambda d, l: jnp.concatenate([d["w_proj_a"][l], d["w_proj_b"][l], d["w_out"][l]], axis=0)
    g_cw = _exchange(conv_w.reshape(L * KC, CW), True, "gather_conv_w").reshape(NDEV, L, KC, CW)
    parts, sc_all = _mod_dist(jnp.broadcast_to(c, (8, D)), wire(ada_w),
                              lax.dynamic_slice_in_dim(ada_b, me * AW, AW, axis=1).reshape(L, 1, AW), "mod_dist")
    mods = [lax.dynamic_index_in_dim(parts[:, l], me, axis=1, keepdims=False).reshape(3, D) for l in range(L)]
    cols = lambda g: jnp.transpose(g, (1, 0, 2)).reshape(g.shape[1], -1)

    def prep(l, g_win, g_rows):
        return _prep_layer(norm_w[l], cols(g_win), gm_ln_w[l], gm_ln_b[l], gm_ws[l], gm_bs[l], cols(g_cw[:, l]), conv_b[l],
                           dt_bias[l], a_log[l], d_skip[l], ssm_norm_w[l], g_rows)

    half = D // 2
    rows0, rows1 = wire(rows_of(W, 0)), wire(rows_of(W, 1))
    hr = rows0.shape[0] // 2
    h0, (g_win0,) = _modulate(x[0], norm_w[0].reshape(1, D), mods[0][0:1], mods[0][1:2], "modulate_l0",
                              [(wire(w_in[0]), "two_level")])
    h, sv0, got, lp0 = _layer_fwd(x[0], mods[0], prep(0, g_win0, None), "l0",
                                  dict(proj_gm=[(rows0[:hr], True)], proj_gate=[(rows0[hr:], True)],
                                       proj_ssd=[(wire(w_in[1][:half]), True)],
                                       ssd_fwd=[(wire(w_in[1][half:]), True), (rows1, True)]), h0)
    lp1 = prep(1, jnp.concatenate([got["proj_ssd"][0], got["ssd_fwd"][0]], axis=1), got["ssd_fwd"][1])
    h, sv1, _, _ = _layer_fwd(h, mods[1], lp1, "l1")
    dh, s_fin = _final_loss(h, loss_target[0], final_norm_w.reshape(1, D), "final_loss")
    loss = lax.psum(0.5 / D * s_fin[1, 0], ("x", "y", "c"))

    small_l = ["gm_ln_w", "gm_ln_b", "gm_ws", "gm_bs", "conv_b", "dt_bias", "a_log", "d_skip", "ssm_norm_w"]
    small_key = dict(gm_ln_w="lw", gm_ln_b="lb", gm_ws="ws", gm_bs="bs", conv_b="cb", dt_bias="dtb", a_log="alog", d_skip="dsk",
                     ssm_norm_w="snw")
    shapes_l = [W[n].shape[1:] for n in small_l] + [(KC, CD)]
    shapes_t = [(L, 3 * D), (L, D), (D,)]
    rows_for = lambda sh: -(-sum(-(-functools.reduce(lambda a, b: a * b, s, 1) // 128) for s in sh) // 256) * 256
    n_rows_l, n_rows_t = rows_for(shapes_l), rows_for(shapes_t)
    slab_of = lambda g: _pack([g[small_key[n]] for n in small_l] + [g["cw"]], n_rows_l)

    dh, g1, _, _ = _layer_bwd(dh, sv1, mods[1], lp1, "l1")
    dh, g0, got, got_own = _layer_bwd(dh, sv0, mods[0], lp0, "l0",
                                      [(_win_blocks(g1), False), (_row_blocks(g1), False), (slab_of(g1), True)], slab_of)
    grads = [g0, g1]
    grad_x = dh[None]
    recv_win = [got_own[1], got[0]]
    recv_rows = [got_own[0], got[1]]
    slab_all = [got_own[2], got[2]]
    st = lambda key: jnp.stack([g[key] for g in grads])
    slab_t = _exchange(_pack([st("mod"), st("nw"), s_fin[0]], n_rows_t), True, "gather_small_tail")

    o_win = _sum_adamw(recv_win, w_in, m_w_in, v_w_in, "adamw_w_in")
    rows_all = lambda d: jnp.concatenate([rows_of(d, l) for l in range(L)], axis=0)
    o_rows = [o.reshape(L, 2 * RA + RB, D) for o in _sum_adamw(recv_rows, rows_all(W), rows_all(M), rows_all(V), "adamw_w_rows")]
    o_pa = [o[:, 0:RA] for o in o_rows]
    o_pb = [o[:, RA:RA + RB] for o in o_rows]
    o_po = [o[:, RA + RB:] for o in o_rows]

    def slab_params(d):
        return jnp.concatenate([_pack([d[n][l] for n in small_l] + [jnp.zeros((KC, CD), f32)], n_rows_l) for l in range(L)], axis=0)

    o_small = [[_unpack(o[l * n_rows_l:(l + 1) * n_rows_l], shapes_l) for l in range(L)]
               for o in _sum_adamw(slab_all, slab_params(W), slab_params(M), slab_params(V), "adamw_small")]
    small_out = {n: [jnp.stack([o_small[k][l][i] for l in range(L)]) for k in range(4)] for i, n in enumerate(small_l)}
    tail_params = lambda d: _pack([d["ada_b"], d["norm_w"], d["final_norm_w"]], n_rows_t)
    o_tail = [_unpack(o, shapes_t) for o in _sum_adamw([slab_t], tail_params(W), tail_params(M), tail_params(V), "adamw_small_tail")]
    for i, n in enumerate(["ada_b", "norm_w", "final_norm_w"]):
        small_out[n] = [o_tail[k][i] for k in range(4)]

    g_cw_mine = jnp.stack([lax.dynamic_slice_in_dim(o_small[0][l][len(small_l)], me * CW, CW, axis=1) for l in range(L)])
    o_cw = _sum_adamw([g_cw_mine.reshape(1, L * KC, CW)], conv_w.reshape(L * KC, CW), m_conv_w.reshape(L * KC, CW),
                      v_conv_w.reshape(L * KC, CW), "adamw_conv_w")
    o_cw = [o.reshape(L, KC, CW) for o in o_cw]

    dmod_all = jnp.stack([_unpack(slab_t[k], shapes_t)[0] for k in range(NDEV)])
    dmod_cols = lax.dynamic_slice_in_dim(dmod_all, me * AW, AW, axis=2).reshape(NDEV, L * AW)
    g_ada_w = jnp.transpose(_ada_w_grad(sc_all, dmod_cols, "ada_w_grad").reshape(D, L, AW), (1, 0, 2))
    o_ada = _sum_adamw([g_ada_w[l][None] for l in range(L)], ada_w, m_ada_w, v_ada_w, "adamw_ada_w")

    big = dict(ada_w=o_ada, w_in=o_win, conv_w=o_cw, w_proj_a=o_pa, w_proj_b=o_pb, w_out=o_po)
    order = ["ada_w", "ada_b", "norm_w", "w_in", "gm_ln_w", "gm_ln_b", "gm_ws", "gm_bs", "conv_w", "conv_b", "dt_bias", "a_log",
             "d_skip", "ssm_norm_w", "w_proj_a", "w_proj_b", "w_out", "final_norm_w"]
    pick = lambda n, k: big[n][k] if n in big else small_out[n][k]
    return (loss, grad_x, *[pick(n, 0) for n in order], *[pick(n, 1) for n in order], *[pick(n, 2) for n in order],
            *[pick(n, 3) for n in order])
```

```python
import functools

import jax
import jax.numpy as jnp
from jax import lax
from jax.experimental import pallas as pl
from jax.experimental.pallas import tpu as pltpu

f32 = jnp.float32
_MXU = jnp.bfloat16
_WIRE = jnp.bfloat16

D = 1024
Q = 128
NG = 8
DI = 2048
NH = 32
P = 64
HPG = 4
NS = 128
KC = 4
CD = 4096
GRP = DI // NG
EPS = 1e-6
NDEV = 8
WA = 3 * D
WB = DI + CD + 256
WG = 2 * D
OFF_XBC = DI
OFF_DT = DI + CD
NIN = 11296
VMEM_LIMIT = 56 * 1024 * 1024
NEG = -1e30

ADAM_LR, ADAM_B1, ADAM_B2, ADAM_EPS, ADAM_WD, ADAM_STEP = 0.001, 0.9, 0.999, 1e-08, 0.01, 10


def _bf(x):
    return x.astype(_MXU)


def _dot(a, b):
    return jnp.dot(a, b, preferred_element_type=f32)


def _dot_nt(a, b):
    return lax.dot_general(a, b, (((1,), (1,)), ((), ())), preferred_element_type=f32)


def _dot_tn(a, b):
    return lax.dot_general(a, b, (((0,), (0,)), ((), ())), preferred_element_type=f32)


def _dot_exact(a, b):
    return jnp.dot(a, b, preferred_element_type=f32, precision=lax.Precision.HIGHEST)


def _sigmoid(x):
    return jax.nn.sigmoid(x)


def _silu(x):
    return x * _sigmoid(x)


def _dsilu(x):
    s = _sigmoid(x)
    return s * (1.0 + x * (1.0 - s))


_GK = 0.7978845608028654
_GC = 0.044715


def _gelu_and_grad(x):
    x2, hx = x * x, 0.5 * x
    t = jnp.tanh(x * (_GK + (_GK * _GC) * x2))
    return hx * (1.0 + t), 0.5 * (1.0 + t) + hx * (1.0 - t * t) * (_GK + (3.0 * _GK * _GC) * x2)


def _gelu(x):
    return 0.5 * x * (1.0 + jnp.tanh(x * (_GK + (_GK * _GC) * (x * x))))


def _softplus(x):
    return jnp.maximum(x, 0.0) + jnp.log1p(jnp.exp(-jnp.abs(x)))


def _tile(n, cap):
    if n <= cap:
        return n
    best = None
    for t in range(128, cap + 1, 128):
        if n % t == 0:
            best = t
    assert best is not None, (n, cap)
    return best


def _params(sem):
    return pltpu.CompilerParams(dimension_semantics=sem, vmem_limit_bytes=VMEM_LIMIT)


def _exch_ops(src_ref, out_ref, send_sems, recv_sems, local_sem, gather):
    x, y, c = lax.axis_index("x"), lax.axis_index("y"), lax.axis_index("c")
    me = 4 * x + 2 * y + c

    def peer(j):
        px = 1 - x if (j >> 2) & 1 else x
        py = 1 - y if (j >> 1) & 1 else y
        pc = 1 - c if j & 1 else c
        return (px, py, pc), 4 * px + 2 * py + pc

    def copy(j, landing):
        dev, idx = peer(j)
        return pltpu.make_async_remote_copy(
            src_ref=src_ref if gather else src_ref.at[idx], dst_ref=out_ref.at[idx] if landing else out_ref.at[me],
            send_sem=send_sems.at[j - 1], recv_sem=recv_sems.at[j - 1], device_id=dev, device_id_type=pl.DeviceIdType.MESH)

    mine = pltpu.make_async_copy(src_ref if gather else src_ref.at[me], out_ref.at[me], local_sem)

    def start():
        mine.start()
        for j in range(1, NDEV):
            copy(j, False).start()

    def finish():
        for j in range(1, NDEV):
            copy(j, True).wait_recv()
        for j in range(1, NDEV):
            copy(j, False).wait_send()
        mine.wait()

    return start, finish


def _gather2_ops(src_ref, out_ref, send_sems, recv_sems, local_sem):
    x, y, c = lax.axis_index("x"), lax.axis_index("y"), lax.axis_index("c")
    me, sibling = (x, y, c), (x, y, 1 - c)
    chips = [(1 - x, y), (x, 1 - y), (1 - x, 1 - y)]

    def slot(px, py, pc):
        return out_ref.at[4 * px + 2 * py + pc]

    def copy(k, block, to, src=None):
        return pltpu.make_async_remote_copy(
            src_ref=slot(*block) if src is None else src, dst_ref=slot(*block), send_sem=send_sems.at[k],
            recv_sem=recv_sems.at[k], device_id=to, device_id_type=pl.DeviceIdType.MESH)

    mine = pltpu.make_async_copy(src_ref, slot(*me), local_sem)
    first = [copy(0, me, sibling, src=src_ref)] + [copy(1 + j, me, (*chip, c), src=src_ref) for j, chip in enumerate(chips)]
    passed = [copy(4 + j, (*chip, c), sibling) for j, chip in enumerate(chips)]

    def start():
        mine.start()
        for cp in first:
            cp.start()

    def finish():
        for j, chip in enumerate(chips):
            copy(1 + j, (*chip, c), me).wait_recv()
            passed[j].start()
        copy(0, sibling, me).wait_recv()
        for j, chip in enumerate(chips):
            copy(4 + j, (*chip, 1 - c), me).wait_recv()
        for cp in first + passed:
            cp.wait_send()
        mine.wait()

    return start, finish


def _exch_shape(src, gather):
    return jax.ShapeDtypeStruct((NDEV,) + tuple(src.shape if gather else src.shape[1:]), src.dtype)


_EXCH_SEMS = [pltpu.SemaphoreType.DMA((NDEV - 1,)), pltpu.SemaphoreType.DMA((NDEV - 1,)), pltpu.SemaphoreType.DMA]


def _carry_call(body, name, grid, in_specs, out_specs, out_shape, scratch_shapes, args, carry=()):
    n_in, n_out, n_sc, nx = len(in_specs), len(out_specs), len(scratch_shapes), len(carry)
    sem = ("arbitrary",) * len(grid)
    if nx == 0:
        outs = pl.pallas_call(body, name=name, grid=grid, in_specs=in_specs, out_specs=out_specs, out_shape=out_shape,
                              scratch_shapes=scratch_shapes, compiler_params=_params(sem))(*args)
        return list(outs), []

    def wrapped(*refs):
        ins, srcs = refs[:n_in], refs[n_in:n_in + nx]
        outs, dsts = refs[n_in + nx:n_in + nx + n_out], refs[n_in + nx + n_out:n_in + 2 * nx + n_out]
        scratch, sems = refs[n_in + 2 * nx + n_out:n_in + 2 * nx + n_out + n_sc], refs[n_in + 2 * nx + n_out + n_sc:]
        ops = [_gather2_ops(srcs[i], dsts[i], sems[3 * i], sems[3 * i + 1], sems[3 * i + 2]) if carry[i][1] == "two_level" else
               _exch_ops(srcs[i], dsts[i], sems[3 * i], sems[3 * i + 1], sems[3 * i + 2], carry[i][1]) for i in range(nx)]
        first = functools.reduce(jnp.logical_and, [pl.program_id(d) == 0 for d in range(len(grid))])
        last = functools.reduce(jnp.logical_and, [pl.program_id(d) == grid[d] - 1 for d in range(len(grid))])

        @pl.when(first)
        def _():
            for start, _ in ops:
                start()

        body(*ins, *outs, *scratch)

        @pl.when(last)
        def _():
            for _, finish in ops:
                finish()

    hbm = pl.BlockSpec(memory_space=pl.ANY)
    outs = pl.pallas_call(
        wrapped, name=name, grid=grid, in_specs=list(in_specs) + [hbm] * nx, out_specs=list(out_specs) + [hbm] * nx,
        out_shape=list(out_shape) + [_exch_shape(s, g) for s, g in carry], scratch_shapes=list(scratch_shapes) + _EXCH_SEMS * nx,
        compiler_params=_params(sem))(*args, *[s for s, _ in carry])
    return list(outs[:n_out]), list(outs[n_out:])


def _mm(a, b, mode, name, out_dtype=f32, tm_cap=1024, tn_cap=1280, tk_cap=1280, carry=()):
    if mode == "nn":
        (M, K), (K2, N) = a.shape, b.shape
    elif mode == "nt":
        (M, K), (N, K2) = a.shape, b.shape
    else:
        (K, M), (K2, N) = a.shape, b.shape
        tk_cap = min(tk_cap, 1024)
    assert K == K2, (a.shape, b.shape, mode)
    tm, tn, tk = _tile(M, tm_cap), _tile(N, tn_cap), _tile(K, tk_cap)
    nk = K // tk
    if mode == "nn":
        a_spec = pl.BlockSpec((tm, tk), lambda i, j, k: (i, k))
        b_spec = pl.BlockSpec((tk, tn), lambda i, j, k: (k, j))
        dot = _dot
    elif mode == "nt":
        a_spec = pl.BlockSpec((tm, tk), lambda i, j, k: (i, k))
        b_spec = pl.BlockSpec((tn, tk), lambda i, j, k: (j, k))
        dot = _dot_nt
    else:
        a_spec = pl.BlockSpec((tk, tm), lambda i, j, k: (k, i))
        b_spec = pl.BlockSpec((tk, tn), lambda i, j, k: (k, j))
        dot = _dot_tn
    o_spec = pl.BlockSpec((tm, tn), lambda i, j, k: (i, j))

    def body_one(a_ref, b_ref, o_ref):
        o_ref[...] = dot(_bf(a_ref[...]), _bf(b_ref[...])).astype(out_dtype)

    def body_acc(a_ref, b_ref, o_ref, acc_ref):
        k = pl.program_id(2)

        @pl.when(k == 0)
        def _():
            acc_ref[...] = jnp.zeros_like(acc_ref)

        acc_ref[...] += dot(_bf(a_ref[...]), _bf(b_ref[...]))

        @pl.when(k == nk - 1)
        def _():
            o_ref[...] = acc_ref[...].astype(out_dtype)

    (out,), exchanged = _carry_call(body_one if nk == 1 else body_acc, name, (M // tm, N // tn, nk), [a_spec, b_spec], [o_spec],
                                    [jax.ShapeDtypeStruct((M, N), out_dtype)], [] if nk == 1 else [pltpu.VMEM((tm, tn), f32)],
                                    (a, b), carry)
    return (out, exchanged) if carry else out


def _row_spec(ts, w, col=0):
    return pl.BlockSpec((ts, w), lambda i: (i, col))


def _full_spec(shape):
    nd = len(shape)
    return pl.BlockSpec(shape, lambda i: (0,) * nd)


def _modulate(x, nw, shift, scale, name, carry=()):
    T = x.shape[0]
    ts = _tile(T, 512)

    def body(x_ref, nw_ref, sh_ref, sc_ref, h_ref):
        xv = x_ref[...]
        r = lax.rsqrt(jnp.mean(xv * xv, axis=-1, keepdims=True) + EPS)
        h_ref[...] = ((xv * r * nw_ref[...]) * (1.0 + sc_ref[...]) + sh_ref[...]).astype(h_ref.dtype)

    (h,), exchanged = _carry_call(body, name, (T // ts,), [_row_spec(ts, D), _full_spec((1, D)), _full_spec((1, D)), _full_spec((1, D))],
                                  [_row_spec(ts, D)], [jax.ShapeDtypeStruct((T, D), _MXU)], [], (x, nw, shift, scale), carry)
    return (h, exchanged) if carry else h


def _final_loss(x, tgt, fw, name):
    T = x.shape[0]
    ts = _tile(T, 512)

    def body(x_ref, t_ref, fw_ref, dx_ref, acc_ref):
        @pl.when(pl.program_id(0) == 0)
        def _():
            acc_ref[...] = jnp.zeros_like(acc_ref)

        xv = x_ref[...]
        r = lax.rsqrt(jnp.mean(xv * xv, axis=-1, keepdims=True) + EPS)
        xn = xv * r
        e = xn * fw_ref[...] - t_ref[...]
        dy = e * (1.0 / D)
        acc_ref[0:1, :] += jnp.sum(dy * xn, axis=0, keepdims=True)
        acc_ref[1:2, :] += jnp.sum(jnp.sum(e * e, axis=0, keepdims=True), axis=1, keepdims=True)
        dxn = dy * fw_ref[...]
        dx_ref[...] = r * (dxn - xn * jnp.mean(dxn * xn, axis=-1, keepdims=True))

    return pl.pallas_call(
        body, name=name, grid=(T // ts,),
        in_specs=[_row_spec(ts, D), _row_spec(ts, D), _full_spec((1, D))],
        out_specs=[_row_spec(ts, D), _full_spec((8, D))],
        out_shape=[jax.ShapeDtypeStruct((T, D), f32), jax.ShapeDtypeStruct((8, D), f32)],
        compiler_params=_params(("arbitrary",)))(x, tgt, fw)


def _tril(n):
    return lax.broadcasted_iota(jnp.int32, (n, n), 0) >= lax.broadcasted_iota(jnp.int32, (n, n), 1)


def _gm_chunk_fwd(u, v, z, lw, lb, ws_ref, bsx, gu=None, gv=None):
    gu, gv = _gelu(u) if gu is None else gu, _gelu(v) if gv is None else gv
    mu = jnp.mean(gv, axis=-1, keepdims=True)
    cen = gv - mu
    rstd = lax.rsqrt(jnp.mean(cen * cen, axis=-1, keepdims=True) + EPS)
    vhat = cen * rstd
    vn = _bf(vhat * lw + lb)
    tri = _tril(Q)
    mixed = jnp.concatenate(
        [_dot(_bf(jnp.where(tri, ws_ref[g], 0.0)), vn[:, g * Q:(g + 1) * Q]) for g in range(NG)], axis=1) + bsx
    return gu, vhat, rstd, vn, mixed


def _gmlp_fwd(pA, lw, lb, ws, bsx, name):
    T = pA.shape[0]
    ts = _tile(T, 512)

    def body(u_ref, v_ref, z_ref, lw_ref, lb_ref, ws_ref, bsx_ref, y_ref):
        def chunk(ci, carry):
            rows = pl.ds(pl.multiple_of(ci * Q, Q), Q)
            u, v, z = u_ref[rows, :].astype(f32), v_ref[rows, :].astype(f32), z_ref[rows, :].astype(f32)
            gu, _, _, _, mixed = _gm_chunk_fwd(u, v, z, lw_ref[...], lb_ref[...], ws_ref, bsx_ref[...])
            y_ref[rows, :] = (gu * mixed * _silu(z)).astype(y_ref.dtype)
            return carry

        lax.fori_loop(0, ts // Q, chunk, 0)

    return pl.pallas_call(
        body, name=name, grid=(T // ts,),
        in_specs=[_row_spec(ts, D, 0), _row_spec(ts, D, 1), _row_spec(ts, D, 2), _full_spec((1, D)), _full_spec((1, D)),
                  _full_spec((NG, Q, Q)), _full_spec((Q, D))],
        out_specs=_row_spec(ts, D), out_shape=jax.ShapeDtypeStruct((T, D), _MXU),
        compiler_params=_params(("arbitrary",)))(pA, pA, pA, lw, lb, ws, bsx)


def _gmlp_bwd(dya, pA, lw, lb, ws, bsx, name):
    T = pA.shape[0]
    ts = _tile(T, 512)

    def body(dy_ref, u_ref, v_ref, z_ref, lw_ref, lb_ref, ws_ref, bsx_ref, dp_ref, acc_ref, dws_ref, dbs_ref):
        @pl.when(pl.program_id(0) == 0)
        def _():
            acc_ref[...] = jnp.zeros_like(acc_ref)
            dws_ref[...] = jnp.zeros_like(dws_ref)
            dbs_ref[...] = jnp.zeros_like(dbs_ref)

        tri = _tril(Q)

        def chunk(ci, carry):
            rows = pl.ds(pl.multiple_of(ci * Q, Q), Q)
            u, v, z, dy = u_ref[rows, :].astype(f32), v_ref[rows, :].astype(f32), z_ref[rows, :].astype(f32), dy_ref[rows, :].astype(f32)
            (gu, d_gu), (gv, d_gv) = _gelu_and_grad(u), _gelu_and_grad(v)
            gu, vhat, rstd, vn, mixed = _gm_chunk_fwd(u, v, z, lw_ref[...], lb_ref[...], ws_ref, bsx_ref[...], gu, gv)
            sz = _silu(z)
            dp_ref[rows, 0:D] = (dy * mixed * sz * d_gu).astype(dp_ref.dtype)
            dp_ref[rows, 2 * D:3 * D] = (dy * gu * mixed * _dsilu(z)).astype(dp_ref.dtype)
            dmixed = dy * gu * sz
            dbs_ref[...] += dmixed
            dmb = _bf(dmixed)
            dvn_parts = []
            for g in range(NG):
                cols = slice(g * Q, (g + 1) * Q)
                wg = _bf(jnp.where(tri, ws_ref[g], 0.0))
                dvn_parts.append(_dot_tn(wg, dmb[:, cols]))
                dws_ref[g] += jnp.where(tri, _dot_nt(dmb[:, cols], vn[:, cols]), 0.0)
            dvn = jnp.concatenate(dvn_parts, axis=1)
            acc_ref[0:1, :] += jnp.sum(dvn * vhat, axis=0, keepdims=True)
            acc_ref[1:2, :] += jnp.sum(dvn, axis=0, keepdims=True)
            dvh = dvn * lw_ref[...]
            dgv = rstd * (dvh - jnp.mean(dvh, axis=-1, keepdims=True) - vhat * jnp.mean(dvh * vhat, axis=-1, keepdims=True))
            dp_ref[rows, D:2 * D] = (dgv * d_gv).astype(dp_ref.dtype)
            return carry

        lax.fori_loop(0, ts // Q, chunk, 0)

    return pl.pallas_call(
        body, name=name, grid=(T // ts,),
        in_specs=[_row_spec(ts, D), _row_spec(ts, D, 0), _row_spec(ts, D, 1), _row_spec(ts, D, 2), _full_spec((1, D)),
                  _full_spec((1, D)), _full_spec((NG, Q, Q)), _full_spec((Q, D))],
        out_specs=[_row_spec(ts, WA), _full_spec((8, D)), _full_spec((NG, Q, Q)), _full_spec((Q, D))],
        out_shape=[jax.ShapeDtypeStruct((T, WA), _MXU), jax.ShapeDtypeStruct((8, D), f32),
                   jax.ShapeDtypeStruct((NG, Q, Q), f32), jax.ShapeDtypeStruct((Q, D), f32)],
        compiler_params=_params(("arbitrary",)))(dya, pA, pA, pA, lw, lb, ws, bsx)


def _head_maps():
    h = lax.broadcasted_iota(jnp.int32, (128, DI), 0)
    ch = lax.broadcasted_iota(jnp.int32, (128, DI), 1)
    ex = (ch // P == h).astype(_MXU)
    return ex, ex.T


def _split(v, parts):
    out = []
    for _ in range(parts - 1):
        p = _bf(v)
        out.append(p)
        v = v - p.astype(f32)
    out.append(_bf(v))
    return out


def _expand(v, ex_ref, parts):
    acc = None
    for p in _split(v, parts):
        t = _dot(p, ex_ref[...])
        acc = t if acc is None else acc + t
    return acc


def _reduce(v, rd_ref, parts=2):
    acc = None
    for p in _split(v, parts):
        t = _dot(p, rd_ref[...])
        acc = t if acc is None else acc + t
    return acc


def _ssd_time(pb_ref, dtb_ref, alog_ref):
    xdt = pb_ref[:, OFF_DT:OFF_DT + 128] + dtb_ref[...]
    dt = _softplus(xdt)
    a = -jnp.exp(alog_ref[...])
    cs = _dot_exact(_tril(Q).astype(f32), dt * a)
    return xdt, dt, a, cs


def _head_mask(r):
    return lax.broadcasted_iota(jnp.int32, (Q, GRP), 1) // P == r


def _ssd_group_fwd(g, xa_ref, s_prev, cs, cs_t, dtx_ref, csx_ref, dsk_ref):
    cols = slice(g * GRP, (g + 1) * GRP)
    xs = xa_ref[:, cols]
    bb = _bf(xa_ref[:, DI + g * NS:DI + (g + 1) * NS])
    cb = _bf(xa_ref[:, DI + NG * NS + g * NS:DI + NG * NS + (g + 1) * NS])
    gm = _dot_nt(cb, bb)
    xd = xs * dtx_ref[:, cols]
    csx = csx_ref[:, cols]
    csl = csx_ref[Q - 1:Q, cols]
    tri = _tril(Q)
    lms = [jnp.exp(jnp.where(tri, cs[:, HPG * g + r:HPG * g + r + 1] - cs_t[HPG * g + r:HPG * g + r + 1, :], NEG))
           for r in range(HPG)]
    mfs = [gm * lm for lm in lms]
    mcat = jnp.concatenate([_bf(m) for m in mfs], axis=1)
    xbd = jnp.concatenate([_bf(jnp.where(_head_mask(r), xd, 0.0)) for r in range(HPG)], axis=0)
    ydiag = _dot(mcat, xbd)
    ecs, dte, ecl = jnp.exp(csx), jnp.exp(csl - csx), jnp.exp(csl)
    yoff = ecs * _dot(cb, _bf(s_prev))
    y = ydiag + yoff + xs * dsk_ref[:, cols]
    xdd = xd * dte
    s_new = s_prev * ecl + _dot_tn(bb, _bf(xdd))
    return y, s_new, (xs, bb, cb, xd, lms, mfs, mcat, xbd, yoff, xdd, ecs, dte, ecl)


def _ssd_fwd(pB, cw, cb, dtb, alog, dsk, snw, name, carry=()):
    T = pB.shape[0]
    nc = T // Q
    ex, _ = _head_maps()

    def body(pb_ref, cw_ref, cb_ref, dtb_ref, alog_ref, dsk_ref, snw_ref, ex_ref, y_ref, xc_ref, st_ref,
             s_ref, ext_ref, xa_ref, dtx_ref, csx_ref):
        @pl.when(pl.program_id(0) == 0)
        def _():
            s_ref[...] = jnp.zeros_like(s_ref)
            ext_ref[...] = jnp.zeros_like(ext_ref)

        ext_ref[8:8 + Q, :] = pb_ref[:, OFF_XBC:OFF_XBC + CD]
        for j in range(CD // 512):
            cj = slice(j * 512, (j + 1) * 512)
            e = ext_ref[:, cj]
            xc = cb_ref[:, cj] + cw_ref[KC - 1:KC, cj] * e[8:8 + Q]
            for s in range(1, KC):
                xc = xc + cw_ref[KC - 1 - s:KC - s, cj] * pltpu.roll(e, s, 0)[8:8 + Q]
            xc_ref[:, cj] = xc
            xa_ref[:, cj] = _silu(xc)
        ext_ref[0:8, :] = ext_ref[Q:Q + 8, :]

        _, dt, _, cs = _ssd_time(pb_ref, dtb_ref, alog_ref)
        cs_t = cs.T
        dtx_ref[...] = _expand(dt, ex_ref, 2)
        csx_ref[...] = _expand(cs, ex_ref, 3)
        for g in range(NG):
            s_prev = s_ref[g]
            st_ref[0, g] = s_prev
            y, s_new, _ = _ssd_group_fwd(g, xa_ref, s_prev, cs, cs_t, dtx_ref, csx_ref, dsk_ref)
            s_ref[g] = s_new
            cols = slice(g * GRP, (g + 1) * GRP)
            yz = y * _silu(pb_ref[:, cols])
            rr = lax.rsqrt(jnp.mean(yz * yz, axis=-1, keepdims=True) + EPS)
            y_ref[:, cols] = (yz * rr * snw_ref[:, cols]).astype(y_ref.dtype)

    outs, exchanged = _carry_call(
        body, name, (nc,),
        [_row_spec(Q, WB), _full_spec((8, CD)), _full_spec((1, CD)), _full_spec((1, 128)), _full_spec((1, 128)),
         _full_spec((1, DI)), _full_spec((1, DI)), _full_spec((128, DI))],
        [_row_spec(Q, DI), _row_spec(Q, CD), pl.BlockSpec((1, NG, NS, GRP), lambda i: (i, 0, 0, 0))],
        [jax.ShapeDtypeStruct((T, DI), _MXU), jax.ShapeDtypeStruct((T, CD), f32), jax.ShapeDtypeStruct((nc, NG, NS, GRP), f32)],
        [pltpu.VMEM((NG, NS, GRP), f32), pltpu.VMEM((Q + 8, CD), f32), pltpu.VMEM((Q, CD), f32),
         pltpu.VMEM((Q, DI), f32), pltpu.VMEM((Q, DI), f32)],
        (pB, cw, cb, dtb, alog, dsk, snw, ex), carry)
    return outs + [exchanged]


def _ssd_bwd(dyb, pB, xc, states, cw, dtb, alog, dsk, snw, name, carry=()):
    T = pB.shape[0]
    nc = T // Q
    ex, rd = _head_maps()
    selr = (lax.broadcasted_iota(jnp.int32, (HPG * Q, 128), 0) // Q == lax.broadcasted_iota(jnp.int32, (HPG * Q, 128), 1)).astype(_MXU)

    def body(dy_ref, pb_ref, xc_ref, st_ref, cw_ref, dtb_ref, alog_ref, dsk_ref, snw_ref, ex_ref, rd_ref, selr_ref,
             dp_ref, dcw_ref, dcb_ref, dhd_ref, dcol_ref,
             ds_ref, xa_ref, sg_ref, dxa_ref, dxe_ref, dtx_ref, csx_ref, rcs_ref, rdt_ref, rl_ref, dcs_ref, dcst_ref):
        @pl.when(pl.program_id(0) == 0)
        def _():
            ds_ref[...] = jnp.zeros_like(ds_ref)
            dxe_ref[...] = jnp.zeros_like(dxe_ref)
            dcw_ref[...] = jnp.zeros_like(dcw_ref)
            dcb_ref[...] = jnp.zeros_like(dcb_ref)
            dhd_ref[...] = jnp.zeros_like(dhd_ref)
            dcol_ref[...] = jnp.zeros_like(dcol_ref)
            rl_ref[...] = jnp.zeros_like(rl_ref)
            dcst_ref[...] = jnp.zeros_like(dcst_ref)

        dcs_ref[...] = jnp.zeros_like(dcs_ref)
        for j in range(CD // 512):
            cj = slice(j * 512, (j + 1) * 512)
            sg_ref[:, cj] = _sigmoid(xc_ref[:, cj])
            xa_ref[:, cj] = xc_ref[:, cj] * sg_ref[:, cj]
        xdt, dt, a, cs = _ssd_time(pb_ref, dtb_ref, alog_ref)
        cs_t = cs.T
        dtx_ref[...] = _expand(dt, ex_ref, 2)
        csx_ref[...] = _expand(cs, ex_ref, 3)

        for g in range(NG):
            cols = slice(g * GRP, (g + 1) * GRP)
            s_prev = st_ref[0, g]
            sb = _bf(s_prev)
            y, _, (xs, bb, cbm, xd, lms, mfs, mcat, xbd, yoff, xdd, ecs, dte, ecl) = _ssd_group_fwd(
                g, xa_ref, s_prev, cs, cs_t, dtx_ref, csx_ref, dsk_ref)
            z = pb_ref[:, cols]
            sz = _silu(z)
            yz = y * sz
            rr = lax.rsqrt(jnp.mean(yz * yz, axis=-1, keepdims=True) + EPS)
            nrm = yz * rr
            dyb_g = dy_ref[:, cols].astype(f32)
            dcol_ref[1:2, cols] += jnp.sum(dyb_g * nrm, axis=0, keepdims=True)
            dn = dyb_g * snw_ref[:, cols]
            dyz = rr * (dn - nrm * jnp.mean(dn * nrm, axis=-1, keepdims=True))
            dyv = dyz * sz
            dp_ref[:, cols] = (dyz * y * _dsilu(z)).astype(dp_ref.dtype)
            dcol_ref[0:1, cols] += jnp.sum(dyv * xs, axis=0, keepdims=True)
            dy16 = _bf(dyv)
            dmcat = _dot_nt(dy16, xbd)
            dg = dmcat[:, 0:Q] * lms[0]
            for r in range(1, HPG):
                dg = dg + dmcat[:, r * Q:(r + 1) * Q] * lms[r]
            e16 = _bf(jnp.concatenate([dmcat[:, r * Q:(r + 1) * Q] * mfs[r] for r in range(HPG)], axis=1))
            rows = _dot(e16, selr_ref[...])
            dcs_ref[...] += rows if g == 0 else pltpu.roll(rows, HPG * g, 1)
            csum = _dot(jnp.ones((8, Q), _MXU), e16)
            for r in range(HPG):
                dcst_ref[HPG * g + r:HPG * g + r + 1, :] = csum[0:1, r * Q:(r + 1) * Q]
            big = _dot_tn(mcat, dy16)
            dxd_diag = jnp.where(_head_mask(0), big[0:Q], 0.0)
            for r in range(1, HPG):
                dxd_diag = dxd_diag + jnp.where(_head_mask(r), big[r * Q:(r + 1) * Q], 0.0)
            dsn = ds_ref[g]
            dsn16 = _bf(dsn)
            dxdd = _dot(bb, dsn16)
            t_state = dxdd * xdd
            dxd = dxd_diag + dxdd * dte
            rcs_ref[:, cols] = dyv * yoff - t_state
            rl_ref[0:1, cols] = (jnp.sum(t_state, axis=0, keepdims=True)
                                 + ecl * jnp.sum(s_prev * dsn, axis=0, keepdims=True))
            rdt_ref[:, cols] = dxd * xs
            dxa_ref[:, cols] = dyv * dsk_ref[:, cols] + dxd * dtx_ref[:, cols]
            dg16 = _bf(dg)
            dw16 = _bf(dyv * ecs)
            dxa_ref[:, DI + NG * NS + g * NS:DI + NG * NS + (g + 1) * NS] = _dot(dg16, bb) + _dot_nt(dw16, sb)
            dxa_ref[:, DI + g * NS:DI + (g + 1) * NS] = _dot_tn(dg16, cbm) + _dot_nt(_bf(xdd), dsn16)
            ds_ref[g] = dsn * ecl + _dot_tn(cbm, dw16)

        row = lax.broadcasted_iota(jnp.int32, (Q, 128), 0)
        dcs = (dcs_ref[...] - dcst_ref[...].T + _reduce(rcs_ref[...], rd_ref)
               + jnp.where(row == Q - 1, _reduce(rl_ref[...], rd_ref)[0:1, :], 0.0))
        upper = lax.broadcasted_iota(jnp.int32, (Q, Q), 0) <= lax.broadcasted_iota(jnp.int32, (Q, Q), 1)
        dadt = _dot_exact(upper.astype(f32), dcs)
        valid = lax.broadcasted_iota(jnp.int32, (Q, 128), 1) < NH
        ddt = jnp.where(valid, _reduce(rdt_ref[...], rd_ref, 1) + dadt * a, 0.0)
        ddtr = ddt * _sigmoid(xdt)
        dhd_ref[0:1, :] += jnp.sum(ddtr, axis=0, keepdims=True)
        dhd_ref[1:2, :] += jnp.sum(jnp.where(valid, dadt * dt * a, 0.0), axis=0, keepdims=True)
        dp_ref[:, OFF_DT:OFF_DT + 128] = ddtr.astype(dp_ref.dtype)
        dp_ref[:, OFF_DT + 128:WB] = jnp.zeros((Q, WB - OFF_DT - 128), dp_ref.dtype)
        for j in range(CD // 512):
            cj = slice(j * 512, (j + 1) * 512)
            sg = sg_ref[:, cj]
            dxc = dxa_ref[:, cj] * (sg + xa_ref[:, cj] * (1.0 - sg))
            dxe_ref[0:Q, cj] = dxc
            dcb_ref[0:1, cj] += jnp.sum(dxc, axis=0, keepdims=True)
            raw = pb_ref[:, OFF_XBC + j * 512:OFF_XBC + (j + 1) * 512]
            e = dxe_ref[:, cj]
            dcw_ref[KC - 1:KC, cj] += jnp.sum(dxc * raw, axis=0, keepdims=True)
            dxb = cw_ref[KC - 1:KC, cj] * dxc
            for s in range(1, KC):
                sh = pltpu.roll(e, Q + 8 - s, 0)[0:Q]
                dcw_ref[KC - 1 - s:KC - s, cj] += jnp.sum(sh * raw, axis=0, keepdims=True)
                dxb = dxb + cw_ref[KC - 1 - s:KC - s, cj] * sh
            dp_ref[:, OFF_XBC + j * 512:OFF_XBC + (j + 1) * 512] = dxb.astype(dp_ref.dtype)
        dxe_ref[Q:Q + 8, :] = dxe_ref[0:8, :]

    rev = lambda i: (nc - 1 - i, 0)
    outs, exchanged = _carry_call(
        body, name, (nc,),
        [pl.BlockSpec((Q, DI), rev), pl.BlockSpec((Q, WB), rev), pl.BlockSpec((Q, CD), rev),
         pl.BlockSpec((1, NG, NS, GRP), lambda i: (nc - 1 - i, 0, 0, 0)),
         _full_spec((8, CD)), _full_spec((1, 128)), _full_spec((1, 128)),
         _full_spec((1, DI)), _full_spec((1, DI)), _full_spec((128, DI)), _full_spec((DI, 128)), _full_spec((HPG * Q, 128))],
        [pl.BlockSpec((Q, WB), rev), _full_spec((8, CD)), _full_spec((8, CD)), _full_spec((8, 128)), _full_spec((8, DI))],
        [jax.ShapeDtypeStruct((T, WB), _MXU), jax.ShapeDtypeStruct((8, CD), f32), jax.ShapeDtypeStruct((8, CD), f32),
         jax.ShapeDtypeStruct((8, 128), f32), jax.ShapeDtypeStruct((8, DI), f32)],
        [pltpu.VMEM((NG, NS, GRP), f32), pltpu.VMEM((Q, CD), f32), pltpu.VMEM((Q, CD), f32), pltpu.VMEM((Q, CD), f32),
         pltpu.VMEM((Q + 8, CD), f32), pltpu.VMEM((Q, DI), f32), pltpu.VMEM((Q, DI), f32),
         pltpu.VMEM((Q, DI), f32), pltpu.VMEM((Q, DI), f32), pltpu.VMEM((8, DI), f32),
         pltpu.VMEM((Q, 128), f32), pltpu.VMEM((128, Q), f32)],
        (dyb, pB, xc, states, cw, dtb, alog, dsk, snw, ex, rd, selr), carry)
    return outs + [exchanged]


def _merge_fwd(ya, yb, pG, x, gate, wa, wb, wo, name, carry=()):
    T = x.shape[0]
    ts = _tile(T, 256)

    def body(ya_ref, yb_ref, g_ref, x_ref, gate_ref, wa_ref, wb_ref, wo_ref, xo_ref, mg_ref, pa_ref, pb_ref):
        pa = _dot(ya_ref[...], wa_ref[...])
        pb = _dot(yb_ref[...], wb_ref[...])
        merged = _sigmoid(g_ref[:, 0:D].astype(f32)) * pa + _sigmoid(g_ref[:, D:2 * D].astype(f32)) * pb
        mg = _bf(merged)
        xo_ref[...] = x_ref[...] + gate_ref[...] * _dot(mg, wo_ref[...])
        mg_ref[...] = mg
        pa_ref[...] = pa.astype(pa_ref.dtype)
        pb_ref[...] = pb.astype(pb_ref.dtype)

    outs, exchanged = _carry_call(
        body, name, (T // ts,),
        [_row_spec(ts, D), _row_spec(ts, DI), _row_spec(ts, WG), _row_spec(ts, D), _full_spec((1, D)),
         _full_spec((D, D)), _full_spec((DI, D)), _full_spec((D, D))],
        [_row_spec(ts, D)] * 4,
        [jax.ShapeDtypeStruct((T, D), f32), jax.ShapeDtypeStruct((T, D), _MXU), jax.ShapeDtypeStruct((T, D), _MXU),
         jax.ShapeDtypeStruct((T, D), _MXU)],
        [], (ya, yb, pG, x, gate, wa, wb, wo), carry)
    return outs + [exchanged]


def _merge_bwd(dxout, merged, pa, pb, pG, gate, wo, name):
    T = dxout.shape[0]
    ts = _tile(T, 256)

    def body(dx_ref, mg_ref, pa_ref, pb_ref, g_ref, gate_ref, wo_ref, do_ref, dpa_ref, dpb_ref, dg_ref, acc_ref):
        @pl.when(pl.program_id(0) == 0)
        def _():
            acc_ref[...] = jnp.zeros_like(acc_ref)

        dxo = dx_ref[...]
        acc_ref[0:1, :] += jnp.sum(dxo * _dot(mg_ref[...], wo_ref[...]), axis=0, keepdims=True)
        do = _bf(dxo * gate_ref[...])
        do_ref[...] = do
        dmerged = _dot_nt(do, wo_ref[...])
        sa, sb = _sigmoid(g_ref[:, 0:D].astype(f32)), _sigmoid(g_ref[:, D:2 * D].astype(f32))
        dpa_ref[...] = (dmerged * sa).astype(dpa_ref.dtype)
        dpb_ref[...] = (dmerged * sb).astype(dpb_ref.dtype)
        dg_ref[:, 0:D] = (dmerged * pa_ref[...].astype(f32) * sa * (1.0 - sa)).astype(dg_ref.dtype)
        dg_ref[:, D:2 * D] = (dmerged * pb_ref[...].astype(f32) * sb * (1.0 - sb)).astype(dg_ref.dtype)

    return pl.pallas_call(
        body, name=name, grid=(T // ts,),
        in_specs=[_row_spec(ts, D), _row_spec(ts, D), _row_spec(ts, D), _row_spec(ts, D), _row_spec(ts, WG),
                  _full_spec((1, D)), _full_spec((D, D))],
        out_specs=[_row_spec(ts, D), _row_spec(ts, D), _row_spec(ts, D), _row_spec(ts, WG), _full_spec((8, D))],
        out_shape=[jax.ShapeDtypeStruct((T, D), _MXU), jax.ShapeDtypeStruct((T, D), _MXU), jax.ShapeDtypeStruct((T, D), _MXU),
                   jax.ShapeDtypeStruct((T, WG), _MXU), jax.ShapeDtypeStruct((8, D), f32)],
        compiler_params=_params(("arbitrary",)))(dxout, merged, pa, pb, pG, gate, wo)


def _rows_split(r):
    ra, rb = D // NDEV, DI // NDEV
    return r[:, 0:ra].reshape(D, D), r[:, ra:ra + rb].reshape(DI, D), r[:, ra + rb:].reshape(D, D)


def _layer_fwd(x, mod, lp, tag, carry=None, h=None):
    carry = carry or {}
    got = {}

    def mm(key, w, **kw):
        if carry.get(key):
            out, got[key] = _mm(h, w, "nn", f"{key}_{tag}", carry=carry[key], **kw)
            return out
        return _mm(h, w, "nn", f"{key}_{tag}", **kw)

    if h is None:
        h = _modulate(x, lp["nw"], mod[0:1], mod[1:2], f"modulate_{tag}")
    pA = mm("proj_gm", lp["w_gm"], out_dtype=_MXU)
    pB = mm("proj_ssd", lp["w_ssd"])
    pG = mm("proj_gate", lp["w_g"], out_dtype=_MXU)
    if "wa" not in lp:
        lp = dict(lp)
        lp["wa"], lp["wb"], lp["wo"] = _rows_split(jnp.concatenate([got["proj_gm"][0], got["proj_gate"][0]], axis=1))
    ya = _gmlp_fwd(pA, lp["lw"], lp["lb"], lp["ws"], lp["bsx"], f"gmlp_fwd_{tag}")
    yb, xc, states, got["ssd_fwd"] = _ssd_fwd(pB, lp["cw"], lp["cb"], lp["dtb"], lp["alog"], lp["dsk"], lp["snw"], f"ssd_fwd_{tag}",
                                              carry.get("ssd_fwd", ()))
    xo, merged, pa, pb, _ = _merge_fwd(ya, yb, pG, x, mod[2:3], lp["wa"], lp["wb"], lp["wo"], f"merge_fwd_{tag}")
    return xo, dict(x=x, h=h, pA=pA, pB=pB, pG=pG, ya=ya, yb=yb, xc=xc, states=states, merged=merged, pa=pa, pb=pb), got, lp


def _dh_modulate_bwd(dpG, dpA, dpB, w_g, w_gm, w_ssd, dxout, x, nw, scale, name, carry=()):
    T = x.shape[0]
    tm = _tile(T, 1024)
    fam = [(WG, _tile(WG, 1024)), (WA, _tile(WA, 1024)), (WB, _tile(WB, 1280))]
    steps = [w // k for w, k in fam]
    first = [0, steps[0], steps[0] + steps[1]]
    ns = sum(steps)

    def chunk(f):
        return lambda i, s: jnp.clip(s - first[f], 0, steps[f] - 1)

    a_specs = [pl.BlockSpec((tm, fam[f][1]), lambda i, s, c=chunk(f): (i, c(i, s))) for f in range(3)]
    b_specs = [pl.BlockSpec((D, fam[f][1]), lambda i, s, c=chunk(f): (0, c(i, s))) for f in range(3)]
    tok = pl.BlockSpec((tm, D), lambda i, s: (i, 0))
    tok_once = pl.BlockSpec((tm, D), lambda i, s: (i, 0), pipeline_mode=pl.Buffered(1))
    vec = pl.BlockSpec((1, D), lambda i, s: (0, 0))

    def body(ag, aa, ab, bg, ba, bb, dxo_ref, x_ref, nw_ref, sc_ref, dx_ref, sum_ref, acc_ref):
        i, s = pl.program_id(0), pl.program_id(1)

        @pl.when(jnp.logical_and(i == 0, s == 0))
        def _():
            sum_ref[...] = jnp.zeros_like(sum_ref)

        @pl.when(s == 0)
        def _():
            acc_ref[...] = jnp.zeros_like(acc_ref)

        @pl.when(s < first[1])
        def _():
            acc_ref[...] += _dot_nt(ag[...], bg[...])

        @pl.when(jnp.logical_and(s >= first[1], s < first[2]))
        def _():
            acc_ref[...] += _dot_nt(aa[...], ba[...])

        @pl.when(s >= first[2])
        def _():
            acc_ref[...] += _dot_nt(ab[...], bb[...])

        @pl.when(s == ns - 1)
        def _():
            xv, dh_v = x_ref[...], acc_ref[...]
            r = lax.rsqrt(jnp.mean(xv * xv, axis=-1, keepdims=True) + EPS)
            xn = xv * r
            hn = xn * nw_ref[...]
            dhn = dh_v * (1.0 + sc_ref[...])
            sum_ref[0:1, :] += jnp.sum(dh_v, axis=0, keepdims=True)
            sum_ref[1:2, :] += jnp.sum(dh_v * hn, axis=0, keepdims=True)
            sum_ref[2:3, :] += jnp.sum(dhn * xn, axis=0, keepdims=True)
            dxn = dhn * nw_ref[...]
            dx_ref[...] = dxo_ref[...] + r * (dxn - xn * jnp.mean(dxn * xn, axis=-1, keepdims=True))

    outs, exchanged = _carry_call(
        body, name, (T // tm, ns), a_specs + b_specs + [tok_once, tok_once, vec, vec], [tok, pl.BlockSpec((8, D), lambda i, s: (0, 0))],
        [jax.ShapeDtypeStruct((T, D), f32), jax.ShapeDtypeStruct((8, D), f32)], [pltpu.VMEM((tm, D), f32)],
        (dpG, dpA, dpB, w_g, w_gm, w_ssd, dxout, x, nw, scale), carry)
    return outs + [exchanged]


def _win_blocks(g):
    full = jnp.concatenate([g["w_gm"], g["w_ssd"][:, 0:DI + CD], g["w_ssd"][:, DI + CD:DI + CD + NH], g["w_g"]], axis=1)
    return jnp.transpose(full.reshape(D, NDEV, NIN // NDEV), (1, 0, 2))


def _row_blocks(g):
    return jnp.concatenate([g["wa"].reshape(NDEV, D // NDEV, D), g["wb"].reshape(NDEV, DI // NDEV, D),
                            g["wo"].reshape(NDEV, D // NDEV, D)], axis=1)


def _layer_bwd(dxo, sv, mod, lp, tag, carry=(), scatter_own=None):
    do, dpa, dpb, dpG, s_gate = _merge_bwd(dxo, sv["merged"], sv["pa"], sv["pb"], sv["pG"], mod[2:3], lp["wo"], f"merge_bwd_{tag}")
    g = {}
    g["wo"] = _mm(sv["merged"], do, "tn", f"dw_out_{tag}", out_dtype=_WIRE)
    g["wa"] = _mm(sv["ya"], dpa, "tn", f"dw_proj_a_{tag}", out_dtype=_WIRE)
    g["wb"] = _mm(sv["yb"], dpb, "tn", f"dw_proj_b_{tag}", out_dtype=_WIRE)
    dya = _mm(dpa, lp["wa"], "nt", f"dy_a_{tag}", out_dtype=_MXU)
    dyb = _mm(dpb, lp["wb"], "nt", f"dy_b_{tag}", out_dtype=_MXU)
    dpA, s_ln, g["ws"], dbsx = _gmlp_bwd(dya, sv["pA"], lp["lw"], lp["lb"], lp["ws"], lp["bsx"], f"gmlp_bwd_{tag}")
    carry = list(carry) + ([(_row_blocks(g), False)] if scatter_own else [])
    dpB, g["cw"], s_cb, s_hd, s_col, got = _ssd_bwd(dyb, sv["pB"], sv["xc"], sv["states"], lp["cw"], lp["dtb"], lp["alog"],
                                                    lp["dsk"], lp["snw"], f"ssd_bwd_{tag}", carry)
    g["lw"], g["lb"] = s_ln[0], s_ln[1]
    g["bs"] = dbsx.reshape(Q, NG, Q).sum(-1).T
    g["cb"] = s_cb[0]
    g["dtb"], g["alog"] = s_hd[0, :NH], s_hd[1, :NH]
    g["dsk"] = s_col[0].reshape(NH, P).sum(-1)
    g["snw"] = s_col[1]
    g["cw"] = g["cw"][0:KC]
    g["w_g"] = _mm(sv["h"], dpG, "tn", f"dw_gate_{tag}", out_dtype=_WIRE)
    g["w_gm"] = _mm(sv["h"], dpA, "tn", f"dw_gm_{tag}", out_dtype=_WIRE)
    if scatter_own:
        g["w_ssd"], got_slab = _mm(sv["h"], dpB, "tn", f"dw_ssd_{tag}", out_dtype=_WIRE, carry=[(scatter_own(g), True)])
    else:
        g["w_ssd"] = _mm(sv["h"], dpB, "tn", f"dw_ssd_{tag}", out_dtype=_WIRE)
    dx, s_mod, got_dh = _dh_modulate_bwd(dpG, dpA, dpB, lp["w_g"], lp["w_gm"], lp["w_ssd"], dxo, sv["x"], lp["nw"], mod[1:2],
                                         f"dh_{tag}", [(_win_blocks(g), False)] if scatter_own else ())
    if scatter_own:
        got, got_own = got[:-1], [got[-1], got_dh[0], got_slab[0]]
    else:
        got_own = []
    g["mod"] = jnp.concatenate([s_mod[0], s_mod[1], s_gate[0]])
    g["nw"] = s_mod[2]
    return dx, g, got, got_own


def _prep_layer(nw, w_in_full, lw, lb, ws, bs, cw_full, cb, dtb, alog, dsk, snw, rows=None):
    z = jnp.zeros((D, WB - (DI + CD + NH)), w_in_full.dtype)
    pad_h = lambda v: jnp.pad(v, (0, 128 - NH)).reshape(1, 128)
    extra = dict(zip(("wa", "wb", "wo"), _rows_split(rows))) if rows is not None else {}
    return dict(
        **extra,
        nw=nw.reshape(1, D),
        w_gm=w_in_full[:, 0:WA],
        w_ssd=jnp.concatenate([w_in_full[:, WA:WA + DI + CD], w_in_full[:, WA + DI + CD:WA + DI + CD + NH], z], axis=1),
        w_g=w_in_full[:, WA + DI + CD + NH:NIN],
        lw=lw.reshape(1, D), lb=lb.reshape(1, D), ws=ws, bsx=jnp.repeat(bs.T, Q, axis=1),
        cw=jnp.pad(cw_full, ((0, 8 - KC), (0, 0))), cb=cb.reshape(1, CD), dtb=pad_h(dtb), alog=pad_h(alog),
        dsk=jnp.repeat(dsk, P).reshape(1, DI), snw=snw.reshape(1, DI))


def _mod_dist(c8, ada_w, ada_b_cols, name):
    L, _, AW = ada_w.shape

    def body(c_ref, w_ref, b_ref, parts_ref, sc_ref, call_ref, mine_ref, send1, recv1, send2, recv2):
        x, y, c = lax.axis_index("x"), lax.axis_index("y"), lax.axis_index("c")
        me = 4 * x + 2 * y + c

        def peer(j):
            px = 1 - x if (j >> 2) & 1 else x
            py = 1 - y if (j >> 1) & 1 else y
            pc = 1 - c if j & 1 else c
            return (px, py, pc), 4 * px + 2 * py + pc

        def copy(j, src, dst, sems, landing):
            dev, idx = peer(j)
            return pltpu.make_async_remote_copy(
                src_ref=src, dst_ref=dst.at[idx] if landing else dst.at[me], send_sem=sems[0].at[j - 1], recv_sem=sems[1].at[j - 1],
                device_id=dev, device_id_type=pl.DeviceIdType.MESH)

        def all_to_all(src, dst, sems):
            for j in range(1, NDEV):
                copy(j, src, dst, sems, False).start()
            for j in range(1, NDEV):
                copy(j, src, dst, sems, True).wait_recv()
            for j in range(1, NDEV):
                copy(j, src, dst, sems, False).wait_send()

        call_ref[me] = c_ref[...]
        all_to_all(c_ref, call_ref, (send1, recv1))
        row = lax.broadcasted_iota(jnp.int32, (8, D), 0)
        cm = jnp.zeros((8, D), f32)
        for k in range(NDEV):
            cm = jnp.where(row == k, call_ref[k], cm)
        sc = _silu(cm)
        sc_ref[...] = sc
        for l in range(L):
            mine_ref[l] = _dot(_bf(sc), w_ref[l]) + b_ref[l]
        parts_ref[me] = mine_ref[...]
        all_to_all(mine_ref, parts_ref, (send2, recv2))

    vmem = pl.BlockSpec(memory_space=pltpu.VMEM)
    sems = pltpu.SemaphoreType.DMA((NDEV - 1,))
    return pl.pallas_call(
        body, name=name, in_specs=[vmem, vmem, vmem], out_specs=[vmem, vmem],
        out_shape=[jax.ShapeDtypeStruct((NDEV, L, 8, AW), f32), jax.ShapeDtypeStruct((8, D), f32)],
        scratch_shapes=[pltpu.VMEM((NDEV, 8, D), f32), pltpu.VMEM((L, 8, AW), f32), sems, sems, sems, sems])(c8, ada_w, ada_b_cols)


def _ada_w_grad(sc_all, dmod_cols, name):
    W = dmod_cols.shape[1]

    def body(s_ref, d_ref, o_ref):
        o_ref[...] = lax.dot_general(s_ref[...], d_ref[...], (((0,), (0,)), ((), ())), preferred_element_type=f32,
                                     precision=lax.Precision.HIGHEST)

    return pl.pallas_call(body, name=name, out_shape=jax.ShapeDtypeStruct((D, W), f32))(sc_all, dmod_cols)


def _adamw_math(w, g, m, v):
    m = ADAM_B1 * m + (1.0 - ADAM_B1) * g
    v = ADAM_B2 * v + (1.0 - ADAM_B2) * (g * g)
    m_hat = m / (1.0 - ADAM_B1 ** ADAM_STEP)
    v_hat = v / (1.0 - ADAM_B2 ** ADAM_STEP)
    delta = -ADAM_LR * (m_hat / (jnp.sqrt(v_hat) + ADAM_EPS) + ADAM_WD * w)
    return delta, m, v


def _sum_adamw(recvs, w, m, v, name, carry=()):
    nl = len(recvs)
    n, R, C = recvs[0].shape
    tr = _tile(R, 128 if C > 1024 else 256)
    nr = R // tr
    stacked = w.ndim == 3
    at = (lambda ref: ref.at[0]) if stacked else (lambda ref: ref)

    def body(*refs):
        r_refs, (w_ref, m_ref, v_ref, g_ref, d_ref, nm_ref, nv_ref) = refs[:nl], [at(r) for r in refs[nl:]]
        for li in range(nl):
            @pl.when(pl.program_id(0) == li)
            def _(r_ref=r_refs[li]):
                g = r_ref[0].astype(f32)
                for k in range(1, n):
                    g = g + r_ref[k].astype(f32)
                g_ref[...] = g
                d_ref[...], nm_ref[...], nv_ref[...] = _adamw_math(w_ref[...], g, m_ref[...], v_ref[...])

    if stacked:
        spec = pl.BlockSpec((1, tr, C), lambda l, i: (l, i, 0))
    else:
        spec = pl.BlockSpec((tr, C), lambda l, i: (l * nr + i, 0))
    r_specs = [pl.BlockSpec((n, tr, C), lambda l, i, li=li: (0, jnp.clip(i + (l - li) * nr, 0, nr - 1), 0)) for li in range(nl)]
    outs, exchanged = _carry_call(body, name, (nl, nr), r_specs + [spec, spec, spec], [spec] * 4,
                                  [jax.ShapeDtypeStruct(w.shape, f32)] * 4, [], (*recvs, w, m, v), carry)
    return (outs, exchanged) if carry else outs


def _pack(arrays, rows):
    flat = []
    for a in arrays:
        a = a.reshape(-1).astype(f32)
        flat.append(jnp.pad(a, (0, (-a.shape[0]) % 128)))
    flat = jnp.concatenate(flat)
    return jnp.pad(flat, (0, rows * 128 - flat.shape[0])).reshape(rows, 128)


def _unpack(slab, shapes):
    flat = slab.reshape(-1)
    out, off = [], 0
    for s in shapes:
        n = 1
        for d in s:
            n *= d
        out.append(flat[off:off + n].reshape(s))
        off += n + (-n) % 128
    return out


def kernel(x, c, ada_w, ada_b, norm_w, w_in, gm_ln_w, gm_ln_b, gm_ws, gm_bs, conv_w, conv_b, dt_bias, a_log, d_skip, ssm_norm_w, w_proj_a, w_proj_b, w_out, final_norm_w, loss_target, m_ada_w, m_ada_b, m_norm_w, m_w_in, m_gm_ln_w, m_gm_ln_b, m_gm_ws, m_gm_bs, m_conv_w, m_conv_b, m_dt_bias, m_a_log, m_d_skip, m_ssm_norm_w, m_w_proj_a, m_w_proj_b, m_w_out, m_final_norm_w, v_ada_w, v_ada_b, v_norm_w, v_w_in, v_gm_ln_w, v_gm_ln_b, v_gm_ws, v_gm_bs, v_conv_w, v_conv_b, v_dt_bias, v_a_log, v_d_skip, v_ssm_norm_w, v_w_proj_a, v_w_proj_b, v_w_out, v_final_norm_w):
    L = 2
    me = 4 * lax.axis_index("x") + 2 * lax.axis_index("y") + lax.axis_index("c")
    W = dict(ada_w=ada_w, ada_b=ada_b, norm_w=norm_w, w_in=w_in, gm_ln_w=gm_ln_w, gm_ln_b=gm_ln_b, gm_ws=gm_ws, gm_bs=gm_bs,
             conv_w=conv_w, conv_b=conv_b, dt_bias=dt_bias, a_log=a_log, d_skip=d_skip, ssm_norm_w=ssm_norm_w, w_proj_a=w_proj_a,
             w_proj_b=w_proj_b, w_out=w_out, final_norm_w=final_norm_w)
    M = dict(ada_w=m_ada_w, ada_b=m_ada_b, norm_w=m_norm_w, w_in=m_w_in, gm_ln_w=m_gm_ln_w, gm_ln_b=m_gm_ln_b, gm_ws=m_gm_ws,
             gm_bs=m_gm_bs, conv_w=m_conv_w, conv_b=m_conv_b, dt_bias=m_dt_bias, a_log=m_a_log, d_skip=m_d_skip,
             ssm_norm_w=m_ssm_norm_w, w_proj_a=m_w_proj_a, w_proj_b=m_w_proj_b, w_out=m_w_out, final_norm_w=m_final_norm_w)
    V = dict(ada_w=v_ada_w, ada_b=v_ada_b, norm_w=v_norm_w, w_in=v_w_in, gm_ln_w=v_gm_ln_w, gm_ln_b=v_gm_ln_b, gm_ws=v_gm_ws,
             gm_bs=v_gm_bs, conv_w=v_conv_w, conv_b=v_conv_b, dt_bias=v_dt_bias, a_log=v_a_log, d_skip=v_d_skip,
             ssm_norm_w=v_ssm_norm_w, w_proj_a=v_w_proj_a, w_proj_b=v_w_proj_b, w_out=v_w_out, final_norm_w=v_final_norm_w)
    SW = NIN // NDEV
    AW = 3 * D // NDEV
    CW = CD // NDEV
    RA, RB = D // NDEV, DI // NDEV

    wire = lambda a: a.astype(_WIRE)
    rows_of = lambda d, l: jnp.concatenate([d["w_proj_a"][l], d["w_proj_b"][l], d["w_out"][l]], axis=0)
    parts, sc_all = _mod_dist(jnp.broadcast_to(c, (8, D)), wire(ada_w),
                              lax.dynamic_slice_in_dim(ada_b, me * AW, AW, axis=1).reshape(L, 1, AW), "mod_dist")
    mods = [lax.dynamic_index_in_dim(parts[:, l], me, axis=1, keepdims=False).reshape(3, D) for l in range(L)]
    cols = lambda g: jnp.transpose(g, (1, 0, 2)).reshape(g.shape[1], -1)

    def prep(l, g_win, g_rows):
        return _prep_layer(norm_w[l], cols(g_win), gm_ln_w[l], gm_ln_b[l], gm_ws[l], gm_bs[l], cols(g_cw[:, l]), conv_b[l],
                           dt_bias[l], a_log[l], d_skip[l], ssm_norm_w[l], g_rows)

    half = D // 2
    rows0, rows1 = wire(rows_of(W, 0)), wire(rows_of(W, 1))
    hr = rows0.shape[0] // 2
    h0, (g_win0, g_cw) = _modulate(x[0], norm_w[0].reshape(1, D), mods[0][0:1], mods[0][1:2], "modulate_l0",
                                   [(wire(w_in[0]), "two_level"), (conv_w.reshape(L * KC, CW), True)])
    g_cw = g_cw.reshape(NDEV, L, KC, CW)
    h, sv0, got, lp0 = _layer_fwd(x[0], mods[0], prep(0, g_win0, None), "l0",
                                  dict(proj_gm=[(rows0[:hr], True)], proj_gate=[(rows0[hr:], True)],
                                       proj_ssd=[(wire(w_in[1][:half]), True)],
                                       ssd_fwd=[(wire(w_in[1][half:]), True), (rows1, True)]), h0)
    lp1 = prep(1, jnp.concatenate([got["proj_ssd"][0], got["ssd_fwd"][0]], axis=1), got["ssd_fwd"][1])
    h, sv1, _, _ = _layer_fwd(h, mods[1], lp1, "l1")
    dh, s_fin = _final_loss(h, loss_target[0], final_norm_w.reshape(1, D), "final_loss")
    loss = lax.psum(0.5 / D * s_fin[1, 0], ("x", "y", "c"))

    small_l = ["gm_ln_w", "gm_ln_b", "gm_ws", "gm_bs", "conv_b", "dt_bias", "a_log", "d_skip", "ssm_norm_w"]
    small_key = dict(gm_ln_w="lw", gm_ln_b="lb", gm_ws="ws", gm_bs="bs", conv_b="cb", dt_bias="dtb", a_log="alog", d_skip="dsk",
                     ssm_norm_w="snw")
    shapes_l = [W[n].shape[1:] for n in small_l] + [(KC, CD)]
    shapes_t = [(L, 3 * D), (L, D), (D,)]
    rows_for = lambda sh: -(-sum(-(-functools.reduce(lambda a, b: a * b, s, 1) // 128) for s in sh) // 256) * 256
    n_rows_l, n_rows_t = rows_for(shapes_l), rows_for(shapes_t)
    slab_of = lambda g: _pack([g[small_key[n]] for n in small_l] + [g["cw"]], n_rows_l)

    dh, g1, _, _ = _layer_bwd(dh, sv1, mods[1], lp1, "l1")
    dh, g0, got, got_own = _layer_bwd(dh, sv0, mods[0], lp0, "l0",
                                      [(_win_blocks(g1), False), (_row_blocks(g1), False), (slab_of(g1), True)], slab_of)
    grads = [g0, g1]
    grad_x = dh[None]
    recv_win = [got_own[1], got[0]]
    recv_rows = [got_own[0], got[1]]
    slab_all = [got_own[2], got[2]]
    st = lambda key: jnp.stack([g[key] for g in grads])
    slab_tail = _pack([st("mod"), st("nw"), s_fin[0]], n_rows_t)

    o_win = _sum_adamw(recv_win, w_in, m_w_in, v_w_in, "adamw_w_in")
    rows_all = lambda d: jnp.concatenate([rows_of(d, l) for l in range(L)], axis=0)
    o_rows, (slab_t,) = _sum_adamw(recv_rows, rows_all(W), rows_all(M), rows_all(V), "adamw_w_rows", [(slab_tail, True)])
    o_rows = [o.reshape(L, 2 * RA + RB, D) for o in o_rows]
    o_pa = [o[:, 0:RA] for o in o_rows]
    o_pb = [o[:, RA:RA + RB] for o in o_rows]
    o_po = [o[:, RA + RB:] for o in o_rows]

    def slab_params(d):
        return jnp.concatenate([_pack([d[n][l] for n in small_l] + [jnp.zeros((KC, CD), f32)], n_rows_l) for l in range(L)], axis=0)

    o_small = [[_unpack(o[l * n_rows_l:(l + 1) * n_rows_l], shapes_l) for l in range(L)]
               for o in _sum_adamw(slab_all, slab_params(W), slab_params(M), slab_params(V), "adamw_small")]
    small_out = {n: [jnp.stack([o_small[k][l][i] for l in range(L)]) for k in range(4)] for i, n in enumerate(small_l)}
    tail_params = lambda d: _pack([d["ada_b"], d["norm_w"], d["final_norm_w"]], n_rows_t)
    o_tail = [_unpack(o, shapes_t) for o in _sum_adamw([slab_t], tail_params(W), tail_params(M), tail_params(V), "adamw_small_tail")]
    for i, n in enumerate(["ada_b", "norm_w", "final_norm_w"]):
        small_out[n] = [o_tail[k][i] for k in range(4)]

    g_cw_mine = jnp.stack([lax.dynamic_slice_in_dim(o_small[0][l][len(small_l)], me * CW, CW, axis=1) for l in range(L)])
    o_cw = _sum_adamw([g_cw_mine.reshape(1, L * KC, CW)], conv_w.reshape(L * KC, CW), m_conv_w.reshape(L * KC, CW),
                      v_conv_w.reshape(L * KC, CW), "adamw_conv_w")
    o_cw = [o.reshape(L, KC, CW) for o in o_cw]

    dmod_all = jnp.stack([_unpack(slab_t[k], shapes_t)[0] for k in range(NDEV)])
    dmod_cols = lax.dynamic_slice_in_dim(dmod_all, me * AW, AW, axis=2).reshape(NDEV, L * AW)
    g_ada_w = jnp.transpose(_ada_w_grad(sc_all, dmod_cols, "ada_w_grad").reshape(D, L, AW), (1, 0, 2))
    o_ada = _sum_adamw([g_ada_w[l][None] for l in range(L)], ada_w, m_ada_w, v_ada_w, "adamw_ada_w")

    big = dict(ada_w=o_ada, w_in=o_win, conv_w=o_cw, w_proj_a=o_pa, w_proj_b=o_pb, w_out=o_po)
    order = ["ada_w", "ada_b", "norm_w", "w_in", "gm_ln_w", "gm_ln_b", "gm_ws", "gm_bs", "conv_w", "conv_b", "dt_bias", "a_log",
             "d_skip", "ssm_norm_w", "w_proj_a", "w_proj_b", "w_out", "final_norm_w"]
    pick = lambda n, k: big[n][k] if n in big else small_out[n][k]
    return (loss, grad_x, *[pick(n, 0) for n in order], *[pick(n, 1) for n in order], *[pick(n, 2) for n in order],
            *[pick(n, 3) for n in order])
```

```python
import functools

import jax
import jax.numpy as jnp
from jax import lax
from jax.experimental import pallas as pl
from jax.experimental.pallas import tpu as pltpu

f32 = jnp.float32
_MXU = jnp.bfloat16
_WIRE = jnp.bfloat16

D = 1024
Q = 128
NG = 8
DI = 2048
NH = 32
P = 64
HPG = 4
NS = 128
KC = 4
CD = 4096
GRP = DI // NG
EPS = 1e-6
NDEV = 8
WA = 3 * D
WB = DI + CD + 256
WG = 2 * D
OFF_XBC = DI
OFF_DT = DI + CD
NIN = 11296
VMEM_LIMIT = 56 * 1024 * 1024
NEG = -1e30

ADAM_LR, ADAM_B1, ADAM_B2, ADAM_EPS, ADAM_WD, ADAM_STEP = 0.001, 0.9, 0.999, 1e-08, 0.01, 10


def _bf(x):
    return x.astype(_MXU)


def _dot(a, b):
    return jnp.dot(a, b, preferred_element_type=f32)


def _dot_nt(a, b):
    return lax.dot_general(a, b, (((1,), (1,)), ((), ())), preferred_element_type=f32)


def _dot_tn(a, b):
    return lax.dot_general(a, b, (((0,), (0,)), ((), ())), preferred_element_type=f32)


def _dot_exact(a, b):
    return jnp.dot(a, b, preferred_element_type=f32, precision=lax.Precision.HIGHEST)


def _sigmoid(x):
    return jax.nn.sigmoid(x)


def _silu(x):
    return x * _sigmoid(x)


def _dsilu(x):
    s = _sigmoid(x)
    return s * (1.0 + x * (1.0 - s))


_GK = 0.7978845608028654
_GC = 0.044715


def _gelu_and_grad(x):
    x2, hx = x * x, 0.5 * x
    t = jnp.tanh(x * (_GK + (_GK * _GC) * x2))
    return hx * (1.0 + t), 0.5 * (1.0 + t) + hx * (1.0 - t * t) * (_GK + (3.0 * _GK * _GC) * x2)


def _gelu(x):
    return 0.5 * x * (1.0 + jnp.tanh(x * (_GK + (_GK * _GC) * (x * x))))


def _softplus(x):
    return jnp.maximum(x, 0.0) + jnp.log1p(jnp.exp(-jnp.abs(x)))


def _tile(n, cap):
    if n <= cap:
        return n
    best = None
    for t in range(128, cap + 1, 128):
        if n % t == 0:
            best = t
    assert best is not None, (n, cap)
    return best


def _params(sem):
    return pltpu.CompilerParams(dimension_semantics=sem, vmem_limit_bytes=VMEM_LIMIT)


def _exch_ops(src_ref, out_ref, send_sems, recv_sems, local_sem, gather):
    x, y, c = lax.axis_index("x"), lax.axis_index("y"), lax.axis_index("c")
    me = 4 * x + 2 * y + c

    def peer(j):
        px = 1 - x if (j >> 2) & 1 else x
        py = 1 - y if (j >> 1) & 1 else y
        pc = 1 - c if j & 1 else c
        return (px, py, pc), 4 * px + 2 * py + pc

    def copy(j, landing):
        dev, idx = peer(j)
        return pltpu.make_async_remote_copy(
            src_ref=src_ref if gather else src_ref.at[idx], dst_ref=out_ref.at[idx] if landing else out_ref.at[me],
            send_sem=send_sems.at[j - 1], recv_sem=recv_sems.at[j - 1], device_id=dev, device_id_type=pl.DeviceIdType.MESH)

    mine = pltpu.make_async_copy(src_ref if gather else src_ref.at[me], out_ref.at[me], local_sem)

    def start():
        mine.start()
        for j in range(1, NDEV):
            copy(j, False).start()

    def finish():
        for j in range(1, NDEV):
            copy(j, True).wait_recv()
        for j in range(1, NDEV):
            copy(j, False).wait_send()
        mine.wait()

    return start, finish


def _gather2_ops(src_ref, out_ref, send_sems, recv_sems, local_sem):
    x, y, c = lax.axis_index("x"), lax.axis_index("y"), lax.axis_index("c")
    me, sibling = (x, y, c), (x, y, 1 - c)
    chips = [(1 - x, y), (x, 1 - y), (1 - x, 1 - y)]

    def slot(px, py, pc):
        return out_ref.at[4 * px + 2 * py + pc]

    def copy(k, block, to, src=None):
        return pltpu.make_async_remote_copy(
            src_ref=slot(*block) if src is None else src, dst_ref=slot(*block), send_sem=send_sems.at[k],
            recv_sem=recv_sems.at[k], device_id=to, device_id_type=pl.DeviceIdType.MESH)

    mine = pltpu.make_async_copy(src_ref, slot(*me), local_sem)
    first = [copy(0, me, sibling, src=src_ref)] + [copy(1 + j, me, (*chip, c), src=src_ref) for j, chip in enumerate(chips)]
    passed = [copy(4 + j, (*chip, c), sibling) for j, chip in enumerate(chips)]

    def start():
        mine.start()
        for cp in first:
            cp.start()

    def finish():
        for j, chip in enumerate(chips):
            copy(1 + j, (*chip, c), me).wait_recv()
            passed[j].start()
        copy(0, sibling, me).wait_recv()
        for j, chip in enumerate(chips):
            copy(4 + j, (*chip, 1 - c), me).wait_recv()
        for cp in first + passed:
            cp.wait_send()
        mine.wait()

    return start, finish


def _exch_shape(src, gather):
    return jax.ShapeDtypeStruct((NDEV,) + tuple(src.shape if gather else src.shape[1:]), src.dtype)


_EXCH_SEMS = [pltpu.SemaphoreType.DMA((NDEV - 1,)), pltpu.SemaphoreType.DMA((NDEV - 1,)), pltpu.SemaphoreType.DMA]


def _carry_call(body, name, grid, in_specs, out_specs, out_shape, scratch_shapes, args, carry=()):
    n_in, n_out, n_sc, nx = len(in_specs), len(out_specs), len(scratch_shapes), len(carry)
    sem = ("arbitrary",) * len(grid)
    if nx == 0:
        outs = pl.pallas_call(body, name=name, grid=grid, in_specs=in_specs, out_specs=out_specs, out_shape=out_shape,
                              scratch_shapes=scratch_shapes, compiler_params=_params(sem))(*args)
        return list(outs), []

    def wrapped(*refs):
        ins, srcs = refs[:n_in], refs[n_in:n_in + nx]
        outs, dsts = refs[n_in + nx:n_in + nx + n_out], refs[n_in + nx + n_out:n_in + 2 * nx + n_out]
        scratch, sems = refs[n_in + 2 * nx + n_out:n_in + 2 * nx + n_out + n_sc], refs[n_in + 2 * nx + n_out + n_sc:]
        ops = [_gather2_ops(srcs[i], dsts[i], sems[3 * i], sems[3 * i + 1], sems[3 * i + 2]) if carry[i][1] == "two_level" else
               _exch_ops(srcs[i], dsts[i], sems[3 * i], sems[3 * i + 1], sems[3 * i + 2], carry[i][1]) for i in range(nx)]
        first = functools.reduce(jnp.logical_and, [pl.program_id(d) == 0 for d in range(len(grid))])
        last = functools.reduce(jnp.logical_and, [pl.program_id(d) == grid[d] - 1 for d in range(len(grid))])

        @pl.when(first)
        def _():
            for start, _ in ops:
                start()

        body(*ins, *outs, *scratch)

        @pl.when(last)
        def _():
            for _, finish in ops:
                finish()

    hbm = pl.BlockSpec(memory_space=pl.ANY)
    outs = pl.pallas_call(
        wrapped, name=name, grid=grid, in_specs=list(in_specs) + [hbm] * nx, out_specs=list(out_specs) + [hbm] * nx,
        out_shape=list(out_shape) + [_exch_shape(s, g) for s, g in carry], scratch_shapes=list(scratch_shapes) + _EXCH_SEMS * nx,
        compiler_params=_params(sem))(*args, *[s for s, _ in carry])
    return list(outs[:n_out]), list(outs[n_out:])


def _mm(a, b, mode, name, out_dtype=f32, tm_cap=1024, tn_cap=1280, tk_cap=1280, carry=()):
    if mode == "nn":
        (M, K), (K2, N) = a.shape, b.shape
    elif mode == "nt":
        (M, K), (N, K2) = a.shape, b.shape
    else:
        (K, M), (K2, N) = a.shape, b.shape
        tk_cap = min(tk_cap, 1024)
    assert K == K2, (a.shape, b.shape, mode)
    tm, tn, tk = _tile(M, tm_cap), _tile(N, tn_cap), _tile(K, tk_cap)
    nk = K // tk
    if mode == "nn":
        a_spec = pl.BlockSpec((tm, tk), lambda i, j, k: (i, k))
        b_spec = pl.BlockSpec((tk, tn), lambda i, j, k: (k, j))
        dot = _dot
    elif mode == "nt":
        a_spec = pl.BlockSpec((tm, tk), lambda i, j, k: (i, k))
        b_spec = pl.BlockSpec((tn, tk), lambda i, j, k: (j, k))
        dot = _dot_nt
    else:
        a_spec = pl.BlockSpec((tk, tm), lambda i, j, k: (k, i))
        b_spec = pl.BlockSpec((tk, tn), lambda i, j, k: (k, j))
        dot = _dot_tn
    o_spec = pl.BlockSpec((tm, tn), lambda i, j, k: (i, j))

    def body_one(a_ref, b_ref, o_ref):
        o_ref[...] = dot(_bf(a_ref[...]), _bf(b_ref[...])).astype(out_dtype)

    def body_acc(a_ref, b_ref, o_ref, acc_ref):
        k = pl.program_id(2)

        @pl.when(k == 0)
        def _():
            acc_ref[...] = jnp.zeros_like(acc_ref)

        acc_ref[...] += dot(_bf(a_ref[...]), _bf(b_ref[...]))

        @pl.when(k == nk - 1)
        def _():
            o_ref[...] = acc_ref[...].astype(out_dtype)

    (out,), exchanged = _carry_call(body_one if nk == 1 else body_acc, name, (M // tm, N // tn, nk), [a_spec, b_spec], [o_spec],
                                    [jax.ShapeDtypeStruct((M, N), out_dtype)], [] if nk == 1 else [pltpu.VMEM((tm, tn), f32)],
                                    (a, b), carry)
    return (out, exchanged) if carry else out


def _row_spec(ts, w, col=0):
    return pl.BlockSpec((ts, w), lambda i: (i, col))


def _full_spec(shape):
    nd = len(shape)
    return pl.BlockSpec(shape, lambda i: (0,) * nd)


def _modulate(x, nw, shift, scale, name, carry=()):
    T = x.shape[0]
    ts = _tile(T, 512)

    def body(x_ref, nw_ref, sh_ref, sc_ref, h_ref):
        xv = x_ref[...]
        r = lax.rsqrt(jnp.mean(xv * xv, axis=-1, keepdims=True) + EPS)
        h_ref[...] = ((xv * r * nw_ref[...]) * (1.0 + sc_ref[...]) + sh_ref[...]).astype(h_ref.dtype)

    (h,), exchanged = _carry_call(body, name, (T // ts,), [_row_spec(ts, D), _full_spec((1, D)), _full_spec((1, D)), _full_spec((1, D))],
                                  [_row_spec(ts, D)], [jax.ShapeDtypeStruct((T, D), _MXU)], [], (x, nw, shift, scale), carry)
    return (h, exchanged) if carry else h


def _final_loss(x, tgt, fw, name):
    T = x.shape[0]
    ts = _tile(T, 512)

    def body(x_ref, t_ref, fw_ref, dx_ref, acc_ref):
        @pl.when(pl.program_id(0) == 0)
        def _():
            acc_ref[...] = jnp.zeros_like(acc_ref)

        xv = x_ref[...]
        r = lax.rsqrt(jnp.mean(xv * xv, axis=-1, keepdims=True) + EPS)
        xn = xv * r
        e = xn * fw_ref[...] - t_ref[...]
        dy = e * (1.0 / D)
        acc_ref[0:1, :] += jnp.sum(dy * xn, axis=0, keepdims=True)
        acc_ref[1:2, :] += jnp.sum(jnp.sum(e * e, axis=0, keepdims=True), axis=1, keepdims=True)
        dxn = dy * fw_ref[...]
        dx_ref[...] = r * (dxn - xn * jnp.mean(dxn * xn, axis=-1, keepdims=True))

    return pl.pallas_call(
        body, name=name, grid=(T // ts,),
        in_specs=[_row_spec(ts, D), _row_spec(ts, D), _full_spec((1, D))],
        out_specs=[_row_spec(ts, D), _full_spec((8, D))],
        out_shape=[jax.ShapeDtypeStruct((T, D), f32), jax.ShapeDtypeStruct((8, D), f32)],
        compiler_params=_params(("arbitrary",)))(x, tgt, fw)


def _tril(n):
    return lax.broadcasted_iota(jnp.int32, (n, n), 0) >= lax.broadcasted_iota(jnp.int32, (n, n), 1)


def _gm_chunk_fwd(u, v, z, lw, lb, ws_ref, bsx, gu=None, gv=None):
    gu, gv = _gelu(u) if gu is None else gu, _gelu(v) if gv is None else gv
    mu = jnp.mean(gv, axis=-1, keepdims=True)
    cen = gv - mu
    rstd = lax.rsqrt(jnp.mean(cen * cen, axis=-1, keepdims=True) + EPS)
    vhat = cen * rstd
    vn = _bf(vhat * lw + lb)
    tri = _tril(Q)
    mixed = jnp.concatenate(
        [_dot(_bf(jnp.where(tri, ws_ref[g], 0.0)), vn[:, g * Q:(g + 1) * Q]) for g in range(NG)], axis=1) + bsx
    return gu, vhat, rstd, vn, mixed


def _gmlp_fwd(pA, lw, lb, ws, bsx, name):
    T = pA.shape[0]
    ts = _tile(T, 512)

    def body(u_ref, v_ref, z_ref, lw_ref, lb_ref, ws_ref, bsx_ref, y_ref):
        def chunk(ci, carry):
            rows = pl.ds(pl.multiple_of(ci * Q, Q), Q)
            u, v, z = u_ref[rows, :].astype(f32), v_ref[rows, :].astype(f32), z_ref[rows, :].astype(f32)
            gu, _, _, _, mixed = _gm_chunk_fwd(u, v, z, lw_ref[...], lb_ref[...], ws_ref, bsx_ref[...])
            y_ref[rows, :] = (gu * mixed * _silu(z)).astype(y_ref.dtype)
            return carry

        lax.fori_loop(0, ts // Q, chunk, 0)

    return pl.pallas_call(
        body, name=name, grid=(T // ts,),
        in_specs=[_row_spec(ts, D, 0), _row_spec(ts, D, 1), _row_spec(ts, D, 2), _full_spec((1, D)), _full_spec((1, D)),
                  _full_spec((NG, Q, Q)), _full_spec((Q, D))],
        out_specs=_row_spec(ts, D), out_shape=jax.ShapeDtypeStruct((T, D), _MXU),
        compiler_params=_params(("arbitrary",)))(pA, pA, pA, lw, lb, ws, bsx)


def _gmlp_bwd(dya, pA, lw, lb, ws, bsx, name):
    T = pA.shape[0]
    ts = _tile(T, 512)

    def body(dy_ref, u_ref, v_ref, z_ref, lw_ref, lb_ref, ws_ref, bsx_ref, dp_ref, acc_ref, dws_ref, dbs_ref):
        @pl.when(pl.program_id(0) == 0)
        def _():
            acc_ref[...] = jnp.zeros_like(acc_ref)
            dws_ref[...] = jnp.zeros_like(dws_ref)
            dbs_ref[...] = jnp.zeros_like(dbs_ref)

        tri = _tril(Q)

        def chunk(ci, carry):
            rows = pl.ds(pl.multiple_of(ci * Q, Q), Q)
            u, v, z, dy = u_ref[rows, :].astype(f32), v_ref[rows, :].astype(f32), z_ref[rows, :].astype(f32), dy_ref[rows, :].astype(f32)
            (gu, d_gu), (gv, d_gv) = _gelu_and_grad(u), _gelu_and_grad(v)
            gu, vhat, rstd, vn, mixed = _gm_chunk_fwd(u, v, z, lw_ref[...], lb_ref[...], ws_ref, bsx_ref[...], gu, gv)
            sz = _silu(z)
            dp_ref[rows, 0:D] = (dy * mixed * sz * d_gu).astype(dp_ref.dtype)
            dp_ref[rows, 2 * D:3 * D] = (dy * gu * mixed * _dsilu(z)).astype(dp_ref.dtype)
            dmixed = dy * gu * sz
            dbs_ref[...] += dmixed
            dmb = _bf(dmixed)
            dvn_parts = []
            for g in range(NG):
                cols = slice(g * Q, (g + 1) * Q)
                wg = _bf(jnp.where(tri, ws_ref[g], 0.0))
                dvn_parts.append(_dot_tn(wg, dmb[:, cols]))
                dws_ref[g] += jnp.where(tri, _dot_nt(dmb[:, cols], vn[:, cols]), 0.0)
            dvn = jnp.concatenate(dvn_parts, axis=1)
            acc_ref[0:1, :] += jnp.sum(dvn * vhat, axis=0, keepdims=True)
            acc_ref[1:2, :] += jnp.sum(dvn, axis=0, keepdims=True)
            dvh = dvn * lw_ref[...]
            dgv = rstd * (dvh - jnp.mean(dvh, axis=-1, keepdims=True) - vhat * jnp.mean(dvh * vhat, axis=-1, keepdims=True))
            dp_ref[rows, D:2 * D] = (dgv * d_gv).astype(dp_ref.dtype)
            return carry

        lax.fori_loop(0, ts // Q, chunk, 0)

    return pl.pallas_call(
        body, name=name, grid=(T // ts,),
        in_specs=[_row_spec(ts, D), _row_spec(ts, D, 0), _row_spec(ts, D, 1), _row_spec(ts, D, 2), _full_spec((1, D)),
                  _full_spec((1, D)), _full_spec((NG, Q, Q)), _full_spec((Q, D))],
        out_specs=[_row_spec(ts, WA), _full_spec((8, D)), _full_spec((NG, Q, Q)), _full_spec((Q, D))],
        out_shape=[jax.ShapeDtypeStruct((T, WA), _MXU), jax.ShapeDtypeStruct((8, D), f32),
                   jax.ShapeDtypeStruct((NG, Q, Q), f32), jax.ShapeDtypeStruct((Q, D), f32)],
        compiler_params=_params(("arbitrary",)))(dya, pA, pA, pA, lw, lb, ws, bsx)


def _head_maps():
    h = lax.broadcasted_iota(jnp.int32, (128, DI), 0)
    ch = lax.broadcasted_iota(jnp.int32, (128, DI), 1)
    ex = (ch // P == h).astype(_MXU)
    return ex, ex.T


def _split(v, parts):
    out = []
    for _ in range(parts - 1):
        p = _bf(v)
        out.append(p)
        v = v - p.astype(f32)
    out.append(_bf(v))
    return out


def _expand(v, ex_ref, parts):
    acc = None
    for p in _split(v, parts):
        t = _dot(p, ex_ref[...])
        acc = t if acc is None else acc + t
    return acc


def _reduce(v, rd_ref, parts=2):
    acc = None
    for p in _split(v, parts):
        t = _dot(p, rd_ref[...])
        acc = t if acc is None else acc + t
    return acc


def _ssd_time(pb_ref, dtb_ref, alog_ref):
    xdt = pb_ref[:, OFF_DT:OFF_DT + 128] + dtb_ref[...]
    dt = _softplus(xdt)
    a = -jnp.exp(alog_ref[...])
    cs = _dot_exact(_tril(Q).astype(f32), dt * a)
    return xdt, dt, a, cs


def _head_mask(r):
    return lax.broadcasted_iota(jnp.int32, (Q, GRP), 1) // P == r


def _ssd_group_fwd(g, xa_ref, s_prev, cs, cs_t, dtx_ref, csx_ref, dsk_ref):
    cols = slice(g * GRP, (g + 1) * GRP)
    xs = xa_ref[:, cols]
    bb = _bf(xa_ref[:, DI + g * NS:DI + (g + 1) * NS])
    cb = _bf(xa_ref[:, DI + NG * NS + g * NS:DI + NG * NS + (g + 1) * NS])
    gm = _dot_nt(cb, bb)
    xd = xs * dtx_ref[:, cols]
    csx = csx_ref[:, cols]
    csl = csx_ref[Q - 1:Q, cols]
    tri = _tril(Q)
    lms = [jnp.exp(jnp.where(tri, cs[:, HPG * g + r:HPG * g + r + 1] - cs_t[HPG * g + r:HPG * g + r + 1, :], NEG))
           for r in range(HPG)]
    mfs = [gm * lm for lm in lms]
    mcat = jnp.concatenate([_bf(m) for m in mfs], axis=1)
    xbd = jnp.concatenate([_bf(jnp.where(_head_mask(r), xd, 0.0)) for r in range(HPG)], axis=0)
    ydiag = _dot(mcat, xbd)
    ecs, dte, ecl = jnp.exp(csx), jnp.exp(csl - csx), jnp.exp(csl)
    yoff = ecs * _dot(cb, _bf(s_prev))
    y = ydiag + yoff + xs * dsk_ref[:, cols]
    xdd = xd * dte
    s_new = s_prev * ecl + _dot_tn(bb, _bf(xdd))
    return y, s_new, (xs, bb, cb, xd, lms, mfs, mcat, xbd, yoff, xdd, ecs, dte, ecl)


def _ssd_fwd(pB, cw, cb, dtb, alog, dsk, snw, name, carry=()):
    T = pB.shape[0]
    nc = T // Q
    ex, _ = _head_maps()

    def body(pb_ref, cw_ref, cb_ref, dtb_ref, alog_ref, dsk_ref, snw_ref, ex_ref, y_ref, xc_ref, st_ref,
             s_ref, ext_ref, xa_ref, dtx_ref, csx_ref):
        @pl.when(pl.program_id(0) == 0)
        def _():
            s_ref[...] = jnp.zeros_like(s_ref)
            ext_ref[...] = jnp.zeros_like(ext_ref)

        ext_ref[8:8 + Q, :] = pb_ref[:, OFF_XBC:OFF_XBC + CD]
        for j in range(CD // 512):
            cj = slice(j * 512, (j + 1) * 512)
            e = ext_ref[:, cj]
            xc = cb_ref[:, cj] + cw_ref[KC - 1:KC, cj] * e[8:8 + Q]
            for s in range(1, KC):
                xc = xc + cw_ref[KC - 1 - s:KC - s, cj] * pltpu.roll(e, s, 0)[8:8 + Q]
            xc_ref[:, cj] = xc
            xa_ref[:, cj] = _silu(xc)
        ext_ref[0:8, :] = ext_ref[Q:Q + 8, :]

        _, dt, _, cs = _ssd_time(pb_ref, dtb_ref, alog_ref)
        cs_t = cs.T
        dtx_ref[...] = _expand(dt, ex_ref, 2)
        csx_ref[...] = _expand(cs, ex_ref, 3)
        for g in range(NG):
            s_prev = s_ref[g]
            st_ref[0, g] = s_prev
            y, s_new, _ = _ssd_group_fwd(g, xa_ref, s_prev, cs, cs_t, dtx_ref, csx_ref, dsk_ref)
            s_ref[g] = s_new
            cols = slice(g * GRP, (g + 1) * GRP)
            yz = y * _silu(pb_ref[:, cols])
            rr = lax.rsqrt(jnp.mean(yz * yz, axis=-1, keepdims=True) + EPS)
            y_ref[:, cols] = (yz * rr * snw_ref[:, cols]).astype(y_ref.dtype)

    outs, exchanged = _carry_call(
        body, name, (nc,),
        [_row_spec(Q, WB), _full_spec((8, CD)), _full_spec((1, CD)), _full_spec((1, 128)), _full_spec((1, 128)),
         _full_spec((1, DI)), _full_spec((1, DI)), _full_spec((128, DI))],
        [_row_spec(Q, DI), _row_spec(Q, CD), pl.BlockSpec((1, NG, NS, GRP), lambda i: (i, 0, 0, 0))],
        [jax.ShapeDtypeStruct((T, DI), _MXU), jax.ShapeDtypeStruct((T, CD), f32), jax.ShapeDtypeStruct((nc, NG, NS, GRP), f32)],
        [pltpu.VMEM((NG, NS, GRP), f32), pltpu.VMEM((Q + 8, CD), f32), pltpu.VMEM((Q, CD), f32),
         pltpu.VMEM((Q, DI), f32), pltpu.VMEM((Q, DI), f32)],
        (pB, cw, cb, dtb, alog, dsk, snw, ex), carry)
    return outs + [exchanged]


def _ssd_bwd(dyb, pB, xc, states, cw, dtb, alog, dsk, snw, name, carry=()):
    T = pB.shape[0]
    nc = T // Q
    ex, rd = _head_maps()
    selr = (lax.broadcasted_iota(jnp.int32, (HPG * Q, 128), 0) // Q == lax.broadcasted_iota(jnp.int32, (HPG * Q, 128), 1)).astype(_MXU)

    def body(dy_ref, pb_ref, xc_ref, st_ref, cw_ref, dtb_ref, alog_ref, dsk_ref, snw_ref, ex_ref, rd_ref, selr_ref,
             dp_ref, dcw_ref, dcb_ref, dhd_ref, dcol_ref,
             ds_ref, xa_ref, sg_ref, dxa_ref, dxe_ref, dtx_ref, csx_ref, rcs_ref, rdt_ref, rl_ref, dcs_ref, dcst_ref):
        @pl.when(pl.program_id(0) == 0)
        def _():
            ds_ref[...] = jnp.zeros_like(ds_ref)
            dxe_ref[...] = jnp.zeros_like(dxe_ref)
            dcw_ref[...] = jnp.zeros_like(dcw_ref)
            dcb_ref[...] = jnp.zeros_like(dcb_ref)
            dhd_ref[...] = jnp.zeros_like(dhd_ref)
            dcol_ref[...] = jnp.zeros_like(dcol_ref)
            rl_ref[...] = jnp.zeros_like(rl_ref)
            dcst_ref[...] = jnp.zeros_like(dcst_ref)

        dcs_ref[...] = jnp.zeros_like(dcs_ref)
        for j in range(CD // 512):
            cj = slice(j * 512, (j + 1) * 512)
            sg_ref[:, cj] = _sigmoid(xc_ref[:, cj])
            xa_ref[:, cj] = xc_ref[:, cj] * sg_ref[:, cj]
        xdt, dt, a, cs = _ssd_time(pb_ref, dtb_ref, alog_ref)
        cs_t = cs.T
        dtx_ref[...] = _expand(dt, ex_ref, 2)
        csx_ref[...] = _expand(cs, ex_ref, 3)

        for g in range(NG):
            cols = slice(g * GRP, (g + 1) * GRP)
            s_prev = st_ref[0, g]
            sb = _bf(s_prev)
            y, _, (xs, bb, cbm, xd, lms, mfs, mcat, xbd, yoff, xdd, ecs, dte, ecl) = _ssd_group_fwd(
                g, xa_ref, s_prev, cs, cs_t, dtx_ref, csx_ref, dsk_ref)
            z = pb_ref[:, cols]
            sz = _silu(z)
            yz = y * sz
            rr = lax.rsqrt(jnp.mean(yz * yz, axis=-1, keepdims=True) + EPS)
            nrm = yz * rr
            dyb_g = dy_ref[:, cols].astype(f32)
            dcol_ref[1:2, cols] += jnp.sum(dyb_g * nrm, axis=0, keepdims=True)
            dn = dyb_g * snw_ref[:, cols]
            dyz = rr * (dn - nrm * jnp.mean(dn * nrm, axis=-1, keepdims=True))
            dyv = dyz * sz
            dp_ref[:, cols] = (dyz * y * _dsilu(z)).astype(dp_ref.dtype)
            dcol_ref[0:1, cols] += jnp.sum(dyv * xs, axis=0, keepdims=True)
            dy16 = _bf(dyv)
            dmcat = _dot_nt(dy16, xbd)
            dg = dmcat[:, 0:Q] * lms[0]
            for r in range(1, HPG):
                dg = dg + dmcat[:, r * Q:(r + 1) * Q] * lms[r]
            e16 = _bf(jnp.concatenate([dmcat[:, r * Q:(r + 1) * Q] * mfs[r] for r in range(HPG)], axis=1))
            rows = _dot(e16, selr_ref[...])
            dcs_ref[...] += rows if g == 0 else pltpu.roll(rows, HPG * g, 1)
            csum = _dot(jnp.ones((8, Q), _MXU), e16)
            for r in range(HPG):
                dcst_ref[HPG * g + r:HPG * g + r + 1, :] = csum[0:1, r * Q:(r + 1) * Q]
            big = _dot_tn(mcat, dy16)
            dxd_diag = jnp.where(_head_mask(0), big[0:Q], 0.0)
            for r in range(1, HPG):
                dxd_diag = dxd_diag + jnp.where(_head_mask(r), big[r * Q:(r + 1) * Q], 0.0)
            dsn = ds_ref[g]
            dsn16 = _bf(dsn)
            dxdd = _dot(bb, dsn16)
            t_state = dxdd * xdd
            dxd = dxd_diag + dxdd * dte
            rcs_ref[:, cols] = dyv * yoff - t_state
            rl_ref[0:1, cols] = (jnp.sum(t_state, axis=0, keepdims=True)
                                 + ecl * jnp.sum(s_prev * dsn, axis=0, keepdims=True))
            rdt_ref[:, cols] = dxd * xs
            dxa_ref[:, cols] = dyv * dsk_ref[:, cols] + dxd * dtx_ref[:, cols]
            dg16 = _bf(dg)
            dw16 = _bf(dyv * ecs)
            dxa_ref[:, DI + NG * NS + g * NS:DI + NG * NS + (g + 1) * NS] = _dot(dg16, bb) + _dot_nt(dw16, sb)
            dxa_ref[:, DI + g * NS:DI + (g + 1) * NS] = _dot_tn(dg16, cbm) + _dot_nt(_bf(xdd), dsn16)
            ds_ref[g] = dsn * ecl + _dot_tn(cbm, dw16)

        row = lax.broadcasted_iota(jnp.int32, (Q, 128), 0)
        dcs = (dcs_ref[...] - dcst_ref[...].T + _reduce(rcs_ref[...], rd_ref)
               + jnp.where(row == Q - 1, _reduce(rl_ref[...], rd_ref)[0:1, :], 0.0))
        upper = lax.broadcasted_iota(jnp.int32, (Q, Q), 0) <= lax.broadcasted_iota(jnp.int32, (Q, Q), 1)
        dadt = _dot_exact(upper.astype(f32), dcs)
        valid = lax.broadcasted_iota(jnp.int32, (Q, 128), 1) < NH
        ddt = jnp.where(valid, _reduce(rdt_ref[...], rd_ref, 1) + dadt * a, 0.0)
        ddtr = ddt * _sigmoid(xdt)
        dhd_ref[0:1, :] += jnp.sum(ddtr, axis=0, keepdims=True)
        dhd_ref[1:2, :] += jnp.sum(jnp.where(valid, dadt * dt * a, 0.0), axis=0, keepdims=True)
        dp_ref[:, OFF_DT:OFF_DT + 128] = ddtr.astype(dp_ref.dtype)
        dp_ref[:, OFF_DT + 128:WB] = jnp.zeros((Q, WB - OFF_DT - 128), dp_ref.dtype)
        for j in range(CD // 512):
            cj = slice(j * 512, (j + 1) * 512)
            sg = sg_ref[:, cj]
            dxc = dxa_ref[:, cj] * (sg + xa_ref[:, cj] * (1.0 - sg))
            dxe_ref[0:Q, cj] = dxc
            dcb_ref[0:1, cj] += jnp.sum(dxc, axis=0, keepdims=True)
            raw = pb_ref[:, OFF_XBC + j * 512:OFF_XBC + (j + 1) * 512]
            e = dxe_ref[:, cj]
            dcw_ref[KC - 1:KC, cj] += jnp.sum(dxc * raw, axis=0, keepdims=True)
            dxb = cw_ref[KC - 1:KC, cj] * dxc
            for s in range(1, KC):
                sh = pltpu.roll(e, Q + 8 - s, 0)[0:Q]
                dcw_ref[KC - 1 - s:KC - s, cj] += jnp.sum(sh * raw, axis=0, keepdims=True)
                dxb = dxb + cw_ref[KC - 1 - s:KC - s, cj] * sh
            dp_ref[:, OFF_XBC + j * 512:OFF_XBC + (j + 1) * 512] = dxb.astype(dp_ref.dtype)
        dxe_ref[Q:Q + 8, :] = dxe_ref[0:8, :]

    rev = lambda i: (nc - 1 - i, 0)
    outs, exchanged = _carry_call(
        body, name, (nc,),
        [pl.BlockSpec((Q, DI), rev), pl.BlockSpec((Q, WB), rev), pl.BlockSpec((Q, CD), rev),
         pl.BlockSpec((1, NG, NS, GRP), lambda i: (nc - 1 - i, 0, 0, 0)),
         _full_spec((8, CD)), _full_spec((1, 128)), _full_spec((1, 128)),
         _full_spec((1, DI)), _full_spec((1, DI)), _full_spec((128, DI)), _full_spec((DI, 128)), _full_spec((HPG * Q, 128))],
        [pl.BlockSpec((Q, WB), rev), _full_spec((8, CD)), _full_spec((8, CD)), _full_spec((8, 128)), _full_spec((8, DI))],
        [jax.ShapeDtypeStruct((T, WB), _MXU), jax.ShapeDtypeStruct((8, CD), f32), jax.ShapeDtypeStruct((8, CD), f32),
         jax.ShapeDtypeStruct((8, 128), f32), jax.ShapeDtypeStruct((8, DI), f32)],
        [pltpu.VMEM((NG, NS, GRP), f32), pltpu.VMEM((Q, CD), f32), pltpu.VMEM((Q, CD), f32), pltpu.VMEM((Q, CD), f32),
         pltpu.VMEM((Q + 8, CD), f32), pltpu.VMEM((Q, DI), f32), pltpu.VMEM((Q, DI), f32),
         pltpu.VMEM((Q, DI), f32), pltpu.VMEM((Q, DI), f32), pltpu.VMEM((8, DI), f32),
         pltpu.VMEM((Q, 128), f32), pltpu.VMEM((128, Q), f32)],
        (dyb, pB, xc, states, cw, dtb, alog, dsk, snw, ex, rd, selr), carry)
    return outs + [exchanged]


def _merge_fwd(ya, yb, pG, x, gate, wa, wb, wo, name, carry=()):
    T = x.shape[0]
    ts = _tile(T, 512)

    def body(ya_ref, yb_ref, g_ref, x_ref, gate_ref, wa_ref, wb_ref, wo_ref, xo_ref, mg_ref, pa_ref, pb_ref):
        pa = _dot(ya_ref[...], wa_ref[...])
        pb = _dot(yb_ref[...], wb_ref[...])
        merged = _sigmoid(g_ref[:, 0:D].astype(f32)) * pa + _sigmoid(g_ref[:, D:2 * D].astype(f32)) * pb
        mg = _bf(merged)
        xo_ref[...] = x_ref[...] + gate_ref[...] * _dot(mg, wo_ref[...])
        mg_ref[...] = mg
        pa_ref[...] = pa.astype(pa_ref.dtype)
        pb_ref[...] = pb.astype(pb_ref.dtype)

    outs, exchanged = _carry_call(
        body, name, (T // ts,),
        [_row_spec(ts, D), _row_spec(ts, DI), _row_spec(ts, WG), _row_spec(ts, D), _full_spec((1, D)),
         _full_spec((D, D)), _full_spec((DI, D)), _full_spec((D, D))],
        [_row_spec(ts, D)] * 4,
        [jax.ShapeDtypeStruct((T, D), f32), jax.ShapeDtypeStruct((T, D), _MXU), jax.ShapeDtypeStruct((T, D), _MXU),
         jax.ShapeDtypeStruct((T, D), _MXU)],
        [], (ya, yb, pG, x, gate, wa, wb, wo), carry)
    return outs + [exchanged]


def _merge_bwd(dxout, merged, pa, pb, pG, gate, wo, name):
    T = dxout.shape[0]
    ts = _tile(T, 512)

    def body(dx_ref, mg_ref, pa_ref, pb_ref, g_ref, gate_ref, wo_ref, do_ref, dpa_ref, dpb_ref, dg_ref, acc_ref):
        @pl.when(pl.program_id(0) == 0)
        def _():
            acc_ref[...] = jnp.zeros_like(acc_ref)

        dxo = dx_ref[...]
        acc_ref[0:1, :] += jnp.sum(dxo * _dot(mg_ref[...], wo_ref[...]), axis=0, keepdims=True)
        do = _bf(dxo * gate_ref[...])
        do_ref[...] = do
        dmerged = _dot_nt(do, wo_ref[...])
        sa, sb = _sigmoid(g_ref[:, 0:D].astype(f32)), _sigmoid(g_ref[:, D:2 * D].astype(f32))
        dpa_ref[...] = (dmerged * sa).astype(dpa_ref.dtype)
        dpb_ref[...] = (dmerged * sb).astype(dpb_ref.dtype)
        dg_ref[:, 0:D] = (dmerged * pa_ref[...].astype(f32) * sa * (1.0 - sa)).astype(dg_ref.dtype)
        dg_ref[:, D:2 * D] = (dmerged * pb_ref[...].astype(f32) * sb * (1.0 - sb)).astype(dg_ref.dtype)

    return pl.pallas_call(
        body, name=name, grid=(T // ts,),
        in_specs=[_row_spec(ts, D), _row_spec(ts, D), _row_spec(ts, D), _row_spec(ts, D), _row_spec(ts, WG),
                  _full_spec((1, D)), _full_spec((D, D))],
        out_specs=[_row_spec(ts, D), _row_spec(ts, D), _row_spec(ts, D), _row_spec(ts, WG), _full_spec((8, D))],
        out_shape=[jax.ShapeDtypeStruct((T, D), _MXU), jax.ShapeDtypeStruct((T, D), _MXU), jax.ShapeDtypeStruct((T, D), _MXU),
                   jax.ShapeDtypeStruct((T, WG), _MXU), jax.ShapeDtypeStruct((8, D), f32)],
        compiler_params=_params(("arbitrary",)))(dxout, merged, pa, pb, pG, gate, wo)


def _rows_split(r):
    ra, rb = D // NDEV, DI // NDEV
    return r[:, 0:ra].reshape(D, D), r[:, ra:ra + rb].reshape(DI, D), r[:, ra + rb:].reshape(D, D)


def _layer_fwd(x, mod, lp, tag, carry=None, h=None):
    carry = carry or {}
    got = {}

    def mm(key, w, **kw):
        if carry.get(key):
            out, got[key] = _mm(h, w, "nn", f"{key}_{tag}", carry=carry[key], **kw)
            return out
        return _mm(h, w, "nn", f"{key}_{tag}", **kw)

    if h is None:
        h = _modulate(x, lp["nw"], mod[0:1], mod[1:2], f"modulate_{tag}")
    pA = mm("proj_gm", lp["w_gm"], out_dtype=_MXU)
    pB = mm("proj_ssd", lp["w_ssd"])
    pG = mm("proj_gate", lp["w_g"], out_dtype=_MXU)
    if "wa" not in lp:
        lp = dict(lp)
        lp["wa"], lp["wb"], lp["wo"] = _rows_split(jnp.concatenate([got["proj_gm"][0], got["proj_gate"][0]], axis=1))
    ya = _gmlp_fwd(pA, lp["lw"], lp["lb"], lp["ws"], lp["bsx"], f"gmlp_fwd_{tag}")
    yb, xc, states, got["ssd_fwd"] = _ssd_fwd(pB, lp["cw"], lp["cb"], lp["dtb"], lp["alog"], lp["dsk"], lp["snw"], f"ssd_fwd_{tag}",
                                              carry.get("ssd_fwd", ()))
    xo, merged, pa, pb, _ = _merge_fwd(ya, yb, pG, x, mod[2:3], lp["wa"], lp["wb"], lp["wo"], f"merge_fwd_{tag}")
    return xo, dict(x=x, h=h, pA=pA, pB=pB, pG=pG, ya=ya, yb=yb, xc=xc, states=states, merged=merged, pa=pa, pb=pb), got, lp


def _dh_modulate_bwd(dpG, dpA, dpB, w_g, w_gm, w_ssd, dxout, x, nw, scale, name, carry=()):
    T = x.shape[0]
    tm = _tile(T, 1024)
    fam = [(WG, _tile(WG, 1024)), (WA, _tile(WA, 1024)), (WB, _tile(WB, 1280))]
    steps = [w // k for w, k in fam]
    first = [0, steps[0], steps[0] + steps[1]]
    ns = sum(steps)

    def chunk(f):
        return lambda i, s: jnp.clip(s - first[f], 0, steps[f] - 1)

    a_specs = [pl.BlockSpec((tm, fam[f][1]), lambda i, s, c=chunk(f): (i, c(i, s))) for f in range(3)]
    b_specs = [pl.BlockSpec((D, fam[f][1]), lambda i, s, c=chunk(f): (0, c(i, s))) for f in range(3)]
    tok = pl.BlockSpec((tm, D), lambda i, s: (i, 0))
    tok_once = pl.BlockSpec((tm, D), lambda i, s: (i, 0), pipeline_mode=pl.Buffered(1))
    vec = pl.BlockSpec((1, D), lambda i, s: (0, 0))

    def body(ag, aa, ab, bg, ba, bb, dxo_ref, x_ref, nw_ref, sc_ref, dx_ref, sum_ref, acc_ref):
        i, s = pl.program_id(0), pl.program_id(1)

        @pl.when(jnp.logical_and(i == 0, s == 0))
        def _():
            sum_ref[...] = jnp.zeros_like(sum_ref)

        @pl.when(s == 0)
        def _():
            acc_ref[...] = jnp.zeros_like(acc_ref)

        @pl.when(s < first[1])
        def _():
            acc_ref[...] += _dot_nt(ag[...], bg[...])

        @pl.when(jnp.logical_and(s >= first[1], s < first[2]))
        def _():
            acc_ref[...] += _dot_nt(aa[...], ba[...])

        @pl.when(s >= first[2])
        def _():
            acc_ref[...] += _dot_nt(ab[...], bb[...])

        @pl.when(s == ns - 1)
        def _():
            xv, dh_v = x_ref[...], acc_ref[...]
            r = lax.rsqrt(jnp.mean(xv * xv, axis=-1, keepdims=True) + EPS)
            xn = xv * r
            hn = xn * nw_ref[...]
            dhn = dh_v * (1.0 + sc_ref[...])
            sum_ref[0:1, :] += jnp.sum(dh_v, axis=0, keepdims=True)
            sum_ref[1:2, :] += jnp.sum(dh_v * hn, axis=0, keepdims=True)
            sum_ref[2:3, :] += jnp.sum(dhn * xn, axis=0, keepdims=True)
            dxn = dhn * nw_ref[...]
            dx_ref[...] = dxo_ref[...] + r * (dxn - xn * jnp.mean(dxn * xn, axis=-1, keepdims=True))

    outs, exchanged = _carry_call(
        body, name, (T // tm, ns), a_specs + b_specs + [tok_once, tok_once, vec, vec], [tok, pl.BlockSpec((8, D), lambda i, s: (0, 0))],
        [jax.ShapeDtypeStruct((T, D), f32), jax.ShapeDtypeStruct((8, D), f32)], [pltpu.VMEM((tm, D), f32)],
        (dpG, dpA, dpB, w_g, w_gm, w_ssd, dxout, x, nw, scale), carry)
    return outs + [exchanged]


def _win_blocks(g):
    full = jnp.concatenate([g["w_gm"], g["w_ssd"][:, 0:DI + CD], g["w_ssd"][:, DI + CD:DI + CD + NH], g["w_g"]], axis=1)
    return jnp.transpose(full.reshape(D, NDEV, NIN // NDEV), (1, 0, 2))


def _row_blocks(g):
    return jnp.concatenate([g["wa"].reshape(NDEV, D // NDEV, D), g["wb"].reshape(NDEV, DI // NDEV, D),
                            g["wo"].reshape(NDEV, D // NDEV, D)], axis=1)


def _layer_bwd(dxo, sv, mod, lp, tag, carry=(), scatter_own=None):
    do, dpa, dpb, dpG, s_gate = _merge_bwd(dxo, sv["merged"], sv["pa"], sv["pb"], sv["pG"], mod[2:3], lp["wo"], f"merge_bwd_{tag}")
    g = {}
    g["wo"] = _mm(sv["merged"], do, "tn", f"dw_out_{tag}", out_dtype=_WIRE)
    g["wa"] = _mm(sv["ya"], dpa, "tn", f"dw_proj_a_{tag}", out_dtype=_WIRE)
    g["wb"] = _mm(sv["yb"], dpb, "tn", f"dw_proj_b_{tag}", out_dtype=_WIRE)
    dya = _mm(dpa, lp["wa"], "nt", f"dy_a_{tag}", out_dtype=_MXU)
    dyb = _mm(dpb, lp["wb"], "nt", f"dy_b_{tag}", out_dtype=_MXU)
    dpA, s_ln, g["ws"], dbsx = _gmlp_bwd(dya, sv["pA"], lp["lw"], lp["lb"], lp["ws"], lp["bsx"], f"gmlp_bwd_{tag}")
    carry = list(carry) + ([(_row_blocks(g), False)] if scatter_own else [])
    dpB, g["cw"], s_cb, s_hd, s_col, got = _ssd_bwd(dyb, sv["pB"], sv["xc"], sv["states"], lp["cw"], lp["dtb"], lp["alog"],
                                                    lp["dsk"], lp["snw"], f"ssd_bwd_{tag}", carry)
    g["lw"], g["lb"] = s_ln[0], s_ln[1]
    g["bs"] = dbsx.reshape(Q, NG, Q).sum(-1).T
    g["cb"] = s_cb[0]
    g["dtb"], g["alog"] = s_hd[0, :NH], s_hd[1, :NH]
    g["dsk"] = s_col[0].reshape(NH, P).sum(-1)
    g["snw"] = s_col[1]
    g["cw"] = g["cw"][0:KC]
    g["w_g"] = _mm(sv["h"], dpG, "tn", f"dw_gate_{tag}", out_dtype=_WIRE)
    g["w_gm"] = _mm(sv["h"], dpA, "tn", f"dw_gm_{tag}", out_dtype=_WIRE)
    if scatter_own:
        g["w_ssd"], got_slab = _mm(sv["h"], dpB, "tn", f"dw_ssd_{tag}", out_dtype=_WIRE, carry=[(scatter_own(g), True)])
    else:
        g["w_ssd"] = _mm(sv["h"], dpB, "tn", f"dw_ssd_{tag}", out_dtype=_WIRE)
    dx, s_mod, got_dh = _dh_modulate_bwd(dpG, dpA, dpB, lp["w_g"], lp["w_gm"], lp["w_ssd"], dxo, sv["x"], lp["nw"], mod[1:2],
                                         f"dh_{tag}", [(_win_blocks(g), False)] if scatter_own else ())
    if scatter_own:
        got, got_own = got[:-1], [got[-1], got_dh[0], got_slab[0]]
    else:
        got_own = []
    g["mod"] = jnp.concatenate([s_mod[0], s_mod[1], s_gate[0]])
    g["nw"] = s_mod[2]
    return dx, g, got, got_own


def _prep_layer(nw, w_in_full, lw, lb, ws, bs, cw_full, cb, dtb, alog, dsk, snw, rows=None):
    z = jnp.zeros((D, WB - (DI + CD + NH)), w_in_full.dtype)
    pad_h = lambda v: jnp.pad(v, (0, 128 - NH)).reshape(1, 128)
    extra = dict(zip(("wa", "wb", "wo"), _rows_split(rows))) if rows is not None else {}
    return dict(
        **extra,
        nw=nw.reshape(1, D),
        w_gm=w_in_full[:, 0:WA],
        w_ssd=jnp.concatenate([w_in_full[:, WA:WA + DI + CD], w_in_full[:, WA + DI + CD:WA + DI + CD + NH], z], axis=1),
        w_g=w_in_full[:, WA + DI + CD + NH:NIN],
        lw=lw.reshape(1, D), lb=lb.reshape(1, D), ws=ws, bsx=jnp.repeat(bs.T, Q, axis=1),
        cw=jnp.pad(cw_full, ((0, 8 - KC), (0, 0))), cb=cb.reshape(1, CD), dtb=pad_h(dtb), alog=pad_h(alog),
        dsk=jnp.repeat(dsk, P).reshape(1, DI), snw=snw.reshape(1, DI))


def _mod_dist(c8, ada_w, ada_b_cols, name):
    L, _, AW = ada_w.shape

    def body(c_ref, w_ref, b_ref, parts_ref, sc_ref, call_ref, mine_ref, send1, recv1, send2, recv2):
        x, y, c = lax.axis_index("x"), lax.axis_index("y"), lax.axis_index("c")
        me = 4 * x + 2 * y + c

        def peer(j):
            px = 1 - x if (j >> 2) & 1 else x
            py = 1 - y if (j >> 1) & 1 else y
            pc = 1 - c if j & 1 else c
            return (px, py, pc), 4 * px + 2 * py + pc

        def copy(j, src, dst, sems, landing):
            dev, idx = peer(j)
            return pltpu.make_async_remote_copy(
                src_ref=src, dst_ref=dst.at[idx] if landing else dst.at[me], send_sem=sems[0].at[j - 1], recv_sem=sems[1].at[j - 1],
                device_id=dev, device_id_type=pl.DeviceIdType.MESH)

        def all_to_all(src, dst, sems):
            for j in range(1, NDEV):
                copy(j, src, dst, sems, False).start()
            for j in range(1, NDEV):
                copy(j, src, dst, sems, True).wait_recv()
            for j in range(1, NDEV):
                copy(j, src, dst, sems, False).wait_send()

        call_ref[me] = c_ref[...]
        all_to_all(c_ref, call_ref, (send1, recv1))
        row = lax.broadcasted_iota(jnp.int32, (8, D), 0)
        cm = jnp.zeros((8, D), f32)
        for k in range(NDEV):
            cm = jnp.where(row == k, call_ref[k], cm)
        sc = _silu(cm)
        sc_ref[...] = sc
        for l in range(L):
            mine_ref[l] = _dot(_bf(sc), w_ref[l]) + b_ref[l]
        parts_ref[me] = mine_ref[...]
        all_to_all(mine_ref, parts_ref, (send2, recv2))

    vmem = pl.BlockSpec(memory_space=pltpu.VMEM)
    sems = pltpu.SemaphoreType.DMA((NDEV - 1,))
    return pl.pallas_call(
        body, name=name, in_specs=[vmem, vmem, vmem], out_specs=[vmem, vmem],
        out_shape=[jax.ShapeDtypeStruct((NDEV, L, 8, AW), f32), jax.ShapeDtypeStruct((8, D), f32)],
        scratch_shapes=[pltpu.VMEM((NDEV, 8, D), f32), pltpu.VMEM((L, 8, AW), f32), sems, sems, sems, sems])(c8, ada_w, ada_b_cols)


def _ada_w_grad(sc_all, dmod_cols, name):
    W = dmod_cols.shape[1]

    def body(s_ref, d_ref, o_ref):
        o_ref[...] = lax.dot_general(s_ref[...], d_ref[...], (((0,), (0,)), ((), ())), preferred_element_type=f32,
                                     precision=lax.Precision.HIGHEST)

    return pl.pallas_call(body, name=name, out_shape=jax.ShapeDtypeStruct((D, W), f32))(sc_all, dmod_cols)


def _adamw_math(w, g, m, v):
    m = ADAM_B1 * m + (1.0 - ADAM_B1) * g
    v = ADAM_B2 * v + (1.0 - ADAM_B2) * (g * g)
    m_hat = m / (1.0 - ADAM_B1 ** ADAM_STEP)
    v_hat = v / (1.0 - ADAM_B2 ** ADAM_STEP)
    delta = -ADAM_LR * (m_hat / (jnp.sqrt(v_hat) + ADAM_EPS) + ADAM_WD * w)
    return delta, m, v


def _sum_adamw(recvs, w, m, v, name, carry=()):
    nl = len(recvs)
    n, R, C = recvs[0].shape
    tr = _tile(R, 128 if C > 1024 else 256)
    nr = R // tr
    stacked = w.ndim == 3
    at = (lambda ref: ref.at[0]) if stacked else (lambda ref: ref)

    def body(*refs):
        r_refs, (w_ref, m_ref, v_ref, g_ref, d_ref, nm_ref, nv_ref) = refs[:nl], [at(r) for r in refs[nl:]]
        for li in range(nl):
            @pl.when(pl.program_id(0) == li)
            def _(r_ref=r_refs[li]):
                g = r_ref[0].astype(f32)
                for k in range(1, n):
                    g = g + r_ref[k].astype(f32)
                g_ref[...] = g
                d_ref[...], nm_ref[...], nv_ref[...] = _adamw_math(w_ref[...], g, m_ref[...], v_ref[...])

    if stacked:
        spec = pl.BlockSpec((1, tr, C), lambda l, i: (l, i, 0))
    else:
        spec = pl.BlockSpec((tr, C), lambda l, i: (l * nr + i, 0))
    r_specs = [pl.BlockSpec((n, tr, C), lambda l, i, li=li: (0, jnp.clip(i + (l - li) * nr, 0, nr - 1), 0)) for li in range(nl)]
    outs, exchanged = _carry_call(body, name, (nl, nr), r_specs + [spec, spec, spec], [spec] * 4,
                                  [jax.ShapeDtypeStruct(w.shape, f32)] * 4, [], (*recvs, w, m, v), carry)
    return (outs, exchanged) if carry else outs


def _pack(arrays, rows):
    flat = []
    for a in arrays:
        a = a.reshape(-1).astype(f32)
        flat.append(jnp.pad(a, (0, (-a.shape[0]) % 128)))
    flat = jnp.concatenate(flat)
    return jnp.pad(flat, (0, rows * 128 - flat.shape[0])).reshape(rows, 128)


def _unpack(slab, shapes):
    flat = slab.reshape(-1)
    out, off = [], 0
    for s in shapes:
        n = 1
        for d in s:
            n *= d
        out.append(flat[off:off + n].reshape(s))
        off += n + (-n) % 128
    return out


def kernel(x, c, ada_w, ada_b, norm_w, w_in, gm_ln_w, gm_ln_b, gm_ws, gm_bs, conv_w, conv_b, dt_bias, a_log, d_skip, ssm_norm_w, w_proj_a, w_proj_b, w_out, final_norm_w, loss_target, m_ada_w, m_ada_b, m_norm_w, m_w_in, m_gm_ln_w, m_gm_ln_b, m_gm_ws, m_gm_bs, m_conv_w, m_conv_b, m_dt_bias, m_a_log, m_d_skip, m_ssm_norm_w, m_w_proj_a, m_w_proj_b, m_w_out, m_final_norm_w, v_ada_w, v_ada_b, v_norm_w, v_w_in, v_gm_ln_w, v_gm_ln_b, v_gm_ws, v_gm_bs, v_conv_w, v_conv_b, v_dt_bias, v_a_log, v_d_skip, v_ssm_norm_w, v_w_proj_a, v_w_proj_b, v_w_out, v_final_norm_w):
    L = 2
    me = 4 * lax.axis_index("x") + 2 * lax.axis_index("y") + lax.axis_index("c")
    W = dict(ada_w=ada_w, ada_b=ada_b, norm_w=norm_w, w_in=w_in, gm_ln_w=gm_ln_w, gm_ln_b=gm_ln_b, gm_ws=gm_ws, gm_bs=gm_bs,
             conv_w=conv_w, conv_b=conv_b, dt_bias=dt_bias, a_log=a_log, d_skip=d_skip, ssm_norm_w=ssm_norm_w, w_proj_a=w_proj_a,
             w_proj_b=w_proj_b, w_out=w_out, final_norm_w=final_norm_w)
    M = dict(ada_w=m_ada_w, ada_b=m_ada_b, norm_w=m_norm_w, w_in=m_w_in, gm_ln_w=m_gm_ln_w, gm_ln_b=m_gm_ln_b, gm_ws=m_gm_ws,
             gm_bs=m_gm_bs, conv_w=m_conv_w, conv_b=m_conv_b, dt_bias=m_dt_bias, a_log=m_a_log, d_skip=m_d_skip,
             ssm_norm_w=m_ssm_norm_w, w_proj_a=m_w_proj_a, w_proj_b=m_w_proj_b, w_out=m_w_out, final_norm_w=m_final_norm_w)
    V = dict(ada_w=v_ada_w, ada_b=v_ada_b, norm_w=v_norm_w, w_in=v_w_in, gm_ln_w=v_gm_ln_w, gm_ln_b=v_gm_ln_b, gm_ws=v_gm_ws,
             gm_bs=v_gm_bs, conv_w=v_conv_w, conv_b=v_conv_b, dt_bias=v_dt_bias, a_log=v_a_log, d_skip=v_d_skip,
             ssm_norm_w=v_ssm_norm_w, w_proj_a=v_w_proj_a, w_proj_b=v_w_proj_b, w_out=v_w_out, final_norm_w=v_final_norm_w)
    SW = NIN // NDEV
    AW = 3 * D // NDEV
    CW = CD // NDEV
    RA, RB = D // NDEV, DI // NDEV

    wire = lambda a: a.astype(_WIRE)
    rows_of = lambda d, l: jnp.concatenate([d["w_proj_a"][l], d["w_proj_b"][l], d["w_out"][l]], axis=0)
    parts, sc_all = _mod_dist(jnp.broadcast_to(c, (8, D)), wire(ada_w),
                              lax.dynamic_slice_in_dim(ada_b, me * AW, AW, axis=1).reshape(L, 1, AW), "mod_dist")
    mods = [lax.dynamic_index_in_dim(parts[:, l], me, axis=1, keepdims=False).reshape(3, D) for l in range(L)]
    cols = lambda g: jnp.transpose(g, (1, 0, 2)).reshape(g.shape[1], -1)

    def prep(l, g_win, g_rows):
        return _prep_layer(norm_w[l], cols(g_win), gm_ln_w[l], gm_ln_b[l], gm_ws[l], gm_bs[l], cols(g_cw[:, l]), conv_b[l],
                           dt_bias[l], a_log[l], d_skip[l], ssm_norm_w[l], g_rows)

    half = D // 2
    rows0, rows1 = wire(rows_of(W, 0)), wire(rows_of(W, 1))
    hr = rows0.shape[0] // 2
    h0, (g_win0, g_cw) = _modulate(x[0], norm_w[0].reshape(1, D), mods[0][0:1], mods[0][1:2], "modulate_l0",
                                   [(wire(w_in[0]), "two_level"), (conv_w.reshape(L * KC, CW), True)])
    g_cw = g_cw.reshape(NDEV, L, KC, CW)
    h, sv0, got, lp0 = _layer_fwd(x[0], mods[0], prep(0, g_win0, None), "l0",
                                  dict(proj_gm=[(rows0[:hr], True)], proj_gate=[(rows0[hr:], True)],
                                       proj_ssd=[(wire(w_in[1][:half]), True)],
                                       ssd_fwd=[(wire(w_in[1][half:]), True), (rows1, True)]), h0)
    lp1 = prep(1, jnp.concatenate([got["proj_ssd"][0], got["ssd_fwd"][0]], axis=1), got["ssd_fwd"][1])
    h, sv1, _, _ = _layer_fwd(h, mods[1], lp1, "l1")
    dh, s_fin = _final_loss(h, loss_target[0], final_norm_w.reshape(1, D), "final_loss")
    loss = lax.psum(0.5 / D * s_fin[1, 0], ("x", "y", "c"))

    small_l = ["gm_ln_w", "gm_ln_b", "gm_ws", "gm_bs", "conv_b", "dt_bias", "a_log", "d_skip", "ssm_norm_w"]
    small_key = dict(gm_ln_w="lw", gm_ln_b="lb", gm_ws="ws", gm_bs="bs", conv_b="cb", dt_bias="dtb", a_log="alog", d_skip="dsk",
                     ssm_norm_w="snw")
    shapes_l = [W[n].shape[1:] for n in small_l] + [(KC, CD)]
    shapes_t = [(L, 3 * D), (L, D), (D,)]
    rows_for = lambda sh: -(-sum(-(-functools.reduce(lambda a, b: a * b, s, 1) // 128) for s in sh) // 256) * 256
    n_rows_l, n_rows_t = rows_for(shapes_l), rows_for(shapes_t)
    slab_of = lambda g: _pack([g[small_key[n]] for n in small_l] + [g["cw"]], n_rows_l)

    dh, g1, _, _ = _layer_bwd(dh, sv1, mods[1], lp1, "l1")
    dh, g0, got, got_own = _layer_bwd(dh, sv0, mods[0], lp0, "l0",
                                      [(_win_blocks(g1), False), (_row_blocks(g1), False), (slab_of(g1), True)], slab_of)
    grads = [g0, g1]
    grad_x = dh[None]
    recv_win = [got_own[1], got[0]]
    recv_rows = [got_own[0], got[1]]
    slab_all = [got_own[2], got[2]]
    st = lambda key: jnp.stack([g[key] for g in grads])
    slab_tail = _pack([st("mod"), st("nw"), s_fin[0]], n_rows_t)

    o_win = _sum_adamw(recv_win, w_in, m_w_in, v_w_in, "adamw_w_in")
    rows_all = lambda d: jnp.concatenate([rows_of(d, l) for l in range(L)], axis=0)
    o_rows, (slab_t,) = _sum_adamw(recv_rows, rows_all(W), rows_all(M), rows_all(V), "adamw_w_rows", [(slab_tail, True)])
    o_rows = [o.reshape(L, 2 * RA + RB, D) for o in o_rows]
    o_pa = [o[:, 0:RA] for o in o_rows]
    o_pb = [o[:, RA:RA + RB] for o in o_rows]
    o_po = [o[:, RA + RB:] for o in o_rows]

    def slab_params(d):
        return jnp.concatenate([_pack([d[n][l] for n in small_l] + [jnp.zeros((KC, CD), f32)], n_rows_l) for l in range(L)], axis=0)

    o_small = [[_unpack(o[l * n_rows_l:(l + 1) * n_rows_l], shapes_l) for l in range(L)]
               for o in _sum_adamw(slab_all, slab_params(W), slab_params(M), slab_params(V), "adamw_small")]
    small_out = {n: [jnp.stack([o_small[k][l][i] for l in range(L)]) for k in range(4)] for i, n in enumerate(small_l)}
    tail_params = lambda d: _pack([d["ada_b"], d["norm_w"], d["final_norm_w"]], n_rows_t)
    o_tail = [_unpack(o, shapes_t) for o in _sum_adamw([slab_t], tail_params(W), tail_params(M), tail_params(V), "adamw_small_tail")]
    for i, n in enumerate(["ada_b", "norm_w", "final_norm_w"]):
        small_out[n] = [o_tail[k][i] for k in range(4)]

    g_cw_mine = jnp.stack([lax.dynamic_slice_in_dim(o_small[0][l][len(small_l)], me * CW, CW, axis=1) for l in range(L)])
    o_cw = _sum_adamw([g_cw_mine.reshape(1, L * KC, CW)], conv_w.reshape(L * KC, CW), m_conv_w.reshape(L * KC, CW),
                      v_conv_w.reshape(L * KC, CW), "adamw_conv_w")
    o_cw = [o.reshape(L, KC, CW) for o in o_cw]

    dmod_all = jnp.stack([_unpack(slab_t[k], shapes_t)[0] for k in range(NDEV)])
    dmod_cols = lax.dynamic_slice_in_dim(dmod_all, me * AW, AW, axis=2).reshape(NDEV, L * AW)
    g_ada_w = jnp.transpose(_ada_w_grad(sc_all, dmod_cols, "ada_w_grad").reshape(D, L, AW), (1, 0, 2))
    o_ada = _sum_adamw([g_ada_w[l][None] for l in range(L)], ada_w, m_ada_w, v_ada_w, "adamw_ada_w")

    big = dict(ada_w=o_ada, w_in=o_win, conv_w=o_cw, w_proj_a=o_pa, w_proj_b=o_pb, w_out=o_po)
    order = ["ada_w", "ada_b", "norm_w", "w_in", "gm_ln_w", "gm_ln_b", "gm_ws", "gm_bs", "conv_w", "conv_b", "dt_bias", "a_log",
             "d_skip", "ssm_norm_w", "w_proj_a", "w_proj_b", "w_out", "final_norm_w"]
    pick = lambda n, k: big[n][k] if n in big else small_out[n][k]
    return (loss, grad_x, *[pick(n, 0) for n in order], *[pick(n, 1) for n in order], *[pick(n, 2) for n in order],
            *[pick(n, 3) for n in order])
```

```python
import functools

import jax
import jax.numpy as jnp
from jax import lax
from jax.experimental import pallas as pl
from jax.experimental.pallas import tpu as pltpu

f32 = jnp.float32
_MXU = jnp.bfloat16
_WIRE = jnp.bfloat16

D = 1024
Q = 128
NG = 8
DI = 2048
NH = 32
P = 64
HPG = 4
NS = 128
KC = 4
CD = 4096
GRP = DI // NG
EPS = 1e-6
NDEV = 8
WA = 3 * D
WB = DI + CD + 256
WG = 2 * D
OFF_XBC = DI
OFF_DT = DI + CD
NIN = 11296
VMEM_LIMIT = 56 * 1024 * 1024
NEG = -1e30

ADAM_LR, ADAM_B1, ADAM_B2, ADAM_EPS, ADAM_WD, ADAM_STEP = 0.001, 0.9, 0.999, 1e-08, 0.01, 10


def _bf(x):
    return x.astype(_MXU)


def _dot(a, b):
    return jnp.dot(a, b, preferred_element_type=f32)


def _dot_nt(a, b):
    return lax.dot_general(a, b, (((1,), (1,)), ((), ())), preferred_element_type=f32)


def _dot_tn(a, b):
    return lax.dot_general(a, b, (((0,), (0,)), ((), ())), preferred_element_type=f32)


def _dot_exact(a, b):
    return jnp.dot(a, b, preferred_element_type=f32, precision=lax.Precision.HIGHEST)


def _sigmoid(x):
    return jax.nn.sigmoid(x)


def _silu(x):
    return x * _sigmoid(x)


def _dsilu(x):
    s = _sigmoid(x)
    return s * (1.0 + x * (1.0 - s))


_GK = 0.7978845608028654
_GC = 0.044715


def _gelu_and_grad(x):
    x2, hx = x * x, 0.5 * x
    t = jnp.tanh(x * (_GK + (_GK * _GC) * x2))
    return hx * (1.0 + t), 0.5 * (1.0 + t) + hx * (1.0 - t * t) * (_GK + (3.0 * _GK * _GC) * x2)


def _gelu(x):
    return 0.5 * x * (1.0 + jnp.tanh(x * (_GK + (_GK * _GC) * (x * x))))


def _softplus(x):
    return jnp.maximum(x, 0.0) + jnp.log1p(jnp.exp(-jnp.abs(x)))


def _tile(n, cap):
    if n <= cap:
        return n
    best = None
    for t in range(128, cap + 1, 128):
        if n % t == 0:
            best = t
    assert best is not None, (n, cap)
    return best


def _params(sem):
    return pltpu.CompilerParams(dimension_semantics=sem, vmem_limit_bytes=VMEM_LIMIT)


def _exch_ops(src_ref, out_ref, send_sems, recv_sems, local_sem, gather):
    x, y, c = lax.axis_index("x"), lax.axis_index("y"), lax.axis_index("c")
    me = 4 * x + 2 * y + c

    def peer(j):
        px = 1 - x if (j >> 2) & 1 else x
        py = 1 - y if (j >> 1) & 1 else y
        pc = 1 - c if j & 1 else c
        return (px, py, pc), 4 * px + 2 * py + pc

    def copy(j, landing):
        dev, idx = peer(j)
        return pltpu.make_async_remote_copy(
            src_ref=src_ref if gather else src_ref.at[idx], dst_ref=out_ref.at[idx] if landing else out_ref.at[me],
            send_sem=send_sems.at[j - 1], recv_sem=recv_sems.at[j - 1], device_id=dev, device_id_type=pl.DeviceIdType.MESH)

    mine = pltpu.make_async_copy(src_ref if gather else src_ref.at[me], out_ref.at[me], local_sem)

    def start():
        mine.start()
        for j in range(1, NDEV):
            copy(j, False).start()

    def finish():
        for j in range(1, NDEV):
            copy(j, True).wait_recv()
        for j in range(1, NDEV):
            copy(j, False).wait_send()
        mine.wait()

    return start, finish


def _gather2_ops(src_ref, out_ref, send_sems, recv_sems, local_sem):
    x, y, c = lax.axis_index("x"), lax.axis_index("y"), lax.axis_index("c")
    me, sibling = (x, y, c), (x, y, 1 - c)
    chips = [(1 - x, y), (x, 1 - y), (1 - x, 1 - y)]

    def slot(px, py, pc):
        return out_ref.at[4 * px + 2 * py + pc]

    def copy(k, block, to, src=None):
        return pltpu.make_async_remote_copy(
            src_ref=slot(*block) if src is None else src, dst_ref=slot(*block), send_sem=send_sems.at[k],
            recv_sem=recv_sems.at[k], device_id=to, device_id_type=pl.DeviceIdType.MESH)

    mine = pltpu.make_async_copy(src_ref, slot(*me), local_sem)
    first = [copy(0, me, sibling, src=src_ref)] + [copy(1 + j, me, (*chip, c), src=src_ref) for j, chip in enumerate(chips)]
    passed = [copy(4 + j, (*chip, c), sibling) for j, chip in enumerate(chips)]

    def start():
        mine.start()
        for cp in first:
            cp.start()

    def finish():
        for j, chip in enumerate(chips):
            copy(1 + j, (*chip, c), me).wait_recv()
            passed[j].start()
        copy(0, sibling, me).wait_recv()
        for j, chip in enumerate(chips):
            copy(4 + j, (*chip, 1 - c), me).wait_recv()
        for cp in first + passed:
            cp.wait_send()
        mine.wait()

    return start, finish


def _exch_shape(src, gather):
    return jax.ShapeDtypeStruct((NDEV,) + tuple(src.shape if gather else src.shape[1:]), src.dtype)


_EXCH_SEMS = [pltpu.SemaphoreType.DMA((NDEV - 1,)), pltpu.SemaphoreType.DMA((NDEV - 1,)), pltpu.SemaphoreType.DMA]


def _carry_call(body, name, grid, in_specs, out_specs, out_shape, scratch_shapes, args, carry=()):
    n_in, n_out, n_sc, nx = len(in_specs), len(out_specs), len(scratch_shapes), len(carry)
    sem = ("arbitrary",) * len(grid)
    if nx == 0:
        outs = pl.pallas_call(body, name=name, grid=grid, in_specs=in_specs, out_specs=out_specs, out_shape=out_shape,
                              scratch_shapes=scratch_shapes, compiler_params=_params(sem))(*args)
        return list(outs), []

    def wrapped(*refs):
        ins, srcs = refs[:n_in], refs[n_in:n_in + nx]
        outs, dsts = refs[n_in + nx:n_in + nx + n_out], refs[n_in + nx + n_out:n_in + 2 * nx + n_out]
        scratch, sems = refs[n_in + 2 * nx + n_out:n_in + 2 * nx + n_out + n_sc], refs[n_in + 2 * nx + n_out + n_sc:]
        ops = [_gather2_ops(srcs[i], dsts[i], sems[3 * i], sems[3 * i + 1], sems[3 * i + 2]) if carry[i][1] == "two_level" else
               _exch_ops(srcs[i], dsts[i], sems[3 * i], sems[3 * i + 1], sems[3 * i + 2], carry[i][1]) for i in range(nx)]
        first = functools.reduce(jnp.logical_and, [pl.program_id(d) == 0 for d in range(len(grid))])
        last = functools.reduce(jnp.logical_and, [pl.program_id(d) == grid[d] - 1 for d in range(len(grid))])

        @pl.when(first)
        def _():
            for start, _ in ops:
                start()

        body(*ins, *outs, *scratch)

        @pl.when(last)
        def _():
            for _, finish in ops:
                finish()

    hbm = pl.BlockSpec(memory_space=pl.ANY)
    outs = pl.pallas_call(
        wrapped, name=name, grid=grid, in_specs=list(in_specs) + [hbm] * nx, out_specs=list(out_specs) + [hbm] * nx,
        out_shape=list(out_shape) + [_exch_shape(s, g) for s, g in carry], scratch_shapes=list(scratch_shapes) + _EXCH_SEMS * nx,
        compiler_params=_params(sem))(*args, *[s for s, _ in carry])
    return list(outs[:n_out]), list(outs[n_out:])


def _mm(a, b, mode, name, out_dtype=f32, tm_cap=1024, tn_cap=1280, tk_cap=1280, carry=()):
    if mode == "nn":
        (M, K), (K2, N) = a.shape, b.shape
    elif mode == "nt":
        (M, K), (N, K2) = a.shape, b.shape
    else:
        (K, M), (K2, N) = a.shape, b.shape
        tk_cap = min(tk_cap, 1024)
    assert K == K2, (a.shape, b.shape, mode)
    tm, tn, tk = _tile(M, tm_cap), _tile(N, tn_cap), _tile(K, tk_cap)
    nk = K // tk
    if mode == "nn":
        a_spec = pl.BlockSpec((tm, tk), lambda i, j, k: (i, k))
        b_spec = pl.BlockSpec((tk, tn), lambda i, j, k: (k, j))
        dot = _dot
    elif mode == "nt":
        a_spec = pl.BlockSpec((tm, tk), lambda i, j, k: (i, k))
        b_spec = pl.BlockSpec((tn, tk), lambda i, j, k: (j, k))
        dot = _dot_nt
    else:
        a_spec = pl.BlockSpec((tk, tm), lambda i, j, k: (k, i))
        b_spec = pl.BlockSpec((tk, tn), lambda i, j, k: (k, j))
        dot = _dot_tn
    o_spec = pl.BlockSpec((tm, tn), lambda i, j, k: (i, j))

    def body_one(a_ref, b_ref, o_ref):
        o_ref[...] = dot(_bf(a_ref[...]), _bf(b_ref[...])).astype(out_dtype)

    def body_acc(a_ref, b_ref, o_ref, acc_ref):
        k = pl.program_id(2)

        @pl.when(k == 0)
        def _():
            acc_ref[...] = jnp.zeros_like(acc_ref)

        acc_ref[...] += dot(_bf(a_ref[...]), _bf(b_ref[...]))

        @pl.when(k == nk - 1)
        def _():
            o_ref[...] = acc_ref[...].astype(out_dtype)

    (out,), exchanged = _carry_call(body_one if nk == 1 else body_acc, name, (M // tm, N // tn, nk), [a_spec, b_spec], [o_spec],
                                    [jax.ShapeDtypeStruct((M, N), out_dtype)], [] if nk == 1 else [pltpu.VMEM((tm, tn), f32)],
                                    (a, b), carry)
    return (out, exchanged) if carry else out


def _row_spec(ts, w, col=0):
    return pl.BlockSpec((ts, w), lambda i: (i, col))


def _full_spec(shape):
    nd = len(shape)
    return pl.BlockSpec(shape, lambda i: (0,) * nd)


def _modulate(x, nw, shift, scale, name, carry=()):
    T = x.shape[0]
    ts = _tile(T, 1024)

    def body(x_ref, nw_ref, sh_ref, sc_ref, h_ref):
        xv = x_ref[...]
        r = lax.rsqrt(jnp.mean(xv * xv, axis=-1, keepdims=True) + EPS)
        h_ref[...] = ((xv * r * nw_ref[...]) * (1.0 + sc_ref[...]) + sh_ref[...]).astype(h_ref.dtype)

    (h,), exchanged = _carry_call(body, name, (T // ts,), [_row_spec(ts, D), _full_spec((1, D)), _full_spec((1, D)), _full_spec((1, D))],
                                  [_row_spec(ts, D)], [jax.ShapeDtypeStruct((T, D), _MXU)], [], (x, nw, shift, scale), carry)
    return (h, exchanged) if carry else h


def _final_loss(x, tgt, fw, name):
    T = x.shape[0]
    ts = _tile(T, 1024)

    def body(x_ref, t_ref, fw_ref, dx_ref, acc_ref):
        @pl.when(pl.program_id(0) == 0)
        def _():
            acc_ref[...] = jnp.zeros_like(acc_ref)

        xv = x_ref[...]
        r = lax.rsqrt(jnp.mean(xv * xv, axis=-1, keepdims=True) + EPS)
        xn = xv * r
        e = xn * fw_ref[...] - t_ref[...]
        dy = e * (1.0 / D)
        acc_ref[0:1, :] += jnp.sum(dy * xn, axis=0, keepdims=True)
        acc_ref[1:2, :] += jnp.sum(jnp.sum(e * e, axis=0, keepdims=True), axis=1, keepdims=True)
        dxn = dy * fw_ref[...]
        dx_ref[...] = r * (dxn - xn * jnp.mean(dxn * xn, axis=-1, keepdims=True))

    return pl.pallas_call(
        body, name=name, grid=(T // ts,),
        in_specs=[_row_spec(ts, D), _row_spec(ts, D), _full_spec((1, D))],
        out_specs=[_row_spec(ts, D), _full_spec((8, D))],
        out_shape=[jax.ShapeDtypeStruct((T, D), f32), jax.ShapeDtypeStruct((8, D), f32)],
        compiler_params=_params(("arbitrary",)))(x, tgt, fw)


def _tril(n):
    return lax.broadcasted_iota(jnp.int32, (n, n), 0) >= lax.broadcasted_iota(jnp.int32, (n, n), 1)


def _gm_chunk_fwd(u, v, z, lw, lb, ws_ref, bsx, gu=None, gv=None):
    gu, gv = _gelu(u) if gu is None else gu, _gelu(v) if gv is None else gv
    mu = jnp.mean(gv, axis=-1, keepdims=True)
    cen = gv - mu
    rstd = lax.rsqrt(jnp.mean(cen * cen, axis=-1, keepdims=True) + EPS)
    vhat = cen * rstd
    vn = _bf(vhat * lw + lb)
    tri = _tril(Q)
    mixed = jnp.concatenate(
        [_dot(_bf(jnp.where(tri, ws_ref[g], 0.0)), vn[:, g * Q:(g + 1) * Q]) for g in range(NG)], axis=1) + bsx
    return gu, vhat, rstd, vn, mixed


def _gmlp_fwd(pA, lw, lb, ws, bsx, name):
    T = pA.shape[0]
    ts = _tile(T, 1024)

    def body(u_ref, v_ref, z_ref, lw_ref, lb_ref, ws_ref, bsx_ref, y_ref):
        def chunk(ci, carry):
            rows = pl.ds(pl.multiple_of(ci * Q, Q), Q)
            u, v, z = u_ref[rows, :].astype(f32), v_ref[rows, :].astype(f32), z_ref[rows, :].astype(f32)
            gu, _, _, _, mixed = _gm_chunk_fwd(u, v, z, lw_ref[...], lb_ref[...], ws_ref, bsx_ref[...])
            y_ref[rows, :] = (gu * mixed * _silu(z)).astype(y_ref.dtype)
            return carry

        lax.fori_loop(0, ts // Q, chunk, 0)

    return pl.pallas_call(
        body, name=name, grid=(T // ts,),
        in_specs=[_row_spec(ts, D, 0), _row_spec(ts, D, 1), _row_spec(ts, D, 2), _full_spec((1, D)), _full_spec((1, D)),
                  _full_spec((NG, Q, Q)), _full_spec((Q, D))],
        out_specs=_row_spec(ts, D), out_shape=jax.ShapeDtypeStruct((T, D), _MXU),
        compiler_params=_params(("arbitrary",)))(pA, pA, pA, lw, lb, ws, bsx)


def _gmlp_bwd(dya, pA, lw, lb, ws, bsx, name):
    T = pA.shape[0]
    ts = _tile(T, 1024)

    def body(dy_ref, u_ref, v_ref, z_ref, lw_ref, lb_ref, ws_ref, bsx_ref, dp_ref, acc_ref, dws_ref, dbs_ref):
        @pl.when(pl.program_id(0) == 0)
        def _():
            acc_ref[...] = jnp.zeros_like(acc_ref)
            dws_ref[...] = jnp.zeros_like(dws_ref)
            dbs_ref[...] = jnp.zeros_like(dbs_ref)

        tri = _tril(Q)

        def chunk(ci, carry):
            rows = pl.ds(pl.multiple_of(ci * Q, Q), Q)
            u, v, z, dy = u_ref[rows, :].astype(f32), v_ref[rows, :].astype(f32), z_ref[rows, :].astype(f32), dy_ref[rows, :].astype(f32)
            (gu, d_gu), (gv, d_gv) = _gelu_and_grad(u), _gelu_and_grad(v)
            gu, vhat, rstd, vn, mixed = _gm_chunk_fwd(u, v, z, lw_ref[...], lb_ref[...], ws_ref, bsx_ref[...], gu, gv)
            sz = _silu(z)
            dp_ref[rows, 0:D] = (dy * mixed * sz * d_gu).astype(dp_ref.dtype)
            dp_ref[rows, 2 * D:3 * D] = (dy * gu * mixed * _dsilu(z)).astype(dp_ref.dtype)
            dmixed = dy * gu * sz
            dbs_ref[...] += dmixed
            dmb = _bf(dmixed)
            dvn_parts = []
            for g in range(NG):
                cols = slice(g * Q, (g + 1) * Q)
                wg = _bf(jnp.where(tri, ws_ref[g], 0.0))
                dvn_parts.append(_dot_tn(wg, dmb[:, cols]))
                dws_ref[g] += jnp.where(tri, _dot_nt(dmb[:, cols], vn[:, cols]), 0.0)
            dvn = jnp.concatenate(dvn_parts, axis=1)
            acc_ref[0:1, :] += jnp.sum(dvn * vhat, axis=0, keepdims=True)
            acc_ref[1:2, :] += jnp.sum(dvn, axis=0, keepdims=True)
            dvh = dvn * lw_ref[...]
            dgv = rstd * (dvh - jnp.mean(dvh, axis=-1, keepdims=True) - vhat * jnp.mean(dvh * vhat, axis=-1, keepdims=True))
            dp_ref[rows, D:2 * D] = (dgv * d_gv).astype(dp_ref.dtype)
            return carry

        lax.fori_loop(0, ts // Q, chunk, 0)

    return pl.pallas_call(
        body, name=name, grid=(T // ts,),
        in_specs=[_row_spec(ts, D), _row_spec(ts, D, 0), _row_spec(ts, D, 1), _row_spec(ts, D, 2), _full_spec((1, D)),
                  _full_spec((1, D)), _full_spec((NG, Q, Q)), _full_spec((Q, D))],
        out_specs=[_row_spec(ts, WA), _full_spec((8, D)), _full_spec((NG, Q, Q)), _full_spec((Q, D))],
        out_shape=[jax.ShapeDtypeStruct((T, WA), _MXU), jax.ShapeDtypeStruct((8, D), f32),
                   jax.ShapeDtypeStruct((NG, Q, Q), f32), jax.ShapeDtypeStruct((Q, D), f32)],
        compiler_params=_params(("arbitrary",)))(dya, pA, pA, pA, lw, lb, ws, bsx)


def _head_maps():
    h = lax.broadcasted_iota(jnp.int32, (128, DI), 0)
    ch = lax.broadcasted_iota(jnp.int32, (128, DI), 1)
    ex = (ch // P == h).astype(_MXU)
    return ex, ex.T


def _split(v, parts):
    out = []
    for _ in range(parts - 1):
        p = _bf(v)
        out.append(p)
        v = v - p.astype(f32)
    out.append(_bf(v))
    return out


def _expand(v, ex_ref, parts):
    acc = None
    for p in _split(v, parts):
        t = _dot(p, ex_ref[...])
        acc = t if acc is None else acc + t
    return acc


def _reduce(v, rd_ref, parts=2):
    acc = None
    for p in _split(v, parts):
        t = _dot(p, rd_ref[...])
        acc = t if acc is None else acc + t
    return acc


def _ssd_time(pb_ref, dtb_ref, alog_ref):
    xdt = pb_ref[:, OFF_DT:OFF_DT + 128] + dtb_ref[...]
    dt = _softplus(xdt)
    a = -jnp.exp(alog_ref[...])
    cs = _dot_exact(_tril(Q).astype(f32), dt * a)
    return xdt, dt, a, cs


def _head_mask(r):
    return lax.broadcasted_iota(jnp.int32, (Q, GRP), 1) // P == r


def _ssd_group_fwd(g, xa_ref, s_prev, cs, cs_t, dtx_ref, csx_ref, dsk_ref):
    cols = slice(g * GRP, (g + 1) * GRP)
    xs = xa_ref[:, cols]
    bb = _bf(xa_ref[:, DI + g * NS:DI + (g + 1) * NS])
    cb = _bf(xa_ref[:, DI + NG * NS + g * NS:DI + NG * NS + (g + 1) * NS])
    gm = _dot_nt(cb, bb)
    xd = xs * dtx_ref[:, cols]
    csx = csx_ref[:, cols]
    csl = csx_ref[Q - 1:Q, cols]
    tri = _tril(Q)
    lms = [jnp.exp(jnp.where(tri, cs[:, HPG * g + r:HPG * g + r + 1] - cs_t[HPG * g + r:HPG * g + r + 1, :], NEG))
           for r in range(HPG)]
    mfs = [gm * lm for lm in lms]
    mcat = jnp.concatenate([_bf(m) for m in mfs], axis=1)
    xbd = jnp.concatenate([_bf(jnp.where(_head_mask(r), xd, 0.0)) for r in range(HPG)], axis=0)
    ydiag = _dot(mcat, xbd)
    ecs, dte, ecl = jnp.exp(csx), jnp.exp(csl - csx), jnp.exp(csl)
    yoff = ecs * _dot(cb, _bf(s_prev))
    y = ydiag + yoff + xs * dsk_ref[:, cols]
    xdd = xd * dte
    s_new = s_prev * ecl + _dot_tn(bb, _bf(xdd))
    return y, s_new, (xs, bb, cb, xd, lms, mfs, mcat, xbd, yoff, xdd, ecs, dte, ecl)


def _ssd_fwd(pB, cw, cb, dtb, alog, dsk, snw, name, carry=()):
    T = pB.shape[0]
    nc = T // Q
    ex, _ = _head_maps()

    def body(pb_ref, cw_ref, cb_ref, dtb_ref, alog_ref, dsk_ref, snw_ref, ex_ref, y_ref, xc_ref, st_ref,
             s_ref, ext_ref, xa_ref, dtx_ref, csx_ref):
        @pl.when(pl.program_id(0) == 0)
        def _():
            s_ref[...] = jnp.zeros_like(s_ref)
            ext_ref[...] = jnp.zeros_like(ext_ref)

        ext_ref[8:8 + Q, :] = pb_ref[:, OFF_XBC:OFF_XBC + CD]
        for j in range(CD // 512):
            cj = slice(j * 512, (j + 1) * 512)
            e = ext_ref[:, cj]
            xc = cb_ref[:, cj] + cw_ref[KC - 1:KC, cj] * e[8:8 + Q]
            for s in range(1, KC):
                xc = xc + cw_ref[KC - 1 - s:KC - s, cj] * pltpu.roll(e, s, 0)[8:8 + Q]
            xc_ref[:, cj] = xc
            xa_ref[:, cj] = _silu(xc)
        ext_ref[0:8, :] = ext_ref[Q:Q + 8, :]

        _, dt, _, cs = _ssd_time(pb_ref, dtb_ref, alog_ref)
        cs_t = cs.T
        dtx_ref[...] = _expand(dt, ex_ref, 2)
        csx_ref[...] = _expand(cs, ex_ref, 3)
        for g in range(NG):
            s_prev = s_ref[g]
            st_ref[0, g] = s_prev
            y, s_new, _ = _ssd_group_fwd(g, xa_ref, s_prev, cs, cs_t, dtx_ref, csx_ref, dsk_ref)
            s_ref[g] = s_new
            cols = slice(g * GRP, (g + 1) * GRP)
            yz = y * _silu(pb_ref[:, cols])
            rr = lax.rsqrt(jnp.mean(yz * yz, axis=-1, keepdims=True) + EPS)
            y_ref[:, cols] = (yz * rr * snw_ref[:, cols]).astype(y_ref.dtype)

    outs, exchanged = _carry_call(
        body, name, (nc,),
        [_row_spec(Q, WB), _full_spec((8, CD)), _full_spec((1, CD)), _full_spec((1, 128)), _full_spec((1, 128)),
         _full_spec((1, DI)), _full_spec((1, DI)), _full_spec((128, DI))],
        [_row_spec(Q, DI), _row_spec(Q, CD), pl.BlockSpec((1, NG, NS, GRP), lambda i: (i, 0, 0, 0))],
        [jax.ShapeDtypeStruct((T, DI), _MXU), jax.ShapeDtypeStruct((T, CD), f32), jax.ShapeDtypeStruct((nc, NG, NS, GRP), f32)],
        [pltpu.VMEM((NG, NS, GRP), f32), pltpu.VMEM((Q + 8, CD), f32), pltpu.VMEM((Q, CD), f32),
         pltpu.VMEM((Q, DI), f32), pltpu.VMEM((Q, DI), f32)],
        (pB, cw, cb, dtb, alog, dsk, snw, ex), carry)
    return outs + [exchanged]


def _ssd_bwd(dyb, pB, xc, states, cw, dtb, alog, dsk, snw, name, carry=()):
    T = pB.shape[0]
    nc = T // Q
    ex, rd = _head_maps()
    selr = (lax.broadcasted_iota(jnp.int32, (HPG * Q, 128), 0) // Q == lax.broadcasted_iota(jnp.int32, (HPG * Q, 128), 1)).astype(_MXU)

    def body(dy_ref, pb_ref, xc_ref, st_ref, cw_ref, dtb_ref, alog_ref, dsk_ref, snw_ref, ex_ref, rd_ref, selr_ref,
             dp_ref, dcw_ref, dcb_ref, dhd_ref, dcol_ref,
             ds_ref, xa_ref, sg_ref, dxa_ref, dxe_ref, dtx_ref, csx_ref, rcs_ref, rdt_ref, rl_ref, dcs_ref, dcst_ref):
        @pl.when(pl.program_id(0) == 0)
        def _():
            ds_ref[...] = jnp.zeros_like(ds_ref)
            dxe_ref[...] = jnp.zeros_like(dxe_ref)
            dcw_ref[...] = jnp.zeros_like(dcw_ref)
            dcb_ref[...] = jnp.zeros_like(dcb_ref)
            dhd_ref[...] = jnp.zeros_like(dhd_ref)
            dcol_ref[...] = jnp.zeros_like(dcol_ref)
            rl_ref[...] = jnp.zeros_like(rl_ref)
            dcst_ref[...] = jnp.zeros_like(dcst_ref)

        dcs_ref[...] = jnp.zeros_like(dcs_ref)
        for j in range(CD // 512):
            cj = slice(j * 512, (j + 1) * 512)
            sg_ref[:, cj] = _sigmoid(xc_ref[:, cj])
            xa_ref[:, cj] = xc_ref[:, cj] * sg_ref[:, cj]
        xdt, dt, a, cs = _ssd_time(pb_ref, dtb_ref, alog_ref)
        cs_t = cs.T
        dtx_ref[...] = _expand(dt, ex_ref, 2)
        csx_ref[...] = _expand(cs, ex_ref, 3)

        for g in range(NG):
            cols = slice(g * GRP, (g + 1) * GRP)
            s_prev = st_ref[0, g]
            sb = _bf(s_prev)
            y, _, (xs, bb, cbm, xd, lms, mfs, mcat, xbd, yoff, xdd, ecs, dte, ecl) = _ssd_group_fwd(
                g, xa_ref, s_prev, cs, cs_t, dtx_ref, csx_ref, dsk_ref)
            z = pb_ref[:, cols]
            sz = _silu(z)
            yz = y * sz
            rr = lax.rsqrt(jnp.mean(yz * yz, axis=-1, keepdims=True) + EPS)
            nrm = yz * rr
            dyb_g = dy_ref[:, cols].astype(f32)
            dcol_ref[1:2, cols] += jnp.sum(dyb_g * nrm, axis=0, keepdims=True)
            dn = dyb_g * snw_ref[:, cols]
            dyz = rr * (dn - nrm * jnp.mean(dn * nrm, axis=-1, keepdims=True))
            dyv = dyz * sz
            dp_ref[:, cols] = (dyz * y * _dsilu(z)).astype(dp_ref.dtype)
            dcol_ref[0:1, cols] += jnp.sum(dyv * xs, axis=0, keepdims=True)
            dy16 = _bf(dyv)
            dmcat = _dot_nt(dy16, xbd)
            dg = dmcat[:, 0:Q] * lms[0]
            for r in range(1, HPG):
                dg = dg + dmcat[:, r * Q:(r + 1) * Q] * lms[r]
            e16 = _bf(jnp.concatenate([dmcat[:, r * Q:(r + 1) * Q] * mfs[r] for r in range(HPG)], axis=1))
            rows = _dot(e16, selr_ref[...])
            dcs_ref[...] += rows if g == 0 else pltpu.roll(rows, HPG * g, 1)
            csum = _dot(jnp.ones((8, Q), _MXU), e16)
            for r in range(HPG):
                dcst_ref[HPG * g + r:HPG * g + r + 1, :] = csum[0:1, r * Q:(r + 1) * Q]
            big = _dot_tn(mcat, dy16)
            dxd_diag = jnp.where(_head_mask(0), big[0:Q], 0.0)
            for r in range(1, HPG):
                dxd_diag = dxd_diag + jnp.where(_head_mask(r), big[r * Q:(r + 1) * Q], 0.0)
            dsn = ds_ref[g]
            dsn16 = _bf(dsn)
            dxdd = _dot(bb, dsn16)
            t_state = dxdd * xdd
            dxd = dxd_diag + dxdd * dte
            rcs_ref[:, cols] = dyv * yoff - t_state
            rl_ref[0:1, cols] = (jnp.sum(t_state, axis=0, keepdims=True)
                                 + ecl * jnp.sum(s_prev * dsn, axis=0, keepdims=True))
            rdt_ref[:, cols] = dxd * xs
            dxa_ref[:, cols] = dyv * dsk_ref[:, cols] + dxd * dtx_ref[:, cols]
            dg16 = _bf(dg)
            dw16 = _bf(dyv * ecs)
            dxa_ref[:, DI + NG * NS + g * NS:DI + NG * NS + (g + 1) * NS] = _dot(dg16, bb) + _dot_nt(dw16, sb)
            dxa_ref[:, DI + g * NS:DI + (g + 1) * NS] = _dot_tn(dg16, cbm) + _dot_nt(_bf(xdd), dsn16)
            ds_ref[g] = dsn * ecl + _dot_tn(cbm, dw16)

        row = lax.broadcasted_iota(jnp.int32, (Q, 128), 0)
        dcs = (dcs_ref[...] - dcst_ref[...].T + _reduce(rcs_ref[...], rd_ref)
               + jnp.where(row == Q - 1, _reduce(rl_ref[...], rd_ref)[0:1, :], 0.0))
        upper = lax.broadcasted_iota(jnp.int32, (Q, Q), 0) <= lax.broadcasted_iota(jnp.int32, (Q, Q), 1)
        dadt = _dot_exact(upper.astype(f32), dcs)
        valid = lax.broadcasted_iota(jnp.int32, (Q, 128), 1) < NH
        ddt = jnp.where(valid, _reduce(rdt_ref[...], rd_ref, 1) + dadt * a, 0.0)
        ddtr = ddt * _sigmoid(xdt)
        dhd_ref[0:1, :] += jnp.sum(ddtr, axis=0, keepdims=True)
        dhd_ref[1:2, :] += jnp.sum(jnp.where(valid, dadt * dt * a, 0.0), axis=0, keepdims=True)
        dp_ref[:, OFF_DT:OFF_DT + 128] = ddtr.astype(dp_ref.dtype)
        dp_ref[:, OFF_DT + 128:WB] = jnp.zeros((Q, WB - OFF_DT - 128), dp_ref.dtype)
        for j in range(CD // 512):
            cj = slice(j * 512, (j + 1) * 512)
            sg = sg_ref[:, cj]
            dxc = dxa_ref[:, cj] * (sg + xa_ref[:, cj] * (1.0 - sg))
            dxe_ref[0:Q, cj] = dxc
            dcb_ref[0:1, cj] += jnp.sum(dxc, axis=0, keepdims=True)
            raw = pb_ref[:, OFF_XBC + j * 512:OFF_XBC + (j + 1) * 512]
            e = dxe_ref[:, cj]
            dcw_ref[KC - 1:KC, cj] += jnp.sum(dxc * raw, axis=0, keepdims=True)
            dxb = cw_ref[KC - 1:KC, cj] * dxc
            for s in range(1, KC):
                sh = pltpu.roll(e, Q + 8 - s, 0)[0:Q]
                dcw_ref[KC - 1 - s:KC - s, cj] += jnp.sum(sh * raw, axis=0, keepdims=True)
                dxb = dxb + cw_ref[KC - 1 - s:KC - s, cj] * sh
            dp_ref[:, OFF_XBC + j * 512:OFF_XBC + (j + 1) * 512] = dxb.astype(dp_ref.dtype)
        dxe_ref[Q:Q + 8, :] = dxe_ref[0:8, :]

    rev = lambda i: (nc - 1 - i, 0)
    outs, exchanged = _carry_call(
        body, name, (nc,),
        [pl.BlockSpec((Q, DI), rev), pl.BlockSpec((Q, WB), rev), pl.BlockSpec((Q, CD), rev),
         pl.BlockSpec((1, NG, NS, GRP), lambda i: (nc - 1 - i, 0, 0, 0)),
         _full_spec((8, CD)), _full_spec((1, 128)), _full_spec((1, 128)),
         _full_spec((1, DI)), _full_spec((1, DI)), _full_spec((128, DI)), _full_spec((DI, 128)), _full_spec((HPG * Q, 128))],
        [pl.BlockSpec((Q, WB), rev), _full_spec((8, CD)), _full_spec((8, CD)), _full_spec((8, 128)), _full_spec((8, DI))],
        [jax.ShapeDtypeStruct((T, WB), _MXU), jax.ShapeDtypeStruct((8, CD), f32), jax.ShapeDtypeStruct((8, CD), f32),
         jax.ShapeDtypeStruct((8, 128), f32), jax.ShapeDtypeStruct((8, DI), f32)],
        [pltpu.VMEM((NG, NS, GRP), f32), pltpu.VMEM((Q, CD), f32), pltpu.VMEM((Q, CD), f32), pltpu.VMEM((Q, CD), f32),
         pltpu.VMEM((Q + 8, CD), f32), pltpu.VMEM((Q, DI), f32), pltpu.VMEM((Q, DI), f32),
         pltpu.VMEM((Q, DI), f32), pltpu.VMEM((Q, DI), f32), pltpu.VMEM((8, DI), f32),
         pltpu.VMEM((Q, 128), f32), pltpu.VMEM((128, Q), f32)],
        (dyb, pB, xc, states, cw, dtb, alog, dsk, snw, ex, rd, selr), carry)
    return outs + [exchanged]


def _merge_fwd(ya, yb, pG, x, gate, wa, wb, wo, name, carry=()):
    T = x.shape[0]
    ts = _tile(T, 512)

    def body(ya_ref, yb_ref, g_ref, x_ref, gate_ref, wa_ref, wb_ref, wo_ref, xo_ref, mg_ref, pa_ref, pb_ref):
        pa = _dot(ya_ref[...], wa_ref[...])
        pb = _dot(yb_ref[...], wb_ref[...])
        merged = _sigmoid(g_ref[:, 0:D].astype(f32)) * pa + _sigmoid(g_ref[:, D:2 * D].astype(f32)) * pb
        mg = _bf(merged)
        xo_ref[...] = x_ref[...] + gate_ref[...] * _dot(mg, wo_ref[...])
        mg_ref[...] = mg
        pa_ref[...] = pa.astype(pa_ref.dtype)
        pb_ref[...] = pb.astype(pb_ref.dtype)

    outs, exchanged = _carry_call(
        body, name, (T // ts,),
        [_row_spec(ts, D), _row_spec(ts, DI), _row_spec(ts, WG), _row_spec(ts, D), _full_spec((1, D)),
         _full_spec((D, D)), _full_spec((DI, D)), _full_spec((D, D))],
        [_row_spec(ts, D)] * 4,
        [jax.ShapeDtypeStruct((T, D), f32), jax.ShapeDtypeStruct((T, D), _MXU), jax.ShapeDtypeStruct((T, D), _MXU),
         jax.ShapeDtypeStruct((T, D), _MXU)],
        [], (ya, yb, pG, x, gate, wa, wb, wo), carry)
    return outs + [exchanged]


def _merge_bwd(dxout, merged, pa, pb, pG, gate, wo, name):
    T = dxout.shape[0]
    ts = _tile(T, 512)

    def body(dx_ref, mg_ref, pa_ref, pb_ref, g_ref, gate_ref, wo_ref, do_ref, dpa_ref, dpb_ref, dg_ref, acc_ref):
        @pl.when(pl.program_id(0) == 0)
        def _():
            acc_ref[...] = jnp.zeros_like(acc_ref)

        dxo = dx_ref[...]
        acc_ref[0:1, :] += jnp.sum(dxo * _dot(mg_ref[...], wo_ref[...]), axis=0, keepdims=True)
        do = _bf(dxo * gate_ref[...])
        do_ref[...] = do
        dmerged = _dot_nt(do, wo_ref[...])
        sa, sb = _sigmoid(g_ref[:, 0:D].astype(f32)), _sigmoid(g_ref[:, D:2 * D].astype(f32))
        dpa_ref[...] = (dmerged * sa).astype(dpa_ref.dtype)
        dpb_ref[...] = (dmerged * sb).astype(dpb_ref.dtype)
        dg_ref[:, 0:D] = (dmerged * pa_ref[...].astype(f32) * sa * (1.0 - sa)).astype(dg_ref.dtype)
        dg_ref[:, D:2 * D] = (dmerged * pb_ref[...].astype(f32) * sb * (1.0 - sb)).astype(dg_ref.dtype)

    return pl.pallas_call(
        body, name=name, grid=(T // ts,),
        in_specs=[_row_spec(ts, D), _row_spec(ts, D), _row_spec(ts, D), _row_spec(ts, D), _row_spec(ts, WG),
                  _full_spec((1, D)), _full_spec((D, D))],
        out_specs=[_row_spec(ts, D), _row_spec(ts, D), _row_spec(ts, D), _row_spec(ts, WG), _full_spec((8, D))],
        out_shape=[jax.ShapeDtypeStruct((T, D), _MXU), jax.ShapeDtypeStruct((T, D), _MXU), jax.ShapeDtypeStruct((T, D), _MXU),
                   jax.ShapeDtypeStruct((T, WG), _MXU), jax.ShapeDtypeStruct((8, D), f32)],
        compiler_params=_params(("arbitrary",)))(dxout, merged, pa, pb, pG, gate, wo)


def _rows_split(r):
    ra, rb = D // NDEV, DI // NDEV
    return r[:, 0:ra].reshape(D, D), r[:, ra:ra + rb].reshape(DI, D), r[:, ra + rb:].reshape(D, D)


def _layer_fwd(x, mod, lp, tag, carry=None, h=None):
    carry = carry or {}
    got = {}

    def mm(key, w, **kw):
        if carry.get(key):
            out, got[key] = _mm(h, w, "nn", f"{key}_{tag}", carry=carry[key], **kw)
            return out
        return _mm(h, w, "nn", f"{key}_{tag}", **kw)

    if h is None:
        h = _modulate(x, lp["nw"], mod[0:1], mod[1:2], f"modulate_{tag}")
    pA = mm("proj_gm", lp["w_gm"], out_dtype=_MXU)
    pB = mm("proj_ssd", lp["w_ssd"])
    pG = mm("proj_gate", lp["w_g"], out_dtype=_MXU)
    if "wa" not in lp:
        lp = dict(lp)
        lp["wa"], lp["wb"], lp["wo"] = _rows_split(jnp.concatenate([got["proj_gm"][0], got["proj_gate"][0]], axis=1))
    ya = _gmlp_fwd(pA, lp["lw"], lp["lb"], lp["ws"], lp["bsx"], f"gmlp_fwd_{tag}")
    yb, xc, states, got["ssd_fwd"] = _ssd_fwd(pB, lp["cw"], lp["cb"], lp["dtb"], lp["alog"], lp["dsk"], lp["snw"], f"ssd_fwd_{tag}",
                                              carry.get("ssd_fwd", ()))
    xo, merged, pa, pb, _ = _merge_fwd(ya, yb, pG, x, mod[2:3], lp["wa"], lp["wb"], lp["wo"], f"merge_fwd_{tag}")
    return xo, dict(x=x, h=h, pA=pA, pB=pB, pG=pG, ya=ya, yb=yb, xc=xc, states=states, merged=merged, pa=pa, pb=pb), got, lp


def _dh_modulate_bwd(dpG, dpA, dpB, w_g, w_gm, w_ssd, dxout, x, nw, scale, name, carry=()):
    T = x.shape[0]
    tm = _tile(T, 1024)
    fam = [(WG, _tile(WG, 1024)), (WA, _tile(WA, 1024)), (WB, _tile(WB, 1280))]
    steps = [w // k for w, k in fam]
    first = [0, steps[0], steps[0] + steps[1]]
    ns = sum(steps)

    def chunk(f):
        return lambda i, s: jnp.clip(s - first[f], 0, steps[f] - 1)

    a_specs = [pl.BlockSpec((tm, fam[f][1]), lambda i, s, c=chunk(f): (i, c(i, s))) for f in range(3)]
    b_specs = [pl.BlockSpec((D, fam[f][1]), lambda i, s, c=chunk(f): (0, c(i, s))) for f in range(3)]
    tok = pl.BlockSpec((tm, D), lambda i, s: (i, 0))
    tok_once = pl.BlockSpec((tm, D), lambda i, s: (i, 0), pipeline_mode=pl.Buffered(1))
    vec = pl.BlockSpec((1, D), lambda i, s: (0, 0))

    def body(ag, aa, ab, bg, ba, bb, dxo_ref, x_ref, nw_ref, sc_ref, dx_ref, sum_ref, acc_ref):
        i, s = pl.program_id(0), pl.program_id(1)

        @pl.when(jnp.logical_and(i == 0, s == 0))
        def _():
            sum_ref[...] = jnp.zeros_like(sum_ref)

        @pl.when(s == 0)
        def _():
            acc_ref[...] = jnp.zeros_like(acc_ref)

        @pl.when(s < first[1])
        def _():
            acc_ref[...] += _dot_nt(ag[...], bg[...])

        @pl.when(jnp.logical_and(s >= first[1], s < first[2]))
        def _():
            acc_ref[...] += _dot_nt(aa[...], ba[...])

        @pl.when(s >= first[2])
        def _():
            acc_ref[...] += _dot_nt(ab[...], bb[...])

        @pl.when(s == ns - 1)
        def _():
            xv, dh_v = x_ref[...], acc_ref[...]
            r = lax.rsqrt(jnp.mean(xv * xv, axis=-1, keepdims=True) + EPS)
            xn = xv * r
            hn = xn * nw_ref[...]
            dhn = dh_v * (1.0 + sc_ref[...])
            sum_ref[0:1, :] += jnp.sum(dh_v, axis=0, keepdims=True)
            sum_ref[1:2, :] += jnp.sum(dh_v * hn, axis=0, keepdims=True)
            sum_ref[2:3, :] += jnp.sum(dhn * xn, axis=0, keepdims=True)
            dxn = dhn * nw_ref[...]
            dx_ref[...] = dxo_ref[...] + r * (dxn - xn * jnp.mean(dxn * xn, axis=-1, keepdims=True))

    outs, exchanged = _carry_call(
        body, name, (T // tm, ns), a_specs + b_specs + [tok_once, tok_once, vec, vec], [tok, pl.BlockSpec((8, D), lambda i, s: (0, 0))],
        [jax.ShapeDtypeStruct((T, D), f32), jax.ShapeDtypeStruct((8, D), f32)], [pltpu.VMEM((tm, D), f32)],
        (dpG, dpA, dpB, w_g, w_gm, w_ssd, dxout, x, nw, scale), carry)
    return outs + [exchanged]


def _win_blocks(g):
    full = jnp.concatenate([g["w_gm"], g["w_ssd"][:, 0:DI + CD], g["w_ssd"][:, DI + CD:DI + CD + NH], g["w_g"]], axis=1)
    return jnp.transpose(full.reshape(D, NDEV, NIN // NDEV), (1, 0, 2))


def _row_blocks(g):
    return jnp.concatenate([g["wa"].reshape(NDEV, D // NDEV, D), g["wb"].reshape(NDEV, DI // NDEV, D),
                            g["wo"].reshape(NDEV, D // NDEV, D)], axis=1)


def _layer_bwd(dxo, sv, mod, lp, tag, carry=(), scatter_own=None):
    do, dpa, dpb, dpG, s_gate = _merge_bwd(dxo, sv["merged"], sv["pa"], sv["pb"], sv["pG"], mod[2:3], lp["wo"], f"merge_bwd_{tag}")
    g = {}
    g["wo"] = _mm(sv["merged"], do, "tn", f"dw_out_{tag}", out_dtype=_WIRE)
    g["wa"] = _mm(sv["ya"], dpa, "tn", f"dw_proj_a_{tag}", out_dtype=_WIRE)
    g["wb"] = _mm(sv["yb"], dpb, "tn", f"dw_proj_b_{tag}", out_dtype=_WIRE)
    dya = _mm(dpa, lp["wa"], "nt", f"dy_a_{tag}", out_dtype=_MXU)
    dyb = _mm(dpb, lp["wb"], "nt", f"dy_b_{tag}", out_dtype=_MXU)
    dpA, s_ln, g["ws"], dbsx = _gmlp_bwd(dya, sv["pA"], lp["lw"], lp["lb"], lp["ws"], lp["bsx"], f"gmlp_bwd_{tag}")
    carry = list(carry) + ([(_row_blocks(g), False)] if scatter_own else [])
    dpB, g["cw"], s_cb, s_hd, s_col, got = _ssd_bwd(dyb, sv["pB"], sv["xc"], sv["states"], lp["cw"], lp["dtb"], lp["alog"],
                                                    lp["dsk"], lp["snw"], f"ssd_bwd_{tag}", carry)
    g["lw"], g["lb"] = s_ln[0], s_ln[1]
    g["bs"] = dbsx.reshape(Q, NG, Q).sum(-1).T
    g["cb"] = s_cb[0]
    g["dtb"], g["alog"] = s_hd[0, :NH], s_hd[1, :NH]
    g["dsk"] = s_col[0].reshape(NH, P).sum(-1)
    g["snw"] = s_col[1]
    g["cw"] = g["cw"][0:KC]
    g["w_g"] = _mm(sv["h"], dpG, "tn", f"dw_gate_{tag}", out_dtype=_WIRE)
    g["w_gm"] = _mm(sv["h"], dpA, "tn", f"dw_gm_{tag}", out_dtype=_WIRE)
    if scatter_own:
        g["w_ssd"], got_slab = _mm(sv["h"], dpB, "tn", f"dw_ssd_{tag}", out_dtype=_WIRE, carry=[(scatter_own(g), True)])
    else:
        g["w_ssd"] = _mm(sv["h"], dpB, "tn", f"dw_ssd_{tag}", out_dtype=_WIRE)
    dx, s_mod, got_dh = _dh_modulate_bwd(dpG, dpA, dpB, lp["w_g"], lp["w_gm"], lp["w_ssd"], dxo, sv["x"], lp["nw"], mod[1:2],
                                         f"dh_{tag}", [(_win_blocks(g), False)] if scatter_own else ())
    if scatter_own:
        got, got_own = got[:-1], [got[-1], got_dh[0], got_slab[0]]
    else:
        got_own = []
    g["mod"] = jnp.concatenate([s_mod[0], s_mod[1], s_gate[0]])
    g["nw"] = s_mod[2]
    return dx, g, got, got_own


def _prep_layer(nw, w_in_full, lw, lb, ws, bs, cw_full, cb, dtb, alog, dsk, snw, rows=None):
    z = jnp.zeros((D, WB - (DI + CD + NH)), w_in_full.dtype)
    pad_h = lambda v: jnp.pad(v, (0, 128 - NH)).reshape(1, 128)
    extra = dict(zip(("wa", "wb", "wo"), _rows_split(rows))) if rows is not None else {}
    return dict(
        **extra,
        nw=nw.reshape(1, D),
        w_gm=w_in_full[:, 0:WA],
        w_ssd=jnp.concatenate([w_in_full[:, WA:WA + DI + CD], w_in_full[:, WA + DI + CD:WA + DI + CD + NH], z], axis=1),
        w_g=w_in_full[:, WA + DI + CD + NH:NIN],
        lw=lw.reshape(1, D), lb=lb.reshape(1, D), ws=ws, bsx=jnp.repeat(bs.T, Q, axis=1),
        cw=jnp.pad(cw_full, ((0, 8 - KC), (0, 0))), cb=cb.reshape(1, CD), dtb=pad_h(dtb), alog=pad_h(alog),
        dsk=jnp.repeat(dsk, P).reshape(1, DI), snw=snw.reshape(1, DI))


def _mod_dist(c8, ada_w, ada_b_cols, name):
    L, _, AW = ada_w.shape

    def body(c_ref, w_ref, b_ref, parts_ref, sc_ref, call_ref, mine_ref, send1, recv1, send2, recv2):
        x, y, c = lax.axis_index("x"), lax.axis_index("y"), lax.axis_index("c")
        me = 4 * x + 2 * y + c

        def peer(j):
            px = 1 - x if (j >> 2) & 1 else x
            py = 1 - y if (j >> 1) & 1 else y
            pc = 1 - c if j & 1 else c
            return (px, py, pc), 4 * px + 2 * py + pc

        def copy(j, src, dst, sems, landing):
            dev, idx = peer(j)
            return pltpu.make_async_remote_copy(
                src_ref=src, dst_ref=dst.at[idx] if landing else dst.at[me], send_sem=sems[0].at[j - 1], recv_sem=sems[1].at[j - 1],
                device_id=dev, device_id_type=pl.DeviceIdType.MESH)

        def all_to_all(src, dst, sems):
            for j in range(1, NDEV):
                copy(j, src, dst, sems, False).start()
            for j in range(1, NDEV):
                copy(j, src, dst, sems, True).wait_recv()
            for j in range(1, NDEV):
                copy(j, src, dst, sems, False).wait_send()

        call_ref[me] = c_ref[...]
        all_to_all(c_ref, call_ref, (send1, recv1))
        row = lax.broadcasted_iota(jnp.int32, (8, D), 0)
        cm = jnp.zeros((8, D), f32)
        for k in range(NDEV):
            cm = jnp.where(row == k, call_ref[k], cm)
        sc = _silu(cm)
        sc_ref[...] = sc
        for l in range(L):
            mine_ref[l] = _dot(_bf(sc), w_ref[l]) + b_ref[l]
        parts_ref[me] = mine_ref[...]
        all_to_all(mine_ref, parts_ref, (send2, recv2))

    vmem = pl.BlockSpec(memory_space=pltpu.VMEM)
    sems = pltpu.SemaphoreType.DMA((NDEV - 1,))
    return pl.pallas_call(
        body, name=name, in_specs=[vmem, vmem, vmem], out_specs=[vmem, vmem],
        out_shape=[jax.ShapeDtypeStruct((NDEV, L, 8, AW), f32), jax.ShapeDtypeStruct((8, D), f32)],
        scratch_shapes=[pltpu.VMEM((NDEV, 8, D), f32), pltpu.VMEM((L, 8, AW), f32), sems, sems, sems, sems])(c8, ada_w, ada_b_cols)


def _ada_w_grad(sc_all, dmod_cols, name):
    W = dmod_cols.shape[1]

    def body(s_ref, d_ref, o_ref):
        o_ref[...] = lax.dot_general(s_ref[...], d_ref[...], (((0,), (0,)), ((), ())), preferred_element_type=f32,
                                     precision=lax.Precision.HIGHEST)

    return pl.pallas_call(body, name=name, out_shape=jax.ShapeDtypeStruct((D, W), f32))(sc_all, dmod_cols)


def _adamw_math(w, g, m, v):
    m = ADAM_B1 * m + (1.0 - ADAM_B1) * g
    v = ADAM_B2 * v + (1.0 - ADAM_B2) * (g * g)
    m_hat = m / (1.0 - ADAM_B1 ** ADAM_STEP)
    v_hat = v / (1.0 - ADAM_B2 ** ADAM_STEP)
    delta = -ADAM_LR * (m_hat / (jnp.sqrt(v_hat) + ADAM_EPS) + ADAM_WD * w)
    return delta, m, v


def _sum_adamw(recvs, w, m, v, name, carry=()):
    nl = len(recvs)
    n, R, C = recvs[0].shape
    tr = _tile(R, 128 if C > 1024 else 256)
    nr = R // tr
    stacked = w.ndim == 3
    at = (lambda ref: ref.at[0]) if stacked else (lambda ref: ref)

    def body(*refs):
        r_refs, (w_ref, m_ref, v_ref, g_ref, d_ref, nm_ref, nv_ref) = refs[:nl], [at(r) for r in refs[nl:]]
        for li in range(nl):
            @pl.when(pl.program_id(0) == li)
            def _(r_ref=r_refs[li]):
                g = r_ref[0].astype(f32)
                for k in range(1, n):
                    g = g + r_ref[k].astype(f32)
                g_ref[...] = g
                d_ref[...], nm_ref[...], nv_ref[...] = _adamw_math(w_ref[...], g, m_ref[...], v_ref[...])

    if stacked:
        spec = pl.BlockSpec((1, tr, C), lambda l, i: (l, i, 0))
    else:
        spec = pl.BlockSpec((tr, C), lambda l, i: (l * nr + i, 0))
    r_specs = [pl.BlockSpec((n, tr, C), lambda l, i, li=li: (0, jnp.clip(i + (l - li) * nr, 0, nr - 1), 0)) for li in range(nl)]
    outs, exchanged = _carry_call(body, name, (nl, nr), r_specs + [spec, spec, spec], [spec] * 4,
                                  [jax.ShapeDtypeStruct(w.shape, f32)] * 4, [], (*recvs, w, m, v), carry)
    return (outs, exchanged) if carry else outs


def _pack(arrays, rows):
    flat = []
    for a in arrays:
        a = a.reshape(-1).astype(f32)
        flat.append(jnp.pad(a, (0, (-a.shape[0]) % 128)))
    flat = jnp.concatenate(flat)
    return jnp.pad(flat, (0, rows * 128 - flat.shape[0])).reshape(rows, 128)


def _unpack(slab, shapes):
    flat = slab.reshape(-1)
    out, off = [], 0
    for s in shapes:
        n = 1
        for d in s:
            n *= d
        out.append(flat[off:off + n].reshape(s))
        off += n + (-n) % 128
    return out


def kernel(x, c, ada_w, ada_b, norm_w, w_in, gm_ln_w, gm_ln_b, gm_ws, gm_bs, conv_w, conv_b, dt_bias, a_log, d_skip, ssm_norm_w, w_proj_a, w_proj_b, w_out, final_norm_w, loss_target, m_ada_w, m_ada_b, m_norm_w, m_w_in, m_gm_ln_w, m_gm_ln_b, m_gm_ws, m_gm_bs, m_conv_w, m_conv_b, m_dt_bias, m_a_log, m_d_skip, m_ssm_norm_w, m_w_proj_a, m_w_proj_b, m_w_out, m_final_norm_w, v_ada_w, v_ada_b, v_norm_w, v_w_in, v_gm_ln_w, v_gm_ln_b, v_gm_ws, v_gm_bs, v_conv_w, v_conv_b, v_dt_bias, v_a_log, v_d_skip, v_ssm_norm_w, v_w_proj_a, v_w_proj_b, v_w_out, v_final_norm_w):
    L = 2
    me = 4 * lax.axis_index("x") + 2 * lax.axis_index("y") + lax.axis_index("c")
    W = dict(ada_w=ada_w, ada_b=ada_b, norm_w=norm_w, w_in=w_in, gm_ln_w=gm_ln_w, gm_ln_b=gm_ln_b, gm_ws=gm_ws, gm_bs=gm_bs,
             conv_w=conv_w, conv_b=conv_b, dt_bias=dt_bias, a_log=a_log, d_skip=d_skip, ssm_norm_w=ssm_norm_w, w_proj_a=w_proj_a,
             w_proj_b=w_proj_b, w_out=w_out, final_norm_w=final_norm_w)
    M = dict(ada_w=m_ada_w, ada_b=m_ada_b, norm_w=m_norm_w, w_in=m_w_in, gm_ln_w=m_gm_ln_w, gm_ln_b=m_gm_ln_b, gm_ws=m_gm_ws,
             gm_bs=m_gm_bs, conv_w=m_conv_w, conv_b=m_conv_b, dt_bias=m_dt_bias, a_log=m_a_log, d_skip=m_d_skip,
             ssm_norm_w=m_ssm_norm_w, w_proj_a=m_w_proj_a, w_proj_b=m_w_proj_b, w_out=m_w_out, final_norm_w=m_final_norm_w)
    V = dict(ada_w=v_ada_w, ada_b=v_ada_b, norm_w=v_norm_w, w_in=v_w_in, gm_ln_w=v_gm_ln_w, gm_ln_b=v_gm_ln_b, gm_ws=v_gm_ws,
             gm_bs=v_gm_bs, conv_w=v_conv_w, conv_b=v_conv_b, dt_bias=v_dt_bias, a_log=v_a_log, d_skip=v_d_skip,
             ssm_norm_w=v_ssm_norm_w, w_proj_a=v_w_proj_a, w_proj_b=v_w_proj_b, w_out=v_w_out, final_norm_w=v_final_norm_w)
    SW = NIN // NDEV
    AW = 3 * D // NDEV
    CW = CD // NDEV
    RA, RB = D // NDEV, DI // NDEV

    wire = lambda a: a.astype(_WIRE)
    rows_of = lambda d, l: jnp.concatenate([d["w_proj_a"][l], d["w_proj_b"][l], d["w_out"][l]], axis=0)
    parts, sc_all = _mod_dist(jnp.broadcast_to(c, (8, D)), wire(ada_w),
                              lax.dynamic_slice_in_dim(ada_b, me * AW, AW, axis=1).reshape(L, 1, AW), "mod_dist")
    mods = [lax.dynamic_index_in_dim(parts[:, l], me, axis=1, keepdims=False).reshape(3, D) for l in range(L)]
    cols = lambda g: jnp.transpose(g, (1, 0, 2)).reshape(g.shape[1], -1)

    def prep(l, g_win, g_rows):
        return _prep_layer(norm_w[l], cols(g_win), gm_ln_w[l], gm_ln_b[l], gm_ws[l], gm_bs[l], cols(g_cw[:, l]), conv_b[l],
                           dt_bias[l], a_log[l], d_skip[l], ssm_norm_w[l], g_rows)

    half = D // 2
    rows0, rows1 = wire(rows_of(W, 0)), wire(rows_of(W, 1))
    hr = rows0.shape[0] // 2
    h0, (g_win0, g_cw) = _modulate(x[0], norm_w[0].reshape(1, D), mods[0][0:1], mods[0][1:2], "modulate_l0",
                                   [(wire(w_in[0]), "two_level"), (conv_w.reshape(L * KC, CW), True)])
    g_cw = g_cw.reshape(NDEV, L, KC, CW)
    h, sv0, got, lp0 = _layer_fwd(x[0], mods[0], prep(0, g_win0, None), "l0",
                                  dict(proj_gm=[(rows0[:hr], True)], proj_gate=[(rows0[hr:], True)],
                                       proj_ssd=[(wire(w_in[1][:half]), True)],
                                       ssd_fwd=[(wire(w_in[1][half:]), True), (rows1, True)]), h0)
    lp1 = prep(1, jnp.concatenate([got["proj_ssd"][0], got["ssd_fwd"][0]], axis=1), got["ssd_fwd"][1])
    h, sv1, _, _ = _layer_fwd(h, mods[1], lp1, "l1")
    dh, s_fin = _final_loss(h, loss_target[0], final_norm_w.reshape(1, D), "final_loss")
    loss = lax.psum(0.5 / D * s_fin[1, 0], ("x", "y", "c"))

    small_l = ["gm_ln_w", "gm_ln_b", "gm_ws", "gm_bs", "conv_b", "dt_bias", "a_log", "d_skip", "ssm_norm_w"]
    small_key = dict(gm_ln_w="lw", gm_ln_b="lb", gm_ws="ws", gm_bs="bs", conv_b="cb", dt_bias="dtb", a_log="alog", d_skip="dsk",
                     ssm_norm_w="snw")
    shapes_l = [W[n].shape[1:] for n in small_l] + [(KC, CD)]
    shapes_t = [(L, 3 * D), (L, D), (D,)]
    rows_for = lambda sh: -(-sum(-(-functools.reduce(lambda a, b: a * b, s, 1) // 128) for s in sh) // 256) * 256
    n_rows_l, n_rows_t = rows_for(shapes_l), rows_for(shapes_t)
    slab_of = lambda g: _pack([g[small_key[n]] for n in small_l] + [g["cw"]], n_rows_l)

    dh, g1, _, _ = _layer_bwd(dh, sv1, mods[1], lp1, "l1")
    dh, g0, got, got_own = _layer_bwd(dh, sv0, mods[0], lp0, "l0",
                                      [(_win_blocks(g1), False), (_row_blocks(g1), False), (slab_of(g1), True)], slab_of)
    grads = [g0, g1]
    grad_x = dh[None]
    recv_win = [got_own[1], got[0]]
    recv_rows = [got_own[0], got[1]]
    slab_all = [got_own[2], got[2]]
    st = lambda key: jnp.stack([g[key] for g in grads])
    slab_tail = _pack([st("mod"), st("nw"), s_fin[0]], n_rows_t)

    o_win = _sum_adamw(recv_win, w_in, m_w_in, v_w_in, "adamw_w_in")
    rows_all = lambda d: jnp.concatenate([rows_of(d, l) for l in range(L)], axis=0)
    o_rows, (slab_t,) = _sum_adamw(recv_rows, rows_all(W), rows_all(M), rows_all(V), "adamw_w_rows", [(slab_tail, True)])
    o_rows = [o.reshape(L, 2 * RA + RB, D) for o in o_rows]
    o_pa = [o[:, 0:RA] for o in o_rows]
    o_pb = [o[:, RA:RA + RB] for o in o_rows]
    o_po = [o[:, RA + RB:] for o in o_rows]

    def slab_params(d):
        return jnp.concatenate([_pack([d[n][l] for n in small_l] + [jnp.zeros((KC, CD), f32)], n_rows_l) for l in range(L)], axis=0)

    o_small = [[_unpack(o[l * n_rows_l:(l + 1) * n_rows_l], shapes_l) for l in range(L)]
               for o in _sum_adamw(slab_all, slab_params(W), slab_params(M), slab_params(V), "adamw_small")]
    small_out = {n: [jnp.stack([o_small[k][l][i] for l in range(L)]) for k in range(4)] for i, n in enumerate(small_l)}
    tail_params = lambda d: _pack([d["ada_b"], d["norm_w"], d["final_norm_w"]], n_rows_t)
    o_tail = [_unpack(o, shapes_t) for o in _sum_adamw([slab_t], tail_params(W), tail_params(M), tail_params(V), "adamw_small_tail")]
    for i, n in enumerate(["ada_b", "norm_w", "final_norm_w"]):
        small_out[n] = [o_tail[k][i] for k in range(4)]

    g_cw_mine = jnp.stack([lax.dynamic_slice_in_dim(o_small[0][l][len(small_l)], me * CW, CW, axis=1) for l in range(L)])
    o_cw = _sum_adamw([g_cw_mine.reshape(1, L * KC, CW)], conv_w.reshape(L * KC, CW), m_conv_w.reshape(L * KC, CW),
                      v_conv_w.reshape(L * KC, CW), "adamw_conv_w")
    o_cw = [o.reshape(L, KC, CW) for o in o_cw]

    dmod_all = jnp.stack([_unpack(slab_t[k], shapes_t)[0] for k in range(NDEV)])
    dmod_cols = lax.dynamic_slice_in_dim(dmod_all, me * AW, AW, axis=2).reshape(NDEV, L * AW)
    g_ada_w = jnp.transpose(_ada_w_grad(sc_all, dmod_cols, "ada_w_grad").reshape(D, L, AW), (1, 0, 2))
    o_ada = _sum_adamw([g_ada_w[l][None] for l in range(L)], ada_w, m_ada_w, v_ada_w, "adamw_ada_w")

    big = dict(ada_w=o_ada, w_in=o_win, conv_w=o_cw, w_proj_a=o_pa, w_proj_b=o_pb, w_out=o_po)
    order = ["ada_w", "ada_b", "norm_w", "w_in", "gm_ln_w", "gm_ln_b", "gm_ws", "gm_bs", "conv_w", "conv_b", "dt_bias", "a_log",
             "d_skip", "ssm_norm_w", "w_proj_a", "w_proj_b", "w_out", "final_norm_w"]
    pick = lambda n, k: big[n][k] if n in big else small_out[n][k]
    return (loss, grad_x, *[pick(n, 0) for n in order], *[pick(n, 1) for n in order], *[pick(n, 2) for n in order],
            *[pick(n, 3) for n in order])
```
